```python
import math, functools
import jax, jax.numpy as jnp
from jax import lax
import numpy as np

D_MODEL = 1024
BATCH = 32
SEQ = 256
DEPTH = 2
DEC_BATCH = 2
DEC_SEQ = 1024
PAST_LEN = 256

GRID_W = 64
N_HEADS_A = 4
DK_A = 64
DV_A = 64
W_A = N_HEADS_A * DV_A
GLA_RANK = 16
GLA_TAU = 16.0
GLA_CHUNK = 64
N_HEADS_B = 4
DH_B = 64
DV_B = 2 * DH_B
W_QK_B = N_HEADS_B * 2 * DH_B
W_B = N_HEADS_B * DV_B
ROPE_THETA = 10000.0
AXIS_PAIRS = DH_B // 4
N_GROUPS_C = 4
DG_C = 64
W_C = N_GROUPS_C * DG_C
SGU_CHUNK = 128
MIX = W_A + W_B + W_C
IN_SPLITS = (W_A, W_A, W_A, W_A, GLA_RANK, GLA_RANK, W_QK_B, W_QK_B, W_B, W_C, W_C)
IN_COLS = 4 * W_A + 2 * GLA_RANK + 2 * W_QK_B + W_B + 2 * W_C
D_FF = 2816
N_EXPERTS = 8
TOP_K = 2
D_FF_EXPERT = 2816
N_DENSE = (DEPTH + 1) // 2
N_MOE = DEPTH // 2
EPS = 1e-6

kernel_name = 'hybrid_diffusion_gla_diffattn_sgu_step'


def rmsnorm(x, g):
    xf = x.astype(jnp.float32)
    y = xf * lax.rsqrt(jnp.mean(xf * xf, axis=-1, keepdims=True) + EPS)
    return (y * g.astype(jnp.float32)).astype(x.dtype)


def head_rmsnorm(o, g):
    H, Dh = o.shape[-2:]
    y = o * lax.rsqrt(jnp.mean(o * o, axis=-1, keepdims=True) + EPS)
    return y * g.astype(jnp.float32).reshape(H, Dh)


def axial_rope_angles(L):
    rows = L // GRID_W
    row = jnp.repeat(jnp.arange(rows, dtype=jnp.float32), GRID_W)
    col = jnp.tile(jnp.arange(GRID_W, dtype=jnp.float32), rows)
    freqs = ROPE_THETA ** (-jnp.arange(AXIS_PAIRS, dtype=jnp.float32) / AXIS_PAIRS)
    return row[:, None] * freqs, col[:, None] * freqs


def apply_axial_rope(x, ang_row, ang_col):
    def rot(xp, ang):
        c = jnp.cos(ang)[:, None, None, :]
        s = jnp.sin(ang)[:, None, None, :]
        x1, x2 = jnp.split(xp.astype(jnp.float32), 2, axis=-1)
        return jnp.concatenate([x1 * c - x2 * s, x1 * s + x2 * c], axis=-1)
    xr, xc = jnp.split(x, 2, axis=-1)
    return jnp.concatenate([rot(xr, ang_row), rot(xc, ang_col)], axis=-1).astype(x.dtype)


def gla_chunked(q, k, v, log_a, s0):
    B, L, H, DK = q.shape
    DV = v.shape[-1]
    nc = L // GLA_CHUNK

    def to_chunks(t):
        return t.astype(jnp.float32).reshape(B, nc, GLA_CHUNK, H, t.shape[-1]).transpose(1, 0, 3, 2, 4)

    qc, kc, vc, ac = to_chunks(q), to_chunks(k), to_chunks(v), to_chunks(log_a)
    mask = jnp.tril(jnp.ones((GLA_CHUNK, GLA_CHUNK), dtype=bool))[:, :, None]

    def step(S, inp):
        qi, ki, vi, ai = inp
        b = jnp.cumsum(ai, axis=2)
        o_inter = jnp.einsum('bhcd,bhde->bhce', qi * jnp.exp(b), S)
        rel = jnp.where(mask, b[:, :, :, None, :] - b[:, :, None, :, :], -jnp.inf)
        scores = jnp.sum(qi[:, :, :, None, :] * ki[:, :, None, :, :] * jnp.exp(rel), axis=-1)
        o_intra = jnp.einsum('bhij,bhje->bhie', scores, vi)
        b_last = b[:, :, -1:, :]
        S_new = jnp.exp(b_last[:, :, 0, :])[..., None] * S + jnp.einsum(
            'bhcd,bhce->bhde', ki * jnp.exp(b_last - b), vi)
        return S_new, o_inter + o_intra

    S_fin, o = lax.scan(step, s0.astype(jnp.float32), (qc, kc, vc, ac))
    o = o.transpose(1, 0, 3, 2, 4).reshape(B, L, H, DV)
    return o, S_fin


def gated_linear_attention(qa, ka, va, ga, zf, zb, w_up, b_up, gain, s0_f, s0_b):
    B, L, _ = qa.shape
    q = qa.reshape(B, L, N_HEADS_A, DK_A) * (DK_A ** -0.5)
    k = ka.reshape(B, L, N_HEADS_A, DK_A)
    v = va.reshape(B, L, N_HEADS_A, DV_A)

    def log_decay(z, w, b):
        return (jax.nn.log_sigmoid((z @ w + b).astype(jnp.float32)) / GLA_TAU).reshape(B, L, N_HEADS_A, DK_A)

    la_f = log_decay(zf, w_up[0], b_up[0])
    la_b = log_decay(zb, w_up[1], b_up[1])
    o_f, s_f = gla_chunked(q, k, v, la_f, s0_f)
    o_b, s_b = gla_chunked(q[:, ::-1], k[:, ::-1], v[:, ::-1], la_b[:, ::-1], s0_b)
    o = head_rmsnorm(o_f + o_b[:, ::-1], gain)
    o = o * jax.nn.silu(ga.astype(jnp.float32)).reshape(B, L, N_HEADS_A, DV_A)
    return o.reshape(B, L, W_A), s_f, s_b


def differential_attention(q, k, v, lam_vec, gain, layer_idx):
    lam_init = 0.8 - 0.6 * math.exp(-0.3 * layer_idx)
    lv = lam_vec.astype(jnp.float32)
    lam = jnp.exp(jnp.sum(lv[0] * lv[1])) - jnp.exp(jnp.sum(lv[2] * lv[3])) + lam_init
    s = jnp.einsum('bqhcd,bkhcd->bchqk', q, k).astype(jnp.float32) * (DH_B ** -0.5)
    p = jax.nn.softmax(s, axis=-1)
    a = p[:, 0] - lam * p[:, 1]
    o = jnp.einsum('bhqk,bkhe->bqhe', a, v.astype(jnp.float32))
    o = head_rmsnorm(o, gain) * (1.0 - lam_init)
    B, L = o.shape[:2]
    return o.reshape(B, L, W_B)


def chunk_sgu(u, v, w_s, b_s):
    B, L, _ = u.shape
    n = L // SGU_CHUNK
    u = jax.nn.gelu(u.astype(jnp.float32)).reshape(B, n, SGU_CHUNK, N_GROUPS_C, DG_C)
    v = jax.nn.gelu(v.astype(jnp.float32)).reshape(B, n, SGU_CHUNK, N_GROUPS_C, DG_C)
    mu = jnp.mean(v, axis=-1, keepdims=True)
    var = jnp.mean(jnp.square(v - mu), axis=-1, keepdims=True)
    vn = (v - mu) * lax.rsqrt(var + EPS)
    s = jnp.einsum('gij,bnjgd->bnigd', w_s.astype(jnp.float32), vn) + b_s.astype(jnp.float32).T[:, :, None]
    return (u * s).reshape(B, L, W_C)


def token_mixers(h, w_in, w_out, gla_w_up, gla_b_up, gla_norm, diff_lambda, diff_norm, sgu_w, sgu_b,
                 layer_idx, rope, ctx):
    B, L, _ = h.shape
    proj = h @ w_in
    qa, ka, va, ga, zf, zb, qb, kb, vb, uc, vc = jnp.split(
        proj, np.cumsum(IN_SPLITS)[:-1].tolist(), axis=-1)
    if ctx is None:
        s0 = jnp.zeros((B, N_HEADS_A, DK_A, DV_A), jnp.float32)
        s0_f, s0_b = s0, s0
    else:
        k_ctx, v_ctx, s0_f, s0_b = ctx
    oa, s_f, s_b = gated_linear_attention(qa, ka, va, ga, zf, zb, gla_w_up, gla_b_up, gla_norm, s0_f, s0_b)

    qb = qb.reshape(B, L, N_HEADS_B, 2, DH_B)
    kb = kb.reshape(B, L, N_HEADS_B, 2, DH_B)
    vb = vb.reshape(B, L, N_HEADS_B, DV_B)
    new_ctx = (kb.reshape(B, L, N_HEADS_B, 2 * DH_B), vb, s_f, s_b)
    if rope is not None:
        qb = apply_axial_rope(qb, *rope)
        kb = apply_axial_rope(kb, *rope)
    if ctx is not None:
        kb = jnp.concatenate([kb, k_ctx.reshape(B, -1, N_HEADS_B, 2, DH_B).astype(kb.dtype)], axis=1)
        vb = jnp.concatenate([vb, v_ctx.astype(vb.dtype)], axis=1)
    ob = differential_attention(qb, kb, vb, diff_lambda, diff_norm, layer_idx)

    oc = chunk_sgu(uc, vc, sgu_w, sgu_b)
    y = jnp.concatenate([oa, ob, oc], axis=-1).astype(h.dtype) @ w_out
    return y, new_ctx


def dense_swiglu(h, wg, wu, wd):
    return (jax.nn.silu(h @ wg) * (h @ wu)) @ wd


def moe_swiglu(h, router_w, wg, wu, wd):
    B, L, D = h.shape
    t = h.reshape(B * L, D)
    logits = (t @ router_w).astype(jnp.float32)
    top_v, top_i = lax.top_k(logits, TOP_K)
    w = jax.nn.softmax(top_v, axis=-1)
    gates = jnp.sum(jax.nn.one_hot(top_i, N_EXPERTS, dtype=jnp.float32) * w[..., None], axis=1)
    out = jnp.zeros((B * L, D), jnp.float32)
    for e in range(N_EXPERTS):
        out = out + gates[:, e:e + 1] * dense_swiglu(t, wg[e], wu[e], wd[e]).astype(jnp.float32)
    return out.astype(h.dtype).reshape(B, L, D)


def trunk_layer(x, cvec, w_ada, b_ada, norm_mix, norm_ffn, mixer_args, ffn, layer_idx, rope, ctx):
    mod = (jax.nn.silu(cvec) @ w_ada + b_ada)[:, None, :]
    sh1, sc1, g1, sh2, sc2, g2 = jnp.split(mod, 6, axis=-1)
    h = rmsnorm(x, norm_mix) * (1 + sc1) + sh1
    y, new_ctx = token_mixers(h, *mixer_args, layer_idx, rope, ctx)
    x = x + g1 * y
    h = rmsnorm(x, norm_ffn) * (1 + sc2) + sh2
    x = x + g2 * ffn(h)
    return x, new_ctx


def setup_inputs(seed: int = 0) -> dict:
    key = jax.random.key(seed)
    ks = jax.random.split(key, 32)
    f32 = jnp.float32

    def nrm(k, shape, scale=1.0):
        return jax.random.normal(k, shape, f32) * scale

    return {
        'x_prompt': nrm(ks[0], (BATCH, SEQ, D_MODEL)),
        'x_sample': nrm(ks[1], (DEC_BATCH, DEC_SEQ, D_MODEL)),
        'cache_k': nrm(ks[2], (DEC_BATCH, DEPTH, PAST_LEN, N_HEADS_B, 2 * DH_B)),
        'cache_v': nrm(ks[3], (DEC_BATCH, DEPTH, PAST_LEN, N_HEADS_B, DV_B)),
        'state_gla_fwd': nrm(ks[4], (DEC_BATCH, DEPTH, N_HEADS_A, DK_A, DV_A), 2.0),
        'state_gla_bwd': nrm(ks[5], (DEC_BATCH, DEPTH, N_HEADS_A, DK_A, DV_A), 2.0),
        'c': nrm(ks[6], (DEC_BATCH, D_MODEL)),
        'c_ctx': nrm(ks[7], (D_MODEL,)),
        'w_ada': nrm(ks[8], (DEPTH, D_MODEL, 6 * D_MODEL), 0.5 * D_MODEL ** -0.5),
        'b_ada': nrm(ks[9], (DEPTH, 6 * D_MODEL), 0.02),
        'norm_mix': 1.0 + nrm(ks[10], (DEPTH, D_MODEL), 0.05),
        'norm_ffn': 1.0 + nrm(ks[11], (DEPTH, D_MODEL), 0.05),
        'w_in': nrm(ks[12], (DEPTH, D_MODEL, IN_COLS), D_MODEL ** -0.5),
        'w_out': nrm(ks[13], (DEPTH, MIX, D_MODEL), MIX ** -0.5),
        'gla_w_up': nrm(ks[14], (DEPTH, 2, GLA_RANK, W_A), GLA_RANK ** -0.5),
        'gla_b_up': nrm(ks[15], (DEPTH, 2, W_A), 0.1),
        'gla_norm': 1.0 + nrm(ks[16], (DEPTH, W_A), 0.05),
        'diff_lambda': nrm(ks[17], (DEPTH, 4, DH_B), 0.1),
        'diff_norm': 1.0 + nrm(ks[18], (DEPTH, W_B), 0.05),
        'sgu_w': nrm(ks[19], (DEPTH, N_GROUPS_C, SGU_CHUNK, SGU_CHUNK), SGU_CHUNK ** -0.5),
        'sgu_b': 1.0 + nrm(ks[20], (DEPTH, N_GROUPS_C, SGU_CHUNK), 0.1),
        'ffn_w_gate': nrm(ks[21], (N_DENSE, D_MODEL, D_FF), D_MODEL ** -0.5),
        'ffn_w_up': nrm(ks[22], (N_DENSE, D_MODEL, D_FF), D_MODEL ** -0.5),
        'ffn_w_down': nrm(ks[23], (N_DENSE, D_FF, D_MODEL), D_FF ** -0.5),
        'router_w': nrm(ks[24], (N_MOE, D_MODEL, N_EXPERTS), D_MODEL ** -0.5),
        'moe_w_gate': nrm(ks[25], (N_MOE, N_EXPERTS, D_MODEL, D_FF_EXPERT), D_MODEL ** -0.5),
        'moe_w_up': nrm(ks[26], (N_MOE, N_EXPERTS, D_MODEL, D_FF_EXPERT), D_MODEL ** -0.5),
        'moe_w_down': nrm(ks[27], (N_MOE, N_EXPERTS, D_FF_EXPERT, D_MODEL), D_FF_EXPERT ** -0.5),
        'norm_f': 1.0 + nrm(ks[28], (D_MODEL,), 0.05),
    }


def reference(x_prompt, x_sample, cache_k, cache_v, state_gla_fwd, state_gla_bwd, c, c_ctx,
              w_ada, b_ada, norm_mix, norm_ffn, w_in, w_out, gla_w_up, gla_b_up, gla_norm,
              diff_lambda, diff_norm, sgu_w, sgu_b, ffn_w_gate, ffn_w_up, ffn_w_down,
              router_w, moe_w_gate, moe_w_up, moe_w_down, norm_f):
    rope = axial_rope_angles(x_sample.shape[1])
    xp, xs = x_prompt, x_sample
    ks, vs, sfs, sbs = [], [], [], []
    for l in range(DEPTH):
        mixer_args = (w_in[l], w_out[l], gla_w_up[l], gla_b_up[l], gla_norm[l],
                      diff_lambda[l], diff_norm[l], sgu_w[l], sgu_b[l])
        if l % 2 == 0:
            ffn = functools.partial(dense_swiglu, wg=ffn_w_gate[l // 2], wu=ffn_w_up[l // 2],
                                    wd=ffn_w_down[l // 2])
        else:
            ffn = functools.partial(moe_swiglu, router_w=router_w[l // 2], wg=moe_w_gate[l // 2],
                                    wu=moe_w_up[l // 2], wd=moe_w_down[l // 2])
        xp, (k_l, v_l, sf_l, sb_l) = trunk_layer(xp, c_ctx[None, :], w_ada[l], b_ada[l], norm_mix[l],
                                                 norm_ffn[l], mixer_args, ffn, l, None, None)
        ks.append(k_l)
        vs.append(v_l)
        sfs.append(sf_l)
        sbs.append(sb_l)
        ctx = (cache_k[:, l], cache_v[:, l], state_gla_fwd[:, l], state_gla_bwd[:, l])
        xs, _ = trunk_layer(xs, c, w_ada[l], b_ada[l], norm_mix[l], norm_ffn[l], mixer_args, ffn,
                            l, rope, ctx)
    y_prompt = rmsnorm(xp, norm_f)
    y_sample = rmsnorm(xs, norm_f)
    new_cache_k = jnp.stack(ks, axis=1)
    new_cache_v = jnp.stack(vs, axis=1)
    new_state_gla_fwd = jnp.stack(sfs, axis=1)
    new_state_gla_bwd = jnp.stack(sbs, axis=1)
    return (y_prompt, y_sample, new_cache_k, new_cache_v, new_state_gla_fwd, new_state_gla_bwd)
```

```python
import functools
import math

import jax
import jax.numpy as jnp
import numpy as np
from jax import lax
from jax.experimental import pallas as pl
from jax.experimental.pallas import tpu as pltpu

F32 = jnp.float32
BF16 = jnp.bfloat16

D_MODEL = 1024
BATCH = 32
SEQ = 256
DEPTH = 2
DEC_BATCH = 2
DEC_SEQ = 1024
PAST_LEN = 256
GRID_W = 64
N_HEADS_A = 4
DK_A = 64
W_A = 256
GLA_RANK = 16
GLA_TAU = 16.0
GLA_CHUNK = 64
N_HEADS_B = 4
DH_B = 64
DV_B = 128
W_B = 512
ROPE_THETA = 10000.0
AXIS_PAIRS = DH_B // 4
N_GROUPS_C = 4
DG_C = 64
W_C = 256
SGU_CHUNK = 128
D_FF = 2816
N_EXPERTS = 8
EPS = 1e-6

T_PROMPT = BATCH * SEQ
T_SAMPLE = DEC_BATCH * DEC_SEQ
T_ALL = T_PROMPT + T_SAMPLE
N_MOD_ROWS = 8
LANES = 128
VMEM_LIMIT = 56 * 1024 * 1024


def _cparams(sem):
    return pltpu.CompilerParams(dimension_semantics=sem, vmem_limit_bytes=VMEM_LIMIT)


def _dot(a, b):
    return jnp.dot(a, b, preferred_element_type=F32)


def _dot_nt(a, b):
    return lax.dot_general(a, b, (((1,), (1,)), ((), ())), preferred_element_type=F32)


def _dot_tn(a, b):
    return lax.dot_general(a, b, (((0,), (0,)), ((), ())), preferred_element_type=F32)


def _split_bf16(x):
    hi = x.astype(BF16)
    lo = (x - hi.astype(F32)).astype(BF16)
    return hi, lo


def _dot3(a, w):
    a_hi, a_lo = _split_bf16(a)
    w_hi, w_lo = _split_bf16(w)
    return _dot(a_hi, w_hi) + (_dot(a_lo, w_hi) + _dot(a_hi, w_lo))


def _sigmoid(x):
    return 1.0 / (1.0 + jnp.exp(-x))


def _silu(x):
    return x * _sigmoid(x)


def _gelu_tanh(x):
    c = math.sqrt(2.0 / math.pi)
    return x * (0.5 * (1.0 + jnp.tanh(c * (x + 0.044715 * (x * x * x)))))


def _log_sigmoid(x):
    return jnp.minimum(x, 0.0) - jnp.log(1.0 + jnp.exp(-jnp.abs(x)))


def _mod_row(i, tm):
    n_p = T_PROMPT // tm
    per_b = DEC_SEQ // tm
    return jnp.where(i < n_p, 0, 1 + (i - n_p) // per_b)


ADA_TN = 1536


def _ada_kernel(c_ref, w_ref, b_ref, o_ref):
    a = _silu(c_ref[...])
    o_ref[...] = _dot3(a, w_ref[...]) + b_ref[...]


def _ada_call(cvecs, w_ada, b_ada):
    n_col = (6 * D_MODEL) // ADA_TN
    return pl.pallas_call(
        _ada_kernel,
        grid=(DEPTH, n_col),
        in_specs=[
            pl.BlockSpec((N_MOD_ROWS, D_MODEL), lambda l, j: (0, 0)),
            pl.BlockSpec((None, D_MODEL, ADA_TN), lambda l, j: (l, 0, j)),
            pl.BlockSpec((None, 1, ADA_TN), lambda l, j: (l, 0, j)),
        ],
        out_specs=pl.BlockSpec((None, N_MOD_ROWS, ADA_TN), lambda l, j: (l, 0, j)),
        out_shape=jax.ShapeDtypeStruct((DEPTH, N_MOD_ROWS, 6 * D_MODEL), F32),
        compiler_params=_cparams(("arbitrary", "arbitrary")),
        name="ada_mod",
    )(cvecs, w_ada, b_ada.reshape(DEPTH, 1, 6 * D_MODEL))


IN_TM = 256
W_MAIN = 3072


def _in_proj_kernel(x_ref, nrm_ref, sh_ref, sc_ref, wm_ref, wz_ref, wup_ref, bup_ref,
                    g4_ref, la_ref, qb_ref, kb_ref, vb_ref, uv_ref):
    x = x_ref[...]
    y = x * lax.rsqrt(jnp.mean(x * x, axis=-1, keepdims=True) + EPS)
    h = (y * nrm_ref[...]) * (1.0 + sc_ref[...]) + sh_ref[...]
    hb = h.astype(BF16)
    g4_ref[...] = _dot(hb, wm_ref[:, 0:1024])
    qb_ref[...] = _dot(hb, wm_ref[:, 1024:1536])
    kb_ref[...] = _dot(hb, wm_ref[:, 1536:2048])
    vb_ref[...] = _dot(hb, wm_ref[:, 2048:2560])
    uv_ref[...] = _dot(hb, wm_ref[:, 2560:3072])
    z = _dot(hb, wz_ref[...])
    zz = _dot(z.astype(BF16), wup_ref[...]) + bup_ref[...]
    la_ref[...] = _log_sigmoid(zz) * (1.0 / GLA_TAU)


def _in_proj_call(x, norm_g, mod4, w_main, w_z, w_up, b_up):
    tm = IN_TM
    n = T_ALL // tm
    row = functools.partial(_mod_row, tm=tm)
    mod_spec = lambda k: pl.BlockSpec((None, None, 1, D_MODEL), lambda i: (row(i), k, 0, 0))
    full = lambda a: pl.BlockSpec(a.shape, lambda i: (0,) * a.ndim)
    out = lambda w: pl.BlockSpec((tm, w), lambda i: (i, 0))
    return pl.pallas_call(
        _in_proj_kernel,
        grid=(n,),
        in_specs=[pl.BlockSpec((tm, D_MODEL), lambda i: (i, 0)), full(norm_g),
                  mod_spec(0), mod_spec(1), full(w_main), full(w_z), full(w_up), full(b_up)],
        out_specs=[out(1024), out(512), out(512), out(512), out(512), out(512)],
        out_shape=[jax.ShapeDtypeStruct((T_ALL, w), F32) for w in (1024, 512, 512, 512, 512, 512)],
        compiler_params=_cparams(("arbitrary",)),
        name="in_proj",
    )(x, norm_g, mod4, mod4, w_main, w_z, w_up, b_up)


def _gla_chunk(q, k, v, la, st, tri, mask, mid, last):
    la_hi, la_lo = _split_bf16(la)
    b = _dot(tri, la_hi) + _dot(tri, la_lo)
    bl = b[last:last + 1, :]
    m = b[mid:mid + 1, :]
    qe = (q * jnp.exp(b - m)).astype(BF16)
    ke = (k * jnp.exp(m - b)).astype(BF16)
    qi = (q * jnp.exp(b)).astype(BF16)
    ks = (k * jnp.exp(bl - b)).astype(BF16)
    vb = v.astype(BF16)
    stb = st.astype(BF16)
    outs, kvs = [], []
    for h in range(N_HEADS_A):
        sl = slice(h * DK_A, (h + 1) * DK_A)
        a = jnp.where(mask, _dot_nt(qe[:, sl], ke[:, sl]), 0.0)
        outs.append(_dot(a.astype(BF16), vb[:, sl]) + _dot_nt(qi[:, sl], stb[:, sl]))
        kvs.append(_dot_tn(vb[:, sl], ks[:, sl]))
    o = jnp.concatenate(outs, axis=-1)
    st_new = st * jnp.exp(bl) + jnp.concatenate(kvs, axis=-1)
    return o, st_new


def _gla_kernel(g4_ref, la_ref, s0f_ref, s0b_ref, gain_ref, o_ref, sf_ref, sb_ref, of_ref, *, seq):
    c = GLA_CHUNK
    nc = seq // c
    r = lax.broadcasted_iota(jnp.int32, (c, c), 0)
    s = lax.broadcasted_iota(jnp.int32, (c, c), 1)
    lower = s <= r
    upper = s >= r
    tri_f = jnp.where(lower, 1.0, 0.0).astype(BF16)
    tri_b = jnp.where(upper, 1.0, 0.0).astype(BF16)
    scale = DK_A ** -0.5

    def load(ci):
        rows = pl.ds(pl.multiple_of(ci * c, c), c)
        q = g4_ref[rows, 0:256] * scale
        k = g4_ref[rows, 256:512]
        v = g4_ref[rows, 512:768]
        return rows, q, k, v

    def fwd(ci, st):
        rows, q, k, v = load(ci)
        o, st = _gla_chunk(q, k, v, la_ref[rows, 0:256], st, tri_f, lower, c // 2 - 1, c - 1)
        of_ref[rows, :] = o
        return st

    sf_ref[...] = lax.fori_loop(0, nc, fwd, s0f_ref[...])

    gain = gain_ref[...]

    def bwd(i, st):
        ci = nc - 1 - i
        rows, q, k, v = load(ci)
        o, st = _gla_chunk(q, k, v, la_ref[rows, 256:512], st, tri_b, upper, c // 2, 0)
        o = o + of_ref[rows, :]
        ys = []
        for h in range(N_HEADS_A):
            oh = o[:, h * DK_A:(h + 1) * DK_A]
            ys.append(oh * lax.rsqrt(jnp.mean(oh * oh, axis=-1, keepdims=True) + EPS))
        y = jnp.concatenate(ys, axis=-1) * gain
        o_ref[rows, :] = (y * _silu(g4_ref[rows, 768:1024])).astype(BF16)
        return st

    sb_ref[...] = lax.fori_loop(0, nc, bwd, s0b_ref[...])


def _gla_call(g4, la, s0f, s0b, gain, *, batch, seq, row_block0):
    tok = lambda w: pl.BlockSpec((seq, w), lambda b: (row_block0 + b, 0))
    st = pl.BlockSpec((None, 64, 256), lambda b: (b, 0, 0))
    return pl.pallas_call(
        functools.partial(_gla_kernel, seq=seq),
        grid=(batch,),
        in_specs=[tok(1024), tok(512), st, st, pl.BlockSpec((1, W_A), lambda b: (0, 0))],
        out_specs=[pl.BlockSpec((seq, W_A), lambda b: (b, 0)), st, st],
        out_shape=[jax.ShapeDtypeStruct((batch * seq, W_A), BF16),
                   jax.ShapeDtypeStruct((batch, 64, 256), F32),
                   jax.ShapeDtypeStruct((batch, 64, 256), F32)],
        scratch_shapes=[pltpu.VMEM((seq, W_A), F32)],
        compiler_params=_cparams(("arbitrary",)),
        name=f"gla_{seq}",
    )(g4, la, s0f, s0b, gain)


def _state_to_kernel(s):
    b = s.shape[0]
    return jnp.transpose(s, (0, 3, 1, 2)).reshape(b, 64, 256)


def _state_from_kernel(st):
    b = st.shape[0]
    return jnp.transpose(st.reshape(b, 64, N_HEADS_A, DK_A), (0, 2, 3, 1))


def _lambda(lv, lam_init):
    l01 = jnp.sum(lv[0:1, :] * lv[1:2, :], axis=-1, keepdims=True)
    l23 = jnp.sum(lv[2:3, :] * lv[3:4, :], axis=-1, keepdims=True)
    return jnp.exp(l01) - jnp.exp(l23) + lam_init


def _softmax_parts(parts):
    mx = functools.reduce(jnp.maximum, [jnp.max(p, axis=-1, keepdims=True) for p in parts])
    es = [jnp.exp(p - mx) for p in parts]
    den = functools.reduce(lambda a, b: a + b, [jnp.sum(e, axis=-1, keepdims=True) for e in es])
    return [e / den for e in es]


def _diff_finish(o, gain, lam_init):
    o = o * lax.rsqrt(jnp.mean(o * o, axis=-1, keepdims=True) + EPS)
    return ((o * gain) * (1.0 - lam_init)).astype(BF16)


def _attn_prompt_kernel(lv_ref, q_ref, k_ref, v_ref, gain_ref, o_ref, *, lam_init):
    lam = _lambda(lv_ref[...], lam_init)
    q = q_ref[...].astype(BF16)
    k = k_ref[...].astype(BF16)
    sc = DH_B ** -0.5
    (p1,) = _softmax_parts([_dot_nt(q[:, 0:64], k[:, 0:64]) * sc])
    (p2,) = _softmax_parts([_dot_nt(q[:, 64:128], k[:, 64:128]) * sc])
    a = p1 - lam * p2
    o = _dot(a.astype(BF16), v_ref[...].astype(BF16))
    o_ref[...] = _diff_finish(o, gain_ref[...], lam_init)


def _attn_prompt_call(lv, qb, kb, vb, gain, *, lam_init):
    blk = pl.BlockSpec((SEQ, DV_B), lambda b, h: (b, h))
    return pl.pallas_call(
        functools.partial(_attn_prompt_kernel, lam_init=lam_init),
        grid=(BATCH, N_HEADS_B),
        in_specs=[pl.BlockSpec((4, DH_B), lambda b, h: (0, 0)), blk, blk, blk,
                  pl.BlockSpec((1, DV_B), lambda b, h: (0, h))],
        out_specs=blk,
        out_shape=jax.ShapeDtypeStruct((T_PROMPT, W_B), BF16),
        compiler_params=_cparams(("arbitrary", "arbitrary")),
        name="diff_attn_prompt",
    )(lv, qb, kb, vb, gain)


def _rope(x, cos, sin_signed):
    lane = lax.broadcasted_iota(jnp.int32, x.shape, 1)
    first = (lane % (2 * AXIS_PAIRS)) < AXIS_PAIRS
    partner = jnp.where(first, pltpu.roll(x, LANES - AXIS_PAIRS, 1), pltpu.roll(x, AXIS_PAIRS, 1))
    return x * cos + partner * sin_signed


ATT_TQ = 256


def _attn_sample_kernel(lv_ref, q_ref, k_ref, v_ref, kc_ref, vc_ref, cosq_ref, sinq_ref,
                        cosk_ref, sink_ref, gain_ref, o_ref, kr_ref, *, lam_init):
    @pl.when(pl.program_id(2) == 0)
    def _():
        kr_ref[...] = _rope(k_ref[...], cosk_ref[...], sink_ref[...]).astype(BF16)

    lam = _lambda(lv_ref[...], lam_init)
    q = _rope(q_ref[...], cosq_ref[...], sinq_ref[...]).astype(BF16)
    k = kr_ref[...]
    kc = kc_ref[...].astype(BF16)
    sc = DH_B ** -0.5
    p1 = _softmax_parts([_dot_nt(q[:, 0:64], k[:, 0:64]) * sc, _dot_nt(q[:, 0:64], kc[:, 0:64]) * sc])
    p2 = _softmax_parts([_dot_nt(q[:, 64:128], k[:, 64:128]) * sc,
                         _dot_nt(q[:, 64:128], kc[:, 64:128]) * sc])
    a_own = p1[0] - lam * p2[0]
    a_ctx = p1[1] - lam * p2[1]
    o = _dot(a_own.astype(BF16), v_ref[...].astype(BF16)) + _dot(a_ctx.astype(BF16), vc_ref[...].astype(BF16))
    o_ref[...] = _diff_finish(o, gain_ref[...], lam_init)


def _attn_sample_call(lv, qb, kb, vb, kc, vc, cos, sin_signed, gain, *, lam_init):
    tq = ATT_TQ
    nq = DEC_SEQ // tq
    p0 = T_PROMPT // tq
    s0 = T_PROMPT // DEC_SEQ
    qblk = pl.BlockSpec((tq, DV_B), lambda b, h, t: (p0 + b * nq + t, h))
    kvblk = pl.BlockSpec((DEC_SEQ, DV_B), lambda b, h, t: (s0 + b, h))
    cblk = pl.BlockSpec((None, PAST_LEN, DV_B), lambda b, h, t: (b, 0, h))
    return pl.pallas_call(
        functools.partial(_attn_sample_kernel, lam_init=lam_init),
        grid=(DEC_BATCH, N_HEADS_B, nq),
        in_specs=[pl.BlockSpec((4, DH_B), lambda b, h, t: (0, 0)), qblk, kvblk, kvblk, cblk, cblk,
                  pl.BlockSpec((tq, DV_B), lambda b, h, t: (t, 0)),
                  pl.BlockSpec((tq, DV_B), lambda b, h, t: (t, 0)),
                  pl.BlockSpec((DEC_SEQ, DV_B), lambda b, h, t: (0, 0)),
                  pl.BlockSpec((DEC_SEQ, DV_B), lambda b, h, t: (0, 0)),
                  pl.BlockSpec((1, DV_B), lambda b, h, t: (0, h))],
        out_specs=pl.BlockSpec((tq, DV_B), lambda b, h, t: (b * nq + t, h)),
        out_shape=jax.ShapeDtypeStruct((T_SAMPLE, W_B), BF16),
        scratch_shapes=[pltpu.VMEM((DEC_SEQ, DV_B), BF16)],
        compiler_params=_cparams(("arbitrary", "arbitrary", "arbitrary")),
        name="diff_attn_sample",
    )(lv, qb, kb, vb, kc, vc, cos, sin_signed, cos, sin_signed, gain)


def _rope_tables():
    rows = DEC_SEQ // GRID_W
    row = jnp.repeat(jnp.arange(rows, dtype=F32), GRID_W)
    col = jnp.tile(jnp.arange(GRID_W, dtype=F32), rows)
    freqs = ROPE_THETA ** (-jnp.arange(AXIS_PAIRS, dtype=F32) / AXIS_PAIRS)
    ar, ac = row[:, None] * freqs, col[:, None] * freqs
    cos64 = jnp.concatenate([jnp.cos(ar), jnp.cos(ar), jnp.cos(ac), jnp.cos(ac)], axis=-1)
    sin64 = jnp.concatenate([-jnp.sin(ar), jnp.sin(ar), -jnp.sin(ac), jnp.sin(ac)], axis=-1)
    return jnp.tile(cos64, (1, 2)), jnp.tile(sin64, (1, 2))


SGU_TM = 512


def _sgu_kernel(uv_ref, ws_ref, bs_ref, o_ref):
    for n in range(SGU_TM // SGU_CHUNK):
        rows = slice(n * SGU_CHUNK, (n + 1) * SGU_CHUNK)
        u = _gelu_tanh(uv_ref[rows, 0:256])
        v = _gelu_tanh(uv_ref[rows, 256:512])
        ss = []
        for g in range(N_GROUPS_C):
            vg = v[:, g * DG_C:(g + 1) * DG_C]
            mu = jnp.mean(vg, axis=-1, keepdims=True)
            d = vg - mu
            vn = d * lax.rsqrt(jnp.mean(d * d, axis=-1, keepdims=True) + EPS)
            ss.append(_dot(ws_ref[g], vn.astype(BF16)))
        s = jnp.concatenate(ss, axis=-1) + bs_ref[...]
        o_ref[rows, :] = (u * s).astype(BF16)


def _sgu_call(uv, ws, bs_full):
    n = T_ALL // SGU_TM
    return pl.pallas_call(
        _sgu_kernel,
        grid=(n,),
        in_specs=[pl.BlockSpec((SGU_TM, 512), lambda i: (i, 0)),
                  pl.BlockSpec(ws.shape, lambda i: (0, 0, 0)),
                  pl.BlockSpec(bs_full.shape, lambda i: (0, 0))],
        out_specs=pl.BlockSpec((SGU_TM, W_C), lambda i: (i, 0)),
        out_shape=jax.ShapeDtypeStruct((T_ALL, W_C), BF16),
        compiler_params=_cparams(("arbitrary",)),
        name="sgu",
    )(uv, ws, bs_full)


OUT_TM = 512


def _top2_gates(logits):
    lane = lax.broadcasted_iota(jnp.int32, logits.shape, 1).astype(F32)
    neg = -jnp.inf
    lg = jnp.where(lane < N_EXPERTS, logits, neg)
    m1 = jnp.max(lg, axis=-1, keepdims=True)
    i1 = jnp.min(jnp.where(lg == m1, lane, float(LANES)), axis=-1, keepdims=True)
    lg2 = jnp.where(lane == i1, neg, lg)
    m2 = jnp.max(lg2, axis=-1, keepdims=True)
    i2 = jnp.min(jnp.where(lg2 == m2, lane, float(LANES)), axis=-1, keepdims=True)
    e2 = jnp.exp(m2 - m1)
    den = 1.0 + e2
    return jnp.where(lane == i1, 1.0 / den, 0.0) + jnp.where(lane == i2, e2 / den, 0.0)


def _out_proj_kernel(oa_ref, ob_ref, oc_ref, x_ref, wo_ref, g1_ref, nrm_ref, sc_ref, sh_ref, *rest, moe):
    if moe:
        rw_ref, x1_ref, h2_ref, gates_ref = rest
    else:
        x1_ref, h2_ref = rest
    y = (_dot(oa_ref[...], wo_ref[0:256, :]) + _dot(ob_ref[...], wo_ref[256:768, :])
         + _dot(oc_ref[...], wo_ref[768:1024, :]))
    x1 = x_ref[...] + g1_ref[...] * y
    x1_ref[...] = x1
    yn = x1 * lax.rsqrt(jnp.mean(x1 * x1, axis=-1, keepdims=True) + EPS)
    h = (yn * nrm_ref[...]) * (1.0 + sc_ref[...]) + sh_ref[...]
    h2_ref[...] = h.astype(BF16)
    if moe:
        gates_ref[...] = _top2_gates(_dot3(h, rw_ref[...]))


def _out_proj_call(oa, ob, oc, x, w_out, mod4, norm_g, router_w=None):
    tm = OUT_TM
    n = T_ALL // tm
    moe = router_w is not None
    row = functools.partial(_mod_row, tm=tm)
    mod_spec = lambda k: pl.BlockSpec((None, None, 1, D_MODEL), lambda i: (row(i), k, 0, 0))
    tok = lambda w: pl.BlockSpec((tm, w), lambda i: (i, 0))
    full = lambda a: pl.BlockSpec(a.shape, lambda i: (0,) * a.ndim)
    in_specs = [tok(W_A), tok(W_B), tok(W_C), tok(D_MODEL), full(w_out), mod_spec(2), full(norm_g),
                mod_spec(4), mod_spec(3)]
    args = [oa, ob, oc, x, w_out, mod4, norm_g, mod4, mod4]
    out_specs = [tok(D_MODEL), tok(D_MODEL)]
    out_shape = [jax.ShapeDtypeStruct((T_ALL, D_MODEL), F32), jax.ShapeDtypeStruct((T_ALL, D_MODEL), BF16)]
    if moe:
        in_specs.append(full(router_w))
        args.append(router_w)
        out_specs.append(tok(LANES))
        out_shape.append(jax.ShapeDtypeStruct((T_ALL, LANES), F32))
    return pl.pallas_call(
        functools.partial(_out_proj_kernel, moe=moe),
        grid=(n,),
        in_specs=in_specs,
        out_specs=out_specs,
        out_shape=out_shape,
        compiler_params=_cparams(("arbitrary",)),
        name="out_proj_moe" if moe else "out_proj",
    )(*args)


FFN_TM = 512
FFN_FC = 1408


def _ffn_kernel(h_ref, x_ref, g2_ref, *rest, moe, final, chunks_per_expert):
    rest = list(rest)
    gates_ref = rest.pop(0) if moe else None
    wg_ref, wu_ref, wd_ref = rest[0:3]
    rest = rest[3:]
    nf_ref = rest.pop(0) if final else None
    o_ref, acc_ref = rest
    j = pl.program_id(1)

    @pl.when(j == 0)
    def _():
        acc_ref[...] = jnp.zeros_like(acc_ref)

    h = h_ref[...]
    act = _silu(_dot(h, wg_ref[...])) * _dot(h, wu_ref[...])
    d = _dot(act.astype(BF16), wd_ref[...])
    if moe:
        e = j // chunks_per_expert
        lane = lax.broadcasted_iota(jnp.int32, gates_ref.shape, 1)
        gate = jnp.sum(jnp.where(lane == e, gates_ref[...], 0.0), axis=-1, keepdims=True)
        d = gate * d
    acc_ref[...] += d

    @pl.when(j == pl.num_programs(1) - 1)
    def _():
        out = x_ref[...] + g2_ref[...] * acc_ref[...]
        if final:
            out = (out * lax.rsqrt(jnp.mean(out * out, axis=-1, keepdims=True) + EPS)) * nf_ref[...]
        o_ref[...] = out


def _ffn_call(h2, x1, mod4, wg, wu, wd, gates=None, norm_f=None):
    tm, fc = FFN_TM, FFN_FC
    n = T_ALL // tm
    n_e = wg.shape[0]
    cpe = D_FF // fc
    moe = gates is not None
    final = norm_f is not None
    row = functools.partial(_mod_row, tm=tm)
    tok = lambda w: pl.BlockSpec((tm, w), lambda i, j: (i, 0))
    in_specs = [tok(D_MODEL), tok(D_MODEL),
                pl.BlockSpec((None, None, 1, D_MODEL), lambda i, j: (row(i), 5, 0, 0))]
    args = [h2, x1, mod4]
    if moe:
        in_specs.append(tok(LANES))
        args.append(gates)
    in_specs += [pl.BlockSpec((None, D_MODEL, fc), lambda i, j: (j // cpe, 0, j % cpe)),
                 pl.BlockSpec((None, D_MODEL, fc), lambda i, j: (j // cpe, 0, j % cpe)),
                 pl.BlockSpec((None, fc, D_MODEL), lambda i, j: (j // cpe, j % cpe, 0))]
    args += [wg, wu, wd]
    if final:
        in_specs.append(pl.BlockSpec((1, D_MODEL), lambda i, j: (0, 0)))
        args.append(norm_f)
    return pl.pallas_call(
        functools.partial(_ffn_kernel, moe=moe, final=final, chunks_per_expert=cpe),
        grid=(n, n_e * cpe),
        in_specs=in_specs,
        out_specs=tok(D_MODEL),
        out_shape=jax.ShapeDtypeStruct((T_ALL, D_MODEL), F32),
        scratch_shapes=[pltpu.VMEM((tm, D_MODEL), F32)],
        compiler_params=_cparams(("arbitrary", "arbitrary")),
        name="ffn_moe" if moe else "ffn_dense",
    )(*args)


def kernel(x_prompt, x_sample, cache_k, cache_v, state_gla_fwd, state_gla_bwd, c, c_ctx, w_ada, b_ada, norm_mix, norm_ffn, w_in, w_out, gla_w_up, gla_b_up, gla_norm, diff_lambda, diff_norm, sgu_w, sgu_b, ffn_w_gate, ffn_w_up, ffn_w_down, router_w, moe_w_gate, moe_w_up, moe_w_down, norm_f):
    x = jnp.concatenate([x_prompt.reshape(T_PROMPT, D_MODEL), x_sample.reshape(T_SAMPLE, D_MODEL)], axis=0)
    cvecs = jnp.concatenate([c_ctx[None, :], c, jnp.zeros((N_MOD_ROWS - 1 - DEC_BATCH, D_MODEL), F32)], axis=0)
    mod = _ada_call(cvecs, w_ada, b_ada)
    cos, sin_signed = _rope_tables()
    zeros_state = jnp.zeros((BATCH, 64, 256), F32)

    ks, vs, sfs, sbs = [], [], [], []
    for l in range(DEPTH):
        mod4 = mod[l].reshape(N_MOD_ROWS, 6, 1, D_MODEL)
        wl = w_in[l]
        w_main = jnp.concatenate([wl[:, 0:1024], wl[:, 1056:3104]], axis=1).astype(BF16)
        w_z = jnp.pad(wl[:, 1024:1056], ((0, 0), (0, LANES - 2 * GLA_RANK))).astype(BF16)
        w_up = jnp.zeros((LANES, 2 * W_A), F32)
        w_up = w_up.at[0:GLA_RANK, 0:W_A].set(gla_w_up[l, 0]).at[GLA_RANK:2 * GLA_RANK, W_A:].set(gla_w_up[l, 1])
        b_up = gla_b_up[l].reshape(1, 2 * W_A)
        g4, la, qb, kb, vb, uv = _in_proj_call(x, norm_mix[l][None, :], mod4, w_main, w_z,
                                               w_up.astype(BF16), b_up)

        gain_a = gla_norm[l][None, :]
        oa_p, sf, sb = _gla_call(g4, la, zeros_state, zeros_state, gain_a,
                                 batch=BATCH, seq=SEQ, row_block0=0)
        oa_s, _, _ = _gla_call(g4, la, _state_to_kernel(state_gla_fwd[:, l]),
                               _state_to_kernel(state_gla_bwd[:, l]), gain_a,
                               batch=DEC_BATCH, seq=DEC_SEQ, row_block0=T_PROMPT // DEC_SEQ)
        oa = jnp.concatenate([oa_p, oa_s], axis=0)

        lam_init = 0.8 - 0.6 * math.exp(-0.3 * l)
        gain_b = diff_norm[l][None, :]
        ob_p = _attn_prompt_call(diff_lambda[l], qb, kb, vb, gain_b, lam_init=lam_init)
        ob_s = _attn_sample_call(diff_lambda[l], qb, kb, vb,
                                 cache_k[:, l].reshape(DEC_BATCH, PAST_LEN, W_B),
                                 cache_v[:, l].reshape(DEC_BATCH, PAST_LEN, W_B),
                                 cos, sin_signed, gain_b, lam_init=lam_init)
        ob = jnp.concatenate([ob_p, ob_s], axis=0)

        bs_full = jnp.repeat(sgu_b[l].T, DG_C, axis=1)
        oc = _sgu_call(uv, sgu_w[l].astype(BF16), bs_full)

        last = l == DEPTH - 1
        nf = norm_f[None, :] if last else None
        if l % 2 == 0:
            x1, h2 = _out_proj_call(oa, ob, oc, x, w_out[l].astype(BF16), mod4, norm_ffn[l][None, :])
            e = l // 2
            x = _ffn_call(h2, x1, mod4, ffn_w_gate[e:e + 1].astype(BF16), ffn_w_up[e:e + 1].astype(BF16),
                          ffn_w_down[e:e + 1].astype(BF16), norm_f=nf)
        else:
            e = l // 2
            rw = jnp.pad(router_w[e], ((0, 0), (0, LANES - N_EXPERTS)))
            x1, h2, gates = _out_proj_call(oa, ob, oc, x, w_out[l].astype(BF16), mod4, norm_ffn[l][None, :], rw)
            x = _ffn_call(h2, x1, mod4, moe_w_gate[e].astype(BF16), moe_w_up[e].astype(BF16),
                          moe_w_down[e].astype(BF16), gates=gates, norm_f=nf)

        ks.append(kb[:T_PROMPT].reshape(BATCH, SEQ, N_HEADS_B, 2 * DH_B))
        vs.append(vb[:T_PROMPT].reshape(BATCH, SEQ, N_HEADS_B, DV_B))
        sfs.append(_state_from_kernel(sf))
        sbs.append(_state_from_kernel(sb))

    y_prompt = x[:T_PROMPT].reshape(BATCH, SEQ, D_MODEL)
    y_sample = x[T_PROMPT:].reshape(DEC_BATCH, DEC_SEQ, D_MODEL)
    return (y_prompt, y_sample, jnp.stack(ks, axis=1), jnp.stack(vs, axis=1),
            jnp.stack(sfs, axis=1), jnp.stack(sbs, axis=1))
```

```python
import functools
import math

import jax
import jax.numpy as jnp
import numpy as np
from jax import lax
from jax.experimental import pallas as pl
from jax.experimental.pallas import tpu as pltpu

F32 = jnp.float32
BF16 = jnp.bfloat16

D_MODEL = 1024
BATCH = 32
SEQ = 256
DEPTH = 2
DEC_BATCH = 2
DEC_SEQ = 1024
PAST_LEN = 256
GRID_W = 64
N_HEADS_A = 4
DK_A = 64
W_A = 256
GLA_RANK = 16
GLA_TAU = 16.0
GLA_CHUNK = 64
N_HEADS_B = 4
DH_B = 64
DV_B = 128
W_B = 512
ROPE_THETA = 10000.0
AXIS_PAIRS = DH_B // 4
N_GROUPS_C = 4
DG_C = 64
W_C = 256
SGU_CHUNK = 128
D_FF = 2816
N_EXPERTS = 8
EPS = 1e-6

T_PROMPT = BATCH * SEQ
T_SAMPLE = DEC_BATCH * DEC_SEQ
T_ALL = T_PROMPT + T_SAMPLE
N_MOD_ROWS = 8
LANES = 128
VMEM_LIMIT = 56 * 1024 * 1024


def _cparams(sem):
    return pltpu.CompilerParams(dimension_semantics=sem, vmem_limit_bytes=VMEM_LIMIT)


def _dot(a, b):
    return jnp.dot(a, b, preferred_element_type=F32)


def _dot_nt(a, b):
    return lax.dot_general(a, b, (((1,), (1,)), ((), ())), preferred_element_type=F32)


def _dot_tn(a, b):
    return lax.dot_general(a, b, (((0,), (0,)), ((), ())), preferred_element_type=F32)


def _split_bf16(x):
    hi = x.astype(BF16)
    lo = (x - hi.astype(F32)).astype(BF16)
    return hi, lo


def _dot3(a, w):
    a_hi, a_lo = _split_bf16(a)
    w_hi, w_lo = _split_bf16(w)
    return _dot(a_hi, w_hi) + (_dot(a_lo, w_hi) + _dot(a_hi, w_lo))


def _sigmoid(x):
    return 1.0 / (1.0 + jnp.exp(-x))


def _silu(x):
    return x * _sigmoid(x)


def _gelu_tanh(x):
    c = math.sqrt(2.0 / math.pi)
    return x * (0.5 * (1.0 + jnp.tanh(c * (x + 0.044715 * (x * x * x)))))


def _log_sigmoid(x):
    return jnp.minimum(x, 0.0) - jnp.log(1.0 + jnp.exp(-jnp.abs(x)))


def _mod_row(i, tm):
    n_p = T_PROMPT // tm
    per_b = DEC_SEQ // tm
    return jnp.where(i < n_p, 0, 1 + (i - n_p) // per_b)


ADA_TN = 1536


def _ada_kernel(c_ref, w_ref, b_ref, o_ref):
    a = _silu(c_ref[...])
    o_ref[...] = _dot3(a, w_ref[...]) + b_ref[...]


def _ada_call(cvecs, w_ada, b_ada):
    n_col = (6 * D_MODEL) // ADA_TN
    return pl.pallas_call(
        _ada_kernel,
        grid=(DEPTH, n_col),
        in_specs=[
            pl.BlockSpec((N_MOD_ROWS, D_MODEL), lambda l, j: (0, 0)),
            pl.BlockSpec((None, D_MODEL, ADA_TN), lambda l, j: (l, 0, j)),
            pl.BlockSpec((None, 1, ADA_TN), lambda l, j: (l, 0, j)),
        ],
        out_specs=pl.BlockSpec((None, N_MOD_ROWS, ADA_TN), lambda l, j: (l, 0, j)),
        out_shape=jax.ShapeDtypeStruct((DEPTH, N_MOD_ROWS, 6 * D_MODEL), F32),
        compiler_params=_cparams(("arbitrary", "arbitrary")),
        name="ada_mod",
    )(cvecs, w_ada, b_ada.reshape(DEPTH, 1, 6 * D_MODEL))


IN_TM = 256
W_MAIN = 3072


def _in_proj_kernel(x_ref, nrm_ref, sh_ref, sc_ref, wm_ref, wz_ref, wup_ref, bup_ref,
                    g4_ref, la_ref, qb_ref, kb_ref, vb_ref, uv_ref):
    x = x_ref[...]
    y = x * lax.rsqrt(jnp.mean(x * x, axis=-1, keepdims=True) + EPS)
    h = (y * nrm_ref[...]) * (1.0 + sc_ref[...]) + sh_ref[...]
    hb = h.astype(BF16)
    g4_ref[...] = _dot(hb, wm_ref[:, 0:1024])
    qb_ref[...] = _dot(hb, wm_ref[:, 1024:1536])
    kb_ref[...] = _dot(hb, wm_ref[:, 1536:2048])
    vb_ref[...] = _dot(hb, wm_ref[:, 2048:2560])
    uv_ref[...] = _dot(hb, wm_ref[:, 2560:3072])
    z = _dot(hb, wz_ref[...])
    zz = _dot(z.astype(BF16), wup_ref[...]) + bup_ref[...]
    la_ref[...] = _log_sigmoid(zz) * (1.0 / GLA_TAU)


def _in_proj_call(x, norm_g, mod4, w_main, w_z, w_up, b_up):
    tm = IN_TM
    n = T_ALL // tm
    row = functools.partial(_mod_row, tm=tm)
    mod_spec = lambda k: pl.BlockSpec((None, None, 1, D_MODEL), lambda i: (row(i), k, 0, 0))
    full = lambda a: pl.BlockSpec(a.shape, lambda i: (0,) * a.ndim)
    out = lambda w: pl.BlockSpec((tm, w), lambda i: (i, 0))
    return pl.pallas_call(
        _in_proj_kernel,
        grid=(n,),
        in_specs=[pl.BlockSpec((tm, D_MODEL), lambda i: (i, 0)), full(norm_g),
                  mod_spec(0), mod_spec(1), full(w_main), full(w_z), full(w_up), full(b_up)],
        out_specs=[out(1024), out(512), out(512), out(512), out(512), out(512)],
        out_shape=[jax.ShapeDtypeStruct((T_ALL, w), F32) for w in (1024, 512, 512, 512, 512, 512)],
        compiler_params=_cparams(("arbitrary",)),
        name="in_proj",
    )(x, norm_g, mod4, mod4, w_main, w_z, w_up, b_up)


def _gla_chunk(q, k, v, la, st, tri, mask, mid, last):
    la_hi, la_lo = _split_bf16(la)
    b = _dot(tri, la_hi) + _dot(tri, la_lo)
    bl = b[last:last + 1, :]
    m = b[mid:mid + 1, :]
    qe = (q * jnp.exp(b - m)).astype(BF16)
    ke = (k * jnp.exp(m - b)).astype(BF16)
    qi = (q * jnp.exp(b)).astype(BF16)
    ks = (k * jnp.exp(bl - b)).astype(BF16)
    vb = v.astype(BF16)
    stb = st.astype(BF16)
    outs, kvs = [], []
    for h in range(N_HEADS_A):
        sl = slice(h * DK_A, (h + 1) * DK_A)
        a = jnp.where(mask, _dot_nt(qe[:, sl], ke[:, sl]), 0.0)
        outs.append(_dot(a.astype(BF16), vb[:, sl]) + _dot_nt(qi[:, sl], stb[:, sl]))
        kvs.append(_dot_tn(vb[:, sl], ks[:, sl]))
    o = jnp.concatenate(outs, axis=-1)
    st_new = st * jnp.exp(bl) + jnp.concatenate(kvs, axis=-1)
    return o, st_new


def _gla_kernel(g4_ref, la_ref, s0f_ref, s0b_ref, gain_ref, o_ref, sf_ref, sb_ref, of_ref, *, seq):
    c = GLA_CHUNK
    nc = seq // c
    r = lax.broadcasted_iota(jnp.int32, (c, c), 0)
    s = lax.broadcasted_iota(jnp.int32, (c, c), 1)
    lower = s <= r
    upper = s >= r
    tri_f = jnp.where(lower, 1.0, 0.0).astype(BF16)
    tri_b = jnp.where(upper, 1.0, 0.0).astype(BF16)
    scale = DK_A ** -0.5

    def load(ci):
        rows = pl.ds(pl.multiple_of(ci * c, c), c)
        q = g4_ref[rows, 0:256] * scale
        k = g4_ref[rows, 256:512]
        v = g4_ref[rows, 512:768]
        return rows, q, k, v

    def fwd(ci, st):
        rows, q, k, v = load(ci)
        o, st = _gla_chunk(q, k, v, la_ref[rows, 0:256], st, tri_f, lower, c // 2 - 1, c - 1)
        of_ref[rows, :] = o
        return st

    sf_ref[...] = lax.fori_loop(0, nc, fwd, s0f_ref[...])

    gain = gain_ref[...]

    def bwd(i, st):
        ci = nc - 1 - i
        rows, q, k, v = load(ci)
        o, st = _gla_chunk(q, k, v, la_ref[rows, 256:512], st, tri_b, upper, c // 2, 0)
        o = o + of_ref[rows, :]
        ys = []
        for h in range(N_HEADS_A):
            oh = o[:, h * DK_A:(h + 1) * DK_A]
            ys.append(oh * lax.rsqrt(jnp.mean(oh * oh, axis=-1, keepdims=True) + EPS))
        y = jnp.concatenate(ys, axis=-1) * gain
        o_ref[rows, :] = (y * _silu(g4_ref[rows, 768:1024])).astype(BF16)
        return st

    sb_ref[...] = lax.fori_loop(0, nc, bwd, s0b_ref[...])


def _gla_call(g4, la, s0f, s0b, gain, *, batch, seq, row_block0):
    tok = lambda w: pl.BlockSpec((seq, w), lambda b: (row_block0 + b, 0))
    st = pl.BlockSpec((None, 64, 256), lambda b: (b, 0, 0))
    return pl.pallas_call(
        functools.partial(_gla_kernel, seq=seq),
        grid=(batch,),
        in_specs=[tok(1024), tok(512), st, st, pl.BlockSpec((1, W_A), lambda b: (0, 0))],
        out_specs=[pl.BlockSpec((seq, W_A), lambda b: (b, 0)), st, st],
        out_shape=[jax.ShapeDtypeStruct((batch * seq, W_A), BF16),
                   jax.ShapeDtypeStruct((batch, 64, 256), F32),
                   jax.ShapeDtypeStruct((batch, 64, 256), F32)],
        scratch_shapes=[pltpu.VMEM((seq, W_A), F32)],
        compiler_params=_cparams(("arbitrary",)),
        name=f"gla_{seq}",
    )(g4, la, s0f, s0b, gain)


def _state_to_kernel(s):
    b = s.shape[0]
    return jnp.transpose(s, (0, 3, 1, 2)).reshape(b, 64, 256)


def _state_from_kernel(st):
    b = st.shape[0]
    return jnp.transpose(st.reshape(b, 64, N_HEADS_A, DK_A), (0, 2, 3, 1))


def _lambda(lv, lam_init):
    l01 = jnp.sum(lv[0:1, :] * lv[1:2, :], axis=-1, keepdims=True)
    l23 = jnp.sum(lv[2:3, :] * lv[3:4, :], axis=-1, keepdims=True)
    return jnp.exp(l01) - jnp.exp(l23) + lam_init


def _softmax_parts(parts):
    mx = functools.reduce(jnp.maximum, [jnp.max(p, axis=-1, keepdims=True) for p in parts])
    es = [jnp.exp(p - mx) for p in parts]
    den = functools.reduce(lambda a, b: a + b, [jnp.sum(e, axis=-1, keepdims=True) for e in es])
    return [e / den for e in es]


def _diff_finish(o, gain, lam_init):
    o = o * lax.rsqrt(jnp.mean(o * o, axis=-1, keepdims=True) + EPS)
    return ((o * gain) * (1.0 - lam_init)).astype(BF16)


def _attn_prompt_kernel(lv_ref, q_ref, k_ref, v_ref, gain_ref, o_ref, *, lam_init):
    lam = _lambda(lv_ref[...], lam_init)
    q = q_ref[...].astype(BF16)
    k = k_ref[...].astype(BF16)
    sc = DH_B ** -0.5
    (p1,) = _softmax_parts([_dot_nt(q[:, 0:64], k[:, 0:64]) * sc])
    (p2,) = _softmax_parts([_dot_nt(q[:, 64:128], k[:, 64:128]) * sc])
    a = p1 - lam * p2
    o = _dot(a.astype(BF16), v_ref[...].astype(BF16))
    o_ref[...] = _diff_finish(o, gain_ref[...], lam_init)


def _attn_prompt_call(lv, qb, kb, vb, gain, *, lam_init):
    blk = pl.BlockSpec((SEQ, DV_B), lambda b, h: (b, h))
    return pl.pallas_call(
        functools.partial(_attn_prompt_kernel, lam_init=lam_init),
        grid=(BATCH, N_HEADS_B),
        in_specs=[pl.BlockSpec((4, DH_B), lambda b, h: (0, 0)), blk, blk, blk,
                  pl.BlockSpec((1, DV_B), lambda b, h: (0, h))],
        out_specs=blk,
        out_shape=jax.ShapeDtypeStruct((T_PROMPT, W_B), BF16),
        compiler_params=_cparams(("arbitrary", "arbitrary")),
        name="diff_attn_prompt",
    )(lv, qb, kb, vb, gain)


def _rope(x, cos, sin_signed):
    lane = lax.broadcasted_iota(jnp.int32, x.shape, 1)
    first = (lane % (2 * AXIS_PAIRS)) < AXIS_PAIRS
    partner = jnp.where(first, pltpu.roll(x, LANES - AXIS_PAIRS, 1), pltpu.roll(x, AXIS_PAIRS, 1))
    return x * cos + partner * sin_signed


ATT_TQ = 256


def _attn_sample_kernel(lv_ref, q_ref, k_ref, v_ref, kc_ref, vc_ref, cosq_ref, sinq_ref,
                        cosk_ref, sink_ref, gain_ref, o_ref, kr_ref, *, lam_init):
    @pl.when(pl.program_id(2) == 0)
    def _():
        kr_ref[...] = _rope(k_ref[...], cosk_ref[...], sink_ref[...]).astype(BF16)

    lam = _lambda(lv_ref[...], lam_init)
    q = _rope(q_ref[...], cosq_ref[...], sinq_ref[...]).astype(BF16)
    k = kr_ref[...]
    kc = kc_ref[...].astype(BF16)
    sc = DH_B ** -0.5
    p1 = _softmax_parts([_dot_nt(q[:, 0:64], k[:, 0:64]) * sc, _dot_nt(q[:, 0:64], kc[:, 0:64]) * sc])
    p2 = _softmax_parts([_dot_nt(q[:, 64:128], k[:, 64:128]) * sc,
                         _dot_nt(q[:, 64:128], kc[:, 64:128]) * sc])
    a_own = p1[0] - lam * p2[0]
    a_ctx = p1[1] - lam * p2[1]
    o = _dot(a_own.astype(BF16), v_ref[...].astype(BF16)) + _dot(a_ctx.astype(BF16), vc_ref[...].astype(BF16))
    o_ref[...] = _diff_finish(o, gain_ref[...], lam_init)


def _attn_sample_call(lv, qb, kb, vb, kc, vc, cos, sin_signed, gain, *, lam_init):
    tq = ATT_TQ
    nq = DEC_SEQ // tq
    p0 = T_PROMPT // tq
    s0 = T_PROMPT // DEC_SEQ
    qblk = pl.BlockSpec((tq, DV_B), lambda b, h, t: (p0 + b * nq + t, h))
    kvblk = pl.BlockSpec((DEC_SEQ, DV_B), lambda b, h, t: (s0 + b, h))
    cblk = pl.BlockSpec((None, PAST_LEN, DV_B), lambda b, h, t: (b, 0, h))
    return pl.pallas_call(
        functools.partial(_attn_sample_kernel, lam_init=lam_init),
        grid=(DEC_BATCH, N_HEADS_B, nq),
        in_specs=[pl.BlockSpec((4, DH_B), lambda b, h, t: (0, 0)), qblk, kvblk, kvblk, cblk, cblk,
                  pl.BlockSpec((tq, DV_B), lambda b, h, t: (t, 0)),
                  pl.BlockSpec((tq, DV_B), lambda b, h, t: (t, 0)),
                  pl.BlockSpec((DEC_SEQ, DV_B), lambda b, h, t: (0, 0)),
                  pl.BlockSpec((DEC_SEQ, DV_B), lambda b, h, t: (0, 0)),
                  pl.BlockSpec((1, DV_B), lambda b, h, t: (0, h))],
        out_specs=pl.BlockSpec((tq, DV_B), lambda b, h, t: (b * nq + t, h)),
        out_shape=jax.ShapeDtypeStruct((T_SAMPLE, W_B), BF16),
        scratch_shapes=[pltpu.VMEM((DEC_SEQ, DV_B), BF16)],
        compiler_params=_cparams(("arbitrary", "arbitrary", "arbitrary")),
        name="diff_attn_sample",
    )(lv, qb, kb, vb, kc, vc, cos, sin_signed, cos, sin_signed, gain)


def _rope_tables():
    rows = DEC_SEQ // GRID_W
    row = jnp.repeat(jnp.arange(rows, dtype=F32), GRID_W)
    col = jnp.tile(jnp.arange(GRID_W, dtype=F32), rows)
    freqs = ROPE_THETA ** (-jnp.arange(AXIS_PAIRS, dtype=F32) / AXIS_PAIRS)
    ar, ac = row[:, None] * freqs, col[:, None] * freqs
    cos64 = jnp.concatenate([jnp.cos(ar), jnp.cos(ar), jnp.cos(ac), jnp.cos(ac)], axis=-1)
    sin64 = jnp.concatenate([-jnp.sin(ar), jnp.sin(ar), -jnp.sin(ac), jnp.sin(ac)], axis=-1)
    return jnp.tile(cos64, (1, 2)), jnp.tile(sin64, (1, 2))


SGU_TM = 512


def _sgu_kernel(uv_ref, ws_ref, bs_ref, o_ref):
    for n in range(SGU_TM // SGU_CHUNK):
        rows = slice(n * SGU_CHUNK, (n + 1) * SGU_CHUNK)
        u = _gelu_tanh(uv_ref[rows, 0:256])
        v = _gelu_tanh(uv_ref[rows, 256:512])
        ss = []
        for g in range(N_GROUPS_C):
            vg = v[:, g * DG_C:(g + 1) * DG_C]
            mu = jnp.mean(vg, axis=-1, keepdims=True)
            d = vg - mu
            vn = d * lax.rsqrt(jnp.mean(d * d, axis=-1, keepdims=True) + EPS)
            ss.append(_dot(ws_ref[g], vn.astype(BF16)))
        s = jnp.concatenate(ss, axis=-1) + bs_ref[...]
        o_ref[rows, :] = (u * s).astype(BF16)


def _sgu_call(uv, ws, bs_full):
    n = T_ALL // SGU_TM
    return pl.pallas_call(
        _sgu_kernel,
        grid=(n,),
        in_specs=[pl.BlockSpec((SGU_TM, 512), lambda i: (i, 0)),
                  pl.BlockSpec(ws.shape, lambda i: (0, 0, 0)),
                  pl.BlockSpec(bs_full.shape, lambda i: (0, 0))],
        out_specs=pl.BlockSpec((SGU_TM, W_C), lambda i: (i, 0)),
        out_shape=jax.ShapeDtypeStruct((T_ALL, W_C), BF16),
        compiler_params=_cparams(("arbitrary",)),
        name="sgu",
    )(uv, ws, bs_full)


OUT_TM = 512


SEL_LANE0 = N_EXPERTS


def _top2_gates(logits):
    lane = lax.broadcasted_iota(jnp.int32, logits.shape, 1).astype(F32)
    neg = -jnp.inf
    lg = jnp.where(lane < N_EXPERTS, logits, neg)
    m1 = jnp.max(lg, axis=-1, keepdims=True)
    i1 = jnp.min(jnp.where(lg == m1, lane, float(LANES)), axis=-1, keepdims=True)
    lg2 = jnp.where(lane == i1, neg, lg)
    m2 = jnp.max(lg2, axis=-1, keepdims=True)
    i2 = jnp.min(jnp.where(lg2 == m2, lane, float(LANES)), axis=-1, keepdims=True)
    e2 = jnp.exp(m2 - m1)
    den = 1.0 + e2
    gates = jnp.where(lane == i1, 1.0 / den, 0.0) + jnp.where(lane == i2, e2 / den, 0.0)
    sel = jnp.where((lane == i1 + SEL_LANE0) | (lane == i2 + SEL_LANE0), 1.0, 0.0)
    return gates + sel


def _store_token_tiles(ref, val):
    n = val.shape[0]
    for k in range(D_MODEL // LANES):
        ref[pl.ds(k, n, stride=8), :] = val[:, k * LANES:(k + 1) * LANES]


def _load_token_tiles(ref, n):
    return jnp.concatenate([ref[pl.ds(k, n, stride=8), :] for k in range(D_MODEL // LANES)], axis=-1)


def _out_proj_kernel(oa_ref, ob_ref, oc_ref, x_ref, wo_ref, g1_ref, nrm_ref, sc_ref, sh_ref, *rest, moe):
    if moe:
        rw_ref, x1_ref, h2t_ref, gates_ref, gates_t_ref = rest
    else:
        x1_ref, h2_ref = rest
    y = (_dot(oa_ref[...], wo_ref[0:256, :]) + _dot(ob_ref[...], wo_ref[256:768, :])
         + _dot(oc_ref[...], wo_ref[768:1024, :]))
    x1 = x_ref[...] + g1_ref[...] * y
    x1_ref[...] = x1
    yn = x1 * lax.rsqrt(jnp.mean(x1 * x1, axis=-1, keepdims=True) + EPS)
    h = (yn * nrm_ref[...]) * (1.0 + sc_ref[...]) + sh_ref[...]
    if moe:
        _store_token_tiles(h2t_ref, h)
        gates = _top2_gates(_dot3(h, rw_ref[...]))
        gates_ref[...] = gates
        gates_t_ref[...] = gates.T
    else:
        h2_ref[...] = h.astype(BF16)


def _out_proj_call(oa, ob, oc, x, w_out, mod4, norm_g, router_w=None):
    tm = OUT_TM
    n = T_ALL // tm
    moe = router_w is not None
    row = functools.partial(_mod_row, tm=tm)
    mod_spec = lambda k: pl.BlockSpec((None, None, 1, D_MODEL), lambda i: (row(i), k, 0, 0))
    tok = lambda w: pl.BlockSpec((tm, w), lambda i: (i, 0))
    full = lambda a: pl.BlockSpec(a.shape, lambda i: (0,) * a.ndim)
    in_specs = [tok(W_A), tok(W_B), tok(W_C), tok(D_MODEL), full(w_out), mod_spec(2), full(norm_g),
                mod_spec(4), mod_spec(3)]
    args = [oa, ob, oc, x, w_out, mod4, norm_g, mod4, mod4]
    if moe:
        in_specs.append(full(router_w))
        args.append(router_w)
        out_specs = [tok(D_MODEL), pl.BlockSpec((tm * 8, LANES), lambda i: (i, 0)), tok(LANES),
                     pl.BlockSpec((LANES, tm), lambda i: (0, i))]
        out_shape = [jax.ShapeDtypeStruct((T_ALL, D_MODEL), F32), jax.ShapeDtypeStruct((T_ALL * 8, LANES), F32),
                     jax.ShapeDtypeStruct((T_ALL, LANES), F32), jax.ShapeDtypeStruct((LANES, T_ALL), F32)]
    else:
        out_specs = [tok(D_MODEL), tok(D_MODEL)]
        out_shape = [jax.ShapeDtypeStruct((T_ALL, D_MODEL), F32), jax.ShapeDtypeStruct((T_ALL, D_MODEL), BF16)]
    return pl.pallas_call(
        functools.partial(_out_proj_kernel, moe=moe),
        grid=(n,),
        in_specs=in_specs,
        out_specs=out_specs,
        out_shape=out_shape,
        compiler_params=_cparams(("arbitrary",)),
        name="out_proj_moe" if moe else "out_proj",
    )(*args)


FFN_TM = 512
FFN_FC = 1408
FFN_NC = D_FF // FFN_FC


def _swiglu_chunk(h, wg_ref, wu_ref, wd_ref):
    act = _silu(_dot(h, wg_ref[...])) * _dot(h, wu_ref[...])
    return _dot(act.astype(BF16), wd_ref[...])


def _finish(out, nf_ref):
    if nf_ref is None:
        return out
    return (out * lax.rsqrt(jnp.mean(out * out, axis=-1, keepdims=True) + EPS)) * nf_ref[...]


def _ffn_kernel(h_ref, x_ref, g2_ref, wg_ref, wu_ref, wd_ref, *rest, final):
    rest = list(rest)
    nf_ref = rest.pop(0) if final else None
    o_ref, acc_ref = rest
    j = pl.program_id(1)

    @pl.when(j == 0)
    def _():
        acc_ref[...] = jnp.zeros_like(acc_ref)

    acc_ref[...] += _swiglu_chunk(h_ref[...], wg_ref, wu_ref, wd_ref)

    @pl.when(j == FFN_NC - 1)
    def _():
        o_ref[...] = _finish(x_ref[...] + g2_ref[...] * acc_ref[...], nf_ref)


def _ffn_call(h2, x1, mod4, wg, wu, wd, norm_f=None):
    tm, fc = FFN_TM, FFN_FC
    final = norm_f is not None
    row = functools.partial(_mod_row, tm=tm)
    tok = lambda w: pl.BlockSpec((tm, w), lambda i, j: (i, 0))
    in_specs = [tok(D_MODEL), tok(D_MODEL),
                pl.BlockSpec((None, None, 1, D_MODEL), lambda i, j: (row(i), 5, 0, 0)),
                pl.BlockSpec((D_MODEL, fc), lambda i, j: (0, j)),
                pl.BlockSpec((D_MODEL, fc), lambda i, j: (0, j)),
                pl.BlockSpec((fc, D_MODEL), lambda i, j: (j, 0))]
    args = [h2, x1, mod4, wg, wu, wd]
    if final:
        in_specs.append(pl.BlockSpec((1, D_MODEL), lambda i, j: (0, 0)))
        args.append(norm_f)
    return pl.pallas_call(
        functools.partial(_ffn_kernel, final=final),
        grid=(T_ALL // tm, FFN_NC),
        in_specs=in_specs,
        out_specs=tok(D_MODEL),
        out_shape=jax.ShapeDtypeStruct((T_ALL, D_MODEL), F32),
        scratch_shapes=[pltpu.VMEM((tm, D_MODEL), F32)],
        compiler_params=_cparams(("arbitrary", "arbitrary")),
        name="ffn_dense",
    )(*args)


MOE_TM = 512
MOE_NT_MAX = (2 * T_ALL) // MOE_TM + N_EXPERTS
MOE_ROWS = MOE_NT_MAX * MOE_TM
PLAN_BLK = 512
MISC_LAST_START = 8
MISC_NT = 16


def _moe_plan_kernel(gt_ref, posa_ref, posb_ref, te_ref, ti_ref, misc_ref):
    tm = float(MOE_TM)
    sel = gt_ref[SEL_LANE0:SEL_LANE0 + N_EXPERTS, :]
    cnt = jnp.sum(sel, axis=1, keepdims=True)
    nt = jnp.floor((cnt + (tm - 1.0)) * (1.0 / tm))
    sub = lax.broadcasted_iota(jnp.int32, (N_EXPERTS, LANES), 0).astype(F32)
    lane = lax.broadcasted_iota(jnp.int32, (N_EXPERTS, LANES), 1).astype(F32)
    nt_b = jnp.broadcast_to(nt, (N_EXPERTS, LANES))
    nt_row = jnp.sum(jnp.where(sub == lane, nt_b, 0.0), axis=0, keepdims=True)
    toff = jnp.sum(jnp.where(lane < sub, jnp.broadcast_to(nt_row, (N_EXPERTS, LANES)), 0.0),
                   axis=1, keepdims=True)
    tend = toff + nt
    n_total = jnp.sum(nt, axis=0, keepdims=True)
    jc = jnp.minimum(lane, n_total - 1.0)
    te = jnp.sum(jnp.where(jc >= tend, 1.0, 0.0), axis=0, keepdims=True)
    te_ref[...] = te.astype(jnp.int32)
    ti_ref[...] = jc[0:1, :].astype(jnp.int32)
    last_start = (tend - 1.0) * tm
    ls_row = jnp.sum(jnp.where(sub + MISC_LAST_START == lane, jnp.broadcast_to(last_start, (N_EXPERTS, LANES)), 0.0),
                     axis=0, keepdims=True)
    nt_row2 = jnp.sum(jnp.where(sub + MISC_NT == lane, nt_b, 0.0), axis=0, keepdims=True)
    misc = jnp.where(lane[0:1, :] == 0.0, n_total, 0.0) + ls_row + nt_row2
    misc_ref[...] = misc.astype(jnp.int32)

    off = toff * tm
    r = lax.broadcasted_iota(jnp.int32, (PLAN_BLK, PLAN_BLK), 0)
    c = lax.broadcasted_iota(jnp.int32, (PLAN_BLK, PLAN_BLK), 1)
    upper = jnp.where(r <= c, 1.0, 0.0).astype(BF16)
    carry = jnp.zeros((N_EXPERTS, 1), F32)
    for blk in range(T_ALL // PLAN_BLK):
        cols = slice(blk * PLAN_BLK, (blk + 1) * PLAN_BLK)
        s = gt_ref[SEL_LANE0:SEL_LANE0 + N_EXPERTS, cols]
        rank = _dot(s.astype(BF16), upper) + carry
        pos = off + rank - 1.0
        posa_ref[:, cols] = jnp.min(jnp.where(s > 0.0, pos, 1e9), axis=0, keepdims=True).astype(jnp.int32)
        posb_ref[:, cols] = jnp.max(jnp.where(s > 0.0, pos, -1.0), axis=0, keepdims=True).astype(jnp.int32)
        carry = carry + jnp.sum(s, axis=1, keepdims=True)


def _moe_plan_call(gates_t):
    row = lambda w: jax.ShapeDtypeStruct((1, w), jnp.int32)
    full = lambda w: pl.BlockSpec((1, w), lambda: (0, 0))
    return pl.pallas_call(
        _moe_plan_kernel,
        in_specs=[pl.BlockSpec((LANES, T_ALL), lambda: (0, 0))],
        out_specs=[full(T_ALL), full(T_ALL), full(LANES), full(LANES), full(LANES)],
        out_shape=[row(T_ALL), row(T_ALL), row(LANES), row(LANES), row(LANES)],
        compiler_params=pltpu.CompilerParams(vmem_limit_bytes=VMEM_LIMIT),
        name="moe_plan",
    )(gates_t)


def _row_tile(ref, row):
    return ref.at[pl.ds(pl.multiple_of(row * 8, 8), 8), :]


def _moe_scatter_kernel(misc_ref, posa_ref, posb_ref, h_ref, xs_ref, zero_ref, sem):
    tm = h_ref.shape[0] // 8

    @pl.when(pl.program_id(0) == 0)
    def _():
        zero_ref[...] = jnp.zeros_like(zero_ref)

        def zero_tile(first_row):
            start = pl.multiple_of(first_row * 8, 8)
            cp = pltpu.make_async_copy(zero_ref, xs_ref.at[pl.ds(start, MOE_TM * 8), :], sem.at[0])
            cp.start()
            cp.wait()

        for e in range(N_EXPERTS):
            @pl.when(misc_ref[0, MISC_NT + e] > 0)
            def _():
                zero_tile(misc_ref[0, MISC_LAST_START + e])

        def zero_tail(j, carry):
            zero_tile(j * MOE_TM)
            return carry

        lax.fori_loop(misc_ref[0, 0], MOE_NT_MAX, zero_tail, 0)

    def issue(r, carry):
        src = _row_tile(h_ref, r)
        pltpu.make_async_copy(src, _row_tile(xs_ref, posa_ref[0, r]), sem.at[0]).start()
        pltpu.make_async_copy(src, _row_tile(xs_ref, posb_ref[0, r]), sem.at[0]).start()
        return carry

    lax.fori_loop(0, tm, issue, 0)

    def drain(r, carry):
        src = _row_tile(h_ref, r)
        pltpu.make_async_copy(src, _row_tile(xs_ref, posa_ref[0, r]), sem.at[0]).wait()
        pltpu.make_async_copy(src, _row_tile(xs_ref, posb_ref[0, r]), sem.at[0]).wait()
        return carry

    lax.fori_loop(0, tm, drain, 0)


SCATTER_TM = 512


def _moe_scatter_call(misc, posa3, posb3, h2t):
    tm = SCATTER_TM
    smem_row = pl.BlockSpec((None, 1, tm), lambda i: (i, 0, 0), memory_space=pltpu.SMEM)
    return pl.pallas_call(
        _moe_scatter_kernel,
        grid=(T_ALL // tm,),
        in_specs=[pl.BlockSpec((1, LANES), lambda i: (0, 0), memory_space=pltpu.SMEM), smem_row, smem_row,
                  pl.BlockSpec((tm * 8, LANES), lambda i: (i, 0))],
        out_specs=pl.BlockSpec(memory_space=pl.ANY),
        out_shape=jax.ShapeDtypeStruct((MOE_ROWS * 8, LANES), F32),
        scratch_shapes=[pltpu.VMEM((MOE_TM * 8, LANES), F32), pltpu.SemaphoreType.DMA((1,))],
        compiler_params=_cparams(("arbitrary",)),
        name="moe_scatter",
    )(misc, posa3, posb3, h2t)


def _ffn_grouped_kernel(te_ref, ti_ref, misc_ref, x_ref, wg_ref, wu_ref, wd_ref, o_ref, xb_ref, acc_ref):
    j = pl.program_id(0)
    c = pl.program_id(1)

    @pl.when(j < misc_ref[0])
    def _():
        @pl.when(c == 0)
        def _():
            xb_ref[...] = _load_token_tiles(x_ref, MOE_TM).astype(BF16)
            acc_ref[...] = jnp.zeros_like(acc_ref)

        acc_ref[...] += _swiglu_chunk(xb_ref[...], wg_ref, wu_ref, wd_ref)

        @pl.when(c == FFN_NC - 1)
        def _():
            _store_token_tiles(o_ref, acc_ref[...])

    @pl.when((j >= misc_ref[0]) & (c == FFN_NC - 1))
    def _():
        o_ref[...] = jnp.zeros_like(o_ref)


def _ffn_grouped_call(te, ti, misc, xs, wg, wu, wd):
    fc = FFN_FC

    def chunk(j, c, misc):
        return jnp.where(j < misc[0], c, FFN_NC - 1)

    grid_spec = pltpu.PrefetchScalarGridSpec(
        num_scalar_prefetch=3,
        grid=(MOE_NT_MAX, FFN_NC),
        in_specs=[pl.BlockSpec((MOE_TM * 8, LANES), lambda j, c, te, ti, misc: (ti[j], 0)),
                  pl.BlockSpec((None, D_MODEL, fc), lambda j, c, te, ti, misc: (te[j], 0, chunk(j, c, misc))),
                  pl.BlockSpec((None, D_MODEL, fc), lambda j, c, te, ti, misc: (te[j], 0, chunk(j, c, misc))),
                  pl.BlockSpec((None, fc, D_MODEL), lambda j, c, te, ti, misc: (te[j], chunk(j, c, misc), 0))],
        out_specs=pl.BlockSpec((MOE_TM * 8, LANES), lambda j, c, te, ti, misc: (j, 0)),
        scratch_shapes=[pltpu.VMEM((MOE_TM, D_MODEL), BF16), pltpu.VMEM((MOE_TM, D_MODEL), F32)],
    )
    return pl.pallas_call(
        _ffn_grouped_kernel,
        grid_spec=grid_spec,
        out_shape=jax.ShapeDtypeStruct((MOE_ROWS * 8, LANES), F32),
        compiler_params=_cparams(("arbitrary", "arbitrary")),
        name="ffn_grouped",
    )(te, ti, misc, xs, wg, wu, wd)


COMBINE_TM = 256


def _moe_combine_kernel(posa_ref, posb_ref, ys_ref, x_ref, g2_ref, gates_ref, *rest, final):
    rest = list(rest)
    nf_ref = rest.pop(0) if final else None
    o_ref, bufa_ref, bufb_ref, sem = rest
    tm = COMBINE_TM

    def issue(r, carry):
        pltpu.make_async_copy(_row_tile(ys_ref, posa_ref[0, r]), _row_tile(bufa_ref, r), sem.at[0]).start()
        pltpu.make_async_copy(_row_tile(ys_ref, posb_ref[0, r]), _row_tile(bufb_ref, r), sem.at[0]).start()
        return carry

    lax.fori_loop(0, tm, issue, 0)

    gates = gates_ref[...]
    lane = lax.broadcasted_iota(jnp.int32, gates.shape, 1).astype(F32)
    is_sel = (lane >= SEL_LANE0) & (lane < SEL_LANE0 + N_EXPERTS) & (gates > 0.0)
    ia = jnp.min(jnp.where(is_sel, lane, float(LANES)), axis=-1, keepdims=True) - SEL_LANE0
    ib = jnp.max(jnp.where(is_sel, lane, -1.0), axis=-1, keepdims=True) - SEL_LANE0
    wa = jnp.sum(jnp.where(lane == ia, gates, 0.0), axis=-1, keepdims=True)
    wb = jnp.sum(jnp.where(lane == ib, gates, 0.0), axis=-1, keepdims=True)

    def drain(r, carry):
        pltpu.make_async_copy(_row_tile(ys_ref, posa_ref[0, r]), _row_tile(bufa_ref, r), sem.at[0]).wait()
        pltpu.make_async_copy(_row_tile(ys_ref, posb_ref[0, r]), _row_tile(bufb_ref, r), sem.at[0]).wait()
        return carry

    lax.fori_loop(0, tm, drain, 0)

    y = wa * _load_token_tiles(bufa_ref, tm) + wb * _load_token_tiles(bufb_ref, tm)
    o_ref[...] = _finish(x_ref[...] + g2_ref[...] * y, nf_ref)


def _moe_combine_call(posa3, posb3, ys, x1, mod4, gates, norm_f=None):
    tm = COMBINE_TM
    final = norm_f is not None
    row = functools.partial(_mod_row, tm=tm)
    smem_row = pl.BlockSpec((None, 1, tm), lambda i: (i, 0, 0), memory_space=pltpu.SMEM)
    tok = lambda w: pl.BlockSpec((tm, w), lambda i: (i, 0))
    in_specs = [smem_row, smem_row, pl.BlockSpec(memory_space=pl.ANY), tok(D_MODEL),
                pl.BlockSpec((None, None, 1, D_MODEL), lambda i: (row(i), 5, 0, 0)), tok(LANES)]
    args = [posa3, posb3, ys, x1, mod4, gates]
    if final:
        in_specs.append(pl.BlockSpec((1, D_MODEL), lambda i: (0, 0)))
        args.append(norm_f)
    return pl.pallas_call(
        functools.partial(_moe_combine_kernel, final=final),
        grid=(T_ALL // tm,),
        in_specs=in_specs,
        out_specs=tok(D_MODEL),
        out_shape=jax.ShapeDtypeStruct((T_ALL, D_MODEL), F32),
        scratch_shapes=[pltpu.VMEM((tm * 8, LANES), F32), pltpu.VMEM((tm * 8, LANES), F32),
                        pltpu.SemaphoreType.DMA((1,))],
        compiler_params=_cparams(("arbitrary",)),
        name="moe_combine",
    )(*args)


def _moe_call(h2t, x1, mod4, gates, gates_t, wg, wu, wd, norm_f=None):
    posa, posb, te, ti, misc = _moe_plan_call(gates_t)
    xs = _moe_scatter_call(misc, posa.reshape(T_ALL // SCATTER_TM, 1, SCATTER_TM),
                           posb.reshape(T_ALL // SCATTER_TM, 1, SCATTER_TM), h2t)
    ys = _ffn_grouped_call(te.reshape(LANES), ti.reshape(LANES), misc.reshape(LANES), xs, wg, wu, wd)
    return _moe_combine_call(posa.reshape(T_ALL // COMBINE_TM, 1, COMBINE_TM),
                             posb.reshape(T_ALL // COMBINE_TM, 1, COMBINE_TM), ys, x1, mod4, gates, norm_f)


def kernel(x_prompt, x_sample, cache_k, cache_v, state_gla_fwd, state_gla_bwd, c, c_ctx, w_ada, b_ada, norm_mix, norm_ffn, w_in, w_out, gla_w_up, gla_b_up, gla_norm, diff_lambda, diff_norm, sgu_w, sgu_b, ffn_w_gate, ffn_w_up, ffn_w_down, router_w, moe_w_gate, moe_w_up, moe_w_down, norm_f):
    x = jnp.concatenate([x_prompt.reshape(T_PROMPT, D_MODEL), x_sample.reshape(T_SAMPLE, D_MODEL)], axis=0)
    cvecs = jnp.concatenate([c_ctx[None, :], c, jnp.zeros((N_MOD_ROWS - 1 - DEC_BATCH, D_MODEL), F32)], axis=0)
    mod = _ada_call(cvecs, w_ada, b_ada)
    cos, sin_signed = _rope_tables()
    zeros_state = jnp.zeros((BATCH, 64, 256), F32)

    ks, vs, sfs, sbs = [], [], [], []
    for l in range(DEPTH):
        mod4 = mod[l].reshape(N_MOD_ROWS, 6, 1, D_MODEL)
        wl = w_in[l]
        w_main = jnp.concatenate([wl[:, 0:1024], wl[:, 1056:3104]], axis=1).astype(BF16)
        w_z = jnp.pad(wl[:, 1024:1056], ((0, 0), (0, LANES - 2 * GLA_RANK))).astype(BF16)
        w_up = jnp.zeros((LANES, 2 * W_A), F32)
        w_up = w_up.at[0:GLA_RANK, 0:W_A].set(gla_w_up[l, 0]).at[GLA_RANK:2 * GLA_RANK, W_A:].set(gla_w_up[l, 1])
        b_up = gla_b_up[l].reshape(1, 2 * W_A)
        g4, la, qb, kb, vb, uv = _in_proj_call(x, norm_mix[l][None, :], mod4, w_main, w_z,
                                               w_up.astype(BF16), b_up)

        gain_a = gla_norm[l][None, :]
        oa_p, sf, sb = _gla_call(g4, la, zeros_state, zeros_state, gain_a,
                                 batch=BATCH, seq=SEQ, row_block0=0)
        oa_s, _, _ = _gla_call(g4, la, _state_to_kernel(state_gla_fwd[:, l]),
                               _state_to_kernel(state_gla_bwd[:, l]), gain_a,
                               batch=DEC_BATCH, seq=DEC_SEQ, row_block0=T_PROMPT // DEC_SEQ)
        oa = jnp.concatenate([oa_p, oa_s], axis=0)

        lam_init = 0.8 - 0.6 * math.exp(-0.3 * l)
        gain_b = diff_norm[l][None, :]
        ob_p = _attn_prompt_call(diff_lambda[l], qb, kb, vb, gain_b, lam_init=lam_init)
        ob_s = _attn_sample_call(diff_lambda[l], qb, kb, vb,
                                 cache_k[:, l].reshape(DEC_BATCH, PAST_LEN, W_B),
                                 cache_v[:, l].reshape(DEC_BATCH, PAST_LEN, W_B),
                                 cos, sin_signed, gain_b, lam_init=lam_init)
        ob = jnp.concatenate([ob_p, ob_s], axis=0)

        bs_full = jnp.repeat(sgu_b[l].T, DG_C, axis=1)
        oc = _sgu_call(uv, sgu_w[l].astype(BF16), bs_full)

        last = l == DEPTH - 1
        nf = norm_f[None, :] if last else None
        if l % 2 == 0:
            x1, h2 = _out_proj_call(oa, ob, oc, x, w_out[l].astype(BF16), mod4, norm_ffn[l][None, :])
            e = l // 2
            x = _ffn_call(h2, x1, mod4, ffn_w_gate[e].astype(BF16), ffn_w_up[e].astype(BF16),
                          ffn_w_down[e].astype(BF16), norm_f=nf)
        else:
            e = l // 2
            rw = jnp.pad(router_w[e], ((0, 0), (0, LANES - N_EXPERTS)))
            x1, h2t, gates, gates_t = _out_proj_call(oa, ob, oc, x, w_out[l].astype(BF16), mod4,
                                                     norm_ffn[l][None, :], rw)
            x = _moe_call(h2t, x1, mod4, gates, gates_t, moe_w_gate[e].astype(BF16), moe_w_up[e].astype(BF16),
                          moe_w_down[e].astype(BF16), norm_f=nf)

        ks.append(kb[:T_PROMPT].reshape(BATCH, SEQ, N_HEADS_B, 2 * DH_B))
        vs.append(vb[:T_PROMPT].reshape(BATCH, SEQ, N_HEADS_B, DV_B))
        sfs.append(_state_from_kernel(sf))
        sbs.append(_state_from_kernel(sb))

    y_prompt = x[:T_PROMPT].reshape(BATCH, SEQ, D_MODEL)
    y_sample = x[T_PROMPT:].reshape(DEC_BATCH, DEC_SEQ, D_MODEL)
    return (y_prompt, y_sample, jnp.stack(ks, axis=1), jnp.stack(vs, axis=1),
            jnp.stack(sfs, axis=1), jnp.stack(sbs, axis=1))
```

```python
import functools
import math

import jax
import jax.numpy as jnp
import numpy as np
from jax import lax
from jax.experimental import pallas as pl
from jax.experimental.pallas import tpu as pltpu

F32 = jnp.float32
BF16 = jnp.bfloat16

D_MODEL = 1024
BATCH = 32
SEQ = 256
DEPTH = 2
DEC_BATCH = 2
DEC_SEQ = 1024
PAST_LEN = 256
GRID_W = 64
N_HEADS_A = 4
DK_A = 64
W_A = 256
GLA_RANK = 16
GLA_TAU = 16.0
GLA_CHUNK = 64
N_HEADS_B = 4
DH_B = 64
DV_B = 128
W_B = 512
ROPE_THETA = 10000.0
AXIS_PAIRS = DH_B // 4
N_GROUPS_C = 4
DG_C = 64
W_C = 256
SGU_CHUNK = 128
D_FF = 2816
N_EXPERTS = 8
EPS = 1e-6

T_PROMPT = BATCH * SEQ
T_SAMPLE = DEC_BATCH * DEC_SEQ
T_ALL = T_PROMPT + T_SAMPLE
N_MOD_ROWS = 8
LANES = 128
VMEM_LIMIT = 56 * 1024 * 1024


def _cparams(sem):
    return pltpu.CompilerParams(dimension_semantics=sem, vmem_limit_bytes=VMEM_LIMIT)


def _dot(a, b):
    return jnp.dot(a, b, preferred_element_type=F32)


def _dot_nt(a, b):
    return lax.dot_general(a, b, (((1,), (1,)), ((), ())), preferred_element_type=F32)


def _dot_tn(a, b):
    return lax.dot_general(a, b, (((0,), (0,)), ((), ())), preferred_element_type=F32)


def _split_bf16(x):
    hi = x.astype(BF16)
    lo = (x - hi.astype(F32)).astype(BF16)
    return hi, lo


def _dot3(a, w):
    a_hi, a_lo = _split_bf16(a)
    w_hi, w_lo = _split_bf16(w)
    return _dot(a_hi, w_hi) + (_dot(a_lo, w_hi) + _dot(a_hi, w_lo))


def _sigmoid(x):
    return 1.0 / (1.0 + jnp.exp(-x))


def _silu(x):
    return x * _sigmoid(x)


def _gelu_tanh(x):
    c = math.sqrt(2.0 / math.pi)
    return x * (0.5 * (1.0 + jnp.tanh(c * (x + 0.044715 * (x * x * x)))))


def _log_sigmoid(x):
    return jnp.minimum(x, 0.0) - jnp.log(1.0 + jnp.exp(-jnp.abs(x)))


def _mod_row(i, tm):
    n_p = T_PROMPT // tm
    per_b = DEC_SEQ // tm
    return jnp.where(i < n_p, 0, 1 + (i - n_p) // per_b)


ADA_TN = 1536


def _ada_kernel(c_ref, w_ref, b_ref, o_ref):
    a = _silu(c_ref[...])
    o_ref[...] = _dot3(a, w_ref[...]) + b_ref[...]


def _ada_call(cvecs, w_ada, b_ada):
    n_col = (6 * D_MODEL) // ADA_TN
    return pl.pallas_call(
        _ada_kernel,
        grid=(DEPTH, n_col),
        in_specs=[
            pl.BlockSpec((N_MOD_ROWS, D_MODEL), lambda l, j: (0, 0)),
            pl.BlockSpec((None, D_MODEL, ADA_TN), lambda l, j: (l, 0, j)),
            pl.BlockSpec((None, 1, ADA_TN), lambda l, j: (l, 0, j)),
        ],
        out_specs=pl.BlockSpec((None, N_MOD_ROWS, ADA_TN), lambda l, j: (l, 0, j)),
        out_shape=jax.ShapeDtypeStruct((DEPTH, N_MOD_ROWS, 6 * D_MODEL), F32),
        compiler_params=_cparams(("arbitrary", "arbitrary")),
        name="ada_mod",
    )(cvecs, w_ada, b_ada.reshape(DEPTH, 1, 6 * D_MODEL))


IN_TM = 256
W_MAIN = 3072


def _in_proj_kernel(x_ref, nrm_ref, sh_ref, sc_ref, wm_ref, wz_ref, wup_ref, bup_ref,
                    g4_ref, la_ref, qb_ref, kb_ref, vb_ref, uv_ref):
    x = x_ref[...]
    y = x * lax.rsqrt(jnp.mean(x * x, axis=-1, keepdims=True) + EPS)
    h = (y * nrm_ref[...]) * (1.0 + sc_ref[...]) + sh_ref[...]
    hb = h.astype(BF16)
    g4_ref[...] = _dot(hb, wm_ref[:, 0:1024])
    qb_ref[...] = _dot(hb, wm_ref[:, 1024:1536])
    kb_ref[...] = _dot(hb, wm_ref[:, 1536:2048])
    vb_ref[...] = _dot(hb, wm_ref[:, 2048:2560])
    uv_ref[...] = _dot(hb, wm_ref[:, 2560:3072])
    z = _dot(hb, wz_ref[...])
    zz = _dot(z.astype(BF16), wup_ref[...]) + bup_ref[...]
    la_ref[...] = _log_sigmoid(zz) * (1.0 / GLA_TAU)


def _in_proj_call(x, norm_g, mod4, w_main, w_z, w_up, b_up):
    tm = IN_TM
    n = T_ALL // tm
    row = functools.partial(_mod_row, tm=tm)
    mod_spec = lambda k: pl.BlockSpec((None, None, 1, D_MODEL), lambda i: (row(i), k, 0, 0))
    full = lambda a: pl.BlockSpec(a.shape, lambda i: (0,) * a.ndim)
    out = lambda w: pl.BlockSpec((tm, w), lambda i: (i, 0))
    return pl.pallas_call(
        _in_proj_kernel,
        grid=(n,),
        in_specs=[pl.BlockSpec((tm, D_MODEL), lambda i: (i, 0)), full(norm_g),
                  mod_spec(0), mod_spec(1), full(w_main), full(w_z), full(w_up), full(b_up)],
        out_specs=[out(1024), out(512), out(512), out(512), out(512), out(512)],
        out_shape=[jax.ShapeDtypeStruct((T_ALL, w), F32) for w in (1024, 512, 512, 512, 512, 512)],
        compiler_params=_cparams(("arbitrary",)),
        name="in_proj",
    )(x, norm_g, mod4, mod4, w_main, w_z, w_up, b_up)


GLA_SB = 256
GLA_NC = GLA_SB // GLA_CHUNK


def _gla_superblock(q, k, vb, v_heads, la, st_all, tri, mask, same64, head_lanes, forward):
    c = GLA_CHUNK
    mid, last = (c // 2 - 1, c - 1) if forward else (c // 2, 0)
    la_hi, la_lo = _split_bf16(la)
    b = _dot(tri, la_hi) + _dot(tri, la_lo)
    rows_of = lambda r: jnp.concatenate(
        [jnp.broadcast_to(b[i * c + r:i * c + r + 1, :], (c, W_A)) for i in range(GLA_NC)], axis=0)
    m = rows_of(mid)
    bl = rows_of(last)
    qe = (q * jnp.exp(b - m)).astype(BF16)
    ke = k * jnp.exp(m - b)
    qi = (q * jnp.exp(b)).astype(BF16)
    ks = (k * jnp.exp(bl - b)).astype(BF16)
    o = jnp.zeros((GLA_SB, W_A), F32)
    for h in range(N_HEADS_A):
        ke_h = jnp.where(head_lanes[h], ke, 0.0).astype(BF16)
        a = jnp.where(mask, _dot_nt(qe, ke_h), 0.0).astype(BF16)
        o = o + _dot(a, v_heads[h])
    outs = [None] * GLA_NC
    for i in (range(GLA_NC) if forward else reversed(range(GLA_NC))):
        rows = slice(i * c, (i + 1) * c)
        outs[i] = o[rows, :] + _dot_nt(qi[rows, :], st_all.astype(BF16))
        kv = _dot_tn(vb[rows, :], ks[rows, :])
        st_all = st_all * jnp.exp(bl[i * c:i * c + 1, :]) + jnp.where(same64, kv, 0.0)
    return jnp.concatenate(outs, axis=0), st_all


def _gla_kernel(g4_ref, la_ref, s0f_ref, s0b_ref, gain_ref, o_ref, sf_ref, sb_ref, of_ref, ob_ref, *, seq):
    n = GLA_SB
    nsb = seq // n
    r = lax.broadcasted_iota(jnp.int32, (n, n), 0)
    s = lax.broadcasted_iota(jnp.int32, (n, n), 1)
    same64 = (r // GLA_CHUNK) == (s // GLA_CHUNK)
    lower = same64 & (s <= r)
    upper = same64 & (s >= r)
    tri_f = jnp.where(lower, 1.0, 0.0).astype(BF16)
    tri_b = jnp.where(upper, 1.0, 0.0).astype(BF16)
    ones64 = jnp.where(same64, 1.0, 0.0).astype(BF16)
    head_lanes = [(s // DK_A) == h for h in range(N_HEADS_A)]
    scale = DK_A ** -0.5
    expand = lambda st: jnp.where(same64, jnp.concatenate([st] * N_HEADS_A, axis=0), 0.0)
    compact = lambda st_all: functools.reduce(
        lambda a, b: a + b, [st_all[h * 64:(h + 1) * 64, :] for h in range(N_HEADS_A)])

    def step(i, carry):
        stf, stb = carry
        rf = pl.ds(pl.multiple_of(i * n, n), n)
        rb = pl.ds(pl.multiple_of((nsb - 1 - i) * n, n), n)
        for rows, forward in ((rf, True), (rb, False)):
            q = g4_ref[rows, 0:256] * scale
            k = g4_ref[rows, 256:512]
            v = g4_ref[rows, 512:768]
            vb = v.astype(BF16)
            v_heads = [jnp.where(head_lanes[h], v, 0.0).astype(BF16) for h in range(N_HEADS_A)]
            if forward:
                o, stf = _gla_superblock(q, k, vb, v_heads, la_ref[rows, 0:256], stf, tri_f, lower, same64,
                                         head_lanes, True)
                of_ref[rows, :] = o
            else:
                o, stb = _gla_superblock(q, k, vb, v_heads, la_ref[rows, 256:512], stb, tri_b, upper, same64,
                                         head_lanes, False)
                ob_ref[rows, :] = o
        return stf, stb

    stf, stb = lax.fori_loop(0, nsb, step, (expand(s0f_ref[...]), expand(s0b_ref[...])))
    sf_ref[...] = compact(stf)
    sb_ref[...] = compact(stb)

    gain = gain_ref[...]

    def finish(i, carry):
        rows = pl.ds(pl.multiple_of(i * n, n), n)
        o = of_ref[rows, :] + ob_ref[rows, :]
        sq_hi, sq_lo = _split_bf16(o * o)
        ms = (_dot(sq_hi, ones64) + _dot(sq_lo, ones64)) * (1.0 / DK_A)
        y = (o * lax.rsqrt(ms + EPS)) * gain
        o_ref[rows, :] = (y * _silu(g4_ref[rows, 768:1024])).astype(BF16)
        return carry

    lax.fori_loop(0, nsb, finish, 0)


def _gla_call(g4, la, s0f, s0b, gain, *, batch, seq, row_block0):
    tok = lambda w: pl.BlockSpec((seq, w), lambda b: (row_block0 + b, 0))
    st = pl.BlockSpec((None, 64, 256), lambda b: (b, 0, 0))
    return pl.pallas_call(
        functools.partial(_gla_kernel, seq=seq),
        grid=(batch,),
        in_specs=[tok(1024), tok(512), st, st, pl.BlockSpec((1, W_A), lambda b: (0, 0))],
        out_specs=[pl.BlockSpec((seq, W_A), lambda b: (b, 0)), st, st],
        out_shape=[jax.ShapeDtypeStruct((batch * seq, W_A), BF16),
                   jax.ShapeDtypeStruct((batch, 64, 256), F32),
                   jax.ShapeDtypeStruct((batch, 64, 256), F32)],
        scratch_shapes=[pltpu.VMEM((seq, W_A), F32), pltpu.VMEM((seq, W_A), F32)],
        compiler_params=_cparams(("arbitrary",)),
        name=f"gla_{seq}",
    )(g4, la, s0f, s0b, gain)


def _state_to_kernel(s):
    b = s.shape[0]
    return jnp.transpose(s, (0, 3, 1, 2)).reshape(b, 64, 256)


def _state_from_kernel(st):
    b = st.shape[0]
    return jnp.transpose(st.reshape(b, 64, N_HEADS_A, DK_A), (0, 2, 3, 1))


def _lambda(lv, lam_init):
    l01 = jnp.sum(lv[0:1, :] * lv[1:2, :], axis=-1, keepdims=True)
    l23 = jnp.sum(lv[2:3, :] * lv[3:4, :], axis=-1, keepdims=True)
    return jnp.exp(l01) - jnp.exp(l23) + lam_init


def _softmax_parts(parts):
    mx = functools.reduce(jnp.maximum, [jnp.max(p, axis=-1, keepdims=True) for p in parts])
    es = [jnp.exp(p - mx) for p in parts]
    den = functools.reduce(lambda a, b: a + b, [jnp.sum(e, axis=-1, keepdims=True) for e in es])
    return [e / den for e in es]


def _diff_finish(o, gain, lam_init):
    o = o * lax.rsqrt(jnp.mean(o * o, axis=-1, keepdims=True) + EPS)
    return ((o * gain) * (1.0 - lam_init)).astype(BF16)


QK_SCALE = DH_B ** -0.5


def _key_halves(k):
    first = lax.broadcasted_iota(jnp.int32, k.shape, 1) < DH_B
    return jnp.where(first, k, 0.0).astype(BF16), jnp.where(first, 0.0, k).astype(BF16)


def _attn_prompt_kernel(lv_ref, q_ref, k_ref, v_ref, gain_ref, o_ref, *, lam_init):
    lam = _lambda(lv_ref[...], lam_init)
    for h in range(N_HEADS_B):
        cols = slice(h * DV_B, (h + 1) * DV_B)
        q = (q_ref[:, cols] * QK_SCALE).astype(BF16)
        k1, k2 = _key_halves(k_ref[:, cols])
        (p1,) = _softmax_parts([_dot_nt(q, k1)])
        (p2,) = _softmax_parts([_dot_nt(q, k2)])
        a = p1 - lam * p2
        o = _dot(a.astype(BF16), v_ref[:, cols].astype(BF16))
        o_ref[:, cols] = _diff_finish(o, gain_ref[:, cols], lam_init)


def _attn_prompt_call(lv, qb, kb, vb, gain, *, lam_init):
    blk = pl.BlockSpec((SEQ, W_B), lambda b: (b, 0))
    return pl.pallas_call(
        functools.partial(_attn_prompt_kernel, lam_init=lam_init),
        grid=(BATCH,),
        in_specs=[pl.BlockSpec((4, DH_B), lambda b: (0, 0)), blk, blk, blk,
                  pl.BlockSpec((1, W_B), lambda b: (0, 0))],
        out_specs=blk,
        out_shape=jax.ShapeDtypeStruct((T_PROMPT, W_B), BF16),
        compiler_params=_cparams(("arbitrary",)),
        name="diff_attn_prompt",
    )(lv, qb, kb, vb, gain)


def _rope(x, cos, sin_signed):
    lane = lax.broadcasted_iota(jnp.int32, x.shape, 1)
    first = (lane % (2 * AXIS_PAIRS)) < AXIS_PAIRS
    partner = jnp.where(first, pltpu.roll(x, LANES - AXIS_PAIRS, 1), pltpu.roll(x, AXIS_PAIRS, 1))
    return x * cos + partner * sin_signed


ATT_TQ = 256


def _attn_sample_kernel(lv_ref, q_ref, k_ref, v_ref, kc_ref, vc_ref, cosq_ref, sinq_ref,
                        cosk_ref, sink_ref, gain_ref, o_ref, k1_ref, k2_ref, *, lam_init):
    @pl.when(pl.program_id(1) == 0)
    def _():
        for h in range(N_HEADS_B):
            cols = slice(h * DV_B, (h + 1) * DV_B)
            k1_ref[:, cols], k2_ref[:, cols] = _key_halves(_rope(k_ref[:, cols], cosk_ref[...], sink_ref[...]))

    lam = _lambda(lv_ref[...], lam_init)
    for h in range(N_HEADS_B):
        cols = slice(h * DV_B, (h + 1) * DV_B)
        q = (_rope(q_ref[:, cols], cosq_ref[...], sinq_ref[...]) * QK_SCALE).astype(BF16)
        c1, c2 = _key_halves(kc_ref[:, cols])
        p1 = _softmax_parts([_dot_nt(q, k1_ref[:, cols]), _dot_nt(q, c1)])
        p2 = _softmax_parts([_dot_nt(q, k2_ref[:, cols]), _dot_nt(q, c2)])
        a_own = p1[0] - lam * p2[0]
        a_ctx = p1[1] - lam * p2[1]
        o = (_dot(a_own.astype(BF16), v_ref[:, cols].astype(BF16))
             + _dot(a_ctx.astype(BF16), vc_ref[:, cols].astype(BF16)))
        o_ref[:, cols] = _diff_finish(o, gain_ref[:, cols], lam_init)


def _attn_sample_call(lv, qb, kb, vb, kc, vc, cos, sin_signed, gain, *, lam_init):
    tq = ATT_TQ
    nq = DEC_SEQ // tq
    p0 = T_PROMPT // tq
    s0 = T_PROMPT // DEC_SEQ
    qblk = pl.BlockSpec((tq, W_B), lambda b, t: (p0 + b * nq + t, 0))
    kvblk = pl.BlockSpec((DEC_SEQ, W_B), lambda b, t: (s0 + b, 0))
    cblk = pl.BlockSpec((None, PAST_LEN, W_B), lambda b, t: (b, 0, 0))
    return pl.pallas_call(
        functools.partial(_attn_sample_kernel, lam_init=lam_init),
        grid=(DEC_BATCH, nq),
        in_specs=[pl.BlockSpec((4, DH_B), lambda b, t: (0, 0)), qblk, kvblk, kvblk, cblk, cblk,
                  pl.BlockSpec((tq, DV_B), lambda b, t: (t, 0)),
                  pl.BlockSpec((tq, DV_B), lambda b, t: (t, 0)),
                  pl.BlockSpec((DEC_SEQ, DV_B), lambda b, t: (0, 0)),
                  pl.BlockSpec((DEC_SEQ, DV_B), lambda b, t: (0, 0)),
                  pl.BlockSpec((1, W_B), lambda b, t: (0, 0))],
        out_specs=pl.BlockSpec((tq, W_B), lambda b, t: (b * nq + t, 0)),
        out_shape=jax.ShapeDtypeStruct((T_SAMPLE, W_B), BF16),
        scratch_shapes=[pltpu.VMEM((DEC_SEQ, W_B), BF16), pltpu.VMEM((DEC_SEQ, W_B), BF16)],
        compiler_params=_cparams(("arbitrary", "arbitrary")),
        name="diff_attn_sample",
    )(lv, qb, kb, vb, kc, vc, cos, sin_signed, cos, sin_signed, gain)


def _rope_tables():
    rows = DEC_SEQ // GRID_W
    row = jnp.repeat(jnp.arange(rows, dtype=F32), GRID_W)
    col = jnp.tile(jnp.arange(GRID_W, dtype=F32), rows)
    freqs = ROPE_THETA ** (-jnp.arange(AXIS_PAIRS, dtype=F32) / AXIS_PAIRS)
    ar, ac = row[:, None] * freqs, col[:, None] * freqs
    cos64 = jnp.concatenate([jnp.cos(ar), jnp.cos(ar), jnp.cos(ac), jnp.cos(ac)], axis=-1)
    sin64 = jnp.concatenate([-jnp.sin(ar), jnp.sin(ar), -jnp.sin(ac), jnp.sin(ac)], axis=-1)
    return jnp.tile(cos64, (1, 2)), jnp.tile(sin64, (1, 2))


SGU_TM = 512


def _group_mean(x, ones64):
    hi, lo = _split_bf16(x)
    return (_dot(hi, ones64) + _dot(lo, ones64)) * (1.0 / DG_C)


def _sgu_kernel(uv_ref, ws_ref, bs_ref, o_ref):
    r = lax.broadcasted_iota(jnp.int32, (W_C, W_C), 0)
    s = lax.broadcasted_iota(jnp.int32, (W_C, W_C), 1)
    ones64 = jnp.where((r // DG_C) == (s // DG_C), 1.0, 0.0).astype(BF16)
    lane = lax.broadcasted_iota(jnp.int32, (SGU_CHUNK, W_C), 1)
    for n in range(SGU_TM // SGU_CHUNK):
        rows = slice(n * SGU_CHUNK, (n + 1) * SGU_CHUNK)
        u = _gelu_tanh(uv_ref[rows, 0:256])
        v = _gelu_tanh(uv_ref[rows, 256:512])
        d = v - _group_mean(v, ones64)
        vn = d * lax.rsqrt(_group_mean(d * d, ones64) + EPS)
        s_mix = bs_ref[...]
        for g in range(N_GROUPS_C):
            vn_g = jnp.where((lane // DG_C) == g, vn, 0.0).astype(BF16)
            s_mix = s_mix + _dot(ws_ref[g], vn_g)
        o_ref[rows, :] = (u * s_mix).astype(BF16)


def _sgu_call(uv, ws, bs_full):
    n = T_ALL // SGU_TM
    return pl.pallas_call(
        _sgu_kernel,
        grid=(n,),
        in_specs=[pl.BlockSpec((SGU_TM, 512), lambda i: (i, 0)),
                  pl.BlockSpec(ws.shape, lambda i: (0, 0, 0)),
                  pl.BlockSpec(bs_full.shape, lambda i: (0, 0))],
        out_specs=pl.BlockSpec((SGU_TM, W_C), lambda i: (i, 0)),
        out_shape=jax.ShapeDtypeStruct((T_ALL, W_C), BF16),
        compiler_params=_cparams(("arbitrary",)),
        name="sgu",
    )(uv, ws, bs_full)


OUT_TM = 512


SEL_LANE0 = N_EXPERTS


def _top2_gates(logits):
    lane = lax.broadcasted_iota(jnp.int32, logits.shape, 1).astype(F32)
    neg = -jnp.inf
    lg = jnp.where(lane < N_EXPERTS, logits, neg)
    m1 = jnp.max(lg, axis=-1, keepdims=True)
    i1 = jnp.min(jnp.where(lg == m1, lane, float(LANES)), axis=-1, keepdims=True)
    lg2 = jnp.where(lane == i1, neg, lg)
    m2 = jnp.max(lg2, axis=-1, keepdims=True)
    i2 = jnp.min(jnp.where(lg2 == m2, lane, float(LANES)), axis=-1, keepdims=True)
    e2 = jnp.exp(m2 - m1)
    den = 1.0 + e2
    gates = jnp.where(lane == i1, 1.0 / den, 0.0) + jnp.where(lane == i2, e2 / den, 0.0)
    sel = jnp.where((lane == i1 + SEL_LANE0) | (lane == i2 + SEL_LANE0), 1.0, 0.0)
    return gates + sel


def _store_token_tiles(ref, val):
    n = val.shape[0]
    for k in range(D_MODEL // LANES):
        ref[pl.ds(k, n, stride=8), :] = val[:, k * LANES:(k + 1) * LANES]


def _load_token_tiles(ref, n):
    return jnp.concatenate([ref[pl.ds(k, n, stride=8), :] for k in range(D_MODEL // LANES)], axis=-1)


def _out_proj_kernel(oa_ref, ob_ref, oc_ref, x_ref, wo_ref, g1_ref, nrm_ref, sc_ref, sh_ref, *rest, moe):
    if moe:
        rw_ref, x1_ref, h2t_ref, gates_ref, gates_t_ref = rest
    else:
        x1_ref, h2_ref = rest
    y = (_dot(oa_ref[...], wo_ref[0:256, :]) + _dot(ob_ref[...], wo_ref[256:768, :])
         + _dot(oc_ref[...], wo_ref[768:1024, :]))
    x1 = x_ref[...] + g1_ref[...] * y
    x1_ref[...] = x1
    yn = x1 * lax.rsqrt(jnp.mean(x1 * x1, axis=-1, keepdims=True) + EPS)
    h = (yn * nrm_ref[...]) * (1.0 + sc_ref[...]) + sh_ref[...]
    if moe:
        _store_token_tiles(h2t_ref, h)
        gates = _top2_gates(_dot3(h, rw_ref[...]))
        gates_ref[...] = gates
        gates_t_ref[...] = gates.T
    else:
        h2_ref[...] = h.astype(BF16)


def _out_proj_call(oa, ob, oc, x, w_out, mod4, norm_g, router_w=None):
    tm = OUT_TM
    n = T_ALL // tm
    moe = router_w is not None
    row = functools.partial(_mod_row, tm=tm)
    mod_spec = lambda k: pl.BlockSpec((None, None, 1, D_MODEL), lambda i: (row(i), k, 0, 0))
    tok = lambda w: pl.BlockSpec((tm, w), lambda i: (i, 0))
    full = lambda a: pl.BlockSpec(a.shape, lambda i: (0,) * a.ndim)
    in_specs = [tok(W_A), tok(W_B), tok(W_C), tok(D_MODEL), full(w_out), mod_spec(2), full(norm_g),
                mod_spec(4), mod_spec(3)]
    args = [oa, ob, oc, x, w_out, mod4, norm_g, mod4, mod4]
    if moe:
        in_specs.append(full(router_w))
        args.append(router_w)
        out_specs = [tok(D_MODEL), pl.BlockSpec((tm * 8, LANES), lambda i: (i, 0)), tok(LANES),
                     pl.BlockSpec((LANES, tm), lambda i: (0, i))]
        out_shape = [jax.ShapeDtypeStruct((T_ALL, D_MODEL), F32), jax.ShapeDtypeStruct((T_ALL * 8, LANES), F32),
                     jax.ShapeDtypeStruct((T_ALL, LANES), F32), jax.ShapeDtypeStruct((LANES, T_ALL), F32)]
    else:
        out_specs = [tok(D_MODEL), tok(D_MODEL)]
        out_shape = [jax.ShapeDtypeStruct((T_ALL, D_MODEL), F32), jax.ShapeDtypeStruct((T_ALL, D_MODEL), BF16)]
    return pl.pallas_call(
        functools.partial(_out_proj_kernel, moe=moe),
        grid=(n,),
        in_specs=in_specs,
        out_specs=out_specs,
        out_shape=out_shape,
        compiler_params=_cparams(("arbitrary",)),
        name="out_proj_moe" if moe else "out_proj",
    )(*args)


FFN_TM = 512
FFN_FC = 1408
FFN_NC = D_FF // FFN_FC


def _swiglu_chunk(h, wg_ref, wu_ref, wd_ref):
    act = _silu(_dot(h, wg_ref[...])) * _dot(h, wu_ref[...])
    return _dot(act.astype(BF16), wd_ref[...])


def _finish(out, nf_ref):
    if nf_ref is None:
        return out
    return (out * lax.rsqrt(jnp.mean(out * out, axis=-1, keepdims=True) + EPS)) * nf_ref[...]


def _ffn_kernel(h_ref, x_ref, g2_ref, wg_ref, wu_ref, wd_ref, *rest, final):
    rest = list(rest)
    nf_ref = rest.pop(0) if final else None
    o_ref, acc_ref = rest
    j = pl.program_id(1)

    @pl.when(j == 0)
    def _():
        acc_ref[...] = jnp.zeros_like(acc_ref)

    acc_ref[...] += _swiglu_chunk(h_ref[...], wg_ref, wu_ref, wd_ref)

    @pl.when(j == FFN_NC - 1)
    def _():
        o_ref[...] = _finish(x_ref[...] + g2_ref[...] * acc_ref[...], nf_ref)


def _ffn_call(h2, x1, mod4, wg, wu, wd, norm_f=None):
    tm, fc = FFN_TM, FFN_FC
    final = norm_f is not None
    row = functools.partial(_mod_row, tm=tm)
    tok = lambda w: pl.BlockSpec((tm, w), lambda i, j: (i, 0))
    in_specs = [tok(D_MODEL), tok(D_MODEL),
                pl.BlockSpec((None, None, 1, D_MODEL), lambda i, j: (row(i), 5, 0, 0)),
                pl.BlockSpec((D_MODEL, fc), lambda i, j: (0, j)),
                pl.BlockSpec((D_MODEL, fc), lambda i, j: (0, j)),
                pl.BlockSpec((fc, D_MODEL), lambda i, j: (j, 0))]
    args = [h2, x1, mod4, wg, wu, wd]
    if final:
        in_specs.append(pl.BlockSpec((1, D_MODEL), lambda i, j: (0, 0)))
        args.append(norm_f)
    return pl.pallas_call(
        functools.partial(_ffn_kernel, final=final),
        grid=(T_ALL // tm, FFN_NC),
        in_specs=in_specs,
        out_specs=tok(D_MODEL),
        out_shape=jax.ShapeDtypeStruct((T_ALL, D_MODEL), F32),
        scratch_shapes=[pltpu.VMEM((tm, D_MODEL), F32)],
        compiler_params=_cparams(("arbitrary", "arbitrary")),
        name="ffn_dense",
    )(*args)


MOE_TM = 512
MOE_NT_MAX = (2 * T_ALL) // MOE_TM + N_EXPERTS
MOE_ROWS = MOE_NT_MAX * MOE_TM
PLAN_BLK = 512
MISC_LAST_START = 8
MISC_NT = 16


def _moe_plan_kernel(gt_ref, posa_ref, posb_ref, te_ref, ti_ref, misc_ref):
    tm = float(MOE_TM)
    sel = gt_ref[SEL_LANE0:SEL_LANE0 + N_EXPERTS, :]
    cnt = jnp.sum(sel, axis=1, keepdims=True)
    nt = jnp.floor((cnt + (tm - 1.0)) * (1.0 / tm))
    sub = lax.broadcasted_iota(jnp.int32, (N_EXPERTS, LANES), 0).astype(F32)
    lane = lax.broadcasted_iota(jnp.int32, (N_EXPERTS, LANES), 1).astype(F32)
    nt_b = jnp.broadcast_to(nt, (N_EXPERTS, LANES))
    nt_row = jnp.sum(jnp.where(sub == lane, nt_b, 0.0), axis=0, keepdims=True)
    toff = jnp.sum(jnp.where(lane < sub, jnp.broadcast_to(nt_row, (N_EXPERTS, LANES)), 0.0),
                   axis=1, keepdims=True)
    tend = toff + nt
    n_total = jnp.sum(nt, axis=0, keepdims=True)
    jc = jnp.minimum(lane, n_total - 1.0)
    te = jnp.sum(jnp.where(jc >= tend, 1.0, 0.0), axis=0, keepdims=True)
    te_ref[...] = te.astype(jnp.int32)
    ti_ref[...] = jc[0:1, :].astype(jnp.int32)
    last_start = (tend - 1.0) * tm
    ls_row = jnp.sum(jnp.where(sub + MISC_LAST_START == lane, jnp.broadcast_to(last_start, (N_EXPERTS, LANES)), 0.0),
                     axis=0, keepdims=True)
    nt_row2 = jnp.sum(jnp.where(sub + MISC_NT == lane, nt_b, 0.0), axis=0, keepdims=True)
    misc = jnp.where(lane[0:1, :] == 0.0, n_total, 0.0) + ls_row + nt_row2
    misc_ref[...] = misc.astype(jnp.int32)

    off = toff * tm
    r = lax.broadcasted_iota(jnp.int32, (PLAN_BLK, PLAN_BLK), 0)
    c = lax.broadcasted_iota(jnp.int32, (PLAN_BLK, PLAN_BLK), 1)
    upper = jnp.where(r <= c, 1.0, 0.0).astype(BF16)
    carry = jnp.zeros((N_EXPERTS, 1), F32)
    for blk in range(T_ALL // PLAN_BLK):
        cols = slice(blk * PLAN_BLK, (blk + 1) * PLAN_BLK)
        s = gt_ref[SEL_LANE0:SEL_LANE0 + N_EXPERTS, cols]
        rank = _dot(s.astype(BF16), upper) + carry
        pos = off + rank - 1.0
        posa_ref[:, cols] = jnp.min(jnp.where(s > 0.0, pos, 1e9), axis=0, keepdims=True).astype(jnp.int32)
        posb_ref[:, cols] = jnp.max(jnp.where(s > 0.0, pos, -1.0), axis=0, keepdims=True).astype(jnp.int32)
        carry = carry + jnp.sum(s, axis=1, keepdims=True)


def _moe_plan_call(gates_t):
    row = lambda w: jax.ShapeDtypeStruct((1, w), jnp.int32)
    full = lambda w: pl.BlockSpec((1, w), lambda: (0, 0))
    return pl.pallas_call(
        _moe_plan_kernel,
        in_specs=[pl.BlockSpec((LANES, T_ALL), lambda: (0, 0))],
        out_specs=[full(T_ALL), full(T_ALL), full(LANES), full(LANES), full(LANES)],
        out_shape=[row(T_ALL), row(T_ALL), row(LANES), row(LANES), row(LANES)],
        compiler_params=pltpu.CompilerParams(vmem_limit_bytes=VMEM_LIMIT),
        name="moe_plan",
    )(gates_t)


def _row_tile(ref, row):
    return ref.at[pl.ds(pl.multiple_of(row * 8, 8), 8), :]


def _moe_scatter_kernel(misc_ref, posa_ref, posb_ref, h_ref, xs_ref, zero_ref, sem):
    tm = h_ref.shape[0] // 8

    @pl.when(pl.program_id(0) == 0)
    def _():
        zero_ref[...] = jnp.zeros_like(zero_ref)

        def zero_tile(first_row):
            start = pl.multiple_of(first_row * 8, 8)
            cp = pltpu.make_async_copy(zero_ref, xs_ref.at[pl.ds(start, MOE_TM * 8), :], sem.at[0])
            cp.start()
            cp.wait()

        for e in range(N_EXPERTS):
            @pl.when(misc_ref[0, MISC_NT + e] > 0)
            def _():
                zero_tile(misc_ref[0, MISC_LAST_START + e])

        def zero_tail(j, carry):
            zero_tile(j * MOE_TM)
            return carry

        lax.fori_loop(misc_ref[0, 0], MOE_NT_MAX, zero_tail, 0)

    def issue(r, carry):
        src = _row_tile(h_ref, r)
        pltpu.make_async_copy(src, _row_tile(xs_ref, posa_ref[0, r]), sem.at[0]).start()
        pltpu.make_async_copy(src, _row_tile(xs_ref, posb_ref[0, r]), sem.at[0]).start()
        return carry

    lax.fori_loop(0, tm, issue, 0)

    def drain(r, carry):
        src = _row_tile(h_ref, r)
        pltpu.make_async_copy(src, _row_tile(xs_ref, posa_ref[0, r]), sem.at[0]).wait()
        pltpu.make_async_copy(src, _row_tile(xs_ref, posb_ref[0, r]), sem.at[0]).wait()
        return carry

    lax.fori_loop(0, tm, drain, 0)


SCATTER_TM = 512


def _moe_scatter_call(misc, posa3, posb3, h2t):
    tm = SCATTER_TM
    smem_row = pl.BlockSpec((None, 1, tm), lambda i: (i, 0, 0), memory_space=pltpu.SMEM)
    return pl.pallas_call(
        _moe_scatter_kernel,
        grid=(T_ALL // tm,),
        in_specs=[pl.BlockSpec((1, LANES), lambda i: (0, 0), memory_space=pltpu.SMEM), smem_row, smem_row,
                  pl.BlockSpec((tm * 8, LANES), lambda i: (i, 0))],
        out_specs=pl.BlockSpec(memory_space=pl.ANY),
        out_shape=jax.ShapeDtypeStruct((MOE_ROWS * 8, LANES), F32),
        scratch_shapes=[pltpu.VMEM((MOE_TM * 8, LANES), F32), pltpu.SemaphoreType.DMA((1,))],
        compiler_params=_cparams(("arbitrary",)),
        name="moe_scatter",
    )(misc, posa3, posb3, h2t)


def _ffn_grouped_kernel(te_ref, ti_ref, misc_ref, x_ref, wg_ref, wu_ref, wd_ref, o_ref, xb_ref, acc_ref):
    j = pl.program_id(0)
    c = pl.program_id(1)

    @pl.when(j < misc_ref[0])
    def _():
        @pl.when(c == 0)
        def _():
            xb_ref[...] = _load_token_tiles(x_ref, MOE_TM).astype(BF16)
            acc_ref[...] = jnp.zeros_like(acc_ref)

        acc_ref[...] += _swiglu_chunk(xb_ref[...], wg_ref, wu_ref, wd_ref)

        @pl.when(c == FFN_NC - 1)
        def _():
            _store_token_tiles(o_ref, acc_ref[...])

    @pl.when((j >= misc_ref[0]) & (c == FFN_NC - 1))
    def _():
        o_ref[...] = jnp.zeros_like(o_ref)


def _ffn_grouped_call(te, ti, misc, xs, wg, wu, wd):
    fc = FFN_FC

    def chunk(j, c, misc):
        return jnp.where(j < misc[0], c, FFN_NC - 1)

    grid_spec = pltpu.PrefetchScalarGridSpec(
        num_scalar_prefetch=3,
        grid=(MOE_NT_MAX, FFN_NC),
        in_specs=[pl.BlockSpec((MOE_TM * 8, LANES), lambda j, c, te, ti, misc: (ti[j], 0)),
                  pl.BlockSpec((None, D_MODEL, fc), lambda j, c, te, ti, misc: (te[j], 0, chunk(j, c, misc))),
                  pl.BlockSpec((None, D_MODEL, fc), lambda j, c, te, ti, misc: (te[j], 0, chunk(j, c, misc))),
                  pl.BlockSpec((None, fc, D_MODEL), lambda j, c, te, ti, misc: (te[j], chunk(j, c, misc), 0))],
        out_specs=pl.BlockSpec((MOE_TM * 8, LANES), lambda j, c, te, ti, misc: (j, 0)),
        scratch_shapes=[pltpu.VMEM((MOE_TM, D_MODEL), BF16), pltpu.VMEM((MOE_TM, D_MODEL), F32)],
    )
    return pl.pallas_call(
        _ffn_grouped_kernel,
        grid_spec=grid_spec,
        out_shape=jax.ShapeDtypeStruct((MOE_ROWS * 8, LANES), F32),
        compiler_params=_cparams(("arbitrary", "arbitrary")),
        name="ffn_grouped",
    )(te, ti, misc, xs, wg, wu, wd)


COMBINE_TM = 256


def _moe_combine_kernel(posa_ref, posb_ref, ys_ref, x_ref, g2_ref, gates_ref, *rest, final):
    rest = list(rest)
    nf_ref = rest.pop(0) if final else None
    o_ref, bufa_ref, bufb_ref, sem = rest
    tm = COMBINE_TM

    def issue(r, carry):
        pltpu.make_async_copy(_row_tile(ys_ref, posa_ref[0, r]), _row_tile(bufa_ref, r), sem.at[0]).start()
        pltpu.make_async_copy(_row_tile(ys_ref, posb_ref[0, r]), _row_tile(bufb_ref, r), sem.at[0]).start()
        return carry

    lax.fori_loop(0, tm, issue, 0)

    gates = gates_ref[...]
    lane = lax.broadcasted_iota(jnp.int32, gates.shape, 1).astype(F32)
    is_sel = (lane >= SEL_LANE0) & (lane < SEL_LANE0 + N_EXPERTS) & (gates > 0.0)
    ia = jnp.min(jnp.where(is_sel, lane, float(LANES)), axis=-1, keepdims=True) - SEL_LANE0
    ib = jnp.max(jnp.where(is_sel, lane, -1.0), axis=-1, keepdims=True) - SEL_LANE0
    wa = jnp.sum(jnp.where(lane == ia, gates, 0.0), axis=-1, keepdims=True)
    wb = jnp.sum(jnp.where(lane == ib, gates, 0.0), axis=-1, keepdims=True)

    def drain(r, carry):
        pltpu.make_async_copy(_row_tile(ys_ref, posa_ref[0, r]), _row_tile(bufa_ref, r), sem.at[0]).wait()
        pltpu.make_async_copy(_row_tile(ys_ref, posb_ref[0, r]), _row_tile(bufb_ref, r), sem.at[0]).wait()
        return carry

    lax.fori_loop(0, tm, drain, 0)

    y = wa * _load_token_tiles(bufa_ref, tm) + wb * _load_token_tiles(bufb_ref, tm)
    o_ref[...] = _finish(x_ref[...] + g2_ref[...] * y, nf_ref)


def _moe_combine_call(posa3, posb3, ys, x1, mod4, gates, norm_f=None):
    tm = COMBINE_TM
    final = norm_f is not None
    row = functools.partial(_mod_row, tm=tm)
    smem_row = pl.BlockSpec((None, 1, tm), lambda i: (i, 0, 0), memory_space=pltpu.SMEM)
    tok = lambda w: pl.BlockSpec((tm, w), lambda i: (i, 0))
    in_specs = [smem_row, smem_row, pl.BlockSpec(memory_space=pl.ANY), tok(D_MODEL),
                pl.BlockSpec((None, None, 1, D_MODEL), lambda i: (row(i), 5, 0, 0)), tok(LANES)]
    args = [posa3, posb3, ys, x1, mod4, gates]
    if final:
        in_specs.append(pl.BlockSpec((1, D_MODEL), lambda i: (0, 0)))
        args.append(norm_f)
    return pl.pallas_call(
        functools.partial(_moe_combine_kernel, final=final),
        grid=(T_ALL // tm,),
        in_specs=in_specs,
        out_specs=tok(D_MODEL),
        out_shape=jax.ShapeDtypeStruct((T_ALL, D_MODEL), F32),
        scratch_shapes=[pltpu.VMEM((tm * 8, LANES), F32), pltpu.VMEM((tm * 8, LANES), F32),
                        pltpu.SemaphoreType.DMA((1,))],
        compiler_params=_cparams(("arbitrary",)),
        name="moe_combine",
    )(*args)


def _moe_call(h2t, x1, mod4, gates, gates_t, wg, wu, wd, norm_f=None):
    posa, posb, te, ti, misc = _moe_plan_call(gates_t)
    xs = _moe_scatter_call(misc, posa.reshape(T_ALL // SCATTER_TM, 1, SCATTER_TM),
                           posb.reshape(T_ALL // SCATTER_TM, 1, SCATTER_TM), h2t)
    ys = _ffn_grouped_call(te.reshape(LANES), ti.reshape(LANES), misc.reshape(LANES), xs, wg, wu, wd)
    return _moe_combine_call(posa.reshape(T_ALL // COMBINE_TM, 1, COMBINE_TM),
                             posb.reshape(T_ALL // COMBINE_TM, 1, COMBINE_TM), ys, x1, mod4, gates, norm_f)


def kernel(x_prompt, x_sample, cache_k, cache_v, state_gla_fwd, state_gla_bwd, c, c_ctx, w_ada, b_ada, norm_mix, norm_ffn, w_in, w_out, gla_w_up, gla_b_up, gla_norm, diff_lambda, diff_norm, sgu_w, sgu_b, ffn_w_gate, ffn_w_up, ffn_w_down, router_w, moe_w_gate, moe_w_up, moe_w_down, norm_f):
    x = jnp.concatenate([x_prompt.reshape(T_PROMPT, D_MODEL), x_sample.reshape(T_SAMPLE, D_MODEL)], axis=0)
    cvecs = jnp.concatenate([c_ctx[None, :], c, jnp.zeros((N_MOD_ROWS - 1 - DEC_BATCH, D_MODEL), F32)], axis=0)
    mod = _ada_call(cvecs, w_ada, b_ada)
    cos, sin_signed = _rope_tables()
    zeros_state = jnp.zeros((BATCH, 64, 256), F32)

    ks, vs, sfs, sbs = [], [], [], []
    for l in range(DEPTH):
        mod4 = mod[l].reshape(N_MOD_ROWS, 6, 1, D_MODEL)
        wl = w_in[l]
        w_main = jnp.concatenate([wl[:, 0:1024], wl[:, 1056:3104]], axis=1).astype(BF16)
        w_z = jnp.pad(wl[:, 1024:1056], ((0, 0), (0, LANES - 2 * GLA_RANK))).astype(BF16)
        w_up = jnp.zeros((LANES, 2 * W_A), F32)
        w_up = w_up.at[0:GLA_RANK, 0:W_A].set(gla_w_up[l, 0]).at[GLA_RANK:2 * GLA_RANK, W_A:].set(gla_w_up[l, 1])
        b_up = gla_b_up[l].reshape(1, 2 * W_A)
        g4, la, qb, kb, vb, uv = _in_proj_call(x, norm_mix[l][None, :], mod4, w_main, w_z,
                                               w_up.astype(BF16), b_up)

        gain_a = gla_norm[l][None, :]
        oa_p, sf, sb = _gla_call(g4, la, zeros_state, zeros_state, gain_a,
                                 batch=BATCH, seq=SEQ, row_block0=0)
        oa_s, _, _ = _gla_call(g4, la, _state_to_kernel(state_gla_fwd[:, l]),
                               _state_to_kernel(state_gla_bwd[:, l]), gain_a,
                               batch=DEC_BATCH, seq=DEC_SEQ, row_block0=T_PROMPT // DEC_SEQ)
        oa = jnp.concatenate([oa_p, oa_s], axis=0)

        lam_init = 0.8 - 0.6 * math.exp(-0.3 * l)
        gain_b = diff_norm[l][None, :]
        ob_p = _attn_prompt_call(diff_lambda[l], qb, kb, vb, gain_b, lam_init=lam_init)
        ob_s = _attn_sample_call(diff_lambda[l], qb, kb, vb,
                                 cache_k[:, l].reshape(DEC_BATCH, PAST_LEN, W_B),
                                 cache_v[:, l].reshape(DEC_BATCH, PAST_LEN, W_B),
                                 cos, sin_signed, gain_b, lam_init=lam_init)
        ob = jnp.concatenate([ob_p, ob_s], axis=0)

        bs_full = jnp.repeat(sgu_b[l].T, DG_C, axis=1)
        oc = _sgu_call(uv, sgu_w[l].astype(BF16), bs_full)

        last = l == DEPTH - 1
        nf = norm_f[None, :] if last else None
        if l % 2 == 0:
            x1, h2 = _out_proj_call(oa, ob, oc, x, w_out[l].astype(BF16), mod4, norm_ffn[l][None, :])
            e = l // 2
            x = _ffn_call(h2, x1, mod4, ffn_w_gate[e].astype(BF16), ffn_w_up[e].astype(BF16),
                          ffn_w_down[e].astype(BF16), norm_f=nf)
        else:
            e = l // 2
            rw = jnp.pad(router_w[e], ((0, 0), (0, LANES - N_EXPERTS)))
            x1, h2t, gates, gates_t = _out_proj_call(oa, ob, oc, x, w_out[l].astype(BF16), mod4,
                                                     norm_ffn[l][None, :], rw)
            x = _moe_call(h2t, x1, mod4, gates, gates_t, moe_w_gate[e].astype(BF16), moe_w_up[e].astype(BF16),
                          moe_w_down[e].astype(BF16), norm_f=nf)

        ks.append(kb[:T_PROMPT].reshape(BATCH, SEQ, N_HEADS_B, 2 * DH_B))
        vs.append(vb[:T_PROMPT].reshape(BATCH, SEQ, N_HEADS_B, DV_B))
        sfs.append(_state_from_kernel(sf))
        sbs.append(_state_from_kernel(sb))

    y_prompt = x[:T_PROMPT].reshape(BATCH, SEQ, D_MODEL)
    y_sample = x[T_PROMPT:].reshape(DEC_BATCH, DEC_SEQ, D_MODEL)
    return (y_prompt, y_sample, jnp.stack(ks, axis=1), jnp.stack(vs, axis=1),
            jnp.stack(sfs, axis=1), jnp.stack(sbs, axis=1))
```

```python
import functools
import math

import jax
import jax.numpy as jnp
import numpy as np
from jax import lax
from jax.experimental import pallas as pl
from jax.experimental.pallas import tpu as pltpu

F32 = jnp.float32
BF16 = jnp.bfloat16

D_MODEL = 1024
BATCH = 32
SEQ = 256
DEPTH = 2
DEC_BATCH = 2
DEC_SEQ = 1024
PAST_LEN = 256
GRID_W = 64
N_HEADS_A = 4
DK_A = 64
W_A = 256
GLA_RANK = 16
GLA_TAU = 16.0
GLA_CHUNK = 64
N_HEADS_B = 4
DH_B = 64
DV_B = 128
W_B = 512
ROPE_THETA = 10000.0
AXIS_PAIRS = DH_B // 4
N_GROUPS_C = 4
DG_C = 64
W_C = 256
SGU_CHUNK = 128
D_FF = 2816
N_EXPERTS = 8
EPS = 1e-6

T_PROMPT = BATCH * SEQ
T_SAMPLE = DEC_BATCH * DEC_SEQ
T_ALL = T_PROMPT + T_SAMPLE
N_MOD_ROWS = 8
LANES = 128
VMEM_LIMIT = 56 * 1024 * 1024


def _cparams(sem):
    return pltpu.CompilerParams(dimension_semantics=sem, vmem_limit_bytes=VMEM_LIMIT)


def _dot(a, b):
    return jnp.dot(a, b, preferred_element_type=F32)


def _dot_nt(a, b):
    return lax.dot_general(a, b, (((1,), (1,)), ((), ())), preferred_element_type=F32)


def _dot_tn(a, b):
    return lax.dot_general(a, b, (((0,), (0,)), ((), ())), preferred_element_type=F32)


def _split_bf16(x):
    hi = x.astype(BF16)
    lo = (x - hi.astype(F32)).astype(BF16)
    return hi, lo


def _dot3(a, w):
    a_hi, a_lo = _split_bf16(a)
    w_hi, w_lo = _split_bf16(w)
    return _dot(a_hi, w_hi) + (_dot(a_lo, w_hi) + _dot(a_hi, w_lo))


def _sigmoid(x):
    return 1.0 / (1.0 + jnp.exp(-x))


def _silu(x):
    return x * _sigmoid(x)


def _gelu_tanh(x):
    c = math.sqrt(2.0 / math.pi)
    return x * (0.5 * (1.0 + jnp.tanh(c * (x + 0.044715 * (x * x * x)))))


def _log_sigmoid(x):
    return jnp.minimum(x, 0.0) - jnp.log(1.0 + jnp.exp(-jnp.abs(x)))


def _mod_row(i, tm):
    n_p = T_PROMPT // tm
    per_b = DEC_SEQ // tm
    return jnp.where(i < n_p, 0, 1 + (i - n_p) // per_b)


ADA_TN = 1536


def _ada_kernel(c_ref, w_ref, b_ref, o_ref):
    a = _silu(c_ref[...])
    o_ref[...] = _dot3(a, w_ref[...]) + b_ref[...]


def _ada_call(cvecs, w_ada, b_ada):
    n_col = (6 * D_MODEL) // ADA_TN
    return pl.pallas_call(
        _ada_kernel,
        grid=(DEPTH, n_col),
        in_specs=[
            pl.BlockSpec((N_MOD_ROWS, D_MODEL), lambda l, j: (0, 0)),
            pl.BlockSpec((None, D_MODEL, ADA_TN), lambda l, j: (l, 0, j)),
            pl.BlockSpec((None, 1, ADA_TN), lambda l, j: (l, 0, j)),
        ],
        out_specs=pl.BlockSpec((None, N_MOD_ROWS, ADA_TN), lambda l, j: (l, 0, j)),
        out_shape=jax.ShapeDtypeStruct((DEPTH, N_MOD_ROWS, 6 * D_MODEL), F32),
        compiler_params=_cparams(("arbitrary", "arbitrary")),
        name="ada_mod",
    )(cvecs, w_ada, b_ada.reshape(DEPTH, 1, 6 * D_MODEL))


IN_TM = 256
IN_COLS = 3104
Z_COL0 = 1024
Z_COLS = 2 * GLA_RANK
W_MAIN = 3072


def _piece_specs(pieces, tm, width):
    specs, t0 = [], 0
    for arr in pieces:
        nt = arr.shape[0] // tm
        specs.append(pl.BlockSpec((tm, width), lambda i, t0=t0, nt=nt: (jnp.clip(i - t0, 0, nt - 1), 0)))
        t0 += nt
    assert t0 * tm == T_ALL and len(pieces) in (1, 2) and (len(pieces) == 1 or pieces[0].shape[0] == T_PROMPT)
    return specs


def _pick_piece(refs, tm):
    if len(refs) == 1:
        return refs[0][...]
    return jnp.where(pl.program_id(0) < T_PROMPT // tm, refs[0][...], refs[1][...])


def _in_proj_kernel(*refs, n_x):
    x_refs = refs[:n_x]
    (nrm_ref, sh_ref, sc_ref, w_ref, wup_ref, bup_ref,
     g4_ref, la_ref, qb_ref, kb_ref, vb_ref, uv_ref, wm_ref, wz_ref) = refs[n_x:]

    @pl.when(pl.program_id(0) == 0)
    def _():
        wm_ref[:, 0:Z_COL0] = w_ref[:, 0:Z_COL0].astype(BF16)
        wm_ref[:, Z_COL0:W_MAIN] = w_ref[:, Z_COL0 + Z_COLS:IN_COLS].astype(BF16)
        wz_ref[...] = jnp.zeros_like(wz_ref)
        wz_ref[:, 0:Z_COLS] = w_ref[:, Z_COL0:Z_COL0 + Z_COLS].astype(BF16)

    x = _pick_piece(x_refs, IN_TM)
    y = x * lax.rsqrt(jnp.mean(x * x, axis=-1, keepdims=True) + EPS)
    h = (y * nrm_ref[...]) * (1.0 + sc_ref[...]) + sh_ref[...]
    hb = h.astype(BF16)
    g4_ref[...] = _dot(hb, wm_ref[:, 0:1024])
    qb_ref[...] = _dot(hb, wm_ref[:, 1024:1536])
    kb_ref[...] = _dot(hb, wm_ref[:, 1536:2048])
    vb_ref[...] = _dot(hb, wm_ref[:, 2048:2560])
    uv_ref[...] = _dot(hb, wm_ref[:, 2560:3072])
    z = _dot(hb, wz_ref[...])
    zz = _dot(z.astype(BF16), wup_ref[...]) + bup_ref[...]
    la_ref[...] = _log_sigmoid(zz) * (1.0 / GLA_TAU)


def _resident(a):
    return pl.BlockSpec(a.shape, lambda *_: (0,) * a.ndim, pipeline_mode=pl.Buffered(1))


def _in_proj_call(x_pieces, norm_g, mod4, w_in, w_up, b_up):
    tm = IN_TM
    n = T_ALL // tm
    row = functools.partial(_mod_row, tm=tm)
    mod_spec = lambda k: pl.BlockSpec((None, None, 1, D_MODEL), lambda i: (row(i), k, 0, 0))
    full = lambda a: pl.BlockSpec(a.shape, lambda i: (0,) * a.ndim)
    out = lambda w: pl.BlockSpec((tm, w), lambda i: (i, 0))
    return pl.pallas_call(
        functools.partial(_in_proj_kernel, n_x=len(x_pieces)),
        grid=(n,),
        in_specs=_piece_specs(x_pieces, tm, D_MODEL) + [full(norm_g), mod_spec(0), mod_spec(1), _resident(w_in),
                                                        full(w_up), full(b_up)],
        out_specs=[out(1024), out(512), out(512), out(512), out(512), out(512)],
        out_shape=[jax.ShapeDtypeStruct((T_ALL, w), F32) for w in (1024, 512, 512, 512, 512, 512)],
        scratch_shapes=[pltpu.VMEM((D_MODEL, W_MAIN), BF16), pltpu.VMEM((D_MODEL, LANES), BF16)],
        compiler_params=_cparams(("arbitrary",)),
        name="in_proj",
    )(*x_pieces, norm_g, mod4, mod4, w_in, w_up, b_up)


GLA_SB = 256
GLA_NC = GLA_SB // GLA_CHUNK


def _gla_superblock(q, k, vb, v_heads, la, st_all, tri, mask, same64, head_lanes, forward):
    c = GLA_CHUNK
    mid, last = (c // 2 - 1, c - 1) if forward else (c // 2, 0)
    la_hi, la_lo = _split_bf16(la)
    b = _dot(tri, la_hi) + _dot(tri, la_lo)
    rows_of = lambda r: jnp.concatenate(
        [jnp.broadcast_to(b[i * c + r:i * c + r + 1, :], (c, W_A)) for i in range(GLA_NC)], axis=0)
    m = rows_of(mid)
    bl = rows_of(last)
    qe = (q * jnp.exp(b - m)).astype(BF16)
    ke = k * jnp.exp(m - b)
    qi = (q * jnp.exp(b)).astype(BF16)
    ks = (k * jnp.exp(bl - b)).astype(BF16)
    o = jnp.zeros((GLA_SB, W_A), F32)
    for h in range(N_HEADS_A):
        ke_h = jnp.where(head_lanes[h], ke, 0.0).astype(BF16)
        a = jnp.where(mask, _dot_nt(qe, ke_h), 0.0).astype(BF16)
        o = o + _dot(a, v_heads[h])
    outs = [None] * GLA_NC
    for i in (range(GLA_NC) if forward else reversed(range(GLA_NC))):
        rows = slice(i * c, (i + 1) * c)
        outs[i] = o[rows, :] + _dot_nt(qi[rows, :], st_all.astype(BF16))
        kv = _dot_tn(vb[rows, :], ks[rows, :])
        st_all = st_all * jnp.exp(bl[i * c:i * c + 1, :]) + jnp.where(same64, kv, 0.0)
    return jnp.concatenate(outs, axis=0), st_all


def _gla_kernel(g4_ref, la_ref, s0f_ref, s0b_ref, gain_ref, o_ref, sf_ref, sb_ref, of_ref, ob_ref, *, seq):
    n = GLA_SB
    nsb = seq // n
    r = lax.broadcasted_iota(jnp.int32, (n, n), 0)
    s = lax.broadcasted_iota(jnp.int32, (n, n), 1)
    same64 = (r // GLA_CHUNK) == (s // GLA_CHUNK)
    lower = same64 & (s <= r)
    upper = same64 & (s >= r)
    tri_f = jnp.where(lower, 1.0, 0.0).astype(BF16)
    tri_b = jnp.where(upper, 1.0, 0.0).astype(BF16)
    ones64 = jnp.where(same64, 1.0, 0.0).astype(BF16)
    head_lanes = [(s // DK_A) == h for h in range(N_HEADS_A)]
    scale = DK_A ** -0.5
    expand = lambda st: jnp.where(same64, jnp.concatenate([st] * N_HEADS_A, axis=0), 0.0)
    compact = lambda st_all: functools.reduce(
        lambda a, b: a + b, [st_all[h * 64:(h + 1) * 64, :] for h in range(N_HEADS_A)])

    def step(i, carry):
        stf, stb = carry
        rf = pl.ds(pl.multiple_of(i * n, n), n)
        rb = pl.ds(pl.multiple_of((nsb - 1 - i) * n, n), n)
        for rows, forward in ((rf, True), (rb, False)):
            q = g4_ref[rows, 0:256] * scale
            k = g4_ref[rows, 256:512]
            v = g4_ref[rows, 512:768]
            vb = v.astype(BF16)
            v_heads = [jnp.where(head_lanes[h], v, 0.0).astype(BF16) for h in range(N_HEADS_A)]
            if forward:
                o, stf = _gla_superblock(q, k, vb, v_heads, la_ref[rows, 0:256], stf, tri_f, lower, same64,
                                         head_lanes, True)
                of_ref[rows, :] = o
            else:
                o, stb = _gla_superblock(q, k, vb, v_heads, la_ref[rows, 256:512], stb, tri_b, upper, same64,
                                         head_lanes, False)
                ob_ref[rows, :] = o
        return stf, stb

    stf, stb = lax.fori_loop(0, nsb, step, (expand(s0f_ref[...]), expand(s0b_ref[...])))
    sf_ref[...] = compact(stf)
    sb_ref[...] = compact(stb)

    gain = gain_ref[...]

    def finish(i, carry):
        rows = pl.ds(pl.multiple_of(i * n, n), n)
        o = of_ref[rows, :] + ob_ref[rows, :]
        sq_hi, sq_lo = _split_bf16(o * o)
        ms = (_dot(sq_hi, ones64) + _dot(sq_lo, ones64)) * (1.0 / DK_A)
        y = (o * lax.rsqrt(ms + EPS)) * gain
        o_ref[rows, :] = (y * _silu(g4_ref[rows, 768:1024])).astype(BF16)
        return carry

    lax.fori_loop(0, nsb, finish, 0)


def _gla_call(g4, la, s0f, s0b, gain, *, batch, seq, row_block0):
    tok = lambda w: pl.BlockSpec((seq, w), lambda b: (row_block0 + b, 0))
    st = pl.BlockSpec((None, 64, 256), lambda b: (b, 0, 0))
    return pl.pallas_call(
        functools.partial(_gla_kernel, seq=seq),
        grid=(batch,),
        in_specs=[tok(1024), tok(512), st, st, pl.BlockSpec((1, W_A), lambda b: (0, 0))],
        out_specs=[pl.BlockSpec((seq, W_A), lambda b: (b, 0)), st, st],
        out_shape=[jax.ShapeDtypeStruct((batch * seq, W_A), BF16),
                   jax.ShapeDtypeStruct((batch, 64, 256), F32),
                   jax.ShapeDtypeStruct((batch, 64, 256), F32)],
        scratch_shapes=[pltpu.VMEM((seq, W_A), F32), pltpu.VMEM((seq, W_A), F32)],
        compiler_params=_cparams(("arbitrary",)),
        name=f"gla_{seq}",
    )(g4, la, s0f, s0b, gain)


def _state_to_kernel(s):
    b = s.shape[0]
    return jnp.transpose(s, (0, 3, 1, 2)).reshape(b, 64, 256)


def _state_from_kernel(st):
    b = st.shape[0]
    return jnp.transpose(st.reshape(b, 64, N_HEADS_A, DK_A), (0, 2, 3, 1))


def _lambda(lv, lam_init):
    l01 = jnp.sum(lv[0:1, :] * lv[1:2, :], axis=-1, keepdims=True)
    l23 = jnp.sum(lv[2:3, :] * lv[3:4, :], axis=-1, keepdims=True)
    return jnp.exp(l01) - jnp.exp(l23) + lam_init


def _softmax_parts(parts):
    mx = functools.reduce(jnp.maximum, [jnp.max(p, axis=-1, keepdims=True) for p in parts])
    es = [jnp.exp(p - mx) for p in parts]
    den = functools.reduce(lambda a, b: a + b, [jnp.sum(e, axis=-1, keepdims=True) for e in es])
    return [e / den for e in es]


def _diff_finish(o, gain, lam_init):
    o = o * lax.rsqrt(jnp.mean(o * o, axis=-1, keepdims=True) + EPS)
    return ((o * gain) * (1.0 - lam_init)).astype(BF16)


QK_SCALE = DH_B ** -0.5


def _key_halves(k):
    first = lax.broadcasted_iota(jnp.int32, k.shape, 1) < DH_B
    return jnp.where(first, k, 0.0).astype(BF16), jnp.where(first, 0.0, k).astype(BF16)


def _attn_prompt_kernel(lv_ref, q_ref, k_ref, v_ref, gain_ref, o_ref, *, lam_init):
    lam = _lambda(lv_ref[...], lam_init)
    for h in range(N_HEADS_B):
        cols = slice(h * DV_B, (h + 1) * DV_B)
        q = (q_ref[:, cols] * QK_SCALE).astype(BF16)
        k1, k2 = _key_halves(k_ref[:, cols])
        (p1,) = _softmax_parts([_dot_nt(q, k1)])
        (p2,) = _softmax_parts([_dot_nt(q, k2)])
        a = p1 - lam * p2
        o = _dot(a.astype(BF16), v_ref[:, cols].astype(BF16))
        o_ref[:, cols] = _diff_finish(o, gain_ref[:, cols], lam_init)


def _attn_prompt_call(lv, qb, kb, vb, gain, *, lam_init):
    blk = pl.BlockSpec((SEQ, W_B), lambda b: (b, 0))
    return pl.pallas_call(
        functools.partial(_attn_prompt_kernel, lam_init=lam_init),
        grid=(BATCH,),
        in_specs=[pl.BlockSpec((4, DH_B), lambda b: (0, 0)), blk, blk, blk,
                  pl.BlockSpec((1, W_B), lambda b: (0, 0))],
        out_specs=blk,
        out_shape=jax.ShapeDtypeStruct((T_PROMPT, W_B), BF16),
        compiler_params=_cparams(("arbitrary",)),
        name="diff_attn_prompt",
    )(lv, qb, kb, vb, gain)


def _rope(x, cos, sin_signed):
    lane = lax.broadcasted_iota(jnp.int32, x.shape, 1)
    first = (lane % (2 * AXIS_PAIRS)) < AXIS_PAIRS
    partner = jnp.where(first, pltpu.roll(x, LANES - AXIS_PAIRS, 1), pltpu.roll(x, AXIS_PAIRS, 1))
    return x * cos + partner * sin_signed


ATT_TQ = 256


def _attn_sample_kernel(lv_ref, q_ref, k_ref, v_ref, kc_ref, vc_ref, cosq_ref, sinq_ref,
                        cosk_ref, sink_ref, gain_ref, o_ref, k1_ref, k2_ref, *, lam_init):
    @pl.when(pl.program_id(1) == 0)
    def _():
        for h in range(N_HEADS_B):
            cols = slice(h * DV_B, (h + 1) * DV_B)
            k1_ref[:, cols], k2_ref[:, cols] = _key_halves(_rope(k_ref[:, cols], cosk_ref[...], sink_ref[...]))

    lam = _lambda(lv_ref[...], lam_init)
    for h in range(N_HEADS_B):
        cols = slice(h * DV_B, (h + 1) * DV_B)
        q = (_rope(q_ref[:, cols], cosq_ref[...], sinq_ref[...]) * QK_SCALE).astype(BF16)
        c1, c2 = _key_halves(kc_ref[:, cols])
        p1 = _softmax_parts([_dot_nt(q, k1_ref[:, cols]), _dot_nt(q, c1)])
        p2 = _softmax_parts([_dot_nt(q, k2_ref[:, cols]), _dot_nt(q, c2)])
        a_own = p1[0] - lam * p2[0]
        a_ctx = p1[1] - lam * p2[1]
        o = (_dot(a_own.astype(BF16), v_ref[:, cols].astype(BF16))
             + _dot(a_ctx.astype(BF16), vc_ref[:, cols].astype(BF16)))
        o_ref[:, cols] = _diff_finish(o, gain_ref[:, cols], lam_init)


def _attn_sample_call(lv, qb, kb, vb, kc, vc, cos, sin_signed, gain, *, lam_init):
    tq = ATT_TQ
    nq = DEC_SEQ // tq
    p0 = T_PROMPT // tq
    s0 = T_PROMPT // DEC_SEQ
    qblk = pl.BlockSpec((tq, W_B), lambda b, t: (p0 + b * nq + t, 0))
    kvblk = pl.BlockSpec((DEC_SEQ, W_B), lambda b, t: (s0 + b, 0))
    cblk = pl.BlockSpec((None, PAST_LEN, W_B), lambda b, t: (b, 0, 0))
    return pl.pallas_call(
        functools.partial(_attn_sample_kernel, lam_init=lam_init),
        grid=(DEC_BATCH, nq),
        in_specs=[pl.BlockSpec((4, DH_B), lambda b, t: (0, 0)), qblk, kvblk, kvblk, cblk, cblk,
                  pl.BlockSpec((tq, DV_B), lambda b, t: (t, 0)),
                  pl.BlockSpec((tq, DV_B), lambda b, t: (t, 0)),
                  pl.BlockSpec((DEC_SEQ, DV_B), lambda b, t: (0, 0)),
                  pl.BlockSpec((DEC_SEQ, DV_B), lambda b, t: (0, 0)),
                  pl.BlockSpec((1, W_B), lambda b, t: (0, 0))],
        out_specs=pl.BlockSpec((tq, W_B), lambda b, t: (b * nq + t, 0)),
        out_shape=jax.ShapeDtypeStruct((T_SAMPLE, W_B), BF16),
        scratch_shapes=[pltpu.VMEM((DEC_SEQ, W_B), BF16), pltpu.VMEM((DEC_SEQ, W_B), BF16)],
        compiler_params=_cparams(("arbitrary", "arbitrary")),
        name="diff_attn_sample",
    )(lv, qb, kb, vb, kc, vc, cos, sin_signed, cos, sin_signed, gain)


def _rope_tables():
    rows = DEC_SEQ // GRID_W
    row = jnp.repeat(jnp.arange(rows, dtype=F32), GRID_W)
    col = jnp.tile(jnp.arange(GRID_W, dtype=F32), rows)
    freqs = ROPE_THETA ** (-jnp.arange(AXIS_PAIRS, dtype=F32) / AXIS_PAIRS)
    ar, ac = row[:, None] * freqs, col[:, None] * freqs
    cos64 = jnp.concatenate([jnp.cos(ar), jnp.cos(ar), jnp.cos(ac), jnp.cos(ac)], axis=-1)
    sin64 = jnp.concatenate([-jnp.sin(ar), jnp.sin(ar), -jnp.sin(ac), jnp.sin(ac)], axis=-1)
    return jnp.tile(cos64, (1, 2)), jnp.tile(sin64, (1, 2))


SGU_TM = 512


def _group_mean(x, ones64):
    hi, lo = _split_bf16(x)
    return (_dot(hi, ones64) + _dot(lo, ones64)) * (1.0 / DG_C)


def _sgu_kernel(uv_ref, ws_ref, bs_ref, o_ref):
    r = lax.broadcasted_iota(jnp.int32, (W_C, W_C), 0)
    s = lax.broadcasted_iota(jnp.int32, (W_C, W_C), 1)
    ones64 = jnp.where((r // DG_C) == (s // DG_C), 1.0, 0.0).astype(BF16)
    lane = lax.broadcasted_iota(jnp.int32, (SGU_CHUNK, W_C), 1)
    for n in range(SGU_TM // SGU_CHUNK):
        rows = slice(n * SGU_CHUNK, (n + 1) * SGU_CHUNK)
        u = _gelu_tanh(uv_ref[rows, 0:256])
        v = _gelu_tanh(uv_ref[rows, 256:512])
        d = v - _group_mean(v, ones64)
        vn = d * lax.rsqrt(_group_mean(d * d, ones64) + EPS)
        s_mix = bs_ref[...]
        for g in range(N_GROUPS_C):
            vn_g = jnp.where((lane // DG_C) == g, vn, 0.0).astype(BF16)
            s_mix = s_mix + _dot(ws_ref[g], vn_g)
        o_ref[rows, :] = (u * s_mix).astype(BF16)


def _sgu_call(uv, ws, bs_full):
    n = T_ALL // SGU_TM
    return pl.pallas_call(
        _sgu_kernel,
        grid=(n,),
        in_specs=[pl.BlockSpec((SGU_TM, 512), lambda i: (i, 0)),
                  pl.BlockSpec(ws.shape, lambda i: (0, 0, 0)),
                  pl.BlockSpec(bs_full.shape, lambda i: (0, 0))],
        out_specs=pl.BlockSpec((SGU_TM, W_C), lambda i: (i, 0)),
        out_shape=jax.ShapeDtypeStruct((T_ALL, W_C), BF16),
        compiler_params=_cparams(("arbitrary",)),
        name="sgu",
    )(uv, ws, bs_full)


OUT_TM = 512


SEL_LANE0 = N_EXPERTS


def _top2_gates(logits):
    lane = lax.broadcasted_iota(jnp.int32, logits.shape, 1).astype(F32)
    neg = -jnp.inf
    lg = jnp.where(lane < N_EXPERTS, logits, neg)
    m1 = jnp.max(lg, axis=-1, keepdims=True)
    i1 = jnp.min(jnp.where(lg == m1, lane, float(LANES)), axis=-1, keepdims=True)
    lg2 = jnp.where(lane == i1, neg, lg)
    m2 = jnp.max(lg2, axis=-1, keepdims=True)
    i2 = jnp.min(jnp.where(lg2 == m2, lane, float(LANES)), axis=-1, keepdims=True)
    e2 = jnp.exp(m2 - m1)
    den = 1.0 + e2
    gates = jnp.where(lane == i1, 1.0 / den, 0.0) + jnp.where(lane == i2, e2 / den, 0.0)
    sel = jnp.where((lane == i1 + SEL_LANE0) | (lane == i2 + SEL_LANE0), 1.0, 0.0)
    return gates + sel


def _store_token_tiles(ref, val):
    n = val.shape[0]
    for k in range(D_MODEL // LANES):
        ref[pl.ds(k, n, stride=8), :] = val[:, k * LANES:(k + 1) * LANES]


def _load_token_tiles(ref, n):
    return jnp.concatenate([ref[pl.ds(k, n, stride=8), :] for k in range(D_MODEL // LANES)], axis=-1)


def _out_proj_kernel(*refs, n_x, moe):
    oa_refs, ob_refs, (oc_ref,), x_refs = refs[0:2], refs[2:4], refs[4:5], refs[5:5 + n_x]
    w_ref, g1_ref, nrm_ref, sc_ref, sh_ref, *rest = refs[5 + n_x:]
    if moe:
        rw_ref, x1_ref, h2t_ref, gates_ref, gates_t_ref, wo_ref = rest
    else:
        x1_ref, h2_ref, wo_ref = rest

    @pl.when(pl.program_id(0) == 0)
    def _():
        wo_ref[...] = w_ref[...].astype(BF16)

    y = (_dot(_pick_piece(oa_refs, OUT_TM), wo_ref[0:256, :]) + _dot(_pick_piece(ob_refs, OUT_TM), wo_ref[256:768, :])
         + _dot(oc_ref[...], wo_ref[768:1024, :]))
    x1 = _pick_piece(x_refs, OUT_TM) + g1_ref[...] * y
    x1_ref[...] = x1
    yn = x1 * lax.rsqrt(jnp.mean(x1 * x1, axis=-1, keepdims=True) + EPS)
    h = (yn * nrm_ref[...]) * (1.0 + sc_ref[...]) + sh_ref[...]
    if moe:
        _store_token_tiles(h2t_ref, h)
        gates = _top2_gates(_dot3(h, rw_ref[...]))
        gates_ref[...] = gates
        gates_t_ref[...] = gates.T
    else:
        h2_ref[...] = h.astype(BF16)


def _out_proj_call(oa_pieces, ob_pieces, oc, x_pieces, w_out, mod4, norm_g, router_w=None):
    tm = OUT_TM
    n = T_ALL // tm
    moe = router_w is not None
    row = functools.partial(_mod_row, tm=tm)
    mod_spec = lambda k: pl.BlockSpec((None, None, 1, D_MODEL), lambda i: (row(i), k, 0, 0))
    tok = lambda w: pl.BlockSpec((tm, w), lambda i: (i, 0))
    full = lambda a: pl.BlockSpec(a.shape, lambda i: (0,) * a.ndim)
    in_specs = (_piece_specs(oa_pieces, tm, W_A) + _piece_specs(ob_pieces, tm, W_B) + [tok(W_C)]
                + _piece_specs(x_pieces, tm, D_MODEL)
                + [_resident(w_out), mod_spec(2), full(norm_g), mod_spec(4), mod_spec(3)])
    args = [*oa_pieces, *ob_pieces, oc, *x_pieces, w_out, mod4, norm_g, mod4, mod4]
    if moe:
        in_specs.append(full(router_w))
        args.append(router_w)
        out_specs = [tok(D_MODEL), pl.BlockSpec((tm * 8, LANES), lambda i: (i, 0)), tok(LANES),
                     pl.BlockSpec((LANES, tm), lambda i: (0, i))]
        out_shape = [jax.ShapeDtypeStruct((T_ALL, D_MODEL), F32), jax.ShapeDtypeStruct((T_ALL * 8, LANES), F32),
                     jax.ShapeDtypeStruct((T_ALL, LANES), F32), jax.ShapeDtypeStruct((LANES, T_ALL), F32)]
    else:
        out_specs = [tok(D_MODEL), tok(D_MODEL)]
        out_shape = [jax.ShapeDtypeStruct((T_ALL, D_MODEL), F32), jax.ShapeDtypeStruct((T_ALL, D_MODEL), BF16)]
    return pl.pallas_call(
        functools.partial(_out_proj_kernel, n_x=len(x_pieces), moe=moe),
        grid=(n,),
        in_specs=in_specs,
        out_specs=out_specs,
        out_shape=out_shape,
        scratch_shapes=[pltpu.VMEM((D_MODEL, D_MODEL), BF16)],
        compiler_params=_cparams(("arbitrary",)),
        name="out_proj_moe" if moe else "out_proj",
    )(*args)


FFN_TM = 512
FFN_FC = 1408
FFN_NC = D_FF // FFN_FC


def _swiglu_chunk(h, wg_ref, wu_ref, wd_ref):
    act = _silu(_dot(h, wg_ref[...])) * _dot(h, wu_ref[...])
    return _dot(act.astype(BF16), wd_ref[...])


def _ffn_kernel(h_ref, x_ref, g2_ref, wg_ref, wu_ref, wd_ref, o_ref, acc_ref):
    j = pl.program_id(1)

    @pl.when(j == 0)
    def _():
        acc_ref[...] = jnp.zeros_like(acc_ref)

    acc_ref[...] += _swiglu_chunk(h_ref[...], wg_ref, wu_ref, wd_ref)

    @pl.when(j == FFN_NC - 1)
    def _():
        o_ref[...] = x_ref[...] + g2_ref[...] * acc_ref[...]


def _ffn_call(h2, x1, mod4, wg, wu, wd):
    tm, fc = FFN_TM, FFN_FC
    row = functools.partial(_mod_row, tm=tm)
    tok = lambda w: pl.BlockSpec((tm, w), lambda i, j: (i, 0))
    return pl.pallas_call(
        _ffn_kernel,
        grid=(T_ALL // tm, FFN_NC),
        in_specs=[tok(D_MODEL), tok(D_MODEL),
                  pl.BlockSpec((None, None, 1, D_MODEL), lambda i, j: (row(i), 5, 0, 0)),
                  pl.BlockSpec((D_MODEL, fc), lambda i, j: (0, j)),
                  pl.BlockSpec((D_MODEL, fc), lambda i, j: (0, j)),
                  pl.BlockSpec((fc, D_MODEL), lambda i, j: (j, 0))],
        out_specs=tok(D_MODEL),
        out_shape=jax.ShapeDtypeStruct((T_ALL, D_MODEL), F32),
        scratch_shapes=[pltpu.VMEM((tm, D_MODEL), F32)],
        compiler_params=_cparams(("arbitrary", "arbitrary")),
        name="ffn_dense",
    )(h2, x1, mod4, wg, wu, wd)


MOE_TM = 512
MOE_NT_MAX = (2 * T_ALL) // MOE_TM + N_EXPERTS
MOE_ROWS = MOE_NT_MAX * MOE_TM
PLAN_BLK = 512
MISC_LAST_START = 8
MISC_NT = 16


def _moe_plan_kernel(gt_ref, posa_ref, posb_ref, te_ref, ti_ref, misc_ref):
    tm = float(MOE_TM)
    sel = gt_ref[SEL_LANE0:SEL_LANE0 + N_EXPERTS, :]
    cnt = jnp.sum(sel, axis=1, keepdims=True)
    nt = jnp.floor((cnt + (tm - 1.0)) * (1.0 / tm))
    sub = lax.broadcasted_iota(jnp.int32, (N_EXPERTS, LANES), 0).astype(F32)
    lane = lax.broadcasted_iota(jnp.int32, (N_EXPERTS, LANES), 1).astype(F32)
    nt_b = jnp.broadcast_to(nt, (N_EXPERTS, LANES))
    nt_row = jnp.sum(jnp.where(sub == lane, nt_b, 0.0), axis=0, keepdims=True)
    toff = jnp.sum(jnp.where(lane < sub, jnp.broadcast_to(nt_row, (N_EXPERTS, LANES)), 0.0),
                   axis=1, keepdims=True)
    tend = toff + nt
    n_total = jnp.sum(nt, axis=0, keepdims=True)
    jc = jnp.minimum(lane, n_total - 1.0)
    te = jnp.sum(jnp.where(jc >= tend, 1.0, 0.0), axis=0, keepdims=True)
    te_ref[...] = te.astype(jnp.int32)
    ti_ref[...] = jc[0:1, :].astype(jnp.int32)
    last_start = (tend - 1.0) * tm
    ls_row = jnp.sum(jnp.where(sub + MISC_LAST_START == lane, jnp.broadcast_to(last_start, (N_EXPERTS, LANES)), 0.0),
                     axis=0, keepdims=True)
    nt_row2 = jnp.sum(jnp.where(sub + MISC_NT == lane, nt_b, 0.0), axis=0, keepdims=True)
    misc = jnp.where(lane[0:1, :] == 0.0, n_total, 0.0) + ls_row + nt_row2
    misc_ref[...] = misc.astype(jnp.int32)

    off = toff * tm
    r = lax.broadcasted_iota(jnp.int32, (PLAN_BLK, PLAN_BLK), 0)
    c = lax.broadcasted_iota(jnp.int32, (PLAN_BLK, PLAN_BLK), 1)
    upper = jnp.where(r <= c, 1.0, 0.0).astype(BF16)
    carry = jnp.zeros((N_EXPERTS, 1), F32)
    for blk in range(T_ALL // PLAN_BLK):
        cols = slice(blk * PLAN_BLK, (blk + 1) * PLAN_BLK)
        s = gt_ref[SEL_LANE0:SEL_LANE0 + N_EXPERTS, cols]
        rank = _dot(s.astype(BF16), upper) + carry
        pos = off + rank - 1.0
        posa_ref[:, cols] = jnp.min(jnp.where(s > 0.0, pos, 1e9), axis=0, keepdims=True).astype(jnp.int32)
        posb_ref[:, cols] = jnp.max(jnp.where(s > 0.0, pos, -1.0), axis=0, keepdims=True).astype(jnp.int32)
        carry = carry + jnp.sum(s, axis=1, keepdims=True)


def _moe_plan_call(gates_t):
    row = lambda w: jax.ShapeDtypeStruct((1, w), jnp.int32)
    full = lambda w: pl.BlockSpec((1, w), lambda: (0, 0))
    return pl.pallas_call(
        _moe_plan_kernel,
        in_specs=[pl.BlockSpec((LANES, T_ALL), lambda: (0, 0))],
        out_specs=[full(T_ALL), full(T_ALL), full(LANES), full(LANES), full(LANES)],
        out_shape=[row(T_ALL), row(T_ALL), row(LANES), row(LANES), row(LANES)],
        compiler_params=pltpu.CompilerParams(vmem_limit_bytes=VMEM_LIMIT),
        name="moe_plan",
    )(gates_t)


DMA_UNROLL = 8


def _row_tile(ref, row):
    return ref.at[pl.ds(pl.multiple_of(row * 8, 8), 8), :]


def _moe_scatter_kernel(misc_ref, posa_ref, posb_ref, h_ref, xs_ref, zero_ref, sem):
    tm = h_ref.shape[0] // 8

    @pl.when(pl.program_id(0) == 0)
    def _():
        zero_ref[...] = jnp.zeros_like(zero_ref)

        def zero_tile(first_row):
            start = pl.multiple_of(first_row * 8, 8)
            cp = pltpu.make_async_copy(zero_ref, xs_ref.at[pl.ds(start, MOE_TM * 8), :], sem.at[0])
            cp.start()
            cp.wait()

        for e in range(N_EXPERTS):
            @pl.when(misc_ref[0, MISC_NT + e] > 0)
            def _():
                zero_tile(misc_ref[0, MISC_LAST_START + e])

        def zero_tail(j, carry):
            zero_tile(j * MOE_TM)
            return carry

        lax.fori_loop(misc_ref[0, 0], MOE_NT_MAX, zero_tail, 0)

    def issue(r, carry):
        src = _row_tile(h_ref, r)
        pltpu.make_async_copy(src, _row_tile(xs_ref, posa_ref[0, r]), sem.at[0]).start(priority=0)
        pltpu.make_async_copy(src, _row_tile(xs_ref, posb_ref[0, r]), sem.at[1]).start(priority=1)
        return carry

    lax.fori_loop(0, tm, issue, 0, unroll=DMA_UNROLL)
    for k in range(2):
        pltpu.make_async_copy(h_ref, xs_ref.at[pl.ds(0, tm * 8), :], sem.at[k]).wait()


SCATTER_TM = 512


def _moe_scatter_call(misc, posa3, posb3, h2t):
    tm = SCATTER_TM
    smem_row = pl.BlockSpec((None, 1, tm), lambda i: (i, 0, 0), memory_space=pltpu.SMEM)
    return pl.pallas_call(
        _moe_scatter_kernel,
        grid=(T_ALL // tm,),
        in_specs=[pl.BlockSpec((1, LANES), lambda i: (0, 0), memory_space=pltpu.SMEM), smem_row, smem_row,
                  pl.BlockSpec((tm * 8, LANES), lambda i: (i, 0))],
        out_specs=pl.BlockSpec(memory_space=pl.ANY),
        out_shape=jax.ShapeDtypeStruct((MOE_ROWS * 8, LANES), F32),
        scratch_shapes=[pltpu.VMEM((MOE_TM * 8, LANES), F32), pltpu.SemaphoreType.DMA((2,))],
        compiler_params=_cparams(("arbitrary",)),
        name="moe_scatter",
    )(misc, posa3, posb3, h2t)


def _ffn_grouped_kernel(te_ref, ti_ref, misc_ref, x_ref, wg_ref, wu_ref, wd_ref, o_ref, xb_ref, acc_ref):
    j = pl.program_id(0)
    c = pl.program_id(1)

    @pl.when(j < misc_ref[0])
    def _():
        @pl.when(c == 0)
        def _():
            xb_ref[...] = _load_token_tiles(x_ref, MOE_TM).astype(BF16)
            acc_ref[...] = jnp.zeros_like(acc_ref)

        acc_ref[...] += _swiglu_chunk(xb_ref[...], wg_ref, wu_ref, wd_ref)

        @pl.when(c == FFN_NC - 1)
        def _():
            _store_token_tiles(o_ref, acc_ref[...])

    @pl.when((j >= misc_ref[0]) & (c == FFN_NC - 1))
    def _():
        o_ref[...] = jnp.zeros_like(o_ref)


def _ffn_grouped_call(te, ti, misc, xs, wg, wu, wd):
    fc = FFN_FC

    def chunk(j, c, misc):
        return jnp.where(j < misc[0], c, FFN_NC - 1)

    grid_spec = pltpu.PrefetchScalarGridSpec(
        num_scalar_prefetch=3,
        grid=(MOE_NT_MAX, FFN_NC),
        in_specs=[pl.BlockSpec((MOE_TM * 8, LANES), lambda j, c, te, ti, misc: (ti[j], 0)),
                  pl.BlockSpec((None, D_MODEL, fc), lambda j, c, te, ti, misc: (te[j], 0, chunk(j, c, misc))),
                  pl.BlockSpec((None, D_MODEL, fc), lambda j, c, te, ti, misc: (te[j], 0, chunk(j, c, misc))),
                  pl.BlockSpec((None, fc, D_MODEL), lambda j, c, te, ti, misc: (te[j], chunk(j, c, misc), 0))],
        out_specs=pl.BlockSpec((MOE_TM * 8, LANES), lambda j, c, te, ti, misc: (j, 0)),
        scratch_shapes=[pltpu.VMEM((MOE_TM, D_MODEL), BF16), pltpu.VMEM((MOE_TM, D_MODEL), F32)],
    )
    return pl.pallas_call(
        _ffn_grouped_kernel,
        grid_spec=grid_spec,
        out_shape=jax.ShapeDtypeStruct((MOE_ROWS * 8, LANES), F32),
        compiler_params=_cparams(("arbitrary", "arbitrary")),
        name="ffn_grouped",
    )(te, ti, misc, xs, wg, wu, wd)


COMBINE_TM = 256


def _moe_combine_kernel(posa_ref, posb_ref, posa_next_ref, posb_next_ref, ys_ref, x_ref, g2_ref, gates_ref, nf_ref,
                        op_ref, os_ref, bufa_ref, bufb_ref, sem):
    tm = COMBINE_TM
    i = pl.program_id(0)
    n = pl.num_programs(0)
    slot = i % 2

    def gather(pa_ref, pb_ref, s):
        def issue(r, carry):
            pltpu.make_async_copy(_row_tile(ys_ref, pa_ref[0, r]), _row_tile(bufa_ref.at[s], r),
                                  sem.at[s, 0]).start(priority=0)
            pltpu.make_async_copy(_row_tile(ys_ref, pb_ref[0, r]), _row_tile(bufb_ref.at[s], r),
                                  sem.at[s, 1]).start(priority=1)
            return carry

        lax.fori_loop(0, tm, issue, 0, unroll=DMA_UNROLL)

    @pl.when(i == 0)
    def _():
        gather(posa_ref, posb_ref, 0)

    @pl.when(i + 1 < n)
    def _():
        gather(posa_next_ref, posb_next_ref, 1 - slot)

    gates = gates_ref[...]
    lane = lax.broadcasted_iota(jnp.int32, gates.shape, 1).astype(F32)
    is_sel = (lane >= SEL_LANE0) & (lane < SEL_LANE0 + N_EXPERTS) & (gates > 0.0)
    ia = jnp.min(jnp.where(is_sel, lane, float(LANES)), axis=-1, keepdims=True) - SEL_LANE0
    ib = jnp.max(jnp.where(is_sel, lane, -1.0), axis=-1, keepdims=True) - SEL_LANE0
    wa = jnp.sum(jnp.where(lane == ia, gates, 0.0), axis=-1, keepdims=True)
    wb = jnp.sum(jnp.where(lane == ib, gates, 0.0), axis=-1, keepdims=True)

    pltpu.make_async_copy(ys_ref.at[pl.ds(0, tm * 8), :], bufa_ref.at[slot], sem.at[slot, 0]).wait()
    pltpu.make_async_copy(ys_ref.at[pl.ds(0, tm * 8), :], bufb_ref.at[slot], sem.at[slot, 1]).wait()

    y = wa * _load_token_tiles(bufa_ref.at[slot], tm) + wb * _load_token_tiles(bufb_ref.at[slot], tm)
    out = x_ref[...] + g2_ref[...] * y
    out = (out * lax.rsqrt(jnp.mean(out * out, axis=-1, keepdims=True) + EPS)) * nf_ref[...]

    @pl.when(i < T_PROMPT // tm)
    def _():
        op_ref[...] = out

    @pl.when(i >= T_PROMPT // tm)
    def _():
        os_ref[...] = out


def _moe_combine_call(posa3, posb3, ys, x1, mod4, gates, norm_f):
    tm = COMBINE_TM
    n = T_ALL // tm
    n_p = T_PROMPT // tm
    row = functools.partial(_mod_row, tm=tm)
    smem_row = pl.BlockSpec((None, 1, tm), lambda i: (i, 0, 0), memory_space=pltpu.SMEM)
    smem_next = pl.BlockSpec((None, 1, tm), lambda i: (jnp.minimum(i + 1, n - 1), 0, 0), memory_space=pltpu.SMEM)
    tok = lambda w: pl.BlockSpec((tm, w), lambda i: (i, 0))
    return pl.pallas_call(
        _moe_combine_kernel,
        grid=(n,),
        in_specs=[smem_row, smem_row, smem_next, smem_next, pl.BlockSpec(memory_space=pl.ANY), tok(D_MODEL),
                  pl.BlockSpec((None, None, 1, D_MODEL), lambda i: (row(i), 5, 0, 0)), tok(LANES),
                  pl.BlockSpec((1, D_MODEL), lambda i: (0, 0))],
        out_specs=[pl.BlockSpec((tm, D_MODEL), lambda i: (jnp.minimum(i, n_p - 1), 0)),
                   pl.BlockSpec((tm, D_MODEL), lambda i: (jnp.maximum(i - n_p, 0), 0))],
        out_shape=[jax.ShapeDtypeStruct((T_PROMPT, D_MODEL), F32), jax.ShapeDtypeStruct((T_SAMPLE, D_MODEL), F32)],
        scratch_shapes=[pltpu.VMEM((2, tm * 8, LANES), F32), pltpu.VMEM((2, tm * 8, LANES), F32),
                        pltpu.SemaphoreType.DMA((2, 2))],
        compiler_params=_cparams(("arbitrary",)),
        name="moe_combine",
    )(posa3, posb3, posa3, posb3, ys, x1, mod4, gates, norm_f)


def _moe_call(h2t, x1, mod4, gates, gates_t, wg, wu, wd, norm_f):
    posa, posb, te, ti, misc = _moe_plan_call(gates_t)
    xs = _moe_scatter_call(misc, posa.reshape(T_ALL // SCATTER_TM, 1, SCATTER_TM),
                           posb.reshape(T_ALL // SCATTER_TM, 1, SCATTER_TM), h2t)
    ys = _ffn_grouped_call(te.reshape(LANES), ti.reshape(LANES), misc.reshape(LANES), xs, wg, wu, wd)
    return _moe_combine_call(posa.reshape(T_ALL // COMBINE_TM, 1, COMBINE_TM),
                             posb.reshape(T_ALL // COMBINE_TM, 1, COMBINE_TM), ys, x1, mod4, gates, norm_f)


def kernel(x_prompt, x_sample, cache_k, cache_v, state_gla_fwd, state_gla_bwd, c, c_ctx, w_ada, b_ada, norm_mix, norm_ffn, w_in, w_out, gla_w_up, gla_b_up, gla_norm, diff_lambda, diff_norm, sgu_w, sgu_b, ffn_w_gate, ffn_w_up, ffn_w_down, router_w, moe_w_gate, moe_w_up, moe_w_down, norm_f):
    assert DEPTH == 2
    x_pieces = [x_prompt.reshape(T_PROMPT, D_MODEL), x_sample.reshape(T_SAMPLE, D_MODEL)]
    cvecs = jnp.concatenate([c_ctx[None, :], c, jnp.zeros((N_MOD_ROWS - 1 - DEC_BATCH, D_MODEL), F32)], axis=0)
    mod = _ada_call(cvecs, w_ada, b_ada)
    cos, sin_signed = _rope_tables()
    zeros_state = jnp.zeros((BATCH, 64, 256), F32)

    ks, vs, sfs, sbs = [], [], [], []
    for l in range(DEPTH):
        mod4 = mod[l].reshape(N_MOD_ROWS, 6, 1, D_MODEL)
        w_up = jnp.zeros((LANES, 2 * W_A), F32)
        w_up = w_up.at[0:GLA_RANK, 0:W_A].set(gla_w_up[l, 0]).at[GLA_RANK:2 * GLA_RANK, W_A:].set(gla_w_up[l, 1])
        b_up = gla_b_up[l].reshape(1, 2 * W_A)
        g4, la, qb, kb, vb, uv = _in_proj_call(x_pieces, norm_mix[l][None, :], mod4, w_in[l], w_up.astype(BF16), b_up)

        gain_a = gla_norm[l][None, :]
        oa_p, sf, sb = _gla_call(g4, la, zeros_state, zeros_state, gain_a,
                                 batch=BATCH, seq=SEQ, row_block0=0)
        oa_s, _, _ = _gla_call(g4, la, _state_to_kernel(state_gla_fwd[:, l]),
                               _state_to_kernel(state_gla_bwd[:, l]), gain_a,
                               batch=DEC_BATCH, seq=DEC_SEQ, row_block0=T_PROMPT // DEC_SEQ)

        lam_init = 0.8 - 0.6 * math.exp(-0.3 * l)
        gain_b = diff_norm[l][None, :]
        ob_p = _attn_prompt_call(diff_lambda[l], qb, kb, vb, gain_b, lam_init=lam_init)
        ob_s = _attn_sample_call(diff_lambda[l], qb, kb, vb,
                                 cache_k[:, l].reshape(DEC_BATCH, PAST_LEN, W_B),
                                 cache_v[:, l].reshape(DEC_BATCH, PAST_LEN, W_B),
                                 cos, sin_signed, gain_b, lam_init=lam_init)

        bs_full = jnp.repeat(sgu_b[l].T, DG_C, axis=1)
        oc = _sgu_call(uv, sgu_w[l].astype(BF16), bs_full)

        if l == 0:
            x1, h2 = _out_proj_call([oa_p, oa_s], [ob_p, ob_s], oc, x_pieces, w_out[l], mod4, norm_ffn[l][None, :])
            x_pieces = [_ffn_call(h2, x1, mod4, ffn_w_gate[0].astype(BF16), ffn_w_up[0].astype(BF16),
                                  ffn_w_down[0].astype(BF16))]
        else:
            rw = jnp.pad(router_w[0], ((0, 0), (0, LANES - N_EXPERTS)))
            x1, h2t, gates, gates_t = _out_proj_call([oa_p, oa_s], [ob_p, ob_s], oc, x_pieces, w_out[l], mod4,
                                                     norm_ffn[l][None, :], rw)
            y_prompt, y_sample = _moe_call(h2t, x1, mod4, gates, gates_t, moe_w_gate[0].astype(BF16),
                                           moe_w_up[0].astype(BF16), moe_w_down[0].astype(BF16), norm_f[None, :])

        ks.append(kb[:T_PROMPT].reshape(BATCH, SEQ, N_HEADS_B, 2 * DH_B))
        vs.append(vb[:T_PROMPT].reshape(BATCH, SEQ, N_HEADS_B, DV_B))
        sfs.append(_state_from_kernel(sf))
        sbs.append(_state_from_kernel(sb))

    return (y_prompt.reshape(BATCH, SEQ, D_MODEL), y_sample.reshape(DEC_BATCH, DEC_SEQ, D_MODEL),
            jnp.stack(ks, axis=1), jnp.stack(vs, axis=1), jnp.stack(sfs, axis=1), jnp.stack(sbs, axis=1))
```

```python
import functools
import math

import jax
import jax.numpy as jnp
import numpy as np
from jax import lax
from jax.experimental import pallas as pl
from jax.experimental.pallas import tpu as pltpu

F32 = jnp.float32
BF16 = jnp.bfloat16

D_MODEL = 1024
BATCH = 32
SEQ = 256
DEPTH = 2
DEC_BATCH = 2
DEC_SEQ = 1024
PAST_LEN = 256
GRID_W = 64
N_HEADS_A = 4
DK_A = 64
W_A = 256
GLA_RANK = 16
GLA_TAU = 16.0
GLA_CHUNK = 64
N_HEADS_B = 4
DH_B = 64
DV_B = 128
W_B = 512
ROPE_THETA = 10000.0
AXIS_PAIRS = DH_B // 4
N_GROUPS_C = 4
DG_C = 64
W_C = 256
SGU_CHUNK = 128
D_FF = 2816
N_EXPERTS = 8
EPS = 1e-6

T_PROMPT = BATCH * SEQ
T_SAMPLE = DEC_BATCH * DEC_SEQ
T_ALL = T_PROMPT + T_SAMPLE
N_MOD_ROWS = 8
LANES = 128
VMEM_LIMIT = 56 * 1024 * 1024


def _cparams(sem):
    return pltpu.CompilerParams(dimension_semantics=sem, vmem_limit_bytes=VMEM_LIMIT)


def _dot(a, b):
    return jnp.dot(a, b, preferred_element_type=F32)


def _dot_nt(a, b):
    return lax.dot_general(a, b, (((1,), (1,)), ((), ())), preferred_element_type=F32)


def _dot_tn(a, b):
    return lax.dot_general(a, b, (((0,), (0,)), ((), ())), preferred_element_type=F32)


def _split_bf16(x):
    hi = x.astype(BF16)
    lo = (x - hi.astype(F32)).astype(BF16)
    return hi, lo


def _dot3(a, w):
    a_hi, a_lo = _split_bf16(a)
    w_hi, w_lo = _split_bf16(w)
    return _dot(a_hi, w_hi) + (_dot(a_lo, w_hi) + _dot(a_hi, w_lo))


def _sigmoid(x):
    return 1.0 / (1.0 + jnp.exp(-x))


def _silu(x):
    return x * _sigmoid(x)


def _gelu_tanh(x):
    c = math.sqrt(2.0 / math.pi)
    return x * (0.5 * (1.0 + jnp.tanh(c * (x + 0.044715 * (x * x * x)))))


def _log_sigmoid(x):
    return jnp.minimum(x, 0.0) - jnp.log(1.0 + jnp.exp(-jnp.abs(x)))


def _mod_row(i, tm):
    n_p = T_PROMPT // tm
    per_b = DEC_SEQ // tm
    return jnp.where(i < n_p, 0, 1 + (i - n_p) // per_b)


ADA_TN = 1536


def _ada_kernel(c_ref, w_ref, b_ref, o_ref):
    a = _silu(c_ref[...])
    o_ref[...] = _dot3(a, w_ref[...]) + b_ref[...]


def _ada_call(cvecs, w_ada, b_ada):
    n_col = (6 * D_MODEL) // ADA_TN
    return pl.pallas_call(
        _ada_kernel,
        grid=(DEPTH, n_col),
        in_specs=[
            pl.BlockSpec((N_MOD_ROWS, D_MODEL), lambda l, j: (0, 0)),
            pl.BlockSpec((None, D_MODEL, ADA_TN), lambda l, j: (l, 0, j)),
            pl.BlockSpec((None, 1, ADA_TN), lambda l, j: (l, 0, j)),
        ],
        out_specs=pl.BlockSpec((None, N_MOD_ROWS, ADA_TN), lambda l, j: (l, 0, j)),
        out_shape=jax.ShapeDtypeStruct((DEPTH, N_MOD_ROWS, 6 * D_MODEL), F32),
        compiler_params=_cparams(("arbitrary", "arbitrary")),
        name="ada_mod",
    )(cvecs, w_ada, b_ada.reshape(DEPTH, 1, 6 * D_MODEL))


IN_TM = 256
IN_COLS = 3104
Z_COL0 = 1024
Z_COLS = 2 * GLA_RANK
W_MAIN = 3072


def _piece_specs(pieces, tm, width):
    specs, t0 = [], 0
    for arr in pieces:
        nt = arr.shape[0] // tm
        specs.append(pl.BlockSpec((tm, width), lambda i, t0=t0, nt=nt: (jnp.clip(i - t0, 0, nt - 1), 0)))
        t0 += nt
    assert t0 * tm == T_ALL and len(pieces) in (1, 2) and (len(pieces) == 1 or pieces[0].shape[0] == T_PROMPT)
    return specs


def _pick_piece(refs, tm):
    if len(refs) == 1:
        return refs[0][...]
    return jnp.where(pl.program_id(0) < T_PROMPT // tm, refs[0][...], refs[1][...])


def _in_proj_kernel(*refs, n_x):
    x_refs = refs[:n_x]
    (nrm_ref, sh_ref, sc_ref, w_ref, wup_ref, bup_ref,
     g4_ref, la_ref, qb_ref, kb_ref, vb_ref, uv_ref, wm_ref, wz_ref) = refs[n_x:]

    @pl.when(pl.program_id(0) == 0)
    def _():
        wm_ref[:, 0:Z_COL0] = w_ref[:, 0:Z_COL0].astype(BF16)
        wm_ref[:, Z_COL0:W_MAIN] = w_ref[:, Z_COL0 + Z_COLS:IN_COLS].astype(BF16)
        wz_ref[...] = jnp.zeros_like(wz_ref)
        wz_ref[:, 0:Z_COLS] = w_ref[:, Z_COL0:Z_COL0 + Z_COLS].astype(BF16)

    x = _pick_piece(x_refs, IN_TM)
    y = x * lax.rsqrt(jnp.mean(x * x, axis=-1, keepdims=True) + EPS)
    h = (y * nrm_ref[...]) * (1.0 + sc_ref[...]) + sh_ref[...]
    hb = h.astype(BF16)
    g4_ref[...] = _dot(hb, wm_ref[:, 0:1024])
    qb_ref[...] = _dot(hb, wm_ref[:, 1024:1536])
    kb_ref[...] = _dot(hb, wm_ref[:, 1536:2048])
    vb_ref[...] = _dot(hb, wm_ref[:, 2048:2560])
    uv_ref[...] = _dot(hb, wm_ref[:, 2560:3072])
    z = _dot(hb, wz_ref[...])
    zz = _dot(z.astype(BF16), wup_ref[...]) + bup_ref[...]
    la_ref[...] = _log_sigmoid(zz) * (1.0 / GLA_TAU)


def _resident_layer(a, l):
    return pl.BlockSpec((None,) + a.shape[1:], lambda *_: (l,) + (0,) * (a.ndim - 1), pipeline_mode=pl.Buffered(1))


def _in_proj_call(x_pieces, norm_g, mod4, w_in, l, w_up, b_up):
    tm = IN_TM
    n = T_ALL // tm
    row = functools.partial(_mod_row, tm=tm)
    mod_spec = lambda k: pl.BlockSpec((None, None, 1, D_MODEL), lambda i: (row(i), k, 0, 0))
    full = lambda a: pl.BlockSpec(a.shape, lambda i: (0,) * a.ndim)
    out = lambda w: pl.BlockSpec((tm, w), lambda i: (i, 0))
    return pl.pallas_call(
        functools.partial(_in_proj_kernel, n_x=len(x_pieces)),
        grid=(n,),
        in_specs=_piece_specs(x_pieces, tm, D_MODEL) + [full(norm_g), mod_spec(0), mod_spec(1), _resident_layer(w_in, l),
                                                        full(w_up), full(b_up)],
        out_specs=[out(1024), out(512), out(512), out(512), out(512), out(512)],
        out_shape=[jax.ShapeDtypeStruct((T_ALL, w), F32) for w in (1024, 512, 512, 512, 512, 512)],
        scratch_shapes=[pltpu.VMEM((D_MODEL, W_MAIN), BF16), pltpu.VMEM((D_MODEL, LANES), BF16)],
        compiler_params=_cparams(("arbitrary",)),
        name="in_proj",
    )(*x_pieces, norm_g, mod4, mod4, w_in, w_up, b_up)


GLA_SB = 256
GLA_NC = GLA_SB // GLA_CHUNK


def _gla_superblock(q, k, vb, v_heads, la, st_all, tri, mask, same64, head_lanes, forward):
    c = GLA_CHUNK
    mid, last = (c // 2 - 1, c - 1) if forward else (c // 2, 0)
    la_hi, la_lo = _split_bf16(la)
    b = _dot(tri, la_hi) + _dot(tri, la_lo)
    rows_of = lambda r: jnp.concatenate(
        [jnp.broadcast_to(b[i * c + r:i * c + r + 1, :], (c, W_A)) for i in range(GLA_NC)], axis=0)
    m = rows_of(mid)
    bl = rows_of(last)
    qe = (q * jnp.exp(b - m)).astype(BF16)
    ke = k * jnp.exp(m - b)
    qi = (q * jnp.exp(b)).astype(BF16)
    ks = (k * jnp.exp(bl - b)).astype(BF16)
    o = jnp.zeros((GLA_SB, W_A), F32)
    for h in range(N_HEADS_A):
        ke_h = jnp.where(head_lanes[h], ke, 0.0).astype(BF16)
        a = jnp.where(mask, _dot_nt(qe, ke_h), 0.0).astype(BF16)
        o = o + _dot(a, v_heads[h])
    outs = [None] * GLA_NC
    for i in (range(GLA_NC) if forward else reversed(range(GLA_NC))):
        rows = slice(i * c, (i + 1) * c)
        outs[i] = o[rows, :] + _dot_nt(qi[rows, :], st_all.astype(BF16))
        kv = _dot_tn(vb[rows, :], ks[rows, :])
        st_all = st_all * jnp.exp(bl[i * c:i * c + 1, :]) + jnp.where(same64, kv, 0.0)
    return jnp.concatenate(outs, axis=0), st_all


def _gla_kernel(g4_ref, la_ref, s0f_ref, s0b_ref, gain_ref, o_ref, sf_ref, sb_ref, of_ref, ob_ref, *, seq):
    n = GLA_SB
    nsb = seq // n
    r = lax.broadcasted_iota(jnp.int32, (n, n), 0)
    s = lax.broadcasted_iota(jnp.int32, (n, n), 1)
    same64 = (r // GLA_CHUNK) == (s // GLA_CHUNK)
    lower = same64 & (s <= r)
    upper = same64 & (s >= r)
    tri_f = jnp.where(lower, 1.0, 0.0).astype(BF16)
    tri_b = jnp.where(upper, 1.0, 0.0).astype(BF16)
    ones64 = jnp.where(same64, 1.0, 0.0).astype(BF16)
    head_lanes = [(s // DK_A) == h for h in range(N_HEADS_A)]
    scale = DK_A ** -0.5
    expand = lambda st: jnp.where(same64, jnp.concatenate([st] * N_HEADS_A, axis=0), 0.0)
    compact = lambda st_all: functools.reduce(
        lambda a, b: a + b, [st_all[h * 64:(h + 1) * 64, :] for h in range(N_HEADS_A)])

    def step(i, carry):
        stf, stb = carry
        rf = pl.ds(pl.multiple_of(i * n, n), n)
        rb = pl.ds(pl.multiple_of((nsb - 1 - i) * n, n), n)
        for rows, forward in ((rf, True), (rb, False)):
            q = g4_ref[rows, 0:256] * scale
            k = g4_ref[rows, 256:512]
            v = g4_ref[rows, 512:768]
            vb = v.astype(BF16)
            v_heads = [jnp.where(head_lanes[h], v, 0.0).astype(BF16) for h in range(N_HEADS_A)]
            if forward:
                o, stf = _gla_superblock(q, k, vb, v_heads, la_ref[rows, 0:256], stf, tri_f, lower, same64,
                                         head_lanes, True)
                of_ref[rows, :] = o
            else:
                o, stb = _gla_superblock(q, k, vb, v_heads, la_ref[rows, 256:512], stb, tri_b, upper, same64,
                                         head_lanes, False)
                ob_ref[rows, :] = o
        return stf, stb

    stf, stb = lax.fori_loop(0, nsb, step, (expand(s0f_ref[...]), expand(s0b_ref[...])))
    sf_ref[...] = compact(stf)
    sb_ref[...] = compact(stb)

    gain = gain_ref[...]

    def finish(i, carry):
        rows = pl.ds(pl.multiple_of(i * n, n), n)
        o = of_ref[rows, :] + ob_ref[rows, :]
        sq_hi, sq_lo = _split_bf16(o * o)
        ms = (_dot(sq_hi, ones64) + _dot(sq_lo, ones64)) * (1.0 / DK_A)
        y = (o * lax.rsqrt(ms + EPS)) * gain
        o_ref[rows, :] = (y * _silu(g4_ref[rows, 768:1024])).astype(BF16)
        return carry

    lax.fori_loop(0, nsb, finish, 0)


def _gla_call(g4, la, s0f, s0b, gain, *, batch, seq, row_block0):
    tok = lambda w: pl.BlockSpec((seq, w), lambda b: (row_block0 + b, 0))
    st = pl.BlockSpec((None, 64, 256), lambda b: (b, 0, 0))
    return pl.pallas_call(
        functools.partial(_gla_kernel, seq=seq),
        grid=(batch,),
        in_specs=[tok(1024), tok(512), st, st, pl.BlockSpec((1, W_A), lambda b: (0, 0))],
        out_specs=[pl.BlockSpec((seq, W_A), lambda b: (b, 0)), st, st],
        out_shape=[jax.ShapeDtypeStruct((batch * seq, W_A), BF16),
                   jax.ShapeDtypeStruct((batch, 64, 256), F32),
                   jax.ShapeDtypeStruct((batch, 64, 256), F32)],
        scratch_shapes=[pltpu.VMEM((seq, W_A), F32), pltpu.VMEM((seq, W_A), F32)],
        compiler_params=_cparams(("arbitrary",)),
        name=f"gla_{seq}",
    )(g4, la, s0f, s0b, gain)


def _state_to_kernel(s):
    b = s.shape[0]
    return jnp.transpose(s, (0, 3, 1, 2)).reshape(b, 64, 256)


def _state_from_kernel(st):
    b = st.shape[0]
    return jnp.transpose(st.reshape(b, 64, N_HEADS_A, DK_A), (0, 2, 3, 1))


def _lambda(lv, lam_init):
    l01 = jnp.sum(lv[0:1, :] * lv[1:2, :], axis=-1, keepdims=True)
    l23 = jnp.sum(lv[2:3, :] * lv[3:4, :], axis=-1, keepdims=True)
    return jnp.exp(l01) - jnp.exp(l23) + lam_init


def _softmax_parts(parts):
    mx = functools.reduce(jnp.maximum, [jnp.max(p, axis=-1, keepdims=True) for p in parts])
    es = [jnp.exp(p - mx) for p in parts]
    den = functools.reduce(lambda a, b: a + b, [jnp.sum(e, axis=-1, keepdims=True) for e in es])
    return [e / den for e in es]


def _diff_finish(o, gain, lam_init):
    o = o * lax.rsqrt(jnp.mean(o * o, axis=-1, keepdims=True) + EPS)
    return ((o * gain) * (1.0 - lam_init)).astype(BF16)


QK_SCALE = DH_B ** -0.5


def _key_halves(k):
    first = lax.broadcasted_iota(jnp.int32, k.shape, 1) < DH_B
    return jnp.where(first, k, 0.0).astype(BF16), jnp.where(first, 0.0, k).astype(BF16)


def _attn_prompt_kernel(lv_ref, q_ref, k_ref, v_ref, gain_ref, *rest, lam_init, n_prev):
    prev_refs, (o_ref, *cache_refs) = rest[:2 * n_prev], rest[2 * n_prev:]
    lam = _lambda(lv_ref[...], lam_init)
    for h in range(N_HEADS_B):
        cols = slice(h * DV_B, (h + 1) * DV_B)
        q = (q_ref[:, cols] * QK_SCALE).astype(BF16)
        k1, k2 = _key_halves(k_ref[:, cols])
        (p1,) = _softmax_parts([_dot_nt(q, k1)])
        (p2,) = _softmax_parts([_dot_nt(q, k2)])
        a = p1 - lam * p2
        o = _dot(a.astype(BF16), v_ref[:, cols].astype(BF16))
        o_ref[:, cols] = _diff_finish(o, gain_ref[:, cols], lam_init)
        if cache_refs:
            ck_ref, cv_ref = cache_refs
            layers_k = [*prev_refs[0::2], k_ref]
            layers_v = [*prev_refs[1::2], v_ref]
            for l in range(n_prev + 1):
                ck_ref[l, :, h, :] = layers_k[l][:, cols]
                cv_ref[l, :, h, :] = layers_v[l][:, cols]


def _attn_prompt_call(lv, qb, kb, vb, gain, prev_kv=(), *, lam_init, write_cache=False):
    blk = pl.BlockSpec((SEQ, W_B), lambda b: (b, 0))
    n_prev = len(prev_kv)
    out_specs = [blk]
    out_shape = [jax.ShapeDtypeStruct((T_PROMPT, W_B), BF16)]
    if write_cache:
        cache = jax.ShapeDtypeStruct((BATCH, n_prev + 1, SEQ, N_HEADS_B, DV_B), F32)
        cblk = pl.BlockSpec((None, n_prev + 1, SEQ, N_HEADS_B, DV_B), lambda b: (b, 0, 0, 0, 0))
        out_specs += [cblk, cblk]
        out_shape += [cache, cache]
    return pl.pallas_call(
        functools.partial(_attn_prompt_kernel, lam_init=lam_init, n_prev=n_prev),
        grid=(BATCH,),
        in_specs=[pl.BlockSpec((4, DH_B), lambda b: (0, 0)), blk, blk, blk,
                  pl.BlockSpec((1, W_B), lambda b: (0, 0))] + [blk] * (2 * n_prev),
        out_specs=out_specs,
        out_shape=out_shape,
        compiler_params=_cparams(("arbitrary",)),
        name="diff_attn_prompt",
    )(lv, qb, kb, vb, gain, *[a for kv in prev_kv for a in kv])


def _rope(x, cos, sin_signed):
    lane = lax.broadcasted_iota(jnp.int32, x.shape, 1)
    first = (lane % (2 * AXIS_PAIRS)) < AXIS_PAIRS
    partner = jnp.where(first, pltpu.roll(x, LANES - AXIS_PAIRS, 1), pltpu.roll(x, AXIS_PAIRS, 1))
    return x * cos + partner * sin_signed


ATT_TQ = 256


def _attn_sample_kernel(lv_ref, q_ref, k_ref, v_ref, kc_ref, vc_ref, cosq_ref, sinq_ref,
                        cosk_ref, sink_ref, gain_ref, o_ref, k1_ref, k2_ref, *, lam_init):
    @pl.when(pl.program_id(1) == 0)
    def _():
        for h in range(N_HEADS_B):
            cols = slice(h * DV_B, (h + 1) * DV_B)
            k1_ref[:, cols], k2_ref[:, cols] = _key_halves(_rope(k_ref[:, cols], cosk_ref[...], sink_ref[...]))

    lam = _lambda(lv_ref[...], lam_init)
    for h in range(N_HEADS_B):
        cols = slice(h * DV_B, (h + 1) * DV_B)
        q = (_rope(q_ref[:, cols], cosq_ref[...], sinq_ref[...]) * QK_SCALE).astype(BF16)
        c1, c2 = _key_halves(kc_ref[:, cols])
        p1 = _softmax_parts([_dot_nt(q, k1_ref[:, cols]), _dot_nt(q, c1)])
        p2 = _softmax_parts([_dot_nt(q, k2_ref[:, cols]), _dot_nt(q, c2)])
        a_own = p1[0] - lam * p2[0]
        a_ctx = p1[1] - lam * p2[1]
        o = (_dot(a_own.astype(BF16), v_ref[:, cols].astype(BF16))
             + _dot(a_ctx.astype(BF16), vc_ref[:, cols].astype(BF16)))
        o_ref[:, cols] = _diff_finish(o, gain_ref[:, cols], lam_init)


def _attn_sample_call(lv, qb, kb, vb, kc, vc, cos, sin_signed, gain, *, lam_init):
    tq = ATT_TQ
    nq = DEC_SEQ // tq
    p0 = T_PROMPT // tq
    s0 = T_PROMPT // DEC_SEQ
    qblk = pl.BlockSpec((tq, W_B), lambda b, t: (p0 + b * nq + t, 0))
    kvblk = pl.BlockSpec((DEC_SEQ, W_B), lambda b, t: (s0 + b, 0))
    cblk = pl.BlockSpec((None, PAST_LEN, W_B), lambda b, t: (b, 0, 0))
    return pl.pallas_call(
        functools.partial(_attn_sample_kernel, lam_init=lam_init),
        grid=(DEC_BATCH, nq),
        in_specs=[pl.BlockSpec((4, DH_B), lambda b, t: (0, 0)), qblk, kvblk, kvblk, cblk, cblk,
                  pl.BlockSpec((tq, DV_B), lambda b, t: (t, 0)),
                  pl.BlockSpec((tq, DV_B), lambda b, t: (t, 0)),
                  pl.BlockSpec((DEC_SEQ, DV_B), lambda b, t: (0, 0)),
                  pl.BlockSpec((DEC_SEQ, DV_B), lambda b, t: (0, 0)),
                  pl.BlockSpec((1, W_B), lambda b, t: (0, 0))],
        out_specs=pl.BlockSpec((tq, W_B), lambda b, t: (b * nq + t, 0)),
        out_shape=jax.ShapeDtypeStruct((T_SAMPLE, W_B), BF16),
        scratch_shapes=[pltpu.VMEM((DEC_SEQ, W_B), BF16), pltpu.VMEM((DEC_SEQ, W_B), BF16)],
        compiler_params=_cparams(("arbitrary", "arbitrary")),
        name="diff_attn_sample",
    )(lv, qb, kb, vb, kc, vc, cos, sin_signed, cos, sin_signed, gain)


def _rope_tables():
    rows = DEC_SEQ // GRID_W
    row = jnp.repeat(jnp.arange(rows, dtype=F32), GRID_W)
    col = jnp.tile(jnp.arange(GRID_W, dtype=F32), rows)
    freqs = ROPE_THETA ** (-jnp.arange(AXIS_PAIRS, dtype=F32) / AXIS_PAIRS)
    ar, ac = row[:, None] * freqs, col[:, None] * freqs
    cos64 = jnp.concatenate([jnp.cos(ar), jnp.cos(ar), jnp.cos(ac), jnp.cos(ac)], axis=-1)
    sin64 = jnp.concatenate([-jnp.sin(ar), jnp.sin(ar), -jnp.sin(ac), jnp.sin(ac)], axis=-1)
    return jnp.tile(cos64, (1, 2)), jnp.tile(sin64, (1, 2))


SGU_TM = 512


def _group_mean(x, ones64):
    hi, lo = _split_bf16(x)
    return (_dot(hi, ones64) + _dot(lo, ones64)) * (1.0 / DG_C)


def _sgu_kernel(uv_ref, ws_ref, bs_ref, o_ref):
    r = lax.broadcasted_iota(jnp.int32, (W_C, W_C), 0)
    s = lax.broadcasted_iota(jnp.int32, (W_C, W_C), 1)
    ones64 = jnp.where((r // DG_C) == (s // DG_C), 1.0, 0.0).astype(BF16)
    lane = lax.broadcasted_iota(jnp.int32, (SGU_CHUNK, W_C), 1)
    for n in range(SGU_TM // SGU_CHUNK):
        rows = slice(n * SGU_CHUNK, (n + 1) * SGU_CHUNK)
        u = _gelu_tanh(uv_ref[rows, 0:256])
        v = _gelu_tanh(uv_ref[rows, 256:512])
        d = v - _group_mean(v, ones64)
        vn = d * lax.rsqrt(_group_mean(d * d, ones64) + EPS)
        s_mix = bs_ref[...]
        for g in range(N_GROUPS_C):
            vn_g = jnp.where((lane // DG_C) == g, vn, 0.0).astype(BF16)
            s_mix = s_mix + _dot(ws_ref[g], vn_g)
        o_ref[rows, :] = (u * s_mix).astype(BF16)


def _sgu_call(uv, ws, bs_full):
    n = T_ALL // SGU_TM
    return pl.pallas_call(
        _sgu_kernel,
        grid=(n,),
        in_specs=[pl.BlockSpec((SGU_TM, 512), lambda i: (i, 0)),
                  pl.BlockSpec(ws.shape, lambda i: (0, 0, 0)),
                  pl.BlockSpec(bs_full.shape, lambda i: (0, 0))],
        out_specs=pl.BlockSpec((SGU_TM, W_C), lambda i: (i, 0)),
        out_shape=jax.ShapeDtypeStruct((T_ALL, W_C), BF16),
        compiler_params=_cparams(("arbitrary",)),
        name="sgu",
    )(uv, ws, bs_full)


OUT_TM = 512


SEL_LANE0 = N_EXPERTS


def _top2_gates(logits):
    lane = lax.broadcasted_iota(jnp.int32, logits.shape, 1).astype(F32)
    neg = -jnp.inf
    lg = jnp.where(lane < N_EXPERTS, logits, neg)
    m1 = jnp.max(lg, axis=-1, keepdims=True)
    i1 = jnp.min(jnp.where(lg == m1, lane, float(LANES)), axis=-1, keepdims=True)
    lg2 = jnp.where(lane == i1, neg, lg)
    m2 = jnp.max(lg2, axis=-1, keepdims=True)
    i2 = jnp.min(jnp.where(lg2 == m2, lane, float(LANES)), axis=-1, keepdims=True)
    e2 = jnp.exp(m2 - m1)
    den = 1.0 + e2
    gates = jnp.where(lane == i1, 1.0 / den, 0.0) + jnp.where(lane == i2, e2 / den, 0.0)
    sel = jnp.where((lane == i1 + SEL_LANE0) | (lane == i2 + SEL_LANE0), 1.0, 0.0)
    return gates + sel


def _store_token_tiles(ref, val):
    n = val.shape[0]
    for k in range(D_MODEL // LANES):
        ref[pl.ds(k, n, stride=8), :] = val[:, k * LANES:(k + 1) * LANES]


def _load_token_tiles(ref, n):
    return jnp.concatenate([ref[pl.ds(k, n, stride=8), :] for k in range(D_MODEL // LANES)], axis=-1)


def _out_proj_kernel(*refs, n_x, moe):
    oa_refs, ob_refs, (oc_ref,), x_refs = refs[0:2], refs[2:4], refs[4:5], refs[5:5 + n_x]
    w_ref, g1_ref, nrm_ref, sc_ref, sh_ref, *rest = refs[5 + n_x:]
    if moe:
        rw_ref, x1_ref, h2t_ref, gates_ref, gates_t_ref, wo_ref = rest
    else:
        x1_ref, h2_ref, wo_ref = rest

    @pl.when(pl.program_id(0) == 0)
    def _():
        wo_ref[...] = w_ref[...].astype(BF16)

    y = (_dot(_pick_piece(oa_refs, OUT_TM), wo_ref[0:256, :]) + _dot(_pick_piece(ob_refs, OUT_TM), wo_ref[256:768, :])
         + _dot(oc_ref[...], wo_ref[768:1024, :]))
    x1 = _pick_piece(x_refs, OUT_TM) + g1_ref[...] * y
    x1_ref[...] = x1
    yn = x1 * lax.rsqrt(jnp.mean(x1 * x1, axis=-1, keepdims=True) + EPS)
    h = (yn * nrm_ref[...]) * (1.0 + sc_ref[...]) + sh_ref[...]
    if moe:
        _store_token_tiles(h2t_ref, h)
        gates = _top2_gates(_dot3(h, rw_ref[...]))
        gates_ref[...] = gates
        gates_t_ref[...] = gates.T
    else:
        h2_ref[...] = h.astype(BF16)


def _out_proj_call(oa_pieces, ob_pieces, oc, x_pieces, w_out, l, mod4, norm_g, router_w=None):
    tm = OUT_TM
    n = T_ALL // tm
    moe = router_w is not None
    row = functools.partial(_mod_row, tm=tm)
    mod_spec = lambda k: pl.BlockSpec((None, None, 1, D_MODEL), lambda i: (row(i), k, 0, 0))
    tok = lambda w: pl.BlockSpec((tm, w), lambda i: (i, 0))
    full = lambda a: pl.BlockSpec(a.shape, lambda i: (0,) * a.ndim)
    in_specs = (_piece_specs(oa_pieces, tm, W_A) + _piece_specs(ob_pieces, tm, W_B) + [tok(W_C)]
                + _piece_specs(x_pieces, tm, D_MODEL)
                + [_resident_layer(w_out, l), mod_spec(2), full(norm_g), mod_spec(4), mod_spec(3)])
    args = [*oa_pieces, *ob_pieces, oc, *x_pieces, w_out, mod4, norm_g, mod4, mod4]
    if moe:
        in_specs.append(full(router_w))
        args.append(router_w)
        out_specs = [tok(D_MODEL), pl.BlockSpec((tm * 8, LANES), lambda i: (i, 0)), tok(LANES),
                     pl.BlockSpec((LANES, tm), lambda i: (0, i))]
        out_shape = [jax.ShapeDtypeStruct((T_ALL, D_MODEL), F32), jax.ShapeDtypeStruct((T_ALL * 8, LANES), F32),
                     jax.ShapeDtypeStruct((T_ALL, LANES), F32), jax.ShapeDtypeStruct((LANES, T_ALL), F32)]
    else:
        out_specs = [tok(D_MODEL), tok(D_MODEL)]
        out_shape = [jax.ShapeDtypeStruct((T_ALL, D_MODEL), F32), jax.ShapeDtypeStruct((T_ALL, D_MODEL), BF16)]
    return pl.pallas_call(
        functools.partial(_out_proj_kernel, n_x=len(x_pieces), moe=moe),
        grid=(n,),
        in_specs=in_specs,
        out_specs=out_specs,
        out_shape=out_shape,
        scratch_shapes=[pltpu.VMEM((D_MODEL, D_MODEL), BF16)],
        compiler_params=_cparams(("arbitrary",)),
        name="out_proj_moe" if moe else "out_proj",
    )(*args)


FFN_TM = 512
FFN_FC = 1408
FFN_NC = D_FF // FFN_FC


def _swiglu_chunk(h, wg_ref, wu_ref, wd_ref):
    act = _silu(_dot(h, wg_ref[...])) * _dot(h, wu_ref[...])
    return _dot(act.astype(BF16), wd_ref[...])


def _ffn_kernel(h_ref, x_ref, g2_ref, wg_ref, wu_ref, wd_ref, o_ref, acc_ref):
    j = pl.program_id(1)

    @pl.when(j == 0)
    def _():
        acc_ref[...] = jnp.zeros_like(acc_ref)

    acc_ref[...] += _swiglu_chunk(h_ref[...], wg_ref, wu_ref, wd_ref)

    @pl.when(j == FFN_NC - 1)
    def _():
        o_ref[...] = x_ref[...] + g2_ref[...] * acc_ref[...]


def _ffn_call(h2, x1, mod4, wg, wu, wd):
    tm, fc = FFN_TM, FFN_FC
    row = functools.partial(_mod_row, tm=tm)
    tok = lambda w: pl.BlockSpec((tm, w), lambda i, j: (i, 0))
    return pl.pallas_call(
        _ffn_kernel,
        grid=(T_ALL // tm, FFN_NC),
        in_specs=[tok(D_MODEL), tok(D_MODEL),
                  pl.BlockSpec((None, None, 1, D_MODEL), lambda i, j: (row(i), 5, 0, 0)),
                  pl.BlockSpec((D_MODEL, fc), lambda i, j: (0, j)),
                  pl.BlockSpec((D_MODEL, fc), lambda i, j: (0, j)),
                  pl.BlockSpec((fc, D_MODEL), lambda i, j: (j, 0))],
        out_specs=tok(D_MODEL),
        out_shape=jax.ShapeDtypeStruct((T_ALL, D_MODEL), F32),
        scratch_shapes=[pltpu.VMEM((tm, D_MODEL), F32)],
        compiler_params=_cparams(("arbitrary", "arbitrary")),
        name="ffn_dense",
    )(h2, x1, mod4, wg, wu, wd)


MOE_TM = 512
MOE_NT_MAX = (2 * T_ALL) // MOE_TM + N_EXPERTS
MOE_ROWS = MOE_NT_MAX * MOE_TM
PLAN_BLK = 512
MISC_LAST_START = 8
MISC_NT = 16


def _moe_plan_kernel(gt_ref, posa_ref, posb_ref, te_ref, ti_ref, misc_ref):
    tm = float(MOE_TM)
    sel = gt_ref[SEL_LANE0:SEL_LANE0 + N_EXPERTS, :]
    cnt = jnp.sum(sel, axis=1, keepdims=True)
    nt = jnp.floor((cnt + (tm - 1.0)) * (1.0 / tm))
    sub = lax.broadcasted_iota(jnp.int32, (N_EXPERTS, LANES), 0).astype(F32)
    lane = lax.broadcasted_iota(jnp.int32, (N_EXPERTS, LANES), 1).astype(F32)
    nt_b = jnp.broadcast_to(nt, (N_EXPERTS, LANES))
    nt_row = jnp.sum(jnp.where(sub == lane, nt_b, 0.0), axis=0, keepdims=True)
    toff = jnp.sum(jnp.where(lane < sub, jnp.broadcast_to(nt_row, (N_EXPERTS, LANES)), 0.0),
                   axis=1, keepdims=True)
    tend = toff + nt
    n_total = jnp.sum(nt, axis=0, keepdims=True)
    jc = jnp.minimum(lane, n_total - 1.0)
    te = jnp.sum(jnp.where(jc >= tend, 1.0, 0.0), axis=0, keepdims=True)
    te_ref[...] = te.astype(jnp.int32)
    ti_ref[...] = jc[0:1, :].astype(jnp.int32)
    last_start = (tend - 1.0) * tm
    ls_row = jnp.sum(jnp.where(sub + MISC_LAST_START == lane, jnp.broadcast_to(last_start, (N_EXPERTS, LANES)), 0.0),
                     axis=0, keepdims=True)
    nt_row2 = jnp.sum(jnp.where(sub + MISC_NT == lane, nt_b, 0.0), axis=0, keepdims=True)
    misc = jnp.where(lane[0:1, :] == 0.0, n_total, 0.0) + ls_row + nt_row2
    misc_ref[...] = misc.astype(jnp.int32)

    off = toff * tm
    r = lax.broadcasted_iota(jnp.int32, (PLAN_BLK, PLAN_BLK), 0)
    c = lax.broadcasted_iota(jnp.int32, (PLAN_BLK, PLAN_BLK), 1)
    upper = jnp.where(r <= c, 1.0, 0.0).astype(BF16)
    carry = jnp.zeros((N_EXPERTS, 1), F32)
    for blk in range(T_ALL // PLAN_BLK):
        cols = slice(blk * PLAN_BLK, (blk + 1) * PLAN_BLK)
        s = gt_ref[SEL_LANE0:SEL_LANE0 + N_EXPERTS, cols]
        rank = _dot(s.astype(BF16), upper) + carry
        pos = off + rank - 1.0
        posa_ref[:, cols] = jnp.min(jnp.where(s > 0.0, pos, 1e9), axis=0, keepdims=True).astype(jnp.int32)
        posb_ref[:, cols] = jnp.max(jnp.where(s > 0.0, pos, -1.0), axis=0, keepdims=True).astype(jnp.int32)
        carry = carry + jnp.sum(s, axis=1, keepdims=True)


def _moe_plan_call(gates_t):
    row = lambda w: jax.ShapeDtypeStruct((1, w), jnp.int32)
    full = lambda w: pl.BlockSpec((1, w), lambda: (0, 0))
    return pl.pallas_call(
        _moe_plan_kernel,
        in_specs=[pl.BlockSpec((LANES, T_ALL), lambda: (0, 0))],
        out_specs=[full(T_ALL), full(T_ALL), full(LANES), full(LANES), full(LANES)],
        out_shape=[row(T_ALL), row(T_ALL), row(LANES), row(LANES), row(LANES)],
        compiler_params=pltpu.CompilerParams(vmem_limit_bytes=VMEM_LIMIT),
        name="moe_plan",
    )(gates_t)


DMA_UNROLL = 8


def _row_tile(ref, row):
    return ref.at[pl.ds(pl.multiple_of(row * 8, 8), 8), :]


def _moe_scatter_kernel(misc_ref, posa_ref, posb_ref, h_ref, xs_ref, zero_ref, sem):
    tm = h_ref.shape[0] // 8

    @pl.when(pl.program_id(0) == 0)
    def _():
        zero_ref[...] = jnp.zeros_like(zero_ref)

        def zero_tile(first_row):
            start = pl.multiple_of(first_row * 8, 8)
            cp = pltpu.make_async_copy(zero_ref, xs_ref.at[pl.ds(start, MOE_TM * 8), :], sem.at[0])
            cp.start()
            cp.wait()

        for e in range(N_EXPERTS):
            @pl.when(misc_ref[0, MISC_NT + e] > 0)
            def _():
                zero_tile(misc_ref[0, MISC_LAST_START + e])

        def zero_tail(j, carry):
            zero_tile(j * MOE_TM)
            return carry

        lax.fori_loop(misc_ref[0, 0], MOE_NT_MAX, zero_tail, 0)

    def issue(r, carry):
        src = _row_tile(h_ref, r)
        pltpu.make_async_copy(src, _row_tile(xs_ref, posa_ref[0, r]), sem.at[0]).start(priority=0)
        pltpu.make_async_copy(src, _row_tile(xs_ref, posb_ref[0, r]), sem.at[1]).start(priority=1)
        return carry

    lax.fori_loop(0, tm, issue, 0, unroll=DMA_UNROLL)
    for k in range(2):
        pltpu.make_async_copy(h_ref, xs_ref.at[pl.ds(0, tm * 8), :], sem.at[k]).wait()


SCATTER_TM = 512


def _moe_scatter_call(misc, posa3, posb3, h2t):
    tm = SCATTER_TM
    smem_row = pl.BlockSpec((None, 1, tm), lambda i: (i, 0, 0), memory_space=pltpu.SMEM)
    return pl.pallas_call(
        _moe_scatter_kernel,
        grid=(T_ALL // tm,),
        in_specs=[pl.BlockSpec((1, LANES), lambda i: (0, 0), memory_space=pltpu.SMEM), smem_row, smem_row,
                  pl.BlockSpec((tm * 8, LANES), lambda i: (i, 0))],
        out_specs=pl.BlockSpec(memory_space=pl.ANY),
        out_shape=jax.ShapeDtypeStruct((MOE_ROWS * 8, LANES), F32),
        scratch_shapes=[pltpu.VMEM((MOE_TM * 8, LANES), F32), pltpu.SemaphoreType.DMA((2,))],
        compiler_params=_cparams(("arbitrary",)),
        name="moe_scatter",
    )(misc, posa3, posb3, h2t)


def _ffn_grouped_kernel(te_ref, ti_ref, misc_ref, x_ref, wg_ref, wu_ref, wd_ref, o_ref, xb_ref, acc_ref):
    j = pl.program_id(0)
    c = pl.program_id(1)

    @pl.when(j < misc_ref[0])
    def _():
        @pl.when(c == 0)
        def _():
            xb_ref[...] = _load_token_tiles(x_ref, MOE_TM).astype(BF16)
            acc_ref[...] = jnp.zeros_like(acc_ref)

        acc_ref[...] += _swiglu_chunk(xb_ref[...], wg_ref, wu_ref, wd_ref)

        @pl.when(c == FFN_NC - 1)
        def _():
            _store_token_tiles(o_ref, acc_ref[...])

    @pl.when((j >= misc_ref[0]) & (c == FFN_NC - 1))
    def _():
        o_ref[...] = jnp.zeros_like(o_ref)


def _ffn_grouped_call(te, ti, misc, xs, wg, wu, wd):
    fc = FFN_FC

    def chunk(j, c, misc):
        return jnp.where(j < misc[0], c, FFN_NC - 1)

    grid_spec = pltpu.PrefetchScalarGridSpec(
        num_scalar_prefetch=3,
        grid=(MOE_NT_MAX, FFN_NC),
        in_specs=[pl.BlockSpec((MOE_TM * 8, LANES), lambda j, c, te, ti, misc: (ti[j], 0)),
                  pl.BlockSpec((None, D_MODEL, fc), lambda j, c, te, ti, misc: (te[j], 0, chunk(j, c, misc))),
                  pl.BlockSpec((None, D_MODEL, fc), lambda j, c, te, ti, misc: (te[j], 0, chunk(j, c, misc))),
                  pl.BlockSpec((None, fc, D_MODEL), lambda j, c, te, ti, misc: (te[j], chunk(j, c, misc), 0))],
        out_specs=pl.BlockSpec((MOE_TM * 8, LANES), lambda j, c, te, ti, misc: (j, 0)),
        scratch_shapes=[pltpu.VMEM((MOE_TM, D_MODEL), BF16), pltpu.VMEM((MOE_TM, D_MODEL), F32)],
    )
    return pl.pallas_call(
        _ffn_grouped_kernel,
        grid_spec=grid_spec,
        out_shape=jax.ShapeDtypeStruct((MOE_ROWS * 8, LANES), F32),
        compiler_params=_cparams(("arbitrary", "arbitrary")),
        name="ffn_grouped",
    )(te, ti, misc, xs, wg, wu, wd)


COMBINE_TM = 256


def _moe_combine_kernel(posa_ref, posb_ref, posa_next_ref, posb_next_ref, ys_ref, x_ref, g2_ref, gates_ref, nf_ref,
                        op_ref, os_ref, bufa_ref, bufb_ref, sem):
    tm = COMBINE_TM
    i = pl.program_id(0)
    n = pl.num_programs(0)
    slot = i % 2

    def gather(pa_ref, pb_ref, s):
        def issue(r, carry):
            pltpu.make_async_copy(_row_tile(ys_ref, pa_ref[0, r]), _row_tile(bufa_ref.at[s], r),
                                  sem.at[s, 0]).start(priority=0)
            pltpu.make_async_copy(_row_tile(ys_ref, pb_ref[0, r]), _row_tile(bufb_ref.at[s], r),
                                  sem.at[s, 1]).start(priority=1)
            return carry

        lax.fori_loop(0, tm, issue, 0, unroll=DMA_UNROLL)

    @pl.when(i == 0)
    def _():
        gather(posa_ref, posb_ref, 0)

    @pl.when(i + 1 < n)
    def _():
        gather(posa_next_ref, posb_next_ref, 1 - slot)

    gates = gates_ref[...]
    lane = lax.broadcasted_iota(jnp.int32, gates.shape, 1).astype(F32)
    is_sel = (lane >= SEL_LANE0) & (lane < SEL_LANE0 + N_EXPERTS) & (gates > 0.0)
    ia = jnp.min(jnp.where(is_sel, lane, float(LANES)), axis=-1, keepdims=True) - SEL_LANE0
    ib = jnp.max(jnp.where(is_sel, lane, -1.0), axis=-1, keepdims=True) - SEL_LANE0
    wa = jnp.sum(jnp.where(lane == ia, gates, 0.0), axis=-1, keepdims=True)
    wb = jnp.sum(jnp.where(lane == ib, gates, 0.0), axis=-1, keepdims=True)

    pltpu.make_async_copy(ys_ref.at[pl.ds(0, tm * 8), :], bufa_ref.at[slot], sem.at[slot, 0]).wait()
    pltpu.make_async_copy(ys_ref.at[pl.ds(0, tm * 8), :], bufb_ref.at[slot], sem.at[slot, 1]).wait()

    y = wa * _load_token_tiles(bufa_ref.at[slot], tm) + wb * _load_token_tiles(bufb_ref.at[slot], tm)
    out = x_ref[...] + g2_ref[...] * y
    out = (out * lax.rsqrt(jnp.mean(out * out, axis=-1, keepdims=True) + EPS)) * nf_ref[...]

    @pl.when(i < T_PROMPT // tm)
    def _():
        op_ref[...] = out

    @pl.when(i >= T_PROMPT // tm)
    def _():
        os_ref[...] = out


def _moe_combine_call(posa3, posb3, ys, x1, mod4, gates, norm_f):
    tm = COMBINE_TM
    n = T_ALL // tm
    n_p = T_PROMPT // tm
    row = functools.partial(_mod_row, tm=tm)
    smem_row = pl.BlockSpec((None, 1, tm), lambda i: (i, 0, 0), memory_space=pltpu.SMEM)
    smem_next = pl.BlockSpec((None, 1, tm), lambda i: (jnp.minimum(i + 1, n - 1), 0, 0), memory_space=pltpu.SMEM)
    tok = lambda w: pl.BlockSpec((tm, w), lambda i: (i, 0))
    return pl.pallas_call(
        _moe_combine_kernel,
        grid=(n,),
        in_specs=[smem_row, smem_row, smem_next, smem_next, pl.BlockSpec(memory_space=pl.ANY), tok(D_MODEL),
                  pl.BlockSpec((None, None, 1, D_MODEL), lambda i: (row(i), 5, 0, 0)), tok(LANES),
                  pl.BlockSpec((1, D_MODEL), lambda i: (0, 0))],
        out_specs=[pl.BlockSpec((tm, D_MODEL), lambda i: (jnp.minimum(i, n_p - 1), 0)),
                   pl.BlockSpec((tm, D_MODEL), lambda i: (jnp.maximum(i - n_p, 0), 0))],
        out_shape=[jax.ShapeDtypeStruct((T_PROMPT, D_MODEL), F32), jax.ShapeDtypeStruct((T_SAMPLE, D_MODEL), F32)],
        scratch_shapes=[pltpu.VMEM((2, tm * 8, LANES), F32), pltpu.VMEM((2, tm * 8, LANES), F32),
                        pltpu.SemaphoreType.DMA((2, 2))],
        compiler_params=_cparams(("arbitrary",)),
        name="moe_combine",
    )(posa3, posb3, posa3, posb3, ys, x1, mod4, gates, norm_f)


def _moe_call(h2t, x1, mod4, gates, gates_t, wg, wu, wd, norm_f):
    posa, posb, te, ti, misc = _moe_plan_call(gates_t)
    xs = _moe_scatter_call(misc, posa.reshape(T_ALL // SCATTER_TM, 1, SCATTER_TM),
                           posb.reshape(T_ALL // SCATTER_TM, 1, SCATTER_TM), h2t)
    ys = _ffn_grouped_call(te.reshape(LANES), ti.reshape(LANES), misc.reshape(LANES), xs, wg, wu, wd)
    return _moe_combine_call(posa.reshape(T_ALL // COMBINE_TM, 1, COMBINE_TM),
                             posb.reshape(T_ALL // COMBINE_TM, 1, COMBINE_TM), ys, x1, mod4, gates, norm_f)


def kernel(x_prompt, x_sample, cache_k, cache_v, state_gla_fwd, state_gla_bwd, c, c_ctx, w_ada, b_ada, norm_mix, norm_ffn, w_in, w_out, gla_w_up, gla_b_up, gla_norm, diff_lambda, diff_norm, sgu_w, sgu_b, ffn_w_gate, ffn_w_up, ffn_w_down, router_w, moe_w_gate, moe_w_up, moe_w_down, norm_f):
    assert DEPTH == 2
    x_pieces = [x_prompt.reshape(T_PROMPT, D_MODEL), x_sample.reshape(T_SAMPLE, D_MODEL)]
    cvecs = jnp.concatenate([c_ctx[None, :], c, jnp.zeros((N_MOD_ROWS - 1 - DEC_BATCH, D_MODEL), F32)], axis=0)
    mod = _ada_call(cvecs, w_ada, b_ada)
    cos, sin_signed = _rope_tables()
    zeros_state = jnp.zeros((BATCH, 64, 256), F32)

    prev_kv, sfs, sbs = [], [], []
    for l in range(DEPTH):
        mod4 = mod[l].reshape(N_MOD_ROWS, 6, 1, D_MODEL)
        w_up = jnp.zeros((LANES, 2 * W_A), F32)
        w_up = w_up.at[0:GLA_RANK, 0:W_A].set(gla_w_up[l, 0]).at[GLA_RANK:2 * GLA_RANK, W_A:].set(gla_w_up[l, 1])
        b_up = gla_b_up[l].reshape(1, 2 * W_A)
        g4, la, qb, kb, vb, uv = _in_proj_call(x_pieces, norm_mix[l][None, :], mod4, w_in, l, w_up.astype(BF16), b_up)

        gain_a = gla_norm[l][None, :]
        oa_p, sf, sb = _gla_call(g4, la, zeros_state, zeros_state, gain_a,
                                 batch=BATCH, seq=SEQ, row_block0=0)
        oa_s, _, _ = _gla_call(g4, la, _state_to_kernel(state_gla_fwd[:, l]),
                               _state_to_kernel(state_gla_bwd[:, l]), gain_a,
                               batch=DEC_BATCH, seq=DEC_SEQ, row_block0=T_PROMPT // DEC_SEQ)

        lam_init = 0.8 - 0.6 * math.exp(-0.3 * l)
        gain_b = diff_norm[l][None, :]
        if l < DEPTH - 1:
            (ob_p,) = _attn_prompt_call(diff_lambda[l], qb, kb, vb, gain_b, lam_init=lam_init)
            prev_kv.append((kb, vb))
        else:
            ob_p, new_cache_k, new_cache_v = _attn_prompt_call(diff_lambda[l], qb, kb, vb, gain_b, prev_kv,
                                                               lam_init=lam_init, write_cache=True)
        ob_s = _attn_sample_call(diff_lambda[l], qb, kb, vb,
                                 cache_k[:, l].reshape(DEC_BATCH, PAST_LEN, W_B),
                                 cache_v[:, l].reshape(DEC_BATCH, PAST_LEN, W_B),
                                 cos, sin_signed, gain_b, lam_init=lam_init)

        bs_full = jnp.repeat(sgu_b[l].T, DG_C, axis=1)
        oc = _sgu_call(uv, sgu_w[l].astype(BF16), bs_full)

        if l == 0:
            x1, h2 = _out_proj_call([oa_p, oa_s], [ob_p, ob_s], oc, x_pieces, w_out, l, mod4, norm_ffn[l][None, :])
            x_pieces = [_ffn_call(h2, x1, mod4, ffn_w_gate[0].astype(BF16), ffn_w_up[0].astype(BF16),
                                  ffn_w_down[0].astype(BF16))]
        else:
            rw = jnp.pad(router_w[0], ((0, 0), (0, LANES - N_EXPERTS)))
            x1, h2t, gates, gates_t = _out_proj_call([oa_p, oa_s], [ob_p, ob_s], oc, x_pieces, w_out, l, mod4,
                                                     norm_ffn[l][None, :], rw)
            y_prompt, y_sample = _moe_call(h2t, x1, mod4, gates, gates_t, moe_w_gate[0].astype(BF16),
                                           moe_w_up[0].astype(BF16), moe_w_down[0].astype(BF16), norm_f[None, :])

        sfs.append(_state_from_kernel(sf))
        sbs.append(_state_from_kernel(sb))

    return (y_prompt.reshape(BATCH, SEQ, D_MODEL), y_sample.reshape(DEC_BATCH, DEC_SEQ, D_MODEL),
            new_cache_k, new_cache_v, jnp.stack(sfs, axis=1), jnp.stack(sbs, axis=1))
```

```python
import functools
import math

import jax
import jax.numpy as jnp
import numpy as np
from jax import lax
from jax.experimental import pallas as pl
from jax.experimental.pallas import tpu as pltpu

F32 = jnp.float32
BF16 = jnp.bfloat16

D_MODEL = 1024
BATCH = 32
SEQ = 256
DEPTH = 2
DEC_BATCH = 2
DEC_SEQ = 1024
PAST_LEN = 256
GRID_W = 64
N_HEADS_A = 4
DK_A = 64
W_A = 256
GLA_RANK = 16
GLA_TAU = 16.0
GLA_CHUNK = 64
N_HEADS_B = 4
DH_B = 64
DV_B = 128
W_B = 512
ROPE_THETA = 10000.0
AXIS_PAIRS = DH_B // 4
N_GROUPS_C = 4
DG_C = 64
W_C = 256
SGU_CHUNK = 128
D_FF = 2816
N_EXPERTS = 8
EPS = 1e-6

T_PROMPT = BATCH * SEQ
T_SAMPLE = DEC_BATCH * DEC_SEQ
T_ALL = T_PROMPT + T_SAMPLE
N_MOD_ROWS = 8
LANES = 128
VMEM_LIMIT = 56 * 1024 * 1024


def _cparams(sem):
    return pltpu.CompilerParams(dimension_semantics=sem, vmem_limit_bytes=VMEM_LIMIT)


def _dot(a, b):
    return jnp.dot(a, b, preferred_element_type=F32)


def _dot_nt(a, b):
    return lax.dot_general(a, b, (((1,), (1,)), ((), ())), preferred_element_type=F32)


def _dot_tn(a, b):
    return lax.dot_general(a, b, (((0,), (0,)), ((), ())), preferred_element_type=F32)


def _split_bf16(x):
    hi = x.astype(BF16)
    lo = (x - hi.astype(F32)).astype(BF16)
    return hi, lo


def _dot3(a, w):
    a_hi, a_lo = _split_bf16(a)
    w_hi, w_lo = _split_bf16(w)
    return _dot(a_hi, w_hi) + (_dot(a_lo, w_hi) + _dot(a_hi, w_lo))


def _sigmoid(x):
    return 1.0 / (1.0 + jnp.exp(-x))


def _silu(x):
    return x * _sigmoid(x)


def _gelu_tanh(x):
    c = math.sqrt(2.0 / math.pi)
    return x * (0.5 * (1.0 + jnp.tanh(c * (x + 0.044715 * (x * x * x)))))


def _log_sigmoid(x):
    return jnp.minimum(x, 0.0) - jnp.log(1.0 + jnp.exp(-jnp.abs(x)))


def _mod_row(i, tm):
    n_p = T_PROMPT // tm
    per_b = DEC_SEQ // tm
    return jnp.where(i < n_p, 0, 1 + (i - n_p) // per_b)


ADA_TN = 1536


def _ada_kernel(c_ref, w_ref, b_ref, o_ref):
    a = _silu(c_ref[...])
    o_ref[...] = _dot3(a, w_ref[...]) + b_ref[...]


def _ada_call(cvecs, w_ada, b_ada):
    n_col = (6 * D_MODEL) // ADA_TN
    return pl.pallas_call(
        _ada_kernel,
        grid=(DEPTH, n_col),
        in_specs=[
            pl.BlockSpec((N_MOD_ROWS, D_MODEL), lambda l, j: (0, 0)),
            pl.BlockSpec((None, D_MODEL, ADA_TN), lambda l, j: (l, 0, j)),
            pl.BlockSpec((None, 1, ADA_TN), lambda l, j: (l, 0, j)),
        ],
        out_specs=pl.BlockSpec((None, N_MOD_ROWS, ADA_TN), lambda l, j: (l, 0, j)),
        out_shape=jax.ShapeDtypeStruct((DEPTH, N_MOD_ROWS, 6 * D_MODEL), F32),
        compiler_params=_cparams(("arbitrary", "arbitrary")),
        name="ada_mod",
    )(cvecs, w_ada, b_ada.reshape(DEPTH, 1, 6 * D_MODEL))


IN_TM = 512
IN_COLS = 3104
Z_COL0 = 1024
Z_COLS = 2 * GLA_RANK
W_MAIN = 3072


def _piece_specs(pieces, tm, width):
    specs, t0 = [], 0
    for arr in pieces:
        nt = arr.shape[0] // tm
        specs.append(pl.BlockSpec((tm, width), lambda i, t0=t0, nt=nt: (jnp.clip(i - t0, 0, nt - 1), 0)))
        t0 += nt
    assert t0 * tm == T_ALL and len(pieces) in (1, 2) and (len(pieces) == 1 or pieces[0].shape[0] == T_PROMPT)
    return specs


def _pick_piece(refs, tm):
    if len(refs) == 1:
        return refs[0][...]
    return jnp.where(pl.program_id(0) < T_PROMPT // tm, refs[0][...], refs[1][...])


def _in_proj_kernel(*refs, n_x):
    x_refs = refs[:n_x]
    (nrm_ref, sh_ref, sc_ref, w_ref, wup_ref, bup_ref,
     g4_ref, la_ref, qb_ref, kb_ref, vb_ref, uv_ref, wm_ref, wz_ref) = refs[n_x:]

    @pl.when(pl.program_id(0) == 0)
    def _():
        wm_ref[0:Z_COL0, :] = w_ref[0:Z_COL0, :].astype(BF16)
        wm_ref[Z_COL0:W_MAIN, :] = w_ref[Z_COL0 + Z_COLS:IN_COLS, :].astype(BF16)
        wz_ref[...] = jnp.zeros_like(wz_ref)
        wz_ref[0:Z_COLS, :] = w_ref[Z_COL0:Z_COL0 + Z_COLS, :].astype(BF16)

    x = _pick_piece(x_refs, IN_TM)
    y = x * lax.rsqrt(jnp.mean(x * x, axis=-1, keepdims=True) + EPS)
    h = (y * nrm_ref[...]) * (1.0 + sc_ref[...]) + sh_ref[...]
    hb = h.astype(BF16)
    g4_ref[...] = _dot_nt(hb, wm_ref[0:1024, :])
    qb_ref[...] = _dot_nt(hb, wm_ref[1024:1536, :])
    kb_ref[...] = _dot_nt(hb, wm_ref[1536:2048, :])
    vb_ref[...] = _dot_nt(hb, wm_ref[2048:2560, :])
    uv_ref[...] = _dot_nt(hb, wm_ref[2560:3072, :])
    z = _dot_nt(hb, wz_ref[...])
    zz = _dot(z.astype(BF16), wup_ref[...]) + bup_ref[...]
    la_ref[...] = _log_sigmoid(zz) * (1.0 / GLA_TAU)


def _resident_layer(a, l):
    return pl.BlockSpec((None,) + a.shape[1:], lambda *_: (l,) + (0,) * (a.ndim - 1), pipeline_mode=pl.Buffered(1))


def _in_proj_call(x_pieces, norm_g, mod4, w_in_t, l, w_up, b_up):
    tm = IN_TM
    n = T_ALL // tm
    row = functools.partial(_mod_row, tm=tm)
    mod_spec = lambda k: pl.BlockSpec((None, None, 1, D_MODEL), lambda i: (row(i), k, 0, 0))
    full = lambda a: pl.BlockSpec(a.shape, lambda i: (0,) * a.ndim)
    out = lambda w: pl.BlockSpec((tm, w), lambda i: (i, 0))
    return pl.pallas_call(
        functools.partial(_in_proj_kernel, n_x=len(x_pieces)),
        grid=(n,),
        in_specs=_piece_specs(x_pieces, tm, D_MODEL) + [full(norm_g), mod_spec(0), mod_spec(1), _resident_layer(w_in_t, l),
                                                        full(w_up), full(b_up)],
        out_specs=[out(1024), out(512), out(512), out(512), out(512), out(512)],
        out_shape=[jax.ShapeDtypeStruct((T_ALL, w), F32) for w in (1024, 512, 512, 512, 512, 512)],
        scratch_shapes=[pltpu.VMEM((W_MAIN, D_MODEL), BF16), pltpu.VMEM((LANES, D_MODEL), BF16)],
        compiler_params=_cparams(("arbitrary",)),
        name="in_proj",
    )(*x_pieces, norm_g, mod4, mod4, w_in_t, w_up, b_up)


GLA_SB = 256
GLA_NC = GLA_SB // GLA_CHUNK


def _gla_superblock(q, k, vb, v_heads, la, st_all, tri, mask, same64, head_lanes, forward):
    c = GLA_CHUNK
    mid, last = (c // 2 - 1, c - 1) if forward else (c // 2, 0)
    la_hi, la_lo = _split_bf16(la)
    b = _dot(tri, la_hi) + _dot(tri, la_lo)
    rows_of = lambda r: jnp.concatenate(
        [jnp.broadcast_to(b[i * c + r:i * c + r + 1, :], (c, W_A)) for i in range(GLA_NC)], axis=0)
    m = rows_of(mid)
    bl = rows_of(last)
    qe = (q * jnp.exp(b - m)).astype(BF16)
    ke = k * jnp.exp(m - b)
    qi = (q * jnp.exp(b)).astype(BF16)
    ks = (k * jnp.exp(bl - b)).astype(BF16)
    o = jnp.zeros((GLA_SB, W_A), F32)
    for h in range(N_HEADS_A):
        ke_h = jnp.where(head_lanes[h], ke, 0.0).astype(BF16)
        a = jnp.where(mask, _dot_nt(qe, ke_h), 0.0).astype(BF16)
        o = o + _dot(a, v_heads[h])
    outs = [None] * GLA_NC
    for i in (range(GLA_NC) if forward else reversed(range(GLA_NC))):
        rows = slice(i * c, (i + 1) * c)
        outs[i] = o[rows, :] + _dot_nt(qi[rows, :], st_all.astype(BF16))
        kv = _dot_tn(vb[rows, :], ks[rows, :])
        st_all = st_all * jnp.exp(bl[i * c:i * c + 1, :]) + jnp.where(same64, kv, 0.0)
    return jnp.concatenate(outs, axis=0), st_all


def _gla_kernel(g4_ref, la_ref, s0f_ref, s0b_ref, gain_ref, o_ref, sf_ref, sb_ref, of_ref, ob_ref, *, seq):
    n = GLA_SB
    nsb = seq // n
    r = lax.broadcasted_iota(jnp.int32, (n, n), 0)
    s = lax.broadcasted_iota(jnp.int32, (n, n), 1)
    same64 = (r // GLA_CHUNK) == (s // GLA_CHUNK)
    lower = same64 & (s <= r)
    upper = same64 & (s >= r)
    tri_f = jnp.where(lower, 1.0, 0.0).astype(BF16)
    tri_b = jnp.where(upper, 1.0, 0.0).astype(BF16)
    ones64 = jnp.where(same64, 1.0, 0.0).astype(BF16)
    head_lanes = [(s // DK_A) == h for h in range(N_HEADS_A)]
    scale = DK_A ** -0.5
    expand = lambda st: jnp.where(same64, jnp.concatenate([st] * N_HEADS_A, axis=0), 0.0)
    compact = lambda st_all: functools.reduce(
        lambda a, b: a + b, [st_all[h * 64:(h + 1) * 64, :] for h in range(N_HEADS_A)])

    def step(i, carry):
        stf, stb = carry
        rf = pl.ds(pl.multiple_of(i * n, n), n)
        rb = pl.ds(pl.multiple_of((nsb - 1 - i) * n, n), n)
        for rows, forward in ((rf, True), (rb, False)):
            q = g4_ref[rows, 0:256] * scale
            k = g4_ref[rows, 256:512]
            v = g4_ref[rows, 512:768]
            vb = v.astype(BF16)
            v_heads = [jnp.where(head_lanes[h], v, 0.0).astype(BF16) for h in range(N_HEADS_A)]
            if forward:
                o, stf = _gla_superblock(q, k, vb, v_heads, la_ref[rows, 0:256], stf, tri_f, lower, same64,
                                         head_lanes, True)
                of_ref[rows, :] = o
            else:
                o, stb = _gla_superblock(q, k, vb, v_heads, la_ref[rows, 256:512], stb, tri_b, upper, same64,
                                         head_lanes, False)
                ob_ref[rows, :] = o
        return stf, stb

    stf, stb = lax.fori_loop(0, nsb, step, (expand(s0f_ref[...]), expand(s0b_ref[...])))
    sf_ref[...] = compact(stf)
    sb_ref[...] = compact(stb)

    gain = gain_ref[...]

    def finish(i, carry):
        rows = pl.ds(pl.multiple_of(i * n, n), n)
        o = of_ref[rows, :] + ob_ref[rows, :]
        sq_hi, sq_lo = _split_bf16(o * o)
        ms = (_dot(sq_hi, ones64) + _dot(sq_lo, ones64)) * (1.0 / DK_A)
        y = (o * lax.rsqrt(ms + EPS)) * gain
        o_ref[rows, :] = (y * _silu(g4_ref[rows, 768:1024])).astype(BF16)
        return carry

    lax.fori_loop(0, nsb, finish, 0)


def _gla_call(g4, la, s0f, s0b, gain, *, batch, seq, row_block0):
    tok = lambda w: pl.BlockSpec((seq, w), lambda b: (row_block0 + b, 0))
    st = pl.BlockSpec((None, 64, 256), lambda b: (b, 0, 0))
    return pl.pallas_call(
        functools.partial(_gla_kernel, seq=seq),
        grid=(batch,),
        in_specs=[tok(1024), tok(512), st, st, pl.BlockSpec((1, W_A), lambda b: (0, 0))],
        out_specs=[pl.BlockSpec((seq, W_A), lambda b: (b, 0)), st, st],
        out_shape=[jax.ShapeDtypeStruct((batch * seq, W_A), BF16),
                   jax.ShapeDtypeStruct((batch, 64, 256), F32),
                   jax.ShapeDtypeStruct((batch, 64, 256), F32)],
        scratch_shapes=[pltpu.VMEM((seq, W_A), F32), pltpu.VMEM((seq, W_A), F32)],
        compiler_params=_cparams(("arbitrary",)),
        name=f"gla_{seq}",
    )(g4, la, s0f, s0b, gain)


def _state_to_kernel(s):
    b = s.shape[0]
    return jnp.transpose(s, (0, 3, 1, 2)).reshape(b, 64, 256)


def _state_from_kernel(st):
    b = st.shape[0]
    return jnp.transpose(st.reshape(b, 64, N_HEADS_A, DK_A), (0, 2, 3, 1))


def _lambda(lv, lam_init):
    l01 = jnp.sum(lv[0:1, :] * lv[1:2, :], axis=-1, keepdims=True)
    l23 = jnp.sum(lv[2:3, :] * lv[3:4, :], axis=-1, keepdims=True)
    return jnp.exp(l01) - jnp.exp(l23) + lam_init


def _softmax_parts(parts):
    mx = functools.reduce(jnp.maximum, [jnp.max(p, axis=-1, keepdims=True) for p in parts])
    es = [jnp.exp(p - mx) for p in parts]
    den = functools.reduce(lambda a, b: a + b, [jnp.sum(e, axis=-1, keepdims=True) for e in es])
    return [e / den for e in es]


def _diff_finish(o, gain, lam_init):
    o = o * lax.rsqrt(jnp.mean(o * o, axis=-1, keepdims=True) + EPS)
    return ((o * gain) * (1.0 - lam_init)).astype(BF16)


QK_SCALE = DH_B ** -0.5


def _key_halves(k):
    first = lax.broadcasted_iota(jnp.int32, k.shape, 1) < DH_B
    return jnp.where(first, k, 0.0).astype(BF16), jnp.where(first, 0.0, k).astype(BF16)


def _attn_prompt_kernel(lv_ref, q_ref, k_ref, v_ref, gain_ref, *rest, lam_init, n_prev):
    prev_refs, (o_ref, *cache_refs) = rest[:2 * n_prev], rest[2 * n_prev:]
    lam = _lambda(lv_ref[...], lam_init)
    for h in range(N_HEADS_B):
        cols = slice(h * DV_B, (h + 1) * DV_B)
        q = (q_ref[:, cols] * QK_SCALE).astype(BF16)
        k1, k2 = _key_halves(k_ref[:, cols])
        (p1,) = _softmax_parts([_dot_nt(q, k1)])
        (p2,) = _softmax_parts([_dot_nt(q, k2)])
        a = p1 - lam * p2
        o = _dot(a.astype(BF16), v_ref[:, cols].astype(BF16))
        o_ref[:, cols] = _diff_finish(o, gain_ref[:, cols], lam_init)
        if cache_refs:
            ck_ref, cv_ref = cache_refs
            layers_k = [*prev_refs[0::2], k_ref]
            layers_v = [*prev_refs[1::2], v_ref]
            for l in range(n_prev + 1):
                ck_ref[l, :, h, :] = layers_k[l][:, cols]
                cv_ref[l, :, h, :] = layers_v[l][:, cols]


def _attn_prompt_call(lv, qb, kb, vb, gain, prev_kv=(), *, lam_init, write_cache=False):
    blk = pl.BlockSpec((SEQ, W_B), lambda b: (b, 0))
    n_prev = len(prev_kv)
    out_specs = [blk]
    out_shape = [jax.ShapeDtypeStruct((T_PROMPT, W_B), BF16)]
    if write_cache:
        cache = jax.ShapeDtypeStruct((BATCH, n_prev + 1, SEQ, N_HEADS_B, DV_B), F32)
        cblk = pl.BlockSpec((None, n_prev + 1, SEQ, N_HEADS_B, DV_B), lambda b: (b, 0, 0, 0, 0))
        out_specs += [cblk, cblk]
        out_shape += [cache, cache]
    return pl.pallas_call(
        functools.partial(_attn_prompt_kernel, lam_init=lam_init, n_prev=n_prev),
        grid=(BATCH,),
        in_specs=[pl.BlockSpec((4, DH_B), lambda b: (0, 0)), blk, blk, blk,
                  pl.BlockSpec((1, W_B), lambda b: (0, 0))] + [blk] * (2 * n_prev),
        out_specs=out_specs,
        out_shape=out_shape,
        compiler_params=_cparams(("arbitrary",)),
        name="diff_attn_prompt",
    )(lv, qb, kb, vb, gain, *[a for kv in prev_kv for a in kv])


def _rope(x, cos, sin_signed):
    lane = lax.broadcasted_iota(jnp.int32, x.shape, 1)
    first = (lane % (2 * AXIS_PAIRS)) < AXIS_PAIRS
    partner = jnp.where(first, pltpu.roll(x, LANES - AXIS_PAIRS, 1), pltpu.roll(x, AXIS_PAIRS, 1))
    return x * cos + partner * sin_signed


ATT_TQ = 256


def _attn_sample_kernel(lv_ref, q_ref, k_ref, v_ref, kc_ref, vc_ref, cosq_ref, sinq_ref,
                        cosk_ref, sink_ref, gain_ref, o_ref, k1_ref, k2_ref, *, lam_init):
    @pl.when(pl.program_id(1) == 0)
    def _():
        for h in range(N_HEADS_B):
            cols = slice(h * DV_B, (h + 1) * DV_B)
            k1_ref[:, cols], k2_ref[:, cols] = _key_halves(_rope(k_ref[:, cols], cosk_ref[...], sink_ref[...]))

    lam = _lambda(lv_ref[...], lam_init)
    for h in range(N_HEADS_B):
        cols = slice(h * DV_B, (h + 1) * DV_B)
        q = (_rope(q_ref[:, cols], cosq_ref[...], sinq_ref[...]) * QK_SCALE).astype(BF16)
        c1, c2 = _key_halves(kc_ref[:, cols])
        p1 = _softmax_parts([_dot_nt(q, k1_ref[:, cols]), _dot_nt(q, c1)])
        p2 = _softmax_parts([_dot_nt(q, k2_ref[:, cols]), _dot_nt(q, c2)])
        a_own = p1[0] - lam * p2[0]
        a_ctx = p1[1] - lam * p2[1]
        o = (_dot(a_own.astype(BF16), v_ref[:, cols].astype(BF16))
             + _dot(a_ctx.astype(BF16), vc_ref[:, cols].astype(BF16)))
        o_ref[:, cols] = _diff_finish(o, gain_ref[:, cols], lam_init)


def _attn_sample_call(lv, qb, kb, vb, kc, vc, cos, sin_signed, gain, *, lam_init):
    tq = ATT_TQ
    nq = DEC_SEQ // tq
    p0 = T_PROMPT // tq
    s0 = T_PROMPT // DEC_SEQ
    qblk = pl.BlockSpec((tq, W_B), lambda b, t: (p0 + b * nq + t, 0))
    kvblk = pl.BlockSpec((DEC_SEQ, W_B), lambda b, t: (s0 + b, 0))
    cblk = pl.BlockSpec((None, PAST_LEN, W_B), lambda b, t: (b, 0, 0))
    return pl.pallas_call(
        functools.partial(_attn_sample_kernel, lam_init=lam_init),
        grid=(DEC_BATCH, nq),
        in_specs=[pl.BlockSpec((4, DH_B), lambda b, t: (0, 0)), qblk, kvblk, kvblk, cblk, cblk,
                  pl.BlockSpec((tq, DV_B), lambda b, t: (t, 0)),
                  pl.BlockSpec((tq, DV_B), lambda b, t: (t, 0)),
                  pl.BlockSpec((DEC_SEQ, DV_B), lambda b, t: (0, 0)),
                  pl.BlockSpec((DEC_SEQ, DV_B), lambda b, t: (0, 0)),
                  pl.BlockSpec((1, W_B), lambda b, t: (0, 0))],
        out_specs=pl.BlockSpec((tq, W_B), lambda b, t: (b * nq + t, 0)),
        out_shape=jax.ShapeDtypeStruct((T_SAMPLE, W_B), BF16),
        scratch_shapes=[pltpu.VMEM((DEC_SEQ, W_B), BF16), pltpu.VMEM((DEC_SEQ, W_B), BF16)],
        compiler_params=_cparams(("arbitrary", "arbitrary")),
        name="diff_attn_sample",
    )(lv, qb, kb, vb, kc, vc, cos, sin_signed, cos, sin_signed, gain)


def _rope_tables():
    rows = DEC_SEQ // GRID_W
    row = jnp.repeat(jnp.arange(rows, dtype=F32), GRID_W)
    col = jnp.tile(jnp.arange(GRID_W, dtype=F32), rows)
    freqs = ROPE_THETA ** (-jnp.arange(AXIS_PAIRS, dtype=F32) / AXIS_PAIRS)
    ar, ac = row[:, None] * freqs, col[:, None] * freqs
    cos64 = jnp.concatenate([jnp.cos(ar), jnp.cos(ar), jnp.cos(ac), jnp.cos(ac)], axis=-1)
    sin64 = jnp.concatenate([-jnp.sin(ar), jnp.sin(ar), -jnp.sin(ac), jnp.sin(ac)], axis=-1)
    return jnp.tile(cos64, (1, 2)), jnp.tile(sin64, (1, 2))


SGU_TM = 512


def _group_mean(x, ones64):
    hi, lo = _split_bf16(x)
    return (_dot(hi, ones64) + _dot(lo, ones64)) * (1.0 / DG_C)


def _sgu_kernel(uv_ref, ws_ref, bs_ref, o_ref):
    r = lax.broadcasted_iota(jnp.int32, (W_C, W_C), 0)
    s = lax.broadcasted_iota(jnp.int32, (W_C, W_C), 1)
    ones64 = jnp.where((r // DG_C) == (s // DG_C), 1.0, 0.0).astype(BF16)
    lane = lax.broadcasted_iota(jnp.int32, (SGU_CHUNK, W_C), 1)
    for n in range(SGU_TM // SGU_CHUNK):
        rows = slice(n * SGU_CHUNK, (n + 1) * SGU_CHUNK)
        u = _gelu_tanh(uv_ref[rows, 0:256])
        v = _gelu_tanh(uv_ref[rows, 256:512])
        d = v - _group_mean(v, ones64)
        vn = d * lax.rsqrt(_group_mean(d * d, ones64) + EPS)
        s_mix = bs_ref[...]
        for g in range(N_GROUPS_C):
            vn_g = jnp.where((lane // DG_C) == g, vn, 0.0).astype(BF16)
            s_mix = s_mix + _dot(ws_ref[g], vn_g)
        o_ref[rows, :] = (u * s_mix).astype(BF16)


def _sgu_call(uv, ws, bs_full):
    n = T_ALL // SGU_TM
    return pl.pallas_call(
        _sgu_kernel,
        grid=(n,),
        in_specs=[pl.BlockSpec((SGU_TM, 512), lambda i: (i, 0)),
                  pl.BlockSpec(ws.shape, lambda i: (0, 0, 0)),
                  pl.BlockSpec(bs_full.shape, lambda i: (0, 0))],
        out_specs=pl.BlockSpec((SGU_TM, W_C), lambda i: (i, 0)),
        out_shape=jax.ShapeDtypeStruct((T_ALL, W_C), BF16),
        compiler_params=_cparams(("arbitrary",)),
        name="sgu",
    )(uv, ws, bs_full)


OUT_TM = 512


SEL_LANE0 = N_EXPERTS


def _top2_gates(logits):
    lane = lax.broadcasted_iota(jnp.int32, logits.shape, 1).astype(F32)
    neg = -jnp.inf
    lg = jnp.where(lane < N_EXPERTS, logits, neg)
    m1 = jnp.max(lg, axis=-1, keepdims=True)
    i1 = jnp.min(jnp.where(lg == m1, lane, float(LANES)), axis=-1, keepdims=True)
    lg2 = jnp.where(lane == i1, neg, lg)
    m2 = jnp.max(lg2, axis=-1, keepdims=True)
    i2 = jnp.min(jnp.where(lg2 == m2, lane, float(LANES)), axis=-1, keepdims=True)
    e2 = jnp.exp(m2 - m1)
    den = 1.0 + e2
    gates = jnp.where(lane == i1, 1.0 / den, 0.0) + jnp.where(lane == i2, e2 / den, 0.0)
    sel = jnp.where((lane == i1 + SEL_LANE0) | (lane == i2 + SEL_LANE0), 1.0, 0.0)
    return gates + sel


def _store_token_tiles(ref, val):
    n = val.shape[0]
    for k in range(D_MODEL // LANES):
        ref[pl.ds(k, n, stride=8), :] = val[:, k * LANES:(k + 1) * LANES]


def _load_token_tiles(ref, n):
    return jnp.concatenate([ref[pl.ds(k, n, stride=8), :] for k in range(D_MODEL // LANES)], axis=-1)


def _out_proj_kernel(*refs, n_x, moe):
    oa_refs, ob_refs, (oc_ref,), x_refs = refs[0:2], refs[2:4], refs[4:5], refs[5:5 + n_x]
    w_ref, g1_ref, nrm_ref, sc_ref, sh_ref, *rest = refs[5 + n_x:]
    if moe:
        rw_ref, x1_ref, h2t_ref, gates_ref, gates_t_ref, wo_ref = rest
    else:
        x1_ref, h2_ref, wo_ref = rest

    @pl.when(pl.program_id(0) == 0)
    def _():
        wo_ref[...] = w_ref[...].astype(BF16)

    y = (_dot(_pick_piece(oa_refs, OUT_TM), wo_ref[0:256, :]) + _dot(_pick_piece(ob_refs, OUT_TM), wo_ref[256:768, :])
         + _dot(oc_ref[...], wo_ref[768:1024, :]))
    x1 = _pick_piece(x_refs, OUT_TM) + g1_ref[...] * y
    x1_ref[...] = x1
    yn = x1 * lax.rsqrt(jnp.mean(x1 * x1, axis=-1, keepdims=True) + EPS)
    h = (yn * nrm_ref[...]) * (1.0 + sc_ref[...]) + sh_ref[...]
    if moe:
        _store_token_tiles(h2t_ref, h)
        gates = _top2_gates(_dot3(h, rw_ref[...]))
        gates_ref[...] = gates
        gates_t_ref[...] = gates.T
    else:
        h2_ref[...] = h.astype(BF16)


def _out_proj_call(oa_pieces, ob_pieces, oc, x_pieces, w_out, l, mod4, norm_g, router_w=None):
    tm = OUT_TM
    n = T_ALL // tm
    moe = router_w is not None
    row = functools.partial(_mod_row, tm=tm)
    mod_spec = lambda k: pl.BlockSpec((None, None, 1, D_MODEL), lambda i: (row(i), k, 0, 0))
    tok = lambda w: pl.BlockSpec((tm, w), lambda i: (i, 0))
    full = lambda a: pl.BlockSpec(a.shape, lambda i: (0,) * a.ndim)
    in_specs = (_piece_specs(oa_pieces, tm, W_A) + _piece_specs(ob_pieces, tm, W_B) + [tok(W_C)]
                + _piece_specs(x_pieces, tm, D_MODEL)
                + [_resident_layer(w_out, l), mod_spec(2), full(norm_g), mod_spec(4), mod_spec(3)])
    args = [*oa_pieces, *ob_pieces, oc, *x_pieces, w_out, mod4, norm_g, mod4, mod4]
    if moe:
        in_specs.append(full(router_w))
        args.append(router_w)
        out_specs = [tok(D_MODEL), pl.BlockSpec((tm * 8, LANES), lambda i: (i, 0)), tok(LANES),
                     pl.BlockSpec((LANES, tm), lambda i: (0, i))]
        out_shape = [jax.ShapeDtypeStruct((T_ALL, D_MODEL), F32), jax.ShapeDtypeStruct((T_ALL * 8, LANES), F32),
                     jax.ShapeDtypeStruct((T_ALL, LANES), F32), jax.ShapeDtypeStruct((LANES, T_ALL), F32)]
    else:
        out_specs = [tok(D_MODEL), tok(D_MODEL)]
        out_shape = [jax.ShapeDtypeStruct((T_ALL, D_MODEL), F32), jax.ShapeDtypeStruct((T_ALL, D_MODEL), BF16)]
    return pl.pallas_call(
        functools.partial(_out_proj_kernel, n_x=len(x_pieces), moe=moe),
        grid=(n,),
        in_specs=in_specs,
        out_specs=out_specs,
        out_shape=out_shape,
        scratch_shapes=[pltpu.VMEM((D_MODEL, D_MODEL), BF16)],
        compiler_params=_cparams(("arbitrary",)),
        name="out_proj_moe" if moe else "out_proj",
    )(*args)


FFN_TM = 512
MXU_N = 256
FFN_SPLITS = (0, 1024, 2048, D_FF)
assert all(s % MXU_N == 0 for s in FFN_SPLITS)


def _swiglu(h, wg_ref, wu_ref, wd_ref):
    out = None
    for c0, c1 in zip(FFN_SPLITS[:-1], FFN_SPLITS[1:]):
        act = _silu(_dot(h, wg_ref[:, c0:c1])) * _dot(h, wu_ref[:, c0:c1])
        d = _dot(act.astype(BF16), wd_ref[c0:c1, :])
        out = d if out is None else out + d
    return out


def _ffn_kernel(h_ref, x_ref, g2_ref, wg_ref, wu_ref, wd_ref, o_ref):
    o_ref[...] = x_ref[...] + g2_ref[...] * _swiglu(h_ref[...], wg_ref, wu_ref, wd_ref)


def _ffn_call(h2, x1, mod4, wg, wu, wd):
    tm = FFN_TM
    row = functools.partial(_mod_row, tm=tm)
    tok = lambda w: pl.BlockSpec((tm, w), lambda i: (i, 0))
    resident = lambda a: pl.BlockSpec(a.shape, lambda i: (0, 0), pipeline_mode=pl.Buffered(1))
    return pl.pallas_call(
        _ffn_kernel,
        grid=(T_ALL // tm,),
        in_specs=[tok(D_MODEL), tok(D_MODEL),
                  pl.BlockSpec((None, None, 1, D_MODEL), lambda i: (row(i), 5, 0, 0)),
                  resident(wg), resident(wu), resident(wd)],
        out_specs=tok(D_MODEL),
        out_shape=jax.ShapeDtypeStruct((T_ALL, D_MODEL), F32),
        compiler_params=_cparams(("arbitrary",)),
        name="ffn_dense",
    )(h2, x1, mod4, wg, wu, wd)


MOE_TM = 512
MOE_NT_MAX = (2 * T_ALL) // MOE_TM + N_EXPERTS
MOE_ROWS = MOE_NT_MAX * MOE_TM
PLAN_BLK = 512
MISC_LAST_START = 8
MISC_NT = 16


def _moe_plan_kernel(gt_ref, posa_ref, posb_ref, te_ref, ti_ref, misc_ref):
    tm = float(MOE_TM)
    sel = gt_ref[SEL_LANE0:SEL_LANE0 + N_EXPERTS, :]
    cnt = jnp.sum(sel, axis=1, keepdims=True)
    nt = jnp.floor((cnt + (tm - 1.0)) * (1.0 / tm))
    sub = lax.broadcasted_iota(jnp.int32, (N_EXPERTS, LANES), 0).astype(F32)
    lane = lax.broadcasted_iota(jnp.int32, (N_EXPERTS, LANES), 1).astype(F32)
    nt_b = jnp.broadcast_to(nt, (N_EXPERTS, LANES))
    nt_row = jnp.sum(jnp.where(sub == lane, nt_b, 0.0), axis=0, keepdims=True)
    toff = jnp.sum(jnp.where(lane < sub, jnp.broadcast_to(nt_row, (N_EXPERTS, LANES)), 0.0),
                   axis=1, keepdims=True)
    tend = toff + nt
    n_total = jnp.sum(nt, axis=0, keepdims=True)
    jc = jnp.minimum(lane, n_total - 1.0)
    te = jnp.sum(jnp.where(jc >= tend, 1.0, 0.0), axis=0, keepdims=True)
    te_ref[...] = te.astype(jnp.int32)
    ti_ref[...] = jc[0:1, :].astype(jnp.int32)
    last_start = (tend - 1.0) * tm
    ls_row = jnp.sum(jnp.where(sub + MISC_LAST_START == lane, jnp.broadcast_to(last_start, (N_EXPERTS, LANES)), 0.0),
                     axis=0, keepdims=True)
    nt_row2 = jnp.sum(jnp.where(sub + MISC_NT == lane, nt_b, 0.0), axis=0, keepdims=True)
    misc = jnp.where(lane[0:1, :] == 0.0, n_total, 0.0) + ls_row + nt_row2
    misc_ref[...] = misc.astype(jnp.int32)

    off = toff * tm
    r = lax.broadcasted_iota(jnp.int32, (PLAN_BLK, PLAN_BLK), 0)
    c = lax.broadcasted_iota(jnp.int32, (PLAN_BLK, PLAN_BLK), 1)
    upper = jnp.where(r <= c, 1.0, 0.0).astype(BF16)
    carry = jnp.zeros((N_EXPERTS, 1), F32)
    for blk in range(T_ALL // PLAN_BLK):
        cols = slice(blk * PLAN_BLK, (blk + 1) * PLAN_BLK)
        s = gt_ref[SEL_LANE0:SEL_LANE0 + N_EXPERTS, cols]
        rank = _dot(s.astype(BF16), upper) + carry
        pos = off + rank - 1.0
        posa_ref[:, cols] = jnp.min(jnp.where(s > 0.0, pos, 1e9), axis=0, keepdims=True).astype(jnp.int32)
        posb_ref[:, cols] = jnp.max(jnp.where(s > 0.0, pos, -1.0), axis=0, keepdims=True).astype(jnp.int32)
        carry = carry + jnp.sum(s, axis=1, keepdims=True)


def _moe_plan_call(gates_t):
    row = lambda w: jax.ShapeDtypeStruct((1, w), jnp.int32)
    full = lambda w: pl.BlockSpec((1, w), lambda: (0, 0))
    return pl.pallas_call(
        _moe_plan_kernel,
        in_specs=[pl.BlockSpec((LANES, T_ALL), lambda: (0, 0))],
        out_specs=[full(T_ALL), full(T_ALL), full(LANES), full(LANES), full(LANES)],
        out_shape=[row(T_ALL), row(T_ALL), row(LANES), row(LANES), row(LANES)],
        compiler_params=pltpu.CompilerParams(vmem_limit_bytes=VMEM_LIMIT),
        name="moe_plan",
    )(gates_t)


DMA_UNROLL = 8


def _row_tile(ref, row):
    return ref.at[pl.ds(pl.multiple_of(row * 8, 8), 8), :]


def _moe_scatter_kernel(misc_ref, posa_ref, posb_ref, h_ref, xs_ref, zero_ref, sem):
    tm = h_ref.shape[0] // 8

    @pl.when(pl.program_id(0) == 0)
    def _():
        zero_ref[...] = jnp.zeros_like(zero_ref)

        def zero_tile(first_row):
            start = pl.multiple_of(first_row * 8, 8)
            cp = pltpu.make_async_copy(zero_ref, xs_ref.at[pl.ds(start, MOE_TM * 8), :], sem.at[0])
            cp.start()
            cp.wait()

        for e in range(N_EXPERTS):
            @pl.when(misc_ref[0, MISC_NT + e] > 0)
            def _():
                zero_tile(misc_ref[0, MISC_LAST_START + e])

        def zero_tail(j, carry):
            zero_tile(j * MOE_TM)
            return carry

        lax.fori_loop(misc_ref[0, 0], MOE_NT_MAX, zero_tail, 0)

    def issue(r, carry):
        src = _row_tile(h_ref, r)
        pltpu.make_async_copy(src, _row_tile(xs_ref, posa_ref[0, r]), sem.at[0]).start(priority=0)
        pltpu.make_async_copy(src, _row_tile(xs_ref, posb_ref[0, r]), sem.at[1]).start(priority=1)
        return carry

    lax.fori_loop(0, tm, issue, 0, unroll=DMA_UNROLL)
    for k in range(2):
        pltpu.make_async_copy(h_ref, xs_ref.at[pl.ds(0, tm * 8), :], sem.at[k]).wait()


SCATTER_TM = 512


def _moe_scatter_call(misc, posa3, posb3, h2t):
    tm = SCATTER_TM
    smem_row = pl.BlockSpec((None, 1, tm), lambda i: (i, 0, 0), memory_space=pltpu.SMEM)
    return pl.pallas_call(
        _moe_scatter_kernel,
        grid=(T_ALL // tm,),
        in_specs=[pl.BlockSpec((1, LANES), lambda i: (0, 0), memory_space=pltpu.SMEM), smem_row, smem_row,
                  pl.BlockSpec((tm * 8, LANES), lambda i: (i, 0))],
        out_specs=pl.BlockSpec(memory_space=pl.ANY),
        out_shape=jax.ShapeDtypeStruct((MOE_ROWS * 8, LANES), F32),
        scratch_shapes=[pltpu.VMEM((MOE_TM * 8, LANES), F32), pltpu.SemaphoreType.DMA((2,))],
        compiler_params=_cparams(("arbitrary",)),
        name="moe_scatter",
    )(misc, posa3, posb3, h2t)


def _ffn_grouped_kernel(te_ref, ti_ref, misc_ref, x_ref, wg_ref, wu_ref, wd_ref, o_ref):
    j = pl.program_id(0)

    @pl.when(j < misc_ref[0])
    def _():
        h = _load_token_tiles(x_ref, MOE_TM).astype(BF16)
        _store_token_tiles(o_ref, _swiglu(h, wg_ref, wu_ref, wd_ref))

    @pl.when(j >= misc_ref[0])
    def _():
        o_ref[...] = jnp.zeros_like(o_ref)


def _ffn_grouped_call(te, ti, misc, xs, wg, wu, wd):
    expert = lambda *s: pl.BlockSpec((None,) + s, lambda j, te, ti, misc: (te[j], 0, 0))
    grid_spec = pltpu.PrefetchScalarGridSpec(
        num_scalar_prefetch=3,
        grid=(MOE_NT_MAX,),
        in_specs=[pl.BlockSpec((MOE_TM * 8, LANES), lambda j, te, ti, misc: (ti[j], 0)),
                  expert(D_MODEL, D_FF), expert(D_MODEL, D_FF), expert(D_FF, D_MODEL)],
        out_specs=pl.BlockSpec((MOE_TM * 8, LANES), lambda j, te, ti, misc: (j, 0)),
    )
    return pl.pallas_call(
        _ffn_grouped_kernel,
        grid_spec=grid_spec,
        out_shape=jax.ShapeDtypeStruct((MOE_ROWS * 8, LANES), F32),
        compiler_params=_cparams(("arbitrary",)),
        name="ffn_grouped",
    )(te, ti, misc, xs, wg, wu, wd)


COMBINE_TM = 256


def _moe_combine_kernel(posa_ref, posb_ref, posa_next_ref, posb_next_ref, ys_ref, x_ref, g2_ref, gates_ref, nf_ref,
                        op_ref, os_ref, bufa_ref, bufb_ref, sem):
    tm = COMBINE_TM
    i = pl.program_id(0)
    n = pl.num_programs(0)
    slot = i % 2

    def gather(pa_ref, pb_ref, s):
        def issue(r, carry):
            pltpu.make_async_copy(_row_tile(ys_ref, pa_ref[0, r]), _row_tile(bufa_ref.at[s], r),
                                  sem.at[s, 0]).start(priority=0)
            pltpu.make_async_copy(_row_tile(ys_ref, pb_ref[0, r]), _row_tile(bufb_ref.at[s], r),
                                  sem.at[s, 1]).start(priority=1)
            return carry

        lax.fori_loop(0, tm, issue, 0, unroll=DMA_UNROLL)

    @pl.when(i == 0)
    def _():
        gather(posa_ref, posb_ref, 0)

    @pl.when(i + 1 < n)
    def _():
        gather(posa_next_ref, posb_next_ref, 1 - slot)

    gates = gates_ref[...]
    lane = lax.broadcasted_iota(jnp.int32, gates.shape, 1).astype(F32)
    is_sel = (lane >= SEL_LANE0) & (lane < SEL_LANE0 + N_EXPERTS) & (gates > 0.0)
    ia = jnp.min(jnp.where(is_sel, lane, float(LANES)), axis=-1, keepdims=True) - SEL_LANE0
    ib = jnp.max(jnp.where(is_sel, lane, -1.0), axis=-1, keepdims=True) - SEL_LANE0
    wa = jnp.sum(jnp.where(lane == ia, gates, 0.0), axis=-1, keepdims=True)
    wb = jnp.sum(jnp.where(lane == ib, gates, 0.0), axis=-1, keepdims=True)

    pltpu.make_async_copy(ys_ref.at[pl.ds(0, tm * 8), :], bufa_ref.at[slot], sem.at[slot, 0]).wait()
    pltpu.make_async_copy(ys_ref.at[pl.ds(0, tm * 8), :], bufb_ref.at[slot], sem.at[slot, 1]).wait()

    y = wa * _load_token_tiles(bufa_ref.at[slot], tm) + wb * _load_token_tiles(bufb_ref.at[slot], tm)
    out = x_ref[...] + g2_ref[...] * y
    out = (out * lax.rsqrt(jnp.mean(out * out, axis=-1, keepdims=True) + EPS)) * nf_ref[...]

    @pl.when(i < T_PROMPT // tm)
    def _():
        op_ref[...] = out

    @pl.when(i >= T_PROMPT // tm)
    def _():
        os_ref[...] = out


def _moe_combine_call(posa3, posb3, ys, x1, mod4, gates, norm_f):
    tm = COMBINE_TM
    n = T_ALL // tm
    n_p = T_PROMPT // tm
    row = functools.partial(_mod_row, tm=tm)
    smem_row = pl.BlockSpec((None, 1, tm), lambda i: (i, 0, 0), memory_space=pltpu.SMEM)
    smem_next = pl.BlockSpec((None, 1, tm), lambda i: (jnp.minimum(i + 1, n - 1), 0, 0), memory_space=pltpu.SMEM)
    tok = lambda w: pl.BlockSpec((tm, w), lambda i: (i, 0))
    return pl.pallas_call(
        _moe_combine_kernel,
        grid=(n,),
        in_specs=[smem_row, smem_row, smem_next, smem_next, pl.BlockSpec(memory_space=pl.ANY), tok(D_MODEL),
                  pl.BlockSpec((None, None, 1, D_MODEL), lambda i: (row(i), 5, 0, 0)), tok(LANES),
                  pl.BlockSpec((1, D_MODEL), lambda i: (0, 0))],
        out_specs=[pl.BlockSpec((tm, D_MODEL), lambda i: (jnp.minimum(i, n_p - 1), 0)),
                   pl.BlockSpec((tm, D_MODEL), lambda i: (jnp.maximum(i - n_p, 0), 0))],
        out_shape=[jax.ShapeDtypeStruct((T_PROMPT, D_MODEL), F32), jax.ShapeDtypeStruct((T_SAMPLE, D_MODEL), F32)],
        scratch_shapes=[pltpu.VMEM((2, tm * 8, LANES), F32), pltpu.VMEM((2, tm * 8, LANES), F32),
                        pltpu.SemaphoreType.DMA((2, 2))],
        compiler_params=_cparams(("arbitrary",)),
        name="moe_combine",
    )(posa3, posb3, posa3, posb3, ys, x1, mod4, gates, norm_f)


def _moe_call(h2t, x1, mod4, gates, gates_t, wg, wu, wd, norm_f):
    posa, posb, te, ti, misc = _moe_plan_call(gates_t)
    xs = _moe_scatter_call(misc, posa.reshape(T_ALL // SCATTER_TM, 1, SCATTER_TM),
                           posb.reshape(T_ALL // SCATTER_TM, 1, SCATTER_TM), h2t)
    ys = _ffn_grouped_call(te.reshape(LANES), ti.reshape(LANES), misc.reshape(LANES), xs, wg, wu, wd)
    return _moe_combine_call(posa.reshape(T_ALL // COMBINE_TM, 1, COMBINE_TM),
                             posb.reshape(T_ALL // COMBINE_TM, 1, COMBINE_TM), ys, x1, mod4, gates, norm_f)


def kernel(x_prompt, x_sample, cache_k, cache_v, state_gla_fwd, state_gla_bwd, c, c_ctx, w_ada, b_ada, norm_mix, norm_ffn, w_in, w_out, gla_w_up, gla_b_up, gla_norm, diff_lambda, diff_norm, sgu_w, sgu_b, ffn_w_gate, ffn_w_up, ffn_w_down, router_w, moe_w_gate, moe_w_up, moe_w_down, norm_f):
    assert DEPTH == 2
    x_pieces = [x_prompt.reshape(T_PROMPT, D_MODEL), x_sample.reshape(T_SAMPLE, D_MODEL)]
    cvecs = jnp.concatenate([c_ctx[None, :], c, jnp.zeros((N_MOD_ROWS - 1 - DEC_BATCH, D_MODEL), F32)], axis=0)
    mod = _ada_call(cvecs, w_ada, b_ada)
    cos, sin_signed = _rope_tables()
    zeros_state = jnp.zeros((BATCH, 64, 256), F32)

    w_in_t = jnp.swapaxes(w_in, 1, 2)
    prev_kv, sfs, sbs = [], [], []
    for l in range(DEPTH):
        mod4 = mod[l].reshape(N_MOD_ROWS, 6, 1, D_MODEL)
        w_up = jnp.zeros((LANES, 2 * W_A), F32)
        w_up = w_up.at[0:GLA_RANK, 0:W_A].set(gla_w_up[l, 0]).at[GLA_RANK:2 * GLA_RANK, W_A:].set(gla_w_up[l, 1])
        b_up = gla_b_up[l].reshape(1, 2 * W_A)
        g4, la, qb, kb, vb, uv = _in_proj_call(x_pieces, norm_mix[l][None, :], mod4, w_in_t, l, w_up.astype(BF16), b_up)

        gain_a = gla_norm[l][None, :]
        oa_p, sf, sb = _gla_call(g4, la, zeros_state, zeros_state, gain_a,
                                 batch=BATCH, seq=SEQ, row_block0=0)
        oa_s, _, _ = _gla_call(g4, la, _state_to_kernel(state_gla_fwd[:, l]),
                               _state_to_kernel(state_gla_bwd[:, l]), gain_a,
                               batch=DEC_BATCH, seq=DEC_SEQ, row_block0=T_PROMPT // DEC_SEQ)

        lam_init = 0.8 - 0.6 * math.exp(-0.3 * l)
        gain_b = diff_norm[l][None, :]
        if l < DEPTH - 1:
            (ob_p,) = _attn_prompt_call(diff_lambda[l], qb, kb, vb, gain_b, lam_init=lam_init)
            prev_kv.append((kb, vb))
        else:
            ob_p, new_cache_k, new_cache_v = _attn_prompt_call(diff_lambda[l], qb, kb, vb, gain_b, prev_kv,
                                                               lam_init=lam_init, write_cache=True)
        ob_s = _attn_sample_call(diff_lambda[l], qb, kb, vb,
                                 cache_k[:, l].reshape(DEC_BATCH, PAST_LEN, W_B),
                                 cache_v[:, l].reshape(DEC_BATCH, PAST_LEN, W_B),
                                 cos, sin_signed, gain_b, lam_init=lam_init)

        bs_full = jnp.repeat(sgu_b[l].T, DG_C, axis=1)
        oc = _sgu_call(uv, sgu_w[l].astype(BF16), bs_full)

        if l == 0:
            x1, h2 = _out_proj_call([oa_p, oa_s], [ob_p, ob_s], oc, x_pieces, w_out, l, mod4, norm_ffn[l][None, :])
            x_pieces = [_ffn_call(h2, x1, mod4, ffn_w_gate[0].astype(BF16), ffn_w_up[0].astype(BF16),
                                  ffn_w_down[0].astype(BF16))]
        else:
            rw = jnp.pad(router_w[0], ((0, 0), (0, LANES - N_EXPERTS)))
            x1, h2t, gates, gates_t = _out_proj_call([oa_p, oa_s], [ob_p, ob_s], oc, x_pieces, w_out, l, mod4,
                                                     norm_ffn[l][None, :], rw)
            y_prompt, y_sample = _moe_call(h2t, x1, mod4, gates, gates_t, moe_w_gate[0].astype(BF16),
                                           moe_w_up[0].astype(BF16), moe_w_down[0].astype(BF16), norm_f[None, :])

        sfs.append(_state_from_kernel(sf))
        sbs.append(_state_from_kernel(sb))

    return (y_prompt.reshape(BATCH, SEQ, D_MODEL), y_sample.reshape(DEC_BATCH, DEC_SEQ, D_MODEL),
            new_cache_k, new_cache_v, jnp.stack(sfs, axis=1), jnp.stack(sbs, axis=1))
```

```python
import functools
import math

import jax
import jax.numpy as jnp
import numpy as np
from jax import lax
from jax.experimental import pallas as pl
from jax.experimental.pallas import tpu as pltpu

F32 = jnp.float32
BF16 = jnp.bfloat16

D_MODEL = 1024
BATCH = 32
SEQ = 256
DEPTH = 2
DEC_BATCH = 2
DEC_SEQ = 1024
PAST_LEN = 256
GRID_W = 64
N_HEADS_A = 4
DK_A = 64
W_A = 256
GLA_RANK = 16
GLA_TAU = 16.0
GLA_CHUNK = 64
N_HEADS_B = 4
DH_B = 64
DV_B = 128
W_B = 512
ROPE_THETA = 10000.0
AXIS_PAIRS = DH_B // 4
N_GROUPS_C = 4
DG_C = 64
W_C = 256
SGU_CHUNK = 128
D_FF = 2816
N_EXPERTS = 8
EPS = 1e-6

T_PROMPT = BATCH * SEQ
T_SAMPLE = DEC_BATCH * DEC_SEQ
T_ALL = T_PROMPT + T_SAMPLE
N_MOD_ROWS = 8
LANES = 128
VMEM_LIMIT = 56 * 1024 * 1024


def _cparams(sem):
    return pltpu.CompilerParams(dimension_semantics=sem, vmem_limit_bytes=VMEM_LIMIT)


def _dot(a, b):
    return jnp.dot(a, b, preferred_element_type=F32)


def _dot_nt(a, b):
    return lax.dot_general(a, b, (((1,), (1,)), ((), ())), preferred_element_type=F32)


def _dot_tn(a, b):
    return lax.dot_general(a, b, (((0,), (0,)), ((), ())), preferred_element_type=F32)


def _split_bf16(x):
    hi = x.astype(BF16)
    lo = (x - hi.astype(F32)).astype(BF16)
    return hi, lo


def _dot3(a, w):
    a_hi, a_lo = _split_bf16(a)
    w_hi, w_lo = _split_bf16(w)
    return _dot(a_hi, w_hi) + (_dot(a_lo, w_hi) + _dot(a_hi, w_lo))


def _sigmoid(x):
    return 1.0 / (1.0 + jnp.exp(-x))


def _silu(x):
    return x * _sigmoid(x)


def _gelu_tanh(x):
    c = math.sqrt(2.0 / math.pi)
    return x * (0.5 * (1.0 + jnp.tanh(c * (x + 0.044715 * (x * x * x)))))


def _log_sigmoid(x):
    return jnp.minimum(x, 0.0) - jnp.log(1.0 + jnp.exp(-jnp.abs(x)))


def _mod_row(i, tm):
    n_p = T_PROMPT // tm
    per_b = DEC_SEQ // tm
    return jnp.where(i < n_p, 0, 1 + (i - n_p) // per_b)


ADA_TN = 1536


def _ada_kernel(c_ref, w_ref, b_ref, o_ref):
    a = _silu(c_ref[...])
    o_ref[...] = _dot3(a, w_ref[...]) + b_ref[...]


def _ada_call(cvecs, w_ada, b_ada):
    n_col = (6 * D_MODEL) // ADA_TN
    return pl.pallas_call(
        _ada_kernel,
        grid=(DEPTH, n_col),
        in_specs=[
            pl.BlockSpec((N_MOD_ROWS, D_MODEL), lambda l, j: (0, 0)),
            pl.BlockSpec((None, D_MODEL, ADA_TN), lambda l, j: (l, 0, j)),
            pl.BlockSpec((None, 1, ADA_TN), lambda l, j: (l, 0, j)),
        ],
        out_specs=pl.BlockSpec((None, N_MOD_ROWS, ADA_TN), lambda l, j: (l, 0, j)),
        out_shape=jax.ShapeDtypeStruct((DEPTH, N_MOD_ROWS, 6 * D_MODEL), F32),
        compiler_params=_cparams(("arbitrary", "arbitrary")),
        name="ada_mod",
    )(cvecs, w_ada, b_ada.reshape(DEPTH, 1, 6 * D_MODEL))


IN_TM = 512
IN_COLS = 3104
Z_COL0 = 1024
Z_COLS = 2 * GLA_RANK
W_MAIN = 3072


def _piece_specs(pieces, tm, width):
    specs, t0 = [], 0
    for arr in pieces:
        nt = arr.shape[0] // tm
        specs.append(pl.BlockSpec((tm, width), lambda i, t0=t0, nt=nt: (jnp.clip(i - t0, 0, nt - 1), 0)))
        t0 += nt
    assert t0 * tm == T_ALL and len(pieces) in (1, 2) and (len(pieces) == 1 or pieces[0].shape[0] == T_PROMPT)
    return specs


def _pick_piece(refs, tm):
    if len(refs) == 1:
        return refs[0][...]
    return jnp.where(pl.program_id(0) < T_PROMPT // tm, refs[0][...], refs[1][...])


def _in_proj_kernel(*refs, n_x):
    x_refs = refs[:n_x]
    (nrm_ref, sh_ref, sc_ref, w_ref, wup_ref, bup_ref,
     g4_ref, la_ref, qb_ref, kb_ref, vb_ref, uv_ref, wm_ref, wz_ref) = refs[n_x:]

    @pl.when(pl.program_id(0) == 0)
    def _():
        wm_ref[0:Z_COL0, :] = w_ref[0:Z_COL0, :].astype(BF16)
        wm_ref[Z_COL0:W_MAIN, :] = w_ref[Z_COL0 + Z_COLS:IN_COLS, :].astype(BF16)
        wz_ref[...] = jnp.zeros_like(wz_ref)
        wz_ref[0:Z_COLS, :] = w_ref[Z_COL0:Z_COL0 + Z_COLS, :].astype(BF16)

    x = _pick_piece(x_refs, IN_TM)
    y = x * lax.rsqrt(jnp.mean(x * x, axis=-1, keepdims=True) + EPS)
    h = (y * nrm_ref[...]) * (1.0 + sc_ref[...]) + sh_ref[...]
    hb = h.astype(BF16)
    g4_ref[...] = _dot_nt(hb, wm_ref[0:1024, :])
    qb_ref[...] = _dot_nt(hb, wm_ref[1024:1536, :])
    kb_ref[...] = _dot_nt(hb, wm_ref[1536:2048, :])
    vb_ref[...] = _dot_nt(hb, wm_ref[2048:2560, :])
    uv_ref[...] = _dot_nt(hb, wm_ref[2560:3072, :])
    z = _dot_nt(hb, wz_ref[...])
    zz = _dot(z.astype(BF16), wup_ref[...]) + bup_ref[...]
    la_ref[...] = _log_sigmoid(zz) * (1.0 / GLA_TAU)


def _resident_layer(a, l):
    return pl.BlockSpec((None,) + a.shape[1:], lambda *_: (l,) + (0,) * (a.ndim - 1), pipeline_mode=pl.Buffered(1))


def _in_proj_call(x_pieces, norm_g, mod4, w_in_t, l, w_up, b_up):
    tm = IN_TM
    n = T_ALL // tm
    row = functools.partial(_mod_row, tm=tm)
    mod_spec = lambda k: pl.BlockSpec((None, None, 1, D_MODEL), lambda i: (row(i), k, 0, 0))
    full = lambda a: pl.BlockSpec(a.shape, lambda i: (0,) * a.ndim)
    out = lambda w: pl.BlockSpec((tm, w), lambda i: (i, 0))
    return pl.pallas_call(
        functools.partial(_in_proj_kernel, n_x=len(x_pieces)),
        grid=(n,),
        in_specs=_piece_specs(x_pieces, tm, D_MODEL) + [full(norm_g), mod_spec(0), mod_spec(1), _resident_layer(w_in_t, l),
                                                        full(w_up), full(b_up)],
        out_specs=[out(1024), out(512), out(512), out(512), out(512), out(512)],
        out_shape=[jax.ShapeDtypeStruct((T_ALL, w), F32) for w in (1024, 512, 512, 512, 512, 512)],
        scratch_shapes=[pltpu.VMEM((W_MAIN, D_MODEL), BF16), pltpu.VMEM((LANES, D_MODEL), BF16)],
        compiler_params=_cparams(("arbitrary",)),
        name="in_proj",
    )(*x_pieces, norm_g, mod4, mod4, w_in_t, w_up, b_up)


GLA_SB = 256
GLA_NC = GLA_SB // GLA_CHUNK


def _gla_superblock(q, k, vb, v_heads, la, st_all, tri, mask, same64, head_lanes, forward):
    c = GLA_CHUNK
    mid, last = (c // 2 - 1, c - 1) if forward else (c // 2, 0)
    la_hi, la_lo = _split_bf16(la)
    b = _dot(tri, la_hi) + _dot(tri, la_lo)
    rows_of = lambda r: jnp.concatenate(
        [jnp.broadcast_to(b[i * c + r:i * c + r + 1, :], (c, W_A)) for i in range(GLA_NC)], axis=0)
    m = rows_of(mid)
    bl = rows_of(last)
    qe = (q * jnp.exp(b - m)).astype(BF16)
    ke = k * jnp.exp(m - b)
    qi = (q * jnp.exp(b)).astype(BF16)
    ks = (k * jnp.exp(bl - b)).astype(BF16)
    o = jnp.zeros((GLA_SB, W_A), F32)
    for h in range(N_HEADS_A):
        ke_h = jnp.where(head_lanes[h], ke, 0.0).astype(BF16)
        a = jnp.where(mask, _dot_nt(qe, ke_h), 0.0).astype(BF16)
        o = o + _dot(a, v_heads[h])
    outs = [None] * GLA_NC
    for i in (range(GLA_NC) if forward else reversed(range(GLA_NC))):
        rows = slice(i * c, (i + 1) * c)
        outs[i] = o[rows, :] + _dot_nt(qi[rows, :], st_all.astype(BF16))
        kv = _dot_tn(vb[rows, :], ks[rows, :])
        st_all = st_all * jnp.exp(bl[i * c:i * c + 1, :]) + jnp.where(same64, kv, 0.0)
    return jnp.concatenate(outs, axis=0), st_all


def _cast_rider(w, n_steps):
    e, r, c = w.shape
    per = n_steps // e
    rows = r // per
    assert per * e == n_steps and rows * per == r and rows % 16 == 0
    spec = pl.BlockSpec((None, rows, c), lambda b: (b // per, b % per, 0))
    return spec, jax.ShapeDtypeStruct(w.shape, BF16)


def _gla_kernel(*refs, seq, rider):
    if rider:
        (g4_ref, la_ref, s0f_ref, s0b_ref, gain_ref, w_ref, o_ref, sf_ref, sb_ref, wb_ref, of_ref, ob_ref) = refs
        wb_ref[...] = w_ref[...].astype(BF16)
    else:
        g4_ref, la_ref, s0f_ref, s0b_ref, gain_ref, o_ref, sf_ref, sb_ref, of_ref, ob_ref = refs
    n = GLA_SB
    nsb = seq // n
    r = lax.broadcasted_iota(jnp.int32, (n, n), 0)
    s = lax.broadcasted_iota(jnp.int32, (n, n), 1)
    same64 = (r // GLA_CHUNK) == (s // GLA_CHUNK)
    lower = same64 & (s <= r)
    upper = same64 & (s >= r)
    tri_f = jnp.where(lower, 1.0, 0.0).astype(BF16)
    tri_b = jnp.where(upper, 1.0, 0.0).astype(BF16)
    ones64 = jnp.where(same64, 1.0, 0.0).astype(BF16)
    head_lanes = [(s // DK_A) == h for h in range(N_HEADS_A)]
    scale = DK_A ** -0.5
    expand = lambda st: jnp.where(same64, jnp.concatenate([st] * N_HEADS_A, axis=0), 0.0)
    compact = lambda st_all: functools.reduce(
        lambda a, b: a + b, [st_all[h * 64:(h + 1) * 64, :] for h in range(N_HEADS_A)])

    def step(i, carry):
        stf, stb = carry
        rf = pl.ds(pl.multiple_of(i * n, n), n)
        rb = pl.ds(pl.multiple_of((nsb - 1 - i) * n, n), n)
        for rows, forward in ((rf, True), (rb, False)):
            q = g4_ref[rows, 0:256] * scale
            k = g4_ref[rows, 256:512]
            v = g4_ref[rows, 512:768]
            vb = v.astype(BF16)
            v_heads = [jnp.where(head_lanes[h], v, 0.0).astype(BF16) for h in range(N_HEADS_A)]
            if forward:
                o, stf = _gla_superblock(q, k, vb, v_heads, la_ref[rows, 0:256], stf, tri_f, lower, same64,
                                         head_lanes, True)
                of_ref[rows, :] = o
            else:
                o, stb = _gla_superblock(q, k, vb, v_heads, la_ref[rows, 256:512], stb, tri_b, upper, same64,
                                         head_lanes, False)
                ob_ref[rows, :] = o
        return stf, stb

    stf, stb = lax.fori_loop(0, nsb, step, (expand(s0f_ref[...]), expand(s0b_ref[...])))
    sf_ref[...] = compact(stf)
    sb_ref[...] = compact(stb)

    gain = gain_ref[...]

    def finish(i, carry):
        rows = pl.ds(pl.multiple_of(i * n, n), n)
        o = of_ref[rows, :] + ob_ref[rows, :]
        sq_hi, sq_lo = _split_bf16(o * o)
        ms = (_dot(sq_hi, ones64) + _dot(sq_lo, ones64)) * (1.0 / DK_A)
        y = (o * lax.rsqrt(ms + EPS)) * gain
        o_ref[rows, :] = (y * _silu(g4_ref[rows, 768:1024])).astype(BF16)
        return carry

    lax.fori_loop(0, nsb, finish, 0)


def _gla_call(g4, la, s0f, s0b, gain, cast_w=None, *, batch, seq, row_block0):
    tok = lambda w: pl.BlockSpec((seq, w), lambda b: (row_block0 + b, 0))
    st = pl.BlockSpec((None, 64, 256), lambda b: (b, 0, 0))
    in_specs = [tok(1024), tok(512), st, st, pl.BlockSpec((1, W_A), lambda b: (0, 0))]
    out_specs = [pl.BlockSpec((seq, W_A), lambda b: (b, 0)), st, st]
    out_shape = [jax.ShapeDtypeStruct((batch * seq, W_A), BF16),
                 jax.ShapeDtypeStruct((batch, 64, 256), F32),
                 jax.ShapeDtypeStruct((batch, 64, 256), F32)]
    args = [g4, la, s0f, s0b, gain]
    if cast_w is not None:
        spec, shape = _cast_rider(cast_w, batch)
        in_specs.append(spec)
        out_specs.append(spec)
        out_shape.append(shape)
        args.append(cast_w)
    return pl.pallas_call(
        functools.partial(_gla_kernel, seq=seq, rider=cast_w is not None),
        grid=(batch,),
        in_specs=in_specs,
        out_specs=out_specs,
        out_shape=out_shape,
        scratch_shapes=[pltpu.VMEM((seq, W_A), F32), pltpu.VMEM((seq, W_A), F32)],
        compiler_params=_cparams(("arbitrary",)),
        name=f"gla_{seq}",
    )(*args)


def _state_to_kernel(s):
    b = s.shape[0]
    return jnp.transpose(s, (0, 3, 1, 2)).reshape(b, 64, 256)


def _state_from_kernel(st):
    b = st.shape[0]
    return jnp.transpose(st.reshape(b, 64, N_HEADS_A, DK_A), (0, 2, 3, 1))


def _lambda(lv, lam_init):
    l01 = jnp.sum(lv[0:1, :] * lv[1:2, :], axis=-1, keepdims=True)
    l23 = jnp.sum(lv[2:3, :] * lv[3:4, :], axis=-1, keepdims=True)
    return jnp.exp(l01) - jnp.exp(l23) + lam_init


def _softmax_parts(parts):
    mx = functools.reduce(jnp.maximum, [jnp.max(p, axis=-1, keepdims=True) for p in parts])
    es = [jnp.exp(p - mx) for p in parts]
    den = functools.reduce(lambda a, b: a + b, [jnp.sum(e, axis=-1, keepdims=True) for e in es])
    return [e / den for e in es]


def _diff_finish(o, gain, lam_init):
    o = o * lax.rsqrt(jnp.mean(o * o, axis=-1, keepdims=True) + EPS)
    return ((o * gain) * (1.0 - lam_init)).astype(BF16)


QK_SCALE = DH_B ** -0.5


def _key_halves(k):
    first = lax.broadcasted_iota(jnp.int32, k.shape, 1) < DH_B
    return jnp.where(first, k, 0.0).astype(BF16), jnp.where(first, 0.0, k).astype(BF16)


def _attn_prompt_kernel(lv_ref, q_ref, k_ref, v_ref, gain_ref, *rest, lam_init, n_prev, rider):
    rest = list(rest)
    if rider:
        wb_ref = rest.pop()
        w_ref = rest.pop(2 * n_prev)
        wb_ref[...] = w_ref[...].astype(BF16)
    prev_refs, (o_ref, *cache_refs) = rest[:2 * n_prev], rest[2 * n_prev:]
    lam = _lambda(lv_ref[...], lam_init)
    for h in range(N_HEADS_B):
        cols = slice(h * DV_B, (h + 1) * DV_B)
        q = (q_ref[:, cols] * QK_SCALE).astype(BF16)
        k1, k2 = _key_halves(k_ref[:, cols])
        (p1,) = _softmax_parts([_dot_nt(q, k1)])
        (p2,) = _softmax_parts([_dot_nt(q, k2)])
        a = p1 - lam * p2
        o = _dot(a.astype(BF16), v_ref[:, cols].astype(BF16))
        o_ref[:, cols] = _diff_finish(o, gain_ref[:, cols], lam_init)
        if cache_refs:
            ck_ref, cv_ref = cache_refs
            layers_k = [*prev_refs[0::2], k_ref]
            layers_v = [*prev_refs[1::2], v_ref]
            for l in range(n_prev + 1):
                ck_ref[l, :, h, :] = layers_k[l][:, cols]
                cv_ref[l, :, h, :] = layers_v[l][:, cols]


def _attn_prompt_call(lv, qb, kb, vb, gain, prev_kv=(), cast_w=None, *, lam_init, write_cache=False):
    blk = pl.BlockSpec((SEQ, W_B), lambda b: (b, 0))
    n_prev = len(prev_kv)
    assert write_cache or not prev_kv
    in_specs = [pl.BlockSpec((4, DH_B), lambda b: (0, 0)), blk, blk, blk,
                pl.BlockSpec((1, W_B), lambda b: (0, 0))] + [blk] * (2 * n_prev)
    args = [lv, qb, kb, vb, gain, *[a for kv in prev_kv for a in kv]]
    out_specs = [blk]
    out_shape = [jax.ShapeDtypeStruct((T_PROMPT, W_B), BF16)]
    if write_cache:
        cache = jax.ShapeDtypeStruct((BATCH, n_prev + 1, SEQ, N_HEADS_B, DV_B), F32)
        cblk = pl.BlockSpec((None, n_prev + 1, SEQ, N_HEADS_B, DV_B), lambda b: (b, 0, 0, 0, 0))
        out_specs += [cblk, cblk]
        out_shape += [cache, cache]
    if cast_w is not None:
        spec, shape = _cast_rider(cast_w, BATCH)
        in_specs.append(spec)
        out_specs.append(spec)
        out_shape.append(shape)
        args.append(cast_w)
    return pl.pallas_call(
        functools.partial(_attn_prompt_kernel, lam_init=lam_init, n_prev=n_prev, rider=cast_w is not None),
        grid=(BATCH,),
        in_specs=in_specs,
        out_specs=out_specs,
        out_shape=out_shape,
        compiler_params=_cparams(("arbitrary",)),
        name="diff_attn_prompt",
    )(*args)


def _rope(x, cos, sin_signed):
    lane = lax.broadcasted_iota(jnp.int32, x.shape, 1)
    first = (lane % (2 * AXIS_PAIRS)) < AXIS_PAIRS
    partner = jnp.where(first, pltpu.roll(x, LANES - AXIS_PAIRS, 1), pltpu.roll(x, AXIS_PAIRS, 1))
    return x * cos + partner * sin_signed


ATT_TQ = 256


def _attn_sample_kernel(lv_ref, q_ref, k_ref, v_ref, kc_ref, vc_ref, cosq_ref, sinq_ref,
                        cosk_ref, sink_ref, gain_ref, o_ref, k1_ref, k2_ref, *, lam_init):
    @pl.when(pl.program_id(1) == 0)
    def _():
        for h in range(N_HEADS_B):
            cols = slice(h * DV_B, (h + 1) * DV_B)
            k1_ref[:, cols], k2_ref[:, cols] = _key_halves(_rope(k_ref[:, cols], cosk_ref[...], sink_ref[...]))

    lam = _lambda(lv_ref[...], lam_init)
    for h in range(N_HEADS_B):
        cols = slice(h * DV_B, (h + 1) * DV_B)
        q = (_rope(q_ref[:, cols], cosq_ref[...], sinq_ref[...]) * QK_SCALE).astype(BF16)
        c1, c2 = _key_halves(kc_ref[:, cols])
        p1 = _softmax_parts([_dot_nt(q, k1_ref[:, cols]), _dot_nt(q, c1)])
        p2 = _softmax_parts([_dot_nt(q, k2_ref[:, cols]), _dot_nt(q, c2)])
        a_own = p1[0] - lam * p2[0]
        a_ctx = p1[1] - lam * p2[1]
        o = (_dot(a_own.astype(BF16), v_ref[:, cols].astype(BF16))
             + _dot(a_ctx.astype(BF16), vc_ref[:, cols].astype(BF16)))
        o_ref[:, cols] = _diff_finish(o, gain_ref[:, cols], lam_init)


def _attn_sample_call(lv, qb, kb, vb, kc, vc, cos, sin_signed, gain, *, lam_init):
    tq = ATT_TQ
    nq = DEC_SEQ // tq
    p0 = T_PROMPT // tq
    s0 = T_PROMPT // DEC_SEQ
    qblk = pl.BlockSpec((tq, W_B), lambda b, t: (p0 + b * nq + t, 0))
    kvblk = pl.BlockSpec((DEC_SEQ, W_B), lambda b, t: (s0 + b, 0))
    cblk = pl.BlockSpec((None, PAST_LEN, W_B), lambda b, t: (b, 0, 0))
    return pl.pallas_call(
        functools.partial(_attn_sample_kernel, lam_init=lam_init),
        grid=(DEC_BATCH, nq),
        in_specs=[pl.BlockSpec((4, DH_B), lambda b, t: (0, 0)), qblk, kvblk, kvblk, cblk, cblk,
                  pl.BlockSpec((tq, DV_B), lambda b, t: (t, 0)),
                  pl.BlockSpec((tq, DV_B), lambda b, t: (t, 0)),
                  pl.BlockSpec((DEC_SEQ, DV_B), lambda b, t: (0, 0)),
                  pl.BlockSpec((DEC_SEQ, DV_B), lambda b, t: (0, 0)),
                  pl.BlockSpec((1, W_B), lambda b, t: (0, 0))],
        out_specs=pl.BlockSpec((tq, W_B), lambda b, t: (b * nq + t, 0)),
        out_shape=jax.ShapeDtypeStruct((T_SAMPLE, W_B), BF16),
        scratch_shapes=[pltpu.VMEM((DEC_SEQ, W_B), BF16), pltpu.VMEM((DEC_SEQ, W_B), BF16)],
        compiler_params=_cparams(("arbitrary", "arbitrary")),
        name="diff_attn_sample",
    )(lv, qb, kb, vb, kc, vc, cos, sin_signed, cos, sin_signed, gain)


def _rope_tables():
    rows = DEC_SEQ // GRID_W
    row = jnp.repeat(jnp.arange(rows, dtype=F32), GRID_W)
    col = jnp.tile(jnp.arange(GRID_W, dtype=F32), rows)
    freqs = ROPE_THETA ** (-jnp.arange(AXIS_PAIRS, dtype=F32) / AXIS_PAIRS)
    ar, ac = row[:, None] * freqs, col[:, None] * freqs
    cos64 = jnp.concatenate([jnp.cos(ar), jnp.cos(ar), jnp.cos(ac), jnp.cos(ac)], axis=-1)
    sin64 = jnp.concatenate([-jnp.sin(ar), jnp.sin(ar), -jnp.sin(ac), jnp.sin(ac)], axis=-1)
    return jnp.tile(cos64, (1, 2)), jnp.tile(sin64, (1, 2))


SGU_TM = 512


def _group_mean(x, ones64):
    hi, lo = _split_bf16(x)
    return (_dot(hi, ones64) + _dot(lo, ones64)) * (1.0 / DG_C)


def _sgu_kernel(uv_ref, ws_ref, bs_ref, o_ref):
    r = lax.broadcasted_iota(jnp.int32, (W_C, W_C), 0)
    s = lax.broadcasted_iota(jnp.int32, (W_C, W_C), 1)
    ones64 = jnp.where((r // DG_C) == (s // DG_C), 1.0, 0.0).astype(BF16)
    lane = lax.broadcasted_iota(jnp.int32, (SGU_CHUNK, W_C), 1)
    for n in range(SGU_TM // SGU_CHUNK):
        rows = slice(n * SGU_CHUNK, (n + 1) * SGU_CHUNK)
        u = _gelu_tanh(uv_ref[rows, 0:256])
        v = _gelu_tanh(uv_ref[rows, 256:512])
        d = v - _group_mean(v, ones64)
        vn = d * lax.rsqrt(_group_mean(d * d, ones64) + EPS)
        s_mix = bs_ref[...]
        for g in range(N_GROUPS_C):
            vn_g = jnp.where((lane // DG_C) == g, vn, 0.0).astype(BF16)
            s_mix = s_mix + _dot(ws_ref[g], vn_g)
        o_ref[rows, :] = (u * s_mix).astype(BF16)


def _sgu_call(uv, ws, bs_full):
    n = T_ALL // SGU_TM
    return pl.pallas_call(
        _sgu_kernel,
        grid=(n,),
        in_specs=[pl.BlockSpec((SGU_TM, 512), lambda i: (i, 0)),
                  pl.BlockSpec(ws.shape, lambda i: (0, 0, 0)),
                  pl.BlockSpec(bs_full.shape, lambda i: (0, 0))],
        out_specs=pl.BlockSpec((SGU_TM, W_C), lambda i: (i, 0)),
        out_shape=jax.ShapeDtypeStruct((T_ALL, W_C), BF16),
        compiler_params=_cparams(("arbitrary",)),
        name="sgu",
    )(uv, ws, bs_full)


OUT_TM = 512


SEL_LANE0 = N_EXPERTS


def _top2_gates(logits):
    lane = lax.broadcasted_iota(jnp.int32, logits.shape, 1).astype(F32)
    neg = -jnp.inf
    lg = jnp.where(lane < N_EXPERTS, logits, neg)
    m1 = jnp.max(lg, axis=-1, keepdims=True)
    i1 = jnp.min(jnp.where(lg == m1, lane, float(LANES)), axis=-1, keepdims=True)
    lg2 = jnp.where(lane == i1, neg, lg)
    m2 = jnp.max(lg2, axis=-1, keepdims=True)
    i2 = jnp.min(jnp.where(lg2 == m2, lane, float(LANES)), axis=-1, keepdims=True)
    e2 = jnp.exp(m2 - m1)
    den = 1.0 + e2
    gates = jnp.where(lane == i1, 1.0 / den, 0.0) + jnp.where(lane == i2, e2 / den, 0.0)
    sel = jnp.where((lane == i1 + SEL_LANE0) | (lane == i2 + SEL_LANE0), 1.0, 0.0)
    return gates + sel


def _store_token_tiles(ref, val):
    n = val.shape[0]
    for k in range(D_MODEL // LANES):
        ref[pl.ds(k, n, stride=8), :] = val[:, k * LANES:(k + 1) * LANES]


def _load_token_tiles(ref, n):
    return jnp.concatenate([ref[pl.ds(k, n, stride=8), :] for k in range(D_MODEL // LANES)], axis=-1)


def _out_proj_kernel(*refs, n_x, moe):
    oa_refs, ob_refs, (oc_ref,), x_refs = refs[0:2], refs[2:4], refs[4:5], refs[5:5 + n_x]
    w_ref, g1_ref, nrm_ref, sc_ref, sh_ref, *rest = refs[5 + n_x:]
    if moe:
        rw_ref, x1_ref, h2t_ref, gates_ref, gates_t_ref, wo_ref = rest
    else:
        x1_ref, h2_ref, wo_ref = rest

    @pl.when(pl.program_id(0) == 0)
    def _():
        wo_ref[...] = w_ref[...].astype(BF16)

    y = (_dot(_pick_piece(oa_refs, OUT_TM), wo_ref[0:256, :]) + _dot(_pick_piece(ob_refs, OUT_TM), wo_ref[256:768, :])
         + _dot(oc_ref[...], wo_ref[768:1024, :]))
    x1 = _pick_piece(x_refs, OUT_TM) + g1_ref[...] * y
    x1_ref[...] = x1
    yn = x1 * lax.rsqrt(jnp.mean(x1 * x1, axis=-1, keepdims=True) + EPS)
    h = (yn * nrm_ref[...]) * (1.0 + sc_ref[...]) + sh_ref[...]
    if moe:
        _store_token_tiles(h2t_ref, h)
        gates = _top2_gates(_dot3(h, rw_ref[...]))
        gates_ref[...] = gates
        gates_t_ref[...] = gates.T
    else:
        h2_ref[...] = h.astype(BF16)


def _out_proj_call(oa_pieces, ob_pieces, oc, x_pieces, w_out, l, mod4, norm_g, router_w=None):
    tm = OUT_TM
    n = T_ALL // tm
    moe = router_w is not None
    row = functools.partial(_mod_row, tm=tm)
    mod_spec = lambda k: pl.BlockSpec((None, None, 1, D_MODEL), lambda i: (row(i), k, 0, 0))
    tok = lambda w: pl.BlockSpec((tm, w), lambda i: (i, 0))
    full = lambda a: pl.BlockSpec(a.shape, lambda i: (0,) * a.ndim)
    in_specs = (_piece_specs(oa_pieces, tm, W_A) + _piece_specs(ob_pieces, tm, W_B) + [tok(W_C)]
                + _piece_specs(x_pieces, tm, D_MODEL)
                + [_resident_layer(w_out, l), mod_spec(2), full(norm_g), mod_spec(4), mod_spec(3)])
    args = [*oa_pieces, *ob_pieces, oc, *x_pieces, w_out, mod4, norm_g, mod4, mod4]
    if moe:
        in_specs.append(full(router_w))
        args.append(router_w)
        out_specs = [tok(D_MODEL), pl.BlockSpec((tm * 8, LANES), lambda i: (i, 0)), tok(LANES),
                     pl.BlockSpec((LANES, tm), lambda i: (0, i))]
        out_shape = [jax.ShapeDtypeStruct((T_ALL, D_MODEL), F32), jax.ShapeDtypeStruct((T_ALL * 8, LANES), F32),
                     jax.ShapeDtypeStruct((T_ALL, LANES), F32), jax.ShapeDtypeStruct((LANES, T_ALL), F32)]
    else:
        out_specs = [tok(D_MODEL), tok(D_MODEL)]
        out_shape = [jax.ShapeDtypeStruct((T_ALL, D_MODEL), F32), jax.ShapeDtypeStruct((T_ALL, D_MODEL), BF16)]
    return pl.pallas_call(
        functools.partial(_out_proj_kernel, n_x=len(x_pieces), moe=moe),
        grid=(n,),
        in_specs=in_specs,
        out_specs=out_specs,
        out_shape=out_shape,
        scratch_shapes=[pltpu.VMEM((D_MODEL, D_MODEL), BF16)],
        compiler_params=_cparams(("arbitrary",)),
        name="out_proj_moe" if moe else "out_proj",
    )(*args)


FFN_TM = 512
MXU_N = 256
FFN_SPLITS = (0, 1024, 2048, D_FF)
assert all(s % MXU_N == 0 for s in FFN_SPLITS)


def _swiglu(h, wg_ref, wu_ref, wd_ref):
    out = None
    for c0, c1 in zip(FFN_SPLITS[:-1], FFN_SPLITS[1:]):
        act = _silu(_dot(h, wg_ref[:, c0:c1])) * _dot(h, wu_ref[:, c0:c1])
        d = _dot(act.astype(BF16), wd_ref[c0:c1, :])
        out = d if out is None else out + d
    return out


def _ffn_kernel(h_ref, x_ref, g2_ref, wg_ref, wu_ref, wd_ref, o_ref):
    o_ref[...] = x_ref[...] + g2_ref[...] * _swiglu(h_ref[...], wg_ref, wu_ref, wd_ref)


def _ffn_call(h2, x1, mod4, wg, wu, wd):
    tm = FFN_TM
    row = functools.partial(_mod_row, tm=tm)
    tok = lambda w: pl.BlockSpec((tm, w), lambda i: (i, 0))
    resident = lambda a: pl.BlockSpec(a.shape, lambda i: (0, 0), pipeline_mode=pl.Buffered(1))
    return pl.pallas_call(
        _ffn_kernel,
        grid=(T_ALL // tm,),
        in_specs=[tok(D_MODEL), tok(D_MODEL),
                  pl.BlockSpec((None, None, 1, D_MODEL), lambda i: (row(i), 5, 0, 0)),
                  resident(wg), resident(wu), resident(wd)],
        out_specs=tok(D_MODEL),
        out_shape=jax.ShapeDtypeStruct((T_ALL, D_MODEL), F32),
        compiler_params=_cparams(("arbitrary",)),
        name="ffn_dense",
    )(h2, x1, mod4, wg, wu, wd)


MOE_TM = 512
MOE_NT_MAX = (2 * T_ALL) // MOE_TM + N_EXPERTS
MOE_ROWS = MOE_NT_MAX * MOE_TM
PLAN_BLK = 512
MISC_LAST_START = 8
MISC_NT = 16


def _moe_plan_kernel(gt_ref, posa_ref, posb_ref, te_ref, ti_ref, misc_ref):
    tm = float(MOE_TM)
    sel = gt_ref[SEL_LANE0:SEL_LANE0 + N_EXPERTS, :]
    cnt = jnp.sum(sel, axis=1, keepdims=True)
    nt = jnp.floor((cnt + (tm - 1.0)) * (1.0 / tm))
    sub = lax.broadcasted_iota(jnp.int32, (N_EXPERTS, LANES), 0).astype(F32)
    lane = lax.broadcasted_iota(jnp.int32, (N_EXPERTS, LANES), 1).astype(F32)
    nt_b = jnp.broadcast_to(nt, (N_EXPERTS, LANES))
    nt_row = jnp.sum(jnp.where(sub == lane, nt_b, 0.0), axis=0, keepdims=True)
    toff = jnp.sum(jnp.where(lane < sub, jnp.broadcast_to(nt_row, (N_EXPERTS, LANES)), 0.0),
                   axis=1, keepdims=True)
    tend = toff + nt
    n_total = jnp.sum(nt, axis=0, keepdims=True)
    jc = jnp.minimum(lane, n_total - 1.0)
    te = jnp.sum(jnp.where(jc >= tend, 1.0, 0.0), axis=0, keepdims=True)
    te_ref[...] = te.astype(jnp.int32)
    ti_ref[...] = jc[0:1, :].astype(jnp.int32)
    last_start = (tend - 1.0) * tm
    ls_row = jnp.sum(jnp.where(sub + MISC_LAST_START == lane, jnp.broadcast_to(last_start, (N_EXPERTS, LANES)), 0.0),
                     axis=0, keepdims=True)
    nt_row2 = jnp.sum(jnp.where(sub + MISC_NT == lane, nt_b, 0.0), axis=0, keepdims=True)
    misc = jnp.where(lane[0:1, :] == 0.0, n_total, 0.0) + ls_row + nt_row2
    misc_ref[...] = misc.astype(jnp.int32)

    off = toff * tm
    r = lax.broadcasted_iota(jnp.int32, (PLAN_BLK, PLAN_BLK), 0)
    c = lax.broadcasted_iota(jnp.int32, (PLAN_BLK, PLAN_BLK), 1)
    upper = jnp.where(r <= c, 1.0, 0.0).astype(BF16)
    carry = jnp.zeros((N_EXPERTS, 1), F32)
    for blk in range(T_ALL // PLAN_BLK):
        cols = slice(blk * PLAN_BLK, (blk + 1) * PLAN_BLK)
        s = gt_ref[SEL_LANE0:SEL_LANE0 + N_EXPERTS, cols]
        rank = _dot(s.astype(BF16), upper) + carry
        pos = off + rank - 1.0
        posa_ref[:, cols] = jnp.min(jnp.where(s > 0.0, pos, 1e9), axis=0, keepdims=True).astype(jnp.int32)
        posb_ref[:, cols] = jnp.max(jnp.where(s > 0.0, pos, -1.0), axis=0, keepdims=True).astype(jnp.int32)
        carry = carry + jnp.sum(s, axis=1, keepdims=True)


def _moe_plan_call(gates_t):
    row = lambda w: jax.ShapeDtypeStruct((1, w), jnp.int32)
    full = lambda w: pl.BlockSpec((1, w), lambda: (0, 0))
    return pl.pallas_call(
        _moe_plan_kernel,
        in_specs=[pl.BlockSpec((LANES, T_ALL), lambda: (0, 0))],
        out_specs=[full(T_ALL), full(T_ALL), full(LANES), full(LANES), full(LANES)],
        out_shape=[row(T_ALL), row(T_ALL), row(LANES), row(LANES), row(LANES)],
        compiler_params=pltpu.CompilerParams(vmem_limit_bytes=VMEM_LIMIT),
        name="moe_plan",
    )(gates_t)


DMA_UNROLL = 8


def _row_tile(ref, row):
    return ref.at[pl.ds(pl.multiple_of(row * 8, 8), 8), :]


def _moe_scatter_kernel(misc_ref, posa_ref, posb_ref, h_ref, xs_ref, zero_ref, sem):
    tm = h_ref.shape[0] // 8

    @pl.when(pl.program_id(0) == 0)
    def _():
        zero_ref[...] = jnp.zeros_like(zero_ref)

        def zero_tile(first_row):
            start = pl.multiple_of(first_row * 8, 8)
            cp = pltpu.make_async_copy(zero_ref, xs_ref.at[pl.ds(start, MOE_TM * 8), :], sem.at[0])
            cp.start()
            cp.wait()

        for e in range(N_EXPERTS):
            @pl.when(misc_ref[0, MISC_NT + e] > 0)
            def _():
                zero_tile(misc_ref[0, MISC_LAST_START + e])

        def zero_tail(j, carry):
            zero_tile(j * MOE_TM)
            return carry

        lax.fori_loop(misc_ref[0, 0], MOE_NT_MAX, zero_tail, 0)

    def issue(r, carry):
        src = _row_tile(h_ref, r)
        pltpu.make_async_copy(src, _row_tile(xs_ref, posa_ref[0, r]), sem.at[0]).start(priority=0)
        pltpu.make_async_copy(src, _row_tile(xs_ref, posb_ref[0, r]), sem.at[1]).start(priority=1)
        return carry

    lax.fori_loop(0, tm, issue, 0, unroll=DMA_UNROLL)
    for k in range(2):
        pltpu.make_async_copy(h_ref, xs_ref.at[pl.ds(0, tm * 8), :], sem.at[k]).wait()


SCATTER_TM = 512


def _moe_scatter_call(misc, posa3, posb3, h2t):
    tm = SCATTER_TM
    smem_row = pl.BlockSpec((None, 1, tm), lambda i: (i, 0, 0), memory_space=pltpu.SMEM)
    return pl.pallas_call(
        _moe_scatter_kernel,
        grid=(T_ALL // tm,),
        in_specs=[pl.BlockSpec((1, LANES), lambda i: (0, 0), memory_space=pltpu.SMEM), smem_row, smem_row,
                  pl.BlockSpec((tm * 8, LANES), lambda i: (i, 0))],
        out_specs=pl.BlockSpec(memory_space=pl.ANY),
        out_shape=jax.ShapeDtypeStruct((MOE_ROWS * 8, LANES), F32),
        scratch_shapes=[pltpu.VMEM((MOE_TM * 8, LANES), F32), pltpu.SemaphoreType.DMA((2,))],
        compiler_params=_cparams(("arbitrary",)),
        name="moe_scatter",
    )(misc, posa3, posb3, h2t)


def _ffn_grouped_kernel(te_ref, ti_ref, misc_ref, x_ref, wg_ref, wu_ref, wd_ref, o_ref):
    j = pl.program_id(0)

    @pl.when(j < misc_ref[0])
    def _():
        h = _load_token_tiles(x_ref, MOE_TM).astype(BF16)
        _store_token_tiles(o_ref, _swiglu(h, wg_ref, wu_ref, wd_ref))

    @pl.when(j >= misc_ref[0])
    def _():
        o_ref[...] = jnp.zeros_like(o_ref)


def _ffn_grouped_call(te, ti, misc, xs, wg, wu, wd):
    expert = lambda *s: pl.BlockSpec((None,) + s, lambda j, te, ti, misc: (te[j], 0, 0))
    grid_spec = pltpu.PrefetchScalarGridSpec(
        num_scalar_prefetch=3,
        grid=(MOE_NT_MAX,),
        in_specs=[pl.BlockSpec((MOE_TM * 8, LANES), lambda j, te, ti, misc: (ti[j], 0)),
                  expert(D_MODEL, D_FF), expert(D_MODEL, D_FF), expert(D_FF, D_MODEL)],
        out_specs=pl.BlockSpec((MOE_TM * 8, LANES), lambda j, te, ti, misc: (j, 0)),
    )
    return pl.pallas_call(
        _ffn_grouped_kernel,
        grid_spec=grid_spec,
        out_shape=jax.ShapeDtypeStruct((MOE_ROWS * 8, LANES), F32),
        compiler_params=_cparams(("arbitrary",)),
        name="ffn_grouped",
    )(te, ti, misc, xs, wg, wu, wd)


COMBINE_TM = 256


def _moe_combine_kernel(posa_ref, posb_ref, posa_next_ref, posb_next_ref, ys_ref, x_ref, g2_ref, gates_ref, nf_ref,
                        op_ref, os_ref, bufa_ref, bufb_ref, sem):
    tm = COMBINE_TM
    i = pl.program_id(0)
    n = pl.num_programs(0)
    slot = i % 2

    def gather(pa_ref, pb_ref, s):
        def issue(r, carry):
            pltpu.make_async_copy(_row_tile(ys_ref, pa_ref[0, r]), _row_tile(bufa_ref.at[s], r),
                                  sem.at[s, 0]).start(priority=0)
            pltpu.make_async_copy(_row_tile(ys_ref, pb_ref[0, r]), _row_tile(bufb_ref.at[s], r),
                                  sem.at[s, 1]).start(priority=1)
            return carry

        lax.fori_loop(0, tm, issue, 0, unroll=DMA_UNROLL)

    @pl.when(i == 0)
    def _():
        gather(posa_ref, posb_ref, 0)

    @pl.when(i + 1 < n)
    def _():
        gather(posa_next_ref, posb_next_ref, 1 - slot)

    gates = gates_ref[...]
    lane = lax.broadcasted_iota(jnp.int32, gates.shape, 1).astype(F32)
    is_sel = (lane >= SEL_LANE0) & (lane < SEL_LANE0 + N_EXPERTS) & (gates > 0.0)
    ia = jnp.min(jnp.where(is_sel, lane, float(LANES)), axis=-1, keepdims=True) - SEL_LANE0
    ib = jnp.max(jnp.where(is_sel, lane, -1.0), axis=-1, keepdims=True) - SEL_LANE0
    wa = jnp.sum(jnp.where(lane == ia, gates, 0.0), axis=-1, keepdims=True)
    wb = jnp.sum(jnp.where(lane == ib, gates, 0.0), axis=-1, keepdims=True)

    pltpu.make_async_copy(ys_ref.at[pl.ds(0, tm * 8), :], bufa_ref.at[slot], sem.at[slot, 0]).wait()
    pltpu.make_async_copy(ys_ref.at[pl.ds(0, tm * 8), :], bufb_ref.at[slot], sem.at[slot, 1]).wait()

    y = wa * _load_token_tiles(bufa_ref.at[slot], tm) + wb * _load_token_tiles(bufb_ref.at[slot], tm)
    out = x_ref[...] + g2_ref[...] * y
    out = (out * lax.rsqrt(jnp.mean(out * out, axis=-1, keepdims=True) + EPS)) * nf_ref[...]

    @pl.when(i < T_PROMPT // tm)
    def _():
        op_ref[...] = out

    @pl.when(i >= T_PROMPT // tm)
    def _():
        os_ref[...] = out


def _moe_combine_call(posa3, posb3, ys, x1, mod4, gates, norm_f):
    tm = COMBINE_TM
    n = T_ALL // tm
    n_p = T_PROMPT // tm
    row = functools.partial(_mod_row, tm=tm)
    smem_row = pl.BlockSpec((None, 1, tm), lambda i: (i, 0, 0), memory_space=pltpu.SMEM)
    smem_next = pl.BlockSpec((None, 1, tm), lambda i: (jnp.minimum(i + 1, n - 1), 0, 0), memory_space=pltpu.SMEM)
    tok = lambda w: pl.BlockSpec((tm, w), lambda i: (i, 0))
    return pl.pallas_call(
        _moe_combine_kernel,
        grid=(n,),
        in_specs=[smem_row, smem_row, smem_next, smem_next, pl.BlockSpec(memory_space=pl.ANY), tok(D_MODEL),
                  pl.BlockSpec((None, None, 1, D_MODEL), lambda i: (row(i), 5, 0, 0)), tok(LANES),
                  pl.BlockSpec((1, D_MODEL), lambda i: (0, 0))],
        out_specs=[pl.BlockSpec((tm, D_MODEL), lambda i: (jnp.minimum(i, n_p - 1), 0)),
                   pl.BlockSpec((tm, D_MODEL), lambda i: (jnp.maximum(i - n_p, 0), 0))],
        out_shape=[jax.ShapeDtypeStruct((T_PROMPT, D_MODEL), F32), jax.ShapeDtypeStruct((T_SAMPLE, D_MODEL), F32)],
        scratch_shapes=[pltpu.VMEM((2, tm * 8, LANES), F32), pltpu.VMEM((2, tm * 8, LANES), F32),
                        pltpu.SemaphoreType.DMA((2, 2))],
        compiler_params=_cparams(("arbitrary",)),
        name="moe_combine",
    )(posa3, posb3, posa3, posb3, ys, x1, mod4, gates, norm_f)


def _moe_call(h2t, x1, mod4, gates, gates_t, wg, wu, wd, norm_f):
    posa, posb, te, ti, misc = _moe_plan_call(gates_t)
    xs = _moe_scatter_call(misc, posa.reshape(T_ALL // SCATTER_TM, 1, SCATTER_TM),
                           posb.reshape(T_ALL // SCATTER_TM, 1, SCATTER_TM), h2t)
    ys = _ffn_grouped_call(te.reshape(LANES), ti.reshape(LANES), misc.reshape(LANES), xs, wg, wu, wd)
    return _moe_combine_call(posa.reshape(T_ALL // COMBINE_TM, 1, COMBINE_TM),
                             posb.reshape(T_ALL // COMBINE_TM, 1, COMBINE_TM), ys, x1, mod4, gates, norm_f)


def kernel(x_prompt, x_sample, cache_k, cache_v, state_gla_fwd, state_gla_bwd, c, c_ctx, w_ada, b_ada, norm_mix, norm_ffn, w_in, w_out, gla_w_up, gla_b_up, gla_norm, diff_lambda, diff_norm, sgu_w, sgu_b, ffn_w_gate, ffn_w_up, ffn_w_down, router_w, moe_w_gate, moe_w_up, moe_w_down, norm_f):
    assert DEPTH == 2
    x_pieces = [x_prompt.reshape(T_PROMPT, D_MODEL), x_sample.reshape(T_SAMPLE, D_MODEL)]
    cvecs = jnp.concatenate([c_ctx[None, :], c, jnp.zeros((N_MOD_ROWS - 1 - DEC_BATCH, D_MODEL), F32)], axis=0)
    mod = _ada_call(cvecs, w_ada, b_ada)
    cos, sin_signed = _rope_tables()
    zeros_state = jnp.zeros((BATCH, 64, 256), F32)

    w_in_t = jnp.swapaxes(w_in, 1, 2)
    moe_w = [moe_w_gate[0], moe_w_up[0], moe_w_down[0]]
    prev_kv, sfs, sbs = [], [], []
    for l in range(DEPTH):
        mod4 = mod[l].reshape(N_MOD_ROWS, 6, 1, D_MODEL)
        w_up = jnp.zeros((LANES, 2 * W_A), F32)
        w_up = w_up.at[0:GLA_RANK, 0:W_A].set(gla_w_up[l, 0]).at[GLA_RANK:2 * GLA_RANK, W_A:].set(gla_w_up[l, 1])
        b_up = gla_b_up[l].reshape(1, 2 * W_A)
        g4, la, qb, kb, vb, uv = _in_proj_call(x_pieces, norm_mix[l][None, :], mod4, w_in_t, l, w_up.astype(BF16), b_up)

        gain_a = gla_norm[l][None, :]
        oa_p, sf, sb, moe_w[l] = _gla_call(g4, la, zeros_state, zeros_state, gain_a, moe_w[l],
                                           batch=BATCH, seq=SEQ, row_block0=0)
        oa_s, _, _ = _gla_call(g4, la, _state_to_kernel(state_gla_fwd[:, l]),
                               _state_to_kernel(state_gla_bwd[:, l]), gain_a,
                               batch=DEC_BATCH, seq=DEC_SEQ, row_block0=T_PROMPT // DEC_SEQ)

        lam_init = 0.8 - 0.6 * math.exp(-0.3 * l)
        gain_b = diff_norm[l][None, :]
        if l < DEPTH - 1:
            ob_p, moe_w[2] = _attn_prompt_call(diff_lambda[l], qb, kb, vb, gain_b, cast_w=moe_w[2], lam_init=lam_init)
            prev_kv.append((kb, vb))
        else:
            ob_p, new_cache_k, new_cache_v = _attn_prompt_call(diff_lambda[l], qb, kb, vb, gain_b, prev_kv,
                                                               lam_init=lam_init, write_cache=True)
        ob_s = _attn_sample_call(diff_lambda[l], qb, kb, vb,
                                 cache_k[:, l].reshape(DEC_BATCH, PAST_LEN, W_B),
                                 cache_v[:, l].reshape(DEC_BATCH, PAST_LEN, W_B),
                                 cos, sin_signed, gain_b, lam_init=lam_init)

        bs_full = jnp.repeat(sgu_b[l].T, DG_C, axis=1)
        oc = _sgu_call(uv, sgu_w[l].astype(BF16), bs_full)

        if l == 0:
            x1, h2 = _out_proj_call([oa_p, oa_s], [ob_p, ob_s], oc, x_pieces, w_out, l, mod4, norm_ffn[l][None, :])
            x_pieces = [_ffn_call(h2, x1, mod4, ffn_w_gate[0].astype(BF16), ffn_w_up[0].astype(BF16),
                                  ffn_w_down[0].astype(BF16))]
        else:
            rw = jnp.pad(router_w[0], ((0, 0), (0, LANES - N_EXPERTS)))
            x1, h2t, gates, gates_t = _out_proj_call([oa_p, oa_s], [ob_p, ob_s], oc, x_pieces, w_out, l, mod4,
                                                     norm_ffn[l][None, :], rw)
            y_prompt, y_sample = _moe_call(h2t, x1, mod4, gates, gates_t, *moe_w, norm_f[None, :])

        sfs.append(_state_from_kernel(sf))
        sbs.append(_state_from_kernel(sb))

    return (y_prompt.reshape(BATCH, SEQ, D_MODEL), y_sample.reshape(DEC_BATCH, DEC_SEQ, D_MODEL),
            new_cache_k, new_cache_v, jnp.stack(sfs, axis=1), jnp.stack(sbs, axis=1))
```

```python
import functools
import math

import jax
import jax.numpy as jnp
import numpy as np
from jax import lax
from jax.experimental import pallas as pl
from jax.experimental.pallas import tpu as pltpu

F32 = jnp.float32
BF16 = jnp.bfloat16

D_MODEL = 1024
BATCH = 32
SEQ = 256
DEPTH = 2
DEC_BATCH = 2
DEC_SEQ = 1024
PAST_LEN = 256
GRID_W = 64
N_HEADS_A = 4
DK_A = 64
W_A = 256
GLA_RANK = 16
GLA_TAU = 16.0
GLA_CHUNK = 64
N_HEADS_B = 4
DH_B = 64
DV_B = 128
W_B = 512
ROPE_THETA = 10000.0
AXIS_PAIRS = DH_B // 4
N_GROUPS_C = 4
DG_C = 64
W_C = 256
SGU_CHUNK = 128
D_FF = 2816
N_EXPERTS = 8
EPS = 1e-6

T_PROMPT = BATCH * SEQ
T_SAMPLE = DEC_BATCH * DEC_SEQ
T_ALL = T_PROMPT + T_SAMPLE
N_MOD_ROWS = 8
LANES = 128
VMEM_LIMIT = 56 * 1024 * 1024


def _cparams(sem):
    return pltpu.CompilerParams(dimension_semantics=sem, vmem_limit_bytes=VMEM_LIMIT)


def _dot(a, b):
    return jnp.dot(a, b, preferred_element_type=F32)


def _dot_nt(a, b):
    return lax.dot_general(a, b, (((1,), (1,)), ((), ())), preferred_element_type=F32)


def _dot_tn(a, b):
    return lax.dot_general(a, b, (((0,), (0,)), ((), ())), preferred_element_type=F32)


def _split_bf16(x):
    hi = x.astype(BF16)
    lo = (x - hi.astype(F32)).astype(BF16)
    return hi, lo


def _dot3(a, w):
    a_hi, a_lo = _split_bf16(a)
    w_hi, w_lo = _split_bf16(w)
    return _dot(a_hi, w_hi) + (_dot(a_lo, w_hi) + _dot(a_hi, w_lo))


def _sigmoid(x):
    return 1.0 / (1.0 + jnp.exp(-x))


def _silu(x):
    return x * _sigmoid(x)


def _gelu_tanh(x):
    c = math.sqrt(2.0 / math.pi)
    return x * (0.5 * (1.0 + jnp.tanh(c * (x + 0.044715 * (x * x * x)))))


def _log_sigmoid(x):
    return jnp.minimum(x, 0.0) - jnp.log(1.0 + jnp.exp(-jnp.abs(x)))


def _mod_row(i, tm):
    n_p = T_PROMPT // tm
    per_b = DEC_SEQ // tm
    return jnp.where(i < n_p, 0, 1 + (i - n_p) // per_b)


ADA_TN = 1536


def _ada_kernel(c_ref, w_ref, b_ref, o_ref):
    a = _silu(c_ref[...])
    o_ref[...] = _dot3(a, w_ref[...]) + b_ref[...]


def _ada_call(cvecs, w_ada, b_ada):
    n_col = (6 * D_MODEL) // ADA_TN
    return pl.pallas_call(
        _ada_kernel,
        grid=(DEPTH, n_col),
        in_specs=[
            pl.BlockSpec((N_MOD_ROWS, D_MODEL), lambda l, j: (0, 0)),
            pl.BlockSpec((None, D_MODEL, ADA_TN), lambda l, j: (l, 0, j)),
            pl.BlockSpec((None, 1, ADA_TN), lambda l, j: (l, 0, j)),
        ],
        out_specs=pl.BlockSpec((None, N_MOD_ROWS, ADA_TN), lambda l, j: (l, 0, j)),
        out_shape=jax.ShapeDtypeStruct((DEPTH, N_MOD_ROWS, 6 * D_MODEL), F32),
        compiler_params=_cparams(("arbitrary", "arbitrary")),
        name="ada_mod",
    )(cvecs, w_ada, b_ada.reshape(DEPTH, 1, 6 * D_MODEL))


IN_TM = 512
IN_COLS = 3104
Z_COL0 = 1024
Z_COLS = 2 * GLA_RANK
W_MAIN = 3072


def _piece_specs(pieces, tm, width):
    specs, t0 = [], 0
    for arr in pieces:
        nt = arr.shape[0] // tm
        specs.append(pl.BlockSpec((tm, width), lambda i, t0=t0, nt=nt: (jnp.clip(i - t0, 0, nt - 1), 0)))
        t0 += nt
    assert t0 * tm == T_ALL and len(pieces) in (1, 2) and (len(pieces) == 1 or pieces[0].shape[0] == T_PROMPT)
    return specs


def _pick_piece(refs, tm):
    if len(refs) == 1:
        return refs[0][...]
    return jnp.where(pl.program_id(0) < T_PROMPT // tm, refs[0][...], refs[1][...])


def _in_proj_kernel(*refs, n_x):
    x_refs = refs[:n_x]
    (nrm_ref, sh_ref, sc_ref, w_ref, wup_ref, bup_ref, ws_ref, bs_ref,
     g4_ref, la_ref, qb_ref, kb_ref, vb_ref, oc_ref, wm_ref, wz_ref) = refs[n_x:]

    @pl.when(pl.program_id(0) == 0)
    def _():
        wm_ref[0:Z_COL0, :] = w_ref[0:Z_COL0, :].astype(BF16)
        wm_ref[Z_COL0:W_MAIN, :] = w_ref[Z_COL0 + Z_COLS:IN_COLS, :].astype(BF16)
        wz_ref[...] = jnp.zeros_like(wz_ref)
        wz_ref[0:Z_COLS, :] = w_ref[Z_COL0:Z_COL0 + Z_COLS, :].astype(BF16)

    x = _pick_piece(x_refs, IN_TM)
    y = x * lax.rsqrt(jnp.mean(x * x, axis=-1, keepdims=True) + EPS)
    h = (y * nrm_ref[...]) * (1.0 + sc_ref[...]) + sh_ref[...]
    hb = h.astype(BF16)
    def plain_job(ref, c0, w0):
        def job():
            ref[:, c0:c0 + MXU_N] = _dot_nt(hb, wm_ref[w0:w0 + MXU_N, :])
        return job

    jobs = [plain_job(ref, c, w0 + c) for ref, w0, width in
            ((g4_ref, 0, 1024), (qb_ref, 1024, 512), (kb_ref, 1536, 512), (vb_ref, 2048, 512))
            for c in range(0, width, MXU_N)]
    z = _dot_nt(hb, wz_ref[...])
    jobs.pop(0)()
    zz = _dot(z.astype(BF16), wup_ref[...]) + bup_ref[...]
    uv = _dot_nt(hb, wm_ref[2560:3072, :])
    jobs.pop(0)()
    la_ref[...] = _log_sigmoid(zz) * (1.0 / GLA_TAU)
    jobs.pop(0)()
    oc_ref[...] = _sgu(uv, ws_ref, bs_ref, jobs)


def _resident_layer(a, l):
    return pl.BlockSpec((None,) + a.shape[1:], lambda *_: (l,) + (0,) * (a.ndim - 1), pipeline_mode=pl.Buffered(1))


def _in_proj_call(x_pieces, norm_g, mod4, w_in_t, l, w_up, b_up, sgu_ws, sgu_bs):
    tm = IN_TM
    n = T_ALL // tm
    row = functools.partial(_mod_row, tm=tm)
    mod_spec = lambda k: pl.BlockSpec((None, None, 1, D_MODEL), lambda i: (row(i), k, 0, 0))
    full = lambda a: pl.BlockSpec(a.shape, lambda i: (0,) * a.ndim)
    out = lambda w: pl.BlockSpec((tm, w), lambda i: (i, 0))
    return pl.pallas_call(
        functools.partial(_in_proj_kernel, n_x=len(x_pieces)),
        grid=(n,),
        in_specs=_piece_specs(x_pieces, tm, D_MODEL) + [full(norm_g), mod_spec(0), mod_spec(1), _resident_layer(w_in_t, l),
                                                        full(w_up), full(b_up), full(sgu_ws), full(sgu_bs)],
        out_specs=[out(1024), out(512), out(512), out(512), out(512), out(W_C)],
        out_shape=[jax.ShapeDtypeStruct((T_ALL, w), F32) for w in (1024, 512, 512, 512, 512)]
        + [jax.ShapeDtypeStruct((T_ALL, W_C), BF16)],
        scratch_shapes=[pltpu.VMEM((W_MAIN, D_MODEL), BF16), pltpu.VMEM((LANES, D_MODEL), BF16)],
        compiler_params=_cparams(("arbitrary",)),
        name="in_proj",
    )(*x_pieces, norm_g, mod4, mod4, w_in_t, w_up, b_up, sgu_ws, sgu_bs)


GLA_SB = 256
GLA_NC = GLA_SB // GLA_CHUNK


def _gla_superblock(q, k, vb, v_heads, la, st_all, tri, mask, same64, head_lanes, forward):
    c = GLA_CHUNK
    mid, last = (c // 2 - 1, c - 1) if forward else (c // 2, 0)
    la_hi, la_lo = _split_bf16(la)
    b = _dot(tri, la_hi) + _dot(tri, la_lo)
    rows_of = lambda r: jnp.concatenate(
        [jnp.broadcast_to(b[i * c + r:i * c + r + 1, :], (c, W_A)) for i in range(GLA_NC)], axis=0)
    m = rows_of(mid)
    bl = rows_of(last)
    qe = (q * jnp.exp(b - m)).astype(BF16)
    ke = k * jnp.exp(m - b)
    qi = (q * jnp.exp(b)).astype(BF16)
    ks = (k * jnp.exp(bl - b)).astype(BF16)
    o = jnp.zeros((GLA_SB, W_A), F32)
    for h in range(N_HEADS_A):
        ke_h = jnp.where(head_lanes[h], ke, 0.0).astype(BF16)
        a = jnp.where(mask, _dot_nt(qe, ke_h), 0.0).astype(BF16)
        o = o + _dot(a, v_heads[h])
    outs = [None] * GLA_NC
    for i in (range(GLA_NC) if forward else reversed(range(GLA_NC))):
        rows = slice(i * c, (i + 1) * c)
        outs[i] = o[rows, :] + _dot_nt(qi[rows, :], st_all.astype(BF16))
        kv = _dot_tn(vb[rows, :], ks[rows, :])
        st_all = st_all * jnp.exp(bl[i * c:i * c + 1, :]) + jnp.where(same64, kv, 0.0)
    return jnp.concatenate(outs, axis=0), st_all


def _cast_rider(w, n_steps):
    e, r, c = w.shape
    per = n_steps // e
    rows = r // per
    assert per * e == n_steps and rows * per == r and rows % 16 == 0
    spec = pl.BlockSpec((None, rows, c), lambda b: (b // per, b % per, 0))
    return spec, jax.ShapeDtypeStruct(w.shape, BF16)


def _gla_kernel(*refs, seq, rider):
    if rider:
        (g4_ref, la_ref, s0f_ref, s0b_ref, gain_ref, w_ref, o_ref, sf_ref, sb_ref, wb_ref, of_ref, ob_ref) = refs
        wb_ref[...] = w_ref[...].astype(BF16)
    else:
        g4_ref, la_ref, s0f_ref, s0b_ref, gain_ref, o_ref, sf_ref, sb_ref, of_ref, ob_ref = refs
    n = GLA_SB
    nsb = seq // n
    r = lax.broadcasted_iota(jnp.int32, (n, n), 0)
    s = lax.broadcasted_iota(jnp.int32, (n, n), 1)
    same64 = (r // GLA_CHUNK) == (s // GLA_CHUNK)
    lower = same64 & (s <= r)
    upper = same64 & (s >= r)
    tri_f = jnp.where(lower, 1.0, 0.0).astype(BF16)
    tri_b = jnp.where(upper, 1.0, 0.0).astype(BF16)
    ones64 = jnp.where(same64, 1.0, 0.0).astype(BF16)
    head_lanes = [(s // DK_A) == h for h in range(N_HEADS_A)]
    scale = DK_A ** -0.5
    expand = lambda st: jnp.where(same64, jnp.concatenate([st] * N_HEADS_A, axis=0), 0.0)
    compact = lambda st_all: functools.reduce(
        lambda a, b: a + b, [st_all[h * 64:(h + 1) * 64, :] for h in range(N_HEADS_A)])

    def step(i, carry):
        stf, stb = carry
        rf = pl.ds(pl.multiple_of(i * n, n), n)
        rb = pl.ds(pl.multiple_of((nsb - 1 - i) * n, n), n)
        for rows, forward in ((rf, True), (rb, False)):
            q = g4_ref[rows, 0:256] * scale
            k = g4_ref[rows, 256:512]
            v = g4_ref[rows, 512:768]
            vb = v.astype(BF16)
            v_heads = [jnp.where(head_lanes[h], v, 0.0).astype(BF16) for h in range(N_HEADS_A)]
            if forward:
                o, stf = _gla_superblock(q, k, vb, v_heads, la_ref[rows, 0:256], stf, tri_f, lower, same64,
                                         head_lanes, True)
                of_ref[rows, :] = o
            else:
                o, stb = _gla_superblock(q, k, vb, v_heads, la_ref[rows, 256:512], stb, tri_b, upper, same64,
                                         head_lanes, False)
                ob_ref[rows, :] = o
        return stf, stb

    stf, stb = lax.fori_loop(0, nsb, step, (expand(s0f_ref[...]), expand(s0b_ref[...])))
    sf_ref[...] = compact(stf)
    sb_ref[...] = compact(stb)

    gain = gain_ref[...]

    def finish(i, carry):
        rows = pl.ds(pl.multiple_of(i * n, n), n)
        o = of_ref[rows, :] + ob_ref[rows, :]
        sq_hi, sq_lo = _split_bf16(o * o)
        ms = (_dot(sq_hi, ones64) + _dot(sq_lo, ones64)) * (1.0 / DK_A)
        y = (o * lax.rsqrt(ms + EPS)) * gain
        o_ref[rows, :] = (y * _silu(g4_ref[rows, 768:1024])).astype(BF16)
        return carry

    lax.fori_loop(0, nsb, finish, 0)


def _gla_call(g4, la, s0f, s0b, gain, cast_w=None, *, batch, seq, row_block0):
    tok = lambda w: pl.BlockSpec((seq, w), lambda b: (row_block0 + b, 0))
    st = pl.BlockSpec((None, 64, 256), lambda b: (b, 0, 0))
    in_specs = [tok(1024), tok(512), st, st, pl.BlockSpec((1, W_A), lambda b: (0, 0))]
    out_specs = [pl.BlockSpec((seq, W_A), lambda b: (b, 0)), st, st]
    out_shape = [jax.ShapeDtypeStruct((batch * seq, W_A), BF16),
                 jax.ShapeDtypeStruct((batch, 64, 256), F32),
                 jax.ShapeDtypeStruct((batch, 64, 256), F32)]
    args = [g4, la, s0f, s0b, gain]
    if cast_w is not None:
        spec, shape = _cast_rider(cast_w, batch)
        in_specs.append(spec)
        out_specs.append(spec)
        out_shape.append(shape)
        args.append(cast_w)
    return pl.pallas_call(
        functools.partial(_gla_kernel, seq=seq, rider=cast_w is not None),
        grid=(batch,),
        in_specs=in_specs,
        out_specs=out_specs,
        out_shape=out_shape,
        scratch_shapes=[pltpu.VMEM((seq, W_A), F32), pltpu.VMEM((seq, W_A), F32)],
        compiler_params=_cparams(("arbitrary",)),
        name=f"gla_{seq}",
    )(*args)


def _state_to_kernel(s):
    b = s.shape[0]
    return jnp.transpose(s, (0, 3, 1, 2)).reshape(b, 64, 256)


def _state_from_kernel(st):
    b = st.shape[0]
    return jnp.transpose(st.reshape(b, 64, N_HEADS_A, DK_A), (0, 2, 3, 1))


def _lambda(lv, lam_init):
    l01 = jnp.sum(lv[0:1, :] * lv[1:2, :], axis=-1, keepdims=True)
    l23 = jnp.sum(lv[2:3, :] * lv[3:4, :], axis=-1, keepdims=True)
    return jnp.exp(l01) - jnp.exp(l23) + lam_init


def _softmax_parts(parts):
    mx = functools.reduce(jnp.maximum, [jnp.max(p, axis=-1, keepdims=True) for p in parts])
    es = [jnp.exp(p - mx) for p in parts]
    den = functools.reduce(lambda a, b: a + b, [jnp.sum(e, axis=-1, keepdims=True) for e in es])
    return [e / den for e in es]


def _diff_finish(o, gain, lam_init):
    o = o * lax.rsqrt(jnp.mean(o * o, axis=-1, keepdims=True) + EPS)
    return ((o * gain) * (1.0 - lam_init)).astype(BF16)


QK_SCALE = DH_B ** -0.5


def _key_halves(k):
    first = lax.broadcasted_iota(jnp.int32, k.shape, 1) < DH_B
    return jnp.where(first, k, 0.0).astype(BF16), jnp.where(first, 0.0, k).astype(BF16)


def _attn_prompt_kernel(lv_ref, q_ref, k_ref, v_ref, gain_ref, *rest, lam_init, n_prev, rider):
    rest = list(rest)
    if rider:
        wb_ref = rest.pop()
        w_ref = rest.pop(2 * n_prev)
        wb_ref[...] = w_ref[...].astype(BF16)
    prev_refs, (o_ref, *cache_refs) = rest[:2 * n_prev], rest[2 * n_prev:]
    lam = _lambda(lv_ref[...], lam_init)
    for h in range(N_HEADS_B):
        cols = slice(h * DV_B, (h + 1) * DV_B)
        q = (q_ref[:, cols] * QK_SCALE).astype(BF16)
        k1, k2 = _key_halves(k_ref[:, cols])
        (p1,) = _softmax_parts([_dot_nt(q, k1)])
        (p2,) = _softmax_parts([_dot_nt(q, k2)])
        a = p1 - lam * p2
        o = _dot(a.astype(BF16), v_ref[:, cols].astype(BF16))
        o_ref[:, cols] = _diff_finish(o, gain_ref[:, cols], lam_init)
        if cache_refs:
            ck_ref, cv_ref = cache_refs
            layers_k = [*prev_refs[0::2], k_ref]
            layers_v = [*prev_refs[1::2], v_ref]
            for l in range(n_prev + 1):
                ck_ref[l, :, h, :] = layers_k[l][:, cols]
                cv_ref[l, :, h, :] = layers_v[l][:, cols]


def _attn_prompt_call(lv, qb, kb, vb, gain, prev_kv=(), cast_w=None, *, lam_init, write_cache=False):
    blk = pl.BlockSpec((SEQ, W_B), lambda b: (b, 0))
    n_prev = len(prev_kv)
    assert write_cache or not prev_kv
    in_specs = [pl.BlockSpec((4, DH_B), lambda b: (0, 0)), blk, blk, blk,
                pl.BlockSpec((1, W_B), lambda b: (0, 0))] + [blk] * (2 * n_prev)
    args = [lv, qb, kb, vb, gain, *[a for kv in prev_kv for a in kv]]
    out_specs = [blk]
    out_shape = [jax.ShapeDtypeStruct((T_PROMPT, W_B), BF16)]
    if write_cache:
        cache = jax.ShapeDtypeStruct((BATCH, n_prev + 1, SEQ, N_HEADS_B, DV_B), F32)
        cblk = pl.BlockSpec((None, n_prev + 1, SEQ, N_HEADS_B, DV_B), lambda b: (b, 0, 0, 0, 0))
        out_specs += [cblk, cblk]
        out_shape += [cache, cache]
    if cast_w is not None:
        spec, shape = _cast_rider(cast_w, BATCH)
        in_specs.append(spec)
        out_specs.append(spec)
        out_shape.append(shape)
        args.append(cast_w)
    return pl.pallas_call(
        functools.partial(_attn_prompt_kernel, lam_init=lam_init, n_prev=n_prev, rider=cast_w is not None),
        grid=(BATCH,),
        in_specs=in_specs,
        out_specs=out_specs,
        out_shape=out_shape,
        compiler_params=_cparams(("arbitrary",)),
        name="diff_attn_prompt",
    )(*args)


def _rope(x, cos, sin_signed):
    lane = lax.broadcasted_iota(jnp.int32, x.shape, 1)
    first = (lane % (2 * AXIS_PAIRS)) < AXIS_PAIRS
    partner = jnp.where(first, pltpu.roll(x, LANES - AXIS_PAIRS, 1), pltpu.roll(x, AXIS_PAIRS, 1))
    return x * cos + partner * sin_signed


ATT_TQ = 256


def _attn_sample_kernel(lv_ref, q_ref, k_ref, v_ref, kc_ref, vc_ref, cosq_ref, sinq_ref,
                        cosk_ref, sink_ref, gain_ref, o_ref, k1_ref, k2_ref, *, lam_init):
    @pl.when(pl.program_id(1) == 0)
    def _():
        for h in range(N_HEADS_B):
            cols = slice(h * DV_B, (h + 1) * DV_B)
            k1_ref[:, cols], k2_ref[:, cols] = _key_halves(_rope(k_ref[:, cols], cosk_ref[...], sink_ref[...]))

    lam = _lambda(lv_ref[...], lam_init)
    for h in range(N_HEADS_B):
        cols = slice(h * DV_B, (h + 1) * DV_B)
        q = (_rope(q_ref[:, cols], cosq_ref[...], sinq_ref[...]) * QK_SCALE).astype(BF16)
        c1, c2 = _key_halves(kc_ref[:, cols])
        p1 = _softmax_parts([_dot_nt(q, k1_ref[:, cols]), _dot_nt(q, c1)])
        p2 = _softmax_parts([_dot_nt(q, k2_ref[:, cols]), _dot_nt(q, c2)])
        a_own = p1[0] - lam * p2[0]
        a_ctx = p1[1] - lam * p2[1]
        o = (_dot(a_own.astype(BF16), v_ref[:, cols].astype(BF16))
             + _dot(a_ctx.astype(BF16), vc_ref[:, cols].astype(BF16)))
        o_ref[:, cols] = _diff_finish(o, gain_ref[:, cols], lam_init)


def _attn_sample_call(lv, qb, kb, vb, kc, vc, cos, sin_signed, gain, *, lam_init):
    tq = ATT_TQ
    nq = DEC_SEQ // tq
    p0 = T_PROMPT // tq
    s0 = T_PROMPT // DEC_SEQ
    qblk = pl.BlockSpec((tq, W_B), lambda b, t: (p0 + b * nq + t, 0))
    kvblk = pl.BlockSpec((DEC_SEQ, W_B), lambda b, t: (s0 + b, 0))
    cblk = pl.BlockSpec((None, PAST_LEN, W_B), lambda b, t: (b, 0, 0))
    return pl.pallas_call(
        functools.partial(_attn_sample_kernel, lam_init=lam_init),
        grid=(DEC_BATCH, nq),
        in_specs=[pl.BlockSpec((4, DH_B), lambda b, t: (0, 0)), qblk, kvblk, kvblk, cblk, cblk,
                  pl.BlockSpec((tq, DV_B), lambda b, t: (t, 0)),
                  pl.BlockSpec((tq, DV_B), lambda b, t: (t, 0)),
                  pl.BlockSpec((DEC_SEQ, DV_B), lambda b, t: (0, 0)),
                  pl.BlockSpec((DEC_SEQ, DV_B), lambda b, t: (0, 0)),
                  pl.BlockSpec((1, W_B), lambda b, t: (0, 0))],
        out_specs=pl.BlockSpec((tq, W_B), lambda b, t: (b * nq + t, 0)),
        out_shape=jax.ShapeDtypeStruct((T_SAMPLE, W_B), BF16),
        scratch_shapes=[pltpu.VMEM((DEC_SEQ, W_B), BF16), pltpu.VMEM((DEC_SEQ, W_B), BF16)],
        compiler_params=_cparams(("arbitrary", "arbitrary")),
        name="diff_attn_sample",
    )(lv, qb, kb, vb, kc, vc, cos, sin_signed, cos, sin_signed, gain)


def _rope_tables():
    rows = DEC_SEQ // GRID_W
    row = jnp.repeat(jnp.arange(rows, dtype=F32), GRID_W)
    col = jnp.tile(jnp.arange(GRID_W, dtype=F32), rows)
    freqs = ROPE_THETA ** (-jnp.arange(AXIS_PAIRS, dtype=F32) / AXIS_PAIRS)
    ar, ac = row[:, None] * freqs, col[:, None] * freqs
    cos64 = jnp.concatenate([jnp.cos(ar), jnp.cos(ar), jnp.cos(ac), jnp.cos(ac)], axis=-1)
    sin64 = jnp.concatenate([-jnp.sin(ar), jnp.sin(ar), -jnp.sin(ac), jnp.sin(ac)], axis=-1)
    return jnp.tile(cos64, (1, 2)), jnp.tile(sin64, (1, 2))


def _group_mean(x, ones64):
    hi, lo = _split_bf16(x)
    return (_dot(hi, ones64) + _dot(lo, ones64)) * (1.0 / DG_C)


def _sgu(uv, ws_ref, bs_ref, fillers=()):
    fillers = list(fillers)
    fill = lambda: fillers.pop(0)() if fillers else None
    r = lax.broadcasted_iota(jnp.int32, (W_C, W_C), 0)
    s = lax.broadcasted_iota(jnp.int32, (W_C, W_C), 1)
    ones64 = jnp.where((r // DG_C) == (s // DG_C), 1.0, 0.0).astype(BF16)
    lane = lax.broadcasted_iota(jnp.int32, (SGU_CHUNK, W_C), 1)
    outs = []
    for n in range(uv.shape[0] // SGU_CHUNK):
        rows = slice(n * SGU_CHUNK, (n + 1) * SGU_CHUNK)
        u = _gelu_tanh(uv[rows, 0:256])
        v = _gelu_tanh(uv[rows, 256:512])
        mu = _group_mean(v, ones64)
        fill()
        d = v - mu
        var = _group_mean(d * d, ones64)
        fill()
        vn = d * lax.rsqrt(var + EPS)
        s_mix = bs_ref[...]
        for g in range(N_GROUPS_C):
            vn_g = jnp.where((lane // DG_C) == g, vn, 0.0).astype(BF16)
            s_mix = s_mix + _dot(ws_ref[g], vn_g)
        fill()
        outs.append((u * s_mix).astype(BF16))
    while fillers:
        fill()
    return jnp.concatenate(outs, axis=0)


OUT_TM = 512


SEL_LANE0 = N_EXPERTS


def _top2_gates(logits):
    lane = lax.broadcasted_iota(jnp.int32, logits.shape, 1).astype(F32)
    neg = -jnp.inf
    lg = jnp.where(lane < N_EXPERTS, logits, neg)
    m1 = jnp.max(lg, axis=-1, keepdims=True)
    i1 = jnp.min(jnp.where(lg == m1, lane, float(LANES)), axis=-1, keepdims=True)
    lg2 = jnp.where(lane == i1, neg, lg)
    m2 = jnp.max(lg2, axis=-1, keepdims=True)
    i2 = jnp.min(jnp.where(lg2 == m2, lane, float(LANES)), axis=-1, keepdims=True)
    e2 = jnp.exp(m2 - m1)
    den = 1.0 + e2
    gates = jnp.where(lane == i1, 1.0 / den, 0.0) + jnp.where(lane == i2, e2 / den, 0.0)
    sel = jnp.where((lane == i1 + SEL_LANE0) | (lane == i2 + SEL_LANE0), 1.0, 0.0)
    return gates + sel


def _store_token_tiles(ref, val):
    n = val.shape[0]
    for k in range(D_MODEL // LANES):
        ref[pl.ds(k, n, stride=8), :] = val[:, k * LANES:(k + 1) * LANES]


def _load_token_tiles(ref, n):
    return jnp.concatenate([ref[pl.ds(k, n, stride=8), :] for k in range(D_MODEL // LANES)], axis=-1)


def _out_proj_kernel(*refs, n_x, moe):
    oa_refs, ob_refs, (oc_ref,), x_refs = refs[0:2], refs[2:4], refs[4:5], refs[5:5 + n_x]
    w_ref, g1_ref, nrm_ref, sc_ref, sh_ref, *rest = refs[5 + n_x:]
    if moe:
        rw_ref, x1_ref, h2t_ref, gates_ref, gates_t_ref, wo_ref = rest
    else:
        x1_ref, h2_ref, wo_ref = rest

    @pl.when(pl.program_id(0) == 0)
    def _():
        wo_ref[...] = w_ref[...].astype(BF16)

    y = (_dot(_pick_piece(oa_refs, OUT_TM), wo_ref[0:256, :]) + _dot(_pick_piece(ob_refs, OUT_TM), wo_ref[256:768, :])
         + _dot(oc_ref[...], wo_ref[768:1024, :]))
    x1 = _pick_piece(x_refs, OUT_TM) + g1_ref[...] * y
    x1_ref[...] = x1
    yn = x1 * lax.rsqrt(jnp.mean(x1 * x1, axis=-1, keepdims=True) + EPS)
    h = (yn * nrm_ref[...]) * (1.0 + sc_ref[...]) + sh_ref[...]
    if moe:
        _store_token_tiles(h2t_ref, h)
        gates = _top2_gates(_dot3(h, rw_ref[...]))
        gates_ref[...] = gates
        gates_t_ref[...] = gates.T
    else:
        h2_ref[...] = h.astype(BF16)


def _out_proj_call(oa_pieces, ob_pieces, oc, x_pieces, w_out, l, mod4, norm_g, router_w=None):
    tm = OUT_TM
    n = T_ALL // tm
    moe = router_w is not None
    row = functools.partial(_mod_row, tm=tm)
    mod_spec = lambda k: pl.BlockSpec((None, None, 1, D_MODEL), lambda i: (row(i), k, 0, 0))
    tok = lambda w: pl.BlockSpec((tm, w), lambda i: (i, 0))
    full = lambda a: pl.BlockSpec(a.shape, lambda i: (0,) * a.ndim)
    in_specs = (_piece_specs(oa_pieces, tm, W_A) + _piece_specs(ob_pieces, tm, W_B) + [tok(W_C)]
                + _piece_specs(x_pieces, tm, D_MODEL)
                + [_resident_layer(w_out, l), mod_spec(2), full(norm_g), mod_spec(4), mod_spec(3)])
    args = [*oa_pieces, *ob_pieces, oc, *x_pieces, w_out, mod4, norm_g, mod4, mod4]
    if moe:
        in_specs.append(full(router_w))
        args.append(router_w)
        out_specs = [tok(D_MODEL), pl.BlockSpec((tm * 8, LANES), lambda i: (i, 0)), tok(LANES),
                     pl.BlockSpec((LANES, tm), lambda i: (0, i))]
        out_shape = [jax.ShapeDtypeStruct((T_ALL, D_MODEL), F32), jax.ShapeDtypeStruct((T_ALL * 8, LANES), F32),
                     jax.ShapeDtypeStruct((T_ALL, LANES), F32), jax.ShapeDtypeStruct((LANES, T_ALL), F32)]
    else:
        out_specs = [tok(D_MODEL), tok(D_MODEL)]
        out_shape = [jax.ShapeDtypeStruct((T_ALL, D_MODEL), F32), jax.ShapeDtypeStruct((T_ALL, D_MODEL), BF16)]
    return pl.pallas_call(
        functools.partial(_out_proj_kernel, n_x=len(x_pieces), moe=moe),
        grid=(n,),
        in_specs=in_specs,
        out_specs=out_specs,
        out_shape=out_shape,
        scratch_shapes=[pltpu.VMEM((D_MODEL, D_MODEL), BF16)],
        compiler_params=_cparams(("arbitrary",)),
        name="out_proj_moe" if moe else "out_proj",
    )(*args)


FFN_TM = 512
MXU_N = 256
FFN_SPLITS = (0, 1024, 2048, D_FF)
assert all(s % MXU_N == 0 for s in FFN_SPLITS)


def _swiglu(h, wg_ref, wu_ref, wd_ref):
    out = None
    for c0, c1 in zip(FFN_SPLITS[:-1], FFN_SPLITS[1:]):
        act = _silu(_dot(h, wg_ref[:, c0:c1])) * _dot(h, wu_ref[:, c0:c1])
        d = _dot(act.astype(BF16), wd_ref[c0:c1, :])
        out = d if out is None else out + d
    return out


def _ffn_kernel(h_ref, x_ref, g2_ref, wg_ref, wu_ref, wd_ref, o_ref):
    o_ref[...] = x_ref[...] + g2_ref[...] * _swiglu(h_ref[...], wg_ref, wu_ref, wd_ref)


def _ffn_call(h2, x1, mod4, wg, wu, wd):
    tm = FFN_TM
    row = functools.partial(_mod_row, tm=tm)
    tok = lambda w: pl.BlockSpec((tm, w), lambda i: (i, 0))
    resident = lambda a: pl.BlockSpec(a.shape, lambda i: (0, 0), pipeline_mode=pl.Buffered(1))
    return pl.pallas_call(
        _ffn_kernel,
        grid=(T_ALL // tm,),
        in_specs=[tok(D_MODEL), tok(D_MODEL),
                  pl.BlockSpec((None, None, 1, D_MODEL), lambda i: (row(i), 5, 0, 0)),
                  resident(wg), resident(wu), resident(wd)],
        out_specs=tok(D_MODEL),
        out_shape=jax.ShapeDtypeStruct((T_ALL, D_MODEL), F32),
        compiler_params=_cparams(("arbitrary",)),
        name="ffn_dense",
    )(h2, x1, mod4, wg, wu, wd)


MOE_TM = 512
MOE_NT_MAX = (2 * T_ALL) // MOE_TM + N_EXPERTS
MOE_ROWS = MOE_NT_MAX * MOE_TM
PLAN_BLK = 512
MISC_LAST_START = 8
MISC_NT = 16


def _moe_plan_kernel(gt_ref, posa_ref, posb_ref, te_ref, ti_ref, misc_ref):
    tm = float(MOE_TM)
    sel = gt_ref[SEL_LANE0:SEL_LANE0 + N_EXPERTS, :]
    cnt = jnp.sum(sel, axis=1, keepdims=True)
    nt = jnp.floor((cnt + (tm - 1.0)) * (1.0 / tm))
    sub = lax.broadcasted_iota(jnp.int32, (N_EXPERTS, LANES), 0).astype(F32)
    lane = lax.broadcasted_iota(jnp.int32, (N_EXPERTS, LANES), 1).astype(F32)
    nt_b = jnp.broadcast_to(nt, (N_EXPERTS, LANES))
    nt_row = jnp.sum(jnp.where(sub == lane, nt_b, 0.0), axis=0, keepdims=True)
    toff = jnp.sum(jnp.where(lane < sub, jnp.broadcast_to(nt_row, (N_EXPERTS, LANES)), 0.0),
                   axis=1, keepdims=True)
    tend = toff + nt
    n_total = jnp.sum(nt, axis=0, keepdims=True)
    jc = jnp.minimum(lane, n_total - 1.0)
    te = jnp.sum(jnp.where(jc >= tend, 1.0, 0.0), axis=0, keepdims=True)
    te_ref[...] = te.astype(jnp.int32)
    ti_ref[...] = jc[0:1, :].astype(jnp.int32)
    last_start = (tend - 1.0) * tm
    ls_row = jnp.sum(jnp.where(sub + MISC_LAST_START == lane, jnp.broadcast_to(last_start, (N_EXPERTS, LANES)), 0.0),
                     axis=0, keepdims=True)
    nt_row2 = jnp.sum(jnp.where(sub + MISC_NT == lane, nt_b, 0.0), axis=0, keepdims=True)
    misc = jnp.where(lane[0:1, :] == 0.0, n_total, 0.0) + ls_row + nt_row2
    misc_ref[...] = misc.astype(jnp.int32)

    off = toff * tm
    r = lax.broadcasted_iota(jnp.int32, (PLAN_BLK, PLAN_BLK), 0)
    c = lax.broadcasted_iota(jnp.int32, (PLAN_BLK, PLAN_BLK), 1)
    upper = jnp.where(r <= c, 1.0, 0.0).astype(BF16)
    carry = jnp.zeros((N_EXPERTS, 1), F32)
    for blk in range(T_ALL // PLAN_BLK):
        cols = slice(blk * PLAN_BLK, (blk + 1) * PLAN_BLK)
        s = gt_ref[SEL_LANE0:SEL_LANE0 + N_EXPERTS, cols]
        rank = _dot(s.astype(BF16), upper) + carry
        pos = off + rank - 1.0
        posa_ref[:, cols] = jnp.min(jnp.where(s > 0.0, pos, 1e9), axis=0, keepdims=True).astype(jnp.int32)
        posb_ref[:, cols] = jnp.max(jnp.where(s > 0.0, pos, -1.0), axis=0, keepdims=True).astype(jnp.int32)
        carry = carry + jnp.sum(s, axis=1, keepdims=True)


def _moe_plan_call(gates_t):
    row = lambda w: jax.ShapeDtypeStruct((1, w), jnp.int32)
    full = lambda w: pl.BlockSpec((1, w), lambda: (0, 0))
    return pl.pallas_call(
        _moe_plan_kernel,
        in_specs=[pl.BlockSpec((LANES, T_ALL), lambda: (0, 0))],
        out_specs=[full(T_ALL), full(T_ALL), full(LANES), full(LANES), full(LANES)],
        out_shape=[row(T_ALL), row(T_ALL), row(LANES), row(LANES), row(LANES)],
        compiler_params=pltpu.CompilerParams(vmem_limit_bytes=VMEM_LIMIT),
        name="moe_plan",
    )(gates_t)


DMA_UNROLL = 8


def _row_tile(ref, row):
    return ref.at[pl.ds(pl.multiple_of(row * 8, 8), 8), :]


def _moe_scatter_kernel(misc_ref, posa_ref, posb_ref, h_ref, xs_ref, zero_ref, sem):
    tm = h_ref.shape[0] // 8

    @pl.when(pl.program_id(0) == 0)
    def _():
        zero_ref[...] = jnp.zeros_like(zero_ref)

        def zero_tile(first_row):
            start = pl.multiple_of(first_row * 8, 8)
            cp = pltpu.make_async_copy(zero_ref, xs_ref.at[pl.ds(start, MOE_TM * 8), :], sem.at[0])
            cp.start()
            cp.wait()

        for e in range(N_EXPERTS):
            @pl.when(misc_ref[0, MISC_NT + e] > 0)
            def _():
                zero_tile(misc_ref[0, MISC_LAST_START + e])

        def zero_tail(j, carry):
            zero_tile(j * MOE_TM)
            return carry

        lax.fori_loop(misc_ref[0, 0], MOE_NT_MAX, zero_tail, 0)

    def issue(r, carry):
        src = _row_tile(h_ref, r)
        pltpu.make_async_copy(src, _row_tile(xs_ref, posa_ref[0, r]), sem.at[0]).start(priority=0)
        pltpu.make_async_copy(src, _row_tile(xs_ref, posb_ref[0, r]), sem.at[1]).start(priority=1)
        return carry

    lax.fori_loop(0, tm, issue, 0, unroll=DMA_UNROLL)
    for k in range(2):
        pltpu.make_async_copy(h_ref, xs_ref.at[pl.ds(0, tm * 8), :], sem.at[k]).wait()


SCATTER_TM = 512


def _moe_scatter_call(misc, posa3, posb3, h2t):
    tm = SCATTER_TM
    smem_row = pl.BlockSpec((None, 1, tm), lambda i: (i, 0, 0), memory_space=pltpu.SMEM)
    return pl.pallas_call(
        _moe_scatter_kernel,
        grid=(T_ALL // tm,),
        in_specs=[pl.BlockSpec((1, LANES), lambda i: (0, 0), memory_space=pltpu.SMEM), smem_row, smem_row,
                  pl.BlockSpec((tm * 8, LANES), lambda i: (i, 0))],
        out_specs=pl.BlockSpec(memory_space=pl.ANY),
        out_shape=jax.ShapeDtypeStruct((MOE_ROWS * 8, LANES), F32),
        scratch_shapes=[pltpu.VMEM((MOE_TM * 8, LANES), F32), pltpu.SemaphoreType.DMA((2,))],
        compiler_params=_cparams(("arbitrary",)),
        name="moe_scatter",
    )(misc, posa3, posb3, h2t)


def _ffn_grouped_kernel(te_ref, ti_ref, misc_ref, x_ref, wg_ref, wu_ref, wd_ref, o_ref):
    j = pl.program_id(0)

    @pl.when(j < misc_ref[0])
    def _():
        h = _load_token_tiles(x_ref, MOE_TM).astype(BF16)
        _store_token_tiles(o_ref, _swiglu(h, wg_ref, wu_ref, wd_ref))

    @pl.when(j >= misc_ref[0])
    def _():
        o_ref[...] = jnp.zeros_like(o_ref)


def _ffn_grouped_call(te, ti, misc, xs, wg, wu, wd):
    expert = lambda *s: pl.BlockSpec((None,) + s, lambda j, te, ti, misc: (te[j], 0, 0))
    grid_spec = pltpu.PrefetchScalarGridSpec(
        num_scalar_prefetch=3,
        grid=(MOE_NT_MAX,),
        in_specs=[pl.BlockSpec((MOE_TM * 8, LANES), lambda j, te, ti, misc: (ti[j], 0)),
                  expert(D_MODEL, D_FF), expert(D_MODEL, D_FF), expert(D_FF, D_MODEL)],
        out_specs=pl.BlockSpec((MOE_TM * 8, LANES), lambda j, te, ti, misc: (j, 0)),
    )
    return pl.pallas_call(
        _ffn_grouped_kernel,
        grid_spec=grid_spec,
        out_shape=jax.ShapeDtypeStruct((MOE_ROWS * 8, LANES), F32),
        compiler_params=_cparams(("arbitrary",)),
        name="ffn_grouped",
    )(te, ti, misc, xs, wg, wu, wd)


COMBINE_TM = 256


def _moe_combine_kernel(posa_ref, posb_ref, posa_next_ref, posb_next_ref, ys_ref, x_ref, g2_ref, gates_ref, nf_ref,
                        op_ref, os_ref, bufa_ref, bufb_ref, sem):
    tm = COMBINE_TM
    i = pl.program_id(0)
    n = pl.num_programs(0)
    slot = i % 2

    def gather(pa_ref, pb_ref, s):
        def issue(r, carry):
            pltpu.make_async_copy(_row_tile(ys_ref, pa_ref[0, r]), _row_tile(bufa_ref.at[s], r),
                                  sem.at[s, 0]).start(priority=0)
            pltpu.make_async_copy(_row_tile(ys_ref, pb_ref[0, r]), _row_tile(bufb_ref.at[s], r),
                                  sem.at[s, 1]).start(priority=1)
            return carry

        lax.fori_loop(0, tm, issue, 0, unroll=DMA_UNROLL)

    @pl.when(i == 0)
    def _():
        gather(posa_ref, posb_ref, 0)

    @pl.when(i + 1 < n)
    def _():
        gather(posa_next_ref, posb_next_ref, 1 - slot)

    gates = gates_ref[...]
    lane = lax.broadcasted_iota(jnp.int32, gates.shape, 1).astype(F32)
    is_sel = (lane >= SEL_LANE0) & (lane < SEL_LANE0 + N_EXPERTS) & (gates > 0.0)
    ia = jnp.min(jnp.where(is_sel, lane, float(LANES)), axis=-1, keepdims=True) - SEL_LANE0
    ib = jnp.max(jnp.where(is_sel, lane, -1.0), axis=-1, keepdims=True) - SEL_LANE0
    wa = jnp.sum(jnp.where(lane == ia, gates, 0.0), axis=-1, keepdims=True)
    wb = jnp.sum(jnp.where(lane == ib, gates, 0.0), axis=-1, keepdims=True)

    pltpu.make_async_copy(ys_ref.at[pl.ds(0, tm * 8), :], bufa_ref.at[slot], sem.at[slot, 0]).wait()
    pltpu.make_async_copy(ys_ref.at[pl.ds(0, tm * 8), :], bufb_ref.at[slot], sem.at[slot, 1]).wait()

    y = wa * _load_token_tiles(bufa_ref.at[slot], tm) + wb * _load_token_tiles(bufb_ref.at[slot], tm)
    out = x_ref[...] + g2_ref[...] * y
    out = (out * lax.rsqrt(jnp.mean(out * out, axis=-1, keepdims=True) + EPS)) * nf_ref[...]

    @pl.when(i < T_PROMPT // tm)
    def _():
        op_ref[...] = out

    @pl.when(i >= T_PROMPT // tm)
    def _():
        os_ref[...] = out


def _moe_combine_call(posa3, posb3, ys, x1, mod4, gates, norm_f):
    tm = COMBINE_TM
    n = T_ALL // tm
    n_p = T_PROMPT // tm
    row = functools.partial(_mod_row, tm=tm)
    smem_row = pl.BlockSpec((None, 1, tm), lambda i: (i, 0, 0), memory_space=pltpu.SMEM)
    smem_next = pl.BlockSpec((None, 1, tm), lambda i: (jnp.minimum(i + 1, n - 1), 0, 0), memory_space=pltpu.SMEM)
    tok = lambda w: pl.BlockSpec((tm, w), lambda i: (i, 0))
    return pl.pallas_call(
        _moe_combine_kernel,
        grid=(n,),
        in_specs=[smem_row, smem_row, smem_next, smem_next, pl.BlockSpec(memory_space=pl.ANY), tok(D_MODEL),
                  pl.BlockSpec((None, None, 1, D_MODEL), lambda i: (row(i), 5, 0, 0)), tok(LANES),
                  pl.BlockSpec((1, D_MODEL), lambda i: (0, 0))],
        out_specs=[pl.BlockSpec((tm, D_MODEL), lambda i: (jnp.minimum(i, n_p - 1), 0)),
                   pl.BlockSpec((tm, D_MODEL), lambda i: (jnp.maximum(i - n_p, 0), 0))],
        out_shape=[jax.ShapeDtypeStruct((T_PROMPT, D_MODEL), F32), jax.ShapeDtypeStruct((T_SAMPLE, D_MODEL), F32)],
        scratch_shapes=[pltpu.VMEM((2, tm * 8, LANES), F32), pltpu.VMEM((2, tm * 8, LANES), F32),
                        pltpu.SemaphoreType.DMA((2, 2))],
        compiler_params=_cparams(("arbitrary",)),
        name="moe_combine",
    )(posa3, posb3, posa3, posb3, ys, x1, mod4, gates, norm_f)


def _moe_call(h2t, x1, mod4, gates, gates_t, wg, wu, wd, norm_f):
    posa, posb, te, ti, misc = _moe_plan_call(gates_t)
    xs = _moe_scatter_call(misc, posa.reshape(T_ALL // SCATTER_TM, 1, SCATTER_TM),
                           posb.reshape(T_ALL // SCATTER_TM, 1, SCATTER_TM), h2t)
    ys = _ffn_grouped_call(te.reshape(LANES), ti.reshape(LANES), misc.reshape(LANES), xs, wg, wu, wd)
    return _moe_combine_call(posa.reshape(T_ALL // COMBINE_TM, 1, COMBINE_TM),
                             posb.reshape(T_ALL // COMBINE_TM, 1, COMBINE_TM), ys, x1, mod4, gates, norm_f)


def kernel(x_prompt, x_sample, cache_k, cache_v, state_gla_fwd, state_gla_bwd, c, c_ctx, w_ada, b_ada, norm_mix, norm_ffn, w_in, w_out, gla_w_up, gla_b_up, gla_norm, diff_lambda, diff_norm, sgu_w, sgu_b, ffn_w_gate, ffn_w_up, ffn_w_down, router_w, moe_w_gate, moe_w_up, moe_w_down, norm_f):
    assert DEPTH == 2
    x_pieces = [x_prompt.reshape(T_PROMPT, D_MODEL), x_sample.reshape(T_SAMPLE, D_MODEL)]
    cvecs = jnp.concatenate([c_ctx[None, :], c, jnp.zeros((N_MOD_ROWS - 1 - DEC_BATCH, D_MODEL), F32)], axis=0)
    mod = _ada_call(cvecs, w_ada, b_ada)
    cos, sin_signed = _rope_tables()
    zeros_state = jnp.zeros((BATCH, 64, 256), F32)

    w_in_t = jnp.swapaxes(w_in, 1, 2)
    moe_w = [moe_w_gate[0], moe_w_up[0], moe_w_down[0]]
    prev_kv, sfs, sbs = [], [], []
    for l in range(DEPTH):
        mod4 = mod[l].reshape(N_MOD_ROWS, 6, 1, D_MODEL)
        w_up = jnp.zeros((LANES, 2 * W_A), F32)
        w_up = w_up.at[0:GLA_RANK, 0:W_A].set(gla_w_up[l, 0]).at[GLA_RANK:2 * GLA_RANK, W_A:].set(gla_w_up[l, 1])
        b_up = gla_b_up[l].reshape(1, 2 * W_A)
        bs_full = jnp.repeat(sgu_b[l].T, DG_C, axis=1)
        g4, la, qb, kb, vb, oc = _in_proj_call(x_pieces, norm_mix[l][None, :], mod4, w_in_t, l, w_up.astype(BF16), b_up,
                                               sgu_w[l].astype(BF16), bs_full)

        gain_a = gla_norm[l][None, :]
        oa_p, sf, sb, moe_w[l] = _gla_call(g4, la, zeros_state, zeros_state, gain_a, moe_w[l],
                                           batch=BATCH, seq=SEQ, row_block0=0)
        oa_s, _, _ = _gla_call(g4, la, _state_to_kernel(state_gla_fwd[:, l]),
                               _state_to_kernel(state_gla_bwd[:, l]), gain_a,
                               batch=DEC_BATCH, seq=DEC_SEQ, row_block0=T_PROMPT // DEC_SEQ)

        lam_init = 0.8 - 0.6 * math.exp(-0.3 * l)
        gain_b = diff_norm[l][None, :]
        if l < DEPTH - 1:
            ob_p, moe_w[2] = _attn_prompt_call(diff_lambda[l], qb, kb, vb, gain_b, cast_w=moe_w[2], lam_init=lam_init)
            prev_kv.append((kb, vb))
        else:
            ob_p, new_cache_k, new_cache_v = _attn_prompt_call(diff_lambda[l], qb, kb, vb, gain_b, prev_kv,
                                                               lam_init=lam_init, write_cache=True)
        ob_s = _attn_sample_call(diff_lambda[l], qb, kb, vb,
                                 cache_k[:, l].reshape(DEC_BATCH, PAST_LEN, W_B),
                                 cache_v[:, l].reshape(DEC_BATCH, PAST_LEN, W_B),
                                 cos, sin_signed, gain_b, lam_init=lam_init)

        if l == 0:
            x1, h2 = _out_proj_call([oa_p, oa_s], [ob_p, ob_s], oc, x_pieces, w_out, l, mod4, norm_ffn[l][None, :])
            x_pieces = [_ffn_call(h2, x1, mod4, ffn_w_gate[0].astype(BF16), ffn_w_up[0].astype(BF16),
                                  ffn_w_down[0].astype(BF16))]
        else:
            rw = jnp.pad(router_w[0], ((0, 0), (0, LANES - N_EXPERTS)))
            x1, h2t, gates, gates_t = _out_proj_call([oa_p, oa_s], [ob_p, ob_s], oc, x_pieces, w_out, l, mod4,
                                                     norm_ffn[l][None, :], rw)
            y_prompt, y_sample = _moe_call(h2t, x1, mod4, gates, gates_t, *moe_w, norm_f[None, :])

        sfs.append(_state_from_kernel(sf))
        sbs.append(_state_from_kernel(sb))

    return (y_prompt.reshape(BATCH, SEQ, D_MODEL), y_sample.reshape(DEC_BATCH, DEC_SEQ, D_MODEL),
            new_cache_k, new_cache_v, jnp.stack(sfs, axis=1), jnp.stack(sbs, axis=1))
```

```python
import functools
import math

import jax
import jax.numpy as jnp
import numpy as np
from jax import lax
from jax.experimental import pallas as pl
from jax.experimental.pallas import tpu as pltpu

F32 = jnp.float32
BF16 = jnp.bfloat16

D_MODEL = 1024
BATCH = 32
SEQ = 256
DEPTH = 2
DEC_BATCH = 2
DEC_SEQ = 1024
PAST_LEN = 256
GRID_W = 64
N_HEADS_A = 4
DK_A = 64
W_A = 256
GLA_RANK = 16
GLA_TAU = 16.0
GLA_CHUNK = 64
N_HEADS_B = 4
DH_B = 64
DV_B = 128
W_B = 512
ROPE_THETA = 10000.0
AXIS_PAIRS = DH_B // 4
N_GROUPS_C = 4
DG_C = 64
W_C = 256
SGU_CHUNK = 128
D_FF = 2816
N_EXPERTS = 8
EPS = 1e-6

T_PROMPT = BATCH * SEQ
T_SAMPLE = DEC_BATCH * DEC_SEQ
T_ALL = T_PROMPT + T_SAMPLE
N_MOD_ROWS = 8
LANES = 128
VMEM_LIMIT = 56 * 1024 * 1024


def _cparams(sem):
    return pltpu.CompilerParams(dimension_semantics=sem, vmem_limit_bytes=VMEM_LIMIT)


def _dot(a, b):
    return jnp.dot(a, b, preferred_element_type=F32)


def _dot_nt(a, b):
    return lax.dot_general(a, b, (((1,), (1,)), ((), ())), preferred_element_type=F32)


def _dot_tn(a, b):
    return lax.dot_general(a, b, (((0,), (0,)), ((), ())), preferred_element_type=F32)


def _split_bf16(x):
    hi = x.astype(BF16)
    lo = (x - hi.astype(F32)).astype(BF16)
    return hi, lo


def _dot3(a, w):
    a_hi, a_lo = _split_bf16(a)
    w_hi, w_lo = _split_bf16(w)
    return _dot(a_hi, w_hi) + (_dot(a_lo, w_hi) + _dot(a_hi, w_lo))


def _sigmoid(x):
    return 1.0 / (1.0 + jnp.exp(-x))


def _silu(x):
    return x * _sigmoid(x)


def _gelu_tanh(x):
    c = math.sqrt(2.0 / math.pi)
    return x * (0.5 * (1.0 + jnp.tanh(c * (x + 0.044715 * (x * x * x)))))


def _log_sigmoid(x):
    return jnp.minimum(x, 0.0) - jnp.log(1.0 + jnp.exp(-jnp.abs(x)))


def _mod_row(i, tm):
    n_p = T_PROMPT // tm
    per_b = DEC_SEQ // tm
    return jnp.where(i < n_p, 0, 1 + (i - n_p) // per_b)


ADA_TN = 1536


def _ada_kernel(c_ref, w_ref, b_ref, o_ref):
    a = _silu(c_ref[...])
    o_ref[...] = _dot3(a, w_ref[...]) + b_ref[...]


def _ada_call(cvecs, w_ada, b_ada):
    n_col = (6 * D_MODEL) // ADA_TN
    return pl.pallas_call(
        _ada_kernel,
        grid=(DEPTH, n_col),
        in_specs=[
            pl.BlockSpec((N_MOD_ROWS, D_MODEL), lambda l, j: (0, 0)),
            pl.BlockSpec((None, D_MODEL, ADA_TN), lambda l, j: (l, 0, j)),
            pl.BlockSpec((None, 1, ADA_TN), lambda l, j: (l, 0, j)),
        ],
        out_specs=pl.BlockSpec((None, N_MOD_ROWS, ADA_TN), lambda l, j: (l, 0, j)),
        out_shape=jax.ShapeDtypeStruct((DEPTH, N_MOD_ROWS, 6 * D_MODEL), F32),
        compiler_params=_cparams(("arbitrary", "arbitrary")),
        name="ada_mod",
    )(cvecs, w_ada, b_ada.reshape(DEPTH, 1, 6 * D_MODEL))


IN_TM = 512
IN_COLS = 3104
Z_COL0 = 1024
Z_COLS = 2 * GLA_RANK
W_MAIN = 3072


def _piece_specs(pieces, tm, width):
    specs, t0 = [], 0
    for arr in pieces:
        nt = arr.shape[0] // tm
        specs.append(pl.BlockSpec((tm, width), lambda i, t0=t0, nt=nt: (jnp.clip(i - t0, 0, nt - 1), 0)))
        t0 += nt
    assert t0 * tm == T_ALL and len(pieces) in (1, 2) and (len(pieces) == 1 or pieces[0].shape[0] == T_PROMPT)
    return specs


def _pick_piece(refs, tm):
    if len(refs) == 1:
        return refs[0][...]
    return jnp.where(pl.program_id(0) < T_PROMPT // tm, refs[0][...], refs[1][...])


def _in_proj_kernel(*refs, n_x):
    x_refs = refs[:n_x]
    (nrm_ref, sh_ref, sc_ref, w_ref, wup_ref, bup_ref, ws_ref, bs_ref,
     g4_ref, la_ref, qb_ref, kb_ref, vb_ref, oc_ref, wm_ref, wz_ref) = refs[n_x:]

    @pl.when(pl.program_id(0) == 0)
    def _():
        wm_ref[0:Z_COL0, :] = w_ref[0:Z_COL0, :].astype(BF16)
        wm_ref[Z_COL0:W_MAIN, :] = w_ref[Z_COL0 + Z_COLS:IN_COLS, :].astype(BF16)
        wz_ref[...] = jnp.zeros_like(wz_ref)
        wz_ref[0:Z_COLS, :] = w_ref[Z_COL0:Z_COL0 + Z_COLS, :].astype(BF16)

    x = _pick_piece(x_refs, IN_TM)
    y = x * lax.rsqrt(jnp.mean(x * x, axis=-1, keepdims=True) + EPS)
    h = (y * nrm_ref[...]) * (1.0 + sc_ref[...]) + sh_ref[...]
    hb = h.astype(BF16)
    def plain_job(ref, c0, w0):
        def job():
            ref[:, c0:c0 + MXU_N] = _dot_nt(hb, wm_ref[w0:w0 + MXU_N, :])
        return job

    jobs = [plain_job(ref, c, w0 + c) for ref, w0, width in
            ((g4_ref, 0, 1024), (qb_ref, 1024, 512), (kb_ref, 1536, 512), (vb_ref, 2048, 512))
            for c in range(0, width, MXU_N)]
    z = _dot_nt(hb, wz_ref[...])
    jobs.pop(0)()
    zz = _dot(z.astype(BF16), wup_ref[...]) + bup_ref[...]
    uv = _dot_nt(hb, wm_ref[2560:3072, :])
    jobs.pop(0)()
    la_ref[...] = _log_sigmoid(zz) * (1.0 / GLA_TAU)
    jobs.pop(0)()
    oc_ref[...] = _sgu(uv, ws_ref, bs_ref, jobs)


def _resident_layer(a, l):
    return pl.BlockSpec((None,) + a.shape[1:], lambda *_: (l,) + (0,) * (a.ndim - 1), pipeline_mode=pl.Buffered(1))


def _in_proj_call(x_pieces, norm_g, mod4, w_in_t, l, w_up, b_up, sgu_ws, sgu_bs):
    tm = IN_TM
    n = T_ALL // tm
    row = functools.partial(_mod_row, tm=tm)
    mod_spec = lambda k: pl.BlockSpec((None, None, 1, D_MODEL), lambda i: (row(i), k, 0, 0))
    full = lambda a: pl.BlockSpec(a.shape, lambda i: (0,) * a.ndim)
    out = lambda w: pl.BlockSpec((tm, w), lambda i: (i, 0))
    return pl.pallas_call(
        functools.partial(_in_proj_kernel, n_x=len(x_pieces)),
        grid=(n,),
        in_specs=_piece_specs(x_pieces, tm, D_MODEL) + [full(norm_g), mod_spec(0), mod_spec(1), _resident_layer(w_in_t, l),
                                                        full(w_up), full(b_up), full(sgu_ws), full(sgu_bs)],
        out_specs=[out(1024), out(512), out(512), out(512), out(512), out(W_C)],
        out_shape=[jax.ShapeDtypeStruct((T_ALL, w), F32) for w in (1024, 512, 512, 512, 512)]
        + [jax.ShapeDtypeStruct((T_ALL, W_C), BF16)],
        scratch_shapes=[pltpu.VMEM((W_MAIN, D_MODEL), BF16), pltpu.VMEM((LANES, D_MODEL), BF16)],
        compiler_params=_cparams(("arbitrary",)),
        name="in_proj",
    )(*x_pieces, norm_g, mod4, mod4, w_in_t, w_up, b_up, sgu_ws, sgu_bs)


GLA_SB = 256
GLA_NC = GLA_SB // GLA_CHUNK


def _gla_superblock(q, k, vb, v_heads, la, st_all, tri, mask, same64, head_lanes, forward):
    c = GLA_CHUNK
    mid, last = (c // 2 - 1, c - 1) if forward else (c // 2, 0)
    la_hi, la_lo = _split_bf16(la)
    b = _dot(tri, la_hi) + _dot(tri, la_lo)
    yield
    rows_of = lambda r: jnp.concatenate(
        [jnp.broadcast_to(b[i * c + r:i * c + r + 1, :], (c, W_A)) for i in range(GLA_NC)], axis=0)
    m = rows_of(mid)
    bl = rows_of(last)
    qe = (q * jnp.exp(b - m)).astype(BF16)
    ke = k * jnp.exp(m - b)
    qi = (q * jnp.exp(b)).astype(BF16)
    ks = (k * jnp.exp(bl - b)).astype(BF16)
    o = jnp.zeros((GLA_SB, W_A), F32)
    for h in range(N_HEADS_A):
        ke_h = jnp.where(head_lanes[h], ke, 0.0).astype(BF16)
        s = _dot_nt(qe, ke_h)
        yield
        o = o + _dot(jnp.where(mask, s, 0.0).astype(BF16), v_heads[h])
        yield
    outs = [None] * GLA_NC
    for i in (range(GLA_NC) if forward else reversed(range(GLA_NC))):
        rows = slice(i * c, (i + 1) * c)
        outs[i] = o[rows, :] + _dot_nt(qi[rows, :], st_all.astype(BF16))
        kv = _dot_tn(vb[rows, :], ks[rows, :])
        yield
        st_all = st_all * jnp.exp(bl[i * c:i * c + 1, :]) + jnp.where(same64, kv, 0.0)
    return jnp.concatenate(outs, axis=0), st_all


def _interleave(*gens):
    results = [None] * len(gens)
    active = list(enumerate(gens))
    while active:
        for item in list(active):
            try:
                next(item[1])
            except StopIteration as stop:
                results[item[0]] = stop.value
                active.remove(item)
    return results


def _cast_rider(w, n_steps):
    e, r, c = w.shape
    per = n_steps // e
    rows = r // per
    assert per * e == n_steps and rows * per == r and rows % 16 == 0
    spec = pl.BlockSpec((None, rows, c), lambda b: (b // per, b % per, 0))
    return spec, jax.ShapeDtypeStruct(w.shape, BF16)


def _gla_kernel(*refs, seq, rider):
    if rider:
        (g4_ref, la_ref, s0f_ref, s0b_ref, gain_ref, w_ref, o_ref, sf_ref, sb_ref, wb_ref, of_ref, ob_ref) = refs
        wb_ref[...] = w_ref[...].astype(BF16)
    else:
        g4_ref, la_ref, s0f_ref, s0b_ref, gain_ref, o_ref, sf_ref, sb_ref, of_ref, ob_ref = refs
    n = GLA_SB
    nsb = seq // n
    r = lax.broadcasted_iota(jnp.int32, (n, n), 0)
    s = lax.broadcasted_iota(jnp.int32, (n, n), 1)
    same64 = (r // GLA_CHUNK) == (s // GLA_CHUNK)
    lower = same64 & (s <= r)
    upper = same64 & (s >= r)
    tri_f = jnp.where(lower, 1.0, 0.0).astype(BF16)
    tri_b = jnp.where(upper, 1.0, 0.0).astype(BF16)
    ones64 = jnp.where(same64, 1.0, 0.0).astype(BF16)
    head_lanes = [(s // DK_A) == h for h in range(N_HEADS_A)]
    scale = DK_A ** -0.5
    expand = lambda st: jnp.where(same64, jnp.concatenate([st] * N_HEADS_A, axis=0), 0.0)
    compact = lambda st_all: functools.reduce(
        lambda a, b: a + b, [st_all[h * 64:(h + 1) * 64, :] for h in range(N_HEADS_A)])

    def step(i, carry):
        stf, stb = carry
        rf = pl.ds(pl.multiple_of(i * n, n), n)
        rb = pl.ds(pl.multiple_of((nsb - 1 - i) * n, n), n)
        def direction(rows, la_cols, st, tri, mask, forward):
            q = g4_ref[rows, 0:256] * scale
            k = g4_ref[rows, 256:512]
            v = g4_ref[rows, 512:768]
            vb = v.astype(BF16)
            v_heads = [jnp.where(head_lanes[h], v, 0.0).astype(BF16) for h in range(N_HEADS_A)]
            return _gla_superblock(q, k, vb, v_heads, la_ref[rows, la_cols], st, tri, mask, same64, head_lanes,
                                   forward)

        (o_f, stf), (o_b, stb) = _interleave(direction(rf, slice(0, 256), stf, tri_f, lower, True),
                                             direction(rb, slice(256, 512), stb, tri_b, upper, False))
        of_ref[rf, :] = o_f
        ob_ref[rb, :] = o_b
        return stf, stb

    stf, stb = lax.fori_loop(0, nsb, step, (expand(s0f_ref[...]), expand(s0b_ref[...])))
    sf_ref[...] = compact(stf)
    sb_ref[...] = compact(stb)

    gain = gain_ref[...]

    def finish(i, carry):
        rows = pl.ds(pl.multiple_of(i * n, n), n)
        o = of_ref[rows, :] + ob_ref[rows, :]
        sq_hi, sq_lo = _split_bf16(o * o)
        ms = (_dot(sq_hi, ones64) + _dot(sq_lo, ones64)) * (1.0 / DK_A)
        y = (o * lax.rsqrt(ms + EPS)) * gain
        o_ref[rows, :] = (y * _silu(g4_ref[rows, 768:1024])).astype(BF16)
        return carry

    lax.fori_loop(0, nsb, finish, 0)


def _gla_call(g4, la, s0f, s0b, gain, cast_w=None, *, batch, seq, row_block0):
    tok = lambda w: pl.BlockSpec((seq, w), lambda b: (row_block0 + b, 0))
    st = pl.BlockSpec((None, 64, 256), lambda b: (b, 0, 0))
    in_specs = [tok(1024), tok(512), st, st, pl.BlockSpec((1, W_A), lambda b: (0, 0))]
    out_specs = [pl.BlockSpec((seq, W_A), lambda b: (b, 0)), st, st]
    out_shape = [jax.ShapeDtypeStruct((batch * seq, W_A), BF16),
                 jax.ShapeDtypeStruct((batch, 64, 256), F32),
                 jax.ShapeDtypeStruct((batch, 64, 256), F32)]
    args = [g4, la, s0f, s0b, gain]
    if cast_w is not None:
        spec, shape = _cast_rider(cast_w, batch)
        in_specs.append(spec)
        out_specs.append(spec)
        out_shape.append(shape)
        args.append(cast_w)
    return pl.pallas_call(
        functools.partial(_gla_kernel, seq=seq, rider=cast_w is not None),
        grid=(batch,),
        in_specs=in_specs,
        out_specs=out_specs,
        out_shape=out_shape,
        scratch_shapes=[pltpu.VMEM((seq, W_A), F32), pltpu.VMEM((seq, W_A), F32)],
        compiler_params=_cparams(("arbitrary",)),
        name=f"gla_{seq}",
    )(*args)


def _state_to_kernel(s):
    b = s.shape[0]
    return jnp.transpose(s, (0, 3, 1, 2)).reshape(b, 64, 256)


def _state_from_kernel(st):
    b = st.shape[0]
    return jnp.transpose(st.reshape(b, 64, N_HEADS_A, DK_A), (0, 2, 3, 1))


def _lambda(lv, lam_init):
    l01 = jnp.sum(lv[0:1, :] * lv[1:2, :], axis=-1, keepdims=True)
    l23 = jnp.sum(lv[2:3, :] * lv[3:4, :], axis=-1, keepdims=True)
    return jnp.exp(l01) - jnp.exp(l23) + lam_init


def _softmax_parts(parts):
    mx = functools.reduce(jnp.maximum, [jnp.max(p, axis=-1, keepdims=True) for p in parts])
    es = [jnp.exp(p - mx) for p in parts]
    den = functools.reduce(lambda a, b: a + b, [jnp.sum(e, axis=-1, keepdims=True) for e in es])
    return [e / den for e in es]


def _diff_finish(o, gain, lam_init):
    o = o * lax.rsqrt(jnp.mean(o * o, axis=-1, keepdims=True) + EPS)
    return ((o * gain) * (1.0 - lam_init)).astype(BF16)


QK_SCALE = DH_B ** -0.5


def _key_halves(k):
    first = lax.broadcasted_iota(jnp.int32, k.shape, 1) < DH_B
    return jnp.where(first, k, 0.0).astype(BF16), jnp.where(first, 0.0, k).astype(BF16)


def _attn_prompt_kernel(lv_ref, q_ref, k_ref, v_ref, gain_ref, *rest, lam_init, n_prev, rider):
    rest = list(rest)
    if rider:
        wb_ref = rest.pop()
        w_ref = rest.pop(2 * n_prev)
        wb_ref[...] = w_ref[...].astype(BF16)
    prev_refs, (o_ref, *cache_refs) = rest[:2 * n_prev], rest[2 * n_prev:]
    lam = _lambda(lv_ref[...], lam_init)

    def head(h):
        cols = slice(h * DV_B, (h + 1) * DV_B)
        q = (q_ref[:, cols] * QK_SCALE).astype(BF16)
        k1, k2 = _key_halves(k_ref[:, cols])
        s1 = _dot_nt(q, k1)
        s2 = _dot_nt(q, k2)
        yield
        (p1,) = _softmax_parts([s1])
        (p2,) = _softmax_parts([s2])
        a = p1 - lam * p2
        o = _dot(a.astype(BF16), v_ref[:, cols].astype(BF16))
        yield
        o_ref[:, cols] = _diff_finish(o, gain_ref[:, cols], lam_init)
        if cache_refs:
            ck_ref, cv_ref = cache_refs
            layers_k = [*prev_refs[0::2], k_ref]
            layers_v = [*prev_refs[1::2], v_ref]
            for l in range(n_prev + 1):
                ck_ref[l, pl.ds(h, SEQ, stride=N_HEADS_B), :] = layers_k[l][:, cols]
                cv_ref[l, pl.ds(h, SEQ, stride=N_HEADS_B), :] = layers_v[l][:, cols]

    _interleave(*[head(h) for h in range(N_HEADS_B)])


def _attn_prompt_call(lv, qb, kb, vb, gain, prev_kv=(), cast_w=None, *, lam_init, write_cache=False):
    blk = pl.BlockSpec((SEQ, W_B), lambda b: (b, 0))
    n_prev = len(prev_kv)
    assert write_cache or not prev_kv
    in_specs = [pl.BlockSpec((4, DH_B), lambda b: (0, 0)), blk, blk, blk,
                pl.BlockSpec((1, W_B), lambda b: (0, 0))] + [blk] * (2 * n_prev)
    args = [lv, qb, kb, vb, gain, *[a for kv in prev_kv for a in kv]]
    out_specs = [blk]
    out_shape = [jax.ShapeDtypeStruct((T_PROMPT, W_B), BF16)]
    if write_cache:
        cache = jax.ShapeDtypeStruct((BATCH, n_prev + 1, SEQ * N_HEADS_B, DV_B), F32)
        cblk = pl.BlockSpec((None, n_prev + 1, SEQ * N_HEADS_B, DV_B), lambda b: (b, 0, 0, 0))
        out_specs += [cblk, cblk]
        out_shape += [cache, cache]
    if cast_w is not None:
        spec, shape = _cast_rider(cast_w, BATCH)
        in_specs.append(spec)
        out_specs.append(spec)
        out_shape.append(shape)
        args.append(cast_w)
    return pl.pallas_call(
        functools.partial(_attn_prompt_kernel, lam_init=lam_init, n_prev=n_prev, rider=cast_w is not None),
        grid=(BATCH,),
        in_specs=in_specs,
        out_specs=out_specs,
        out_shape=out_shape,
        compiler_params=_cparams(("arbitrary",)),
        name="diff_attn_prompt",
    )(*args)


def _rope(x, cos, sin_signed):
    lane = lax.broadcasted_iota(jnp.int32, x.shape, 1)
    first = (lane % (2 * AXIS_PAIRS)) < AXIS_PAIRS
    partner = jnp.where(first, pltpu.roll(x, LANES - AXIS_PAIRS, 1), pltpu.roll(x, AXIS_PAIRS, 1))
    return x * cos + partner * sin_signed


ATT_TQ = 256


def _attn_sample_kernel(lv_ref, q_ref, k_ref, v_ref, kc_ref, vc_ref, cosq_ref, sinq_ref,
                        cosk_ref, sink_ref, gain_ref, o_ref, k1_ref, k2_ref, *, lam_init):
    @pl.when(pl.program_id(1) == 0)
    def _():
        for h in range(N_HEADS_B):
            cols = slice(h * DV_B, (h + 1) * DV_B)
            k1_ref[:, cols], k2_ref[:, cols] = _key_halves(_rope(k_ref[:, cols], cosk_ref[...], sink_ref[...]))

    lam = _lambda(lv_ref[...], lam_init)

    def head(h):
        cols = slice(h * DV_B, (h + 1) * DV_B)
        q = (_rope(q_ref[:, cols], cosq_ref[...], sinq_ref[...]) * QK_SCALE).astype(BF16)
        c1, c2 = _key_halves(kc_ref[:, cols])
        s1 = [_dot_nt(q, k1_ref[:, cols]), _dot_nt(q, c1)]
        s2 = [_dot_nt(q, k2_ref[:, cols]), _dot_nt(q, c2)]
        yield
        p1 = _softmax_parts(s1)
        p2 = _softmax_parts(s2)
        a_own = p1[0] - lam * p2[0]
        a_ctx = p1[1] - lam * p2[1]
        o = (_dot(a_own.astype(BF16), v_ref[:, cols].astype(BF16))
             + _dot(a_ctx.astype(BF16), vc_ref[:, cols].astype(BF16)))
        yield
        o_ref[:, cols] = _diff_finish(o, gain_ref[:, cols], lam_init)

    _interleave(*[head(h) for h in range(N_HEADS_B)])


def _attn_sample_call(lv, qb, kb, vb, kc, vc, cos, sin_signed, gain, *, lam_init):
    tq = ATT_TQ
    nq = DEC_SEQ // tq
    p0 = T_PROMPT // tq
    s0 = T_PROMPT // DEC_SEQ
    qblk = pl.BlockSpec((tq, W_B), lambda b, t: (p0 + b * nq + t, 0))
    kvblk = pl.BlockSpec((DEC_SEQ, W_B), lambda b, t: (s0 + b, 0))
    cblk = pl.BlockSpec((None, PAST_LEN, W_B), lambda b, t: (b, 0, 0))
    return pl.pallas_call(
        functools.partial(_attn_sample_kernel, lam_init=lam_init),
        grid=(DEC_BATCH, nq),
        in_specs=[pl.BlockSpec((4, DH_B), lambda b, t: (0, 0)), qblk, kvblk, kvblk, cblk, cblk,
                  pl.BlockSpec((tq, DV_B), lambda b, t: (t, 0)),
                  pl.BlockSpec((tq, DV_B), lambda b, t: (t, 0)),
                  pl.BlockSpec((DEC_SEQ, DV_B), lambda b, t: (0, 0)),
                  pl.BlockSpec((DEC_SEQ, DV_B), lambda b, t: (0, 0)),
                  pl.BlockSpec((1, W_B), lambda b, t: (0, 0))],
        out_specs=pl.BlockSpec((tq, W_B), lambda b, t: (b * nq + t, 0)),
        out_shape=jax.ShapeDtypeStruct((T_SAMPLE, W_B), BF16),
        scratch_shapes=[pltpu.VMEM((DEC_SEQ, W_B), BF16), pltpu.VMEM((DEC_SEQ, W_B), BF16)],
        compiler_params=_cparams(("arbitrary", "arbitrary")),
        name="diff_attn_sample",
    )(lv, qb, kb, vb, kc, vc, cos, sin_signed, cos, sin_signed, gain)


def _rope_tables():
    rows = DEC_SEQ // GRID_W
    row = jnp.repeat(jnp.arange(rows, dtype=F32), GRID_W)
    col = jnp.tile(jnp.arange(GRID_W, dtype=F32), rows)
    freqs = ROPE_THETA ** (-jnp.arange(AXIS_PAIRS, dtype=F32) / AXIS_PAIRS)
    ar, ac = row[:, None] * freqs, col[:, None] * freqs
    cos64 = jnp.concatenate([jnp.cos(ar), jnp.cos(ar), jnp.cos(ac), jnp.cos(ac)], axis=-1)
    sin64 = jnp.concatenate([-jnp.sin(ar), jnp.sin(ar), -jnp.sin(ac), jnp.sin(ac)], axis=-1)
    return jnp.tile(cos64, (1, 2)), jnp.tile(sin64, (1, 2))


def _group_mean(x, ones64):
    hi, lo = _split_bf16(x)
    return (_dot(hi, ones64) + _dot(lo, ones64)) * (1.0 / DG_C)


def _sgu(uv, ws_ref, bs_ref, fillers=()):
    fillers = list(fillers)
    fill = lambda: fillers.pop(0)() if fillers else None
    r = lax.broadcasted_iota(jnp.int32, (W_C, W_C), 0)
    s = lax.broadcasted_iota(jnp.int32, (W_C, W_C), 1)
    ones64 = jnp.where((r // DG_C) == (s // DG_C), 1.0, 0.0).astype(BF16)
    lane = lax.broadcasted_iota(jnp.int32, (SGU_CHUNK, W_C), 1)
    outs = []
    for n in range(uv.shape[0] // SGU_CHUNK):
        rows = slice(n * SGU_CHUNK, (n + 1) * SGU_CHUNK)
        u = _gelu_tanh(uv[rows, 0:256])
        v = _gelu_tanh(uv[rows, 256:512])
        mu = _group_mean(v, ones64)
        fill()
        d = v - mu
        var = _group_mean(d * d, ones64)
        fill()
        vn = d * lax.rsqrt(var + EPS)
        s_mix = bs_ref[...]
        for g in range(N_GROUPS_C):
            vn_g = jnp.where((lane // DG_C) == g, vn, 0.0).astype(BF16)
            s_mix = s_mix + _dot(ws_ref[g], vn_g)
        fill()
        outs.append((u * s_mix).astype(BF16))
    while fillers:
        fill()
    return jnp.concatenate(outs, axis=0)


OUT_TM = 512


SEL_LANE0 = N_EXPERTS


def _top2_gates(logits):
    lane = lax.broadcasted_iota(jnp.int32, logits.shape, 1).astype(F32)
    neg = -jnp.inf
    lg = jnp.where(lane < N_EXPERTS, logits, neg)
    m1 = jnp.max(lg, axis=-1, keepdims=True)
    i1 = jnp.min(jnp.where(lg == m1, lane, float(LANES)), axis=-1, keepdims=True)
    lg2 = jnp.where(lane == i1, neg, lg)
    m2 = jnp.max(lg2, axis=-1, keepdims=True)
    i2 = jnp.min(jnp.where(lg2 == m2, lane, float(LANES)), axis=-1, keepdims=True)
    e2 = jnp.exp(m2 - m1)
    den = 1.0 + e2
    gates = jnp.where(lane == i1, 1.0 / den, 0.0) + jnp.where(lane == i2, e2 / den, 0.0)
    sel = jnp.where((lane == i1 + SEL_LANE0) | (lane == i2 + SEL_LANE0), 1.0, 0.0)
    return gates + sel


def _store_token_tiles(ref, val):
    n = val.shape[0]
    for k in range(D_MODEL // LANES):
        ref[pl.ds(k, n, stride=8), :] = val[:, k * LANES:(k + 1) * LANES]


def _load_token_tiles(ref, n):
    return jnp.concatenate([ref[pl.ds(k, n, stride=8), :] for k in range(D_MODEL // LANES)], axis=-1)


def _out_proj_kernel(*refs, n_x, moe):
    oa_refs, ob_refs, (oc_ref,), x_refs = refs[0:2], refs[2:4], refs[4:5], refs[5:5 + n_x]
    w_ref, g1_ref, nrm_ref, sc_ref, sh_ref, *rest = refs[5 + n_x:]
    if moe:
        rw_ref, x1_ref, h2t_ref, gates_ref, gates_t_ref, wo_ref = rest
    else:
        x1_ref, h2_ref, wo_ref = rest

    @pl.when(pl.program_id(0) == 0)
    def _():
        wo_ref[...] = w_ref[...].astype(BF16)

    y = (_dot(_pick_piece(oa_refs, OUT_TM), wo_ref[0:256, :]) + _dot(_pick_piece(ob_refs, OUT_TM), wo_ref[256:768, :])
         + _dot(oc_ref[...], wo_ref[768:1024, :]))
    x1 = _pick_piece(x_refs, OUT_TM) + g1_ref[...] * y
    x1_ref[...] = x1
    yn = x1 * lax.rsqrt(jnp.mean(x1 * x1, axis=-1, keepdims=True) + EPS)
    h = (yn * nrm_ref[...]) * (1.0 + sc_ref[...]) + sh_ref[...]
    if moe:
        _store_token_tiles(h2t_ref, h)
        gates = _top2_gates(_dot3(h, rw_ref[...]))
        gates_ref[...] = gates
        gates_t_ref[...] = gates.T
    else:
        h2_ref[...] = h.astype(BF16)


def _out_proj_call(oa_pieces, ob_pieces, oc, x_pieces, w_out, l, mod4, norm_g, router_w=None):
    tm = OUT_TM
    n = T_ALL // tm
    moe = router_w is not None
    row = functools.partial(_mod_row, tm=tm)
    mod_spec = lambda k: pl.BlockSpec((None, None, 1, D_MODEL), lambda i: (row(i), k, 0, 0))
    tok = lambda w: pl.BlockSpec((tm, w), lambda i: (i, 0))
    full = lambda a: pl.BlockSpec(a.shape, lambda i: (0,) * a.ndim)
    in_specs = (_piece_specs(oa_pieces, tm, W_A) + _piece_specs(ob_pieces, tm, W_B) + [tok(W_C)]
                + _piece_specs(x_pieces, tm, D_MODEL)
                + [_resident_layer(w_out, l), mod_spec(2), full(norm_g), mod_spec(4), mod_spec(3)])
    args = [*oa_pieces, *ob_pieces, oc, *x_pieces, w_out, mod4, norm_g, mod4, mod4]
    if moe:
        in_specs.append(full(router_w))
        args.append(router_w)
        out_specs = [tok(D_MODEL), pl.BlockSpec((tm * 8, LANES), lambda i: (i, 0)), tok(LANES),
                     pl.BlockSpec((LANES, tm), lambda i: (0, i))]
        out_shape = [jax.ShapeDtypeStruct((T_ALL, D_MODEL), F32), jax.ShapeDtypeStruct((T_ALL * 8, LANES), F32),
                     jax.ShapeDtypeStruct((T_ALL, LANES), F32), jax.ShapeDtypeStruct((LANES, T_ALL), F32)]
    else:
        out_specs = [tok(D_MODEL), tok(D_MODEL)]
        out_shape = [jax.ShapeDtypeStruct((T_ALL, D_MODEL), F32), jax.ShapeDtypeStruct((T_ALL, D_MODEL), BF16)]
    return pl.pallas_call(
        functools.partial(_out_proj_kernel, n_x=len(x_pieces), moe=moe),
        grid=(n,),
        in_specs=in_specs,
        out_specs=out_specs,
        out_shape=out_shape,
        scratch_shapes=[pltpu.VMEM((D_MODEL, D_MODEL), BF16)],
        compiler_params=_cparams(("arbitrary",)),
        name="out_proj_moe" if moe else "out_proj",
    )(*args)


FFN_TM = 512
MXU_N = 256
FFN_SPLITS = (0, 1024, 2048, D_FF)
assert all(s % MXU_N == 0 for s in FFN_SPLITS)


def _swiglu(h, wg_ref, wu_ref, wd_ref):
    out = None
    for c0, c1 in zip(FFN_SPLITS[:-1], FFN_SPLITS[1:]):
        act = _silu(_dot(h, wg_ref[:, c0:c1])) * _dot(h, wu_ref[:, c0:c1])
        d = _dot(act.astype(BF16), wd_ref[c0:c1, :])
        out = d if out is None else out + d
    return out


def _ffn_kernel(h_ref, x_ref, g2_ref, wg_ref, wu_ref, wd_ref, o_ref):
    o_ref[...] = x_ref[...] + g2_ref[...] * _swiglu(h_ref[...], wg_ref, wu_ref, wd_ref)


def _ffn_call(h2, x1, mod4, wg, wu, wd):
    tm = FFN_TM
    row = functools.partial(_mod_row, tm=tm)
    tok = lambda w: pl.BlockSpec((tm, w), lambda i: (i, 0))
    resident = lambda a: pl.BlockSpec(a.shape, lambda i: (0, 0), pipeline_mode=pl.Buffered(1))
    return pl.pallas_call(
        _ffn_kernel,
        grid=(T_ALL // tm,),
        in_specs=[tok(D_MODEL), tok(D_MODEL),
                  pl.BlockSpec((None, None, 1, D_MODEL), lambda i: (row(i), 5, 0, 0)),
                  resident(wg), resident(wu), resident(wd)],
        out_specs=tok(D_MODEL),
        out_shape=jax.ShapeDtypeStruct((T_ALL, D_MODEL), F32),
        compiler_params=_cparams(("arbitrary",)),
        name="ffn_dense",
    )(h2, x1, mod4, wg, wu, wd)


MOE_TM = 512
MOE_NT_MAX = (2 * T_ALL) // MOE_TM + N_EXPERTS
MOE_ROWS = MOE_NT_MAX * MOE_TM
PLAN_BLK = 512
MISC_LAST_START = 8
MISC_NT = 16


def _moe_plan_kernel(gt_ref, posa_ref, posb_ref, te_ref, ti_ref, misc_ref):
    tm = float(MOE_TM)
    sel = gt_ref[SEL_LANE0:SEL_LANE0 + N_EXPERTS, :]
    cnt = jnp.sum(sel, axis=1, keepdims=True)
    nt = jnp.floor((cnt + (tm - 1.0)) * (1.0 / tm))
    sub = lax.broadcasted_iota(jnp.int32, (N_EXPERTS, LANES), 0).astype(F32)
    lane = lax.broadcasted_iota(jnp.int32, (N_EXPERTS, LANES), 1).astype(F32)
    nt_b = jnp.broadcast_to(nt, (N_EXPERTS, LANES))
    nt_row = jnp.sum(jnp.where(sub == lane, nt_b, 0.0), axis=0, keepdims=True)
    toff = jnp.sum(jnp.where(lane < sub, jnp.broadcast_to(nt_row, (N_EXPERTS, LANES)), 0.0),
                   axis=1, keepdims=True)
    tend = toff + nt
    n_total = jnp.sum(nt, axis=0, keepdims=True)
    jc = jnp.minimum(lane, n_total - 1.0)
    te = jnp.sum(jnp.where(jc >= tend, 1.0, 0.0), axis=0, keepdims=True)
    te_ref[...] = te.astype(jnp.int32)
    ti_ref[...] = jc[0:1, :].astype(jnp.int32)
    last_start = (tend - 1.0) * tm
    ls_row = jnp.sum(jnp.where(sub + MISC_LAST_START == lane, jnp.broadcast_to(last_start, (N_EXPERTS, LANES)), 0.0),
                     axis=0, keepdims=True)
    nt_row2 = jnp.sum(jnp.where(sub + MISC_NT == lane, nt_b, 0.0), axis=0, keepdims=True)
    misc = jnp.where(lane[0:1, :] == 0.0, n_total, 0.0) + ls_row + nt_row2
    misc_ref[...] = misc.astype(jnp.int32)

    off = toff * tm
    r = lax.broadcasted_iota(jnp.int32, (PLAN_BLK, PLAN_BLK), 0)
    c = lax.broadcasted_iota(jnp.int32, (PLAN_BLK, PLAN_BLK), 1)
    upper = jnp.where(r <= c, 1.0, 0.0).astype(BF16)
    carry = jnp.zeros((N_EXPERTS, 1), F32)
    for blk in range(T_ALL // PLAN_BLK):
        cols = slice(blk * PLAN_BLK, (blk + 1) * PLAN_BLK)
        s = gt_ref[SEL_LANE0:SEL_LANE0 + N_EXPERTS, cols]
        rank = _dot(s.astype(BF16), upper) + carry
        pos = off + rank - 1.0
        posa_ref[:, cols] = jnp.min(jnp.where(s > 0.0, pos, 1e9), axis=0, keepdims=True).astype(jnp.int32)
        posb_ref[:, cols] = jnp.max(jnp.where(s > 0.0, pos, -1.0), axis=0, keepdims=True).astype(jnp.int32)
        carry = carry + jnp.sum(s, axis=1, keepdims=True)


def _moe_plan_call(gates_t):
    row = lambda w: jax.ShapeDtypeStruct((1, w), jnp.int32)
    full = lambda w: pl.BlockSpec((1, w), lambda: (0, 0))
    return pl.pallas_call(
        _moe_plan_kernel,
        in_specs=[pl.BlockSpec((LANES, T_ALL), lambda: (0, 0))],
        out_specs=[full(T_ALL), full(T_ALL), full(LANES), full(LANES), full(LANES)],
        out_shape=[row(T_ALL), row(T_ALL), row(LANES), row(LANES), row(LANES)],
        compiler_params=pltpu.CompilerParams(vmem_limit_bytes=VMEM_LIMIT),
        name="moe_plan",
    )(gates_t)


DMA_UNROLL = 8


def _row_tile(ref, row):
    return ref.at[pl.ds(pl.multiple_of(row * 8, 8), 8), :]


def _moe_scatter_kernel(misc_ref, posa_ref, posb_ref, h_ref, xs_ref, zero_ref, sem):
    tm = h_ref.shape[0] // 8

    @pl.when(pl.program_id(0) == 0)
    def _():
        zero_ref[...] = jnp.zeros_like(zero_ref)

        def zero_tile(first_row):
            start = pl.multiple_of(first_row * 8, 8)
            cp = pltpu.make_async_copy(zero_ref, xs_ref.at[pl.ds(start, MOE_TM * 8), :], sem.at[0])
            cp.start()
            cp.wait()

        for e in range(N_EXPERTS):
            @pl.when(misc_ref[0, MISC_NT + e] > 0)
            def _():
                zero_tile(misc_ref[0, MISC_LAST_START + e])

        def zero_tail(j, carry):
            zero_tile(j * MOE_TM)
            return carry

        lax.fori_loop(misc_ref[0, 0], MOE_NT_MAX, zero_tail, 0)

    def issue(r, carry):
        src = _row_tile(h_ref, r)
        pltpu.make_async_copy(src, _row_tile(xs_ref, posa_ref[0, r]), sem.at[0]).start(priority=0)
        pltpu.make_async_copy(src, _row_tile(xs_ref, posb_ref[0, r]), sem.at[1]).start(priority=1)
        return carry

    lax.fori_loop(0, tm, issue, 0, unroll=DMA_UNROLL)
    for k in range(2):
        pltpu.make_async_copy(h_ref, xs_ref.at[pl.ds(0, tm * 8), :], sem.at[k]).wait()


SCATTER_TM = 512


def _moe_scatter_call(misc, posa3, posb3, h2t):
    tm = SCATTER_TM
    smem_row = pl.BlockSpec((None, 1, tm), lambda i: (i, 0, 0), memory_space=pltpu.SMEM)
    return pl.pallas_call(
        _moe_scatter_kernel,
        grid=(T_ALL // tm,),
        in_specs=[pl.BlockSpec((1, LANES), lambda i: (0, 0), memory_space=pltpu.SMEM), smem_row, smem_row,
                  pl.BlockSpec((tm * 8, LANES), lambda i: (i, 0))],
        out_specs=pl.BlockSpec(memory_space=pl.ANY),
        out_shape=jax.ShapeDtypeStruct((MOE_ROWS * 8, LANES), F32),
        scratch_shapes=[pltpu.VMEM((MOE_TM * 8, LANES), F32), pltpu.SemaphoreType.DMA((2,))],
        compiler_params=_cparams(("arbitrary",)),
        name="moe_scatter",
    )(misc, posa3, posb3, h2t)


def _ffn_grouped_kernel(te_ref, ti_ref, misc_ref, x_ref, wg_ref, wu_ref, wd_ref, o_ref):
    j = pl.program_id(0)

    @pl.when(j < misc_ref[0])
    def _():
        h = _load_token_tiles(x_ref, MOE_TM).astype(BF16)
        _store_token_tiles(o_ref, _swiglu(h, wg_ref, wu_ref, wd_ref))

    @pl.when(j >= misc_ref[0])
    def _():
        o_ref[...] = jnp.zeros_like(o_ref)


def _ffn_grouped_call(te, ti, misc, xs, wg, wu, wd):
    expert = lambda *s: pl.BlockSpec((None,) + s, lambda j, te, ti, misc: (te[j], 0, 0))
    grid_spec = pltpu.PrefetchScalarGridSpec(
        num_scalar_prefetch=3,
        grid=(MOE_NT_MAX,),
        in_specs=[pl.BlockSpec((MOE_TM * 8, LANES), lambda j, te, ti, misc: (ti[j], 0)),
                  expert(D_MODEL, D_FF), expert(D_MODEL, D_FF), expert(D_FF, D_MODEL)],
        out_specs=pl.BlockSpec((MOE_TM * 8, LANES), lambda j, te, ti, misc: (j, 0)),
    )
    return pl.pallas_call(
        _ffn_grouped_kernel,
        grid_spec=grid_spec,
        out_shape=jax.ShapeDtypeStruct((MOE_ROWS * 8, LANES), F32),
        compiler_params=_cparams(("arbitrary",)),
        name="ffn_grouped",
    )(te, ti, misc, xs, wg, wu, wd)


COMBINE_TM = 256


def _moe_combine_kernel(posa_ref, posb_ref, posa_next_ref, posb_next_ref, ys_ref, x_ref, g2_ref, gates_ref, nf_ref,
                        op_ref, os_ref, bufa_ref, bufb_ref, sem):
    tm = COMBINE_TM
    i = pl.program_id(0)
    n = pl.num_programs(0)
    slot = i % 2

    def gather(pa_ref, pb_ref, s):
        def issue(r, carry):
            pltpu.make_async_copy(_row_tile(ys_ref, pa_ref[0, r]), _row_tile(bufa_ref.at[s], r),
                                  sem.at[s, 0]).start(priority=0)
            pltpu.make_async_copy(_row_tile(ys_ref, pb_ref[0, r]), _row_tile(bufb_ref.at[s], r),
                                  sem.at[s, 1]).start(priority=1)
            return carry

        lax.fori_loop(0, tm, issue, 0, unroll=DMA_UNROLL)

    @pl.when(i == 0)
    def _():
        gather(posa_ref, posb_ref, 0)

    @pl.when(i + 1 < n)
    def _():
        gather(posa_next_ref, posb_next_ref, 1 - slot)

    gates = gates_ref[...]
    lane = lax.broadcasted_iota(jnp.int32, gates.shape, 1).astype(F32)
    is_sel = (lane >= SEL_LANE0) & (lane < SEL_LANE0 + N_EXPERTS) & (gates > 0.0)
    ia = jnp.min(jnp.where(is_sel, lane, float(LANES)), axis=-1, keepdims=True) - SEL_LANE0
    ib = jnp.max(jnp.where(is_sel, lane, -1.0), axis=-1, keepdims=True) - SEL_LANE0
    wa = jnp.sum(jnp.where(lane == ia, gates, 0.0), axis=-1, keepdims=True)
    wb = jnp.sum(jnp.where(lane == ib, gates, 0.0), axis=-1, keepdims=True)

    pltpu.make_async_copy(ys_ref.at[pl.ds(0, tm * 8), :], bufa_ref.at[slot], sem.at[slot, 0]).wait()
    pltpu.make_async_copy(ys_ref.at[pl.ds(0, tm * 8), :], bufb_ref.at[slot], sem.at[slot, 1]).wait()

    y = wa * _load_token_tiles(bufa_ref.at[slot], tm) + wb * _load_token_tiles(bufb_ref.at[slot], tm)
    out = x_ref[...] + g2_ref[...] * y
    out = (out * lax.rsqrt(jnp.mean(out * out, axis=-1, keepdims=True) + EPS)) * nf_ref[...]

    @pl.when(i < T_PROMPT // tm)
    def _():
        op_ref[...] = out

    @pl.when(i >= T_PROMPT // tm)
    def _():
        os_ref[...] = out


def _moe_combine_call(posa3, posb3, ys, x1, mod4, gates, norm_f):
    tm = COMBINE_TM
    n = T_ALL // tm
    n_p = T_PROMPT // tm
    row = functools.partial(_mod_row, tm=tm)
    smem_row = pl.BlockSpec((None, 1, tm), lambda i: (i, 0, 0), memory_space=pltpu.SMEM)
    smem_next = pl.BlockSpec((None, 1, tm), lambda i: (jnp.minimum(i + 1, n - 1), 0, 0), memory_space=pltpu.SMEM)
    tok = lambda w: pl.BlockSpec((tm, w), lambda i: (i, 0))
    return pl.pallas_call(
        _moe_combine_kernel,
        grid=(n,),
        in_specs=[smem_row, smem_row, smem_next, smem_next, pl.BlockSpec(memory_space=pl.ANY), tok(D_MODEL),
                  pl.BlockSpec((None, None, 1, D_MODEL), lambda i: (row(i), 5, 0, 0)), tok(LANES),
                  pl.BlockSpec((1, D_MODEL), lambda i: (0, 0))],
        out_specs=[pl.BlockSpec((tm, D_MODEL), lambda i: (jnp.minimum(i, n_p - 1), 0)),
                   pl.BlockSpec((tm, D_MODEL), lambda i: (jnp.maximum(i - n_p, 0), 0))],
        out_shape=[jax.ShapeDtypeStruct((T_PROMPT, D_MODEL), F32), jax.ShapeDtypeStruct((T_SAMPLE, D_MODEL), F32)],
        scratch_shapes=[pltpu.VMEM((2, tm * 8, LANES), F32), pltpu.VMEM((2, tm * 8, LANES), F32),
                        pltpu.SemaphoreType.DMA((2, 2))],
        compiler_params=_cparams(("arbitrary",)),
        name="moe_combine",
    )(posa3, posb3, posa3, posb3, ys, x1, mod4, gates, norm_f)


def _moe_call(h2t, x1, mod4, gates, gates_t, wg, wu, wd, norm_f):
    posa, posb, te, ti, misc = _moe_plan_call(gates_t)
    xs = _moe_scatter_call(misc, posa.reshape(T_ALL // SCATTER_TM, 1, SCATTER_TM),
                           posb.reshape(T_ALL // SCATTER_TM, 1, SCATTER_TM), h2t)
    ys = _ffn_grouped_call(te.reshape(LANES), ti.reshape(LANES), misc.reshape(LANES), xs, wg, wu, wd)
    return _moe_combine_call(posa.reshape(T_ALL // COMBINE_TM, 1, COMBINE_TM),
                             posb.reshape(T_ALL // COMBINE_TM, 1, COMBINE_TM), ys, x1, mod4, gates, norm_f)


def kernel(x_prompt, x_sample, cache_k, cache_v, state_gla_fwd, state_gla_bwd, c, c_ctx, w_ada, b_ada, norm_mix, norm_ffn, w_in, w_out, gla_w_up, gla_b_up, gla_norm, diff_lambda, diff_norm, sgu_w, sgu_b, ffn_w_gate, ffn_w_up, ffn_w_down, router_w, moe_w_gate, moe_w_up, moe_w_down, norm_f):
    assert DEPTH == 2
    x_pieces = [x_prompt.reshape(T_PROMPT, D_MODEL), x_sample.reshape(T_SAMPLE, D_MODEL)]
    cvecs = jnp.concatenate([c_ctx[None, :], c, jnp.zeros((N_MOD_ROWS - 1 - DEC_BATCH, D_MODEL), F32)], axis=0)
    mod = _ada_call(cvecs, w_ada, b_ada)
    cos, sin_signed = _rope_tables()
    zeros_state = jnp.zeros((BATCH, 64, 256), F32)

    w_in_t = jnp.swapaxes(w_in, 1, 2)
    moe_w = [moe_w_gate[0], moe_w_up[0], moe_w_down[0]]
    prev_kv, sfs, sbs = [], [], []
    for l in range(DEPTH):
        mod4 = mod[l].reshape(N_MOD_ROWS, 6, 1, D_MODEL)
        w_up = jnp.zeros((LANES, 2 * W_A), F32)
        w_up = w_up.at[0:GLA_RANK, 0:W_A].set(gla_w_up[l, 0]).at[GLA_RANK:2 * GLA_RANK, W_A:].set(gla_w_up[l, 1])
        b_up = gla_b_up[l].reshape(1, 2 * W_A)
        bs_full = jnp.repeat(sgu_b[l].T, DG_C, axis=1)
        g4, la, qb, kb, vb, oc = _in_proj_call(x_pieces, norm_mix[l][None, :], mod4, w_in_t, l, w_up.astype(BF16), b_up,
                                               sgu_w[l].astype(BF16), bs_full)

        gain_a = gla_norm[l][None, :]
        oa_p, sf, sb, moe_w[l] = _gla_call(g4, la, zeros_state, zeros_state, gain_a, moe_w[l],
                                           batch=BATCH, seq=SEQ, row_block0=0)
        oa_s, _, _ = _gla_call(g4, la, _state_to_kernel(state_gla_fwd[:, l]),
                               _state_to_kernel(state_gla_bwd[:, l]), gain_a,
                               batch=DEC_BATCH, seq=DEC_SEQ, row_block0=T_PROMPT // DEC_SEQ)

        lam_init = 0.8 - 0.6 * math.exp(-0.3 * l)
        gain_b = diff_norm[l][None, :]
        if l < DEPTH - 1:
            ob_p, moe_w[2] = _attn_prompt_call(diff_lambda[l], qb, kb, vb, gain_b, cast_w=moe_w[2], lam_init=lam_init)
            prev_kv.append((kb, vb))
        else:
            ob_p, new_cache_k, new_cache_v = _attn_prompt_call(diff_lambda[l], qb, kb, vb, gain_b, prev_kv,
                                                               lam_init=lam_init, write_cache=True)
        ob_s = _attn_sample_call(diff_lambda[l], qb, kb, vb,
                                 cache_k[:, l].reshape(DEC_BATCH, PAST_LEN, W_B),
                                 cache_v[:, l].reshape(DEC_BATCH, PAST_LEN, W_B),
                                 cos, sin_signed, gain_b, lam_init=lam_init)

        if l == 0:
            x1, h2 = _out_proj_call([oa_p, oa_s], [ob_p, ob_s], oc, x_pieces, w_out, l, mod4, norm_ffn[l][None, :])
            x_pieces = [_ffn_call(h2, x1, mod4, ffn_w_gate[0].astype(BF16), ffn_w_up[0].astype(BF16),
                                  ffn_w_down[0].astype(BF16))]
        else:
            rw = jnp.pad(router_w[0], ((0, 0), (0, LANES - N_EXPERTS)))
            x1, h2t, gates, gates_t = _out_proj_call([oa_p, oa_s], [ob_p, ob_s], oc, x_pieces, w_out, l, mod4,
                                                     norm_ffn[l][None, :], rw)
            y_prompt, y_sample = _moe_call(h2t, x1, mod4, gates, gates_t, *moe_w, norm_f[None, :])

        sfs.append(_state_from_kernel(sf))
        sbs.append(_state_from_kernel(sb))

    return (y_prompt.reshape(BATCH, SEQ, D_MODEL), y_sample.reshape(DEC_BATCH, DEC_SEQ, D_MODEL),
            new_cache_k.reshape(BATCH, DEPTH, SEQ, N_HEADS_B, DV_B), new_cache_v.reshape(BATCH, DEPTH, SEQ, N_HEADS_B, DV_B),
            jnp.stack(sfs, axis=1), jnp.stack(sbs, axis=1))
```

```python
import functools
import math

import jax
import jax.numpy as jnp
import numpy as np
from jax import lax
from jax.experimental import pallas as pl
from jax.experimental.pallas import tpu as pltpu

F32 = jnp.float32
BF16 = jnp.bfloat16

D_MODEL = 1024
BATCH = 32
SEQ = 256
DEPTH = 2
DEC_BATCH = 2
DEC_SEQ = 1024
PAST_LEN = 256
GRID_W = 64
N_HEADS_A = 4
DK_A = 64
W_A = 256
GLA_RANK = 16
GLA_TAU = 16.0
GLA_CHUNK = 64
N_HEADS_B = 4
DH_B = 64
DV_B = 128
W_B = 512
ROPE_THETA = 10000.0
AXIS_PAIRS = DH_B // 4
N_GROUPS_C = 4
DG_C = 64
W_C = 256
SGU_CHUNK = 128
D_FF = 2816
N_EXPERTS = 8
EPS = 1e-6

T_PROMPT = BATCH * SEQ
T_SAMPLE = DEC_BATCH * DEC_SEQ
T_ALL = T_PROMPT + T_SAMPLE
N_MOD_ROWS = 8
LANES = 128
VMEM_LIMIT = 56 * 1024 * 1024


def _cparams(sem):
    return pltpu.CompilerParams(dimension_semantics=sem, vmem_limit_bytes=VMEM_LIMIT)


def _dot(a, b):
    return jnp.dot(a, b, preferred_element_type=F32)


def _dot_nt(a, b):
    return lax.dot_general(a, b, (((1,), (1,)), ((), ())), preferred_element_type=F32)


def _dot_tn(a, b):
    return lax.dot_general(a, b, (((0,), (0,)), ((), ())), preferred_element_type=F32)


def _split_bf16(x):
    hi = x.astype(BF16)
    lo = (x - hi.astype(F32)).astype(BF16)
    return hi, lo


def _dot3(a, w):
    a_hi, a_lo = _split_bf16(a)
    w_hi, w_lo = _split_bf16(w)
    return _dot(a_hi, w_hi) + (_dot(a_lo, w_hi) + _dot(a_hi, w_lo))


def _sigmoid(x):
    return 1.0 / (1.0 + jnp.exp(-x))


def _silu(x):
    return x * _sigmoid(x)


def _gelu_tanh(x):
    c = math.sqrt(2.0 / math.pi)
    return x * (0.5 * (1.0 + jnp.tanh(c * (x + 0.044715 * (x * x * x)))))


def _log_sigmoid(x):
    return jnp.minimum(x, 0.0) - jnp.log(1.0 + jnp.exp(-jnp.abs(x)))


def _mod_row(i, tm):
    n_p = T_PROMPT // tm
    per_b = DEC_SEQ // tm
    return jnp.where(i < n_p, 0, 1 + (i - n_p) // per_b)


ADA_TN = 1536


def _ada_kernel(c_ref, w_ref, b_ref, o_ref):
    a = _silu(c_ref[...])
    o_ref[...] = _dot3(a, w_ref[...]) + b_ref[...]


def _ada_call(cvecs, w_ada, b_ada):
    n_col = (6 * D_MODEL) // ADA_TN
    return pl.pallas_call(
        _ada_kernel,
        grid=(DEPTH, n_col),
        in_specs=[
            pl.BlockSpec((N_MOD_ROWS, D_MODEL), lambda l, j: (0, 0)),
            pl.BlockSpec((None, D_MODEL, ADA_TN), lambda l, j: (l, 0, j)),
            pl.BlockSpec((None, 1, ADA_TN), lambda l, j: (l, 0, j)),
        ],
        out_specs=pl.BlockSpec((None, N_MOD_ROWS, ADA_TN), lambda l, j: (l, 0, j)),
        out_shape=jax.ShapeDtypeStruct((DEPTH, N_MOD_ROWS, 6 * D_MODEL), F32),
        compiler_params=_cparams(("arbitrary", "arbitrary")),
        name="ada_mod",
    )(cvecs, w_ada, b_ada.reshape(DEPTH, 1, 6 * D_MODEL))


IN_TM = 512
IN_COLS = 3104
Z_COL0 = 1024
Z_COLS = 2 * GLA_RANK
W_MAIN = 3072


def _piece_specs(pieces, tm, width):
    specs, t0 = [], 0
    for arr in pieces:
        nt = arr.shape[0] // tm
        specs.append(pl.BlockSpec((tm, width), lambda i, t0=t0, nt=nt: (jnp.clip(i - t0, 0, nt - 1), 0)))
        t0 += nt
    assert t0 * tm == T_ALL and len(pieces) in (1, 2) and (len(pieces) == 1 or pieces[0].shape[0] == T_PROMPT)
    return specs


def _pick_piece(refs, tm, rows=slice(None)):
    if len(refs) == 1:
        return refs[0][rows, :]
    return jnp.where(pl.program_id(0) < T_PROMPT // tm, refs[0][rows, :], refs[1][rows, :])


def _in_proj_kernel(*refs, n_x):
    x_refs = refs[:n_x]
    (nrm_ref, sh_ref, sc_ref, w_ref, wup_ref, bup_ref, ws_ref, bs_ref,
     g4_ref, la_ref, qb_ref, kb_ref, vb_ref, oc_ref, wm_ref, wz_ref) = refs[n_x:]

    @pl.when(pl.program_id(0) == 0)
    def _():
        wm_ref[0:Z_COL0, :] = w_ref[0:Z_COL0, :].astype(BF16)
        wm_ref[Z_COL0:W_MAIN, :] = w_ref[Z_COL0 + Z_COLS:IN_COLS, :].astype(BF16)
        wz_ref[...] = jnp.zeros_like(wz_ref)
        wz_ref[0:Z_COLS, :] = w_ref[Z_COL0:Z_COL0 + Z_COLS, :].astype(BF16)

    x = _pick_piece(x_refs, IN_TM)
    y = x * lax.rsqrt(jnp.mean(x * x, axis=-1, keepdims=True) + EPS)
    h = (y * nrm_ref[...]) * (1.0 + sc_ref[...]) + sh_ref[...]
    hb = h.astype(BF16)
    def plain_job(ref, c0, w0):
        def job():
            ref[:, c0:c0 + MXU_N] = _dot_nt(hb, wm_ref[w0:w0 + MXU_N, :])
        return job

    jobs = [plain_job(ref, c, w0 + c) for ref, w0, width in
            ((g4_ref, 0, 1024), (qb_ref, 1024, 512), (kb_ref, 1536, 512), (vb_ref, 2048, 512))
            for c in range(0, width, MXU_N)]
    z = _dot_nt(hb, wz_ref[...])
    jobs.pop(0)()
    zz = _dot(z.astype(BF16), wup_ref[...]) + bup_ref[...]
    uv = _dot_nt(hb, wm_ref[2560:3072, :])
    jobs.pop(0)()
    la_ref[...] = _log_sigmoid(zz) * (1.0 / GLA_TAU)
    jobs.pop(0)()
    oc_ref[...] = _sgu(uv, ws_ref, bs_ref, jobs)


def _resident_layer(a, l):
    return pl.BlockSpec((None,) + a.shape[1:], lambda *_: (l,) + (0,) * (a.ndim - 1), pipeline_mode=pl.Buffered(1))


def _in_proj_call(x_pieces, norm_g, mod4, w_in_t, l, w_up, b_up, sgu_ws, sgu_bs):
    tm = IN_TM
    n = T_ALL // tm
    row = functools.partial(_mod_row, tm=tm)
    mod_spec = lambda k: pl.BlockSpec((None, None, 1, D_MODEL), lambda i: (row(i), k, 0, 0))
    full = lambda a: pl.BlockSpec(a.shape, lambda i: (0,) * a.ndim)
    out = lambda w: pl.BlockSpec((tm, w), lambda i: (i, 0))
    return pl.pallas_call(
        functools.partial(_in_proj_kernel, n_x=len(x_pieces)),
        grid=(n,),
        in_specs=_piece_specs(x_pieces, tm, D_MODEL) + [full(norm_g), mod_spec(0), mod_spec(1), _resident_layer(w_in_t, l),
                                                        full(w_up), full(b_up), full(sgu_ws), full(sgu_bs)],
        out_specs=[out(1024), out(512), out(512), out(512), out(512), out(W_C)],
        out_shape=[jax.ShapeDtypeStruct((T_ALL, w), F32) for w in (1024, 512, 512, 512, 512)]
        + [jax.ShapeDtypeStruct((T_ALL, W_C), BF16)],
        scratch_shapes=[pltpu.VMEM((W_MAIN, D_MODEL), BF16), pltpu.VMEM((LANES, D_MODEL), BF16)],
        compiler_params=_cparams(("arbitrary",)),
        name="in_proj",
    )(*x_pieces, norm_g, mod4, mod4, w_in_t, w_up, b_up, sgu_ws, sgu_bs)


GLA_SB = 256
GLA_NC = GLA_SB // GLA_CHUNK


def _gla_superblock(q, k, vb, v_heads, la, st_all, tri, mask, same64, head_lanes, forward):
    c = GLA_CHUNK
    mid, last = (c // 2 - 1, c - 1) if forward else (c // 2, 0)
    la_hi, la_lo = _split_bf16(la)
    b = _dot(tri, la_hi) + _dot(tri, la_lo)
    yield
    rows_of = lambda r: jnp.concatenate(
        [jnp.broadcast_to(b[i * c + r:i * c + r + 1, :], (c, W_A)) for i in range(GLA_NC)], axis=0)
    m = rows_of(mid)
    bl = rows_of(last)
    qe = (q * jnp.exp(b - m)).astype(BF16)
    ke = k * jnp.exp(m - b)
    qi = (q * jnp.exp(b)).astype(BF16)
    ks = (k * jnp.exp(bl - b)).astype(BF16)
    o = jnp.zeros((GLA_SB, W_A), F32)
    for h in range(N_HEADS_A):
        ke_h = jnp.where(head_lanes[h], ke, 0.0).astype(BF16)
        s = _dot_nt(qe, ke_h)
        yield
        o = o + _dot(jnp.where(mask, s, 0.0).astype(BF16), v_heads[h])
        yield
    outs = [None] * GLA_NC
    for i in (range(GLA_NC) if forward else reversed(range(GLA_NC))):
        rows = slice(i * c, (i + 1) * c)
        outs[i] = o[rows, :] + _dot_nt(qi[rows, :], st_all.astype(BF16))
        kv = _dot_tn(vb[rows, :], ks[rows, :])
        yield
        st_all = st_all * jnp.exp(bl[i * c:i * c + 1, :]) + jnp.where(same64, kv, 0.0)
    return jnp.concatenate(outs, axis=0), st_all


def _interleave(*gens):
    results = [None] * len(gens)
    active = list(enumerate(gens))
    while active:
        for item in list(active):
            try:
                next(item[1])
            except StopIteration as stop:
                results[item[0]] = stop.value
                active.remove(item)
    return results


def _cast_rider(w, n_steps):
    e, r, c = w.shape
    per = n_steps // e
    rows = r // per
    assert per * e == n_steps and rows * per == r and rows % 16 == 0
    spec = pl.BlockSpec((None, rows, c), lambda b: (b // per, b % per, 0))
    return spec, jax.ShapeDtypeStruct(w.shape, BF16)


def _gla_kernel(*refs, seq, rider):
    if rider:
        (g4_ref, la_ref, s0f_ref, s0b_ref, gain_ref, w_ref, o_ref, sf_ref, sb_ref, wb_ref, of_ref, ob_ref) = refs
        wb_ref[...] = w_ref[...].astype(BF16)
    else:
        g4_ref, la_ref, s0f_ref, s0b_ref, gain_ref, o_ref, sf_ref, sb_ref, of_ref, ob_ref = refs
    n = GLA_SB
    nsb = seq // n
    r = lax.broadcasted_iota(jnp.int32, (n, n), 0)
    s = lax.broadcasted_iota(jnp.int32, (n, n), 1)
    same64 = (r // GLA_CHUNK) == (s // GLA_CHUNK)
    lower = same64 & (s <= r)
    upper = same64 & (s >= r)
    tri_f = jnp.where(lower, 1.0, 0.0).astype(BF16)
    tri_b = jnp.where(upper, 1.0, 0.0).astype(BF16)
    ones64 = jnp.where(same64, 1.0, 0.0).astype(BF16)
    head_lanes = [(s // DK_A) == h for h in range(N_HEADS_A)]
    scale = DK_A ** -0.5
    expand = lambda st: jnp.where(same64, jnp.concatenate([st] * N_HEADS_A, axis=0), 0.0)
    compact = lambda st_all: functools.reduce(
        lambda a, b: a + b, [st_all[h * 64:(h + 1) * 64, :] for h in range(N_HEADS_A)])

    def step(i, carry):
        stf, stb = carry
        rf = pl.ds(pl.multiple_of(i * n, n), n)
        rb = pl.ds(pl.multiple_of((nsb - 1 - i) * n, n), n)
        def direction(rows, la_cols, st, tri, mask, forward):
            q = g4_ref[rows, 0:256] * scale
            k = g4_ref[rows, 256:512]
            v = g4_ref[rows, 512:768]
            vb = v.astype(BF16)
            v_heads = [jnp.where(head_lanes[h], v, 0.0).astype(BF16) for h in range(N_HEADS_A)]
            return _gla_superblock(q, k, vb, v_heads, la_ref[rows, la_cols], st, tri, mask, same64, head_lanes,
                                   forward)

        (o_f, stf), (o_b, stb) = _interleave(direction(rf, slice(0, 256), stf, tri_f, lower, True),
                                             direction(rb, slice(256, 512), stb, tri_b, upper, False))
        of_ref[rf, :] = o_f
        ob_ref[rb, :] = o_b
        return stf, stb

    stf, stb = lax.fori_loop(0, nsb, step, (expand(s0f_ref[...]), expand(s0b_ref[...])))
    sf_ref[...] = compact(stf)
    sb_ref[...] = compact(stb)

    gain = gain_ref[...]

    def finish(i, carry):
        rows = pl.ds(pl.multiple_of(i * n, n), n)
        o = of_ref[rows, :] + ob_ref[rows, :]
        sq_hi, sq_lo = _split_bf16(o * o)
        ms = (_dot(sq_hi, ones64) + _dot(sq_lo, ones64)) * (1.0 / DK_A)
        y = (o * lax.rsqrt(ms + EPS)) * gain
        o_ref[rows, :] = (y * _silu(g4_ref[rows, 768:1024])).astype(BF16)
        return carry

    lax.fori_loop(0, nsb, finish, 0)


def _gla_call(g4, la, s0f, s0b, gain, cast_w=None, *, batch, seq, row_block0):
    tok = lambda w: pl.BlockSpec((seq, w), lambda b: (row_block0 + b, 0))
    st = pl.BlockSpec((None, 64, 256), lambda b: (b, 0, 0))
    in_specs = [tok(1024), tok(512), st, st, pl.BlockSpec((1, W_A), lambda b: (0, 0))]
    out_specs = [pl.BlockSpec((seq, W_A), lambda b: (b, 0)), st, st]
    out_shape = [jax.ShapeDtypeStruct((batch * seq, W_A), BF16),
                 jax.ShapeDtypeStruct((batch, 64, 256), F32),
                 jax.ShapeDtypeStruct((batch, 64, 256), F32)]
    args = [g4, la, s0f, s0b, gain]
    if cast_w is not None:
        spec, shape = _cast_rider(cast_w, batch)
        in_specs.append(spec)
        out_specs.append(spec)
        out_shape.append(shape)
        args.append(cast_w)
    return pl.pallas_call(
        functools.partial(_gla_kernel, seq=seq, rider=cast_w is not None),
        grid=(batch,),
        in_specs=in_specs,
        out_specs=out_specs,
        out_shape=out_shape,
        scratch_shapes=[pltpu.VMEM((seq, W_A), F32), pltpu.VMEM((seq, W_A), F32)],
        compiler_params=_cparams(("arbitrary",)),
        name=f"gla_{seq}",
    )(*args)


def _state_to_kernel(s):
    b = s.shape[0]
    return jnp.transpose(s, (0, 3, 1, 2)).reshape(b, 64, 256)


def _state_from_kernel(st):
    b = st.shape[0]
    return jnp.transpose(st.reshape(b, 64, N_HEADS_A, DK_A), (0, 2, 3, 1))


def _lambda(lv, lam_init):
    l01 = jnp.sum(lv[0:1, :] * lv[1:2, :], axis=-1, keepdims=True)
    l23 = jnp.sum(lv[2:3, :] * lv[3:4, :], axis=-1, keepdims=True)
    return jnp.exp(l01) - jnp.exp(l23) + lam_init


def _softmax_parts(parts):
    mx = functools.reduce(jnp.maximum, [jnp.max(p, axis=-1, keepdims=True) for p in parts])
    es = [jnp.exp(p - mx) for p in parts]
    den = functools.reduce(lambda a, b: a + b, [jnp.sum(e, axis=-1, keepdims=True) for e in es])
    return [e / den for e in es]


def _diff_finish(o, gain, lam_init):
    o = o * lax.rsqrt(jnp.mean(o * o, axis=-1, keepdims=True) + EPS)
    return ((o * gain) * (1.0 - lam_init)).astype(BF16)


QK_SCALE = DH_B ** -0.5


def _key_halves(k):
    first = lax.broadcasted_iota(jnp.int32, k.shape, 1) < DH_B
    return jnp.where(first, k, 0.0).astype(BF16), jnp.where(first, 0.0, k).astype(BF16)


def _attn_prompt_kernel(lv_ref, q_ref, k_ref, v_ref, gain_ref, *rest, lam_init, n_prev, rider):
    rest = list(rest)
    if rider:
        wb_ref = rest.pop()
        w_ref = rest.pop(2 * n_prev)
        wb_ref[...] = w_ref[...].astype(BF16)
    prev_refs, (o_ref, *cache_refs) = rest[:2 * n_prev], rest[2 * n_prev:]
    lam = _lambda(lv_ref[...], lam_init)

    def head(h):
        cols = slice(h * DV_B, (h + 1) * DV_B)
        q = (q_ref[:, cols] * QK_SCALE).astype(BF16)
        k1, k2 = _key_halves(k_ref[:, cols])
        s1 = _dot_nt(q, k1)
        s2 = _dot_nt(q, k2)
        yield
        (p1,) = _softmax_parts([s1])
        (p2,) = _softmax_parts([s2])
        a = p1 - lam * p2
        o = _dot(a.astype(BF16), v_ref[:, cols].astype(BF16))
        yield
        o_ref[:, cols] = _diff_finish(o, gain_ref[:, cols], lam_init)
        if cache_refs:
            ck_ref, cv_ref = cache_refs
            layers_k = [*prev_refs[0::2], k_ref]
            layers_v = [*prev_refs[1::2], v_ref]
            for l in range(n_prev + 1):
                ck_ref[l, pl.ds(h, SEQ, stride=N_HEADS_B), :] = layers_k[l][:, cols]
                cv_ref[l, pl.ds(h, SEQ, stride=N_HEADS_B), :] = layers_v[l][:, cols]

    _interleave(*[head(h) for h in range(N_HEADS_B)])


def _attn_prompt_call(lv, qb, kb, vb, gain, prev_kv=(), cast_w=None, *, lam_init, write_cache=False):
    blk = pl.BlockSpec((SEQ, W_B), lambda b: (b, 0))
    n_prev = len(prev_kv)
    assert write_cache or not prev_kv
    in_specs = [pl.BlockSpec((4, DH_B), lambda b: (0, 0)), blk, blk, blk,
                pl.BlockSpec((1, W_B), lambda b: (0, 0))] + [blk] * (2 * n_prev)
    args = [lv, qb, kb, vb, gain, *[a for kv in prev_kv for a in kv]]
    out_specs = [blk]
    out_shape = [jax.ShapeDtypeStruct((T_PROMPT, W_B), BF16)]
    if write_cache:
        cache = jax.ShapeDtypeStruct((BATCH, n_prev + 1, SEQ * N_HEADS_B, DV_B), F32)
        cblk = pl.BlockSpec((None, n_prev + 1, SEQ * N_HEADS_B, DV_B), lambda b: (b, 0, 0, 0))
        out_specs += [cblk, cblk]
        out_shape += [cache, cache]
    if cast_w is not None:
        spec, shape = _cast_rider(cast_w, BATCH)
        in_specs.append(spec)
        out_specs.append(spec)
        out_shape.append(shape)
        args.append(cast_w)
    return pl.pallas_call(
        functools.partial(_attn_prompt_kernel, lam_init=lam_init, n_prev=n_prev, rider=cast_w is not None),
        grid=(BATCH,),
        in_specs=in_specs,
        out_specs=out_specs,
        out_shape=out_shape,
        compiler_params=_cparams(("arbitrary",)),
        name="diff_attn_prompt",
    )(*args)


def _rope(x, cos, sin_signed):
    lane = lax.broadcasted_iota(jnp.int32, x.shape, 1)
    first = (lane % (2 * AXIS_PAIRS)) < AXIS_PAIRS
    partner = jnp.where(first, pltpu.roll(x, LANES - AXIS_PAIRS, 1), pltpu.roll(x, AXIS_PAIRS, 1))
    return x * cos + partner * sin_signed


ATT_TQ = 256


def _attn_sample_kernel(lv_ref, q_ref, k_ref, v_ref, kc_ref, vc_ref, cosq_ref, sinq_ref,
                        cosk_ref, sink_ref, gain_ref, o_ref, k1_ref, k2_ref, *, lam_init):
    @pl.when(pl.program_id(1) == 0)
    def _():
        for h in range(N_HEADS_B):
            cols = slice(h * DV_B, (h + 1) * DV_B)
            k1_ref[:, cols], k2_ref[:, cols] = _key_halves(_rope(k_ref[:, cols], cosk_ref[...], sink_ref[...]))

    lam = _lambda(lv_ref[...], lam_init)

    def head(h):
        cols = slice(h * DV_B, (h + 1) * DV_B)
        q = (_rope(q_ref[:, cols], cosq_ref[...], sinq_ref[...]) * QK_SCALE).astype(BF16)
        c1, c2 = _key_halves(kc_ref[:, cols])
        s1 = [_dot_nt(q, k1_ref[:, cols]), _dot_nt(q, c1)]
        s2 = [_dot_nt(q, k2_ref[:, cols]), _dot_nt(q, c2)]
        yield
        p1 = _softmax_parts(s1)
        p2 = _softmax_parts(s2)
        a_own = p1[0] - lam * p2[0]
        a_ctx = p1[1] - lam * p2[1]
        o = (_dot(a_own.astype(BF16), v_ref[:, cols].astype(BF16))
             + _dot(a_ctx.astype(BF16), vc_ref[:, cols].astype(BF16)))
        yield
        o_ref[:, cols] = _diff_finish(o, gain_ref[:, cols], lam_init)

    _interleave(*[head(h) for h in range(N_HEADS_B)])


def _attn_sample_call(lv, qb, kb, vb, kc, vc, cos, sin_signed, gain, *, lam_init):
    tq = ATT_TQ
    nq = DEC_SEQ // tq
    p0 = T_PROMPT // tq
    s0 = T_PROMPT // DEC_SEQ
    qblk = pl.BlockSpec((tq, W_B), lambda b, t: (p0 + b * nq + t, 0))
    kvblk = pl.BlockSpec((DEC_SEQ, W_B), lambda b, t: (s0 + b, 0))
    cblk = pl.BlockSpec((None, PAST_LEN, W_B), lambda b, t: (b, 0, 0))
    return pl.pallas_call(
        functools.partial(_attn_sample_kernel, lam_init=lam_init),
        grid=(DEC_BATCH, nq),
        in_specs=[pl.BlockSpec((4, DH_B), lambda b, t: (0, 0)), qblk, kvblk, kvblk, cblk, cblk,
                  pl.BlockSpec((tq, DV_B), lambda b, t: (t, 0)),
                  pl.BlockSpec((tq, DV_B), lambda b, t: (t, 0)),
                  pl.BlockSpec((DEC_SEQ, DV_B), lambda b, t: (0, 0)),
                  pl.BlockSpec((DEC_SEQ, DV_B), lambda b, t: (0, 0)),
                  pl.BlockSpec((1, W_B), lambda b, t: (0, 0))],
        out_specs=pl.BlockSpec((tq, W_B), lambda b, t: (b * nq + t, 0)),
        out_shape=jax.ShapeDtypeStruct((T_SAMPLE, W_B), BF16),
        scratch_shapes=[pltpu.VMEM((DEC_SEQ, W_B), BF16), pltpu.VMEM((DEC_SEQ, W_B), BF16)],
        compiler_params=_cparams(("arbitrary", "arbitrary")),
        name="diff_attn_sample",
    )(lv, qb, kb, vb, kc, vc, cos, sin_signed, cos, sin_signed, gain)


def _rope_tables():
    rows = DEC_SEQ // GRID_W
    row = jnp.repeat(jnp.arange(rows, dtype=F32), GRID_W)
    col = jnp.tile(jnp.arange(GRID_W, dtype=F32), rows)
    freqs = ROPE_THETA ** (-jnp.arange(AXIS_PAIRS, dtype=F32) / AXIS_PAIRS)
    ar, ac = row[:, None] * freqs, col[:, None] * freqs
    cos64 = jnp.concatenate([jnp.cos(ar), jnp.cos(ar), jnp.cos(ac), jnp.cos(ac)], axis=-1)
    sin64 = jnp.concatenate([-jnp.sin(ar), jnp.sin(ar), -jnp.sin(ac), jnp.sin(ac)], axis=-1)
    return jnp.tile(cos64, (1, 2)), jnp.tile(sin64, (1, 2))


def _group_mean(x, ones64):
    hi, lo = _split_bf16(x)
    return (_dot(hi, ones64) + _dot(lo, ones64)) * (1.0 / DG_C)


def _sgu(uv, ws_ref, bs_ref, fillers=()):
    fillers = list(fillers)
    fill = lambda: fillers.pop(0)() if fillers else None
    r = lax.broadcasted_iota(jnp.int32, (W_C, W_C), 0)
    s = lax.broadcasted_iota(jnp.int32, (W_C, W_C), 1)
    ones64 = jnp.where((r // DG_C) == (s // DG_C), 1.0, 0.0).astype(BF16)
    lane = lax.broadcasted_iota(jnp.int32, (SGU_CHUNK, W_C), 1)
    outs = []
    for n in range(uv.shape[0] // SGU_CHUNK):
        rows = slice(n * SGU_CHUNK, (n + 1) * SGU_CHUNK)
        u = _gelu_tanh(uv[rows, 0:256])
        v = _gelu_tanh(uv[rows, 256:512])
        mu = _group_mean(v, ones64)
        fill()
        d = v - mu
        var = _group_mean(d * d, ones64)
        fill()
        vn = d * lax.rsqrt(var + EPS)
        s_mix = bs_ref[...]
        for g in range(N_GROUPS_C):
            vn_g = jnp.where((lane // DG_C) == g, vn, 0.0).astype(BF16)
            s_mix = s_mix + _dot(ws_ref[g], vn_g)
        fill()
        outs.append((u * s_mix).astype(BF16))
    while fillers:
        fill()
    return jnp.concatenate(outs, axis=0)


OUT_TM = 512
OUT_SLABS = 2


SEL_LANE0 = N_EXPERTS


def _top2_gates(logits):
    lane = lax.broadcasted_iota(jnp.int32, logits.shape, 1).astype(F32)
    neg = -jnp.inf
    lg = jnp.where(lane < N_EXPERTS, logits, neg)
    m1 = jnp.max(lg, axis=-1, keepdims=True)
    i1 = jnp.min(jnp.where(lg == m1, lane, float(LANES)), axis=-1, keepdims=True)
    lg2 = jnp.where(lane == i1, neg, lg)
    m2 = jnp.max(lg2, axis=-1, keepdims=True)
    i2 = jnp.min(jnp.where(lg2 == m2, lane, float(LANES)), axis=-1, keepdims=True)
    e2 = jnp.exp(m2 - m1)
    den = 1.0 + e2
    gates = jnp.where(lane == i1, 1.0 / den, 0.0) + jnp.where(lane == i2, e2 / den, 0.0)
    sel = jnp.where((lane == i1 + SEL_LANE0) | (lane == i2 + SEL_LANE0), 1.0, 0.0)
    return gates + sel


def _store_token_tiles(ref, val, r0=0):
    n = val.shape[0]
    for k in range(D_MODEL // LANES):
        ref[pl.ds(8 * r0 + k, n, stride=8), :] = val[:, k * LANES:(k + 1) * LANES]


def _load_token_tiles(ref, n):
    return jnp.concatenate([ref[pl.ds(k, n, stride=8), :] for k in range(D_MODEL // LANES)], axis=-1)


def _out_proj_kernel(*refs, n_x, moe):
    oa_refs, ob_refs, (oc_ref,), x_refs = refs[0:2], refs[2:4], refs[4:5], refs[5:5 + n_x]
    w_ref, g1_ref, nrm_ref, sc_ref, sh_ref, *rest = refs[5 + n_x:]
    if moe:
        rw_ref, x1_ref, h2t_ref, gates_ref, gates_t_ref, wo_ref = rest
    else:
        x1_ref, h2_ref, wo_ref = rest

    @pl.when(pl.program_id(0) == 0)
    def _():
        wo_ref[...] = w_ref[...].astype(BF16)

    if moe:
        rw_hi, rw_lo = _split_bf16(rw_ref[...])

    def rows_chain(r0, n):
        rs = slice(r0, r0 + n)
        y = (_dot(_pick_piece(oa_refs, OUT_TM, rs), wo_ref[0:256, :])
             + _dot(_pick_piece(ob_refs, OUT_TM, rs), wo_ref[256:768, :]) + _dot(oc_ref[rs, :], wo_ref[768:1024, :]))
        yield
        x1 = _pick_piece(x_refs, OUT_TM, rs) + g1_ref[...] * y
        x1_ref[rs, :] = x1
        yn = x1 * lax.rsqrt(jnp.mean(x1 * x1, axis=-1, keepdims=True) + EPS)
        h = (yn * nrm_ref[...]) * (1.0 + sc_ref[...]) + sh_ref[...]
        if moe:
            _store_token_tiles(h2t_ref, h, r0)
            h_hi, h_lo = _split_bf16(h)
            logits = _dot(h_hi, rw_hi) + (_dot(h_lo, rw_hi) + _dot(h_hi, rw_lo))
            yield
            gates = _top2_gates(logits)
            gates_ref[rs, :] = gates
            gates_t_ref[:, rs] = gates.T
        else:
            h2_ref[rs, :] = h.astype(BF16)

    n_slab = OUT_TM // OUT_SLABS
    _interleave(*[rows_chain(i * n_slab, n_slab) for i in range(OUT_SLABS)])


def _out_proj_call(oa_pieces, ob_pieces, oc, x_pieces, w_out, l, mod4, norm_g, router_w=None):
    tm = OUT_TM
    n = T_ALL // tm
    moe = router_w is not None
    row = functools.partial(_mod_row, tm=tm)
    mod_spec = lambda k: pl.BlockSpec((None, None, 1, D_MODEL), lambda i: (row(i), k, 0, 0))
    tok = lambda w: pl.BlockSpec((tm, w), lambda i: (i, 0))
    full = lambda a: pl.BlockSpec(a.shape, lambda i: (0,) * a.ndim)
    in_specs = (_piece_specs(oa_pieces, tm, W_A) + _piece_specs(ob_pieces, tm, W_B) + [tok(W_C)]
                + _piece_specs(x_pieces, tm, D_MODEL)
                + [_resident_layer(w_out, l), mod_spec(2), full(norm_g), mod_spec(4), mod_spec(3)])
    args = [*oa_pieces, *ob_pieces, oc, *x_pieces, w_out, mod4, norm_g, mod4, mod4]
    if moe:
        in_specs.append(full(router_w))
        args.append(router_w)
        out_specs = [tok(D_MODEL), pl.BlockSpec((tm * 8, LANES), lambda i: (i, 0)), tok(LANES),
                     pl.BlockSpec((LANES, tm), lambda i: (0, i))]
        out_shape = [jax.ShapeDtypeStruct((T_ALL, D_MODEL), F32), jax.ShapeDtypeStruct((T_ALL * 8, LANES), F32),
                     jax.ShapeDtypeStruct((T_ALL, LANES), F32), jax.ShapeDtypeStruct((LANES, T_ALL), F32)]
    else:
        out_specs = [tok(D_MODEL), tok(D_MODEL)]
        out_shape = [jax.ShapeDtypeStruct((T_ALL, D_MODEL), F32), jax.ShapeDtypeStruct((T_ALL, D_MODEL), BF16)]
    return pl.pallas_call(
        functools.partial(_out_proj_kernel, n_x=len(x_pieces), moe=moe),
        grid=(n,),
        in_specs=in_specs,
        out_specs=out_specs,
        out_shape=out_shape,
        scratch_shapes=[pltpu.VMEM((D_MODEL, D_MODEL), BF16)],
        compiler_params=_cparams(("arbitrary",)),
        name="out_proj_moe" if moe else "out_proj",
    )(*args)


FFN_TM = 512
MXU_N = 256
FFN_SPLITS = (0, 1024, 2048, D_FF)
assert all(s % MXU_N == 0 for s in FFN_SPLITS)


def _swiglu(h, wg_ref, wu_ref, wd_ref):
    out = None
    for c0, c1 in zip(FFN_SPLITS[:-1], FFN_SPLITS[1:]):
        act = _silu(_dot(h, wg_ref[:, c0:c1])) * _dot(h, wu_ref[:, c0:c1])
        d = _dot(act.astype(BF16), wd_ref[c0:c1, :])
        out = d if out is None else out + d
    return out


def _ffn_kernel(h_ref, x_ref, g2_ref, wg_ref, wu_ref, wd_ref, o_ref):
    o_ref[...] = x_ref[...] + g2_ref[...] * _swiglu(h_ref[...], wg_ref, wu_ref, wd_ref)


def _ffn_call(h2, x1, mod4, wg, wu, wd):
    tm = FFN_TM
    row = functools.partial(_mod_row, tm=tm)
    tok = lambda w: pl.BlockSpec((tm, w), lambda i: (i, 0))
    resident = lambda a: pl.BlockSpec(a.shape, lambda i: (0, 0), pipeline_mode=pl.Buffered(1))
    return pl.pallas_call(
        _ffn_kernel,
        grid=(T_ALL // tm,),
        in_specs=[tok(D_MODEL), tok(D_MODEL),
                  pl.BlockSpec((None, None, 1, D_MODEL), lambda i: (row(i), 5, 0, 0)),
                  resident(wg), resident(wu), resident(wd)],
        out_specs=tok(D_MODEL),
        out_shape=jax.ShapeDtypeStruct((T_ALL, D_MODEL), F32),
        compiler_params=_cparams(("arbitrary",)),
        name="ffn_dense",
    )(h2, x1, mod4, wg, wu, wd)


MOE_TM = 512
MOE_NT_MAX = (2 * T_ALL) // MOE_TM + N_EXPERTS
MOE_ROWS = MOE_NT_MAX * MOE_TM
PLAN_BLK = 512
MISC_LAST_START = 8
MISC_NT = 16


def _moe_plan_kernel(gt_ref, posa_ref, posb_ref, te_ref, ti_ref, misc_ref):
    tm = float(MOE_TM)
    sel = gt_ref[SEL_LANE0:SEL_LANE0 + N_EXPERTS, :]
    cnt = jnp.sum(sel, axis=1, keepdims=True)
    nt = jnp.floor((cnt + (tm - 1.0)) * (1.0 / tm))
    sub = lax.broadcasted_iota(jnp.int32, (N_EXPERTS, LANES), 0).astype(F32)
    lane = lax.broadcasted_iota(jnp.int32, (N_EXPERTS, LANES), 1).astype(F32)
    nt_b = jnp.broadcast_to(nt, (N_EXPERTS, LANES))
    nt_row = jnp.sum(jnp.where(sub == lane, nt_b, 0.0), axis=0, keepdims=True)
    toff = jnp.sum(jnp.where(lane < sub, jnp.broadcast_to(nt_row, (N_EXPERTS, LANES)), 0.0),
                   axis=1, keepdims=True)
    tend = toff + nt
    n_total = jnp.sum(nt, axis=0, keepdims=True)
    jc = jnp.minimum(lane, n_total - 1.0)
    te = jnp.sum(jnp.where(jc >= tend, 1.0, 0.0), axis=0, keepdims=True)
    te_ref[...] = te.astype(jnp.int32)
    ti_ref[...] = jc[0:1, :].astype(jnp.int32)
    last_start = (tend - 1.0) * tm
    ls_row = jnp.sum(jnp.where(sub + MISC_LAST_START == lane, jnp.broadcast_to(last_start, (N_EXPERTS, LANES)), 0.0),
                     axis=0, keepdims=True)
    nt_row2 = jnp.sum(jnp.where(sub + MISC_NT == lane, nt_b, 0.0), axis=0, keepdims=True)
    misc = jnp.where(lane[0:1, :] == 0.0, n_total, 0.0) + ls_row + nt_row2
    misc_ref[...] = misc.astype(jnp.int32)

    off = toff * tm
    r = lax.broadcasted_iota(jnp.int32, (PLAN_BLK, PLAN_BLK), 0)
    c = lax.broadcasted_iota(jnp.int32, (PLAN_BLK, PLAN_BLK), 1)
    upper = jnp.where(r <= c, 1.0, 0.0).astype(BF16)
    carry = jnp.zeros((N_EXPERTS, 1), F32)
    for blk in range(T_ALL // PLAN_BLK):
        cols = slice(blk * PLAN_BLK, (blk + 1) * PLAN_BLK)
        s = gt_ref[SEL_LANE0:SEL_LANE0 + N_EXPERTS, cols]
        rank = _dot(s.astype(BF16), upper) + carry
        pos = off + rank - 1.0
        posa_ref[:, cols] = jnp.min(jnp.where(s > 0.0, pos, 1e9), axis=0, keepdims=True).astype(jnp.int32)
        posb_ref[:, cols] = jnp.max(jnp.where(s > 0.0, pos, -1.0), axis=0, keepdims=True).astype(jnp.int32)
        carry = carry + jnp.sum(s, axis=1, keepdims=True)


def _moe_plan_call(gates_t):
    row = lambda w: jax.ShapeDtypeStruct((1, w), jnp.int32)
    full = lambda w: pl.BlockSpec((1, w), lambda: (0, 0))
    return pl.pallas_call(
        _moe_plan_kernel,
        in_specs=[pl.BlockSpec((LANES, T_ALL), lambda: (0, 0))],
        out_specs=[full(T_ALL), full(T_ALL), full(LANES), full(LANES), full(LANES)],
        out_shape=[row(T_ALL), row(T_ALL), row(LANES), row(LANES), row(LANES)],
        compiler_params=pltpu.CompilerParams(vmem_limit_bytes=VMEM_LIMIT),
        name="moe_plan",
    )(gates_t)


DMA_UNROLL = 8


def _row_tile(ref, row):
    return ref.at[pl.ds(pl.multiple_of(row * 8, 8), 8), :]


def _moe_scatter_kernel(misc_ref, posa_ref, posb_ref, h_ref, xs_ref, zero_ref, sem):
    tm = h_ref.shape[0] // 8

    @pl.when(pl.program_id(0) == 0)
    def _():
        zero_ref[...] = jnp.zeros_like(zero_ref)

        def zero_tile(first_row):
            start = pl.multiple_of(first_row * 8, 8)
            cp = pltpu.make_async_copy(zero_ref, xs_ref.at[pl.ds(start, MOE_TM * 8), :], sem.at[0])
            cp.start()
            cp.wait()

        for e in range(N_EXPERTS):
            @pl.when(misc_ref[0, MISC_NT + e] > 0)
            def _():
                zero_tile(misc_ref[0, MISC_LAST_START + e])

        def zero_tail(j, carry):
            zero_tile(j * MOE_TM)
            return carry

        lax.fori_loop(misc_ref[0, 0], MOE_NT_MAX, zero_tail, 0)

    def issue(r, carry):
        src = _row_tile(h_ref, r)
        pltpu.make_async_copy(src, _row_tile(xs_ref, posa_ref[0, r]), sem.at[0]).start(priority=0)
        pltpu.make_async_copy(src, _row_tile(xs_ref, posb_ref[0, r]), sem.at[1]).start(priority=1)
        return carry

    lax.fori_loop(0, tm, issue, 0, unroll=DMA_UNROLL)
    for k in range(2):
        pltpu.make_async_copy(h_ref, xs_ref.at[pl.ds(0, tm * 8), :], sem.at[k]).wait()


SCATTER_TM = 512


def _moe_scatter_call(misc, posa3, posb3, h2t):
    tm = SCATTER_TM
    smem_row = pl.BlockSpec((None, 1, tm), lambda i: (i, 0, 0), memory_space=pltpu.SMEM)
    return pl.pallas_call(
        _moe_scatter_kernel,
        grid=(T_ALL // tm,),
        in_specs=[pl.BlockSpec((1, LANES), lambda i: (0, 0), memory_space=pltpu.SMEM), smem_row, smem_row,
                  pl.BlockSpec((tm * 8, LANES), lambda i: (i, 0))],
        out_specs=pl.BlockSpec(memory_space=pl.ANY),
        out_shape=jax.ShapeDtypeStruct((MOE_ROWS * 8, LANES), F32),
        scratch_shapes=[pltpu.VMEM((MOE_TM * 8, LANES), F32), pltpu.SemaphoreType.DMA((2,))],
        compiler_params=_cparams(("arbitrary",)),
        name="moe_scatter",
    )(misc, posa3, posb3, h2t)


def _ffn_grouped_kernel(te_ref, ti_ref, misc_ref, x_ref, wg_ref, wu_ref, wd_ref, o_ref):
    j = pl.program_id(0)

    @pl.when(j < misc_ref[0])
    def _():
        h = _load_token_tiles(x_ref, MOE_TM).astype(BF16)
        _store_token_tiles(o_ref, _swiglu(h, wg_ref, wu_ref, wd_ref))

    @pl.when(j >= misc_ref[0])
    def _():
        o_ref[...] = jnp.zeros_like(o_ref)


def _ffn_grouped_call(te, ti, misc, xs, wg, wu, wd):
    expert = lambda *s: pl.BlockSpec((None,) + s, lambda j, te, ti, misc: (te[j], 0, 0))
    grid_spec = pltpu.PrefetchScalarGridSpec(
        num_scalar_prefetch=3,
        grid=(MOE_NT_MAX,),
        in_specs=[pl.BlockSpec((MOE_TM * 8, LANES), lambda j, te, ti, misc: (ti[j], 0)),
                  expert(D_MODEL, D_FF), expert(D_MODEL, D_FF), expert(D_FF, D_MODEL)],
        out_specs=pl.BlockSpec((MOE_TM * 8, LANES), lambda j, te, ti, misc: (j, 0)),
    )
    return pl.pallas_call(
        _ffn_grouped_kernel,
        grid_spec=grid_spec,
        out_shape=jax.ShapeDtypeStruct((MOE_ROWS * 8, LANES), F32),
        compiler_params=_cparams(("arbitrary",)),
        name="ffn_grouped",
    )(te, ti, misc, xs, wg, wu, wd)


COMBINE_TM = 256


def _moe_combine_kernel(posa_ref, posb_ref, posa_next_ref, posb_next_ref, ys_ref, x_ref, g2_ref, gates_ref, nf_ref,
                        op_ref, os_ref, bufa_ref, bufb_ref, sem):
    tm = COMBINE_TM
    i = pl.program_id(0)
    n = pl.num_programs(0)
    slot = i % 2

    def gather(pa_ref, pb_ref, s):
        def issue(r, carry):
            pltpu.make_async_copy(_row_tile(ys_ref, pa_ref[0, r]), _row_tile(bufa_ref.at[s], r),
                                  sem.at[s, 0]).start(priority=0)
            pltpu.make_async_copy(_row_tile(ys_ref, pb_ref[0, r]), _row_tile(bufb_ref.at[s], r),
                                  sem.at[s, 1]).start(priority=1)
            return carry

        lax.fori_loop(0, tm, issue, 0, unroll=DMA_UNROLL)

    @pl.when(i == 0)
    def _():
        gather(posa_ref, posb_ref, 0)

    @pl.when(i + 1 < n)
    def _():
        gather(posa_next_ref, posb_next_ref, 1 - slot)

    gates = gates_ref[...]
    lane = lax.broadcasted_iota(jnp.int32, gates.shape, 1).astype(F32)
    is_sel = (lane >= SEL_LANE0) & (lane < SEL_LANE0 + N_EXPERTS) & (gates > 0.0)
    ia = jnp.min(jnp.where(is_sel, lane, float(LANES)), axis=-1, keepdims=True) - SEL_LANE0
    ib = jnp.max(jnp.where(is_sel, lane, -1.0), axis=-1, keepdims=True) - SEL_LANE0
    wa = jnp.sum(jnp.where(lane == ia, gates, 0.0), axis=-1, keepdims=True)
    wb = jnp.sum(jnp.where(lane == ib, gates, 0.0), axis=-1, keepdims=True)

    pltpu.make_async_copy(ys_ref.at[pl.ds(0, tm * 8), :], bufa_ref.at[slot], sem.at[slot, 0]).wait()
    pltpu.make_async_copy(ys_ref.at[pl.ds(0, tm * 8), :], bufb_ref.at[slot], sem.at[slot, 1]).wait()

    y = wa * _load_token_tiles(bufa_ref.at[slot], tm) + wb * _load_token_tiles(bufb_ref.at[slot], tm)
    out = x_ref[...] + g2_ref[...] * y
    out = (out * lax.rsqrt(jnp.mean(out * out, axis=-1, keepdims=True) + EPS)) * nf_ref[...]

    @pl.when(i < T_PROMPT // tm)
    def _():
        op_ref[...] = out

    @pl.when(i >= T_PROMPT // tm)
    def _():
        os_ref[...] = out


def _moe_combine_call(posa3, posb3, ys, x1, mod4, gates, norm_f):
    tm = COMBINE_TM
    n = T_ALL // tm
    n_p = T_PROMPT // tm
    row = functools.partial(_mod_row, tm=tm)
    smem_row = pl.BlockSpec((None, 1, tm), lambda i: (i, 0, 0), memory_space=pltpu.SMEM)
    smem_next = pl.BlockSpec((None, 1, tm), lambda i: (jnp.minimum(i + 1, n - 1), 0, 0), memory_space=pltpu.SMEM)
    tok = lambda w: pl.BlockSpec((tm, w), lambda i: (i, 0))
    return pl.pallas_call(
        _moe_combine_kernel,
        grid=(n,),
        in_specs=[smem_row, smem_row, smem_next, smem_next, pl.BlockSpec(memory_space=pl.ANY), tok(D_MODEL),
                  pl.BlockSpec((None, None, 1, D_MODEL), lambda i: (row(i), 5, 0, 0)), tok(LANES),
                  pl.BlockSpec((1, D_MODEL), lambda i: (0, 0))],
        out_specs=[pl.BlockSpec((tm, D_MODEL), lambda i: (jnp.minimum(i, n_p - 1), 0)),
                   pl.BlockSpec((tm, D_MODEL), lambda i: (jnp.maximum(i - n_p, 0), 0))],
        out_shape=[jax.ShapeDtypeStruct((T_PROMPT, D_MODEL), F32), jax.ShapeDtypeStruct((T_SAMPLE, D_MODEL), F32)],
        scratch_shapes=[pltpu.VMEM((2, tm * 8, LANES), F32), pltpu.VMEM((2, tm * 8, LANES), F32),
                        pltpu.SemaphoreType.DMA((2, 2))],
        compiler_params=_cparams(("arbitrary",)),
        name="moe_combine",
    )(posa3, posb3, posa3, posb3, ys, x1, mod4, gates, norm_f)


def _moe_call(h2t, x1, mod4, gates, gates_t, wg, wu, wd, norm_f):
    posa, posb, te, ti, misc = _moe_plan_call(gates_t)
    xs = _moe_scatter_call(misc, posa.reshape(T_ALL // SCATTER_TM, 1, SCATTER_TM),
                           posb.reshape(T_ALL // SCATTER_TM, 1, SCATTER_TM), h2t)
    ys = _ffn_grouped_call(te.reshape(LANES), ti.reshape(LANES), misc.reshape(LANES), xs, wg, wu, wd)
    return _moe_combine_call(posa.reshape(T_ALL // COMBINE_TM, 1, COMBINE_TM),
                             posb.reshape(T_ALL // COMBINE_TM, 1, COMBINE_TM), ys, x1, mod4, gates, norm_f)


def kernel(x_prompt, x_sample, cache_k, cache_v, state_gla_fwd, state_gla_bwd, c, c_ctx, w_ada, b_ada, norm_mix, norm_ffn, w_in, w_out, gla_w_up, gla_b_up, gla_norm, diff_lambda, diff_norm, sgu_w, sgu_b, ffn_w_gate, ffn_w_up, ffn_w_down, router_w, moe_w_gate, moe_w_up, moe_w_down, norm_f):
    assert DEPTH == 2
    x_pieces = [x_prompt.reshape(T_PROMPT, D_MODEL), x_sample.reshape(T_SAMPLE, D_MODEL)]
    cvecs = jnp.concatenate([c_ctx[None, :], c, jnp.zeros((N_MOD_ROWS - 1 - DEC_BATCH, D_MODEL), F32)], axis=0)
    mod = _ada_call(cvecs, w_ada, b_ada)
    cos, sin_signed = _rope_tables()
    zeros_state = jnp.zeros((BATCH, 64, 256), F32)

    w_in_t = jnp.swapaxes(w_in, 1, 2)
    moe_w = [moe_w_gate[0], moe_w_up[0], moe_w_down[0]]
    prev_kv, sfs, sbs = [], [], []
    for l in range(DEPTH):
        mod4 = mod[l].reshape(N_MOD_ROWS, 6, 1, D_MODEL)
        w_up = jnp.zeros((LANES, 2 * W_A), F32)
        w_up = w_up.at[0:GLA_RANK, 0:W_A].set(gla_w_up[l, 0]).at[GLA_RANK:2 * GLA_RANK, W_A:].set(gla_w_up[l, 1])
        b_up = gla_b_up[l].reshape(1, 2 * W_A)
        bs_full = jnp.repeat(sgu_b[l].T, DG_C, axis=1)
        g4, la, qb, kb, vb, oc = _in_proj_call(x_pieces, norm_mix[l][None, :], mod4, w_in_t, l, w_up.astype(BF16), b_up,
                                               sgu_w[l].astype(BF16), bs_full)

        gain_a = gla_norm[l][None, :]
        oa_p, sf, sb, moe_w[l] = _gla_call(g4, la, zeros_state, zeros_state, gain_a, moe_w[l],
                                           batch=BATCH, seq=SEQ, row_block0=0)
        oa_s, _, _ = _gla_call(g4, la, _state_to_kernel(state_gla_fwd[:, l]),
                               _state_to_kernel(state_gla_bwd[:, l]), gain_a,
                               batch=DEC_BATCH, seq=DEC_SEQ, row_block0=T_PROMPT // DEC_SEQ)

        lam_init = 0.8 - 0.6 * math.exp(-0.3 * l)
        gain_b = diff_norm[l][None, :]
        if l < DEPTH - 1:
            ob_p, moe_w[2] = _attn_prompt_call(diff_lambda[l], qb, kb, vb, gain_b, cast_w=moe_w[2], lam_init=lam_init)
            prev_kv.append((kb, vb))
        else:
            ob_p, new_cache_k, new_cache_v = _attn_prompt_call(diff_lambda[l], qb, kb, vb, gain_b, prev_kv,
                                                               lam_init=lam_init, write_cache=True)
        ob_s = _attn_sample_call(diff_lambda[l], qb, kb, vb,
                                 cache_k[:, l].reshape(DEC_BATCH, PAST_LEN, W_B),
                                 cache_v[:, l].reshape(DEC_BATCH, PAST_LEN, W_B),
                                 cos, sin_signed, gain_b, lam_init=lam_init)

        if l == 0:
            x1, h2 = _out_proj_call([oa_p, oa_s], [ob_p, ob_s], oc, x_pieces, w_out, l, mod4, norm_ffn[l][None, :])
            x_pieces = [_ffn_call(h2, x1, mod4, ffn_w_gate[0].astype(BF16), ffn_w_up[0].astype(BF16),
                                  ffn_w_down[0].astype(BF16))]
        else:
            rw = jnp.pad(router_w[0], ((0, 0), (0, LANES - N_EXPERTS)))
            x1, h2t, gates, gates_t = _out_proj_call([oa_p, oa_s], [ob_p, ob_s], oc, x_pieces, w_out, l, mod4,
                                                     norm_ffn[l][None, :], rw)
            y_prompt, y_sample = _moe_call(h2t, x1, mod4, gates, gates_t, *moe_w, norm_f[None, :])

        sfs.append(_state_from_kernel(sf))
        sbs.append(_state_from_kernel(sb))

    return (y_prompt.reshape(BATCH, SEQ, D_MODEL), y_sample.reshape(DEC_BATCH, DEC_SEQ, D_MODEL),
            new_cache_k.reshape(BATCH, DEPTH, SEQ, N_HEADS_B, DV_B), new_cache_v.reshape(BATCH, DEPTH, SEQ, N_HEADS_B, DV_B),
            jnp.stack(sfs, axis=1), jnp.stack(sbs, axis=1))
```

```python
import functools
import math

import jax
import jax.numpy as jnp
import numpy as np
from jax import lax
from jax.experimental import pallas as pl
from jax.experimental.pallas import tpu as pltpu

F32 = jnp.float32
BF16 = jnp.bfloat16

D_MODEL = 1024
BATCH = 32
SEQ = 256
DEPTH = 2
DEC_BATCH = 2
DEC_SEQ = 1024
PAST_LEN = 256
GRID_W = 64
N_HEADS_A = 4
DK_A = 64
W_A = 256
GLA_RANK = 16
GLA_TAU = 16.0
GLA_CHUNK = 64
N_HEADS_B = 4
DH_B = 64
DV_B = 128
W_B = 512
ROPE_THETA = 10000.0
AXIS_PAIRS = DH_B // 4
N_GROUPS_C = 4
DG_C = 64
W_C = 256
SGU_CHUNK = 128
D_FF = 2816
N_EXPERTS = 8
EPS = 1e-6

T_PROMPT = BATCH * SEQ
T_SAMPLE = DEC_BATCH * DEC_SEQ
T_ALL = T_PROMPT + T_SAMPLE
N_MOD_ROWS = 8
LANES = 128
VMEM_LIMIT = 56 * 1024 * 1024


def _cparams(sem):
    return pltpu.CompilerParams(dimension_semantics=sem, vmem_limit_bytes=VMEM_LIMIT)


def _dot(a, b):
    return jnp.dot(a, b, preferred_element_type=F32)


def _dot_nt(a, b):
    return lax.dot_general(a, b, (((1,), (1,)), ((), ())), preferred_element_type=F32)


def _dot_tn(a, b):
    return lax.dot_general(a, b, (((0,), (0,)), ((), ())), preferred_element_type=F32)


def _split_bf16(x):
    hi = x.astype(BF16)
    lo = (x - hi.astype(F32)).astype(BF16)
    return hi, lo


def _dot3(a, w):
    a_hi, a_lo = _split_bf16(a)
    w_hi, w_lo = _split_bf16(w)
    return _dot(a_hi, w_hi) + (_dot(a_lo, w_hi) + _dot(a_hi, w_lo))


def _sigmoid(x):
    return 1.0 / (1.0 + jnp.exp(-x))


def _silu(x):
    return x * _sigmoid(x)


def _gelu_tanh(x):
    c = math.sqrt(2.0 / math.pi)
    return x * (0.5 * (1.0 + jnp.tanh(c * (x + 0.044715 * (x * x * x)))))


def _log_sigmoid(x):
    return jnp.minimum(x, 0.0) - jnp.log(1.0 + jnp.exp(-jnp.abs(x)))


def _mod_row(i, tm):
    n_p = T_PROMPT // tm
    per_b = DEC_SEQ // tm
    return jnp.where(i < n_p, 0, 1 + (i - n_p) // per_b)


ADA_TN = 1536


def _ada_kernel(c_ref, w_ref, b_ref, o_ref):
    a = _silu(c_ref[...])
    o_ref[...] = _dot3(a, w_ref[...]) + b_ref[...]


def _ada_call(cvecs, w_ada, b_ada):
    n_col = (6 * D_MODEL) // ADA_TN
    return pl.pallas_call(
        _ada_kernel,
        grid=(DEPTH, n_col),
        in_specs=[
            pl.BlockSpec((N_MOD_ROWS, D_MODEL), lambda l, j: (0, 0)),
            pl.BlockSpec((None, D_MODEL, ADA_TN), lambda l, j: (l, 0, j)),
            pl.BlockSpec((None, 1, ADA_TN), lambda l, j: (l, 0, j)),
        ],
        out_specs=pl.BlockSpec((None, N_MOD_ROWS, ADA_TN), lambda l, j: (l, 0, j)),
        out_shape=jax.ShapeDtypeStruct((DEPTH, N_MOD_ROWS, 6 * D_MODEL), F32),
        compiler_params=_cparams(("arbitrary", "arbitrary")),
        name="ada_mod",
    )(cvecs, w_ada, b_ada.reshape(DEPTH, 1, 6 * D_MODEL))


IN_TM = 512
IN_COLS = 3104
Z_COL0 = 1024
Z_COLS = 2 * GLA_RANK
W_MAIN = 3072


def _piece_specs(pieces, tm, width):
    specs, t0 = [], 0
    for arr in pieces:
        nt = arr.shape[0] // tm
        specs.append(pl.BlockSpec((tm, width), lambda i, t0=t0, nt=nt: (jnp.clip(i - t0, 0, nt - 1), 0)))
        t0 += nt
    assert t0 * tm == T_ALL and len(pieces) in (1, 2) and (len(pieces) == 1 or pieces[0].shape[0] == T_PROMPT)
    return specs


def _pick_piece(refs, tm, rows=slice(None)):
    if len(refs) == 1:
        return refs[0][rows, :]
    return jnp.where(pl.program_id(0) < T_PROMPT // tm, refs[0][rows, :], refs[1][rows, :])


def _in_proj_kernel(*refs, n_x):
    x_refs = refs[:n_x]
    (nrm_ref, sh_ref, sc_ref, w_ref, wup_ref, bup_ref, ws_ref, bs_ref,
     g4_ref, la_ref, qb_ref, kb_ref, vb_ref, oc_ref, wm_ref, wz_ref) = refs[n_x:]

    @pl.when(pl.program_id(0) == 0)
    def _():
        wm_ref[0:Z_COL0, :] = w_ref[0:Z_COL0, :].astype(BF16)
        wm_ref[Z_COL0:W_MAIN, :] = w_ref[Z_COL0 + Z_COLS:IN_COLS, :].astype(BF16)
        wz_ref[...] = jnp.zeros_like(wz_ref)
        wz_ref[0:Z_COLS, :] = w_ref[Z_COL0:Z_COL0 + Z_COLS, :].astype(BF16)

    x = _pick_piece(x_refs, IN_TM)
    y = x * lax.rsqrt(jnp.mean(x * x, axis=-1, keepdims=True) + EPS)
    h = (y * nrm_ref[...]) * (1.0 + sc_ref[...]) + sh_ref[...]
    hb = h.astype(BF16)
    def plain_job(ref, c0, w0):
        def job():
            ref[:, c0:c0 + MXU_N] = _dot_nt(hb, wm_ref[w0:w0 + MXU_N, :])
        return job

    jobs = [plain_job(ref, c, w0 + c) for ref, w0, width in
            ((g4_ref, 0, 1024), (qb_ref, 1024, 512), (kb_ref, 1536, 512), (vb_ref, 2048, 512))
            for c in range(0, width, MXU_N)]
    z = _dot_nt(hb, wz_ref[...])
    jobs.pop(0)()
    zz = _dot(z.astype(BF16), wup_ref[...]) + bup_ref[...]
    uv = _dot_nt(hb, wm_ref[2560:3072, :])
    jobs.pop(0)()
    la_ref[...] = _log_sigmoid(zz) * (1.0 / GLA_TAU)
    jobs.pop(0)()
    oc_ref[...] = _sgu(uv, ws_ref, bs_ref, jobs)


def _resident_layer(a, l):
    return pl.BlockSpec((None,) + a.shape[1:], lambda *_: (l,) + (0,) * (a.ndim - 1), pipeline_mode=pl.Buffered(1))


def _in_proj_call(x_pieces, norm_g, mod4, w_in_t, l, w_up, b_up, sgu_ws, sgu_bs):
    tm = IN_TM
    n = T_ALL // tm
    row = functools.partial(_mod_row, tm=tm)
    mod_spec = lambda k: pl.BlockSpec((None, None, 1, D_MODEL), lambda i: (row(i), k, 0, 0))
    full = lambda a: pl.BlockSpec(a.shape, lambda i: (0,) * a.ndim)
    out = lambda w: pl.BlockSpec((tm, w), lambda i: (i, 0))
    return pl.pallas_call(
        functools.partial(_in_proj_kernel, n_x=len(x_pieces)),
        grid=(n,),
        in_specs=_piece_specs(x_pieces, tm, D_MODEL) + [full(norm_g), mod_spec(0), mod_spec(1), _resident_layer(w_in_t, l),
                                                        full(w_up), full(b_up), full(sgu_ws), full(sgu_bs)],
        out_specs=[out(1024), out(512), out(512), out(512), out(512), out(W_C)],
        out_shape=[jax.ShapeDtypeStruct((T_ALL, w), F32) for w in (1024, 512, 512, 512, 512)]
        + [jax.ShapeDtypeStruct((T_ALL, W_C), BF16)],
        scratch_shapes=[pltpu.VMEM((W_MAIN, D_MODEL), BF16), pltpu.VMEM((LANES, D_MODEL), BF16)],
        compiler_params=_cparams(("arbitrary",)),
        name="in_proj",
    )(*x_pieces, norm_g, mod4, mod4, w_in_t, w_up, b_up, sgu_ws, sgu_bs)


GLA_SB = 256
GLA_NC = GLA_SB // GLA_CHUNK


def _gla_superblock(q, k, vb, v_heads, la, st_all, tri, mask, same64, head_lanes, forward):
    c = GLA_CHUNK
    mid, last = (c // 2 - 1, c - 1) if forward else (c // 2, 0)
    la_hi, la_lo = _split_bf16(la)
    b = _dot(tri, la_hi) + _dot(tri, la_lo)
    yield
    rows_of = lambda r: jnp.concatenate(
        [jnp.broadcast_to(b[i * c + r:i * c + r + 1, :], (c, W_A)) for i in range(GLA_NC)], axis=0)
    m = rows_of(mid)
    bl = rows_of(last)
    qe = (q * jnp.exp(b - m)).astype(BF16)
    ke = k * jnp.exp(m - b)
    qi = (q * jnp.exp(b)).astype(BF16)
    ks = (k * jnp.exp(bl - b)).astype(BF16)
    o = jnp.zeros((GLA_SB, W_A), F32)
    for h in range(N_HEADS_A):
        ke_h = jnp.where(head_lanes[h], ke, 0.0).astype(BF16)
        s = _dot_nt(qe, ke_h)
        yield
        o = o + _dot(jnp.where(mask, s, 0.0).astype(BF16), v_heads[h])
        yield
    outs = [None] * GLA_NC
    for i in (range(GLA_NC) if forward else reversed(range(GLA_NC))):
        rows = slice(i * c, (i + 1) * c)
        outs[i] = o[rows, :] + _dot_nt(qi[rows, :], st_all.astype(BF16))
        kv = _dot_tn(vb[rows, :], ks[rows, :])
        yield
        st_all = st_all * jnp.exp(bl[i * c:i * c + 1, :]) + jnp.where(same64, kv, 0.0)
    return jnp.concatenate(outs, axis=0), st_all


def _interleave(*gens):
    results = [None] * len(gens)
    active = list(enumerate(gens))
    while active:
        for item in list(active):
            try:
                next(item[1])
            except StopIteration as stop:
                results[item[0]] = stop.value
                active.remove(item)
    return results


def _cast_rider(w, n_blocks):
    e, r, c = w.shape
    per = n_blocks // e
    rows = r // per
    assert per * e == n_blocks and rows * per == r and rows % 16 == 0

    def index(b):
        b = jnp.minimum(b, n_blocks - 1)
        return b // per, b % per, 0

    spec = pl.BlockSpec((None, rows, c), index)
    return spec, jax.ShapeDtypeStruct(w.shape, BF16)


def _gla_kernel(*refs, seq, rider):
    if rider:
        (g4_ref, la_ref, s0f_ref, s0b_ref, gain_ref, w_ref, o_ref, sf_ref, sb_ref, wb_ref, of_ref, ob_ref) = refs
        wb_ref[...] = w_ref[...].astype(BF16)
    else:
        g4_ref, la_ref, s0f_ref, s0b_ref, gain_ref, o_ref, sf_ref, sb_ref, of_ref, ob_ref = refs
    n = GLA_SB
    nsb = seq // n
    r = lax.broadcasted_iota(jnp.int32, (n, n), 0)
    s = lax.broadcasted_iota(jnp.int32, (n, n), 1)
    same64 = (r // GLA_CHUNK) == (s // GLA_CHUNK)
    lower = same64 & (s <= r)
    upper = same64 & (s >= r)
    tri_f = jnp.where(lower, 1.0, 0.0).astype(BF16)
    tri_b = jnp.where(upper, 1.0, 0.0).astype(BF16)
    ones64 = jnp.where(same64, 1.0, 0.0).astype(BF16)
    head_lanes = [(s // DK_A) == h for h in range(N_HEADS_A)]
    scale = DK_A ** -0.5
    expand = lambda st: jnp.where(same64, jnp.concatenate([st] * N_HEADS_A, axis=0), 0.0)
    compact = lambda st_all: functools.reduce(
        lambda a, b: a + b, [st_all[h * 64:(h + 1) * 64, :] for h in range(N_HEADS_A)])

    def step(i, carry):
        stf, stb = carry
        rf = pl.ds(pl.multiple_of(i * n, n), n)
        rb = pl.ds(pl.multiple_of((nsb - 1 - i) * n, n), n)
        def direction(rows, la_cols, st, tri, mask, forward):
            q = g4_ref[rows, 0:256] * scale
            k = g4_ref[rows, 256:512]
            v = g4_ref[rows, 512:768]
            vb = v.astype(BF16)
            v_heads = [jnp.where(head_lanes[h], v, 0.0).astype(BF16) for h in range(N_HEADS_A)]
            return _gla_superblock(q, k, vb, v_heads, la_ref[rows, la_cols], st, tri, mask, same64, head_lanes,
                                   forward)

        (o_f, stf), (o_b, stb) = _interleave(direction(rf, slice(0, 256), stf, tri_f, lower, True),
                                             direction(rb, slice(256, 512), stb, tri_b, upper, False))
        of_ref[rf, :] = o_f
        ob_ref[rb, :] = o_b
        return stf, stb

    stf, stb = lax.fori_loop(0, nsb, step, (expand(s0f_ref[...]), expand(s0b_ref[...])))
    sf_ref[...] = compact(stf)
    sb_ref[...] = compact(stb)

    gain = gain_ref[...]

    def finish(i, carry):
        rows = pl.ds(pl.multiple_of(i * n, n), n)
        o = of_ref[rows, :] + ob_ref[rows, :]
        sq_hi, sq_lo = _split_bf16(o * o)
        ms = (_dot(sq_hi, ones64) + _dot(sq_lo, ones64)) * (1.0 / DK_A)
        y = (o * lax.rsqrt(ms + EPS)) * gain
        o_ref[rows, :] = (y * _silu(g4_ref[rows, 768:1024])).astype(BF16)
        return carry

    lax.fori_loop(0, nsb, finish, 0)


def _gla_call(g4, la, s0f, s0b, gain, cast_w=None, *, batch, seq, row_block0):
    tok = lambda w: pl.BlockSpec((seq, w), lambda b: (row_block0 + b, 0))
    st = pl.BlockSpec((None, 64, 256), lambda b: (b, 0, 0))
    in_specs = [tok(1024), tok(512), st, st, pl.BlockSpec((1, W_A), lambda b: (0, 0))]
    out_specs = [pl.BlockSpec((seq, W_A), lambda b: (b, 0)), st, st]
    out_shape = [jax.ShapeDtypeStruct((batch * seq, W_A), BF16),
                 jax.ShapeDtypeStruct((batch, 64, 256), F32),
                 jax.ShapeDtypeStruct((batch, 64, 256), F32)]
    args = [g4, la, s0f, s0b, gain]
    if cast_w is not None:
        spec, shape = _cast_rider(cast_w, batch)
        in_specs.append(spec)
        out_specs.append(spec)
        out_shape.append(shape)
        args.append(cast_w)
    return pl.pallas_call(
        functools.partial(_gla_kernel, seq=seq, rider=cast_w is not None),
        grid=(batch,),
        in_specs=in_specs,
        out_specs=out_specs,
        out_shape=out_shape,
        scratch_shapes=[pltpu.VMEM((seq, W_A), F32), pltpu.VMEM((seq, W_A), F32)],
        compiler_params=_cparams(("arbitrary",)),
        name=f"gla_{seq}",
    )(*args)


def _state_to_kernel(s):
    b = s.shape[0]
    return jnp.transpose(s, (0, 3, 1, 2)).reshape(b, 64, 256)


def _state_from_kernel(st):
    b = st.shape[0]
    return jnp.transpose(st.reshape(b, 64, N_HEADS_A, DK_A), (0, 2, 3, 1))


def _lambda(lv, lam_init):
    l01 = jnp.sum(lv[0:1, :] * lv[1:2, :], axis=-1, keepdims=True)
    l23 = jnp.sum(lv[2:3, :] * lv[3:4, :], axis=-1, keepdims=True)
    return jnp.exp(l01) - jnp.exp(l23) + lam_init


def _softmax_parts(parts):
    mx = functools.reduce(jnp.maximum, [jnp.max(p, axis=-1, keepdims=True) for p in parts])
    es = [jnp.exp(p - mx) for p in parts]
    den = functools.reduce(lambda a, b: a + b, [jnp.sum(e, axis=-1, keepdims=True) for e in es])
    return [e / den for e in es]


def _diff_finish(o, gain, lam_init):
    o = o * lax.rsqrt(jnp.mean(o * o, axis=-1, keepdims=True) + EPS)
    return ((o * gain) * (1.0 - lam_init)).astype(BF16)


QK_SCALE = DH_B ** -0.5


def _key_halves(k):
    first = lax.broadcasted_iota(jnp.int32, k.shape, 1) < DH_B
    return jnp.where(first, k, 0.0).astype(BF16), jnp.where(first, 0.0, k).astype(BF16)


def _attn_prompt_kernel(lv_ref, q_ref, k_ref, v_ref, gain_ref, *rest, lam_init, n_prev, rider):
    rest = list(rest)
    if rider:
        wb_ref = rest.pop()
        w_ref = rest.pop(2 * n_prev)
        wb_ref[...] = w_ref[...].astype(BF16)
    prev_refs, (o_ref, *cache_refs) = rest[:2 * n_prev], rest[2 * n_prev:]
    lam = _lambda(lv_ref[...], lam_init)

    def head(h):
        cols = slice(h * DV_B, (h + 1) * DV_B)
        q = (q_ref[:, cols] * QK_SCALE).astype(BF16)
        k1, k2 = _key_halves(k_ref[:, cols])
        s1 = _dot_nt(q, k1)
        s2 = _dot_nt(q, k2)
        yield
        (p1,) = _softmax_parts([s1])
        (p2,) = _softmax_parts([s2])
        a = p1 - lam * p2
        o = _dot(a.astype(BF16), v_ref[:, cols].astype(BF16))
        yield
        o_ref[:, cols] = _diff_finish(o, gain_ref[:, cols], lam_init)
        if cache_refs:
            ck_ref, cv_ref = cache_refs
            layers_k = [*prev_refs[0::2], k_ref]
            layers_v = [*prev_refs[1::2], v_ref]
            for l in range(n_prev + 1):
                ck_ref[l, pl.ds(h, SEQ, stride=N_HEADS_B), :] = layers_k[l][:, cols]
                cv_ref[l, pl.ds(h, SEQ, stride=N_HEADS_B), :] = layers_v[l][:, cols]

    _interleave(*[head(h) for h in range(N_HEADS_B)])


def _attn_prompt_call(lv, qb, kb, vb, gain, prev_kv=(), cast_w=None, *, lam_init, write_cache=False):
    blk = pl.BlockSpec((SEQ, W_B), lambda b: (b, 0))
    n_prev = len(prev_kv)
    assert write_cache or not prev_kv
    in_specs = [pl.BlockSpec((4, DH_B), lambda b: (0, 0)), blk, blk, blk,
                pl.BlockSpec((1, W_B), lambda b: (0, 0))] + [blk] * (2 * n_prev)
    args = [lv, qb, kb, vb, gain, *[a for kv in prev_kv for a in kv]]
    out_specs = [blk]
    out_shape = [jax.ShapeDtypeStruct((T_PROMPT, W_B), BF16)]
    if write_cache:
        cache = jax.ShapeDtypeStruct((BATCH, n_prev + 1, SEQ * N_HEADS_B, DV_B), F32)
        cblk = pl.BlockSpec((None, n_prev + 1, SEQ * N_HEADS_B, DV_B), lambda b: (b, 0, 0, 0))
        out_specs += [cblk, cblk]
        out_shape += [cache, cache]
    if cast_w is not None:
        spec, shape = _cast_rider(cast_w, BATCH)
        in_specs.append(spec)
        out_specs.append(spec)
        out_shape.append(shape)
        args.append(cast_w)
    return pl.pallas_call(
        functools.partial(_attn_prompt_kernel, lam_init=lam_init, n_prev=n_prev, rider=cast_w is not None),
        grid=(BATCH,),
        in_specs=in_specs,
        out_specs=out_specs,
        out_shape=out_shape,
        compiler_params=_cparams(("arbitrary",)),
        name="diff_attn_prompt",
    )(*args)


def _rope(x, cos, sin_signed):
    lane = lax.broadcasted_iota(jnp.int32, x.shape, 1)
    first = (lane % (2 * AXIS_PAIRS)) < AXIS_PAIRS
    partner = jnp.where(first, pltpu.roll(x, LANES - AXIS_PAIRS, 1), pltpu.roll(x, AXIS_PAIRS, 1))
    return x * cos + partner * sin_signed


ATT_TQ = 256


def _attn_sample_kernel(lv_ref, q_ref, k_ref, v_ref, kc_ref, vc_ref, cosq_ref, sinq_ref,
                        cosk_ref, sink_ref, gain_ref, o_ref, k1_ref, k2_ref, *, lam_init):
    @pl.when(pl.program_id(1) == 0)
    def _():
        for h in range(N_HEADS_B):
            cols = slice(h * DV_B, (h + 1) * DV_B)
            k1_ref[:, cols], k2_ref[:, cols] = _key_halves(_rope(k_ref[:, cols], cosk_ref[...], sink_ref[...]))

    lam = _lambda(lv_ref[...], lam_init)

    def head(h):
        cols = slice(h * DV_B, (h + 1) * DV_B)
        q = (_rope(q_ref[:, cols], cosq_ref[...], sinq_ref[...]) * QK_SCALE).astype(BF16)
        c1, c2 = _key_halves(kc_ref[:, cols])
        s1 = [_dot_nt(q, k1_ref[:, cols]), _dot_nt(q, c1)]
        s2 = [_dot_nt(q, k2_ref[:, cols]), _dot_nt(q, c2)]
        yield
        p1 = _softmax_parts(s1)
        p2 = _softmax_parts(s2)
        a_own = p1[0] - lam * p2[0]
        a_ctx = p1[1] - lam * p2[1]
        o = (_dot(a_own.astype(BF16), v_ref[:, cols].astype(BF16))
             + _dot(a_ctx.astype(BF16), vc_ref[:, cols].astype(BF16)))
        yield
        o_ref[:, cols] = _diff_finish(o, gain_ref[:, cols], lam_init)

    _interleave(*[head(h) for h in range(N_HEADS_B)])


def _attn_sample_call(lv, qb, kb, vb, kc, vc, cos, sin_signed, gain, *, lam_init):
    tq = ATT_TQ
    nq = DEC_SEQ // tq
    p0 = T_PROMPT // tq
    s0 = T_PROMPT // DEC_SEQ
    qblk = pl.BlockSpec((tq, W_B), lambda b, t: (p0 + b * nq + t, 0))
    kvblk = pl.BlockSpec((DEC_SEQ, W_B), lambda b, t: (s0 + b, 0))
    cblk = pl.BlockSpec((None, PAST_LEN, W_B), lambda b, t: (b, 0, 0))
    return pl.pallas_call(
        functools.partial(_attn_sample_kernel, lam_init=lam_init),
        grid=(DEC_BATCH, nq),
        in_specs=[pl.BlockSpec((4, DH_B), lambda b, t: (0, 0)), qblk, kvblk, kvblk, cblk, cblk,
                  pl.BlockSpec((tq, DV_B), lambda b, t: (t, 0)),
                  pl.BlockSpec((tq, DV_B), lambda b, t: (t, 0)),
                  pl.BlockSpec((DEC_SEQ, DV_B), lambda b, t: (0, 0)),
                  pl.BlockSpec((DEC_SEQ, DV_B), lambda b, t: (0, 0)),
                  pl.BlockSpec((1, W_B), lambda b, t: (0, 0))],
        out_specs=pl.BlockSpec((tq, W_B), lambda b, t: (b * nq + t, 0)),
        out_shape=jax.ShapeDtypeStruct((T_SAMPLE, W_B), BF16),
        scratch_shapes=[pltpu.VMEM((DEC_SEQ, W_B), BF16), pltpu.VMEM((DEC_SEQ, W_B), BF16)],
        compiler_params=_cparams(("arbitrary", "arbitrary")),
        name="diff_attn_sample",
    )(lv, qb, kb, vb, kc, vc, cos, sin_signed, cos, sin_signed, gain)


def _rope_tables():
    rows = DEC_SEQ // GRID_W
    row = jnp.repeat(jnp.arange(rows, dtype=F32), GRID_W)
    col = jnp.tile(jnp.arange(GRID_W, dtype=F32), rows)
    freqs = ROPE_THETA ** (-jnp.arange(AXIS_PAIRS, dtype=F32) / AXIS_PAIRS)
    ar, ac = row[:, None] * freqs, col[:, None] * freqs
    cos64 = jnp.concatenate([jnp.cos(ar), jnp.cos(ar), jnp.cos(ac), jnp.cos(ac)], axis=-1)
    sin64 = jnp.concatenate([-jnp.sin(ar), jnp.sin(ar), -jnp.sin(ac), jnp.sin(ac)], axis=-1)
    return jnp.tile(cos64, (1, 2)), jnp.tile(sin64, (1, 2))


def _group_mean(x, ones64):
    hi, lo = _split_bf16(x)
    return (_dot(hi, ones64) + _dot(lo, ones64)) * (1.0 / DG_C)


def _sgu(uv, ws_ref, bs_ref, fillers=()):
    fillers = list(fillers)
    fill = lambda: fillers.pop(0)() if fillers else None
    r = lax.broadcasted_iota(jnp.int32, (W_C, W_C), 0)
    s = lax.broadcasted_iota(jnp.int32, (W_C, W_C), 1)
    ones64 = jnp.where((r // DG_C) == (s // DG_C), 1.0, 0.0).astype(BF16)
    lane = lax.broadcasted_iota(jnp.int32, (SGU_CHUNK, W_C), 1)
    outs = []
    for n in range(uv.shape[0] // SGU_CHUNK):
        rows = slice(n * SGU_CHUNK, (n + 1) * SGU_CHUNK)
        u = _gelu_tanh(uv[rows, 0:256])
        v = _gelu_tanh(uv[rows, 256:512])
        mu = _group_mean(v, ones64)
        fill()
        d = v - mu
        var = _group_mean(d * d, ones64)
        fill()
        vn = d * lax.rsqrt(var + EPS)
        s_mix = bs_ref[...]
        for g in range(N_GROUPS_C):
            vn_g = jnp.where((lane // DG_C) == g, vn, 0.0).astype(BF16)
            s_mix = s_mix + _dot(ws_ref[g], vn_g)
        fill()
        outs.append((u * s_mix).astype(BF16))
    while fillers:
        fill()
    return jnp.concatenate(outs, axis=0)


OUT_TM = 1024
OUT_SLABS = 4


SEL_LANE0 = N_EXPERTS


def _top2_gates(logits):
    lane = lax.broadcasted_iota(jnp.int32, logits.shape, 1).astype(F32)
    neg = -jnp.inf
    lg = jnp.where(lane < N_EXPERTS, logits, neg)
    m1 = jnp.max(lg, axis=-1, keepdims=True)
    i1 = jnp.min(jnp.where(lg == m1, lane, float(LANES)), axis=-1, keepdims=True)
    lg2 = jnp.where(lane == i1, neg, lg)
    m2 = jnp.max(lg2, axis=-1, keepdims=True)
    i2 = jnp.min(jnp.where(lg2 == m2, lane, float(LANES)), axis=-1, keepdims=True)
    e2 = jnp.exp(m2 - m1)
    den = 1.0 + e2
    gates = jnp.where(lane == i1, 1.0 / den, 0.0) + jnp.where(lane == i2, e2 / den, 0.0)
    sel = jnp.where((lane == i1 + SEL_LANE0) | (lane == i2 + SEL_LANE0), 1.0, 0.0)
    return gates + sel


def _store_token_tiles(ref, val, r0=0):
    n = val.shape[0]
    for k in range(D_MODEL // LANES):
        ref[pl.ds(8 * r0 + k, n, stride=8), :] = val[:, k * LANES:(k + 1) * LANES]


def _load_token_tiles(ref, n):
    return jnp.concatenate([ref[pl.ds(k, n, stride=8), :] for k in range(D_MODEL // LANES)], axis=-1)


def _out_proj_kernel(*refs, n_x, moe):
    oa_refs, ob_refs, (oc_ref,), x_refs = refs[0:2], refs[2:4], refs[4:5], refs[5:5 + n_x]
    w_ref, g1_ref, nrm_ref, sc_ref, sh_ref, *rest = refs[5 + n_x:]
    if moe:
        rw_ref, x1_ref, h2t_ref, gates_ref, gates_t_ref, wo_ref = rest
    else:
        x1_ref, h2_ref, wo_ref = rest

    @pl.when(pl.program_id(0) == 0)
    def _():
        wo_ref[...] = w_ref[...].astype(BF16)

    if moe:
        rw_hi, rw_lo = _split_bf16(rw_ref[...])

    def rows_chain(r0, n):
        rs = slice(r0, r0 + n)
        y = (_dot(_pick_piece(oa_refs, OUT_TM, rs), wo_ref[0:256, :])
             + _dot(_pick_piece(ob_refs, OUT_TM, rs), wo_ref[256:768, :]) + _dot(oc_ref[rs, :], wo_ref[768:1024, :]))
        yield
        x1 = _pick_piece(x_refs, OUT_TM, rs) + g1_ref[...] * y
        x1_ref[rs, :] = x1
        yn = x1 * lax.rsqrt(jnp.mean(x1 * x1, axis=-1, keepdims=True) + EPS)
        h = (yn * nrm_ref[...]) * (1.0 + sc_ref[...]) + sh_ref[...]
        if moe:
            _store_token_tiles(h2t_ref, h, r0)
            h_hi, h_lo = _split_bf16(h)
            logits = _dot(h_hi, rw_hi) + (_dot(h_lo, rw_hi) + _dot(h_hi, rw_lo))
            yield
            gates = _top2_gates(logits)
            gates_ref[rs, :] = gates
            gates_t_ref[:, rs] = gates.T
        else:
            h2_ref[rs, :] = h.astype(BF16)

    n_slab = OUT_TM // OUT_SLABS
    _interleave(*[rows_chain(i * n_slab, n_slab) for i in range(OUT_SLABS)])


def _out_proj_call(oa_pieces, ob_pieces, oc, x_pieces, w_out, l, mod4, norm_g, router_w=None):
    tm = OUT_TM
    n = T_ALL // tm
    moe = router_w is not None
    row = functools.partial(_mod_row, tm=tm)
    mod_spec = lambda k: pl.BlockSpec((None, None, 1, D_MODEL), lambda i: (row(i), k, 0, 0))
    tok = lambda w: pl.BlockSpec((tm, w), lambda i: (i, 0))
    full = lambda a: pl.BlockSpec(a.shape, lambda i: (0,) * a.ndim)
    in_specs = (_piece_specs(oa_pieces, tm, W_A) + _piece_specs(ob_pieces, tm, W_B) + [tok(W_C)]
                + _piece_specs(x_pieces, tm, D_MODEL)
                + [_resident_layer(w_out, l), mod_spec(2), full(norm_g), mod_spec(4), mod_spec(3)])
    args = [*oa_pieces, *ob_pieces, oc, *x_pieces, w_out, mod4, norm_g, mod4, mod4]
    if moe:
        in_specs.append(full(router_w))
        args.append(router_w)
        out_specs = [tok(D_MODEL), pl.BlockSpec((tm * 8, LANES), lambda i: (i, 0)), tok(LANES),
                     pl.BlockSpec((LANES, tm), lambda i: (0, i))]
        out_shape = [jax.ShapeDtypeStruct((T_ALL, D_MODEL), F32), jax.ShapeDtypeStruct((T_ALL * 8, LANES), F32),
                     jax.ShapeDtypeStruct((T_ALL, LANES), F32), jax.ShapeDtypeStruct((LANES, T_ALL), F32)]
    else:
        out_specs = [tok(D_MODEL), tok(D_MODEL)]
        out_shape = [jax.ShapeDtypeStruct((T_ALL, D_MODEL), F32), jax.ShapeDtypeStruct((T_ALL, D_MODEL), BF16)]
    return pl.pallas_call(
        functools.partial(_out_proj_kernel, n_x=len(x_pieces), moe=moe),
        grid=(n,),
        in_specs=in_specs,
        out_specs=out_specs,
        out_shape=out_shape,
        scratch_shapes=[pltpu.VMEM((D_MODEL, D_MODEL), BF16)],
        compiler_params=_cparams(("arbitrary",)),
        name="out_proj_moe" if moe else "out_proj",
    )(*args)


FFN_TM = 512
MXU_N = 256
FFN_SPLITS = (0, 1024, 2048, D_FF)
assert all(s % MXU_N == 0 for s in FFN_SPLITS)


def _swiglu(h, wg_ref, wu_ref, wd_ref):
    out = None
    for c0, c1 in zip(FFN_SPLITS[:-1], FFN_SPLITS[1:]):
        act = _silu(_dot(h, wg_ref[:, c0:c1])) * _dot(h, wu_ref[:, c0:c1])
        d = _dot(act.astype(BF16), wd_ref[c0:c1, :])
        out = d if out is None else out + d
    return out


FFN_RIDER_BLOCKS = 16


def _ffn_kernel(h_ref, x_ref, g2_ref, wg_ref, wu_ref, wd_ref, w_ref, o_ref, wb_ref):
    @pl.when(pl.program_id(0) < FFN_RIDER_BLOCKS)
    def _():
        wb_ref[...] = w_ref[...].astype(BF16)

    o_ref[...] = x_ref[...] + g2_ref[...] * _swiglu(h_ref[...], wg_ref, wu_ref, wd_ref)


def _ffn_call(h2, x1, mod4, wg, wu, wd, cast_w):
    tm = FFN_TM
    row = functools.partial(_mod_row, tm=tm)
    tok = lambda w: pl.BlockSpec((tm, w), lambda i: (i, 0))
    resident = lambda a: pl.BlockSpec(a.shape, lambda i: (0, 0), pipeline_mode=pl.Buffered(1))
    rider_spec, rider_shape = _cast_rider(cast_w, FFN_RIDER_BLOCKS)
    return pl.pallas_call(
        _ffn_kernel,
        grid=(T_ALL // tm,),
        in_specs=[tok(D_MODEL), tok(D_MODEL),
                  pl.BlockSpec((None, None, 1, D_MODEL), lambda i: (row(i), 5, 0, 0)),
                  resident(wg), resident(wu), resident(wd), rider_spec],
        out_specs=[tok(D_MODEL), rider_spec],
        out_shape=[jax.ShapeDtypeStruct((T_ALL, D_MODEL), F32), rider_shape],
        compiler_params=_cparams(("arbitrary",)),
        name="ffn_dense",
    )(h2, x1, mod4, wg, wu, wd, cast_w)


MOE_TM = 512
MOE_NT_MAX = (2 * T_ALL) // MOE_TM + N_EXPERTS
MOE_ROWS = MOE_NT_MAX * MOE_TM
PLAN_BLK = 512
MISC_LAST_START = 8
MISC_NT = 16


def _moe_plan_kernel(gt_ref, posa_ref, posb_ref, te_ref, ti_ref, misc_ref):
    tm = float(MOE_TM)
    sel = gt_ref[SEL_LANE0:SEL_LANE0 + N_EXPERTS, :]
    cnt = jnp.sum(sel, axis=1, keepdims=True)
    nt = jnp.floor((cnt + (tm - 1.0)) * (1.0 / tm))
    sub = lax.broadcasted_iota(jnp.int32, (N_EXPERTS, LANES), 0).astype(F32)
    lane = lax.broadcasted_iota(jnp.int32, (N_EXPERTS, LANES), 1).astype(F32)
    nt_b = jnp.broadcast_to(nt, (N_EXPERTS, LANES))
    nt_row = jnp.sum(jnp.where(sub == lane, nt_b, 0.0), axis=0, keepdims=True)
    toff = jnp.sum(jnp.where(lane < sub, jnp.broadcast_to(nt_row, (N_EXPERTS, LANES)), 0.0),
                   axis=1, keepdims=True)
    tend = toff + nt
    n_total = jnp.sum(nt, axis=0, keepdims=True)
    jc = jnp.minimum(lane, n_total - 1.0)
    te = jnp.sum(jnp.where(jc >= tend, 1.0, 0.0), axis=0, keepdims=True)
    te_ref[...] = te.astype(jnp.int32)
    ti_ref[...] = jc[0:1, :].astype(jnp.int32)
    last_start = (tend - 1.0) * tm
    ls_row = jnp.sum(jnp.where(sub + MISC_LAST_START == lane, jnp.broadcast_to(last_start, (N_EXPERTS, LANES)), 0.0),
                     axis=0, keepdims=True)
    nt_row2 = jnp.sum(jnp.where(sub + MISC_NT == lane, nt_b, 0.0), axis=0, keepdims=True)
    misc = jnp.where(lane[0:1, :] == 0.0, n_total, 0.0) + ls_row + nt_row2
    misc_ref[...] = misc.astype(jnp.int32)

    off = toff * tm
    r = lax.broadcasted_iota(jnp.int32, (PLAN_BLK, PLAN_BLK), 0)
    c = lax.broadcasted_iota(jnp.int32, (PLAN_BLK, PLAN_BLK), 1)
    upper = jnp.where(r <= c, 1.0, 0.0).astype(BF16)
    carry = jnp.zeros((N_EXPERTS, 1), F32)
    for blk in range(T_ALL // PLAN_BLK):
        cols = slice(blk * PLAN_BLK, (blk + 1) * PLAN_BLK)
        s = gt_ref[SEL_LANE0:SEL_LANE0 + N_EXPERTS, cols]
        rank = _dot(s.astype(BF16), upper) + carry
        pos = off + rank - 1.0
        posa_ref[:, cols] = jnp.min(jnp.where(s > 0.0, pos, 1e9), axis=0, keepdims=True).astype(jnp.int32)
        posb_ref[:, cols] = jnp.max(jnp.where(s > 0.0, pos, -1.0), axis=0, keepdims=True).astype(jnp.int32)
        carry = carry + jnp.sum(s, axis=1, keepdims=True)


def _moe_plan_call(gates_t):
    row = lambda w: jax.ShapeDtypeStruct((1, w), jnp.int32)
    full = lambda w: pl.BlockSpec((1, w), lambda: (0, 0))
    return pl.pallas_call(
        _moe_plan_kernel,
        in_specs=[pl.BlockSpec((LANES, T_ALL), lambda: (0, 0))],
        out_specs=[full(T_ALL), full(T_ALL), full(LANES), full(LANES), full(LANES)],
        out_shape=[row(T_ALL), row(T_ALL), row(LANES), row(LANES), row(LANES)],
        compiler_params=pltpu.CompilerParams(vmem_limit_bytes=VMEM_LIMIT),
        name="moe_plan",
    )(gates_t)


DMA_UNROLL = 8


def _row_tile(ref, row):
    return ref.at[pl.ds(pl.multiple_of(row * 8, 8), 8), :]


def _moe_scatter_kernel(misc_ref, posa_ref, posb_ref, h_ref, xs_ref, zero_ref, sem):
    tm = h_ref.shape[0] // 8

    @pl.when(pl.program_id(0) == 0)
    def _():
        zero_ref[...] = jnp.zeros_like(zero_ref)

        def zero_tile(first_row):
            start = pl.multiple_of(first_row * 8, 8)
            cp = pltpu.make_async_copy(zero_ref, xs_ref.at[pl.ds(start, MOE_TM * 8), :], sem.at[0])
            cp.start()
            cp.wait()

        for e in range(N_EXPERTS):
            @pl.when(misc_ref[0, MISC_NT + e] > 0)
            def _():
                zero_tile(misc_ref[0, MISC_LAST_START + e])

        def zero_tail(j, carry):
            zero_tile(j * MOE_TM)
            return carry

        lax.fori_loop(misc_ref[0, 0], MOE_NT_MAX, zero_tail, 0)

    def issue(r, carry):
        src = _row_tile(h_ref, r)
        pltpu.make_async_copy(src, _row_tile(xs_ref, posa_ref[0, r]), sem.at[0]).start(priority=0)
        pltpu.make_async_copy(src, _row_tile(xs_ref, posb_ref[0, r]), sem.at[1]).start(priority=1)
        return carry

    lax.fori_loop(0, tm, issue, 0, unroll=DMA_UNROLL)
    for k in range(2):
        pltpu.make_async_copy(h_ref, xs_ref.at[pl.ds(0, tm * 8), :], sem.at[k]).wait()


SCATTER_TM = 512


def _moe_scatter_call(misc, posa3, posb3, h2t):
    tm = SCATTER_TM
    smem_row = pl.BlockSpec((None, 1, tm), lambda i: (i, 0, 0), memory_space=pltpu.SMEM)
    return pl.pallas_call(
        _moe_scatter_kernel,
        grid=(T_ALL // tm,),
        in_specs=[pl.BlockSpec((1, LANES), lambda i: (0, 0), memory_space=pltpu.SMEM), smem_row, smem_row,
                  pl.BlockSpec((tm * 8, LANES), lambda i: (i, 0))],
        out_specs=pl.BlockSpec(memory_space=pl.ANY),
        out_shape=jax.ShapeDtypeStruct((MOE_ROWS * 8, LANES), F32),
        scratch_shapes=[pltpu.VMEM((MOE_TM * 8, LANES), F32), pltpu.SemaphoreType.DMA((2,))],
        compiler_params=_cparams(("arbitrary",)),
        name="moe_scatter",
    )(misc, posa3, posb3, h2t)


def _ffn_grouped_kernel(te_ref, ti_ref, misc_ref, x_ref, wg_ref, wu_ref, wd_ref, o_ref):
    j = pl.program_id(0)

    @pl.when(j < misc_ref[0])
    def _():
        h = _load_token_tiles(x_ref, MOE_TM).astype(BF16)
        _store_token_tiles(o_ref, _swiglu(h, wg_ref, wu_ref, wd_ref))

    @pl.when(j >= misc_ref[0])
    def _():
        o_ref[...] = jnp.zeros_like(o_ref)


def _ffn_grouped_call(te, ti, misc, xs, wg, wu, wd):
    expert = lambda *s: pl.BlockSpec((None,) + s, lambda j, te, ti, misc: (te[j], 0, 0))
    grid_spec = pltpu.PrefetchScalarGridSpec(
        num_scalar_prefetch=3,
        grid=(MOE_NT_MAX,),
        in_specs=[pl.BlockSpec((MOE_TM * 8, LANES), lambda j, te, ti, misc: (ti[j], 0)),
                  expert(D_MODEL, D_FF), expert(D_MODEL, D_FF), expert(D_FF, D_MODEL)],
        out_specs=pl.BlockSpec((MOE_TM * 8, LANES), lambda j, te, ti, misc: (j, 0)),
    )
    return pl.pallas_call(
        _ffn_grouped_kernel,
        grid_spec=grid_spec,
        out_shape=jax.ShapeDtypeStruct((MOE_ROWS * 8, LANES), F32),
        compiler_params=_cparams(("arbitrary",)),
        name="ffn_grouped",
    )(te, ti, misc, xs, wg, wu, wd)


COMBINE_TM = 256


def _moe_combine_kernel(posa_ref, posb_ref, posa_next_ref, posb_next_ref, ys_ref, x_ref, g2_ref, gates_ref, nf_ref,
                        op_ref, os_ref, bufa_ref, bufb_ref, sem):
    tm = COMBINE_TM
    i = pl.program_id(0)
    n = pl.num_programs(0)
    slot = i % 2

    def gather(pa_ref, pb_ref, s):
        def issue(r, carry):
            pltpu.make_async_copy(_row_tile(ys_ref, pa_ref[0, r]), _row_tile(bufa_ref.at[s], r),
                                  sem.at[s, 0]).start(priority=0)
            pltpu.make_async_copy(_row_tile(ys_ref, pb_ref[0, r]), _row_tile(bufb_ref.at[s], r),
                                  sem.at[s, 1]).start(priority=1)
            return carry

        lax.fori_loop(0, tm, issue, 0, unroll=DMA_UNROLL)

    @pl.when(i == 0)
    def _():
        gather(posa_ref, posb_ref, 0)

    @pl.when(i + 1 < n)
    def _():
        gather(posa_next_ref, posb_next_ref, 1 - slot)

    gates = gates_ref[...]
    lane = lax.broadcasted_iota(jnp.int32, gates.shape, 1).astype(F32)
    is_sel = (lane >= SEL_LANE0) & (lane < SEL_LANE0 + N_EXPERTS) & (gates > 0.0)
    ia = jnp.min(jnp.where(is_sel, lane, float(LANES)), axis=-1, keepdims=True) - SEL_LANE0
    ib = jnp.max(jnp.where(is_sel, lane, -1.0), axis=-1, keepdims=True) - SEL_LANE0
    wa = jnp.sum(jnp.where(lane == ia, gates, 0.0), axis=-1, keepdims=True)
    wb = jnp.sum(jnp.where(lane == ib, gates, 0.0), axis=-1, keepdims=True)

    pltpu.make_async_copy(ys_ref.at[pl.ds(0, tm * 8), :], bufa_ref.at[slot], sem.at[slot, 0]).wait()
    pltpu.make_async_copy(ys_ref.at[pl.ds(0, tm * 8), :], bufb_ref.at[slot], sem.at[slot, 1]).wait()

    y = wa * _load_token_tiles(bufa_ref.at[slot], tm) + wb * _load_token_tiles(bufb_ref.at[slot], tm)
    out = x_ref[...] + g2_ref[...] * y
    out = (out * lax.rsqrt(jnp.mean(out * out, axis=-1, keepdims=True) + EPS)) * nf_ref[...]

    @pl.when(i < T_PROMPT // tm)
    def _():
        op_ref[...] = out

    @pl.when(i >= T_PROMPT // tm)
    def _():
        os_ref[...] = out


def _moe_combine_call(posa3, posb3, ys, x1, mod4, gates, norm_f):
    tm = COMBINE_TM
    n = T_ALL // tm
    n_p = T_PROMPT // tm
    row = functools.partial(_mod_row, tm=tm)
    smem_row = pl.BlockSpec((None, 1, tm), lambda i: (i, 0, 0), memory_space=pltpu.SMEM)
    smem_next = pl.BlockSpec((None, 1, tm), lambda i: (jnp.minimum(i + 1, n - 1), 0, 0), memory_space=pltpu.SMEM)
    tok = lambda w: pl.BlockSpec((tm, w), lambda i: (i, 0))
    return pl.pallas_call(
        _moe_combine_kernel,
        grid=(n,),
        in_specs=[smem_row, smem_row, smem_next, smem_next, pl.BlockSpec(memory_space=pl.ANY), tok(D_MODEL),
                  pl.BlockSpec((None, None, 1, D_MODEL), lambda i: (row(i), 5, 0, 0)), tok(LANES),
                  pl.BlockSpec((1, D_MODEL), lambda i: (0, 0))],
        out_specs=[pl.BlockSpec((tm, D_MODEL), lambda i: (jnp.minimum(i, n_p - 1), 0)),
                   pl.BlockSpec((tm, D_MODEL), lambda i: (jnp.maximum(i - n_p, 0), 0))],
        out_shape=[jax.ShapeDtypeStruct((T_PROMPT, D_MODEL), F32), jax.ShapeDtypeStruct((T_SAMPLE, D_MODEL), F32)],
        scratch_shapes=[pltpu.VMEM((2, tm * 8, LANES), F32), pltpu.VMEM((2, tm * 8, LANES), F32),
                        pltpu.SemaphoreType.DMA((2, 2))],
        compiler_params=_cparams(("arbitrary",)),
        name="moe_combine",
    )(posa3, posb3, posa3, posb3, ys, x1, mod4, gates, norm_f)


def _moe_call(h2t, x1, mod4, gates, gates_t, wg, wu, wd, norm_f):
    posa, posb, te, ti, misc = _moe_plan_call(gates_t)
    xs = _moe_scatter_call(misc, posa.reshape(T_ALL // SCATTER_TM, 1, SCATTER_TM),
                           posb.reshape(T_ALL // SCATTER_TM, 1, SCATTER_TM), h2t)
    ys = _ffn_grouped_call(te.reshape(LANES), ti.reshape(LANES), misc.reshape(LANES), xs, wg, wu, wd)
    return _moe_combine_call(posa.reshape(T_ALL // COMBINE_TM, 1, COMBINE_TM),
                             posb.reshape(T_ALL // COMBINE_TM, 1, COMBINE_TM), ys, x1, mod4, gates, norm_f)


def kernel(x_prompt, x_sample, cache_k, cache_v, state_gla_fwd, state_gla_bwd, c, c_ctx, w_ada, b_ada, norm_mix, norm_ffn, w_in, w_out, gla_w_up, gla_b_up, gla_norm, diff_lambda, diff_norm, sgu_w, sgu_b, ffn_w_gate, ffn_w_up, ffn_w_down, router_w, moe_w_gate, moe_w_up, moe_w_down, norm_f):
    assert DEPTH == 2
    x_pieces = [x_prompt.reshape(T_PROMPT, D_MODEL), x_sample.reshape(T_SAMPLE, D_MODEL)]
    cvecs = jnp.concatenate([c_ctx[None, :], c, jnp.zeros((N_MOD_ROWS - 1 - DEC_BATCH, D_MODEL), F32)], axis=0)
    mod = _ada_call(cvecs, w_ada, b_ada)
    cos, sin_signed = _rope_tables()
    zeros_state = jnp.zeros((BATCH, 64, 256), F32)

    w_in_t = jnp.swapaxes(w_in, 1, 2)
    moe_w = [moe_w_gate[0], moe_w_up[0], moe_w_down[0]]
    prev_kv, sfs, sbs = [], [], []
    for l in range(DEPTH):
        mod4 = mod[l].reshape(N_MOD_ROWS, 6, 1, D_MODEL)
        w_up = jnp.zeros((LANES, 2 * W_A), F32)
        w_up = w_up.at[0:GLA_RANK, 0:W_A].set(gla_w_up[l, 0]).at[GLA_RANK:2 * GLA_RANK, W_A:].set(gla_w_up[l, 1])
        b_up = gla_b_up[l].reshape(1, 2 * W_A)
        bs_full = jnp.repeat(sgu_b[l].T, DG_C, axis=1)
        g4, la, qb, kb, vb, oc = _in_proj_call(x_pieces, norm_mix[l][None, :], mod4, w_in_t, l, w_up.astype(BF16), b_up,
                                               sgu_w[l].astype(BF16), bs_full)

        gain_a = gla_norm[l][None, :]
        oa_p, sf, sb, moe_w[l] = _gla_call(g4, la, zeros_state, zeros_state, gain_a, moe_w[l],
                                           batch=BATCH, seq=SEQ, row_block0=0)
        oa_s, _, _ = _gla_call(g4, la, _state_to_kernel(state_gla_fwd[:, l]),
                               _state_to_kernel(state_gla_bwd[:, l]), gain_a,
                               batch=DEC_BATCH, seq=DEC_SEQ, row_block0=T_PROMPT // DEC_SEQ)

        lam_init = 0.8 - 0.6 * math.exp(-0.3 * l)
        gain_b = diff_norm[l][None, :]
        if l < DEPTH - 1:
            (ob_p,) = _attn_prompt_call(diff_lambda[l], qb, kb, vb, gain_b, lam_init=lam_init)
            prev_kv.append((kb, vb))
        else:
            ob_p, new_cache_k, new_cache_v = _attn_prompt_call(diff_lambda[l], qb, kb, vb, gain_b, prev_kv,
                                                               lam_init=lam_init, write_cache=True)
        ob_s = _attn_sample_call(diff_lambda[l], qb, kb, vb,
                                 cache_k[:, l].reshape(DEC_BATCH, PAST_LEN, W_B),
                                 cache_v[:, l].reshape(DEC_BATCH, PAST_LEN, W_B),
                                 cos, sin_signed, gain_b, lam_init=lam_init)

        if l == 0:
            x1, h2 = _out_proj_call([oa_p, oa_s], [ob_p, ob_s], oc, x_pieces, w_out, l, mod4, norm_ffn[l][None, :])
            x_next, moe_w[2] = _ffn_call(h2, x1, mod4, ffn_w_gate[0].astype(BF16), ffn_w_up[0].astype(BF16),
                                         ffn_w_down[0].astype(BF16), moe_w[2])
            x_pieces = [x_next]
        else:
            rw = jnp.pad(router_w[0], ((0, 0), (0, LANES - N_EXPERTS)))
            x1, h2t, gates, gates_t = _out_proj_call([oa_p, oa_s], [ob_p, ob_s], oc, x_pieces, w_out, l, mod4,
                                                     norm_ffn[l][None, :], rw)
            y_prompt, y_sample = _moe_call(h2t, x1, mod4, gates, gates_t, *moe_w, norm_f[None, :])

        sfs.append(_state_from_kernel(sf))
        sbs.append(_state_from_kernel(sb))

    return (y_prompt.reshape(BATCH, SEQ, D_MODEL), y_sample.reshape(DEC_BATCH, DEC_SEQ, D_MODEL),
            new_cache_k.reshape(BATCH, DEPTH, SEQ, N_HEADS_B, DV_B), new_cache_v.reshape(BATCH, DEPTH, SEQ, N_HEADS_B, DV_B),
            jnp.stack(sfs, axis=1), jnp.stack(sbs, axis=1))
```

```python
import functools
import math

import jax
import jax.numpy as jnp
import numpy as np
from jax import lax
from jax.experimental import pallas as pl
from jax.experimental.pallas import tpu as pltpu

F32 = jnp.float32
BF16 = jnp.bfloat16

D_MODEL = 1024
BATCH = 32
SEQ = 256
DEPTH = 2
DEC_BATCH = 2
DEC_SEQ = 1024
PAST_LEN = 256
GRID_W = 64
N_HEADS_A = 4
DK_A = 64
W_A = 256
GLA_RANK = 16
GLA_TAU = 16.0
GLA_CHUNK = 64
N_HEADS_B = 4
DH_B = 64
DV_B = 128
W_B = 512
ROPE_THETA = 10000.0
AXIS_PAIRS = DH_B // 4
N_GROUPS_C = 4
DG_C = 64
W_C = 256
SGU_CHUNK = 128
D_FF = 2816
N_EXPERTS = 8
EPS = 1e-6

T_PROMPT = BATCH * SEQ
T_SAMPLE = DEC_BATCH * DEC_SEQ
T_ALL = T_PROMPT + T_SAMPLE
N_MOD_ROWS = 8
LANES = 128
VMEM_LIMIT = 56 * 1024 * 1024


def _cparams(sem):
    return pltpu.CompilerParams(dimension_semantics=sem, vmem_limit_bytes=VMEM_LIMIT)


def _dot(a, b):
    return jnp.dot(a, b, preferred_element_type=F32)


def _dot_nt(a, b):
    return lax.dot_general(a, b, (((1,), (1,)), ((), ())), preferred_element_type=F32)


def _dot_tn(a, b):
    return lax.dot_general(a, b, (((0,), (0,)), ((), ())), preferred_element_type=F32)


def _split_bf16(x):
    hi = x.astype(BF16)
    lo = (x - hi.astype(F32)).astype(BF16)
    return hi, lo


def _dot3(a, w):
    a_hi, a_lo = _split_bf16(a)
    w_hi, w_lo = _split_bf16(w)
    return _dot(a_hi, w_hi) + (_dot(a_lo, w_hi) + _dot(a_hi, w_lo))


def _sigmoid(x):
    return 1.0 / (1.0 + jnp.exp(-x))


def _silu(x):
    return x * _sigmoid(x)


def _gelu_tanh(x):
    c = math.sqrt(2.0 / math.pi)
    return x * (0.5 * (1.0 + jnp.tanh(c * (x + 0.044715 * (x * x * x)))))


def _log_sigmoid(x):
    return jnp.minimum(x, 0.0) - jnp.log(1.0 + jnp.exp(-jnp.abs(x)))


def _mod_row(i, tm):
    n_p = T_PROMPT // tm
    per_b = DEC_SEQ // tm
    return jnp.where(i < n_p, 0, 1 + (i - n_p) // per_b)


ADA_TN = 1536


def _ada_kernel(c_ref, w_ref, b_ref, o_ref):
    a = _silu(c_ref[...])
    o_ref[...] = _dot3(a, w_ref[...]) + b_ref[...]


def _ada_call(cvecs, w_ada, b_ada):
    n_col = (6 * D_MODEL) // ADA_TN
    return pl.pallas_call(
        _ada_kernel,
        grid=(DEPTH, n_col),
        in_specs=[
            pl.BlockSpec((N_MOD_ROWS, D_MODEL), lambda l, j: (0, 0)),
            pl.BlockSpec((None, D_MODEL, ADA_TN), lambda l, j: (l, 0, j)),
            pl.BlockSpec((None, 1, ADA_TN), lambda l, j: (l, 0, j)),
        ],
        out_specs=pl.BlockSpec((None, N_MOD_ROWS, ADA_TN), lambda l, j: (l, 0, j)),
        out_shape=jax.ShapeDtypeStruct((DEPTH, N_MOD_ROWS, 6 * D_MODEL), F32),
        compiler_params=_cparams(("arbitrary", "arbitrary")),
        name="ada_mod",
    )(cvecs, w_ada, b_ada.reshape(DEPTH, 1, 6 * D_MODEL))


IN_TM = 512
IN_COLS = 3104
Z_COL0 = 1024
Z_COLS = 2 * GLA_RANK
W_MAIN = 3072


def _piece_specs(pieces, tm, width):
    specs, t0 = [], 0
    for arr in pieces:
        nt = arr.shape[0] // tm
        specs.append(pl.BlockSpec((tm, width), lambda i, t0=t0, nt=nt: (jnp.clip(i - t0, 0, nt - 1), 0)))
        t0 += nt
    assert t0 * tm == T_ALL and len(pieces) in (1, 2) and (len(pieces) == 1 or pieces[0].shape[0] == T_PROMPT)
    return specs


def _pick_piece(refs, tm, rows=slice(None)):
    if len(refs) == 1:
        return refs[0][rows, :]
    return jnp.where(pl.program_id(0) < T_PROMPT // tm, refs[0][rows, :], refs[1][rows, :])


def _in_proj_kernel(*refs, n_x):
    x_refs = refs[:n_x]
    (nrm_ref, sh_ref, sc_ref, w_ref, wup_ref, bup_ref, ws_ref, bs_ref,
     g4_ref, la_ref, qb_ref, kb_ref, vb_ref, oc_ref, wm_ref, wz_ref) = refs[n_x:]

    @pl.when(pl.program_id(0) == 0)
    def _():
        wm_ref[0:Z_COL0, :] = w_ref[0:Z_COL0, :].astype(BF16)
        wm_ref[Z_COL0:W_MAIN, :] = w_ref[Z_COL0 + Z_COLS:IN_COLS, :].astype(BF16)
        wz_ref[...] = jnp.zeros_like(wz_ref)
        wz_ref[0:Z_COLS, :] = w_ref[Z_COL0:Z_COL0 + Z_COLS, :].astype(BF16)

    x = _pick_piece(x_refs, IN_TM)
    y = x * lax.rsqrt(jnp.mean(x * x, axis=-1, keepdims=True) + EPS)
    h = (y * nrm_ref[...]) * (1.0 + sc_ref[...]) + sh_ref[...]
    hb = h.astype(BF16)
    def plain_job(ref, c0, w0):
        def job():
            ref[:, c0:c0 + MXU_N] = _dot_nt(hb, wm_ref[w0:w0 + MXU_N, :])
        return job

    jobs = [plain_job(ref, c, w0 + c) for ref, w0, width in
            ((g4_ref, 0, 1024), (qb_ref, 1024, 512), (kb_ref, 1536, 512), (vb_ref, 2048, 512))
            for c in range(0, width, MXU_N)]
    z = _dot_nt(hb, wz_ref[...])
    jobs.pop(0)()
    zz = _dot(z.astype(BF16), wup_ref[...]) + bup_ref[...]
    uv = _dot_nt(hb, wm_ref[2560:3072, :])
    jobs.pop(0)()
    la_ref[...] = _log_sigmoid(zz) * (1.0 / GLA_TAU)
    jobs.pop(0)()
    oc_ref[...] = _sgu(uv, ws_ref, bs_ref, jobs)


def _resident_layer(a, l):
    return pl.BlockSpec((None,) + a.shape[1:], lambda *_: (l,) + (0,) * (a.ndim - 1), pipeline_mode=pl.Buffered(1))


def _in_proj_call(x_pieces, norm_g, mod4, w_in_t, l, w_up, b_up, sgu_ws, sgu_bs):
    tm = IN_TM
    n = T_ALL // tm
    row = functools.partial(_mod_row, tm=tm)
    mod_spec = lambda k: pl.BlockSpec((None, None, 1, D_MODEL), lambda i: (row(i), k, 0, 0))
    full = lambda a: pl.BlockSpec(a.shape, lambda i: (0,) * a.ndim)
    out = lambda w: pl.BlockSpec((tm, w), lambda i: (i, 0))
    return pl.pallas_call(
        functools.partial(_in_proj_kernel, n_x=len(x_pieces)),
        grid=(n,),
        in_specs=_piece_specs(x_pieces, tm, D_MODEL) + [full(norm_g), mod_spec(0), mod_spec(1), _resident_layer(w_in_t, l),
                                                        full(w_up), full(b_up), full(sgu_ws), full(sgu_bs)],
        out_specs=[out(1024), out(512), out(512), out(512), out(512), out(W_C)],
        out_shape=[jax.ShapeDtypeStruct((T_ALL, w), F32) for w in (1024, 512, 512, 512, 512)]
        + [jax.ShapeDtypeStruct((T_ALL, W_C), BF16)],
        scratch_shapes=[pltpu.VMEM((W_MAIN, D_MODEL), BF16), pltpu.VMEM((LANES, D_MODEL), BF16)],
        compiler_params=_cparams(("arbitrary",)),
        name="in_proj",
    )(*x_pieces, norm_g, mod4, mod4, w_in_t, w_up, b_up, sgu_ws, sgu_bs)


GLA_SB = 256
GLA_NC = GLA_SB // GLA_CHUNK


def _head_blocks(x, same64):
    return jnp.where(same64, jnp.concatenate([x] * N_HEADS_A, axis=0), 0.0).astype(BF16)


def _gla_superblock(q, k, vb, v_blocks, la, st_all, tri, mask4, same64, forward):
    c = GLA_CHUNK
    mid, last = (c // 2 - 1, c - 1) if forward else (c // 2, 0)
    la_hi, la_lo = _split_bf16(la)
    b = _dot(tri, la_hi) + _dot(tri, la_lo)
    yield
    rows_of = lambda r: jnp.concatenate(
        [jnp.broadcast_to(b[i * c + r:i * c + r + 1, :], (c, W_A)) for i in range(GLA_NC)], axis=0)
    m = rows_of(mid)
    bl = rows_of(last)
    qe = (q * jnp.exp(b - m)).astype(BF16)
    ke = k * jnp.exp(m - b)
    qi = (q * jnp.exp(b)).astype(BF16)
    ks = (k * jnp.exp(bl - b)).astype(BF16)
    outs = [None] * GLA_NC
    for i in (range(GLA_NC) if forward else reversed(range(GLA_NC))):
        rows = slice(i * c, (i + 1) * c)
        s = _dot_nt(qe[rows, :], _head_blocks(ke[rows, :], same64))
        kv = _dot_tn(vb[rows, :], ks[rows, :])
        yield
        a = jnp.where(mask4, s, 0.0).astype(BF16)
        outs[i] = _dot(a, v_blocks[i]) + _dot_nt(qi[rows, :], st_all.astype(BF16))
        st_all = st_all * jnp.exp(bl[i * c:i * c + 1, :]) + jnp.where(same64, kv, 0.0)
        yield
    return jnp.concatenate(outs, axis=0), st_all


def _interleave(*gens):
    results = [None] * len(gens)
    active = list(enumerate(gens))
    while active:
        for item in list(active):
            try:
                next(item[1])
            except StopIteration as stop:
                results[item[0]] = stop.value
                active.remove(item)
    return results


def _cast_rider(w, n_blocks):
    e, r, c = w.shape
    per = n_blocks // e
    rows = r // per
    assert per * e == n_blocks and rows * per == r and rows % 16 == 0

    def index(b):
        b = jnp.minimum(b, n_blocks - 1)
        return b // per, b % per, 0

    spec = pl.BlockSpec((None, rows, c), index)
    return spec, jax.ShapeDtypeStruct(w.shape, BF16)


def _gla_kernel(*refs, seq, rider):
    if rider:
        (g4_ref, la_ref, s0f_ref, s0b_ref, gain_ref, w_ref, o_ref, sf_ref, sb_ref, wb_ref, of_ref, ob_ref) = refs
        wb_ref[...] = w_ref[...].astype(BF16)
    else:
        g4_ref, la_ref, s0f_ref, s0b_ref, gain_ref, o_ref, sf_ref, sb_ref, of_ref, ob_ref = refs
    n = GLA_SB
    nsb = seq // n
    r = lax.broadcasted_iota(jnp.int32, (n, n), 0)
    s = lax.broadcasted_iota(jnp.int32, (n, n), 1)
    same64 = (r // GLA_CHUNK) == (s // GLA_CHUNK)
    lower = same64 & (s <= r)
    upper = same64 & (s >= r)
    tri_f = jnp.where(lower, 1.0, 0.0).astype(BF16)
    tri_b = jnp.where(upper, 1.0, 0.0).astype(BF16)
    ones64 = jnp.where(same64, 1.0, 0.0).astype(BF16)
    key_row = lax.broadcasted_iota(jnp.int32, (GLA_CHUNK, n), 1) % GLA_CHUNK
    qry_row = lax.broadcasted_iota(jnp.int32, (GLA_CHUNK, n), 0)
    mask4_f = key_row <= qry_row
    mask4_b = key_row >= qry_row
    scale = DK_A ** -0.5
    expand = lambda st: jnp.where(same64, jnp.concatenate([st] * N_HEADS_A, axis=0), 0.0)
    compact = lambda st_all: functools.reduce(
        lambda a, b: a + b, [st_all[h * 64:(h + 1) * 64, :] for h in range(N_HEADS_A)])

    def step(i, carry):
        stf, stb = carry
        rf = pl.ds(pl.multiple_of(i * n, n), n)
        rb = pl.ds(pl.multiple_of((nsb - 1 - i) * n, n), n)
        def direction(rows, la_cols, st, tri, mask4, forward):
            q = g4_ref[rows, 0:256] * scale
            k = g4_ref[rows, 256:512]
            v = g4_ref[rows, 512:768]
            v_blocks = [_head_blocks(v[i * GLA_CHUNK:(i + 1) * GLA_CHUNK, :], same64) for i in range(GLA_NC)]
            return _gla_superblock(q, k, v.astype(BF16), v_blocks, la_ref[rows, la_cols], st, tri, mask4, same64,
                                   forward)

        (o_f, stf), (o_b, stb) = _interleave(direction(rf, slice(0, 256), stf, tri_f, mask4_f, True),
                                             direction(rb, slice(256, 512), stb, tri_b, mask4_b, False))
        of_ref[rf, :] = o_f
        ob_ref[rb, :] = o_b
        return stf, stb

    stf, stb = lax.fori_loop(0, nsb, step, (expand(s0f_ref[...]), expand(s0b_ref[...])))
    sf_ref[...] = compact(stf)
    sb_ref[...] = compact(stb)

    gain = gain_ref[...]

    def finish(i, carry):
        rows = pl.ds(pl.multiple_of(i * n, n), n)
        o = of_ref[rows, :] + ob_ref[rows, :]
        sq_hi, sq_lo = _split_bf16(o * o)
        ms = (_dot(sq_hi, ones64) + _dot(sq_lo, ones64)) * (1.0 / DK_A)
        y = (o * lax.rsqrt(ms + EPS)) * gain
        o_ref[rows, :] = (y * _silu(g4_ref[rows, 768:1024])).astype(BF16)
        return carry

    lax.fori_loop(0, nsb, finish, 0)


def _gla_call(g4, la, s0f, s0b, gain, cast_w=None, *, batch, seq, row_block0):
    tok = lambda w: pl.BlockSpec((seq, w), lambda b: (row_block0 + b, 0))
    st = pl.BlockSpec((None, 64, 256), lambda b: (b, 0, 0))
    in_specs = [tok(1024), tok(512), st, st, pl.BlockSpec((1, W_A), lambda b: (0, 0))]
    out_specs = [pl.BlockSpec((seq, W_A), lambda b: (b, 0)), st, st]
    out_shape = [jax.ShapeDtypeStruct((batch * seq, W_A), BF16),
                 jax.ShapeDtypeStruct((batch, 64, 256), F32),
                 jax.ShapeDtypeStruct((batch, 64, 256), F32)]
    args = [g4, la, s0f, s0b, gain]
    if cast_w is not None:
        spec, shape = _cast_rider(cast_w, batch)
        in_specs.append(spec)
        out_specs.append(spec)
        out_shape.append(shape)
        args.append(cast_w)
    return pl.pallas_call(
        functools.partial(_gla_kernel, seq=seq, rider=cast_w is not None),
        grid=(batch,),
        in_specs=in_specs,
        out_specs=out_specs,
        out_shape=out_shape,
        scratch_shapes=[pltpu.VMEM((seq, W_A), F32), pltpu.VMEM((seq, W_A), F32)],
        compiler_params=_cparams(("arbitrary",)),
        name=f"gla_{seq}",
    )(*args)


def _state_to_kernel(s):
    b = s.shape[0]
    return jnp.transpose(s, (0, 3, 1, 2)).reshape(b, 64, 256)


def _state_from_kernel(st):
    b = st.shape[0]
    return jnp.transpose(st.reshape(b, 64, N_HEADS_A, DK_A), (0, 2, 3, 1))


def _lambda(lv, lam_init):
    l01 = jnp.sum(lv[0:1, :] * lv[1:2, :], axis=-1, keepdims=True)
    l23 = jnp.sum(lv[2:3, :] * lv[3:4, :], axis=-1, keepdims=True)
    return jnp.exp(l01) - jnp.exp(l23) + lam_init


def _softmax_parts(parts):
    mx = functools.reduce(jnp.maximum, [jnp.max(p, axis=-1, keepdims=True) for p in parts])
    es = [jnp.exp(p - mx) for p in parts]
    den = functools.reduce(lambda a, b: a + b, [jnp.sum(e, axis=-1, keepdims=True) for e in es])
    return [e / den for e in es]


def _diff_finish(o, gain, lam_init):
    o = o * lax.rsqrt(jnp.mean(o * o, axis=-1, keepdims=True) + EPS)
    return ((o * gain) * (1.0 - lam_init)).astype(BF16)


QK_SCALE = DH_B ** -0.5


def _key_halves(k):
    first = lax.broadcasted_iota(jnp.int32, k.shape, 1) < DH_B
    return jnp.where(first, k, 0.0).astype(BF16), jnp.where(first, 0.0, k).astype(BF16)


def _attn_prompt_kernel(lv_ref, q_ref, k_ref, v_ref, gain_ref, *rest, lam_init, n_prev, rider):
    rest = list(rest)
    if rider:
        wb_ref = rest.pop()
        w_ref = rest.pop(2 * n_prev)
        wb_ref[...] = w_ref[...].astype(BF16)
    prev_refs, (o_ref, *cache_refs) = rest[:2 * n_prev], rest[2 * n_prev:]
    lam = _lambda(lv_ref[...], lam_init)

    def head(h):
        cols = slice(h * DV_B, (h + 1) * DV_B)
        q = (q_ref[:, cols] * QK_SCALE).astype(BF16)
        k1, k2 = _key_halves(k_ref[:, cols])
        s1 = _dot_nt(q, k1)
        s2 = _dot_nt(q, k2)
        yield
        (p1,) = _softmax_parts([s1])
        (p2,) = _softmax_parts([s2])
        a = p1 - lam * p2
        o = _dot(a.astype(BF16), v_ref[:, cols].astype(BF16))
        yield
        o_ref[:, cols] = _diff_finish(o, gain_ref[:, cols], lam_init)
        if cache_refs:
            ck_ref, cv_ref = cache_refs
            layers_k = [*prev_refs[0::2], k_ref]
            layers_v = [*prev_refs[1::2], v_ref]
            for l in range(n_prev + 1):
                ck_ref[l, pl.ds(h, SEQ, stride=N_HEADS_B), :] = layers_k[l][:, cols]
                cv_ref[l, pl.ds(h, SEQ, stride=N_HEADS_B), :] = layers_v[l][:, cols]

    _interleave(*[head(h) for h in range(N_HEADS_B)])


def _attn_prompt_call(lv, qb, kb, vb, gain, prev_kv=(), cast_w=None, *, lam_init, write_cache=False):
    blk = pl.BlockSpec((SEQ, W_B), lambda b: (b, 0))
    n_prev = len(prev_kv)
    assert write_cache or not prev_kv
    in_specs = [pl.BlockSpec((4, DH_B), lambda b: (0, 0)), blk, blk, blk,
                pl.BlockSpec((1, W_B), lambda b: (0, 0))] + [blk] * (2 * n_prev)
    args = [lv, qb, kb, vb, gain, *[a for kv in prev_kv for a in kv]]
    out_specs = [blk]
    out_shape = [jax.ShapeDtypeStruct((T_PROMPT, W_B), BF16)]
    if write_cache:
        cache = jax.ShapeDtypeStruct((BATCH, n_prev + 1, SEQ * N_HEADS_B, DV_B), F32)
        cblk = pl.BlockSpec((None, n_prev + 1, SEQ * N_HEADS_B, DV_B), lambda b: (b, 0, 0, 0))
        out_specs += [cblk, cblk]
        out_shape += [cache, cache]
    if cast_w is not None:
        spec, shape = _cast_rider(cast_w, BATCH)
        in_specs.append(spec)
        out_specs.append(spec)
        out_shape.append(shape)
        args.append(cast_w)
    return pl.pallas_call(
        functools.partial(_attn_prompt_kernel, lam_init=lam_init, n_prev=n_prev, rider=cast_w is not None),
        grid=(BATCH,),
        in_specs=in_specs,
        out_specs=out_specs,
        out_shape=out_shape,
        compiler_params=_cparams(("arbitrary",)),
        name="diff_attn_prompt",
    )(*args)


def _rope(x, cos, sin_signed):
    lane = lax.broadcasted_iota(jnp.int32, x.shape, 1)
    first = (lane % (2 * AXIS_PAIRS)) < AXIS_PAIRS
    partner = jnp.where(first, pltpu.roll(x, LANES - AXIS_PAIRS, 1), pltpu.roll(x, AXIS_PAIRS, 1))
    return x * cos + partner * sin_signed


ATT_TQ = 256


def _attn_sample_kernel(lv_ref, q_ref, k_ref, v_ref, kc_ref, vc_ref, cosq_ref, sinq_ref,
                        cosk_ref, sink_ref, gain_ref, o_ref, k1_ref, k2_ref, *, lam_init):
    @pl.when(pl.program_id(1) == 0)
    def _():
        for h in range(N_HEADS_B):
            cols = slice(h * DV_B, (h + 1) * DV_B)
            k1_ref[:, cols], k2_ref[:, cols] = _key_halves(_rope(k_ref[:, cols], cosk_ref[...], sink_ref[...]))

    lam = _lambda(lv_ref[...], lam_init)

    def head(h):
        cols = slice(h * DV_B, (h + 1) * DV_B)
        q = (_rope(q_ref[:, cols], cosq_ref[...], sinq_ref[...]) * QK_SCALE).astype(BF16)
        c1, c2 = _key_halves(kc_ref[:, cols])
        s1 = [_dot_nt(q, k1_ref[:, cols]), _dot_nt(q, c1)]
        s2 = [_dot_nt(q, k2_ref[:, cols]), _dot_nt(q, c2)]
        yield
        p1 = _softmax_parts(s1)
        p2 = _softmax_parts(s2)
        a_own = p1[0] - lam * p2[0]
        a_ctx = p1[1] - lam * p2[1]
        o = (_dot(a_own.astype(BF16), v_ref[:, cols].astype(BF16))
             + _dot(a_ctx.astype(BF16), vc_ref[:, cols].astype(BF16)))
        yield
        o_ref[:, cols] = _diff_finish(o, gain_ref[:, cols], lam_init)

    _interleave(*[head(h) for h in range(N_HEADS_B)])


def _attn_sample_call(lv, qb, kb, vb, kc, vc, cos, sin_signed, gain, *, lam_init):
    tq = ATT_TQ
    nq = DEC_SEQ // tq
    p0 = T_PROMPT // tq
    s0 = T_PROMPT // DEC_SEQ
    qblk = pl.BlockSpec((tq, W_B), lambda b, t: (p0 + b * nq + t, 0))
    kvblk = pl.BlockSpec((DEC_SEQ, W_B), lambda b, t: (s0 + b, 0))
    cblk = pl.BlockSpec((None, PAST_LEN, W_B), lambda b, t: (b, 0, 0))
    return pl.pallas_call(
        functools.partial(_attn_sample_kernel, lam_init=lam_init),
        grid=(DEC_BATCH, nq),
        in_specs=[pl.BlockSpec((4, DH_B), lambda b, t: (0, 0)), qblk, kvblk, kvblk, cblk, cblk,
                  pl.BlockSpec((tq, DV_B), lambda b, t: (t, 0)),
                  pl.BlockSpec((tq, DV_B), lambda b, t: (t, 0)),
                  pl.BlockSpec((DEC_SEQ, DV_B), lambda b, t: (0, 0)),
                  pl.BlockSpec((DEC_SEQ, DV_B), lambda b, t: (0, 0)),
                  pl.BlockSpec((1, W_B), lambda b, t: (0, 0))],
        out_specs=pl.BlockSpec((tq, W_B), lambda b, t: (b * nq + t, 0)),
        out_shape=jax.ShapeDtypeStruct((T_SAMPLE, W_B), BF16),
        scratch_shapes=[pltpu.VMEM((DEC_SEQ, W_B), BF16), pltpu.VMEM((DEC_SEQ, W_B), BF16)],
        compiler_params=_cparams(("arbitrary", "arbitrary")),
        name="diff_attn_sample",
    )(lv, qb, kb, vb, kc, vc, cos, sin_signed, cos, sin_signed, gain)


def _rope_tables():
    rows = DEC_SEQ // GRID_W
    row = jnp.repeat(jnp.arange(rows, dtype=F32), GRID_W)
    col = jnp.tile(jnp.arange(GRID_W, dtype=F32), rows)
    freqs = ROPE_THETA ** (-jnp.arange(AXIS_PAIRS, dtype=F32) / AXIS_PAIRS)
    ar, ac = row[:, None] * freqs, col[:, None] * freqs
    cos64 = jnp.concatenate([jnp.cos(ar), jnp.cos(ar), jnp.cos(ac), jnp.cos(ac)], axis=-1)
    sin64 = jnp.concatenate([-jnp.sin(ar), jnp.sin(ar), -jnp.sin(ac), jnp.sin(ac)], axis=-1)
    return jnp.tile(cos64, (1, 2)), jnp.tile(sin64, (1, 2))


def _group_mean(x, ones64):
    hi, lo = _split_bf16(x)
    return (_dot(hi, ones64) + _dot(lo, ones64)) * (1.0 / DG_C)


def _sgu(uv, ws_ref, bs_ref, fillers=()):
    fillers = list(fillers)
    fill = lambda: fillers.pop(0)() if fillers else None
    r = lax.broadcasted_iota(jnp.int32, (W_C, W_C), 0)
    s = lax.broadcasted_iota(jnp.int32, (W_C, W_C), 1)
    ones64 = jnp.where((r // DG_C) == (s // DG_C), 1.0, 0.0).astype(BF16)
    lane = lax.broadcasted_iota(jnp.int32, (SGU_CHUNK, W_C), 1)
    outs = []
    for n in range(uv.shape[0] // SGU_CHUNK):
        rows = slice(n * SGU_CHUNK, (n + 1) * SGU_CHUNK)
        u = _gelu_tanh(uv[rows, 0:256])
        v = _gelu_tanh(uv[rows, 256:512])
        mu = _group_mean(v, ones64)
        fill()
        d = v - mu
        var = _group_mean(d * d, ones64)
        fill()
        vn = d * lax.rsqrt(var + EPS)
        s_mix = bs_ref[...]
        for g in range(N_GROUPS_C):
            vn_g = jnp.where((lane // DG_C) == g, vn, 0.0).astype(BF16)
            s_mix = s_mix + _dot(ws_ref[g], vn_g)
        fill()
        outs.append((u * s_mix).astype(BF16))
    while fillers:
        fill()
    return jnp.concatenate(outs, axis=0)


OUT_TM = 1024
OUT_SLABS = 4


SEL_LANE0 = N_EXPERTS


def _top2_gates(logits):
    lane = lax.broadcasted_iota(jnp.int32, logits.shape, 1).astype(F32)
    neg = -jnp.inf
    lg = jnp.where(lane < N_EXPERTS, logits, neg)
    m1 = jnp.max(lg, axis=-1, keepdims=True)
    i1 = jnp.min(jnp.where(lg == m1, lane, float(LANES)), axis=-1, keepdims=True)
    lg2 = jnp.where(lane == i1, neg, lg)
    m2 = jnp.max(lg2, axis=-1, keepdims=True)
    i2 = jnp.min(jnp.where(lg2 == m2, lane, float(LANES)), axis=-1, keepdims=True)
    e2 = jnp.exp(m2 - m1)
    den = 1.0 + e2
    gates = jnp.where(lane == i1, 1.0 / den, 0.0) + jnp.where(lane == i2, e2 / den, 0.0)
    sel = jnp.where((lane == i1 + SEL_LANE0) | (lane == i2 + SEL_LANE0), 1.0, 0.0)
    return gates + sel


def _store_token_tiles(ref, val, r0=0):
    n = val.shape[0]
    for k in range(D_MODEL // LANES):
        ref[pl.ds(8 * r0 + k, n, stride=8), :] = val[:, k * LANES:(k + 1) * LANES]


def _load_token_tiles(ref, n):
    return jnp.concatenate([ref[pl.ds(k, n, stride=8), :] for k in range(D_MODEL // LANES)], axis=-1)


def _out_proj_kernel(*refs, n_x, moe):
    oa_refs, ob_refs, (oc_ref,), x_refs = refs[0:2], refs[2:4], refs[4:5], refs[5:5 + n_x]
    w_ref, g1_ref, nrm_ref, sc_ref, sh_ref, *rest = refs[5 + n_x:]
    if moe:
        rw_ref, x1_ref, h2t_ref, gates_ref, gates_t_ref, wo_ref = rest
    else:
        x1_ref, h2_ref, wo_ref = rest

    @pl.when(pl.program_id(0) == 0)
    def _():
        wo_ref[...] = w_ref[...].astype(BF16)

    if moe:
        rw_hi, rw_lo = _split_bf16(rw_ref[...])

    def rows_chain(r0, n):
        rs = slice(r0, r0 + n)
        y = (_dot(_pick_piece(oa_refs, OUT_TM, rs), wo_ref[0:256, :])
             + _dot(_pick_piece(ob_refs, OUT_TM, rs), wo_ref[256:768, :]) + _dot(oc_ref[rs, :], wo_ref[768:1024, :]))
        yield
        x1 = _pick_piece(x_refs, OUT_TM, rs) + g1_ref[...] * y
        x1_ref[rs, :] = x1
        yn = x1 * lax.rsqrt(jnp.mean(x1 * x1, axis=-1, keepdims=True) + EPS)
        h = (yn * nrm_ref[...]) * (1.0 + sc_ref[...]) + sh_ref[...]
        if moe:
            _store_token_tiles(h2t_ref, h, r0)
            h_hi, h_lo = _split_bf16(h)
            logits = _dot(h_hi, rw_hi) + (_dot(h_lo, rw_hi) + _dot(h_hi, rw_lo))
            yield
            gates = _top2_gates(logits)
            gates_ref[rs, :] = gates
            gates_t_ref[:, rs] = gates.T
        else:
            h2_ref[rs, :] = h.astype(BF16)

    n_slab = OUT_TM // OUT_SLABS
    _interleave(*[rows_chain(i * n_slab, n_slab) for i in range(OUT_SLABS)])


def _out_proj_call(oa_pieces, ob_pieces, oc, x_pieces, w_out, l, mod4, norm_g, router_w=None):
    tm = OUT_TM
    n = T_ALL // tm
    moe = router_w is not None
    row = functools.partial(_mod_row, tm=tm)
    mod_spec = lambda k: pl.BlockSpec((None, None, 1, D_MODEL), lambda i: (row(i), k, 0, 0))
    tok = lambda w: pl.BlockSpec((tm, w), lambda i: (i, 0))
    full = lambda a: pl.BlockSpec(a.shape, lambda i: (0,) * a.ndim)
    in_specs = (_piece_specs(oa_pieces, tm, W_A) + _piece_specs(ob_pieces, tm, W_B) + [tok(W_C)]
                + _piece_specs(x_pieces, tm, D_MODEL)
                + [_resident_layer(w_out, l), mod_spec(2), full(norm_g), mod_spec(4), mod_spec(3)])
    args = [*oa_pieces, *ob_pieces, oc, *x_pieces, w_out, mod4, norm_g, mod4, mod4]
    if moe:
        in_specs.append(full(router_w))
        args.append(router_w)
        out_specs = [tok(D_MODEL), pl.BlockSpec((tm * 8, LANES), lambda i: (i, 0)), tok(LANES),
                     pl.BlockSpec((LANES, tm), lambda i: (0, i))]
        out_shape = [jax.ShapeDtypeStruct((T_ALL, D_MODEL), F32), jax.ShapeDtypeStruct((T_ALL * 8, LANES), F32),
                     jax.ShapeDtypeStruct((T_ALL, LANES), F32), jax.ShapeDtypeStruct((LANES, T_ALL), F32)]
    else:
        out_specs = [tok(D_MODEL), tok(D_MODEL)]
        out_shape = [jax.ShapeDtypeStruct((T_ALL, D_MODEL), F32), jax.ShapeDtypeStruct((T_ALL, D_MODEL), BF16)]
    return pl.pallas_call(
        functools.partial(_out_proj_kernel, n_x=len(x_pieces), moe=moe),
        grid=(n,),
        in_specs=in_specs,
        out_specs=out_specs,
        out_shape=out_shape,
        scratch_shapes=[pltpu.VMEM((D_MODEL, D_MODEL), BF16)],
        compiler_params=_cparams(("arbitrary",)),
        name="out_proj_moe" if moe else "out_proj",
    )(*args)


FFN_TM = 512
MXU_N = 256
FFN_SPLITS = (0, 1024, 2048, D_FF)
assert all(s % MXU_N == 0 for s in FFN_SPLITS)


def _swiglu(h, wg_ref, wu_ref, wd_ref):
    out = None
    for c0, c1 in zip(FFN_SPLITS[:-1], FFN_SPLITS[1:]):
        act = _silu(_dot(h, wg_ref[:, c0:c1])) * _dot(h, wu_ref[:, c0:c1])
        d = _dot(act.astype(BF16), wd_ref[c0:c1, :])
        out = d if out is None else out + d
    return out


FFN_RIDER_BLOCKS = 16


def _ffn_kernel(h_ref, x_ref, g2_ref, wg_ref, wu_ref, wd_ref, w_ref, o_ref, wb_ref):
    @pl.when(pl.program_id(0) < FFN_RIDER_BLOCKS)
    def _():
        wb_ref[...] = w_ref[...].astype(BF16)

    o_ref[...] = x_ref[...] + g2_ref[...] * _swiglu(h_ref[...], wg_ref, wu_ref, wd_ref)


def _ffn_call(h2, x1, mod4, wg, wu, wd, cast_w):
    tm = FFN_TM
    row = functools.partial(_mod_row, tm=tm)
    tok = lambda w: pl.BlockSpec((tm, w), lambda i: (i, 0))
    resident = lambda a: pl.BlockSpec(a.shape, lambda i: (0, 0), pipeline_mode=pl.Buffered(1))
    rider_spec, rider_shape = _cast_rider(cast_w, FFN_RIDER_BLOCKS)
    return pl.pallas_call(
        _ffn_kernel,
        grid=(T_ALL // tm,),
        in_specs=[tok(D_MODEL), tok(D_MODEL),
                  pl.BlockSpec((None, None, 1, D_MODEL), lambda i: (row(i), 5, 0, 0)),
                  resident(wg), resident(wu), resident(wd), rider_spec],
        out_specs=[tok(D_MODEL), rider_spec],
        out_shape=[jax.ShapeDtypeStruct((T_ALL, D_MODEL), F32), rider_shape],
        compiler_params=_cparams(("arbitrary",)),
        name="ffn_dense",
    )(h2, x1, mod4, wg, wu, wd, cast_w)


MOE_TM = 512
MOE_NT_MAX = (2 * T_ALL) // MOE_TM + N_EXPERTS
MOE_ROWS = MOE_NT_MAX * MOE_TM
PLAN_BLK = 512
MISC_LAST_START = 8
MISC_NT = 16


def _moe_plan_kernel(gt_ref, posa_ref, posb_ref, te_ref, ti_ref, misc_ref):
    tm = float(MOE_TM)
    sel = gt_ref[SEL_LANE0:SEL_LANE0 + N_EXPERTS, :]
    cnt = jnp.sum(sel, axis=1, keepdims=True)
    nt = jnp.floor((cnt + (tm - 1.0)) * (1.0 / tm))
    sub = lax.broadcasted_iota(jnp.int32, (N_EXPERTS, LANES), 0).astype(F32)
    lane = lax.broadcasted_iota(jnp.int32, (N_EXPERTS, LANES), 1).astype(F32)
    nt_b = jnp.broadcast_to(nt, (N_EXPERTS, LANES))
    nt_row = jnp.sum(jnp.where(sub == lane, nt_b, 0.0), axis=0, keepdims=True)
    toff = jnp.sum(jnp.where(lane < sub, jnp.broadcast_to(nt_row, (N_EXPERTS, LANES)), 0.0),
                   axis=1, keepdims=True)
    tend = toff + nt
    n_total = jnp.sum(nt, axis=0, keepdims=True)
    jc = jnp.minimum(lane, n_total - 1.0)
    te = jnp.sum(jnp.where(jc >= tend, 1.0, 0.0), axis=0, keepdims=True)
    te_ref[...] = te.astype(jnp.int32)
    ti_ref[...] = jc[0:1, :].astype(jnp.int32)
    last_start = (tend - 1.0) * tm
    ls_row = jnp.sum(jnp.where(sub + MISC_LAST_START == lane, jnp.broadcast_to(last_start, (N_EXPERTS, LANES)), 0.0),
                     axis=0, keepdims=True)
    nt_row2 = jnp.sum(jnp.where(sub + MISC_NT == lane, nt_b, 0.0), axis=0, keepdims=True)
    misc = jnp.where(lane[0:1, :] == 0.0, n_total, 0.0) + ls_row + nt_row2
    misc_ref[...] = misc.astype(jnp.int32)

    off = toff * tm
    r = lax.broadcasted_iota(jnp.int32, (PLAN_BLK, PLAN_BLK), 0)
    c = lax.broadcasted_iota(jnp.int32, (PLAN_BLK, PLAN_BLK), 1)
    upper = jnp.where(r <= c, 1.0, 0.0).astype(BF16)
    carry = jnp.zeros((N_EXPERTS, 1), F32)
    for blk in range(T_ALL // PLAN_BLK):
        cols = slice(blk * PLAN_BLK, (blk + 1) * PLAN_BLK)
        s = gt_ref[SEL_LANE0:SEL_LANE0 + N_EXPERTS, cols]
        rank = _dot(s.astype(BF16), upper) + carry
        pos = off + rank - 1.0
        posa_ref[:, cols] = jnp.min(jnp.where(s > 0.0, pos, 1e9), axis=0, keepdims=True).astype(jnp.int32)
        posb_ref[:, cols] = jnp.max(jnp.where(s > 0.0, pos, -1.0), axis=0, keepdims=True).astype(jnp.int32)
        carry = carry + jnp.sum(s, axis=1, keepdims=True)


def _moe_plan_call(gates_t):
    row = lambda w: jax.ShapeDtypeStruct((1, w), jnp.int32)
    full = lambda w: pl.BlockSpec((1, w), lambda: (0, 0))
    return pl.pallas_call(
        _moe_plan_kernel,
        in_specs=[pl.BlockSpec((LANES, T_ALL), lambda: (0, 0))],
        out_specs=[full(T_ALL), full(T_ALL), full(LANES), full(LANES), full(LANES)],
        out_shape=[row(T_ALL), row(T_ALL), row(LANES), row(LANES), row(LANES)],
        compiler_params=pltpu.CompilerParams(vmem_limit_bytes=VMEM_LIMIT),
        name="moe_plan",
    )(gates_t)


DMA_UNROLL = 8


def _row_tile(ref, row):
    return ref.at[pl.ds(pl.multiple_of(row * 8, 8), 8), :]


def _moe_scatter_kernel(misc_ref, posa_ref, posb_ref, h_ref, xs_ref, zero_ref, sem):
    tm = h_ref.shape[0] // 8

    @pl.when(pl.program_id(0) == 0)
    def _():
        zero_ref[...] = jnp.zeros_like(zero_ref)

        def zero_tile(first_row):
            start = pl.multiple_of(first_row * 8, 8)
            cp = pltpu.make_async_copy(zero_ref, xs_ref.at[pl.ds(start, MOE_TM * 8), :], sem.at[0])
            cp.start()
            cp.wait()

        for e in range(N_EXPERTS):
            @pl.when(misc_ref[0, MISC_NT + e] > 0)
            def _():
                zero_tile(misc_ref[0, MISC_LAST_START + e])

        def zero_tail(j, carry):
            zero_tile(j * MOE_TM)
            return carry

        lax.fori_loop(misc_ref[0, 0], MOE_NT_MAX, zero_tail, 0)

    def issue(r, carry):
        src = _row_tile(h_ref, r)
        pltpu.make_async_copy(src, _row_tile(xs_ref, posa_ref[0, r]), sem.at[0]).start(priority=0)
        pltpu.make_async_copy(src, _row_tile(xs_ref, posb_ref[0, r]), sem.at[1]).start(priority=1)
        return carry

    lax.fori_loop(0, tm, issue, 0, unroll=DMA_UNROLL)
    for k in range(2):
        pltpu.make_async_copy(h_ref, xs_ref.at[pl.ds(0, tm * 8), :], sem.at[k]).wait()


SCATTER_TM = 512


def _moe_scatter_call(misc, posa3, posb3, h2t):
    tm = SCATTER_TM
    smem_row = pl.BlockSpec((None, 1, tm), lambda i: (i, 0, 0), memory_space=pltpu.SMEM)
    return pl.pallas_call(
        _moe_scatter_kernel,
        grid=(T_ALL // tm,),
        in_specs=[pl.BlockSpec((1, LANES), lambda i: (0, 0), memory_space=pltpu.SMEM), smem_row, smem_row,
                  pl.BlockSpec((tm * 8, LANES), lambda i: (i, 0))],
        out_specs=pl.BlockSpec(memory_space=pl.ANY),
        out_shape=jax.ShapeDtypeStruct((MOE_ROWS * 8, LANES), F32),
        scratch_shapes=[pltpu.VMEM((MOE_TM * 8, LANES), F32), pltpu.SemaphoreType.DMA((2,))],
        compiler_params=_cparams(("arbitrary",)),
        name="moe_scatter",
    )(misc, posa3, posb3, h2t)


def _ffn_grouped_kernel(te_ref, ti_ref, misc_ref, x_ref, wg_ref, wu_ref, wd_ref, o_ref):
    j = pl.program_id(0)

    @pl.when(j < misc_ref[0])
    def _():
        h = _load_token_tiles(x_ref, MOE_TM).astype(BF16)
        _store_token_tiles(o_ref, _swiglu(h, wg_ref, wu_ref, wd_ref))

    @pl.when(j >= misc_ref[0])
    def _():
        o_ref[...] = jnp.zeros_like(o_ref)


def _ffn_grouped_call(te, ti, misc, xs, wg, wu, wd):
    expert = lambda *s: pl.BlockSpec((None,) + s, lambda j, te, ti, misc: (te[j], 0, 0))
    grid_spec = pltpu.PrefetchScalarGridSpec(
        num_scalar_prefetch=3,
        grid=(MOE_NT_MAX,),
        in_specs=[pl.BlockSpec((MOE_TM * 8, LANES), lambda j, te, ti, misc: (ti[j], 0)),
                  expert(D_MODEL, D_FF), expert(D_MODEL, D_FF), expert(D_FF, D_MODEL)],
        out_specs=pl.BlockSpec((MOE_TM * 8, LANES), lambda j, te, ti, misc: (j, 0)),
    )
    return pl.pallas_call(
        _ffn_grouped_kernel,
        grid_spec=grid_spec,
        out_shape=jax.ShapeDtypeStruct((MOE_ROWS * 8, LANES), F32),
        compiler_params=_cparams(("arbitrary",)),
        name="ffn_grouped",
    )(te, ti, misc, xs, wg, wu, wd)


COMBINE_TM = 256


def _moe_combine_kernel(posa_ref, posb_ref, posa_next_ref, posb_next_ref, ys_ref, x_ref, g2_ref, gates_ref, nf_ref,
                        op_ref, os_ref, bufa_ref, bufb_ref, sem):
    tm = COMBINE_TM
    i = pl.program_id(0)
    n = pl.num_programs(0)
    slot = i % 2

    def gather(pa_ref, pb_ref, s):
        def issue(r, carry):
            pltpu.make_async_copy(_row_tile(ys_ref, pa_ref[0, r]), _row_tile(bufa_ref.at[s], r),
                                  sem.at[s, 0]).start(priority=0)
            pltpu.make_async_copy(_row_tile(ys_ref, pb_ref[0, r]), _row_tile(bufb_ref.at[s], r),
                                  sem.at[s, 1]).start(priority=1)
            return carry

        lax.fori_loop(0, tm, issue, 0, unroll=DMA_UNROLL)

    @pl.when(i == 0)
    def _():
        gather(posa_ref, posb_ref, 0)

    @pl.when(i + 1 < n)
    def _():
        gather(posa_next_ref, posb_next_ref, 1 - slot)

    gates = gates_ref[...]
    lane = lax.broadcasted_iota(jnp.int32, gates.shape, 1).astype(F32)
    is_sel = (lane >= SEL_LANE0) & (lane < SEL_LANE0 + N_EXPERTS) & (gates > 0.0)
    ia = jnp.min(jnp.where(is_sel, lane, float(LANES)), axis=-1, keepdims=True) - SEL_LANE0
    ib = jnp.max(jnp.where(is_sel, lane, -1.0), axis=-1, keepdims=True) - SEL_LANE0
    wa = jnp.sum(jnp.where(lane == ia, gates, 0.0), axis=-1, keepdims=True)
    wb = jnp.sum(jnp.where(lane == ib, gates, 0.0), axis=-1, keepdims=True)

    pltpu.make_async_copy(ys_ref.at[pl.ds(0, tm * 8), :], bufa_ref.at[slot], sem.at[slot, 0]).wait()
    pltpu.make_async_copy(ys_ref.at[pl.ds(0, tm * 8), :], bufb_ref.at[slot], sem.at[slot, 1]).wait()

    y = wa * _load_token_tiles(bufa_ref.at[slot], tm) + wb * _load_token_tiles(bufb_ref.at[slot], tm)
    out = x_ref[...] + g2_ref[...] * y
    out = (out * lax.rsqrt(jnp.mean(out * out, axis=-1, keepdims=True) + EPS)) * nf_ref[...]

    @pl.when(i < T_PROMPT // tm)
    def _():
        op_ref[...] = out

    @pl.when(i >= T_PROMPT // tm)
    def _():
        os_ref[...] = out


def _moe_combine_call(posa3, posb3, ys, x1, mod4, gates, norm_f):
    tm = COMBINE_TM
    n = T_ALL // tm
    n_p = T_PROMPT // tm
    row = functools.partial(_mod_row, tm=tm)
    smem_row = pl.BlockSpec((None, 1, tm), lambda i: (i, 0, 0), memory_space=pltpu.SMEM)
    smem_next = pl.BlockSpec((None, 1, tm), lambda i: (jnp.minimum(i + 1, n - 1), 0, 0), memory_space=pltpu.SMEM)
    tok = lambda w: pl.BlockSpec((tm, w), lambda i: (i, 0))
    return pl.pallas_call(
        _moe_combine_kernel,
        grid=(n,),
        in_specs=[smem_row, smem_row, smem_next, smem_next, pl.BlockSpec(memory_space=pl.ANY), tok(D_MODEL),
                  pl.BlockSpec((None, None, 1, D_MODEL), lambda i: (row(i), 5, 0, 0)), tok(LANES),
                  pl.BlockSpec((1, D_MODEL), lambda i: (0, 0))],
        out_specs=[pl.BlockSpec((tm, D_MODEL), lambda i: (jnp.minimum(i, n_p - 1), 0)),
                   pl.BlockSpec((tm, D_MODEL), lambda i: (jnp.maximum(i - n_p, 0), 0))],
        out_shape=[jax.ShapeDtypeStruct((T_PROMPT, D_MODEL), F32), jax.ShapeDtypeStruct((T_SAMPLE, D_MODEL), F32)],
        scratch_shapes=[pltpu.VMEM((2, tm * 8, LANES), F32), pltpu.VMEM((2, tm * 8, LANES), F32),
                        pltpu.SemaphoreType.DMA((2, 2))],
        compiler_params=_cparams(("arbitrary",)),
        name="moe_combine",
    )(posa3, posb3, posa3, posb3, ys, x1, mod4, gates, norm_f)


def _moe_call(h2t, x1, mod4, gates, gates_t, wg, wu, wd, norm_f):
    posa, posb, te, ti, misc = _moe_plan_call(gates_t)
    xs = _moe_scatter_call(misc, posa.reshape(T_ALL // SCATTER_TM, 1, SCATTER_TM),
                           posb.reshape(T_ALL // SCATTER_TM, 1, SCATTER_TM), h2t)
    ys = _ffn_grouped_call(te.reshape(LANES), ti.reshape(LANES), misc.reshape(LANES), xs, wg, wu, wd)
    return _moe_combine_call(posa.reshape(T_ALL // COMBINE_TM, 1, COMBINE_TM),
                             posb.reshape(T_ALL // COMBINE_TM, 1, COMBINE_TM), ys, x1, mod4, gates, norm_f)


def kernel(x_prompt, x_sample, cache_k, cache_v, state_gla_fwd, state_gla_bwd, c, c_ctx, w_ada, b_ada, norm_mix, norm_ffn, w_in, w_out, gla_w_up, gla_b_up, gla_norm, diff_lambda, diff_norm, sgu_w, sgu_b, ffn_w_gate, ffn_w_up, ffn_w_down, router_w, moe_w_gate, moe_w_up, moe_w_down, norm_f):
    assert DEPTH == 2
    x_pieces = [x_prompt.reshape(T_PROMPT, D_MODEL), x_sample.reshape(T_SAMPLE, D_MODEL)]
    cvecs = jnp.concatenate([c_ctx[None, :], c, jnp.zeros((N_MOD_ROWS - 1 - DEC_BATCH, D_MODEL), F32)], axis=0)
    mod = _ada_call(cvecs, w_ada, b_ada)
    cos, sin_signed = _rope_tables()
    zeros_state = jnp.zeros((BATCH, 64, 256), F32)

    w_in_t = jnp.swapaxes(w_in, 1, 2)
    moe_w = [moe_w_gate[0], moe_w_up[0], moe_w_down[0]]
    prev_kv, sfs, sbs = [], [], []
    for l in range(DEPTH):
        mod4 = mod[l].reshape(N_MOD_ROWS, 6, 1, D_MODEL)
        w_up = jnp.zeros((LANES, 2 * W_A), F32)
        w_up = w_up.at[0:GLA_RANK, 0:W_A].set(gla_w_up[l, 0]).at[GLA_RANK:2 * GLA_RANK, W_A:].set(gla_w_up[l, 1])
        b_up = gla_b_up[l].reshape(1, 2 * W_A)
        bs_full = jnp.repeat(sgu_b[l].T, DG_C, axis=1)
        g4, la, qb, kb, vb, oc = _in_proj_call(x_pieces, norm_mix[l][None, :], mod4, w_in_t, l, w_up.astype(BF16), b_up,
                                               sgu_w[l].astype(BF16), bs_full)

        gain_a = gla_norm[l][None, :]
        oa_p, sf, sb, moe_w[l] = _gla_call(g4, la, zeros_state, zeros_state, gain_a, moe_w[l],
                                           batch=BATCH, seq=SEQ, row_block0=0)
        oa_s, _, _ = _gla_call(g4, la, _state_to_kernel(state_gla_fwd[:, l]),
                               _state_to_kernel(state_gla_bwd[:, l]), gain_a,
                               batch=DEC_BATCH, seq=DEC_SEQ, row_block0=T_PROMPT // DEC_SEQ)

        lam_init = 0.8 - 0.6 * math.exp(-0.3 * l)
        gain_b = diff_norm[l][None, :]
        if l < DEPTH - 1:
            (ob_p,) = _attn_prompt_call(diff_lambda[l], qb, kb, vb, gain_b, lam_init=lam_init)
            prev_kv.append((kb, vb))
        else:
            ob_p, new_cache_k, new_cache_v = _attn_prompt_call(diff_lambda[l], qb, kb, vb, gain_b, prev_kv,
                                                               lam_init=lam_init, write_cache=True)
        ob_s = _attn_sample_call(diff_lambda[l], qb, kb, vb,
                                 cache_k[:, l].reshape(DEC_BATCH, PAST_LEN, W_B),
                                 cache_v[:, l].reshape(DEC_BATCH, PAST_LEN, W_B),
                                 cos, sin_signed, gain_b, lam_init=lam_init)

        if l == 0:
            x1, h2 = _out_proj_call([oa_p, oa_s], [ob_p, ob_s], oc, x_pieces, w_out, l, mod4, norm_ffn[l][None, :])
            x_next, moe_w[2] = _ffn_call(h2, x1, mod4, ffn_w_gate[0].astype(BF16), ffn_w_up[0].astype(BF16),
                                         ffn_w_down[0].astype(BF16), moe_w[2])
            x_pieces = [x_next]
        else:
            rw = jnp.pad(router_w[0], ((0, 0), (0, LANES - N_EXPERTS)))
            x1, h2t, gates, gates_t = _out_proj_call([oa_p, oa_s], [ob_p, ob_s], oc, x_pieces, w_out, l, mod4,
                                                     norm_ffn[l][None, :], rw)
            y_prompt, y_sample = _moe_call(h2t, x1, mod4, gates, gates_t, *moe_w, norm_f[None, :])

        sfs.append(_state_from_kernel(sf))
        sbs.append(_state_from_kernel(sb))

    return (y_prompt.reshape(BATCH, SEQ, D_MODEL), y_sample.reshape(DEC_BATCH, DEC_SEQ, D_MODEL),
            new_cache_k.reshape(BATCH, DEPTH, SEQ, N_HEADS_B, DV_B), new_cache_v.reshape(BATCH, DEPTH, SEQ, N_HEADS_B, DV_B),
            jnp.stack(sfs, axis=1), jnp.stack(sbs, axis=1))
```

```python
import functools
import math

import jax
import jax.numpy as jnp
import numpy as np
from jax import lax
from jax.experimental import pallas as pl
from jax.experimental.pallas import tpu as pltpu

F32 = jnp.float32
BF16 = jnp.bfloat16

D_MODEL = 1024
BATCH = 32
SEQ = 256
DEPTH = 2
DEC_BATCH = 2
DEC_SEQ = 1024
PAST_LEN = 256
GRID_W = 64
N_HEADS_A = 4
DK_A = 64
W_A = 256
GLA_RANK = 16
GLA_TAU = 16.0
GLA_CHUNK = 64
N_HEADS_B = 4
DH_B = 64
DV_B = 128
W_B = 512
ROPE_THETA = 10000.0
AXIS_PAIRS = DH_B // 4
N_GROUPS_C = 4
DG_C = 64
W_C = 256
SGU_CHUNK = 128
D_FF = 2816
N_EXPERTS = 8
EPS = 1e-6

T_PROMPT = BATCH * SEQ
T_SAMPLE = DEC_BATCH * DEC_SEQ
T_ALL = T_PROMPT + T_SAMPLE
N_MOD_ROWS = 8
LANES = 128
VMEM_LIMIT = 56 * 1024 * 1024


def _cparams(sem):
    return pltpu.CompilerParams(dimension_semantics=sem, vmem_limit_bytes=VMEM_LIMIT)


def _dot(a, b):
    return jnp.dot(a, b, preferred_element_type=F32)


def _dot_nt(a, b):
    return lax.dot_general(a, b, (((1,), (1,)), ((), ())), preferred_element_type=F32)


def _dot_tn(a, b):
    return lax.dot_general(a, b, (((0,), (0,)), ((), ())), preferred_element_type=F32)


def _split_bf16(x):
    hi = x.astype(BF16)
    lo = (x - hi.astype(F32)).astype(BF16)
    return hi, lo


def _dot3(a, w):
    a_hi, a_lo = _split_bf16(a)
    w_hi, w_lo = _split_bf16(w)
    return _dot(a_hi, w_hi) + (_dot(a_lo, w_hi) + _dot(a_hi, w_lo))


def _sigmoid(x):
    return 1.0 / (1.0 + jnp.exp(-x))


def _silu(x):
    return x * _sigmoid(x)


def _gelu_tanh(x):
    c = math.sqrt(2.0 / math.pi)
    return x * (0.5 * (1.0 + jnp.tanh(c * (x + 0.044715 * (x * x * x)))))


def _log_sigmoid(x):
    return jnp.minimum(x, 0.0) - jnp.log(1.0 + jnp.exp(-jnp.abs(x)))


def _mod_row(i, tm):
    n_p = T_PROMPT // tm
    per_b = DEC_SEQ // tm
    return jnp.where(i < n_p, 0, 1 + (i - n_p) // per_b)


ADA_TN = 1536


def _ada_kernel(c_ref, w_ref, b_ref, o_ref):
    a = _silu(c_ref[...])
    o_ref[...] = _dot3(a, w_ref[...]) + b_ref[...]


def _ada_call(cvecs, w_ada, b_ada):
    n_col = (6 * D_MODEL) // ADA_TN
    return pl.pallas_call(
        _ada_kernel,
        grid=(DEPTH, n_col),
        in_specs=[
            pl.BlockSpec((N_MOD_ROWS, D_MODEL), lambda l, j: (0, 0)),
            pl.BlockSpec((None, D_MODEL, ADA_TN), lambda l, j: (l, 0, j)),
            pl.BlockSpec((None, 1, ADA_TN), lambda l, j: (l, 0, j)),
        ],
        out_specs=pl.BlockSpec((None, N_MOD_ROWS, ADA_TN), lambda l, j: (l, 0, j)),
        out_shape=jax.ShapeDtypeStruct((DEPTH, N_MOD_ROWS, 6 * D_MODEL), F32),
        compiler_params=_cparams(("arbitrary", "arbitrary")),
        name="ada_mod",
    )(cvecs, w_ada, b_ada.reshape(DEPTH, 1, 6 * D_MODEL))


IN_TM = 512
IN_COLS = 3104
Z_COL0 = 1024
Z_COLS = 2 * GLA_RANK
W_MAIN = 3072


def _piece_specs(pieces, tm, width):
    specs, t0 = [], 0
    for arr in pieces:
        nt = arr.shape[0] // tm
        specs.append(pl.BlockSpec((tm, width), lambda i, t0=t0, nt=nt: (jnp.clip(i - t0, 0, nt - 1), 0)))
        t0 += nt
    assert t0 * tm == T_ALL and len(pieces) in (1, 2) and (len(pieces) == 1 or pieces[0].shape[0] == T_PROMPT)
    return specs


def _pick_piece(refs, tm, rows=slice(None)):
    if len(refs) == 1:
        return refs[0][rows, :]
    return jnp.where(pl.program_id(0) < T_PROMPT // tm, refs[0][rows, :], refs[1][rows, :])


def _in_proj_kernel(*refs, n_x):
    x_refs = refs[:n_x]
    (nrm_ref, sh_ref, sc_ref, w_ref, wup_ref, bup_ref, ws_ref, bs_ref,
     g4_ref, la_ref, qb_ref, kb_ref, vb_ref, oc_ref, wm_ref, wz_ref) = refs[n_x:]

    @pl.when(pl.program_id(0) == 0)
    def _():
        wm_ref[0:Z_COL0, :] = w_ref[0:Z_COL0, :].astype(BF16)
        wm_ref[Z_COL0:W_MAIN, :] = w_ref[Z_COL0 + Z_COLS:IN_COLS, :].astype(BF16)
        wz_ref[...] = jnp.zeros_like(wz_ref)
        wz_ref[0:Z_COLS, :] = w_ref[Z_COL0:Z_COL0 + Z_COLS, :].astype(BF16)

    x = _pick_piece(x_refs, IN_TM)
    y = x * lax.rsqrt(jnp.mean(x * x, axis=-1, keepdims=True) + EPS)
    h = (y * nrm_ref[...]) * (1.0 + sc_ref[...]) + sh_ref[...]
    hb = h.astype(BF16)
    def plain_job(ref, c0, w0):
        def job():
            ref[:, c0:c0 + MXU_N] = _dot_nt(hb, wm_ref[w0:w0 + MXU_N, :])
        return job

    jobs = [plain_job(ref, c, w0 + c) for ref, w0, width in
            ((g4_ref, 0, 1024), (qb_ref, 1024, 512), (kb_ref, 1536, 512), (vb_ref, 2048, 512))
            for c in range(0, width, MXU_N)]
    z = _dot_nt(hb, wz_ref[...])
    jobs.pop(0)()
    zz = _dot(z.astype(BF16), wup_ref[...]) + bup_ref[...]
    uv = _dot_nt(hb, wm_ref[2560:3072, :])
    jobs.pop(0)()
    la_ref[...] = _log_sigmoid(zz) * (1.0 / GLA_TAU)
    jobs.pop(0)()
    oc_ref[...] = _sgu(uv, ws_ref, bs_ref, jobs)


def _resident_layer(a, l):
    return pl.BlockSpec((None,) + a.shape[1:], lambda *_: (l,) + (0,) * (a.ndim - 1), pipeline_mode=pl.Buffered(1))


def _in_proj_call(x_pieces, norm_g, mod4, w_in_t, l, w_up, b_up, sgu_ws, sgu_bs):
    tm = IN_TM
    n = T_ALL // tm
    row = functools.partial(_mod_row, tm=tm)
    mod_spec = lambda k: pl.BlockSpec((None, None, 1, D_MODEL), lambda i: (row(i), k, 0, 0))
    full = lambda a: pl.BlockSpec(a.shape, lambda i: (0,) * a.ndim)
    out = lambda w: pl.BlockSpec((tm, w), lambda i: (i, 0))
    return pl.pallas_call(
        functools.partial(_in_proj_kernel, n_x=len(x_pieces)),
        grid=(n,),
        in_specs=_piece_specs(x_pieces, tm, D_MODEL) + [full(norm_g), mod_spec(0), mod_spec(1), _resident_layer(w_in_t, l),
                                                        full(w_up), full(b_up), full(sgu_ws), full(sgu_bs)],
        out_specs=[out(1024), out(512), out(512), out(512), out(512), out(W_C)],
        out_shape=[jax.ShapeDtypeStruct((T_ALL, w), F32) for w in (1024, 512, 512, 512, 512)]
        + [jax.ShapeDtypeStruct((T_ALL, W_C), BF16)],
        scratch_shapes=[pltpu.VMEM((W_MAIN, D_MODEL), BF16), pltpu.VMEM((LANES, D_MODEL), BF16)],
        compiler_params=_cparams(("arbitrary",)),
        name="in_proj",
    )(*x_pieces, norm_g, mod4, mod4, w_in_t, w_up, b_up, sgu_ws, sgu_bs)


GLA_SB = 256
GLA_NC = GLA_SB // GLA_CHUNK


def _head_blocks(x, same64):
    return jnp.where(same64, jnp.concatenate([x] * N_HEADS_A, axis=0), 0.0).astype(BF16)


def _gla_superblock(q, k, vb, v_blocks, la, st_all, tri, mask4, same64, forward):
    c = GLA_CHUNK
    mid, last = (c // 2 - 1, c - 1) if forward else (c // 2, 0)
    la_hi, la_lo = _split_bf16(la)
    b = _dot(tri, la_hi) + _dot(tri, la_lo)
    yield
    rows_of = lambda r: jnp.concatenate(
        [jnp.broadcast_to(b[i * c + r:i * c + r + 1, :], (c, W_A)) for i in range(GLA_NC)], axis=0)
    m = rows_of(mid)
    bl = rows_of(last)
    qe = (q * jnp.exp(b - m)).astype(BF16)
    ke = k * jnp.exp(m - b)
    qi = (q * jnp.exp(b)).astype(BF16)
    ks = (k * jnp.exp(bl - b)).astype(BF16)
    outs = [None] * GLA_NC
    for i in (range(GLA_NC) if forward else reversed(range(GLA_NC))):
        rows = slice(i * c, (i + 1) * c)
        s = _dot_nt(qe[rows, :], _head_blocks(ke[rows, :], same64))
        kv = _dot_tn(vb[rows, :], ks[rows, :])
        yield
        a = jnp.where(mask4, s, 0.0).astype(BF16)
        outs[i] = _dot(a, v_blocks[i]) + _dot_nt(qi[rows, :], st_all.astype(BF16))
        st_all = st_all * jnp.exp(bl[i * c:i * c + 1, :]) + jnp.where(same64, kv, 0.0)
        yield
    return jnp.concatenate(outs, axis=0), st_all


def _interleave(*gens):
    results = [None] * len(gens)
    active = list(enumerate(gens))
    while active:
        for item in list(active):
            try:
                next(item[1])
            except StopIteration as stop:
                results[item[0]] = stop.value
                active.remove(item)
    return results


def _cast_rider(w, n_blocks):
    e, r, c = w.shape
    per = n_blocks // e
    rows = r // per
    assert per * e == n_blocks and rows * per == r and rows % 16 == 0

    def index(b):
        b = jnp.minimum(b, n_blocks - 1)
        return b // per, b % per, 0

    spec = pl.BlockSpec((None, rows, c), index)
    return spec, jax.ShapeDtypeStruct(w.shape, BF16)


def _gla_kernel(*refs, seq, rider):
    if rider:
        (g4_ref, la_ref, s0f_ref, s0b_ref, gain_ref, w_ref, o_ref, sf_ref, sb_ref, wb_ref, of_ref, ob_ref) = refs
        wb_ref[...] = w_ref[...].astype(BF16)
    else:
        g4_ref, la_ref, s0f_ref, s0b_ref, gain_ref, o_ref, sf_ref, sb_ref, of_ref, ob_ref = refs
    n = GLA_SB
    nsb = seq // n
    r = lax.broadcasted_iota(jnp.int32, (n, n), 0)
    s = lax.broadcasted_iota(jnp.int32, (n, n), 1)
    same64 = (r // GLA_CHUNK) == (s // GLA_CHUNK)
    lower = same64 & (s <= r)
    upper = same64 & (s >= r)
    tri_f = jnp.where(lower, 1.0, 0.0).astype(BF16)
    tri_b = jnp.where(upper, 1.0, 0.0).astype(BF16)
    ones64 = jnp.where(same64, 1.0, 0.0).astype(BF16)
    key_row = lax.broadcasted_iota(jnp.int32, (GLA_CHUNK, n), 1) % GLA_CHUNK
    qry_row = lax.broadcasted_iota(jnp.int32, (GLA_CHUNK, n), 0)
    mask4_f = key_row <= qry_row
    mask4_b = key_row >= qry_row
    scale = DK_A ** -0.5
    expand = lambda st: jnp.where(same64, jnp.concatenate([st] * N_HEADS_A, axis=0), 0.0)
    compact = lambda st_all: functools.reduce(
        lambda a, b: a + b, [st_all[h * 64:(h + 1) * 64, :] for h in range(N_HEADS_A)])

    def step(i, carry):
        stf, stb = carry
        rf = pl.ds(pl.multiple_of(i * n, n), n)
        rb = pl.ds(pl.multiple_of((nsb - 1 - i) * n, n), n)
        def direction(rows, la_cols, st, tri, mask4, forward):
            q = g4_ref[rows, 0:256] * scale
            k = g4_ref[rows, 256:512]
            v = g4_ref[rows, 512:768]
            v_blocks = [_head_blocks(v[i * GLA_CHUNK:(i + 1) * GLA_CHUNK, :], same64) for i in range(GLA_NC)]
            return _gla_superblock(q, k, v.astype(BF16), v_blocks, la_ref[rows, la_cols], st, tri, mask4, same64,
                                   forward)

        (o_f, stf), (o_b, stb) = _interleave(direction(rf, slice(0, 256), stf, tri_f, mask4_f, True),
                                             direction(rb, slice(256, 512), stb, tri_b, mask4_b, False))
        of_ref[rf, :] = o_f
        ob_ref[rb, :] = o_b
        return stf, stb

    stf, stb = lax.fori_loop(0, nsb, step, (expand(s0f_ref[...]), expand(s0b_ref[...])))
    sf_ref[...] = compact(stf)
    sb_ref[...] = compact(stb)

    gain = gain_ref[...]

    def finish(i, carry):
        rows = pl.ds(pl.multiple_of(i * n, n), n)
        o = of_ref[rows, :] + ob_ref[rows, :]
        sq_hi, sq_lo = _split_bf16(o * o)
        ms = (_dot(sq_hi, ones64) + _dot(sq_lo, ones64)) * (1.0 / DK_A)
        y = (o * lax.rsqrt(ms + EPS)) * gain
        o_ref[rows, :] = (y * _silu(g4_ref[rows, 768:1024])).astype(BF16)
        return carry

    lax.fori_loop(0, nsb, finish, 0)


def _gla_call(g4, la, s0f, s0b, gain, cast_w=None, *, batch, seq, row_block0):
    tok = lambda w: pl.BlockSpec((seq, w), lambda b: (row_block0 + b, 0))
    st = pl.BlockSpec((None, 64, 256), lambda b: (b, 0, 0))
    in_specs = [tok(1024), tok(512), st, st, pl.BlockSpec((1, W_A), lambda b: (0, 0))]
    out_specs = [pl.BlockSpec((seq, W_A), lambda b: (b, 0)), st, st]
    out_shape = [jax.ShapeDtypeStruct((batch * seq, W_A), BF16),
                 jax.ShapeDtypeStruct((batch, 64, 256), F32),
                 jax.ShapeDtypeStruct((batch, 64, 256), F32)]
    args = [g4, la, s0f, s0b, gain]
    if cast_w is not None:
        spec, shape = _cast_rider(cast_w, batch)
        in_specs.append(spec)
        out_specs.append(spec)
        out_shape.append(shape)
        args.append(cast_w)
    return pl.pallas_call(
        functools.partial(_gla_kernel, seq=seq, rider=cast_w is not None),
        grid=(batch,),
        in_specs=in_specs,
        out_specs=out_specs,
        out_shape=out_shape,
        scratch_shapes=[pltpu.VMEM((seq, W_A), F32), pltpu.VMEM((seq, W_A), F32)],
        compiler_params=_cparams(("arbitrary",)),
        name=f"gla_{seq}",
    )(*args)


def _state_to_kernel(s):
    b = s.shape[0]
    return jnp.transpose(s, (0, 3, 1, 2)).reshape(b, 64, 256)


def _state_from_kernel(st):
    b = st.shape[0]
    return jnp.transpose(st.reshape(b, 64, N_HEADS_A, DK_A), (0, 2, 3, 1))


def _lambda(lv, lam_init):
    l01 = jnp.sum(lv[0:1, :] * lv[1:2, :], axis=-1, keepdims=True)
    l23 = jnp.sum(lv[2:3, :] * lv[3:4, :], axis=-1, keepdims=True)
    return jnp.exp(l01) - jnp.exp(l23) + lam_init


def _softmax_parts(parts):
    mx = functools.reduce(jnp.maximum, [jnp.max(p, axis=-1, keepdims=True) for p in parts])
    es = [jnp.exp(p - mx) for p in parts]
    den = functools.reduce(lambda a, b: a + b, [jnp.sum(e, axis=-1, keepdims=True) for e in es])
    return [e / den for e in es]


def _diff_finish(o, gain, lam_init):
    o = o * lax.rsqrt(jnp.mean(o * o, axis=-1, keepdims=True) + EPS)
    return ((o * gain) * (1.0 - lam_init)).astype(BF16)


QK_SCALE = DH_B ** -0.5


def _key_halves(k):
    first = lax.broadcasted_iota(jnp.int32, k.shape, 1) < DH_B
    return jnp.where(first, k, 0.0).astype(BF16), jnp.where(first, 0.0, k).astype(BF16)


def _attn_prompt_kernel(lv_ref, q_ref, k_ref, v_ref, gain_ref, *rest, lam_init, n_prev, rider):
    rest = list(rest)
    if rider:
        wb_ref = rest.pop()
        w_ref = rest.pop(2 * n_prev)
        wb_ref[...] = w_ref[...].astype(BF16)
    prev_refs, (o_ref, *cache_refs) = rest[:2 * n_prev], rest[2 * n_prev:]
    lam = _lambda(lv_ref[...], lam_init)

    def head(h):
        cols = slice(h * DV_B, (h + 1) * DV_B)
        q = (q_ref[:, cols] * QK_SCALE).astype(BF16)
        k1, k2 = _key_halves(k_ref[:, cols])
        s1 = _dot_nt(q, k1)
        s2 = _dot_nt(q, k2)
        yield
        (p1,) = _softmax_parts([s1])
        (p2,) = _softmax_parts([s2])
        a = p1 - lam * p2
        o = _dot(a.astype(BF16), v_ref[:, cols].astype(BF16))
        yield
        o_ref[:, cols] = _diff_finish(o, gain_ref[:, cols], lam_init)
        if cache_refs:
            ck_ref, cv_ref = cache_refs
            layers_k = [*prev_refs[0::2], k_ref]
            layers_v = [*prev_refs[1::2], v_ref]
            for l in range(n_prev + 1):
                ck_ref[l, pl.ds(h, SEQ, stride=N_HEADS_B), :] = layers_k[l][:, cols]
                cv_ref[l, pl.ds(h, SEQ, stride=N_HEADS_B), :] = layers_v[l][:, cols]

    _interleave(*[head(h) for h in range(N_HEADS_B)])


def _attn_prompt_call(lv, qb, kb, vb, gain, prev_kv=(), cast_w=None, *, lam_init, write_cache=False):
    blk = pl.BlockSpec((SEQ, W_B), lambda b: (b, 0))
    n_prev = len(prev_kv)
    assert write_cache or not prev_kv
    in_specs = [pl.BlockSpec((4, DH_B), lambda b: (0, 0)), blk, blk, blk,
                pl.BlockSpec((1, W_B), lambda b: (0, 0))] + [blk] * (2 * n_prev)
    args = [lv, qb, kb, vb, gain, *[a for kv in prev_kv for a in kv]]
    out_specs = [blk]
    out_shape = [jax.ShapeDtypeStruct((T_PROMPT, W_B), BF16)]
    if write_cache:
        cache = jax.ShapeDtypeStruct((BATCH, n_prev + 1, SEQ * N_HEADS_B, DV_B), F32)
        cblk = pl.BlockSpec((None, n_prev + 1, SEQ * N_HEADS_B, DV_B), lambda b: (b, 0, 0, 0))
        out_specs += [cblk, cblk]
        out_shape += [cache, cache]
    if cast_w is not None:
        spec, shape = _cast_rider(cast_w, BATCH)
        in_specs.append(spec)
        out_specs.append(spec)
        out_shape.append(shape)
        args.append(cast_w)
    return pl.pallas_call(
        functools.partial(_attn_prompt_kernel, lam_init=lam_init, n_prev=n_prev, rider=cast_w is not None),
        grid=(BATCH,),
        in_specs=in_specs,
        out_specs=out_specs,
        out_shape=out_shape,
        compiler_params=_cparams(("arbitrary",)),
        name="diff_attn_prompt",
    )(*args)


def _rope(x, cos, sin_signed):
    lane = lax.broadcasted_iota(jnp.int32, x.shape, 1)
    first = (lane % (2 * AXIS_PAIRS)) < AXIS_PAIRS
    partner = jnp.where(first, pltpu.roll(x, LANES - AXIS_PAIRS, 1), pltpu.roll(x, AXIS_PAIRS, 1))
    return x * cos + partner * sin_signed


ATT_TQ = 256


def _attn_sample_kernel(lv_ref, q_ref, k_ref, v_ref, kc_ref, vc_ref, cosq_ref, sinq_ref,
                        cosk_ref, sink_ref, gain_ref, *rest, lam_init, n_cast):
    w_refs, o_ref, wb_refs, (k1_ref, k2_ref) = rest[:n_cast], rest[n_cast], rest[n_cast + 1:2 * n_cast + 1], rest[-2:]
    for w_ref, wb_ref in zip(w_refs, wb_refs):
        wb_ref[...] = w_ref[...].astype(BF16)

    @pl.when(pl.program_id(1) == 0)
    def _():
        for h in range(N_HEADS_B):
            cols = slice(h * DV_B, (h + 1) * DV_B)
            k1_ref[:, cols], k2_ref[:, cols] = _key_halves(_rope(k_ref[:, cols], cosk_ref[...], sink_ref[...]))

    lam = _lambda(lv_ref[...], lam_init)

    def head(h):
        cols = slice(h * DV_B, (h + 1) * DV_B)
        q = (_rope(q_ref[:, cols], cosq_ref[...], sinq_ref[...]) * QK_SCALE).astype(BF16)
        c1, c2 = _key_halves(kc_ref[:, cols])
        s1 = [_dot_nt(q, k1_ref[:, cols]), _dot_nt(q, c1)]
        s2 = [_dot_nt(q, k2_ref[:, cols]), _dot_nt(q, c2)]
        yield
        p1 = _softmax_parts(s1)
        p2 = _softmax_parts(s2)
        a_own = p1[0] - lam * p2[0]
        a_ctx = p1[1] - lam * p2[1]
        o = (_dot(a_own.astype(BF16), v_ref[:, cols].astype(BF16))
             + _dot(a_ctx.astype(BF16), vc_ref[:, cols].astype(BF16)))
        yield
        o_ref[:, cols] = _diff_finish(o, gain_ref[:, cols], lam_init)

    _interleave(*[head(h) for h in range(N_HEADS_B)])


def _attn_sample_call(lv, qb, kb, vb, kc, vc, cos, sin_signed, gain, cast_ws=(), *, lam_init):
    tq = ATT_TQ
    nq = DEC_SEQ // tq
    p0 = T_PROMPT // tq
    s0 = T_PROMPT // DEC_SEQ
    qblk = pl.BlockSpec((tq, W_B), lambda b, t: (p0 + b * nq + t, 0))
    kvblk = pl.BlockSpec((DEC_SEQ, W_B), lambda b, t: (s0 + b, 0))
    cblk = pl.BlockSpec((None, PAST_LEN, W_B), lambda b, t: (b, 0, 0))
    n_steps = DEC_BATCH * nq
    cast_specs = []
    for w in cast_ws:
        assert w.shape[0] % (16 * n_steps) == 0
        cast_specs.append(pl.BlockSpec((w.shape[0] // n_steps, w.shape[1]), lambda b, t: (b * nq + t, 0)))
    return pl.pallas_call(
        functools.partial(_attn_sample_kernel, lam_init=lam_init, n_cast=len(cast_ws)),
        grid=(DEC_BATCH, nq),
        in_specs=[pl.BlockSpec((4, DH_B), lambda b, t: (0, 0)), qblk, kvblk, kvblk, cblk, cblk,
                  pl.BlockSpec((tq, DV_B), lambda b, t: (t, 0)),
                  pl.BlockSpec((tq, DV_B), lambda b, t: (t, 0)),
                  pl.BlockSpec((DEC_SEQ, DV_B), lambda b, t: (0, 0)),
                  pl.BlockSpec((DEC_SEQ, DV_B), lambda b, t: (0, 0)),
                  pl.BlockSpec((1, W_B), lambda b, t: (0, 0))] + cast_specs,
        out_specs=[pl.BlockSpec((tq, W_B), lambda b, t: (b * nq + t, 0))] + cast_specs,
        out_shape=[jax.ShapeDtypeStruct((T_SAMPLE, W_B), BF16)] + [jax.ShapeDtypeStruct(w.shape, BF16) for w in cast_ws],
        scratch_shapes=[pltpu.VMEM((DEC_SEQ, W_B), BF16), pltpu.VMEM((DEC_SEQ, W_B), BF16)],
        compiler_params=_cparams(("arbitrary", "arbitrary")),
        name="diff_attn_sample",
    )(lv, qb, kb, vb, kc, vc, cos, sin_signed, cos, sin_signed, gain, *cast_ws)


def _rope_tables():
    rows = DEC_SEQ // GRID_W
    row = jnp.repeat(jnp.arange(rows, dtype=F32), GRID_W)
    col = jnp.tile(jnp.arange(GRID_W, dtype=F32), rows)
    freqs = ROPE_THETA ** (-jnp.arange(AXIS_PAIRS, dtype=F32) / AXIS_PAIRS)
    ar, ac = row[:, None] * freqs, col[:, None] * freqs
    cos64 = jnp.concatenate([jnp.cos(ar), jnp.cos(ar), jnp.cos(ac), jnp.cos(ac)], axis=-1)
    sin64 = jnp.concatenate([-jnp.sin(ar), jnp.sin(ar), -jnp.sin(ac), jnp.sin(ac)], axis=-1)
    return jnp.tile(cos64, (1, 2)), jnp.tile(sin64, (1, 2))


def _group_mean(x, ones64):
    hi, lo = _split_bf16(x)
    return (_dot(hi, ones64) + _dot(lo, ones64)) * (1.0 / DG_C)


def _sgu(uv, ws_ref, bs_ref, fillers=()):
    fillers = list(fillers)
    fill = lambda: fillers.pop(0)() if fillers else None
    r = lax.broadcasted_iota(jnp.int32, (W_C, W_C), 0)
    s = lax.broadcasted_iota(jnp.int32, (W_C, W_C), 1)
    ones64 = jnp.where((r // DG_C) == (s // DG_C), 1.0, 0.0).astype(BF16)
    lane = lax.broadcasted_iota(jnp.int32, (SGU_CHUNK, W_C), 1)
    outs = []
    for n in range(uv.shape[0] // SGU_CHUNK):
        rows = slice(n * SGU_CHUNK, (n + 1) * SGU_CHUNK)
        u = _gelu_tanh(uv[rows, 0:256])
        v = _gelu_tanh(uv[rows, 256:512])
        mu = _group_mean(v, ones64)
        fill()
        d = v - mu
        var = _group_mean(d * d, ones64)
        fill()
        vn = d * lax.rsqrt(var + EPS)
        s_mix = bs_ref[...]
        for g in range(N_GROUPS_C):
            vn_g = jnp.where((lane // DG_C) == g, vn, 0.0).astype(BF16)
            s_mix = s_mix + _dot(ws_ref[g], vn_g)
        fill()
        outs.append((u * s_mix).astype(BF16))
    while fillers:
        fill()
    return jnp.concatenate(outs, axis=0)


OUT_TM = 1024
OUT_SLABS = 4


SEL_LANE0 = N_EXPERTS


def _top2_gates(logits):
    lane = lax.broadcasted_iota(jnp.int32, logits.shape, 1).astype(F32)
    neg = -jnp.inf
    lg = jnp.where(lane < N_EXPERTS, logits, neg)
    m1 = jnp.max(lg, axis=-1, keepdims=True)
    i1 = jnp.min(jnp.where(lg == m1, lane, float(LANES)), axis=-1, keepdims=True)
    lg2 = jnp.where(lane == i1, neg, lg)
    m2 = jnp.max(lg2, axis=-1, keepdims=True)
    i2 = jnp.min(jnp.where(lg2 == m2, lane, float(LANES)), axis=-1, keepdims=True)
    e2 = jnp.exp(m2 - m1)
    den = 1.0 + e2
    gates = jnp.where(lane == i1, 1.0 / den, 0.0) + jnp.where(lane == i2, e2 / den, 0.0)
    sel = jnp.where((lane == i1 + SEL_LANE0) | (lane == i2 + SEL_LANE0), 1.0, 0.0)
    return gates + sel


def _store_token_tiles(ref, val, r0=0):
    n = val.shape[0]
    for k in range(D_MODEL // LANES):
        ref[pl.ds(8 * r0 + k, n, stride=8), :] = val[:, k * LANES:(k + 1) * LANES]


def _load_token_tiles(ref, n):
    return jnp.concatenate([ref[pl.ds(k, n, stride=8), :] for k in range(D_MODEL // LANES)], axis=-1)


def _out_proj_kernel(*refs, n_x, moe):
    oa_refs, ob_refs, (oc_ref,), x_refs = refs[0:2], refs[2:4], refs[4:5], refs[5:5 + n_x]
    w_ref, g1_ref, nrm_ref, sc_ref, sh_ref, *rest = refs[5 + n_x:]
    if moe:
        rw_ref, x1_ref, h2t_ref, gates_ref, gates_t_ref, wo_ref = rest
    else:
        x1_ref, h2_ref, wo_ref = rest

    @pl.when(pl.program_id(0) == 0)
    def _():
        wo_ref[...] = w_ref[...].astype(BF16)

    if moe:
        rw_hi, rw_lo = _split_bf16(rw_ref[...])

    def rows_chain(r0, n):
        rs = slice(r0, r0 + n)
        y = (_dot(_pick_piece(oa_refs, OUT_TM, rs), wo_ref[0:256, :])
             + _dot(_pick_piece(ob_refs, OUT_TM, rs), wo_ref[256:768, :]) + _dot(oc_ref[rs, :], wo_ref[768:1024, :]))
        yield
        x1 = _pick_piece(x_refs, OUT_TM, rs) + g1_ref[...] * y
        x1_ref[rs, :] = x1
        yn = x1 * lax.rsqrt(jnp.mean(x1 * x1, axis=-1, keepdims=True) + EPS)
        h = (yn * nrm_ref[...]) * (1.0 + sc_ref[...]) + sh_ref[...]
        if moe:
            _store_token_tiles(h2t_ref, h, r0)
            h_hi, h_lo = _split_bf16(h)
            logits = _dot(h_hi, rw_hi) + (_dot(h_lo, rw_hi) + _dot(h_hi, rw_lo))
            yield
            gates = _top2_gates(logits)
            gates_ref[rs, :] = gates
            gates_t_ref[:, rs] = gates.T
        else:
            h2_ref[rs, :] = h.astype(BF16)

    n_slab = OUT_TM // OUT_SLABS
    _interleave(*[rows_chain(i * n_slab, n_slab) for i in range(OUT_SLABS)])


def _out_proj_call(oa_pieces, ob_pieces, oc, x_pieces, w_out, l, mod4, norm_g, router_w=None):
    tm = OUT_TM
    n = T_ALL // tm
    moe = router_w is not None
    row = functools.partial(_mod_row, tm=tm)
    mod_spec = lambda k: pl.BlockSpec((None, None, 1, D_MODEL), lambda i: (row(i), k, 0, 0))
    tok = lambda w: pl.BlockSpec((tm, w), lambda i: (i, 0))
    full = lambda a: pl.BlockSpec(a.shape, lambda i: (0,) * a.ndim)
    in_specs = (_piece_specs(oa_pieces, tm, W_A) + _piece_specs(ob_pieces, tm, W_B) + [tok(W_C)]
                + _piece_specs(x_pieces, tm, D_MODEL)
                + [_resident_layer(w_out, l), mod_spec(2), full(norm_g), mod_spec(4), mod_spec(3)])
    args = [*oa_pieces, *ob_pieces, oc, *x_pieces, w_out, mod4, norm_g, mod4, mod4]
    if moe:
        in_specs.append(full(router_w))
        args.append(router_w)
        out_specs = [tok(D_MODEL), pl.BlockSpec((tm * 8, LANES), lambda i: (i, 0)), tok(LANES),
                     pl.BlockSpec((LANES, tm), lambda i: (0, i))]
        out_shape = [jax.ShapeDtypeStruct((T_ALL, D_MODEL), F32), jax.ShapeDtypeStruct((T_ALL * 8, LANES), F32),
                     jax.ShapeDtypeStruct((T_ALL, LANES), F32), jax.ShapeDtypeStruct((LANES, T_ALL), F32)]
    else:
        out_specs = [tok(D_MODEL), tok(D_MODEL)]
        out_shape = [jax.ShapeDtypeStruct((T_ALL, D_MODEL), F32), jax.ShapeDtypeStruct((T_ALL, D_MODEL), BF16)]
    return pl.pallas_call(
        functools.partial(_out_proj_kernel, n_x=len(x_pieces), moe=moe),
        grid=(n,),
        in_specs=in_specs,
        out_specs=out_specs,
        out_shape=out_shape,
        scratch_shapes=[pltpu.VMEM((D_MODEL, D_MODEL), BF16)],
        compiler_params=_cparams(("arbitrary",)),
        name="out_proj_moe" if moe else "out_proj",
    )(*args)


FFN_TM = 512
MXU_N = 256
FFN_SPLITS = (0, 1024, 2048, D_FF)
assert all(s % MXU_N == 0 for s in FFN_SPLITS)


def _swiglu(h, wg_ref, wu_ref, wd_ref):
    out = None
    for c0, c1 in zip(FFN_SPLITS[:-1], FFN_SPLITS[1:]):
        act = _silu(_dot(h, wg_ref[:, c0:c1])) * _dot(h, wu_ref[:, c0:c1])
        d = _dot(act.astype(BF16), wd_ref[c0:c1, :])
        out = d if out is None else out + d
    return out


FFN_RIDER_BLOCKS = 16


def _ffn_kernel(h_ref, x_ref, g2_ref, wg_ref, wu_ref, wd_ref, w_ref, o_ref, wb_ref):
    @pl.when(pl.program_id(0) < FFN_RIDER_BLOCKS)
    def _():
        wb_ref[...] = w_ref[...].astype(BF16)

    o_ref[...] = x_ref[...] + g2_ref[...] * _swiglu(h_ref[...], wg_ref, wu_ref, wd_ref)


def _ffn_call(h2, x1, mod4, wg, wu, wd, cast_w):
    tm = FFN_TM
    row = functools.partial(_mod_row, tm=tm)
    tok = lambda w: pl.BlockSpec((tm, w), lambda i: (i, 0))
    resident = lambda a: pl.BlockSpec(a.shape, lambda i: (0, 0), pipeline_mode=pl.Buffered(1))
    rider_spec, rider_shape = _cast_rider(cast_w, FFN_RIDER_BLOCKS)
    return pl.pallas_call(
        _ffn_kernel,
        grid=(T_ALL // tm,),
        in_specs=[tok(D_MODEL), tok(D_MODEL),
                  pl.BlockSpec((None, None, 1, D_MODEL), lambda i: (row(i), 5, 0, 0)),
                  resident(wg), resident(wu), resident(wd), rider_spec],
        out_specs=[tok(D_MODEL), rider_spec],
        out_shape=[jax.ShapeDtypeStruct((T_ALL, D_MODEL), F32), rider_shape],
        compiler_params=_cparams(("arbitrary",)),
        name="ffn_dense",
    )(h2, x1, mod4, wg, wu, wd, cast_w)


MOE_TM = 512
MOE_NT_MAX = (2 * T_ALL) // MOE_TM + N_EXPERTS
MOE_ROWS = MOE_NT_MAX * MOE_TM
PLAN_BLK = 512
MISC_LAST_START = 8
MISC_NT = 16


def _moe_plan_kernel(gt_ref, posa_ref, posb_ref, te_ref, ti_ref, misc_ref):
    tm = float(MOE_TM)
    sel = gt_ref[SEL_LANE0:SEL_LANE0 + N_EXPERTS, :]
    cnt = jnp.sum(sel, axis=1, keepdims=True)
    nt = jnp.floor((cnt + (tm - 1.0)) * (1.0 / tm))
    sub = lax.broadcasted_iota(jnp.int32, (N_EXPERTS, LANES), 0).astype(F32)
    lane = lax.broadcasted_iota(jnp.int32, (N_EXPERTS, LANES), 1).astype(F32)
    nt_b = jnp.broadcast_to(nt, (N_EXPERTS, LANES))
    nt_row = jnp.sum(jnp.where(sub == lane, nt_b, 0.0), axis=0, keepdims=True)
    toff = jnp.sum(jnp.where(lane < sub, jnp.broadcast_to(nt_row, (N_EXPERTS, LANES)), 0.0),
                   axis=1, keepdims=True)
    tend = toff + nt
    n_total = jnp.sum(nt, axis=0, keepdims=True)
    jc = jnp.minimum(lane, n_total - 1.0)
    te = jnp.sum(jnp.where(jc >= tend, 1.0, 0.0), axis=0, keepdims=True)
    te_ref[...] = te.astype(jnp.int32)
    ti_ref[...] = jc[0:1, :].astype(jnp.int32)
    last_start = (tend - 1.0) * tm
    ls_row = jnp.sum(jnp.where(sub + MISC_LAST_START == lane, jnp.broadcast_to(last_start, (N_EXPERTS, LANES)), 0.0),
                     axis=0, keepdims=True)
    nt_row2 = jnp.sum(jnp.where(sub + MISC_NT == lane, nt_b, 0.0), axis=0, keepdims=True)
    misc = jnp.where(lane[0:1, :] == 0.0, n_total, 0.0) + ls_row + nt_row2
    misc_ref[...] = misc.astype(jnp.int32)

    off = toff * tm
    r = lax.broadcasted_iota(jnp.int32, (PLAN_BLK, PLAN_BLK), 0)
    c = lax.broadcasted_iota(jnp.int32, (PLAN_BLK, PLAN_BLK), 1)
    upper = jnp.where(r <= c, 1.0, 0.0).astype(BF16)
    carry = jnp.zeros((N_EXPERTS, 1), F32)
    for blk in range(T_ALL // PLAN_BLK):
        cols = slice(blk * PLAN_BLK, (blk + 1) * PLAN_BLK)
        s = gt_ref[SEL_LANE0:SEL_LANE0 + N_EXPERTS, cols]
        rank = _dot(s.astype(BF16), upper) + carry
        pos = off + rank - 1.0
        posa_ref[:, cols] = jnp.min(jnp.where(s > 0.0, pos, 1e9), axis=0, keepdims=True).astype(jnp.int32)
        posb_ref[:, cols] = jnp.max(jnp.where(s > 0.0, pos, -1.0), axis=0, keepdims=True).astype(jnp.int32)
        carry = carry + jnp.sum(s, axis=1, keepdims=True)


def _moe_plan_call(gates_t):
    row = lambda w: jax.ShapeDtypeStruct((1, w), jnp.int32)
    full = lambda w: pl.BlockSpec((1, w), lambda: (0, 0))
    return pl.pallas_call(
        _moe_plan_kernel,
        in_specs=[pl.BlockSpec((LANES, T_ALL), lambda: (0, 0))],
        out_specs=[full(T_ALL), full(T_ALL), full(LANES), full(LANES), full(LANES)],
        out_shape=[row(T_ALL), row(T_ALL), row(LANES), row(LANES), row(LANES)],
        compiler_params=pltpu.CompilerParams(vmem_limit_bytes=VMEM_LIMIT),
        name="moe_plan",
    )(gates_t)


DMA_UNROLL = 8


def _row_tile(ref, row):
    return ref.at[pl.ds(pl.multiple_of(row * 8, 8), 8), :]


def _moe_scatter_kernel(misc_ref, posa_ref, posb_ref, h_ref, xs_ref, zero_ref, sem):
    tm = h_ref.shape[0] // 8

    @pl.when(pl.program_id(0) == 0)
    def _():
        zero_ref[...] = jnp.zeros_like(zero_ref)

        def zero_tile(first_row):
            start = pl.multiple_of(first_row * 8, 8)
            cp = pltpu.make_async_copy(zero_ref, xs_ref.at[pl.ds(start, MOE_TM * 8), :], sem.at[0])
            cp.start()
            cp.wait()

        for e in range(N_EXPERTS):
            @pl.when(misc_ref[0, MISC_NT + e] > 0)
            def _():
                zero_tile(misc_ref[0, MISC_LAST_START + e])

        def zero_tail(j, carry):
            zero_tile(j * MOE_TM)
            return carry

        lax.fori_loop(misc_ref[0, 0], MOE_NT_MAX, zero_tail, 0)

    def issue(r, carry):
        src = _row_tile(h_ref, r)
        pltpu.make_async_copy(src, _row_tile(xs_ref, posa_ref[0, r]), sem.at[0]).start(priority=0)
        pltpu.make_async_copy(src, _row_tile(xs_ref, posb_ref[0, r]), sem.at[1]).start(priority=1)
        return carry

    lax.fori_loop(0, tm, issue, 0, unroll=DMA_UNROLL)
    for k in range(2):
        pltpu.make_async_copy(h_ref, xs_ref.at[pl.ds(0, tm * 8), :], sem.at[k]).wait()


SCATTER_TM = 512


def _moe_scatter_call(misc, posa3, posb3, h2t):
    tm = SCATTER_TM
    smem_row = pl.BlockSpec((None, 1, tm), lambda i: (i, 0, 0), memory_space=pltpu.SMEM)
    return pl.pallas_call(
        _moe_scatter_kernel,
        grid=(T_ALL // tm,),
        in_specs=[pl.BlockSpec((1, LANES), lambda i: (0, 0), memory_space=pltpu.SMEM), smem_row, smem_row,
                  pl.BlockSpec((tm * 8, LANES), lambda i: (i, 0))],
        out_specs=pl.BlockSpec(memory_space=pl.ANY),
        out_shape=jax.ShapeDtypeStruct((MOE_ROWS * 8, LANES), F32),
        scratch_shapes=[pltpu.VMEM((MOE_TM * 8, LANES), F32), pltpu.SemaphoreType.DMA((2,))],
        compiler_params=_cparams(("arbitrary",)),
        name="moe_scatter",
    )(misc, posa3, posb3, h2t)


def _ffn_grouped_kernel(te_ref, ti_ref, misc_ref, x_ref, wg_ref, wu_ref, wd_ref, o_ref):
    j = pl.program_id(0)

    @pl.when(j < misc_ref[0])
    def _():
        h = _load_token_tiles(x_ref, MOE_TM).astype(BF16)
        _store_token_tiles(o_ref, _swiglu(h, wg_ref, wu_ref, wd_ref))

    @pl.when(j >= misc_ref[0])
    def _():
        o_ref[...] = jnp.zeros_like(o_ref)


def _ffn_grouped_call(te, ti, misc, xs, wg, wu, wd):
    expert = lambda *s: pl.BlockSpec((None,) + s, lambda j, te, ti, misc: (te[j], 0, 0))
    grid_spec = pltpu.PrefetchScalarGridSpec(
        num_scalar_prefetch=3,
        grid=(MOE_NT_MAX,),
        in_specs=[pl.BlockSpec((MOE_TM * 8, LANES), lambda j, te, ti, misc: (ti[j], 0)),
                  expert(D_MODEL, D_FF), expert(D_MODEL, D_FF), expert(D_FF, D_MODEL)],
        out_specs=pl.BlockSpec((MOE_TM * 8, LANES), lambda j, te, ti, misc: (j, 0)),
    )
    return pl.pallas_call(
        _ffn_grouped_kernel,
        grid_spec=grid_spec,
        out_shape=jax.ShapeDtypeStruct((MOE_ROWS * 8, LANES), F32),
        compiler_params=_cparams(("arbitrary",)),
        name="ffn_grouped",
    )(te, ti, misc, xs, wg, wu, wd)


COMBINE_TM = 256


def _moe_combine_kernel(posa_ref, posb_ref, posa_next_ref, posb_next_ref, ys_ref, x_ref, g2_ref, gates_ref, nf_ref,
                        op_ref, os_ref, bufa_ref, bufb_ref, sem):
    tm = COMBINE_TM
    i = pl.program_id(0)
    n = pl.num_programs(0)
    slot = i % 2

    def gather(pa_ref, pb_ref, s):
        def issue(r, carry):
            pltpu.make_async_copy(_row_tile(ys_ref, pa_ref[0, r]), _row_tile(bufa_ref.at[s], r),
                                  sem.at[s, 0]).start(priority=0)
            pltpu.make_async_copy(_row_tile(ys_ref, pb_ref[0, r]), _row_tile(bufb_ref.at[s], r),
                                  sem.at[s, 1]).start(priority=1)
            return carry

        lax.fori_loop(0, tm, issue, 0, unroll=DMA_UNROLL)

    @pl.when(i == 0)
    def _():
        gather(posa_ref, posb_ref, 0)

    @pl.when(i + 1 < n)
    def _():
        gather(posa_next_ref, posb_next_ref, 1 - slot)

    gates = gates_ref[...]
    lane = lax.broadcasted_iota(jnp.int32, gates.shape, 1).astype(F32)
    is_sel = (lane >= SEL_LANE0) & (lane < SEL_LANE0 + N_EXPERTS) & (gates > 0.0)
    ia = jnp.min(jnp.where(is_sel, lane, float(LANES)), axis=-1, keepdims=True) - SEL_LANE0
    ib = jnp.max(jnp.where(is_sel, lane, -1.0), axis=-1, keepdims=True) - SEL_LANE0
    wa = jnp.sum(jnp.where(lane == ia, gates, 0.0), axis=-1, keepdims=True)
    wb = jnp.sum(jnp.where(lane == ib, gates, 0.0), axis=-1, keepdims=True)

    pltpu.make_async_copy(ys_ref.at[pl.ds(0, tm * 8), :], bufa_ref.at[slot], sem.at[slot, 0]).wait()
    pltpu.make_async_copy(ys_ref.at[pl.ds(0, tm * 8), :], bufb_ref.at[slot], sem.at[slot, 1]).wait()

    y = wa * _load_token_tiles(bufa_ref.at[slot], tm) + wb * _load_token_tiles(bufb_ref.at[slot], tm)
    out = x_ref[...] + g2_ref[...] * y
    out = (out * lax.rsqrt(jnp.mean(out * out, axis=-1, keepdims=True) + EPS)) * nf_ref[...]

    @pl.when(i < T_PROMPT // tm)
    def _():
        op_ref[...] = out

    @pl.when(i >= T_PROMPT // tm)
    def _():
        os_ref[...] = out


def _moe_combine_call(posa3, posb3, ys, x1, mod4, gates, norm_f):
    tm = COMBINE_TM
    n = T_ALL // tm
    n_p = T_PROMPT // tm
    row = functools.partial(_mod_row, tm=tm)
    smem_row = pl.BlockSpec((None, 1, tm), lambda i: (i, 0, 0), memory_space=pltpu.SMEM)
    smem_next = pl.BlockSpec((None, 1, tm), lambda i: (jnp.minimum(i + 1, n - 1), 0, 0), memory_space=pltpu.SMEM)
    tok = lambda w: pl.BlockSpec((tm, w), lambda i: (i, 0))
    return pl.pallas_call(
        _moe_combine_kernel,
        grid=(n,),
        in_specs=[smem_row, smem_row, smem_next, smem_next, pl.BlockSpec(memory_space=pl.ANY), tok(D_MODEL),
                  pl.BlockSpec((None, None, 1, D_MODEL), lambda i: (row(i), 5, 0, 0)), tok(LANES),
                  pl.BlockSpec((1, D_MODEL), lambda i: (0, 0))],
        out_specs=[pl.BlockSpec((tm, D_MODEL), lambda i: (jnp.minimum(i, n_p - 1), 0)),
                   pl.BlockSpec((tm, D_MODEL), lambda i: (jnp.maximum(i - n_p, 0), 0))],
        out_shape=[jax.ShapeDtypeStruct((T_PROMPT, D_MODEL), F32), jax.ShapeDtypeStruct((T_SAMPLE, D_MODEL), F32)],
        scratch_shapes=[pltpu.VMEM((2, tm * 8, LANES), F32), pltpu.VMEM((2, tm * 8, LANES), F32),
                        pltpu.SemaphoreType.DMA((2, 2))],
        compiler_params=_cparams(("arbitrary",)),
        name="moe_combine",
    )(posa3, posb3, posa3, posb3, ys, x1, mod4, gates, norm_f)


def _moe_call(h2t, x1, mod4, gates, gates_t, wg, wu, wd, norm_f):
    posa, posb, te, ti, misc = _moe_plan_call(gates_t)
    xs = _moe_scatter_call(misc, posa.reshape(T_ALL // SCATTER_TM, 1, SCATTER_TM),
                           posb.reshape(T_ALL // SCATTER_TM, 1, SCATTER_TM), h2t)
    ys = _ffn_grouped_call(te.reshape(LANES), ti.reshape(LANES), misc.reshape(LANES), xs, wg, wu, wd)
    return _moe_combine_call(posa.reshape(T_ALL // COMBINE_TM, 1, COMBINE_TM),
                             posb.reshape(T_ALL // COMBINE_TM, 1, COMBINE_TM), ys, x1, mod4, gates, norm_f)


def kernel(x_prompt, x_sample, cache_k, cache_v, state_gla_fwd, state_gla_bwd, c, c_ctx, w_ada, b_ada, norm_mix, norm_ffn, w_in, w_out, gla_w_up, gla_b_up, gla_norm, diff_lambda, diff_norm, sgu_w, sgu_b, ffn_w_gate, ffn_w_up, ffn_w_down, router_w, moe_w_gate, moe_w_up, moe_w_down, norm_f):
    assert DEPTH == 2
    x_pieces = [x_prompt.reshape(T_PROMPT, D_MODEL), x_sample.reshape(T_SAMPLE, D_MODEL)]
    cvecs = jnp.concatenate([c_ctx[None, :], c, jnp.zeros((N_MOD_ROWS - 1 - DEC_BATCH, D_MODEL), F32)], axis=0)
    mod = _ada_call(cvecs, w_ada, b_ada)
    cos, sin_signed = _rope_tables()
    zeros_state = jnp.zeros((BATCH, 64, 256), F32)

    w_in_t = jnp.swapaxes(w_in, 1, 2)
    moe_w = [moe_w_gate[0], moe_w_up[0], moe_w_down[0]]
    prev_kv, sfs, sbs = [], [], []
    for l in range(DEPTH):
        mod4 = mod[l].reshape(N_MOD_ROWS, 6, 1, D_MODEL)
        w_up = jnp.zeros((LANES, 2 * W_A), F32)
        w_up = w_up.at[0:GLA_RANK, 0:W_A].set(gla_w_up[l, 0]).at[GLA_RANK:2 * GLA_RANK, W_A:].set(gla_w_up[l, 1])
        b_up = gla_b_up[l].reshape(1, 2 * W_A)
        bs_full = jnp.repeat(sgu_b[l].T, DG_C, axis=1)
        g4, la, qb, kb, vb, oc = _in_proj_call(x_pieces, norm_mix[l][None, :], mod4, w_in_t, l, w_up.astype(BF16), b_up,
                                               sgu_w[l].astype(BF16), bs_full)

        gain_a = gla_norm[l][None, :]
        oa_p, sf, sb, moe_w[l] = _gla_call(g4, la, zeros_state, zeros_state, gain_a, moe_w[l],
                                           batch=BATCH, seq=SEQ, row_block0=0)
        oa_s, _, _ = _gla_call(g4, la, _state_to_kernel(state_gla_fwd[:, l]),
                               _state_to_kernel(state_gla_bwd[:, l]), gain_a,
                               batch=DEC_BATCH, seq=DEC_SEQ, row_block0=T_PROMPT // DEC_SEQ)

        lam_init = 0.8 - 0.6 * math.exp(-0.3 * l)
        gain_b = diff_norm[l][None, :]
        if l < DEPTH - 1:
            (ob_p,) = _attn_prompt_call(diff_lambda[l], qb, kb, vb, gain_b, lam_init=lam_init)
            prev_kv.append((kb, vb))
        else:
            ob_p, new_cache_k, new_cache_v = _attn_prompt_call(diff_lambda[l], qb, kb, vb, gain_b, prev_kv,
                                                               lam_init=lam_init, write_cache=True)
        ob_s, *dense_w = _attn_sample_call(diff_lambda[l], qb, kb, vb,
                                           cache_k[:, l].reshape(DEC_BATCH, PAST_LEN, W_B),
                                           cache_v[:, l].reshape(DEC_BATCH, PAST_LEN, W_B),
                                           cos, sin_signed, gain_b,
                                           (ffn_w_gate[0], ffn_w_up[0], ffn_w_down[0]) if l == 0 else (),
                                           lam_init=lam_init)

        if l == 0:
            x1, h2 = _out_proj_call([oa_p, oa_s], [ob_p, ob_s], oc, x_pieces, w_out, l, mod4, norm_ffn[l][None, :])
            x_next, moe_w[2] = _ffn_call(h2, x1, mod4, *dense_w, moe_w[2])
            x_pieces = [x_next]
        else:
            rw = jnp.pad(router_w[0], ((0, 0), (0, LANES - N_EXPERTS)))
            x1, h2t, gates, gates_t = _out_proj_call([oa_p, oa_s], [ob_p, ob_s], oc, x_pieces, w_out, l, mod4,
                                                     norm_ffn[l][None, :], rw)
            y_prompt, y_sample = _moe_call(h2t, x1, mod4, gates, gates_t, *moe_w, norm_f[None, :])

        sfs.append(_state_from_kernel(sf))
        sbs.append(_state_from_kernel(sb))

    return (y_prompt.reshape(BATCH, SEQ, D_MODEL), y_sample.reshape(DEC_BATCH, DEC_SEQ, D_MODEL),
            new_cache_k.reshape(BATCH, DEPTH, SEQ, N_HEADS_B, DV_B), new_cache_v.reshape(BATCH, DEPTH, SEQ, N_HEADS_B, DV_B),
            jnp.stack(sfs, axis=1), jnp.stack(sbs, axis=1))
```

```python
import functools
import math

import jax
import jax.numpy as jnp
import numpy as np
from jax import lax
from jax.experimental import pallas as pl
from jax.experimental.pallas import tpu as pltpu

F32 = jnp.float32
BF16 = jnp.bfloat16

D_MODEL = 1024
BATCH = 32
SEQ = 256
DEPTH = 2
DEC_BATCH = 2
DEC_SEQ = 1024
PAST_LEN = 256
GRID_W = 64
N_HEADS_A = 4
DK_A = 64
W_A = 256
GLA_RANK = 16
GLA_TAU = 16.0
GLA_CHUNK = 64
N_HEADS_B = 4
DH_B = 64
DV_B = 128
W_B = 512
ROPE_THETA = 10000.0
AXIS_PAIRS = DH_B // 4
N_GROUPS_C = 4
DG_C = 64
W_C = 256
SGU_CHUNK = 128
D_FF = 2816
N_EXPERTS = 8
EPS = 1e-6

T_PROMPT = BATCH * SEQ
T_SAMPLE = DEC_BATCH * DEC_SEQ
T_ALL = T_PROMPT + T_SAMPLE
N_MOD_ROWS = 8
LANES = 128
VMEM_LIMIT = 56 * 1024 * 1024


def _cparams(sem):
    return pltpu.CompilerParams(dimension_semantics=sem, vmem_limit_bytes=VMEM_LIMIT)


def _dot(a, b):
    return jnp.dot(a, b, preferred_element_type=F32)


def _dot_nt(a, b):
    return lax.dot_general(a, b, (((1,), (1,)), ((), ())), preferred_element_type=F32)


def _dot_tn(a, b):
    return lax.dot_general(a, b, (((0,), (0,)), ((), ())), preferred_element_type=F32)


def _split_bf16(x):
    hi = x.astype(BF16)
    lo = (x - hi.astype(F32)).astype(BF16)
    return hi, lo


def _dot3(a, w):
    a_hi, a_lo = _split_bf16(a)
    w_hi, w_lo = _split_bf16(w)
    return _dot(a_hi, w_hi) + (_dot(a_lo, w_hi) + _dot(a_hi, w_lo))


def _sigmoid(x):
    return 1.0 / (1.0 + jnp.exp(-x))


def _silu(x):
    return x * _sigmoid(x)


def _gelu_tanh(x):
    c = math.sqrt(2.0 / math.pi)
    return x * (0.5 * (1.0 + jnp.tanh(c * (x + 0.044715 * (x * x * x)))))


def _log_sigmoid(x):
    return jnp.minimum(x, 0.0) - jnp.log(1.0 + jnp.exp(-jnp.abs(x)))


def _mod_row(i, tm):
    n_p = T_PROMPT // tm
    per_b = DEC_SEQ // tm
    return jnp.where(i < n_p, 0, 1 + (i - n_p) // per_b)


ADA_TN = 1536


def _ada_kernel(c_ref, w_ref, b_ref, o_ref):
    a = _silu(c_ref[...])
    o_ref[...] = _dot3(a, w_ref[...]) + b_ref[...]


def _ada_call(cvecs, w_ada, b_ada):
    n_col = (6 * D_MODEL) // ADA_TN
    return pl.pallas_call(
        _ada_kernel,
        grid=(DEPTH, n_col),
        in_specs=[
            pl.BlockSpec((N_MOD_ROWS, D_MODEL), lambda l, j: (0, 0)),
            pl.BlockSpec((None, D_MODEL, ADA_TN), lambda l, j: (l, 0, j)),
            pl.BlockSpec((None, 1, ADA_TN), lambda l, j: (l, 0, j)),
        ],
        out_specs=pl.BlockSpec((None, N_MOD_ROWS, ADA_TN), lambda l, j: (l, 0, j)),
        out_shape=jax.ShapeDtypeStruct((DEPTH, N_MOD_ROWS, 6 * D_MODEL), F32),
        compiler_params=_cparams(("arbitrary", "arbitrary")),
        name="ada_mod",
    )(cvecs, w_ada, b_ada.reshape(DEPTH, 1, 6 * D_MODEL))


IN_TM = 512
IN_COLS = 3104
Z_COL0 = 1024
Z_COLS = 2 * GLA_RANK
W_MAIN = 3072


def _piece_specs(pieces, tm, width):
    specs, t0 = [], 0
    for arr in pieces:
        nt = arr.shape[0] // tm
        specs.append(pl.BlockSpec((tm, width), lambda i, t0=t0, nt=nt: (jnp.clip(i - t0, 0, nt - 1), 0)))
        t0 += nt
    assert t0 * tm == T_ALL and len(pieces) in (1, 2) and (len(pieces) == 1 or pieces[0].shape[0] == T_PROMPT)
    return specs


def _pick_piece(refs, tm, rows=slice(None)):
    if len(refs) == 1:
        return refs[0][rows, :]
    return jnp.where(pl.program_id(0) < T_PROMPT // tm, refs[0][rows, :], refs[1][rows, :])


RIDER_BLOCKS_20 = 16


W_STAGE_ROWS = 512


def _in_proj_kernel(*refs, n_x, layer):
    x_refs = refs[:n_x]
    (nrm_ref, sh_ref, sc_ref, w_hbm, wup_ref, bup_ref, ws_ref, bs_ref, cast_ref,
     g4_ref, la_ref, qb_ref, kb_ref, vb_ref, oc_ref, cast_out_ref, wm_ref, wz_ref, stage_ref, sem) = refs[n_x:]

    @pl.when(pl.program_id(0) < RIDER_BLOCKS_20)
    def _():
        cast_out_ref[...] = cast_ref[...].astype(BF16)

    @pl.when(pl.program_id(0) == 0)
    def _():
        n = W_STAGE_ROWS
        moves = [(src, src if src < Z_COL0 else src - Z_COLS)
                 for src in (*range(0, Z_COL0, n), *range(Z_COL0 + Z_COLS, IN_COLS, n))]
        copy = lambda k: pltpu.make_async_copy(w_hbm.at[layer, pl.ds(moves[k][0], n), :], stage_ref.at[k % 2],
                                               sem.at[k % 2])
        copy(0).start()
        for k, (_, dst) in enumerate(moves):
            if k + 1 < len(moves):
                copy(k + 1).start()
            copy(k).wait()
            wm_ref[dst:dst + n, :] = stage_ref[k % 2].astype(BF16)
        gate_rows = pltpu.make_async_copy(w_hbm.at[layer, pl.ds(Z_COL0, Z_COLS), :],
                                          stage_ref.at[0, pl.ds(0, Z_COLS), :], sem.at[0])
        gate_rows.start()
        gate_rows.wait()
        wz_ref[...] = jnp.zeros_like(wz_ref)
        wz_ref[0:Z_COLS, :] = stage_ref[0, 0:Z_COLS, :].astype(BF16)

    x = _pick_piece(x_refs, IN_TM)
    y = x * lax.rsqrt(jnp.mean(x * x, axis=-1, keepdims=True) + EPS)
    h = (y * nrm_ref[...]) * (1.0 + sc_ref[...]) + sh_ref[...]
    hb = h.astype(BF16)
    def plain_job(ref, c0, w0):
        def job():
            ref[:, c0:c0 + MXU_N] = _dot_nt(hb, wm_ref[w0:w0 + MXU_N, :])
        return job

    jobs = [plain_job(ref, c, w0 + c) for ref, w0, width in
            ((g4_ref, 0, 1024), (qb_ref, 1024, 512), (kb_ref, 1536, 512), (vb_ref, 2048, 512))
            for c in range(0, width, MXU_N)]
    z = _dot_nt(hb, wz_ref[...])
    jobs.pop(0)()
    zz = _dot(z.astype(BF16), wup_ref[...]) + bup_ref[...]
    uv = _dot_nt(hb, wm_ref[2560:3072, :])
    jobs.pop(0)()
    la_ref[...] = _log_sigmoid(zz) * (1.0 / GLA_TAU)
    jobs.pop(0)()
    oc_ref[...] = _sgu(uv, ws_ref, bs_ref, jobs)


def _resident_layer(a, l):
    return pl.BlockSpec((None,) + a.shape[1:], lambda *_: (l,) + (0,) * (a.ndim - 1), pipeline_mode=pl.Buffered(1))


def _in_proj_call(x_pieces, norm_g, mod4, w_in_t, l, w_up, b_up, sgu_ws, sgu_bs, cast_w):
    tm = IN_TM
    rider_spec, rider_shape = _cast_rider(cast_w, RIDER_BLOCKS_20)
    n = T_ALL // tm
    row = functools.partial(_mod_row, tm=tm)
    mod_spec = lambda k: pl.BlockSpec((None, None, 1, D_MODEL), lambda i: (row(i), k, 0, 0))
    full = lambda a: pl.BlockSpec(a.shape, lambda i: (0,) * a.ndim)
    out = lambda w: pl.BlockSpec((tm, w), lambda i: (i, 0))
    return pl.pallas_call(
        functools.partial(_in_proj_kernel, n_x=len(x_pieces), layer=l),
        grid=(n,),
        in_specs=_piece_specs(x_pieces, tm, D_MODEL) + [full(norm_g), mod_spec(0), mod_spec(1), pl.BlockSpec(memory_space=pl.ANY),
                                                        full(w_up), full(b_up), full(sgu_ws), full(sgu_bs),
                                                        rider_spec],
        out_specs=[out(1024), out(512), out(512), out(512), out(512), out(W_C), rider_spec],
        out_shape=[jax.ShapeDtypeStruct((T_ALL, w), F32) for w in (1024, 512, 512, 512, 512)]
        + [jax.ShapeDtypeStruct((T_ALL, W_C), BF16), rider_shape],
        scratch_shapes=[pltpu.VMEM((W_MAIN, D_MODEL), BF16), pltpu.VMEM((LANES, D_MODEL), BF16),
                        pltpu.VMEM((2, W_STAGE_ROWS, D_MODEL), F32), pltpu.SemaphoreType.DMA((2,))],
        compiler_params=_cparams(("arbitrary",)),
        name="in_proj",
    )(*x_pieces, norm_g, mod4, mod4, w_in_t, w_up, b_up, sgu_ws, sgu_bs, cast_w)


GLA_SB = 256
GLA_NC = GLA_SB // GLA_CHUNK


def _head_blocks(x, same64):
    return jnp.where(same64, jnp.concatenate([x] * N_HEADS_A, axis=0), 0.0).astype(BF16)


def _gla_superblock(q, k, vb, v_blocks, la, st_all, tri, mask4, same64, forward):
    c = GLA_CHUNK
    mid, last = (c // 2 - 1, c - 1) if forward else (c // 2, 0)
    la_hi, la_lo = _split_bf16(la)
    b = _dot(tri, la_hi) + _dot(tri, la_lo)
    yield
    rows_of = lambda r: jnp.concatenate(
        [jnp.broadcast_to(b[i * c + r:i * c + r + 1, :], (c, W_A)) for i in range(GLA_NC)], axis=0)
    m = rows_of(mid)
    bl = rows_of(last)
    qe = (q * jnp.exp(b - m)).astype(BF16)
    ke = k * jnp.exp(m - b)
    qi = (q * jnp.exp(b)).astype(BF16)
    ks = (k * jnp.exp(bl - b)).astype(BF16)
    outs = [None] * GLA_NC
    for i in (range(GLA_NC) if forward else reversed(range(GLA_NC))):
        rows = slice(i * c, (i + 1) * c)
        s = _dot_nt(qe[rows, :], _head_blocks(ke[rows, :], same64))
        kv = _dot_tn(vb[rows, :], ks[rows, :])
        yield
        a = jnp.where(mask4, s, 0.0).astype(BF16)
        outs[i] = _dot(a, v_blocks[i]) + _dot_nt(qi[rows, :], st_all.astype(BF16))
        st_all = st_all * jnp.exp(bl[i * c:i * c + 1, :]) + jnp.where(same64, kv, 0.0)
        yield
    return jnp.concatenate(outs, axis=0), st_all


def _interleave(*gens):
    results = [None] * len(gens)
    active = list(enumerate(gens))
    while active:
        for item in list(active):
            try:
                next(item[1])
            except StopIteration as stop:
                results[item[0]] = stop.value
                active.remove(item)
    return results


def _cast_rider(w, n_blocks):
    e, r, c = w.shape
    per = n_blocks // e
    rows = r // per
    assert per * e == n_blocks and rows * per == r and rows % 16 == 0

    def index(b):
        b = jnp.minimum(b, n_blocks - 1)
        return b // per, b % per, 0

    spec = pl.BlockSpec((None, rows, c), index)
    return spec, jax.ShapeDtypeStruct(w.shape, BF16)


def _gla_kernel(*refs, seq, rider):
    if rider:
        (g4_ref, la_ref, s0f_ref, s0b_ref, gain_ref, w_ref, o_ref, sf_ref, sb_ref, wb_ref, of_ref, ob_ref) = refs
        wb_ref[...] = w_ref[...].astype(BF16)
    else:
        g4_ref, la_ref, s0f_ref, s0b_ref, gain_ref, o_ref, sf_ref, sb_ref, of_ref, ob_ref = refs
    n = GLA_SB
    nsb = seq // n
    r = lax.broadcasted_iota(jnp.int32, (n, n), 0)
    s = lax.broadcasted_iota(jnp.int32, (n, n), 1)
    same64 = (r // GLA_CHUNK) == (s // GLA_CHUNK)
    lower = same64 & (s <= r)
    upper = same64 & (s >= r)
    tri_f = jnp.where(lower, 1.0, 0.0).astype(BF16)
    tri_b = jnp.where(upper, 1.0, 0.0).astype(BF16)
    ones64 = jnp.where(same64, 1.0, 0.0).astype(BF16)
    key_row = lax.broadcasted_iota(jnp.int32, (GLA_CHUNK, n), 1) % GLA_CHUNK
    qry_row = lax.broadcasted_iota(jnp.int32, (GLA_CHUNK, n), 0)
    mask4_f = key_row <= qry_row
    mask4_b = key_row >= qry_row
    scale = DK_A ** -0.5
    expand = lambda st: jnp.where(same64, jnp.concatenate([st] * N_HEADS_A, axis=0), 0.0)
    compact = lambda st_all: functools.reduce(
        lambda a, b: a + b, [st_all[h * 64:(h + 1) * 64, :] for h in range(N_HEADS_A)])

    def step(i, carry):
        stf, stb = carry
        rf = pl.ds(pl.multiple_of(i * n, n), n)
        rb = pl.ds(pl.multiple_of((nsb - 1 - i) * n, n), n)
        def direction(rows, la_cols, st, tri, mask4, forward):
            q = g4_ref[rows, 0:256] * scale
            k = g4_ref[rows, 256:512]
            v = g4_ref[rows, 512:768]
            v_blocks = [_head_blocks(v[i * GLA_CHUNK:(i + 1) * GLA_CHUNK, :], same64) for i in range(GLA_NC)]
            return _gla_superblock(q, k, v.astype(BF16), v_blocks, la_ref[rows, la_cols], st, tri, mask4, same64,
                                   forward)

        (o_f, stf), (o_b, stb) = _interleave(direction(rf, slice(0, 256), stf, tri_f, mask4_f, True),
                                             direction(rb, slice(256, 512), stb, tri_b, mask4_b, False))
        of_ref[rf, :] = o_f
        ob_ref[rb, :] = o_b
        return stf, stb

    stf, stb = lax.fori_loop(0, nsb, step, (expand(s0f_ref[...]), expand(s0b_ref[...])))
    sf_ref[...] = compact(stf)
    sb_ref[...] = compact(stb)

    gain = gain_ref[...]

    def finish(i, carry):
        rows = pl.ds(pl.multiple_of(i * n, n), n)
        o = of_ref[rows, :] + ob_ref[rows, :]
        sq_hi, sq_lo = _split_bf16(o * o)
        ms = (_dot(sq_hi, ones64) + _dot(sq_lo, ones64)) * (1.0 / DK_A)
        y = (o * lax.rsqrt(ms + EPS)) * gain
        o_ref[rows, :] = (y * _silu(g4_ref[rows, 768:1024])).astype(BF16)
        return carry

    lax.fori_loop(0, nsb, finish, 0)


def _gla_call(g4, la, s0f, s0b, gain, cast_w=None, *, batch, seq, row_block0):
    tok = lambda w: pl.BlockSpec((seq, w), lambda b: (row_block0 + b, 0))
    st = pl.BlockSpec((None, 64, 256), lambda b: (b, 0, 0))
    in_specs = [tok(1024), tok(512), st, st, pl.BlockSpec((1, W_A), lambda b: (0, 0))]
    out_specs = [pl.BlockSpec((seq, W_A), lambda b: (b, 0)), st, st]
    out_shape = [jax.ShapeDtypeStruct((batch * seq, W_A), BF16),
                 jax.ShapeDtypeStruct((batch, 64, 256), F32),
                 jax.ShapeDtypeStruct((batch, 64, 256), F32)]
    args = [g4, la, s0f, s0b, gain]
    if cast_w is not None:
        spec, shape = _cast_rider(cast_w, batch)
        in_specs.append(spec)
        out_specs.append(spec)
        out_shape.append(shape)
        args.append(cast_w)
    return pl.pallas_call(
        functools.partial(_gla_kernel, seq=seq, rider=cast_w is not None),
        grid=(batch,),
        in_specs=in_specs,
        out_specs=out_specs,
        out_shape=out_shape,
        scratch_shapes=[pltpu.VMEM((seq, W_A), F32), pltpu.VMEM((seq, W_A), F32)],
        compiler_params=_cparams(("arbitrary",)),
        name=f"gla_{seq}",
    )(*args)


def _state_to_kernel(s):
    b = s.shape[0]
    return jnp.transpose(s, (0, 3, 1, 2)).reshape(b, 64, 256)


def _state_from_kernel(st):
    b = st.shape[0]
    return jnp.transpose(st.reshape(b, 64, N_HEADS_A, DK_A), (0, 2, 3, 1))


def _lambda(lv, lam_init):
    l01 = jnp.sum(lv[0:1, :] * lv[1:2, :], axis=-1, keepdims=True)
    l23 = jnp.sum(lv[2:3, :] * lv[3:4, :], axis=-1, keepdims=True)
    return jnp.exp(l01) - jnp.exp(l23) + lam_init


def _softmax_parts(parts):
    mx = functools.reduce(jnp.maximum, [jnp.max(p, axis=-1, keepdims=True) for p in parts])
    es = [jnp.exp(p - mx) for p in parts]
    den = functools.reduce(lambda a, b: a + b, [jnp.sum(e, axis=-1, keepdims=True) for e in es])
    return [e / den for e in es]


def _diff_finish(o, gain, lam_init):
    o = o * lax.rsqrt(jnp.mean(o * o, axis=-1, keepdims=True) + EPS)
    return ((o * gain) * (1.0 - lam_init)).astype(BF16)


QK_SCALE = DH_B ** -0.5


def _key_halves(k):
    first = lax.broadcasted_iota(jnp.int32, k.shape, 1) < DH_B
    return jnp.where(first, k, 0.0).astype(BF16), jnp.where(first, 0.0, k).astype(BF16)


def _attn_prompt_kernel(lv_ref, q_ref, k_ref, v_ref, gain_ref, *rest, lam_init, n_prev, rider):
    rest = list(rest)
    if rider:
        wb_ref = rest.pop()
        w_ref = rest.pop(2 * n_prev)
        wb_ref[...] = w_ref[...].astype(BF16)
    prev_refs, (o_ref, *cache_refs) = rest[:2 * n_prev], rest[2 * n_prev:]
    lam = _lambda(lv_ref[...], lam_init)

    def head(h):
        cols = slice(h * DV_B, (h + 1) * DV_B)
        q = (q_ref[:, cols] * QK_SCALE).astype(BF16)
        k1, k2 = _key_halves(k_ref[:, cols])
        s1 = _dot_nt(q, k1)
        s2 = _dot_nt(q, k2)
        yield
        (p1,) = _softmax_parts([s1])
        (p2,) = _softmax_parts([s2])
        a = p1 - lam * p2
        o = _dot(a.astype(BF16), v_ref[:, cols].astype(BF16))
        yield
        o_ref[:, cols] = _diff_finish(o, gain_ref[:, cols], lam_init)
        if cache_refs:
            ck_ref, cv_ref = cache_refs
            layers_k = [*prev_refs[0::2], k_ref]
            layers_v = [*prev_refs[1::2], v_ref]
            for l in range(n_prev + 1):
                ck_ref[l, pl.ds(h, SEQ, stride=N_HEADS_B), :] = layers_k[l][:, cols]
                cv_ref[l, pl.ds(h, SEQ, stride=N_HEADS_B), :] = layers_v[l][:, cols]

    _interleave(*[head(h) for h in range(N_HEADS_B)])


def _attn_prompt_call(lv, qb, kb, vb, gain, prev_kv=(), cast_w=None, *, lam_init, write_cache=False):
    blk = pl.BlockSpec((SEQ, W_B), lambda b: (b, 0))
    n_prev = len(prev_kv)
    assert write_cache or not prev_kv
    in_specs = [pl.BlockSpec((4, DH_B), lambda b: (0, 0)), blk, blk, blk,
                pl.BlockSpec((1, W_B), lambda b: (0, 0))] + [blk] * (2 * n_prev)
    args = [lv, qb, kb, vb, gain, *[a for kv in prev_kv for a in kv]]
    out_specs = [blk]
    out_shape = [jax.ShapeDtypeStruct((T_PROMPT, W_B), BF16)]
    if write_cache:
        cache = jax.ShapeDtypeStruct((BATCH, n_prev + 1, SEQ * N_HEADS_B, DV_B), F32)
        cblk = pl.BlockSpec((None, n_prev + 1, SEQ * N_HEADS_B, DV_B), lambda b: (b, 0, 0, 0))
        out_specs += [cblk, cblk]
        out_shape += [cache, cache]
    if cast_w is not None:
        spec, shape = _cast_rider(cast_w, BATCH)
        in_specs.append(spec)
        out_specs.append(spec)
        out_shape.append(shape)
        args.append(cast_w)
    return pl.pallas_call(
        functools.partial(_attn_prompt_kernel, lam_init=lam_init, n_prev=n_prev, rider=cast_w is not None),
        grid=(BATCH,),
        in_specs=in_specs,
        out_specs=out_specs,
        out_shape=out_shape,
        compiler_params=_cparams(("arbitrary",)),
        name="diff_attn_prompt",
    )(*args)


def _rope(x, cos, sin_signed):
    lane = lax.broadcasted_iota(jnp.int32, x.shape, 1)
    first = (lane % (2 * AXIS_PAIRS)) < AXIS_PAIRS
    partner = jnp.where(first, pltpu.roll(x, LANES - AXIS_PAIRS, 1), pltpu.roll(x, AXIS_PAIRS, 1))
    return x * cos + partner * sin_signed


ATT_TQ = 256


def _attn_sample_kernel(lv_ref, q_ref, k_ref, v_ref, kc_ref, vc_ref, cosq_ref, sinq_ref,
                        cosk_ref, sink_ref, gain_ref, *rest, lam_init, n_cast):
    w_refs, o_ref, wb_refs, (k1_ref, k2_ref) = rest[:n_cast], rest[n_cast], rest[n_cast + 1:2 * n_cast + 1], rest[-2:]
    for w_ref, wb_ref in zip(w_refs, wb_refs):
        wb_ref[...] = w_ref[...].astype(BF16)

    @pl.when(pl.program_id(1) == 0)
    def _():
        for h in range(N_HEADS_B):
            cols = slice(h * DV_B, (h + 1) * DV_B)
            k1_ref[:, cols], k2_ref[:, cols] = _key_halves(_rope(k_ref[:, cols], cosk_ref[...], sink_ref[...]))

    lam = _lambda(lv_ref[...], lam_init)

    def head(h):
        cols = slice(h * DV_B, (h + 1) * DV_B)
        q = (_rope(q_ref[:, cols], cosq_ref[...], sinq_ref[...]) * QK_SCALE).astype(BF16)
        c1, c2 = _key_halves(kc_ref[:, cols])
        s1 = [_dot_nt(q, k1_ref[:, cols]), _dot_nt(q, c1)]
        s2 = [_dot_nt(q, k2_ref[:, cols]), _dot_nt(q, c2)]
        yield
        p1 = _softmax_parts(s1)
        p2 = _softmax_parts(s2)
        a_own = p1[0] - lam * p2[0]
        a_ctx = p1[1] - lam * p2[1]
        o = (_dot(a_own.astype(BF16), v_ref[:, cols].astype(BF16))
             + _dot(a_ctx.astype(BF16), vc_ref[:, cols].astype(BF16)))
        yield
        o_ref[:, cols] = _diff_finish(o, gain_ref[:, cols], lam_init)

    _interleave(*[head(h) for h in range(N_HEADS_B)])


def _attn_sample_call(lv, qb, kb, vb, kc, vc, cos, sin_signed, gain, cast_ws=(), *, lam_init):
    tq = ATT_TQ
    nq = DEC_SEQ // tq
    p0 = T_PROMPT // tq
    s0 = T_PROMPT // DEC_SEQ
    qblk = pl.BlockSpec((tq, W_B), lambda b, t: (p0 + b * nq + t, 0))
    kvblk = pl.BlockSpec((DEC_SEQ, W_B), lambda b, t: (s0 + b, 0))
    cblk = pl.BlockSpec((None, PAST_LEN, W_B), lambda b, t: (b, 0, 0))
    n_steps = DEC_BATCH * nq
    cast_specs = []
    for w in cast_ws:
        assert w.shape[0] % (16 * n_steps) == 0
        cast_specs.append(pl.BlockSpec((w.shape[0] // n_steps, w.shape[1]), lambda b, t: (b * nq + t, 0)))
    return pl.pallas_call(
        functools.partial(_attn_sample_kernel, lam_init=lam_init, n_cast=len(cast_ws)),
        grid=(DEC_BATCH, nq),
        in_specs=[pl.BlockSpec((4, DH_B), lambda b, t: (0, 0)), qblk, kvblk, kvblk, cblk, cblk,
                  pl.BlockSpec((tq, DV_B), lambda b, t: (t, 0)),
                  pl.BlockSpec((tq, DV_B), lambda b, t: (t, 0)),
                  pl.BlockSpec((DEC_SEQ, DV_B), lambda b, t: (0, 0)),
                  pl.BlockSpec((DEC_SEQ, DV_B), lambda b, t: (0, 0)),
                  pl.BlockSpec((1, W_B), lambda b, t: (0, 0))] + cast_specs,
        out_specs=[pl.BlockSpec((tq, W_B), lambda b, t: (b * nq + t, 0))] + cast_specs,
        out_shape=[jax.ShapeDtypeStruct((T_SAMPLE, W_B), BF16)] + [jax.ShapeDtypeStruct(w.shape, BF16) for w in cast_ws],
        scratch_shapes=[pltpu.VMEM((DEC_SEQ, W_B), BF16), pltpu.VMEM((DEC_SEQ, W_B), BF16)],
        compiler_params=_cparams(("arbitrary", "arbitrary")),
        name="diff_attn_sample",
    )(lv, qb, kb, vb, kc, vc, cos, sin_signed, cos, sin_signed, gain, *cast_ws)


def _rope_tables():
    rows = DEC_SEQ // GRID_W
    row = jnp.repeat(jnp.arange(rows, dtype=F32), GRID_W)
    col = jnp.tile(jnp.arange(GRID_W, dtype=F32), rows)
    freqs = ROPE_THETA ** (-jnp.arange(AXIS_PAIRS, dtype=F32) / AXIS_PAIRS)
    ar, ac = row[:, None] * freqs, col[:, None] * freqs
    cos64 = jnp.concatenate([jnp.cos(ar), jnp.cos(ar), jnp.cos(ac), jnp.cos(ac)], axis=-1)
    sin64 = jnp.concatenate([-jnp.sin(ar), jnp.sin(ar), -jnp.sin(ac), jnp.sin(ac)], axis=-1)
    return jnp.tile(cos64, (1, 2)), jnp.tile(sin64, (1, 2))


def _group_mean(x, ones64):
    hi, lo = _split_bf16(x)
    return (_dot(hi, ones64) + _dot(lo, ones64)) * (1.0 / DG_C)


def _sgu(uv, ws_ref, bs_ref, fillers=()):
    fillers = list(fillers)
    fill = lambda: fillers.pop(0)() if fillers else None
    r = lax.broadcasted_iota(jnp.int32, (W_C, W_C), 0)
    s = lax.broadcasted_iota(jnp.int32, (W_C, W_C), 1)
    ones64 = jnp.where((r // DG_C) == (s // DG_C), 1.0, 0.0).astype(BF16)
    lane = lax.broadcasted_iota(jnp.int32, (SGU_CHUNK, W_C), 1)
    outs = []
    for n in range(uv.shape[0] // SGU_CHUNK):
        rows = slice(n * SGU_CHUNK, (n + 1) * SGU_CHUNK)
        u = _gelu_tanh(uv[rows, 0:256])
        v = _gelu_tanh(uv[rows, 256:512])
        mu = _group_mean(v, ones64)
        fill()
        d = v - mu
        var = _group_mean(d * d, ones64)
        fill()
        vn = d * lax.rsqrt(var + EPS)
        s_mix = bs_ref[...]
        for g in range(N_GROUPS_C):
            vn_g = jnp.where((lane // DG_C) == g, vn, 0.0).astype(BF16)
            s_mix = s_mix + _dot(ws_ref[g], vn_g)
        fill()
        outs.append((u * s_mix).astype(BF16))
    while fillers:
        fill()
    return jnp.concatenate(outs, axis=0)


OUT_TM = 1024
OUT_SLABS = 4


SEL_LANE0 = N_EXPERTS


def _top2_gates(logits):
    lane = lax.broadcasted_iota(jnp.int32, logits.shape, 1).astype(F32)
    neg = -jnp.inf
    lg = jnp.where(lane < N_EXPERTS, logits, neg)
    m1 = jnp.max(lg, axis=-1, keepdims=True)
    i1 = jnp.min(jnp.where(lg == m1, lane, float(LANES)), axis=-1, keepdims=True)
    lg2 = jnp.where(lane == i1, neg, lg)
    m2 = jnp.max(lg2, axis=-1, keepdims=True)
    i2 = jnp.min(jnp.where(lg2 == m2, lane, float(LANES)), axis=-1, keepdims=True)
    e2 = jnp.exp(m2 - m1)
    den = 1.0 + e2
    gates = jnp.where(lane == i1, 1.0 / den, 0.0) + jnp.where(lane == i2, e2 / den, 0.0)
    sel = jnp.where((lane == i1 + SEL_LANE0) | (lane == i2 + SEL_LANE0), 1.0, 0.0)
    return gates + sel


def _store_token_tiles(ref, val, r0=0):
    n = val.shape[0]
    for k in range(D_MODEL // LANES):
        ref[pl.ds(8 * r0 + k, n, stride=8), :] = val[:, k * LANES:(k + 1) * LANES]


def _load_token_tiles(ref, n):
    return jnp.concatenate([ref[pl.ds(k, n, stride=8), :] for k in range(D_MODEL // LANES)], axis=-1)


def _out_proj_kernel(*refs, n_x, moe):
    oa_refs, ob_refs, (oc_ref,), x_refs = refs[0:2], refs[2:4], refs[4:5], refs[5:5 + n_x]
    w_ref, g1_ref, nrm_ref, sc_ref, sh_ref, *rest = refs[5 + n_x:]
    if moe:
        rw_ref, x1_ref, h2t_ref, gates_ref, gates_t_ref, wo_ref = rest
    else:
        x1_ref, h2_ref, wo_ref = rest

    @pl.when(pl.program_id(0) == 0)
    def _():
        wo_ref[...] = w_ref[...].astype(BF16)

    if moe:
        rw_hi, rw_lo = _split_bf16(rw_ref[...])

    def rows_chain(r0, n):
        rs = slice(r0, r0 + n)
        y = (_dot(_pick_piece(oa_refs, OUT_TM, rs), wo_ref[0:256, :])
             + _dot(_pick_piece(ob_refs, OUT_TM, rs), wo_ref[256:768, :]) + _dot(oc_ref[rs, :], wo_ref[768:1024, :]))
        yield
        x1 = _pick_piece(x_refs, OUT_TM, rs) + g1_ref[...] * y
        x1_ref[rs, :] = x1
        yn = x1 * lax.rsqrt(jnp.mean(x1 * x1, axis=-1, keepdims=True) + EPS)
        h = (yn * nrm_ref[...]) * (1.0 + sc_ref[...]) + sh_ref[...]
        if moe:
            _store_token_tiles(h2t_ref, h, r0)
            h_hi, h_lo = _split_bf16(h)
            logits = _dot(h_hi, rw_hi) + (_dot(h_lo, rw_hi) + _dot(h_hi, rw_lo))
            yield
            gates = _top2_gates(logits)
            gates_ref[rs, :] = gates
            gates_t_ref[:, rs] = gates.T
        else:
            h2_ref[rs, :] = h.astype(BF16)

    n_slab = OUT_TM // OUT_SLABS
    _interleave(*[rows_chain(i * n_slab, n_slab) for i in range(OUT_SLABS)])


def _out_proj_call(oa_pieces, ob_pieces, oc, x_pieces, w_out, l, mod4, norm_g, router_w=None):
    tm = OUT_TM
    n = T_ALL // tm
    moe = router_w is not None
    row = functools.partial(_mod_row, tm=tm)
    mod_spec = lambda k: pl.BlockSpec((None, None, 1, D_MODEL), lambda i: (row(i), k, 0, 0))
    tok = lambda w: pl.BlockSpec((tm, w), lambda i: (i, 0))
    full = lambda a: pl.BlockSpec(a.shape, lambda i: (0,) * a.ndim)
    in_specs = (_piece_specs(oa_pieces, tm, W_A) + _piece_specs(ob_pieces, tm, W_B) + [tok(W_C)]
                + _piece_specs(x_pieces, tm, D_MODEL)
                + [_resident_layer(w_out, l), mod_spec(2), full(norm_g), mod_spec(4), mod_spec(3)])
    args = [*oa_pieces, *ob_pieces, oc, *x_pieces, w_out, mod4, norm_g, mod4, mod4]
    if moe:
        in_specs.append(full(router_w))
        args.append(router_w)
        out_specs = [tok(D_MODEL), pl.BlockSpec((tm * 8, LANES), lambda i: (i, 0)), tok(LANES),
                     pl.BlockSpec((LANES, tm), lambda i: (0, i))]
        out_shape = [jax.ShapeDtypeStruct((T_ALL, D_MODEL), F32), jax.ShapeDtypeStruct((T_ALL * 8, LANES), F32),
                     jax.ShapeDtypeStruct((T_ALL, LANES), F32), jax.ShapeDtypeStruct((LANES, T_ALL), F32)]
    else:
        out_specs = [tok(D_MODEL), tok(D_MODEL)]
        out_shape = [jax.ShapeDtypeStruct((T_ALL, D_MODEL), F32), jax.ShapeDtypeStruct((T_ALL, D_MODEL), BF16)]
    return pl.pallas_call(
        functools.partial(_out_proj_kernel, n_x=len(x_pieces), moe=moe),
        grid=(n,),
        in_specs=in_specs,
        out_specs=out_specs,
        out_shape=out_shape,
        scratch_shapes=[pltpu.VMEM((D_MODEL, D_MODEL), BF16)],
        compiler_params=_cparams(("arbitrary",)),
        name="out_proj_moe" if moe else "out_proj",
    )(*args)


FFN_TM = 512
MXU_N = 256
FFN_SPLITS = (0, 1024, 2048, D_FF)
assert all(s % MXU_N == 0 for s in FFN_SPLITS)


def _swiglu(h, wg_ref, wu_ref, wd_ref):
    out = None
    for c0, c1 in zip(FFN_SPLITS[:-1], FFN_SPLITS[1:]):
        act = _silu(_dot(h, wg_ref[:, c0:c1])) * _dot(h, wu_ref[:, c0:c1])
        d = _dot(act.astype(BF16), wd_ref[c0:c1, :])
        out = d if out is None else out + d
    return out


def _ffn_kernel(h_ref, x_ref, g2_ref, wg_ref, wu_ref, wd_ref, w_ref, o_ref, wb_ref):
    @pl.when(pl.program_id(0) < RIDER_BLOCKS_20)
    def _():
        wb_ref[...] = w_ref[...].astype(BF16)

    o_ref[...] = x_ref[...] + g2_ref[...] * _swiglu(h_ref[...], wg_ref, wu_ref, wd_ref)


def _ffn_call(h2, x1, mod4, wg, wu, wd, cast_w):
    tm = FFN_TM
    row = functools.partial(_mod_row, tm=tm)
    tok = lambda w: pl.BlockSpec((tm, w), lambda i: (i, 0))
    resident = lambda a: pl.BlockSpec(a.shape, lambda i: (0, 0), pipeline_mode=pl.Buffered(1))
    rider_spec, rider_shape = _cast_rider(cast_w, RIDER_BLOCKS_20)
    return pl.pallas_call(
        _ffn_kernel,
        grid=(T_ALL // tm,),
        in_specs=[tok(D_MODEL), tok(D_MODEL),
                  pl.BlockSpec((None, None, 1, D_MODEL), lambda i: (row(i), 5, 0, 0)),
                  resident(wg), resident(wu), resident(wd), rider_spec],
        out_specs=[tok(D_MODEL), rider_spec],
        out_shape=[jax.ShapeDtypeStruct((T_ALL, D_MODEL), F32), rider_shape],
        compiler_params=_cparams(("arbitrary",)),
        name="ffn_dense",
    )(h2, x1, mod4, wg, wu, wd, cast_w)


MOE_TM = 512
MOE_NT_MAX = (2 * T_ALL) // MOE_TM + N_EXPERTS
MOE_ROWS = MOE_NT_MAX * MOE_TM
PLAN_BLK = 512
MISC_LAST_START = 8
MISC_NT = 16


def _moe_plan_kernel(gt_ref, posa_ref, posb_ref, te_ref, ti_ref, misc_ref):
    tm = float(MOE_TM)
    sel = gt_ref[SEL_LANE0:SEL_LANE0 + N_EXPERTS, :]
    cnt = jnp.sum(sel, axis=1, keepdims=True)
    nt = jnp.floor((cnt + (tm - 1.0)) * (1.0 / tm))
    sub = lax.broadcasted_iota(jnp.int32, (N_EXPERTS, LANES), 0).astype(F32)
    lane = lax.broadcasted_iota(jnp.int32, (N_EXPERTS, LANES), 1).astype(F32)
    nt_b = jnp.broadcast_to(nt, (N_EXPERTS, LANES))
    nt_row = jnp.sum(jnp.where(sub == lane, nt_b, 0.0), axis=0, keepdims=True)
    toff = jnp.sum(jnp.where(lane < sub, jnp.broadcast_to(nt_row, (N_EXPERTS, LANES)), 0.0),
                   axis=1, keepdims=True)
    tend = toff + nt
    n_total = jnp.sum(nt, axis=0, keepdims=True)
    jc = jnp.minimum(lane, n_total - 1.0)
    te = jnp.sum(jnp.where(jc >= tend, 1.0, 0.0), axis=0, keepdims=True)
    te_ref[...] = te.astype(jnp.int32)
    ti_ref[...] = jc[0:1, :].astype(jnp.int32)
    last_start = (tend - 1.0) * tm
    ls_row = jnp.sum(jnp.where(sub + MISC_LAST_START == lane, jnp.broadcast_to(last_start, (N_EXPERTS, LANES)), 0.0),
                     axis=0, keepdims=True)
    nt_row2 = jnp.sum(jnp.where(sub + MISC_NT == lane, nt_b, 0.0), axis=0, keepdims=True)
    misc = jnp.where(lane[0:1, :] == 0.0, n_total, 0.0) + ls_row + nt_row2
    misc_ref[...] = misc.astype(jnp.int32)

    off = toff * tm
    r = lax.broadcasted_iota(jnp.int32, (PLAN_BLK, PLAN_BLK), 0)
    c = lax.broadcasted_iota(jnp.int32, (PLAN_BLK, PLAN_BLK), 1)
    upper = jnp.where(r <= c, 1.0, 0.0).astype(BF16)
    carry = jnp.zeros((N_EXPERTS, 1), F32)
    for blk in range(T_ALL // PLAN_BLK):
        cols = slice(blk * PLAN_BLK, (blk + 1) * PLAN_BLK)
        s = gt_ref[SEL_LANE0:SEL_LANE0 + N_EXPERTS, cols]
        rank = _dot(s.astype(BF16), upper) + carry
        pos = off + rank - 1.0
        posa_ref[:, cols] = jnp.min(jnp.where(s > 0.0, pos, 1e9), axis=0, keepdims=True).astype(jnp.int32)
        posb_ref[:, cols] = jnp.max(jnp.where(s > 0.0, pos, -1.0), axis=0, keepdims=True).astype(jnp.int32)
        carry = carry + jnp.sum(s, axis=1, keepdims=True)


def _moe_plan_call(gates_t):
    row = lambda w: jax.ShapeDtypeStruct((1, w), jnp.int32)
    full = lambda w: pl.BlockSpec((1, w), lambda: (0, 0))
    return pl.pallas_call(
        _moe_plan_kernel,
        in_specs=[pl.BlockSpec((LANES, T_ALL), lambda: (0, 0))],
        out_specs=[full(T_ALL), full(T_ALL), full(LANES), full(LANES), full(LANES)],
        out_shape=[row(T_ALL), row(T_ALL), row(LANES), row(LANES), row(LANES)],
        compiler_params=pltpu.CompilerParams(vmem_limit_bytes=VMEM_LIMIT),
        name="moe_plan",
    )(gates_t)


DMA_UNROLL = 8


def _row_tile(ref, row):
    return ref.at[pl.ds(pl.multiple_of(row * 8, 8), 8), :]


def _moe_scatter_kernel(misc_ref, posa_ref, posb_ref, h_ref, xs_ref, zero_ref, sem):
    tm = h_ref.shape[0] // 8

    @pl.when(pl.program_id(0) == 0)
    def _():
        zero_ref[...] = jnp.zeros_like(zero_ref)

        def zero_tile(first_row):
            start = pl.multiple_of(first_row * 8, 8)
            cp = pltpu.make_async_copy(zero_ref, xs_ref.at[pl.ds(start, MOE_TM * 8), :], sem.at[0])
            cp.start()
            cp.wait()

        for e in range(N_EXPERTS):
            @pl.when(misc_ref[0, MISC_NT + e] > 0)
            def _():
                zero_tile(misc_ref[0, MISC_LAST_START + e])

        def zero_tail(j, carry):
            zero_tile(j * MOE_TM)
            return carry

        lax.fori_loop(misc_ref[0, 0], MOE_NT_MAX, zero_tail, 0)

    def issue(r, carry):
        src = _row_tile(h_ref, r)
        pltpu.make_async_copy(src, _row_tile(xs_ref, posa_ref[0, r]), sem.at[0]).start(priority=0)
        pltpu.make_async_copy(src, _row_tile(xs_ref, posb_ref[0, r]), sem.at[1]).start(priority=1)
        return carry

    lax.fori_loop(0, tm, issue, 0, unroll=DMA_UNROLL)
    for k in range(2):
        pltpu.make_async_copy(h_ref, xs_ref.at[pl.ds(0, tm * 8), :], sem.at[k]).wait()


SCATTER_TM = 512


def _moe_scatter_call(misc, posa3, posb3, h2t):
    tm = SCATTER_TM
    smem_row = pl.BlockSpec((None, 1, tm), lambda i: (i, 0, 0), memory_space=pltpu.SMEM)
    return pl.pallas_call(
        _moe_scatter_kernel,
        grid=(T_ALL // tm,),
        in_specs=[pl.BlockSpec((1, LANES), lambda i: (0, 0), memory_space=pltpu.SMEM), smem_row, smem_row,
                  pl.BlockSpec((tm * 8, LANES), lambda i: (i, 0))],
        out_specs=pl.BlockSpec(memory_space=pl.ANY),
        out_shape=jax.ShapeDtypeStruct((MOE_ROWS * 8, LANES), F32),
        scratch_shapes=[pltpu.VMEM((MOE_TM * 8, LANES), F32), pltpu.SemaphoreType.DMA((2,))],
        compiler_params=_cparams(("arbitrary",)),
        name="moe_scatter",
    )(misc, posa3, posb3, h2t)


def _ffn_grouped_kernel(te_ref, ti_ref, misc_ref, x_ref, wg_ref, wu_ref, wd_ref, o_ref):
    j = pl.program_id(0)

    @pl.when(j < misc_ref[0])
    def _():
        h = _load_token_tiles(x_ref, MOE_TM).astype(BF16)
        _store_token_tiles(o_ref, _swiglu(h, wg_ref, wu_ref, wd_ref))

    @pl.when(j >= misc_ref[0])
    def _():
        o_ref[...] = jnp.zeros_like(o_ref)


def _ffn_grouped_call(te, ti, misc, xs, wg, wu, wd):
    expert = lambda *s: pl.BlockSpec((None,) + s, lambda j, te, ti, misc: (te[j], 0, 0))
    grid_spec = pltpu.PrefetchScalarGridSpec(
        num_scalar_prefetch=3,
        grid=(MOE_NT_MAX,),
        in_specs=[pl.BlockSpec((MOE_TM * 8, LANES), lambda j, te, ti, misc: (ti[j], 0)),
                  expert(D_MODEL, D_FF), expert(D_MODEL, D_FF), expert(D_FF, D_MODEL)],
        out_specs=pl.BlockSpec((MOE_TM * 8, LANES), lambda j, te, ti, misc: (j, 0)),
    )
    return pl.pallas_call(
        _ffn_grouped_kernel,
        grid_spec=grid_spec,
        out_shape=jax.ShapeDtypeStruct((MOE_ROWS * 8, LANES), F32),
        compiler_params=_cparams(("arbitrary",)),
        name="ffn_grouped",
    )(te, ti, misc, xs, wg, wu, wd)


COMBINE_TM = 256


def _moe_combine_kernel(posa_ref, posb_ref, posa_next_ref, posb_next_ref, ys_ref, x_ref, g2_ref, gates_ref, nf_ref,
                        op_ref, os_ref, bufa_ref, bufb_ref, sem):
    tm = COMBINE_TM
    i = pl.program_id(0)
    n = pl.num_programs(0)
    slot = i % 2

    def gather(pa_ref, pb_ref, s):
        def issue(r, carry):
            pltpu.make_async_copy(_row_tile(ys_ref, pa_ref[0, r]), _row_tile(bufa_ref.at[s], r),
                                  sem.at[s, 0]).start(priority=0)
            pltpu.make_async_copy(_row_tile(ys_ref, pb_ref[0, r]), _row_tile(bufb_ref.at[s], r),
                                  sem.at[s, 1]).start(priority=1)
            return carry

        lax.fori_loop(0, tm, issue, 0, unroll=DMA_UNROLL)

    @pl.when(i == 0)
    def _():
        gather(posa_ref, posb_ref, 0)

    @pl.when(i + 1 < n)
    def _():
        gather(posa_next_ref, posb_next_ref, 1 - slot)

    gates = gates_ref[...]
    lane = lax.broadcasted_iota(jnp.int32, gates.shape, 1).astype(F32)
    is_sel = (lane >= SEL_LANE0) & (lane < SEL_LANE0 + N_EXPERTS) & (gates > 0.0)
    ia = jnp.min(jnp.where(is_sel, lane, float(LANES)), axis=-1, keepdims=True) - SEL_LANE0
    ib = jnp.max(jnp.where(is_sel, lane, -1.0), axis=-1, keepdims=True) - SEL_LANE0
    wa = jnp.sum(jnp.where(lane == ia, gates, 0.0), axis=-1, keepdims=True)
    wb = jnp.sum(jnp.where(lane == ib, gates, 0.0), axis=-1, keepdims=True)

    pltpu.make_async_copy(ys_ref.at[pl.ds(0, tm * 8), :], bufa_ref.at[slot], sem.at[slot, 0]).wait()
    pltpu.make_async_copy(ys_ref.at[pl.ds(0, tm * 8), :], bufb_ref.at[slot], sem.at[slot, 1]).wait()

    y = wa * _load_token_tiles(bufa_ref.at[slot], tm) + wb * _load_token_tiles(bufb_ref.at[slot], tm)
    out = x_ref[...] + g2_ref[...] * y
    out = (out * lax.rsqrt(jnp.mean(out * out, axis=-1, keepdims=True) + EPS)) * nf_ref[...]

    @pl.when(i < T_PROMPT // tm)
    def _():
        op_ref[...] = out

    @pl.when(i >= T_PROMPT // tm)
    def _():
        os_ref[...] = out


def _moe_combine_call(posa3, posb3, ys, x1, mod4, gates, norm_f):
    tm = COMBINE_TM
    n = T_ALL // tm
    n_p = T_PROMPT // tm
    row = functools.partial(_mod_row, tm=tm)
    smem_row = pl.BlockSpec((None, 1, tm), lambda i: (i, 0, 0), memory_space=pltpu.SMEM)
    smem_next = pl.BlockSpec((None, 1, tm), lambda i: (jnp.minimum(i + 1, n - 1), 0, 0), memory_space=pltpu.SMEM)
    tok = lambda w: pl.BlockSpec((tm, w), lambda i: (i, 0))
    return pl.pallas_call(
        _moe_combine_kernel,
        grid=(n,),
        in_specs=[smem_row, smem_row, smem_next, smem_next, pl.BlockSpec(memory_space=pl.ANY), tok(D_MODEL),
                  pl.BlockSpec((None, None, 1, D_MODEL), lambda i: (row(i), 5, 0, 0)), tok(LANES),
                  pl.BlockSpec((1, D_MODEL), lambda i: (0, 0))],
        out_specs=[pl.BlockSpec((tm, D_MODEL), lambda i: (jnp.minimum(i, n_p - 1), 0)),
                   pl.BlockSpec((tm, D_MODEL), lambda i: (jnp.maximum(i - n_p, 0), 0))],
        out_shape=[jax.ShapeDtypeStruct((T_PROMPT, D_MODEL), F32), jax.ShapeDtypeStruct((T_SAMPLE, D_MODEL), F32)],
        scratch_shapes=[pltpu.VMEM((2, tm * 8, LANES), F32), pltpu.VMEM((2, tm * 8, LANES), F32),
                        pltpu.SemaphoreType.DMA((2, 2))],
        compiler_params=_cparams(("arbitrary",)),
        name="moe_combine",
    )(posa3, posb3, posa3, posb3, ys, x1, mod4, gates, norm_f)


def _moe_call(h2t, x1, mod4, gates, gates_t, wg, wu, wd, norm_f):
    posa, posb, te, ti, misc = _moe_plan_call(gates_t)
    xs = _moe_scatter_call(misc, posa.reshape(T_ALL // SCATTER_TM, 1, SCATTER_TM),
                           posb.reshape(T_ALL // SCATTER_TM, 1, SCATTER_TM), h2t)
    ys = _ffn_grouped_call(te.reshape(LANES), ti.reshape(LANES), misc.reshape(LANES), xs, wg, wu, wd)
    return _moe_combine_call(posa.reshape(T_ALL // COMBINE_TM, 1, COMBINE_TM),
                             posb.reshape(T_ALL // COMBINE_TM, 1, COMBINE_TM), ys, x1, mod4, gates, norm_f)


def kernel(x_prompt, x_sample, cache_k, cache_v, state_gla_fwd, state_gla_bwd, c, c_ctx, w_ada, b_ada, norm_mix, norm_ffn, w_in, w_out, gla_w_up, gla_b_up, gla_norm, diff_lambda, diff_norm, sgu_w, sgu_b, ffn_w_gate, ffn_w_up, ffn_w_down, router_w, moe_w_gate, moe_w_up, moe_w_down, norm_f):
    assert DEPTH == 2
    x_pieces = [x_prompt.reshape(T_PROMPT, D_MODEL), x_sample.reshape(T_SAMPLE, D_MODEL)]
    cvecs = jnp.concatenate([c_ctx[None, :], c, jnp.zeros((N_MOD_ROWS - 1 - DEC_BATCH, D_MODEL), F32)], axis=0)
    mod = _ada_call(cvecs, w_ada, b_ada)
    cos, sin_signed = _rope_tables()
    zeros_state = jnp.zeros((BATCH, 64, 256), F32)

    w_in_t = jnp.swapaxes(w_in, 1, 2)
    moe_w = [moe_w_gate[0], moe_w_up[0], moe_w_down[0]]
    prev_kv, sfs, sbs = [], [], []
    for l in range(DEPTH):
        mod4 = mod[l].reshape(N_MOD_ROWS, 6, 1, D_MODEL)
        w_up = jnp.zeros((LANES, 2 * W_A), F32)
        w_up = w_up.at[0:GLA_RANK, 0:W_A].set(gla_w_up[l, 0]).at[GLA_RANK:2 * GLA_RANK, W_A:].set(gla_w_up[l, 1])
        b_up = gla_b_up[l].reshape(1, 2 * W_A)
        bs_full = jnp.repeat(sgu_b[l].T, DG_C, axis=1)
        g4, la, qb, kb, vb, oc, moe_w[l] = _in_proj_call(x_pieces, norm_mix[l][None, :], mod4, w_in_t, l,
                                                         w_up.astype(BF16), b_up, sgu_w[l].astype(BF16), bs_full,
                                                         moe_w[l])

        gain_a = gla_norm[l][None, :]
        oa_p, sf, sb = _gla_call(g4, la, zeros_state, zeros_state, gain_a, batch=BATCH, seq=SEQ, row_block0=0)
        oa_s, _, _ = _gla_call(g4, la, _state_to_kernel(state_gla_fwd[:, l]),
                               _state_to_kernel(state_gla_bwd[:, l]), gain_a,
                               batch=DEC_BATCH, seq=DEC_SEQ, row_block0=T_PROMPT // DEC_SEQ)

        lam_init = 0.8 - 0.6 * math.exp(-0.3 * l)
        gain_b = diff_norm[l][None, :]
        if l < DEPTH - 1:
            (ob_p,) = _attn_prompt_call(diff_lambda[l], qb, kb, vb, gain_b, lam_init=lam_init)
            prev_kv.append((kb, vb))
        else:
            ob_p, new_cache_k, new_cache_v = _attn_prompt_call(diff_lambda[l], qb, kb, vb, gain_b, prev_kv,
                                                               lam_init=lam_init, write_cache=True)
        ob_s, *dense_w = _attn_sample_call(diff_lambda[l], qb, kb, vb,
                                           cache_k[:, l].reshape(DEC_BATCH, PAST_LEN, W_B),
                                           cache_v[:, l].reshape(DEC_BATCH, PAST_LEN, W_B),
                                           cos, sin_signed, gain_b,
                                           (ffn_w_gate[0], ffn_w_up[0], ffn_w_down[0]) if l == 0 else (),
                                           lam_init=lam_init)

        if l == 0:
            x1, h2 = _out_proj_call([oa_p, oa_s], [ob_p, ob_s], oc, x_pieces, w_out, l, mod4, norm_ffn[l][None, :])
            x_next, moe_w[2] = _ffn_call(h2, x1, mod4, *dense_w, moe_w[2])
            x_pieces = [x_next]
        else:
            rw = jnp.pad(router_w[0], ((0, 0), (0, LANES - N_EXPERTS)))
            x1, h2t, gates, gates_t = _out_proj_call([oa_p, oa_s], [ob_p, ob_s], oc, x_pieces, w_out, l, mod4,
                                                     norm_ffn[l][None, :], rw)
            y_prompt, y_sample = _moe_call(h2t, x1, mod4, gates, gates_t, *moe_w, norm_f[None, :])

        sfs.append(_state_from_kernel(sf))
        sbs.append(_state_from_kernel(sb))

    return (y_prompt.reshape(BATCH, SEQ, D_MODEL), y_sample.reshape(DEC_BATCH, DEC_SEQ, D_MODEL),
            new_cache_k.reshape(BATCH, DEPTH, SEQ, N_HEADS_B, DV_B), new_cache_v.reshape(BATCH, DEPTH, SEQ, N_HEADS_B, DV_B),
            jnp.stack(sfs, axis=1), jnp.stack(sbs, axis=1))
```

```python
import functools
import math

import jax
import jax.numpy as jnp
import numpy as np
from jax import lax
from jax.experimental import pallas as pl
from jax.experimental.pallas import tpu as pltpu

F32 = jnp.float32
BF16 = jnp.bfloat16

D_MODEL = 1024
BATCH = 32
SEQ = 256
DEPTH = 2
DEC_BATCH = 2
DEC_SEQ = 1024
PAST_LEN = 256
GRID_W = 64
N_HEADS_A = 4
DK_A = 64
W_A = 256
GLA_RANK = 16
GLA_TAU = 16.0
GLA_CHUNK = 64
N_HEADS_B = 4
DH_B = 64
DV_B = 128
W_B = 512
ROPE_THETA = 10000.0
AXIS_PAIRS = DH_B // 4
N_GROUPS_C = 4
DG_C = 64
W_C = 256
SGU_CHUNK = 128
D_FF = 2816
N_EXPERTS = 8
EPS = 1e-6

T_PROMPT = BATCH * SEQ
T_SAMPLE = DEC_BATCH * DEC_SEQ
T_ALL = T_PROMPT + T_SAMPLE
N_MOD_ROWS = 8
LANES = 128
VMEM_LIMIT = 56 * 1024 * 1024


def _cparams(sem):
    return pltpu.CompilerParams(dimension_semantics=sem, vmem_limit_bytes=VMEM_LIMIT)


def _dot(a, b):
    return jnp.dot(a, b, preferred_element_type=F32)


def _dot_nt(a, b):
    return lax.dot_general(a, b, (((1,), (1,)), ((), ())), preferred_element_type=F32)


def _dot_tn(a, b):
    return lax.dot_general(a, b, (((0,), (0,)), ((), ())), preferred_element_type=F32)


def _split_bf16(x):
    hi = x.astype(BF16)
    lo = (x - hi.astype(F32)).astype(BF16)
    return hi, lo


def _dot3(a, w):
    a_hi, a_lo = _split_bf16(a)
    w_hi, w_lo = _split_bf16(w)
    return _dot(a_hi, w_hi) + (_dot(a_lo, w_hi) + _dot(a_hi, w_lo))


def _sigmoid(x):
    return 1.0 / (1.0 + jnp.exp(-x))


def _silu(x):
    return x * _sigmoid(x)


def _gelu_tanh(x):
    c = math.sqrt(2.0 / math.pi)
    return x * (0.5 * (1.0 + jnp.tanh(c * (x + 0.044715 * (x * x * x)))))


def _log_sigmoid(x):
    return jnp.minimum(x, 0.0) - jnp.log(1.0 + jnp.exp(-jnp.abs(x)))


def _mod_row(i, tm):
    n_p = T_PROMPT // tm
    per_b = DEC_SEQ // tm
    return jnp.where(i < n_p, 0, 1 + (i - n_p) // per_b)


ADA_TN = 1536


def _ada_kernel(c_ref, w_ref, b_ref, o_ref):
    a = _silu(c_ref[...])
    o_ref[...] = _dot3(a, w_ref[...]) + b_ref[...]


def _ada_call(cvecs, w_ada, b_ada):
    n_col = (6 * D_MODEL) // ADA_TN
    return pl.pallas_call(
        _ada_kernel,
        grid=(DEPTH, n_col),
        in_specs=[
            pl.BlockSpec((N_MOD_ROWS, D_MODEL), lambda l, j: (0, 0)),
            pl.BlockSpec((None, D_MODEL, ADA_TN), lambda l, j: (l, 0, j)),
            pl.BlockSpec((None, 1, ADA_TN), lambda l, j: (l, 0, j)),
        ],
        out_specs=pl.BlockSpec((None, N_MOD_ROWS, ADA_TN), lambda l, j: (l, 0, j)),
        out_shape=jax.ShapeDtypeStruct((DEPTH, N_MOD_ROWS, 6 * D_MODEL), F32),
        compiler_params=_cparams(("arbitrary", "arbitrary")),
        name="ada_mod",
    )(cvecs, w_ada, b_ada.reshape(DEPTH, 1, 6 * D_MODEL))


IN_TM = 512
IN_COLS = 3104
Z_COL0 = 1024
Z_COLS = 2 * GLA_RANK
W_MAIN = 3072


def _piece_specs(pieces, tm, width):
    specs, t0 = [], 0
    for arr in pieces:
        nt = arr.shape[0] // tm
        specs.append(pl.BlockSpec((tm, width), lambda i, t0=t0, nt=nt: (jnp.clip(i - t0, 0, nt - 1), 0)))
        t0 += nt
    assert t0 * tm == T_ALL and len(pieces) in (1, 2) and (len(pieces) == 1 or pieces[0].shape[0] == T_PROMPT)
    return specs


def _pick_piece(refs, tm, rows=slice(None)):
    if len(refs) == 1:
        return refs[0][rows, :]
    return jnp.where(pl.program_id(0) < T_PROMPT // tm, refs[0][rows, :], refs[1][rows, :])


RIDER_BLOCKS_20 = 16


W_STAGE_ROWS = 512


def _in_proj_kernel(*refs, n_x, layer):
    x_refs = refs[:n_x]
    (nrm_ref, sh_ref, sc_ref, w_hbm, wup_ref, bup_ref, ws_ref, bs_ref, cast_ref,
     g4_ref, la_ref, qb_ref, kb_ref, vb_ref, oc_ref, cast_out_ref, wm_ref, wz_ref, stage_ref, sem) = refs[n_x:]

    @pl.when(pl.program_id(0) < RIDER_BLOCKS_20)
    def _():
        cast_out_ref[...] = cast_ref[...].astype(BF16)

    @pl.when(pl.program_id(0) == 0)
    def _():
        n = W_STAGE_ROWS
        moves = [(src, src if src < Z_COL0 else src - Z_COLS)
                 for src in (*range(0, Z_COL0, n), *range(Z_COL0 + Z_COLS, IN_COLS, n))]
        copy = lambda k: pltpu.make_async_copy(w_hbm.at[layer, pl.ds(moves[k][0], n), :], stage_ref.at[k % 2],
                                               sem.at[k % 2])
        copy(0).start()
        for k, (_, dst) in enumerate(moves):
            if k + 1 < len(moves):
                copy(k + 1).start()
            copy(k).wait()
            wm_ref[dst:dst + n, :] = stage_ref[k % 2].astype(BF16)
        gate_rows = pltpu.make_async_copy(w_hbm.at[layer, pl.ds(Z_COL0, Z_COLS), :],
                                          stage_ref.at[0, pl.ds(0, Z_COLS), :], sem.at[0])
        gate_rows.start()
        gate_rows.wait()
        wz_ref[...] = jnp.zeros_like(wz_ref)
        wz_ref[0:Z_COLS, :] = stage_ref[0, 0:Z_COLS, :].astype(BF16)

    x = _pick_piece(x_refs, IN_TM)
    y = x * lax.rsqrt(jnp.mean(x * x, axis=-1, keepdims=True) + EPS)
    h = (y * nrm_ref[...]) * (1.0 + sc_ref[...]) + sh_ref[...]
    hb = h.astype(BF16)
    def plain_job(ref, c0, w0):
        def job():
            ref[:, c0:c0 + MXU_N] = _dot_nt(hb, wm_ref[w0:w0 + MXU_N, :])
        return job

    jobs = [plain_job(ref, c, w0 + c) for ref, w0, width in
            ((g4_ref, 0, 1024), (qb_ref, 1024, 512), (kb_ref, 1536, 512), (vb_ref, 2048, 512))
            for c in range(0, width, MXU_N)]
    z = _dot_nt(hb, wz_ref[...])
    jobs.pop(0)()
    zz = _dot(z.astype(BF16), wup_ref[...]) + bup_ref[...]
    uv = _dot_nt(hb, wm_ref[2560:3072, :])
    jobs.pop(0)()
    la_ref[...] = _log_sigmoid(zz) * (1.0 / GLA_TAU)
    jobs.pop(0)()
    oc_ref[...] = _sgu(uv, ws_ref, bs_ref, jobs)


def _resident_layer(a, l):
    return pl.BlockSpec((None,) + a.shape[1:], lambda *_: (l,) + (0,) * (a.ndim - 1), pipeline_mode=pl.Buffered(1))


def _in_proj_call(x_pieces, norm_g, mod4, w_in_t, l, w_up, b_up, sgu_ws, sgu_bs, cast_w):
    tm = IN_TM
    rider_spec, rider_shape = _cast_rider(cast_w, RIDER_BLOCKS_20)
    n = T_ALL // tm
    row = functools.partial(_mod_row, tm=tm)
    mod_spec = lambda k: pl.BlockSpec((None, None, 1, D_MODEL), lambda i: (row(i), k, 0, 0))
    full = lambda a: pl.BlockSpec(a.shape, lambda i: (0,) * a.ndim)
    out = lambda w: pl.BlockSpec((tm, w), lambda i: (i, 0))
    return pl.pallas_call(
        functools.partial(_in_proj_kernel, n_x=len(x_pieces), layer=l),
        grid=(n,),
        in_specs=_piece_specs(x_pieces, tm, D_MODEL) + [full(norm_g), mod_spec(0), mod_spec(1), pl.BlockSpec(memory_space=pl.ANY),
                                                        full(w_up), full(b_up), full(sgu_ws), full(sgu_bs),
                                                        rider_spec],
        out_specs=[out(1024), out(512), out(512), out(512), out(512), out(W_C), rider_spec],
        out_shape=[jax.ShapeDtypeStruct((T_ALL, w), F32) for w in (1024, 512, 512, 512, 512)]
        + [jax.ShapeDtypeStruct((T_ALL, W_C), BF16), rider_shape],
        scratch_shapes=[pltpu.VMEM((W_MAIN, D_MODEL), BF16), pltpu.VMEM((LANES, D_MODEL), BF16),
                        pltpu.VMEM((2, W_STAGE_ROWS, D_MODEL), F32), pltpu.SemaphoreType.DMA((2,))],
        compiler_params=_cparams(("arbitrary",)),
        name="in_proj",
    )(*x_pieces, norm_g, mod4, mod4, w_in_t, w_up, b_up, sgu_ws, sgu_bs, cast_w)


GLA_SB = 256
GLA_NC = GLA_SB // GLA_CHUNK


def _head_blocks(x, same64):
    return jnp.where(same64, jnp.concatenate([x] * N_HEADS_A, axis=0), 0.0).astype(BF16)


def _gla_superblock(q, k, vb, v_blocks, la, st_all, tri, mask4, same64, forward):
    c = GLA_CHUNK
    mid, last = (c // 2 - 1, c - 1) if forward else (c // 2, 0)
    la_hi, la_lo = _split_bf16(la)
    b = _dot(tri, la_hi) + _dot(tri, la_lo)
    yield
    rows_of = lambda r: jnp.concatenate(
        [jnp.broadcast_to(b[i * c + r:i * c + r + 1, :], (c, W_A)) for i in range(GLA_NC)], axis=0)
    m = rows_of(mid)
    bl = rows_of(last)
    qe = (q * jnp.exp(b - m)).astype(BF16)
    ke = k * jnp.exp(m - b)
    qi = (q * jnp.exp(b)).astype(BF16)
    ks = (k * jnp.exp(bl - b)).astype(BF16)
    outs = [None] * GLA_NC
    for i in (range(GLA_NC) if forward else reversed(range(GLA_NC))):
        rows = slice(i * c, (i + 1) * c)
        s = _dot_nt(qe[rows, :], _head_blocks(ke[rows, :], same64))
        kv = _dot_tn(vb[rows, :], ks[rows, :])
        yield
        a = jnp.where(mask4, s, 0.0).astype(BF16)
        outs[i] = _dot(a, v_blocks[i]) + _dot_nt(qi[rows, :], st_all.astype(BF16))
        st_all = st_all * jnp.exp(bl[i * c:i * c + 1, :]) + jnp.where(same64, kv, 0.0)
        yield
    return jnp.concatenate(outs, axis=0), st_all


def _interleave(*gens):
    results = [None] * len(gens)
    active = list(enumerate(gens))
    while active:
        for item in list(active):
            try:
                next(item[1])
            except StopIteration as stop:
                results[item[0]] = stop.value
                active.remove(item)
    return results


def _cast_rider(w, n_blocks):
    e, r, c = w.shape
    per = n_blocks // e
    rows = r // per
    assert per * e == n_blocks and rows * per == r and rows % 16 == 0

    def index(b):
        b = jnp.minimum(b, n_blocks - 1)
        return b // per, b % per, 0

    spec = pl.BlockSpec((None, rows, c), index)
    return spec, jax.ShapeDtypeStruct(w.shape, BF16)


def _gla_kernel(*refs, seq, final):
    if final:
        (g4_ref, la_ref, s0f_ref, s0b_ref, gain_ref, sf_prev_ref, sb_prev_ref, k_prev_ref, k_ref,
         o_ref, sf_ref, sb_ref, ck_ref, of_ref, ob_ref) = refs
        for l, src in enumerate((k_prev_ref, k_ref)):
            for h in range(N_HEADS_B):
                ck_ref[l, pl.ds(h, SEQ, stride=N_HEADS_B), :] = src[:, h * DV_B:(h + 1) * DV_B]
    else:
        g4_ref, la_ref, s0f_ref, s0b_ref, gain_ref, o_ref, sf_ref, sb_ref, of_ref, ob_ref = refs
    n = GLA_SB
    nsb = seq // n
    r = lax.broadcasted_iota(jnp.int32, (n, n), 0)
    s = lax.broadcasted_iota(jnp.int32, (n, n), 1)
    same64 = (r // GLA_CHUNK) == (s // GLA_CHUNK)
    lower = same64 & (s <= r)
    upper = same64 & (s >= r)
    tri_f = jnp.where(lower, 1.0, 0.0).astype(BF16)
    tri_b = jnp.where(upper, 1.0, 0.0).astype(BF16)
    ones64 = jnp.where(same64, 1.0, 0.0).astype(BF16)
    key_row = lax.broadcasted_iota(jnp.int32, (GLA_CHUNK, n), 1) % GLA_CHUNK
    qry_row = lax.broadcasted_iota(jnp.int32, (GLA_CHUNK, n), 0)
    mask4_f = key_row <= qry_row
    mask4_b = key_row >= qry_row
    scale = DK_A ** -0.5
    expand = lambda st: jnp.where(same64, jnp.concatenate([st] * N_HEADS_A, axis=0), 0.0)
    compact = lambda st_all: functools.reduce(
        lambda a, b: a + b, [st_all[h * 64:(h + 1) * 64, :] for h in range(N_HEADS_A)])

    def step(i, carry):
        stf, stb = carry
        rf = pl.ds(pl.multiple_of(i * n, n), n)
        rb = pl.ds(pl.multiple_of((nsb - 1 - i) * n, n), n)
        def direction(rows, la_cols, st, tri, mask4, forward):
            q = g4_ref[rows, 0:256] * scale
            k = g4_ref[rows, 256:512]
            v = g4_ref[rows, 512:768]
            v_blocks = [_head_blocks(v[i * GLA_CHUNK:(i + 1) * GLA_CHUNK, :], same64) for i in range(GLA_NC)]
            return _gla_superblock(q, k, v.astype(BF16), v_blocks, la_ref[rows, la_cols], st, tri, mask4, same64,
                                   forward)

        (o_f, stf), (o_b, stb) = _interleave(direction(rf, slice(0, 256), stf, tri_f, mask4_f, True),
                                             direction(rb, slice(256, 512), stb, tri_b, mask4_b, False))
        of_ref[rf, :] = o_f
        ob_ref[rb, :] = o_b
        return stf, stb

    stf, stb = lax.fori_loop(0, nsb, step, (expand(s0f_ref[...]), expand(s0b_ref[...])))
    if final:
        sf_ref[0] = sf_prev_ref[...].T
        sb_ref[0] = sb_prev_ref[...].T
        sf_ref[1] = compact(stf).T
        sb_ref[1] = compact(stb).T
    else:
        sf_ref[...] = compact(stf)
        sb_ref[...] = compact(stb)

    gain = gain_ref[...]

    def finish(i, carry):
        rows = pl.ds(pl.multiple_of(i * n, n), n)
        o = of_ref[rows, :] + ob_ref[rows, :]
        sq_hi, sq_lo = _split_bf16(o * o)
        ms = (_dot(sq_hi, ones64) + _dot(sq_lo, ones64)) * (1.0 / DK_A)
        y = (o * lax.rsqrt(ms + EPS)) * gain
        o_ref[rows, :] = (y * _silu(g4_ref[rows, 768:1024])).astype(BF16)
        return carry

    lax.fori_loop(0, nsb, finish, 0)


def _gla_call(g4, la, s0f, s0b, gain, prev=None, *, batch, seq, row_block0):
    tok = lambda w: pl.BlockSpec((seq, w), lambda b: (row_block0 + b, 0))
    st = pl.BlockSpec((None, 64, 256), lambda b: (b, 0, 0))
    in_specs = [tok(1024), tok(512), st, st, pl.BlockSpec((1, W_A), lambda b: (0, 0))]
    args = [g4, la, s0f, s0b, gain]
    o_spec = pl.BlockSpec((seq, W_A), lambda b: (b, 0))
    o_shape = jax.ShapeDtypeStruct((batch * seq, W_A), BF16)
    if prev is None:
        out_specs = [o_spec, st, st]
        out_shape = [o_shape, jax.ShapeDtypeStruct((batch, 64, 256), F32), jax.ShapeDtypeStruct((batch, 64, 256), F32)]
    else:
        assert seq == SEQ and batch == BATCH and DEPTH == 2
        kblk = pl.BlockSpec((SEQ, W_B), lambda b: (b, 0))
        in_specs += [st, st, kblk, kblk]
        args += list(prev)
        st2 = pl.BlockSpec((None, DEPTH, 256, 64), lambda b: (b, 0, 0, 0))
        st2_shape = jax.ShapeDtypeStruct((batch, DEPTH, 256, 64), F32)
        out_specs = [o_spec, st2, st2, pl.BlockSpec((None, DEPTH, SEQ * N_HEADS_B, DV_B), lambda b: (b, 0, 0, 0))]
        out_shape = [o_shape, st2_shape, st2_shape, jax.ShapeDtypeStruct((batch, DEPTH, SEQ * N_HEADS_B, DV_B), F32)]
    return pl.pallas_call(
        functools.partial(_gla_kernel, seq=seq, final=prev is not None),
        grid=(batch,),
        in_specs=in_specs,
        out_specs=out_specs,
        out_shape=out_shape,
        scratch_shapes=[pltpu.VMEM((seq, W_A), F32), pltpu.VMEM((seq, W_A), F32)],
        compiler_params=_cparams(("arbitrary",)),
        name=f"gla_{seq}",
    )(*args)


def _state_to_kernel(s):
    b = s.shape[0]
    return jnp.transpose(s, (0, 3, 1, 2)).reshape(b, 64, 256)


def _lambda(lv, lam_init):
    l01 = jnp.sum(lv[0:1, :] * lv[1:2, :], axis=-1, keepdims=True)
    l23 = jnp.sum(lv[2:3, :] * lv[3:4, :], axis=-1, keepdims=True)
    return jnp.exp(l01) - jnp.exp(l23) + lam_init


def _softmax_parts(parts):
    mx = functools.reduce(jnp.maximum, [jnp.max(p, axis=-1, keepdims=True) for p in parts])
    es = [jnp.exp(p - mx) for p in parts]
    den = functools.reduce(lambda a, b: a + b, [jnp.sum(e, axis=-1, keepdims=True) for e in es])
    return [e / den for e in es]


def _diff_finish(o, gain, lam_init):
    o = o * lax.rsqrt(jnp.mean(o * o, axis=-1, keepdims=True) + EPS)
    return ((o * gain) * (1.0 - lam_init)).astype(BF16)


QK_SCALE = DH_B ** -0.5


def _key_halves(k):
    first = lax.broadcasted_iota(jnp.int32, k.shape, 1) < DH_B
    return jnp.where(first, k, 0.0).astype(BF16), jnp.where(first, 0.0, k).astype(BF16)


def _attn_prompt_kernel(lv_ref, q_ref, k_ref, v_ref, gain_ref, *rest, lam_init, n_prev):
    prev_refs, (o_ref, *cache_refs) = rest[:n_prev], rest[n_prev:]
    lam = _lambda(lv_ref[...], lam_init)

    def head(h):
        cols = slice(h * DV_B, (h + 1) * DV_B)
        q = (q_ref[:, cols] * QK_SCALE).astype(BF16)
        k1, k2 = _key_halves(k_ref[:, cols])
        s1 = _dot_nt(q, k1)
        s2 = _dot_nt(q, k2)
        yield
        (p1,) = _softmax_parts([s1])
        (p2,) = _softmax_parts([s2])
        a = p1 - lam * p2
        o = _dot(a.astype(BF16), v_ref[:, cols].astype(BF16))
        yield
        o_ref[:, cols] = _diff_finish(o, gain_ref[:, cols], lam_init)
        if cache_refs:
            (cv_ref,) = cache_refs
            for l, src in enumerate((*prev_refs, v_ref)):
                cv_ref[l, pl.ds(h, SEQ, stride=N_HEADS_B), :] = src[:, cols]

    _interleave(*[head(h) for h in range(N_HEADS_B)])


def _attn_prompt_call(lv, qb, kb, vb, gain, prev_v=(), *, lam_init, write_cache=False):
    blk = pl.BlockSpec((SEQ, W_B), lambda b: (b, 0))
    n_prev = len(prev_v)
    assert write_cache or not prev_v
    in_specs = [pl.BlockSpec((4, DH_B), lambda b: (0, 0)), blk, blk, blk,
                pl.BlockSpec((1, W_B), lambda b: (0, 0))] + [blk] * n_prev
    args = [lv, qb, kb, vb, gain, *prev_v]
    out_specs = [blk]
    out_shape = [jax.ShapeDtypeStruct((T_PROMPT, W_B), BF16)]
    if write_cache:
        out_specs.append(pl.BlockSpec((None, n_prev + 1, SEQ * N_HEADS_B, DV_B), lambda b: (b, 0, 0, 0)))
        out_shape.append(jax.ShapeDtypeStruct((BATCH, n_prev + 1, SEQ * N_HEADS_B, DV_B), F32))
    return pl.pallas_call(
        functools.partial(_attn_prompt_kernel, lam_init=lam_init, n_prev=n_prev),
        grid=(BATCH,),
        in_specs=in_specs,
        out_specs=out_specs,
        out_shape=out_shape,
        compiler_params=_cparams(("arbitrary",)),
        name="diff_attn_prompt",
    )(*args)


def _rope(x, cos, sin_signed):
    lane = lax.broadcasted_iota(jnp.int32, x.shape, 1)
    first = (lane % (2 * AXIS_PAIRS)) < AXIS_PAIRS
    partner = jnp.where(first, pltpu.roll(x, LANES - AXIS_PAIRS, 1), pltpu.roll(x, AXIS_PAIRS, 1))
    return x * cos + partner * sin_signed


ATT_TQ = 256


def _attn_sample_kernel(lv_ref, q_ref, k_ref, v_ref, kc_ref, vc_ref, cosq_ref, sinq_ref,
                        cosk_ref, sink_ref, gain_ref, *rest, lam_init, n_cast):
    w_refs, o_ref, wb_refs, (k1_ref, k2_ref) = rest[:n_cast], rest[n_cast], rest[n_cast + 1:2 * n_cast + 1], rest[-2:]
    for w_ref, wb_ref in zip(w_refs, wb_refs):
        wb_ref[...] = w_ref[...].astype(BF16)

    @pl.when(pl.program_id(1) == 0)
    def _():
        for h in range(N_HEADS_B):
            cols = slice(h * DV_B, (h + 1) * DV_B)
            k1_ref[:, cols], k2_ref[:, cols] = _key_halves(_rope(k_ref[:, cols], cosk_ref[...], sink_ref[...]))

    lam = _lambda(lv_ref[...], lam_init)

    def head(h):
        cols = slice(h * DV_B, (h + 1) * DV_B)
        q = (_rope(q_ref[:, cols], cosq_ref[...], sinq_ref[...]) * QK_SCALE).astype(BF16)
        c1, c2 = _key_halves(kc_ref[:, cols])
        s1 = [_dot_nt(q, k1_ref[:, cols]), _dot_nt(q, c1)]
        s2 = [_dot_nt(q, k2_ref[:, cols]), _dot_nt(q, c2)]
        yield
        p1 = _softmax_parts(s1)
        p2 = _softmax_parts(s2)
        a_own = p1[0] - lam * p2[0]
        a_ctx = p1[1] - lam * p2[1]
        o = (_dot(a_own.astype(BF16), v_ref[:, cols].astype(BF16))
             + _dot(a_ctx.astype(BF16), vc_ref[:, cols].astype(BF16)))
        yield
        o_ref[:, cols] = _diff_finish(o, gain_ref[:, cols], lam_init)

    _interleave(*[head(h) for h in range(N_HEADS_B)])


def _attn_sample_call(lv, qb, kb, vb, kc, vc, cos, sin_signed, gain, cast_ws=(), *, lam_init):
    tq = ATT_TQ
    nq = DEC_SEQ // tq
    p0 = T_PROMPT // tq
    s0 = T_PROMPT // DEC_SEQ
    qblk = pl.BlockSpec((tq, W_B), lambda b, t: (p0 + b * nq + t, 0))
    kvblk = pl.BlockSpec((DEC_SEQ, W_B), lambda b, t: (s0 + b, 0))
    cblk = pl.BlockSpec((None, PAST_LEN, W_B), lambda b, t: (b, 0, 0))
    n_steps = DEC_BATCH * nq
    cast_specs = []
    for w in cast_ws:
        assert w.shape[0] % (16 * n_steps) == 0
        cast_specs.append(pl.BlockSpec((w.shape[0] // n_steps, w.shape[1]), lambda b, t: (b * nq + t, 0)))
    return pl.pallas_call(
        functools.partial(_attn_sample_kernel, lam_init=lam_init, n_cast=len(cast_ws)),
        grid=(DEC_BATCH, nq),
        in_specs=[pl.BlockSpec((4, DH_B), lambda b, t: (0, 0)), qblk, kvblk, kvblk, cblk, cblk,
                  pl.BlockSpec((tq, DV_B), lambda b, t: (t, 0)),
                  pl.BlockSpec((tq, DV_B), lambda b, t: (t, 0)),
                  pl.BlockSpec((DEC_SEQ, DV_B), lambda b, t: (0, 0)),
                  pl.BlockSpec((DEC_SEQ, DV_B), lambda b, t: (0, 0)),
                  pl.BlockSpec((1, W_B), lambda b, t: (0, 0))] + cast_specs,
        out_specs=[pl.BlockSpec((tq, W_B), lambda b, t: (b * nq + t, 0))] + cast_specs,
        out_shape=[jax.ShapeDtypeStruct((T_SAMPLE, W_B), BF16)] + [jax.ShapeDtypeStruct(w.shape, BF16) for w in cast_ws],
        scratch_shapes=[pltpu.VMEM((DEC_SEQ, W_B), BF16), pltpu.VMEM((DEC_SEQ, W_B), BF16)],
        compiler_params=_cparams(("arbitrary", "arbitrary")),
        name="diff_attn_sample",
    )(lv, qb, kb, vb, kc, vc, cos, sin_signed, cos, sin_signed, gain, *cast_ws)


def _rope_tables():
    rows = DEC_SEQ // GRID_W
    row = jnp.repeat(jnp.arange(rows, dtype=F32), GRID_W)
    col = jnp.tile(jnp.arange(GRID_W, dtype=F32), rows)
    freqs = ROPE_THETA ** (-jnp.arange(AXIS_PAIRS, dtype=F32) / AXIS_PAIRS)
    ar, ac = row[:, None] * freqs, col[:, None] * freqs
    cos64 = jnp.concatenate([jnp.cos(ar), jnp.cos(ar), jnp.cos(ac), jnp.cos(ac)], axis=-1)
    sin64 = jnp.concatenate([-jnp.sin(ar), jnp.sin(ar), -jnp.sin(ac), jnp.sin(ac)], axis=-1)
    return jnp.tile(cos64, (1, 2)), jnp.tile(sin64, (1, 2))


def _group_mean(x, ones64):
    hi, lo = _split_bf16(x)
    return (_dot(hi, ones64) + _dot(lo, ones64)) * (1.0 / DG_C)


def _sgu(uv, ws_ref, bs_ref, fillers=()):
    fillers = list(fillers)
    fill = lambda: fillers.pop(0)() if fillers else None
    r = lax.broadcasted_iota(jnp.int32, (W_C, W_C), 0)
    s = lax.broadcasted_iota(jnp.int32, (W_C, W_C), 1)
    ones64 = jnp.where((r // DG_C) == (s // DG_C), 1.0, 0.0).astype(BF16)
    lane = lax.broadcasted_iota(jnp.int32, (SGU_CHUNK, W_C), 1)
    outs = []
    for n in range(uv.shape[0] // SGU_CHUNK):
        rows = slice(n * SGU_CHUNK, (n + 1) * SGU_CHUNK)
        u = _gelu_tanh(uv[rows, 0:256])
        v = _gelu_tanh(uv[rows, 256:512])
        mu = _group_mean(v, ones64)
        fill()
        d = v - mu
        var = _group_mean(d * d, ones64)
        fill()
        vn = d * lax.rsqrt(var + EPS)
        s_mix = bs_ref[...]
        for g in range(N_GROUPS_C):
            vn_g = jnp.where((lane // DG_C) == g, vn, 0.0).astype(BF16)
            s_mix = s_mix + _dot(ws_ref[g], vn_g)
        fill()
        outs.append((u * s_mix).astype(BF16))
    while fillers:
        fill()
    return jnp.concatenate(outs, axis=0)


OUT_TM = 1024
OUT_SLABS = 4


SEL_LANE0 = N_EXPERTS


def _top2_gates(logits):
    lane = lax.broadcasted_iota(jnp.int32, logits.shape, 1).astype(F32)
    neg = -jnp.inf
    lg = jnp.where(lane < N_EXPERTS, logits, neg)
    m1 = jnp.max(lg, axis=-1, keepdims=True)
    i1 = jnp.min(jnp.where(lg == m1, lane, float(LANES)), axis=-1, keepdims=True)
    lg2 = jnp.where(lane == i1, neg, lg)
    m2 = jnp.max(lg2, axis=-1, keepdims=True)
    i2 = jnp.min(jnp.where(lg2 == m2, lane, float(LANES)), axis=-1, keepdims=True)
    e2 = jnp.exp(m2 - m1)
    den = 1.0 + e2
    gates = jnp.where(lane == i1, 1.0 / den, 0.0) + jnp.where(lane == i2, e2 / den, 0.0)
    sel = jnp.where((lane == i1 + SEL_LANE0) | (lane == i2 + SEL_LANE0), 1.0, 0.0)
    return gates + sel


def _store_token_tiles(ref, val, r0=0):
    n = val.shape[0]
    for k in range(D_MODEL // LANES):
        ref[pl.ds(8 * r0 + k, n, stride=8), :] = val[:, k * LANES:(k + 1) * LANES]


def _load_token_tiles(ref, n):
    return jnp.concatenate([ref[pl.ds(k, n, stride=8), :] for k in range(D_MODEL // LANES)], axis=-1)


def _out_proj_kernel(*refs, n_x, moe):
    oa_refs, ob_refs, (oc_ref,), x_refs = refs[0:2], refs[2:4], refs[4:5], refs[5:5 + n_x]
    w_ref, g1_ref, nrm_ref, sc_ref, sh_ref, *rest = refs[5 + n_x:]
    if moe:
        rw_ref, x1_ref, h2t_ref, gates_ref, gates_t_ref, wo_ref = rest
    else:
        x1_ref, h2_ref, wo_ref = rest

    @pl.when(pl.program_id(0) == 0)
    def _():
        wo_ref[...] = w_ref[...].astype(BF16)

    if moe:
        rw_hi, rw_lo = _split_bf16(rw_ref[...])

    def rows_chain(r0, n):
        rs = slice(r0, r0 + n)
        y = (_dot(_pick_piece(oa_refs, OUT_TM, rs), wo_ref[0:256, :])
             + _dot(_pick_piece(ob_refs, OUT_TM, rs), wo_ref[256:768, :]) + _dot(oc_ref[rs, :], wo_ref[768:1024, :]))
        yield
        x1 = _pick_piece(x_refs, OUT_TM, rs) + g1_ref[...] * y
        x1_ref[rs, :] = x1
        yn = x1 * lax.rsqrt(jnp.mean(x1 * x1, axis=-1, keepdims=True) + EPS)
        h = (yn * nrm_ref[...]) * (1.0 + sc_ref[...]) + sh_ref[...]
        if moe:
            _store_token_tiles(h2t_ref, h, r0)
            h_hi, h_lo = _split_bf16(h)
            logits = _dot(h_hi, rw_hi) + (_dot(h_lo, rw_hi) + _dot(h_hi, rw_lo))
            yield
            gates = _top2_gates(logits)
            gates_ref[rs, :] = gates
            gates_t_ref[:, rs] = gates.T
        else:
            h2_ref[rs, :] = h.astype(BF16)

    n_slab = OUT_TM // OUT_SLABS
    _interleave(*[rows_chain(i * n_slab, n_slab) for i in range(OUT_SLABS)])


def _out_proj_call(oa_pieces, ob_pieces, oc, x_pieces, w_out, l, mod4, norm_g, router_w=None):
    tm = OUT_TM
    n = T_ALL // tm
    moe = router_w is not None
    row = functools.partial(_mod_row, tm=tm)
    mod_spec = lambda k: pl.BlockSpec((None, None, 1, D_MODEL), lambda i: (row(i), k, 0, 0))
    tok = lambda w: pl.BlockSpec((tm, w), lambda i: (i, 0))
    full = lambda a: pl.BlockSpec(a.shape, lambda i: (0,) * a.ndim)
    in_specs = (_piece_specs(oa_pieces, tm, W_A) + _piece_specs(ob_pieces, tm, W_B) + [tok(W_C)]
                + _piece_specs(x_pieces, tm, D_MODEL)
                + [_resident_layer(w_out, l), mod_spec(2), full(norm_g), mod_spec(4), mod_spec(3)])
    args = [*oa_pieces, *ob_pieces, oc, *x_pieces, w_out, mod4, norm_g, mod4, mod4]
    if moe:
        in_specs.append(full(router_w))
        args.append(router_w)
        out_specs = [tok(D_MODEL), pl.BlockSpec((tm * 8, LANES), lambda i: (i, 0)), tok(LANES),
                     pl.BlockSpec((LANES, tm), lambda i: (0, i))]
        out_shape = [jax.ShapeDtypeStruct((T_ALL, D_MODEL), F32), jax.ShapeDtypeStruct((T_ALL * 8, LANES), F32),
                     jax.ShapeDtypeStruct((T_ALL, LANES), F32), jax.ShapeDtypeStruct((LANES, T_ALL), F32)]
    else:
        out_specs = [tok(D_MODEL), tok(D_MODEL)]
        out_shape = [jax.ShapeDtypeStruct((T_ALL, D_MODEL), F32), jax.ShapeDtypeStruct((T_ALL, D_MODEL), BF16)]
    return pl.pallas_call(
        functools.partial(_out_proj_kernel, n_x=len(x_pieces), moe=moe),
        grid=(n,),
        in_specs=in_specs,
        out_specs=out_specs,
        out_shape=out_shape,
        scratch_shapes=[pltpu.VMEM((D_MODEL, D_MODEL), BF16)],
        compiler_params=_cparams(("arbitrary",)),
        name="out_proj_moe" if moe else "out_proj",
    )(*args)


FFN_TM = 512
MXU_N = 256
FFN_SPLITS = (0, 1024, 2048, D_FF)
assert all(s % MXU_N == 0 for s in FFN_SPLITS)


def _swiglu(h, wg_ref, wu_ref, wd_ref):
    out = None
    for c0, c1 in zip(FFN_SPLITS[:-1], FFN_SPLITS[1:]):
        act = _silu(_dot(h, wg_ref[:, c0:c1])) * _dot(h, wu_ref[:, c0:c1])
        d = _dot(act.astype(BF16), wd_ref[c0:c1, :])
        out = d if out is None else out + d
    return out


def _ffn_kernel(h_ref, x_ref, g2_ref, wg_ref, wu_ref, wd_ref, w_ref, o_ref, wb_ref):
    @pl.when(pl.program_id(0) < RIDER_BLOCKS_20)
    def _():
        wb_ref[...] = w_ref[...].astype(BF16)

    o_ref[...] = x_ref[...] + g2_ref[...] * _swiglu(h_ref[...], wg_ref, wu_ref, wd_ref)


def _ffn_call(h2, x1, mod4, wg, wu, wd, cast_w):
    tm = FFN_TM
    row = functools.partial(_mod_row, tm=tm)
    tok = lambda w: pl.BlockSpec((tm, w), lambda i: (i, 0))
    resident = lambda a: pl.BlockSpec(a.shape, lambda i: (0, 0), pipeline_mode=pl.Buffered(1))
    rider_spec, rider_shape = _cast_rider(cast_w, RIDER_BLOCKS_20)
    return pl.pallas_call(
        _ffn_kernel,
        grid=(T_ALL // tm,),
        in_specs=[tok(D_MODEL), tok(D_MODEL),
                  pl.BlockSpec((None, None, 1, D_MODEL), lambda i: (row(i), 5, 0, 0)),
                  resident(wg), resident(wu), resident(wd), rider_spec],
        out_specs=[tok(D_MODEL), rider_spec],
        out_shape=[jax.ShapeDtypeStruct((T_ALL, D_MODEL), F32), rider_shape],
        compiler_params=_cparams(("arbitrary",)),
        name="ffn_dense",
    )(h2, x1, mod4, wg, wu, wd, cast_w)


MOE_TM = 512
MOE_NT_MAX = (2 * T_ALL) // MOE_TM + N_EXPERTS
MOE_ROWS = MOE_NT_MAX * MOE_TM
PLAN_BLK = 512
MISC_LAST_START = 8
MISC_NT = 16


def _moe_plan_kernel(gt_ref, posa_ref, posb_ref, te_ref, ti_ref, misc_ref):
    tm = float(MOE_TM)
    sel = gt_ref[SEL_LANE0:SEL_LANE0 + N_EXPERTS, :]
    cnt = jnp.sum(sel, axis=1, keepdims=True)
    nt = jnp.floor((cnt + (tm - 1.0)) * (1.0 / tm))
    sub = lax.broadcasted_iota(jnp.int32, (N_EXPERTS, LANES), 0).astype(F32)
    lane = lax.broadcasted_iota(jnp.int32, (N_EXPERTS, LANES), 1).astype(F32)
    nt_b = jnp.broadcast_to(nt, (N_EXPERTS, LANES))
    nt_row = jnp.sum(jnp.where(sub == lane, nt_b, 0.0), axis=0, keepdims=True)
    toff = jnp.sum(jnp.where(lane < sub, jnp.broadcast_to(nt_row, (N_EXPERTS, LANES)), 0.0),
                   axis=1, keepdims=True)
    tend = toff + nt
    n_total = jnp.sum(nt, axis=0, keepdims=True)
    jc = jnp.minimum(lane, n_total - 1.0)
    te = jnp.sum(jnp.where(jc >= tend, 1.0, 0.0), axis=0, keepdims=True)
    te_ref[...] = te.astype(jnp.int32)
    ti_ref[...] = jc[0:1, :].astype(jnp.int32)
    last_start = (tend - 1.0) * tm
    ls_row = jnp.sum(jnp.where(sub + MISC_LAST_START == lane, jnp.broadcast_to(last_start, (N_EXPERTS, LANES)), 0.0),
                     axis=0, keepdims=True)
    nt_row2 = jnp.sum(jnp.where(sub + MISC_NT == lane, nt_b, 0.0), axis=0, keepdims=True)
    misc = jnp.where(lane[0:1, :] == 0.0, n_total, 0.0) + ls_row + nt_row2
    misc_ref[...] = misc.astype(jnp.int32)

    off = toff * tm
    r = lax.broadcasted_iota(jnp.int32, (PLAN_BLK, PLAN_BLK), 0)
    c = lax.broadcasted_iota(jnp.int32, (PLAN_BLK, PLAN_BLK), 1)
    upper = jnp.where(r <= c, 1.0, 0.0).astype(BF16)
    carry = jnp.zeros((N_EXPERTS, 1), F32)
    for blk in range(T_ALL // PLAN_BLK):
        cols = slice(blk * PLAN_BLK, (blk + 1) * PLAN_BLK)
        s = gt_ref[SEL_LANE0:SEL_LANE0 + N_EXPERTS, cols]
        rank = _dot(s.astype(BF16), upper) + carry
        pos = off + rank - 1.0
        posa_ref[:, cols] = jnp.min(jnp.where(s > 0.0, pos, 1e9), axis=0, keepdims=True).astype(jnp.int32)
        posb_ref[:, cols] = jnp.max(jnp.where(s > 0.0, pos, -1.0), axis=0, keepdims=True).astype(jnp.int32)
        carry = carry + jnp.sum(s, axis=1, keepdims=True)


def _moe_plan_call(gates_t):
    row = lambda w: jax.ShapeDtypeStruct((1, w), jnp.int32)
    full = lambda w: pl.BlockSpec((1, w), lambda: (0, 0))
    return pl.pallas_call(
        _moe_plan_kernel,
        in_specs=[pl.BlockSpec((LANES, T_ALL), lambda: (0, 0))],
        out_specs=[full(T_ALL), full(T_ALL), full(LANES), full(LANES), full(LANES)],
        out_shape=[row(T_ALL), row(T_ALL), row(LANES), row(LANES), row(LANES)],
        compiler_params=pltpu.CompilerParams(vmem_limit_bytes=VMEM_LIMIT),
        name="moe_plan",
    )(gates_t)


DMA_UNROLL = 8


def _row_tile(ref, row):
    return ref.at[pl.ds(pl.multiple_of(row * 8, 8), 8), :]


def _moe_scatter_kernel(misc_ref, posa_ref, posb_ref, h_ref, xs_ref, zero_ref, sem):
    tm = h_ref.shape[0] // 8

    @pl.when(pl.program_id(0) == 0)
    def _():
        zero_ref[...] = jnp.zeros_like(zero_ref)

        def zero_tile(first_row):
            start = pl.multiple_of(first_row * 8, 8)
            cp = pltpu.make_async_copy(zero_ref, xs_ref.at[pl.ds(start, MOE_TM * 8), :], sem.at[0])
            cp.start()
            cp.wait()

        for e in range(N_EXPERTS):
            @pl.when(misc_ref[0, MISC_NT + e] > 0)
            def _():
                zero_tile(misc_ref[0, MISC_LAST_START + e])

        def zero_tail(j, carry):
            zero_tile(j * MOE_TM)
            return carry

        lax.fori_loop(misc_ref[0, 0], MOE_NT_MAX, zero_tail, 0)

    def issue(r, carry):
        src = _row_tile(h_ref, r)
        pltpu.make_async_copy(src, _row_tile(xs_ref, posa_ref[0, r]), sem.at[0]).start(priority=0)
        pltpu.make_async_copy(src, _row_tile(xs_ref, posb_ref[0, r]), sem.at[1]).start(priority=1)
        return carry

    lax.fori_loop(0, tm, issue, 0, unroll=DMA_UNROLL)
    for k in range(2):
        pltpu.make_async_copy(h_ref, xs_ref.at[pl.ds(0, tm * 8), :], sem.at[k]).wait()


SCATTER_TM = 512


def _moe_scatter_call(misc, posa3, posb3, h2t):
    tm = SCATTER_TM
    smem_row = pl.BlockSpec((None, 1, tm), lambda i: (i, 0, 0), memory_space=pltpu.SMEM)
    return pl.pallas_call(
        _moe_scatter_kernel,
        grid=(T_ALL // tm,),
        in_specs=[pl.BlockSpec((1, LANES), lambda i: (0, 0), memory_space=pltpu.SMEM), smem_row, smem_row,
                  pl.BlockSpec((tm * 8, LANES), lambda i: (i, 0))],
        out_specs=pl.BlockSpec(memory_space=pl.ANY),
        out_shape=jax.ShapeDtypeStruct((MOE_ROWS * 8, LANES), F32),
        scratch_shapes=[pltpu.VMEM((MOE_TM * 8, LANES), F32), pltpu.SemaphoreType.DMA((2,))],
        compiler_params=_cparams(("arbitrary",)),
        name="moe_scatter",
    )(misc, posa3, posb3, h2t)


def _ffn_grouped_kernel(te_ref, ti_ref, misc_ref, x_ref, wg_ref, wu_ref, wd_ref, o_ref):
    j = pl.program_id(0)

    @pl.when(j < misc_ref[0])
    def _():
        h = _load_token_tiles(x_ref, MOE_TM).astype(BF16)
        _store_token_tiles(o_ref, _swiglu(h, wg_ref, wu_ref, wd_ref))

    @pl.when(j >= misc_ref[0])
    def _():
        o_ref[...] = jnp.zeros_like(o_ref)


def _ffn_grouped_call(te, ti, misc, xs, wg, wu, wd):
    expert = lambda *s: pl.BlockSpec((None,) + s, lambda j, te, ti, misc: (te[j], 0, 0))
    grid_spec = pltpu.PrefetchScalarGridSpec(
        num_scalar_prefetch=3,
        grid=(MOE_NT_MAX,),
        in_specs=[pl.BlockSpec((MOE_TM * 8, LANES), lambda j, te, ti, misc: (ti[j], 0)),
                  expert(D_MODEL, D_FF), expert(D_MODEL, D_FF), expert(D_FF, D_MODEL)],
        out_specs=pl.BlockSpec((MOE_TM * 8, LANES), lambda j, te, ti, misc: (j, 0)),
    )
    return pl.pallas_call(
        _ffn_grouped_kernel,
        grid_spec=grid_spec,
        out_shape=jax.ShapeDtypeStruct((MOE_ROWS * 8, LANES), F32),
        compiler_params=_cparams(("arbitrary",)),
        name="ffn_grouped",
    )(te, ti, misc, xs, wg, wu, wd)


COMBINE_TM = 256


def _moe_combine_kernel(posa_ref, posb_ref, posa_next_ref, posb_next_ref, ys_ref, x_ref, g2_ref, gates_ref, nf_ref,
                        op_ref, os_ref, bufa_ref, bufb_ref, sem):
    tm = COMBINE_TM
    i = pl.program_id(0)
    n = pl.num_programs(0)
    slot = i % 2

    def gather(pa_ref, pb_ref, s):
        def issue(r, carry):
            pltpu.make_async_copy(_row_tile(ys_ref, pa_ref[0, r]), _row_tile(bufa_ref.at[s], r),
                                  sem.at[s, 0]).start(priority=0)
            pltpu.make_async_copy(_row_tile(ys_ref, pb_ref[0, r]), _row_tile(bufb_ref.at[s], r),
                                  sem.at[s, 1]).start(priority=1)
            return carry

        lax.fori_loop(0, tm, issue, 0, unroll=DMA_UNROLL)

    @pl.when(i == 0)
    def _():
        gather(posa_ref, posb_ref, 0)

    @pl.when(i + 1 < n)
    def _():
        gather(posa_next_ref, posb_next_ref, 1 - slot)

    gates = gates_ref[...]
    lane = lax.broadcasted_iota(jnp.int32, gates.shape, 1).astype(F32)
    is_sel = (lane >= SEL_LANE0) & (lane < SEL_LANE0 + N_EXPERTS) & (gates > 0.0)
    ia = jnp.min(jnp.where(is_sel, lane, float(LANES)), axis=-1, keepdims=True) - SEL_LANE0
    ib = jnp.max(jnp.where(is_sel, lane, -1.0), axis=-1, keepdims=True) - SEL_LANE0
    wa = jnp.sum(jnp.where(lane == ia, gates, 0.0), axis=-1, keepdims=True)
    wb = jnp.sum(jnp.where(lane == ib, gates, 0.0), axis=-1, keepdims=True)

    pltpu.make_async_copy(ys_ref.at[pl.ds(0, tm * 8), :], bufa_ref.at[slot], sem.at[slot, 0]).wait()
    pltpu.make_async_copy(ys_ref.at[pl.ds(0, tm * 8), :], bufb_ref.at[slot], sem.at[slot, 1]).wait()

    y = wa * _load_token_tiles(bufa_ref.at[slot], tm) + wb * _load_token_tiles(bufb_ref.at[slot], tm)
    out = x_ref[...] + g2_ref[...] * y
    out = (out * lax.rsqrt(jnp.mean(out * out, axis=-1, keepdims=True) + EPS)) * nf_ref[...]

    @pl.when(i < T_PROMPT // tm)
    def _():
        op_ref[...] = out

    @pl.when(i >= T_PROMPT // tm)
    def _():
        os_ref[...] = out


def _moe_combine_call(posa3, posb3, ys, x1, mod4, gates, norm_f):
    tm = COMBINE_TM
    n = T_ALL // tm
    n_p = T_PROMPT // tm
    row = functools.partial(_mod_row, tm=tm)
    smem_row = pl.BlockSpec((None, 1, tm), lambda i: (i, 0, 0), memory_space=pltpu.SMEM)
    smem_next = pl.BlockSpec((None, 1, tm), lambda i: (jnp.minimum(i + 1, n - 1), 0, 0), memory_space=pltpu.SMEM)
    tok = lambda w: pl.BlockSpec((tm, w), lambda i: (i, 0))
    return pl.pallas_call(
        _moe_combine_kernel,
        grid=(n,),
        in_specs=[smem_row, smem_row, smem_next, smem_next, pl.BlockSpec(memory_space=pl.ANY), tok(D_MODEL),
                  pl.BlockSpec((None, None, 1, D_MODEL), lambda i: (row(i), 5, 0, 0)), tok(LANES),
                  pl.BlockSpec((1, D_MODEL), lambda i: (0, 0))],
        out_specs=[pl.BlockSpec((tm, D_MODEL), lambda i: (jnp.minimum(i, n_p - 1), 0)),
                   pl.BlockSpec((tm, D_MODEL), lambda i: (jnp.maximum(i - n_p, 0), 0))],
        out_shape=[jax.ShapeDtypeStruct((T_PROMPT, D_MODEL), F32), jax.ShapeDtypeStruct((T_SAMPLE, D_MODEL), F32)],
        scratch_shapes=[pltpu.VMEM((2, tm * 8, LANES), F32), pltpu.VMEM((2, tm * 8, LANES), F32),
                        pltpu.SemaphoreType.DMA((2, 2))],
        compiler_params=_cparams(("arbitrary",)),
        name="moe_combine",
    )(posa3, posb3, posa3, posb3, ys, x1, mod4, gates, norm_f)


def _moe_call(h2t, x1, mod4, gates, gates_t, wg, wu, wd, norm_f):
    posa, posb, te, ti, misc = _moe_plan_call(gates_t)
    xs = _moe_scatter_call(misc, posa.reshape(T_ALL // SCATTER_TM, 1, SCATTER_TM),
                           posb.reshape(T_ALL // SCATTER_TM, 1, SCATTER_TM), h2t)
    ys = _ffn_grouped_call(te.reshape(LANES), ti.reshape(LANES), misc.reshape(LANES), xs, wg, wu, wd)
    return _moe_combine_call(posa.reshape(T_ALL // COMBINE_TM, 1, COMBINE_TM),
                             posb.reshape(T_ALL // COMBINE_TM, 1, COMBINE_TM), ys, x1, mod4, gates, norm_f)


def kernel(x_prompt, x_sample, cache_k, cache_v, state_gla_fwd, state_gla_bwd, c, c_ctx, w_ada, b_ada, norm_mix, norm_ffn, w_in, w_out, gla_w_up, gla_b_up, gla_norm, diff_lambda, diff_norm, sgu_w, sgu_b, ffn_w_gate, ffn_w_up, ffn_w_down, router_w, moe_w_gate, moe_w_up, moe_w_down, norm_f):
    assert DEPTH == 2
    x_pieces = [x_prompt.reshape(T_PROMPT, D_MODEL), x_sample.reshape(T_SAMPLE, D_MODEL)]
    cvecs = jnp.concatenate([c_ctx[None, :], c, jnp.zeros((N_MOD_ROWS - 1 - DEC_BATCH, D_MODEL), F32)], axis=0)
    mod = _ada_call(cvecs, w_ada, b_ada)
    cos, sin_signed = _rope_tables()
    zeros_state = jnp.zeros((BATCH, 64, 256), F32)

    w_in_t = jnp.swapaxes(w_in, 1, 2)
    moe_w = [moe_w_gate[0], moe_w_up[0], moe_w_down[0]]
    for l in range(DEPTH):
        mod4 = mod[l].reshape(N_MOD_ROWS, 6, 1, D_MODEL)
        w_up = jnp.zeros((LANES, 2 * W_A), F32)
        w_up = w_up.at[0:GLA_RANK, 0:W_A].set(gla_w_up[l, 0]).at[GLA_RANK:2 * GLA_RANK, W_A:].set(gla_w_up[l, 1])
        b_up = gla_b_up[l].reshape(1, 2 * W_A)
        bs_full = jnp.repeat(sgu_b[l].T, DG_C, axis=1)
        g4, la, qb, kb, vb, oc, moe_w[l] = _in_proj_call(x_pieces, norm_mix[l][None, :], mod4, w_in_t, l,
                                                         w_up.astype(BF16), b_up, sgu_w[l].astype(BF16), bs_full,
                                                         moe_w[l])

        gain_a = gla_norm[l][None, :]
        if l == 0:
            oa_p, sf0, sb0 = _gla_call(g4, la, zeros_state, zeros_state, gain_a, batch=BATCH, seq=SEQ, row_block0=0)
            kb0, vb0 = kb, vb
        else:
            oa_p, new_sf, new_sb, new_cache_k = _gla_call(g4, la, zeros_state, zeros_state, gain_a, (sf0, sb0, kb0, kb),
                                                          batch=BATCH, seq=SEQ, row_block0=0)
        oa_s, _, _ = _gla_call(g4, la, _state_to_kernel(state_gla_fwd[:, l]),
                               _state_to_kernel(state_gla_bwd[:, l]), gain_a,
                               batch=DEC_BATCH, seq=DEC_SEQ, row_block0=T_PROMPT // DEC_SEQ)

        lam_init = 0.8 - 0.6 * math.exp(-0.3 * l)
        gain_b = diff_norm[l][None, :]
        if l == 0:
            (ob_p,) = _attn_prompt_call(diff_lambda[l], qb, kb, vb, gain_b, lam_init=lam_init)
        else:
            ob_p, new_cache_v = _attn_prompt_call(diff_lambda[l], qb, kb, vb, gain_b, (vb0,),
                                                  lam_init=lam_init, write_cache=True)
        ob_s, *dense_w = _attn_sample_call(diff_lambda[l], qb, kb, vb,
                                           cache_k[:, l].reshape(DEC_BATCH, PAST_LEN, W_B),
                                           cache_v[:, l].reshape(DEC_BATCH, PAST_LEN, W_B),
                                           cos, sin_signed, gain_b,
                                           (ffn_w_gate[0], ffn_w_up[0], ffn_w_down[0]) if l == 0 else (),
                                           lam_init=lam_init)

        if l == 0:
            x1, h2 = _out_proj_call([oa_p, oa_s], [ob_p, ob_s], oc, x_pieces, w_out, l, mod4, norm_ffn[l][None, :])
            x_next, moe_w[2] = _ffn_call(h2, x1, mod4, *dense_w, moe_w[2])
            x_pieces = [x_next]
        else:
            rw = jnp.pad(router_w[0], ((0, 0), (0, LANES - N_EXPERTS)))
            x1, h2t, gates, gates_t = _out_proj_call([oa_p, oa_s], [ob_p, ob_s], oc, x_pieces, w_out, l, mod4,
                                                     norm_ffn[l][None, :], rw)
            y_prompt, y_sample = _moe_call(h2t, x1, mod4, gates, gates_t, *moe_w, norm_f[None, :])

    state_shape = (BATCH, DEPTH, N_HEADS_A, DK_A, DK_A)
    return (y_prompt.reshape(BATCH, SEQ, D_MODEL), y_sample.reshape(DEC_BATCH, DEC_SEQ, D_MODEL),
            new_cache_k.reshape(BATCH, DEPTH, SEQ, N_HEADS_B, DV_B), new_cache_v.reshape(BATCH, DEPTH, SEQ, N_HEADS_B, DV_B),
            new_sf.reshape(state_shape), new_sb.reshape(state_shape))
```

```python
import functools
import math

import jax
import jax.numpy as jnp
import numpy as np
from jax import lax
from jax.experimental import pallas as pl
from jax.experimental.pallas import tpu as pltpu

F32 = jnp.float32
BF16 = jnp.bfloat16

D_MODEL = 1024
BATCH = 32
SEQ = 256
DEPTH = 2
DEC_BATCH = 2
DEC_SEQ = 1024
PAST_LEN = 256
GRID_W = 64
N_HEADS_A = 4
DK_A = 64
W_A = 256
GLA_RANK = 16
GLA_TAU = 16.0
GLA_CHUNK = 64
N_HEADS_B = 4
DH_B = 64
DV_B = 128
W_B = 512
ROPE_THETA = 10000.0
AXIS_PAIRS = DH_B // 4
N_GROUPS_C = 4
DG_C = 64
W_C = 256
SGU_CHUNK = 128
D_FF = 2816
N_EXPERTS = 8
EPS = 1e-6

T_PROMPT = BATCH * SEQ
T_SAMPLE = DEC_BATCH * DEC_SEQ
T_ALL = T_PROMPT + T_SAMPLE
N_MOD_ROWS = 8
LANES = 128
VMEM_LIMIT = 56 * 1024 * 1024


def _cparams(sem):
    return pltpu.CompilerParams(dimension_semantics=sem, vmem_limit_bytes=VMEM_LIMIT)


def _dot(a, b):
    return jnp.dot(a, b, preferred_element_type=F32)


def _dot_nt(a, b):
    return lax.dot_general(a, b, (((1,), (1,)), ((), ())), preferred_element_type=F32)


def _dot_tn(a, b):
    return lax.dot_general(a, b, (((0,), (0,)), ((), ())), preferred_element_type=F32)


def _split_bf16(x):
    hi = x.astype(BF16)
    lo = (x - hi.astype(F32)).astype(BF16)
    return hi, lo


def _dot3(a, w):
    a_hi, a_lo = _split_bf16(a)
    w_hi, w_lo = _split_bf16(w)
    return _dot(a_hi, w_hi) + (_dot(a_lo, w_hi) + _dot(a_hi, w_lo))


def _sigmoid(x):
    return 1.0 / (1.0 + jnp.exp(-x))


def _silu(x):
    return x * _sigmoid(x)


def _gelu_tanh(x):
    c = math.sqrt(2.0 / math.pi)
    return x * (0.5 * (1.0 + jnp.tanh(c * (x + 0.044715 * (x * x * x)))))


def _log_sigmoid(x):
    return jnp.minimum(x, 0.0) - jnp.log(1.0 + jnp.exp(-jnp.abs(x)))


def _mod_row(i, tm):
    n_p = T_PROMPT // tm
    per_b = DEC_SEQ // tm
    return jnp.where(i < n_p, 0, 1 + (i - n_p) // per_b)


ADA_TN = 1536


def _ada_kernel(c_ref, w_ref, b_ref, o_ref):
    a = _silu(c_ref[...])
    o_ref[...] = _dot3(a, w_ref[...]) + b_ref[...]


def _ada_call(cvecs, w_ada, b_ada):
    n_col = (6 * D_MODEL) // ADA_TN
    return pl.pallas_call(
        _ada_kernel,
        grid=(DEPTH, n_col),
        in_specs=[
            pl.BlockSpec((N_MOD_ROWS, D_MODEL), lambda l, j: (0, 0)),
            pl.BlockSpec((None, D_MODEL, ADA_TN), lambda l, j: (l, 0, j)),
            pl.BlockSpec((None, 1, ADA_TN), lambda l, j: (l, 0, j)),
        ],
        out_specs=pl.BlockSpec((None, N_MOD_ROWS, ADA_TN), lambda l, j: (l, 0, j)),
        out_shape=jax.ShapeDtypeStruct((DEPTH, N_MOD_ROWS, 6 * D_MODEL), F32),
        compiler_params=_cparams(("arbitrary", "arbitrary")),
        name="ada_mod",
    )(cvecs, w_ada, b_ada.reshape(DEPTH, 1, 6 * D_MODEL))


IN_TM = 512
IN_COLS = 3104
Z_COL0 = 1024
Z_COLS = 2 * GLA_RANK
W_MAIN = 3072


def _piece_specs(pieces, tm, width):
    specs, t0 = [], 0
    for arr in pieces:
        nt = arr.shape[0] // tm
        specs.append(pl.BlockSpec((tm, width), lambda i, t0=t0, nt=nt: (jnp.clip(i - t0, 0, nt - 1), 0)))
        t0 += nt
    assert t0 * tm == T_ALL and len(pieces) in (1, 2) and (len(pieces) == 1 or pieces[0].shape[0] == T_PROMPT)
    return specs


def _pick_piece(refs, tm, rows=slice(None)):
    if len(refs) == 1:
        return refs[0][rows, :]
    return jnp.where(pl.program_id(0) < T_PROMPT // tm, refs[0][rows, :], refs[1][rows, :])


RIDER_BLOCKS_20 = 16


W_STAGE_ROWS = 512


def _in_proj_kernel(*refs, n_x, layer):
    x_refs = refs[:n_x]
    (nrm_ref, sh_ref, sc_ref, w_hbm, wup_ref, bup_ref, ws_ref, bs_ref, cast_ref,
     g4_ref, la_ref, qb_ref, kb_ref, vb_ref, oc_ref, cast_out_ref, wm_ref, wz_ref, stage_ref, sem) = refs[n_x:]

    @pl.when(pl.program_id(0) == 0)
    def _():
        n = W_STAGE_ROWS
        moves = [(src, src if src < Z_COL0 else src - Z_COLS)
                 for src in (*range(0, Z_COL0, n), *range(Z_COL0 + Z_COLS, IN_COLS, n))]
        copy = lambda k: pltpu.make_async_copy(w_hbm.at[layer, pl.ds(moves[k][0], n), :], stage_ref.at[k % 2],
                                               sem.at[k % 2])
        copy(0).start()
        for k, (_, dst) in enumerate(moves):
            if k + 1 < len(moves):
                copy(k + 1).start()
            copy(k).wait()
            wm_ref[dst:dst + n, :] = stage_ref[k % 2].astype(BF16)
        gate_rows = pltpu.make_async_copy(w_hbm.at[layer, pl.ds(Z_COL0, Z_COLS), :],
                                          stage_ref.at[0, pl.ds(0, Z_COLS), :], sem.at[0])
        gate_rows.start()
        gate_rows.wait()
        wz_ref[...] = jnp.zeros_like(wz_ref)
        wz_ref[0:Z_COLS, :] = stage_ref[0, 0:Z_COLS, :].astype(BF16)

    x = _pick_piece(x_refs, IN_TM)
    y = x * lax.rsqrt(jnp.mean(x * x, axis=-1, keepdims=True) + EPS)
    h = (y * nrm_ref[...]) * (1.0 + sc_ref[...]) + sh_ref[...]
    hb = h.astype(BF16)
    cast_rows = cast_ref.shape[0] // 8

    def plain_job(k, ref, c0, w0):
        def job():
            ref[:, c0:c0 + MXU_N] = _dot_nt(hb, wm_ref[w0:w0 + MXU_N, :])
            if k < 8:
                rows = slice(k * cast_rows, (k + 1) * cast_rows)
                cast_out_ref[rows, :] = cast_ref[rows, :].astype(BF16)
        return job

    jobs = [plain_job(k, *a) for k, a in enumerate(
        (ref, c, w0 + c) for ref, w0, width in
        ((g4_ref, 0, 1024), (qb_ref, 1024, 512), (kb_ref, 1536, 512), (vb_ref, 2048, 512))
        for c in range(0, width, MXU_N))]
    z = _dot_nt(hb, wz_ref[...])
    jobs.pop(0)()
    zz = _dot(z.astype(BF16), wup_ref[...]) + bup_ref[...]
    uv = _dot_nt(hb, wm_ref[2560:3072, :])
    jobs.pop(0)()
    la_ref[...] = _log_sigmoid(zz) * (1.0 / GLA_TAU)
    jobs.pop(0)()
    oc_ref[...] = _sgu(uv, ws_ref, bs_ref, jobs)


def _resident_layer(a, l):
    return pl.BlockSpec((None,) + a.shape[1:], lambda *_: (l,) + (0,) * (a.ndim - 1), pipeline_mode=pl.Buffered(1))


def _in_proj_call(x_pieces, norm_g, mod4, w_in_t, l, w_up, b_up, sgu_ws, sgu_bs, cast_w):
    tm = IN_TM
    rider_spec, rider_shape = _cast_rider(cast_w, RIDER_BLOCKS_20)
    n = T_ALL // tm
    row = functools.partial(_mod_row, tm=tm)
    mod_spec = lambda k: pl.BlockSpec((None, None, 1, D_MODEL), lambda i: (row(i), k, 0, 0))
    full = lambda a: pl.BlockSpec(a.shape, lambda i: (0,) * a.ndim)
    out = lambda w: pl.BlockSpec((tm, w), lambda i: (i, 0))
    return pl.pallas_call(
        functools.partial(_in_proj_kernel, n_x=len(x_pieces), layer=l),
        grid=(n,),
        in_specs=_piece_specs(x_pieces, tm, D_MODEL) + [full(norm_g), mod_spec(0), mod_spec(1), pl.BlockSpec(memory_space=pl.ANY),
                                                        full(w_up), full(b_up), full(sgu_ws), full(sgu_bs),
                                                        rider_spec],
        out_specs=[out(1024), out(512), out(512), out(512), out(512), out(W_C), rider_spec],
        out_shape=[jax.ShapeDtypeStruct((T_ALL, w), F32) for w in (1024, 512, 512, 512, 512)]
        + [jax.ShapeDtypeStruct((T_ALL, W_C), BF16), rider_shape],
        scratch_shapes=[pltpu.VMEM((W_MAIN, D_MODEL), BF16), pltpu.VMEM((LANES, D_MODEL), BF16),
                        pltpu.VMEM((2, W_STAGE_ROWS, D_MODEL), F32), pltpu.SemaphoreType.DMA((2,))],
        compiler_params=_cparams(("arbitrary",)),
        name="in_proj",
    )(*x_pieces, norm_g, mod4, mod4, w_in_t, w_up, b_up, sgu_ws, sgu_bs, cast_w)


GLA_SB = 256
GLA_NC = GLA_SB // GLA_CHUNK


def _head_blocks(x, same64):
    return jnp.where(same64, jnp.concatenate([x] * N_HEADS_A, axis=0), 0.0).astype(BF16)


def _gla_superblock(q, k, vb, v_blocks, la, st_all, tri, mask4, same64, forward):
    c = GLA_CHUNK
    mid, last = (c // 2 - 1, c - 1) if forward else (c // 2, 0)
    la_hi, la_lo = _split_bf16(la)
    b = _dot(tri, la_hi) + _dot(tri, la_lo)
    yield
    rows_of = lambda r: jnp.concatenate(
        [jnp.broadcast_to(b[i * c + r:i * c + r + 1, :], (c, W_A)) for i in range(GLA_NC)], axis=0)
    m = rows_of(mid)
    bl = rows_of(last)
    qe = (q * jnp.exp(b - m)).astype(BF16)
    ke = k * jnp.exp(m - b)
    qi = (q * jnp.exp(b)).astype(BF16)
    ks = (k * jnp.exp(bl - b)).astype(BF16)
    outs = [None] * GLA_NC
    for i in (range(GLA_NC) if forward else reversed(range(GLA_NC))):
        rows = slice(i * c, (i + 1) * c)
        s = _dot_nt(qe[rows, :], _head_blocks(ke[rows, :], same64))
        kv = _dot_tn(vb[rows, :], ks[rows, :])
        yield
        a = jnp.where(mask4, s, 0.0).astype(BF16)
        outs[i] = _dot(a, v_blocks[i]) + _dot_nt(qi[rows, :], st_all.astype(BF16))
        st_all = st_all * jnp.exp(bl[i * c:i * c + 1, :]) + jnp.where(same64, kv, 0.0)
        yield
    return jnp.concatenate(outs, axis=0), st_all


def _interleave(*gens):
    results = [None] * len(gens)
    active = list(enumerate(gens))
    while active:
        for item in list(active):
            try:
                next(item[1])
            except StopIteration as stop:
                results[item[0]] = stop.value
                active.remove(item)
    return results


def _cast_rider(w, n_blocks):
    e, r, c = w.shape
    per = n_blocks // e
    rows = r // per
    assert per * e == n_blocks and rows * per == r and rows % 16 == 0

    def index(b):
        b = jnp.minimum(b, n_blocks - 1)
        return b // per, b % per, 0

    spec = pl.BlockSpec((None, rows, c), index)
    return spec, jax.ShapeDtypeStruct(w.shape, BF16)


def _gla_kernel(*refs, seq, final):
    if final:
        (g4_ref, la_ref, s0f_ref, s0b_ref, gain_ref, sf_prev_ref, sb_prev_ref, k_prev_ref, k_ref,
         o_ref, sf_ref, sb_ref, ck_ref, of_ref, ob_ref) = refs
        for l, src in enumerate((k_prev_ref, k_ref)):
            for h in range(N_HEADS_B):
                ck_ref[l, pl.ds(h, SEQ, stride=N_HEADS_B), :] = src[:, h * DV_B:(h + 1) * DV_B]
    else:
        g4_ref, la_ref, s0f_ref, s0b_ref, gain_ref, o_ref, sf_ref, sb_ref, of_ref, ob_ref = refs
    n = GLA_SB
    nsb = seq // n
    r = lax.broadcasted_iota(jnp.int32, (n, n), 0)
    s = lax.broadcasted_iota(jnp.int32, (n, n), 1)
    same64 = (r // GLA_CHUNK) == (s // GLA_CHUNK)
    lower = same64 & (s <= r)
    upper = same64 & (s >= r)
    tri_f = jnp.where(lower, 1.0, 0.0).astype(BF16)
    tri_b = jnp.where(upper, 1.0, 0.0).astype(BF16)
    ones64 = jnp.where(same64, 1.0, 0.0).astype(BF16)
    key_row = lax.broadcasted_iota(jnp.int32, (GLA_CHUNK, n), 1) % GLA_CHUNK
    qry_row = lax.broadcasted_iota(jnp.int32, (GLA_CHUNK, n), 0)
    mask4_f = key_row <= qry_row
    mask4_b = key_row >= qry_row
    scale = DK_A ** -0.5
    expand = lambda st: jnp.where(same64, jnp.concatenate([st] * N_HEADS_A, axis=0), 0.0)
    compact = lambda st_all: functools.reduce(
        lambda a, b: a + b, [st_all[h * 64:(h + 1) * 64, :] for h in range(N_HEADS_A)])

    def step(i, carry):
        stf, stb = carry
        rf = pl.ds(pl.multiple_of(i * n, n), n)
        rb = pl.ds(pl.multiple_of((nsb - 1 - i) * n, n), n)
        def direction(rows, la_cols, st, tri, mask4, forward):
            q = g4_ref[rows, 0:256] * scale
            k = g4_ref[rows, 256:512]
            v = g4_ref[rows, 512:768]
            v_blocks = [_head_blocks(v[i * GLA_CHUNK:(i + 1) * GLA_CHUNK, :], same64) for i in range(GLA_NC)]
            return _gla_superblock(q, k, v.astype(BF16), v_blocks, la_ref[rows, la_cols], st, tri, mask4, same64,
                                   forward)

        (o_f, stf), (o_b, stb) = _interleave(direction(rf, slice(0, 256), stf, tri_f, mask4_f, True),
                                             direction(rb, slice(256, 512), stb, tri_b, mask4_b, False))
        of_ref[rf, :] = o_f
        ob_ref[rb, :] = o_b
        return stf, stb

    stf, stb = lax.fori_loop(0, nsb, step, (expand(s0f_ref[...]), expand(s0b_ref[...])))
    if final:
        sf_ref[0] = sf_prev_ref[...].T
        sb_ref[0] = sb_prev_ref[...].T
        sf_ref[1] = compact(stf).T
        sb_ref[1] = compact(stb).T
    else:
        sf_ref[...] = compact(stf)
        sb_ref[...] = compact(stb)

    gain = gain_ref[...]

    def finish(i, carry):
        rows = pl.ds(pl.multiple_of(i * n, n), n)
        o = of_ref[rows, :] + ob_ref[rows, :]
        sq_hi, sq_lo = _split_bf16(o * o)
        ms = (_dot(sq_hi, ones64) + _dot(sq_lo, ones64)) * (1.0 / DK_A)
        y = (o * lax.rsqrt(ms + EPS)) * gain
        o_ref[rows, :] = (y * _silu(g4_ref[rows, 768:1024])).astype(BF16)
        return carry

    lax.fori_loop(0, nsb, finish, 0)


def _gla_call(g4, la, s0f, s0b, gain, prev=None, *, batch, seq, row_block0):
    tok = lambda w: pl.BlockSpec((seq, w), lambda b: (row_block0 + b, 0))
    st = pl.BlockSpec((None, 64, 256), lambda b: (b, 0, 0))
    in_specs = [tok(1024), tok(512), st, st, pl.BlockSpec((1, W_A), lambda b: (0, 0))]
    args = [g4, la, s0f, s0b, gain]
    o_spec = pl.BlockSpec((seq, W_A), lambda b: (b, 0))
    o_shape = jax.ShapeDtypeStruct((batch * seq, W_A), BF16)
    if prev is None:
        out_specs = [o_spec, st, st]
        out_shape = [o_shape, jax.ShapeDtypeStruct((batch, 64, 256), F32), jax.ShapeDtypeStruct((batch, 64, 256), F32)]
    else:
        assert seq == SEQ and batch == BATCH and DEPTH == 2
        kblk = pl.BlockSpec((SEQ, W_B), lambda b: (b, 0))
        in_specs += [st, st, kblk, kblk]
        args += list(prev)
        st2 = pl.BlockSpec((None, DEPTH, 256, 64), lambda b: (b, 0, 0, 0))
        st2_shape = jax.ShapeDtypeStruct((batch, DEPTH, 256, 64), F32)
        out_specs = [o_spec, st2, st2, pl.BlockSpec((None, DEPTH, SEQ * N_HEADS_B, DV_B), lambda b: (b, 0, 0, 0))]
        out_shape = [o_shape, st2_shape, st2_shape, jax.ShapeDtypeStruct((batch, DEPTH, SEQ * N_HEADS_B, DV_B), F32)]
    return pl.pallas_call(
        functools.partial(_gla_kernel, seq=seq, final=prev is not None),
        grid=(batch,),
        in_specs=in_specs,
        out_specs=out_specs,
        out_shape=out_shape,
        scratch_shapes=[pltpu.VMEM((seq, W_A), F32), pltpu.VMEM((seq, W_A), F32)],
        compiler_params=_cparams(("arbitrary",)),
        name=f"gla_{seq}",
    )(*args)


def _state_to_kernel(s):
    b = s.shape[0]
    return jnp.transpose(s, (0, 3, 1, 2)).reshape(b, 64, 256)


def _lambda(lv, lam_init):
    l01 = jnp.sum(lv[0:1, :] * lv[1:2, :], axis=-1, keepdims=True)
    l23 = jnp.sum(lv[2:3, :] * lv[3:4, :], axis=-1, keepdims=True)
    return jnp.exp(l01) - jnp.exp(l23) + lam_init


def _softmax_parts(parts):
    mx = functools.reduce(jnp.maximum, [jnp.max(p, axis=-1, keepdims=True) for p in parts])
    es = [jnp.exp(p - mx) for p in parts]
    den = functools.reduce(lambda a, b: a + b, [jnp.sum(e, axis=-1, keepdims=True) for e in es])
    return [e / den for e in es]


def _diff_finish(o, gain, lam_init):
    o = o * lax.rsqrt(jnp.mean(o * o, axis=-1, keepdims=True) + EPS)
    return ((o * gain) * (1.0 - lam_init)).astype(BF16)


QK_SCALE = DH_B ** -0.5


def _key_halves(k):
    first = lax.broadcasted_iota(jnp.int32, k.shape, 1) < DH_B
    return jnp.where(first, k, 0.0).astype(BF16), jnp.where(first, 0.0, k).astype(BF16)


def _attn_prompt_kernel(lv_ref, q_ref, k_ref, v_ref, gain_ref, *rest, lam_init, n_prev):
    prev_refs, (o_ref, *cache_refs) = rest[:n_prev], rest[n_prev:]
    lam = _lambda(lv_ref[...], lam_init)

    def head(h):
        cols = slice(h * DV_B, (h + 1) * DV_B)
        q = (q_ref[:, cols] * QK_SCALE).astype(BF16)
        k1, k2 = _key_halves(k_ref[:, cols])
        s1 = _dot_nt(q, k1)
        s2 = _dot_nt(q, k2)
        yield
        (p1,) = _softmax_parts([s1])
        (p2,) = _softmax_parts([s2])
        a = p1 - lam * p2
        o = _dot(a.astype(BF16), v_ref[:, cols].astype(BF16))
        yield
        o_ref[:, cols] = _diff_finish(o, gain_ref[:, cols], lam_init)
        if cache_refs:
            (cv_ref,) = cache_refs
            for l, src in enumerate((*prev_refs, v_ref)):
                cv_ref[l, pl.ds(h, SEQ, stride=N_HEADS_B), :] = src[:, cols]

    _interleave(*[head(h) for h in range(N_HEADS_B)])


def _attn_prompt_call(lv, qb, kb, vb, gain, prev_v=(), *, lam_init, write_cache=False):
    blk = pl.BlockSpec((SEQ, W_B), lambda b: (b, 0))
    n_prev = len(prev_v)
    assert write_cache or not prev_v
    in_specs = [pl.BlockSpec((4, DH_B), lambda b: (0, 0)), blk, blk, blk,
                pl.BlockSpec((1, W_B), lambda b: (0, 0))] + [blk] * n_prev
    args = [lv, qb, kb, vb, gain, *prev_v]
    out_specs = [blk]
    out_shape = [jax.ShapeDtypeStruct((T_PROMPT, W_B), BF16)]
    if write_cache:
        out_specs.append(pl.BlockSpec((None, n_prev + 1, SEQ * N_HEADS_B, DV_B), lambda b: (b, 0, 0, 0)))
        out_shape.append(jax.ShapeDtypeStruct((BATCH, n_prev + 1, SEQ * N_HEADS_B, DV_B), F32))
    return pl.pallas_call(
        functools.partial(_attn_prompt_kernel, lam_init=lam_init, n_prev=n_prev),
        grid=(BATCH,),
        in_specs=in_specs,
        out_specs=out_specs,
        out_shape=out_shape,
        compiler_params=_cparams(("arbitrary",)),
        name="diff_attn_prompt",
    )(*args)


def _rope(x, cos, sin_signed):
    lane = lax.broadcasted_iota(jnp.int32, x.shape, 1)
    first = (lane % (2 * AXIS_PAIRS)) < AXIS_PAIRS
    partner = jnp.where(first, pltpu.roll(x, LANES - AXIS_PAIRS, 1), pltpu.roll(x, AXIS_PAIRS, 1))
    return x * cos + partner * sin_signed


ATT_TQ = 256


def _attn_sample_kernel(lv_ref, q_ref, k_ref, v_ref, kc_ref, vc_ref, cosq_ref, sinq_ref,
                        cosk_ref, sink_ref, gain_ref, *rest, lam_init, n_cast):
    w_refs, o_ref, wb_refs, (k1_ref, k2_ref) = rest[:n_cast], rest[n_cast], rest[n_cast + 1:2 * n_cast + 1], rest[-2:]
    for w_ref, wb_ref in zip(w_refs, wb_refs):
        wb_ref[...] = w_ref[...].astype(BF16)

    @pl.when(pl.program_id(1) == 0)
    def _():
        for h in range(N_HEADS_B):
            cols = slice(h * DV_B, (h + 1) * DV_B)
            k1_ref[:, cols], k2_ref[:, cols] = _key_halves(_rope(k_ref[:, cols], cosk_ref[...], sink_ref[...]))

    lam = _lambda(lv_ref[...], lam_init)

    def head(h):
        cols = slice(h * DV_B, (h + 1) * DV_B)
        q = (_rope(q_ref[:, cols], cosq_ref[...], sinq_ref[...]) * QK_SCALE).astype(BF16)
        c1, c2 = _key_halves(kc_ref[:, cols])
        s1 = [_dot_nt(q, k1_ref[:, cols]), _dot_nt(q, c1)]
        s2 = [_dot_nt(q, k2_ref[:, cols]), _dot_nt(q, c2)]
        yield
        p1 = _softmax_parts(s1)
        p2 = _softmax_parts(s2)
        a_own = p1[0] - lam * p2[0]
        a_ctx = p1[1] - lam * p2[1]
        o = (_dot(a_own.astype(BF16), v_ref[:, cols].astype(BF16))
             + _dot(a_ctx.astype(BF16), vc_ref[:, cols].astype(BF16)))
        yield
        o_ref[:, cols] = _diff_finish(o, gain_ref[:, cols], lam_init)

    _interleave(*[head(h) for h in range(N_HEADS_B)])


def _attn_sample_call(lv, qb, kb, vb, kc, vc, cos, sin_signed, gain, cast_ws=(), *, lam_init):
    tq = ATT_TQ
    nq = DEC_SEQ // tq
    p0 = T_PROMPT // tq
    s0 = T_PROMPT // DEC_SEQ
    qblk = pl.BlockSpec((tq, W_B), lambda b, t: (p0 + b * nq + t, 0))
    kvblk = pl.BlockSpec((DEC_SEQ, W_B), lambda b, t: (s0 + b, 0))
    cblk = pl.BlockSpec((None, PAST_LEN, W_B), lambda b, t: (b, 0, 0))
    n_steps = DEC_BATCH * nq
    cast_specs = []
    for w in cast_ws:
        assert w.shape[0] % (16 * n_steps) == 0
        cast_specs.append(pl.BlockSpec((w.shape[0] // n_steps, w.shape[1]), lambda b, t: (b * nq + t, 0)))
    return pl.pallas_call(
        functools.partial(_attn_sample_kernel, lam_init=lam_init, n_cast=len(cast_ws)),
        grid=(DEC_BATCH, nq),
        in_specs=[pl.BlockSpec((4, DH_B), lambda b, t: (0, 0)), qblk, kvblk, kvblk, cblk, cblk,
                  pl.BlockSpec((tq, DV_B), lambda b, t: (t, 0)),
                  pl.BlockSpec((tq, DV_B), lambda b, t: (t, 0)),
                  pl.BlockSpec((DEC_SEQ, DV_B), lambda b, t: (0, 0)),
                  pl.BlockSpec((DEC_SEQ, DV_B), lambda b, t: (0, 0)),
                  pl.BlockSpec((1, W_B), lambda b, t: (0, 0))] + cast_specs,
        out_specs=[pl.BlockSpec((tq, W_B), lambda b, t: (b * nq + t, 0))] + cast_specs,
        out_shape=[jax.ShapeDtypeStruct((T_SAMPLE, W_B), BF16)] + [jax.ShapeDtypeStruct(w.shape, BF16) for w in cast_ws],
        scratch_shapes=[pltpu.VMEM((DEC_SEQ, W_B), BF16), pltpu.VMEM((DEC_SEQ, W_B), BF16)],
        compiler_params=_cparams(("arbitrary", "arbitrary")),
        name="diff_attn_sample",
    )(lv, qb, kb, vb, kc, vc, cos, sin_signed, cos, sin_signed, gain, *cast_ws)


def _rope_tables():
    rows = DEC_SEQ // GRID_W
    row = jnp.repeat(jnp.arange(rows, dtype=F32), GRID_W)
    col = jnp.tile(jnp.arange(GRID_W, dtype=F32), rows)
    freqs = ROPE_THETA ** (-jnp.arange(AXIS_PAIRS, dtype=F32) / AXIS_PAIRS)
    ar, ac = row[:, None] * freqs, col[:, None] * freqs
    cos64 = jnp.concatenate([jnp.cos(ar), jnp.cos(ar), jnp.cos(ac), jnp.cos(ac)], axis=-1)
    sin64 = jnp.concatenate([-jnp.sin(ar), jnp.sin(ar), -jnp.sin(ac), jnp.sin(ac)], axis=-1)
    return jnp.tile(cos64, (1, 2)), jnp.tile(sin64, (1, 2))


def _group_mean(x, ones64):
    hi, lo = _split_bf16(x)
    return (_dot(hi, ones64) + _dot(lo, ones64)) * (1.0 / DG_C)


def _sgu(uv, ws_ref, bs_ref, fillers=()):
    fillers = list(fillers)
    fill = lambda: fillers.pop(0)() if fillers else None
    r = lax.broadcasted_iota(jnp.int32, (W_C, W_C), 0)
    s = lax.broadcasted_iota(jnp.int32, (W_C, W_C), 1)
    ones64 = jnp.where((r // DG_C) == (s // DG_C), 1.0, 0.0).astype(BF16)
    lane = lax.broadcasted_iota(jnp.int32, (SGU_CHUNK, W_C), 1)
    outs = []
    for n in range(uv.shape[0] // SGU_CHUNK):
        rows = slice(n * SGU_CHUNK, (n + 1) * SGU_CHUNK)
        u = _gelu_tanh(uv[rows, 0:256])
        v = _gelu_tanh(uv[rows, 256:512])
        mu = _group_mean(v, ones64)
        fill()
        d = v - mu
        var = _group_mean(d * d, ones64)
        fill()
        vn = d * lax.rsqrt(var + EPS)
        s_mix = bs_ref[...]
        for g in range(N_GROUPS_C):
            vn_g = jnp.where((lane // DG_C) == g, vn, 0.0).astype(BF16)
            s_mix = s_mix + _dot(ws_ref[g], vn_g)
        fill()
        outs.append((u * s_mix).astype(BF16))
    while fillers:
        fill()
    return jnp.concatenate(outs, axis=0)


OUT_TM = 1024
OUT_SLABS = 4


SEL_LANE0 = N_EXPERTS


def _top2_gates(logits):
    lane = lax.broadcasted_iota(jnp.int32, logits.shape, 1).astype(F32)
    neg = -jnp.inf
    lg = jnp.where(lane < N_EXPERTS, logits, neg)
    m1 = jnp.max(lg, axis=-1, keepdims=True)
    i1 = jnp.min(jnp.where(lg == m1, lane, float(LANES)), axis=-1, keepdims=True)
    lg2 = jnp.where(lane == i1, neg, lg)
    m2 = jnp.max(lg2, axis=-1, keepdims=True)
    i2 = jnp.min(jnp.where(lg2 == m2, lane, float(LANES)), axis=-1, keepdims=True)
    e2 = jnp.exp(m2 - m1)
    den = 1.0 + e2
    gates = jnp.where(lane == i1, 1.0 / den, 0.0) + jnp.where(lane == i2, e2 / den, 0.0)
    sel = jnp.where((lane == i1 + SEL_LANE0) | (lane == i2 + SEL_LANE0), 1.0, 0.0)
    return gates + sel


def _store_token_tiles(ref, val, r0=0):
    n = val.shape[0]
    for k in range(D_MODEL // LANES):
        ref[pl.ds(8 * r0 + k, n, stride=8), :] = val[:, k * LANES:(k + 1) * LANES]


def _load_token_tiles(ref, n):
    return jnp.concatenate([ref[pl.ds(k, n, stride=8), :] for k in range(D_MODEL // LANES)], axis=-1)


def _out_proj_kernel(*refs, n_x, moe):
    oa_refs, ob_refs, (oc_ref,), x_refs = refs[0:2], refs[2:4], refs[4:5], refs[5:5 + n_x]
    w_ref, g1_ref, nrm_ref, sc_ref, sh_ref, *rest = refs[5 + n_x:]
    if moe:
        rw_ref, x1_ref, h2t_ref, gates_ref, gates_t_ref, wo_ref = rest
    else:
        x1_ref, h2_ref, wo_ref = rest

    @pl.when(pl.program_id(0) == 0)
    def _():
        wo_ref[...] = w_ref[...].astype(BF16)

    if moe:
        rw_hi, rw_lo = _split_bf16(rw_ref[...])

    def rows_chain(r0, n):
        rs = slice(r0, r0 + n)
        y = (_dot(_pick_piece(oa_refs, OUT_TM, rs), wo_ref[0:256, :])
             + _dot(_pick_piece(ob_refs, OUT_TM, rs), wo_ref[256:768, :]) + _dot(oc_ref[rs, :], wo_ref[768:1024, :]))
        yield
        x1 = _pick_piece(x_refs, OUT_TM, rs) + g1_ref[...] * y
        x1_ref[rs, :] = x1
        yn = x1 * lax.rsqrt(jnp.mean(x1 * x1, axis=-1, keepdims=True) + EPS)
        h = (yn * nrm_ref[...]) * (1.0 + sc_ref[...]) + sh_ref[...]
        if moe:
            _store_token_tiles(h2t_ref, h, r0)
            h_hi, h_lo = _split_bf16(h)
            logits = _dot(h_hi, rw_hi) + (_dot(h_lo, rw_hi) + _dot(h_hi, rw_lo))
            yield
            gates = _top2_gates(logits)
            gates_ref[rs, :] = gates
            gates_t_ref[:, rs] = gates.T
        else:
            h2_ref[rs, :] = h.astype(BF16)

    n_slab = OUT_TM // OUT_SLABS
    _interleave(*[rows_chain(i * n_slab, n_slab) for i in range(OUT_SLABS)])


def _out_proj_call(oa_pieces, ob_pieces, oc, x_pieces, w_out, l, mod4, norm_g, router_w=None):
    tm = OUT_TM
    n = T_ALL // tm
    moe = router_w is not None
    row = functools.partial(_mod_row, tm=tm)
    mod_spec = lambda k: pl.BlockSpec((None, None, 1, D_MODEL), lambda i: (row(i), k, 0, 0))
    tok = lambda w: pl.BlockSpec((tm, w), lambda i: (i, 0))
    full = lambda a: pl.BlockSpec(a.shape, lambda i: (0,) * a.ndim)
    in_specs = (_piece_specs(oa_pieces, tm, W_A) + _piece_specs(ob_pieces, tm, W_B) + [tok(W_C)]
                + _piece_specs(x_pieces, tm, D_MODEL)
                + [_resident_layer(w_out, l), mod_spec(2), full(norm_g), mod_spec(4), mod_spec(3)])
    args = [*oa_pieces, *ob_pieces, oc, *x_pieces, w_out, mod4, norm_g, mod4, mod4]
    if moe:
        in_specs.append(full(router_w))
        args.append(router_w)
        out_specs = [tok(D_MODEL), pl.BlockSpec((tm * 8, LANES), lambda i: (i, 0)), tok(LANES),
                     pl.BlockSpec((LANES, tm), lambda i: (0, i))]
        out_shape = [jax.ShapeDtypeStruct((T_ALL, D_MODEL), F32), jax.ShapeDtypeStruct((T_ALL * 8, LANES), F32),
                     jax.ShapeDtypeStruct((T_ALL, LANES), F32), jax.ShapeDtypeStruct((LANES, T_ALL), F32)]
    else:
        out_specs = [tok(D_MODEL), tok(D_MODEL)]
        out_shape = [jax.ShapeDtypeStruct((T_ALL, D_MODEL), F32), jax.ShapeDtypeStruct((T_ALL, D_MODEL), BF16)]
    return pl.pallas_call(
        functools.partial(_out_proj_kernel, n_x=len(x_pieces), moe=moe),
        grid=(n,),
        in_specs=in_specs,
        out_specs=out_specs,
        out_shape=out_shape,
        scratch_shapes=[pltpu.VMEM((D_MODEL, D_MODEL), BF16)],
        compiler_params=_cparams(("arbitrary",)),
        name="out_proj_moe" if moe else "out_proj",
    )(*args)


FFN_TM = 512
MXU_N = 256
FFN_SPLITS = (0, 1024, 2048, D_FF)
assert all(s % MXU_N == 0 for s in FFN_SPLITS)


def _swiglu(h, wg_ref, wu_ref, wd_ref, side_jobs=()):
    side_jobs = list(side_jobs)
    out = None
    for c0, c1 in zip(FFN_SPLITS[:-1], FFN_SPLITS[1:]):
        act = _silu(_dot(h, wg_ref[:, c0:c1])) * _dot(h, wu_ref[:, c0:c1])
        if side_jobs:
            side_jobs.pop(0)()
        d = _dot(act.astype(BF16), wd_ref[c0:c1, :])
        out = d if out is None else out + d
    return out


def _ffn_kernel(h_ref, x_ref, g2_ref, wg_ref, wu_ref, wd_ref, w_ref, o_ref, wb_ref):
    n = w_ref.shape[0]
    cuts = [0, n // 3 // 16 * 16, 2 * (n // 3 // 16 * 16), n]

    def cast_job(r0, r1):
        def job():
            wb_ref[r0:r1, :] = w_ref[r0:r1, :].astype(BF16)
        return job

    jobs = [cast_job(r0, r1) for r0, r1 in zip(cuts[:-1], cuts[1:])]
    o_ref[...] = x_ref[...] + g2_ref[...] * _swiglu(h_ref[...], wg_ref, wu_ref, wd_ref, jobs)


def _ffn_call(h2, x1, mod4, wg, wu, wd, cast_w):
    tm = FFN_TM
    row = functools.partial(_mod_row, tm=tm)
    tok = lambda w: pl.BlockSpec((tm, w), lambda i: (i, 0))
    resident = lambda a: pl.BlockSpec(a.shape, lambda i: (0, 0), pipeline_mode=pl.Buffered(1))
    rider_spec, rider_shape = _cast_rider(cast_w, RIDER_BLOCKS_20)
    return pl.pallas_call(
        _ffn_kernel,
        grid=(T_ALL // tm,),
        in_specs=[tok(D_MODEL), tok(D_MODEL),
                  pl.BlockSpec((None, None, 1, D_MODEL), lambda i: (row(i), 5, 0, 0)),
                  resident(wg), resident(wu), resident(wd), rider_spec],
        out_specs=[tok(D_MODEL), rider_spec],
        out_shape=[jax.ShapeDtypeStruct((T_ALL, D_MODEL), F32), rider_shape],
        compiler_params=_cparams(("arbitrary",)),
        name="ffn_dense",
    )(h2, x1, mod4, wg, wu, wd, cast_w)


MOE_TM = 512
MOE_NT_MAX = (2 * T_ALL) // MOE_TM + N_EXPERTS
MOE_ROWS = MOE_NT_MAX * MOE_TM
PLAN_BLK = 512
MISC_LAST_START = 8
MISC_NT = 16


def _moe_plan_kernel(gt_ref, posa_ref, posb_ref, te_ref, ti_ref, misc_ref):
    tm = float(MOE_TM)
    sel = gt_ref[SEL_LANE0:SEL_LANE0 + N_EXPERTS, :]
    cnt = jnp.sum(sel, axis=1, keepdims=True)
    nt = jnp.floor((cnt + (tm - 1.0)) * (1.0 / tm))
    sub = lax.broadcasted_iota(jnp.int32, (N_EXPERTS, LANES), 0).astype(F32)
    lane = lax.broadcasted_iota(jnp.int32, (N_EXPERTS, LANES), 1).astype(F32)
    nt_b = jnp.broadcast_to(nt, (N_EXPERTS, LANES))
    nt_row = jnp.sum(jnp.where(sub == lane, nt_b, 0.0), axis=0, keepdims=True)
    toff = jnp.sum(jnp.where(lane < sub, jnp.broadcast_to(nt_row, (N_EXPERTS, LANES)), 0.0),
                   axis=1, keepdims=True)
    tend = toff + nt
    n_total = jnp.sum(nt, axis=0, keepdims=True)
    jc = jnp.minimum(lane, n_total - 1.0)
    te = jnp.sum(jnp.where(jc >= tend, 1.0, 0.0), axis=0, keepdims=True)
    te_ref[...] = te.astype(jnp.int32)
    ti_ref[...] = jc[0:1, :].astype(jnp.int32)
    last_start = (tend - 1.0) * tm
    ls_row = jnp.sum(jnp.where(sub + MISC_LAST_START == lane, jnp.broadcast_to(last_start, (N_EXPERTS, LANES)), 0.0),
                     axis=0, keepdims=True)
    nt_row2 = jnp.sum(jnp.where(sub + MISC_NT == lane, nt_b, 0.0), axis=0, keepdims=True)
    misc = jnp.where(lane[0:1, :] == 0.0, n_total, 0.0) + ls_row + nt_row2
    misc_ref[...] = misc.astype(jnp.int32)

    off = toff * tm
    r = lax.broadcasted_iota(jnp.int32, (PLAN_BLK, PLAN_BLK), 0)
    c = lax.broadcasted_iota(jnp.int32, (PLAN_BLK, PLAN_BLK), 1)
    upper = jnp.where(r <= c, 1.0, 0.0).astype(BF16)
    carry = jnp.zeros((N_EXPERTS, 1), F32)
    for blk in range(T_ALL // PLAN_BLK):
        cols = slice(blk * PLAN_BLK, (blk + 1) * PLAN_BLK)
        s = gt_ref[SEL_LANE0:SEL_LANE0 + N_EXPERTS, cols]
        rank = _dot(s.astype(BF16), upper) + carry
        pos = off + rank - 1.0
        posa_ref[:, cols] = jnp.min(jnp.where(s > 0.0, pos, 1e9), axis=0, keepdims=True).astype(jnp.int32)
        posb_ref[:, cols] = jnp.max(jnp.where(s > 0.0, pos, -1.0), axis=0, keepdims=True).astype(jnp.int32)
        carry = carry + jnp.sum(s, axis=1, keepdims=True)


def _moe_plan_call(gates_t):
    row = lambda w: jax.ShapeDtypeStruct((1, w), jnp.int32)
    full = lambda w: pl.BlockSpec((1, w), lambda: (0, 0))
    return pl.pallas_call(
        _moe_plan_kernel,
        in_specs=[pl.BlockSpec((LANES, T_ALL), lambda: (0, 0))],
        out_specs=[full(T_ALL), full(T_ALL), full(LANES), full(LANES), full(LANES)],
        out_shape=[row(T_ALL), row(T_ALL), row(LANES), row(LANES), row(LANES)],
        compiler_params=pltpu.CompilerParams(vmem_limit_bytes=VMEM_LIMIT),
        name="moe_plan",
    )(gates_t)


DMA_UNROLL = 8


def _row_tile(ref, row):
    return ref.at[pl.ds(pl.multiple_of(row * 8, 8), 8), :]


def _moe_scatter_kernel(misc_ref, posa_ref, posb_ref, h_ref, xs_ref, zero_ref, sem):
    tm = h_ref.shape[0] // 8

    @pl.when(pl.program_id(0) == 0)
    def _():
        zero_ref[...] = jnp.zeros_like(zero_ref)

        def zero_tile(first_row):
            start = pl.multiple_of(first_row * 8, 8)
            cp = pltpu.make_async_copy(zero_ref, xs_ref.at[pl.ds(start, MOE_TM * 8), :], sem.at[0])
            cp.start()
            cp.wait()

        for e in range(N_EXPERTS):
            @pl.when(misc_ref[0, MISC_NT + e] > 0)
            def _():
                zero_tile(misc_ref[0, MISC_LAST_START + e])

        def zero_tail(j, carry):
            zero_tile(j * MOE_TM)
            return carry

        lax.fori_loop(misc_ref[0, 0], MOE_NT_MAX, zero_tail, 0)

    def issue(r, carry):
        src = _row_tile(h_ref, r)
        pltpu.make_async_copy(src, _row_tile(xs_ref, posa_ref[0, r]), sem.at[0]).start(priority=0)
        pltpu.make_async_copy(src, _row_tile(xs_ref, posb_ref[0, r]), sem.at[1]).start(priority=1)
        return carry

    lax.fori_loop(0, tm, issue, 0, unroll=DMA_UNROLL)
    for k in range(2):
        pltpu.make_async_copy(h_ref, xs_ref.at[pl.ds(0, tm * 8), :], sem.at[k]).wait()


SCATTER_TM = 512


def _moe_scatter_call(misc, posa3, posb3, h2t):
    tm = SCATTER_TM
    smem_row = pl.BlockSpec((None, 1, tm), lambda i: (i, 0, 0), memory_space=pltpu.SMEM)
    return pl.pallas_call(
        _moe_scatter_kernel,
        grid=(T_ALL // tm,),
        in_specs=[pl.BlockSpec((1, LANES), lambda i: (0, 0), memory_space=pltpu.SMEM), smem_row, smem_row,
                  pl.BlockSpec((tm * 8, LANES), lambda i: (i, 0))],
        out_specs=pl.BlockSpec(memory_space=pl.ANY),
        out_shape=jax.ShapeDtypeStruct((MOE_ROWS * 8, LANES), F32),
        scratch_shapes=[pltpu.VMEM((MOE_TM * 8, LANES), F32), pltpu.SemaphoreType.DMA((2,))],
        compiler_params=_cparams(("arbitrary",)),
        name="moe_scatter",
    )(misc, posa3, posb3, h2t)


def _ffn_grouped_kernel(te_ref, ti_ref, misc_ref, x_ref, wg_ref, wu_ref, wd_ref, o_ref):
    j = pl.program_id(0)

    @pl.when(j < misc_ref[0])
    def _():
        h = _load_token_tiles(x_ref, MOE_TM).astype(BF16)
        _store_token_tiles(o_ref, _swiglu(h, wg_ref, wu_ref, wd_ref))

    @pl.when(j >= misc_ref[0])
    def _():
        o_ref[...] = jnp.zeros_like(o_ref)


def _ffn_grouped_call(te, ti, misc, xs, wg, wu, wd):
    expert = lambda *s: pl.BlockSpec((None,) + s, lambda j, te, ti, misc: (te[j], 0, 0))
    grid_spec = pltpu.PrefetchScalarGridSpec(
        num_scalar_prefetch=3,
        grid=(MOE_NT_MAX,),
        in_specs=[pl.BlockSpec((MOE_TM * 8, LANES), lambda j, te, ti, misc: (ti[j], 0)),
                  expert(D_MODEL, D_FF), expert(D_MODEL, D_FF), expert(D_FF, D_MODEL)],
        out_specs=pl.BlockSpec((MOE_TM * 8, LANES), lambda j, te, ti, misc: (j, 0)),
    )
    return pl.pallas_call(
        _ffn_grouped_kernel,
        grid_spec=grid_spec,
        out_shape=jax.ShapeDtypeStruct((MOE_ROWS * 8, LANES), F32),
        compiler_params=_cparams(("arbitrary",)),
        name="ffn_grouped",
    )(te, ti, misc, xs, wg, wu, wd)


COMBINE_TM = 256


def _moe_combine_kernel(posa_ref, posb_ref, posa_next_ref, posb_next_ref, ys_ref, x_ref, g2_ref, gates_ref, nf_ref,
                        op_ref, os_ref, bufa_ref, bufb_ref, sem):
    tm = COMBINE_TM
    i = pl.program_id(0)
    n = pl.num_programs(0)
    slot = i % 2

    def gather(pa_ref, pb_ref, s):
        def issue(r, carry):
            pltpu.make_async_copy(_row_tile(ys_ref, pa_ref[0, r]), _row_tile(bufa_ref.at[s], r),
                                  sem.at[s, 0]).start(priority=0)
            pltpu.make_async_copy(_row_tile(ys_ref, pb_ref[0, r]), _row_tile(bufb_ref.at[s], r),
                                  sem.at[s, 1]).start(priority=1)
            return carry

        lax.fori_loop(0, tm, issue, 0, unroll=DMA_UNROLL)

    @pl.when(i == 0)
    def _():
        gather(posa_ref, posb_ref, 0)

    @pl.when(i + 1 < n)
    def _():
        gather(posa_next_ref, posb_next_ref, 1 - slot)

    gates = gates_ref[...]
    lane = lax.broadcasted_iota(jnp.int32, gates.shape, 1).astype(F32)
    is_sel = (lane >= SEL_LANE0) & (lane < SEL_LANE0 + N_EXPERTS) & (gates > 0.0)
    ia = jnp.min(jnp.where(is_sel, lane, float(LANES)), axis=-1, keepdims=True) - SEL_LANE0
    ib = jnp.max(jnp.where(is_sel, lane, -1.0), axis=-1, keepdims=True) - SEL_LANE0
    wa = jnp.sum(jnp.where(lane == ia, gates, 0.0), axis=-1, keepdims=True)
    wb = jnp.sum(jnp.where(lane == ib, gates, 0.0), axis=-1, keepdims=True)

    pltpu.make_async_copy(ys_ref.at[pl.ds(0, tm * 8), :], bufa_ref.at[slot], sem.at[slot, 0]).wait()
    pltpu.make_async_copy(ys_ref.at[pl.ds(0, tm * 8), :], bufb_ref.at[slot], sem.at[slot, 1]).wait()

    y = wa * _load_token_tiles(bufa_ref.at[slot], tm) + wb * _load_token_tiles(bufb_ref.at[slot], tm)
    out = x_ref[...] + g2_ref[...] * y
    out = (out * lax.rsqrt(jnp.mean(out * out, axis=-1, keepdims=True) + EPS)) * nf_ref[...]

    @pl.when(i < T_PROMPT // tm)
    def _():
        op_ref[...] = out

    @pl.when(i >= T_PROMPT // tm)
    def _():
        os_ref[...] = out


def _moe_combine_call(posa3, posb3, ys, x1, mod4, gates, norm_f):
    tm = COMBINE_TM
    n = T_ALL // tm
    n_p = T_PROMPT // tm
    row = functools.partial(_mod_row, tm=tm)
    smem_row = pl.BlockSpec((None, 1, tm), lambda i: (i, 0, 0), memory_space=pltpu.SMEM)
    smem_next = pl.BlockSpec((None, 1, tm), lambda i: (jnp.minimum(i + 1, n - 1), 0, 0), memory_space=pltpu.SMEM)
    tok = lambda w: pl.BlockSpec((tm, w), lambda i: (i, 0))
    return pl.pallas_call(
        _moe_combine_kernel,
        grid=(n,),
        in_specs=[smem_row, smem_row, smem_next, smem_next, pl.BlockSpec(memory_space=pl.ANY), tok(D_MODEL),
                  pl.BlockSpec((None, None, 1, D_MODEL), lambda i: (row(i), 5, 0, 0)), tok(LANES),
                  pl.BlockSpec((1, D_MODEL), lambda i: (0, 0))],
        out_specs=[pl.BlockSpec((tm, D_MODEL), lambda i: (jnp.minimum(i, n_p - 1), 0)),
                   pl.BlockSpec((tm, D_MODEL), lambda i: (jnp.maximum(i - n_p, 0), 0))],
        out_shape=[jax.ShapeDtypeStruct((T_PROMPT, D_MODEL), F32), jax.ShapeDtypeStruct((T_SAMPLE, D_MODEL), F32)],
        scratch_shapes=[pltpu.VMEM((2, tm * 8, LANES), F32), pltpu.VMEM((2, tm * 8, LANES), F32),
                        pltpu.SemaphoreType.DMA((2, 2))],
        compiler_params=_cparams(("arbitrary",)),
        name="moe_combine",
    )(posa3, posb3, posa3, posb3, ys, x1, mod4, gates, norm_f)


def _moe_call(h2t, x1, mod4, gates, gates_t, wg, wu, wd, norm_f):
    posa, posb, te, ti, misc = _moe_plan_call(gates_t)
    xs = _moe_scatter_call(misc, posa.reshape(T_ALL // SCATTER_TM, 1, SCATTER_TM),
                           posb.reshape(T_ALL // SCATTER_TM, 1, SCATTER_TM), h2t)
    ys = _ffn_grouped_call(te.reshape(LANES), ti.reshape(LANES), misc.reshape(LANES), xs, wg, wu, wd)
    return _moe_combine_call(posa.reshape(T_ALL // COMBINE_TM, 1, COMBINE_TM),
                             posb.reshape(T_ALL // COMBINE_TM, 1, COMBINE_TM), ys, x1, mod4, gates, norm_f)


def kernel(x_prompt, x_sample, cache_k, cache_v, state_gla_fwd, state_gla_bwd, c, c_ctx, w_ada, b_ada, norm_mix, norm_ffn, w_in, w_out, gla_w_up, gla_b_up, gla_norm, diff_lambda, diff_norm, sgu_w, sgu_b, ffn_w_gate, ffn_w_up, ffn_w_down, router_w, moe_w_gate, moe_w_up, moe_w_down, norm_f):
    assert DEPTH == 2
    x_pieces = [x_prompt.reshape(T_PROMPT, D_MODEL), x_sample.reshape(T_SAMPLE, D_MODEL)]
    cvecs = jnp.concatenate([c_ctx[None, :], c, jnp.zeros((N_MOD_ROWS - 1 - DEC_BATCH, D_MODEL), F32)], axis=0)
    mod = _ada_call(cvecs, w_ada, b_ada)
    cos, sin_signed = _rope_tables()
    zeros_state = jnp.zeros((BATCH, 64, 256), F32)

    w_in_t = jnp.swapaxes(w_in, 1, 2)
    moe_w = [moe_w_gate[0], moe_w_up[0], moe_w_down[0]]
    for l in range(DEPTH):
        mod4 = mod[l].reshape(N_MOD_ROWS, 6, 1, D_MODEL)
        w_up = jnp.zeros((LANES, 2 * W_A), F32)
        w_up = w_up.at[0:GLA_RANK, 0:W_A].set(gla_w_up[l, 0]).at[GLA_RANK:2 * GLA_RANK, W_A:].set(gla_w_up[l, 1])
        b_up = gla_b_up[l].reshape(1, 2 * W_A)
        bs_full = jnp.repeat(sgu_b[l].T, DG_C, axis=1)
        g4, la, qb, kb, vb, oc, moe_w[l] = _in_proj_call(x_pieces, norm_mix[l][None, :], mod4, w_in_t, l,
                                                         w_up.astype(BF16), b_up, sgu_w[l].astype(BF16), bs_full,
                                                         moe_w[l])

        gain_a = gla_norm[l][None, :]
        if l == 0:
            oa_p, sf0, sb0 = _gla_call(g4, la, zeros_state, zeros_state, gain_a, batch=BATCH, seq=SEQ, row_block0=0)
            kb0, vb0 = kb, vb
        else:
            oa_p, new_sf, new_sb, new_cache_k = _gla_call(g4, la, zeros_state, zeros_state, gain_a, (sf0, sb0, kb0, kb),
                                                          batch=BATCH, seq=SEQ, row_block0=0)
        oa_s, _, _ = _gla_call(g4, la, _state_to_kernel(state_gla_fwd[:, l]),
                               _state_to_kernel(state_gla_bwd[:, l]), gain_a,
                               batch=DEC_BATCH, seq=DEC_SEQ, row_block0=T_PROMPT // DEC_SEQ)

        lam_init = 0.8 - 0.6 * math.exp(-0.3 * l)
        gain_b = diff_norm[l][None, :]
        if l == 0:
            (ob_p,) = _attn_prompt_call(diff_lambda[l], qb, kb, vb, gain_b, lam_init=lam_init)
        else:
            ob_p, new_cache_v = _attn_prompt_call(diff_lambda[l], qb, kb, vb, gain_b, (vb0,),
                                                  lam_init=lam_init, write_cache=True)
        ob_s, *dense_w = _attn_sample_call(diff_lambda[l], qb, kb, vb,
                                           cache_k[:, l].reshape(DEC_BATCH, PAST_LEN, W_B),
                                           cache_v[:, l].reshape(DEC_BATCH, PAST_LEN, W_B),
                                           cos, sin_signed, gain_b,
                                           (ffn_w_gate[0], ffn_w_up[0], ffn_w_down[0]) if l == 0 else (),
                                           lam_init=lam_init)

        if l == 0:
            x1, h2 = _out_proj_call([oa_p, oa_s], [ob_p, ob_s], oc, x_pieces, w_out, l, mod4, norm_ffn[l][None, :])
            x_next, moe_w[2] = _ffn_call(h2, x1, mod4, *dense_w, moe_w[2])
            x_pieces = [x_next]
        else:
            rw = jnp.pad(router_w[0], ((0, 0), (0, LANES - N_EXPERTS)))
            x1, h2t, gates, gates_t = _out_proj_call([oa_p, oa_s], [ob_p, ob_s], oc, x_pieces, w_out, l, mod4,
                                                     norm_ffn[l][None, :], rw)
            y_prompt, y_sample = _moe_call(h2t, x1, mod4, gates, gates_t, *moe_w, norm_f[None, :])

    state_shape = (BATCH, DEPTH, N_HEADS_A, DK_A, DK_A)
    return (y_prompt.reshape(BATCH, SEQ, D_MODEL), y_sample.reshape(DEC_BATCH, DEC_SEQ, D_MODEL),
            new_cache_k.reshape(BATCH, DEPTH, SEQ, N_HEADS_B, DV_B), new_cache_v.reshape(BATCH, DEPTH, SEQ, N_HEADS_B, DV_B),
            new_sf.reshape(state_shape), new_sb.reshape(state_shape))
```

```python
import functools
import math

import jax
import jax.numpy as jnp
import numpy as np
from jax import lax
from jax.experimental import pallas as pl
from jax.experimental.pallas import tpu as pltpu

F32 = jnp.float32
BF16 = jnp.bfloat16

D_MODEL = 1024
BATCH = 32
SEQ = 256
DEPTH = 2
DEC_BATCH = 2
DEC_SEQ = 1024
PAST_LEN = 256
GRID_W = 64
N_HEADS_A = 4
DK_A = 64
W_A = 256
GLA_RANK = 16
GLA_TAU = 16.0
GLA_CHUNK = 64
N_HEADS_B = 4
DH_B = 64
DV_B = 128
W_B = 512
ROPE_THETA = 10000.0
AXIS_PAIRS = DH_B // 4
N_GROUPS_C = 4
DG_C = 64
W_C = 256
SGU_CHUNK = 128
D_FF = 2816
N_EXPERTS = 8
EPS = 1e-6

T_PROMPT = BATCH * SEQ
T_SAMPLE = DEC_BATCH * DEC_SEQ
T_ALL = T_PROMPT + T_SAMPLE
N_MOD_ROWS = 8
LANES = 128
VMEM_LIMIT = 56 * 1024 * 1024


def _cparams(sem):
    return pltpu.CompilerParams(dimension_semantics=sem, vmem_limit_bytes=VMEM_LIMIT)


def _dot(a, b):
    return jnp.dot(a, b, preferred_element_type=F32)


def _dot_nt(a, b):
    return lax.dot_general(a, b, (((1,), (1,)), ((), ())), preferred_element_type=F32)


def _dot_tn(a, b):
    return lax.dot_general(a, b, (((0,), (0,)), ((), ())), preferred_element_type=F32)


def _split_bf16(x):
    hi = x.astype(BF16)
    lo = (x - hi.astype(F32)).astype(BF16)
    return hi, lo


def _dot3(a, w):
    a_hi, a_lo = _split_bf16(a)
    w_hi, w_lo = _split_bf16(w)
    return _dot(a_hi, w_hi) + (_dot(a_lo, w_hi) + _dot(a_hi, w_lo))


def _sigmoid(x):
    return 1.0 / (1.0 + jnp.exp(-x))


def _silu(x):
    return x * _sigmoid(x)


def _gelu_tanh(x):
    c = math.sqrt(2.0 / math.pi)
    return x * (0.5 * (1.0 + jnp.tanh(c * (x + 0.044715 * (x * x * x)))))


def _log_sigmoid(x):
    return jnp.minimum(x, 0.0) - jnp.log(1.0 + jnp.exp(-jnp.abs(x)))


def _mod_row(i, tm):
    n_p = T_PROMPT // tm
    per_b = DEC_SEQ // tm
    return jnp.where(i < n_p, 0, 1 + (i - n_p) // per_b)


ADA_TN = 1536


def _ada_kernel(c_ref, w_ref, b_ref, o_ref):
    a = _silu(c_ref[...])
    o_ref[...] = _dot3(a, w_ref[...]) + b_ref[...]


def _ada_call(cvecs, w_ada, b_ada):
    n_col = (6 * D_MODEL) // ADA_TN
    return pl.pallas_call(
        _ada_kernel,
        grid=(DEPTH, n_col),
        in_specs=[
            pl.BlockSpec((N_MOD_ROWS, D_MODEL), lambda l, j: (0, 0)),
            pl.BlockSpec((None, D_MODEL, ADA_TN), lambda l, j: (l, 0, j)),
            pl.BlockSpec((None, 1, ADA_TN), lambda l, j: (l, 0, j)),
        ],
        out_specs=pl.BlockSpec((None, N_MOD_ROWS, ADA_TN), lambda l, j: (l, 0, j)),
        out_shape=jax.ShapeDtypeStruct((DEPTH, N_MOD_ROWS, 6 * D_MODEL), F32),
        compiler_params=_cparams(("arbitrary", "arbitrary")),
        name="ada_mod",
    )(cvecs, w_ada, b_ada.reshape(DEPTH, 1, 6 * D_MODEL))


IN_TM = 512
IN_COLS = 3104
Z_COL0 = 1024
Z_COLS = 2 * GLA_RANK
W_MAIN = 3072


def _piece_specs(pieces, tm, width):
    specs, t0 = [], 0
    for arr in pieces:
        nt = arr.shape[0] // tm
        specs.append(pl.BlockSpec((tm, width), lambda i, t0=t0, nt=nt: (jnp.clip(i - t0, 0, nt - 1), 0)))
        t0 += nt
    assert t0 * tm == T_ALL and len(pieces) in (1, 2) and (len(pieces) == 1 or pieces[0].shape[0] == T_PROMPT)
    return specs


def _pick_piece(refs, tm, rows=slice(None)):
    if len(refs) == 1:
        return refs[0][rows, :]
    return jnp.where(pl.program_id(0) < T_PROMPT // tm, refs[0][rows, :], refs[1][rows, :])


RIDER_BLOCKS_20 = 16


W_STAGE_ROWS = 512


def _in_proj_kernel(*refs, n_x, layer):
    x_refs = refs[:n_x]
    (nrm_ref, sh_ref, sc_ref, w_hbm, wup_ref, bup_ref, ws_ref, bs_ref, cast_ref,
     g4_ref, la_ref, qb_ref, kb_ref, vb_ref, oc_ref, cast_out_ref, wm_ref, wz_ref, stage_ref, sem) = refs[n_x:]

    @pl.when(pl.program_id(0) < RIDER_BLOCKS_20)
    def _():
        cast_out_ref[...] = cast_ref[...].astype(BF16)

    @pl.when(pl.program_id(0) == 0)
    def _():
        n = W_STAGE_ROWS
        moves = [(src, src if src < Z_COL0 else src - Z_COLS)
                 for src in (*range(0, Z_COL0, n), *range(Z_COL0 + Z_COLS, IN_COLS, n))]
        copy = lambda k: pltpu.make_async_copy(w_hbm.at[layer, pl.ds(moves[k][0], n), :], stage_ref.at[k % 2],
                                               sem.at[k % 2])
        copy(0).start()
        for k, (_, dst) in enumerate(moves):
            if k + 1 < len(moves):
                copy(k + 1).start()
            copy(k).wait()
            wm_ref[dst:dst + n, :] = stage_ref[k % 2].astype(BF16)
        gate_rows = pltpu.make_async_copy(w_hbm.at[layer, pl.ds(Z_COL0, Z_COLS), :],
                                          stage_ref.at[0, pl.ds(0, Z_COLS), :], sem.at[0])
        gate_rows.start()
        gate_rows.wait()
        wz_ref[...] = jnp.zeros_like(wz_ref)
        wz_ref[0:Z_COLS, :] = stage_ref[0, 0:Z_COLS, :].astype(BF16)

    x = _pick_piece(x_refs, IN_TM)
    y = x * lax.rsqrt(jnp.mean(x * x, axis=-1, keepdims=True) + EPS)
    h = (y * nrm_ref[...]) * (1.0 + sc_ref[...]) + sh_ref[...]
    hb = h.astype(BF16)
    def plain_job(ref, c0, w0):
        def job():
            ref[:, c0:c0 + MXU_N] = _dot_nt(hb, wm_ref[w0:w0 + MXU_N, :])
        return job

    jobs = [plain_job(ref, c, w0 + c) for ref, w0, width in
            ((g4_ref, 0, 1024), (qb_ref, 1024, 512), (kb_ref, 1536, 512), (vb_ref, 2048, 512))
            for c in range(0, width, MXU_N)]
    z = _dot_nt(hb, wz_ref[...])
    jobs.pop(0)()
    zz = _dot(z.astype(BF16), wup_ref[...]) + bup_ref[...]
    uv = _dot_nt(hb, wm_ref[2560:3072, :])
    jobs.pop(0)()
    la_ref[...] = _log_sigmoid(zz) * (1.0 / GLA_TAU)
    jobs.pop(0)()
    oc_ref[...] = _sgu(uv, ws_ref, bs_ref, jobs)


def _resident_layer(a, l):
    return pl.BlockSpec((None,) + a.shape[1:], lambda *_: (l,) + (0,) * (a.ndim - 1), pipeline_mode=pl.Buffered(1))


def _in_proj_call(x_pieces, norm_g, mod4, w_in_t, l, w_up, b_up, sgu_ws, sgu_bs, cast_w):
    tm = IN_TM
    rider_spec, rider_shape = _cast_rider(cast_w, RIDER_BLOCKS_20)
    n = T_ALL // tm
    row = functools.partial(_mod_row, tm=tm)
    mod_spec = lambda k: pl.BlockSpec((None, None, 1, D_MODEL), lambda i: (row(i), k, 0, 0))
    full = lambda a: pl.BlockSpec(a.shape, lambda i: (0,) * a.ndim)
    out = lambda w: pl.BlockSpec((tm, w), lambda i: (i, 0))
    return pl.pallas_call(
        functools.partial(_in_proj_kernel, n_x=len(x_pieces), layer=l),
        grid=(n,),
        in_specs=_piece_specs(x_pieces, tm, D_MODEL) + [full(norm_g), mod_spec(0), mod_spec(1), pl.BlockSpec(memory_space=pl.ANY),
                                                        full(w_up), full(b_up), full(sgu_ws), full(sgu_bs),
                                                        rider_spec],
        out_specs=[out(1024), out(512), out(512), out(512), out(512), out(W_C), rider_spec],
        out_shape=[jax.ShapeDtypeStruct((T_ALL, w), F32) for w in (1024, 512, 512, 512, 512)]
        + [jax.ShapeDtypeStruct((T_ALL, W_C), BF16), rider_shape],
        scratch_shapes=[pltpu.VMEM((W_MAIN, D_MODEL), BF16), pltpu.VMEM((LANES, D_MODEL), BF16),
                        pltpu.VMEM((2, W_STAGE_ROWS, D_MODEL), F32), pltpu.SemaphoreType.DMA((2,))],
        compiler_params=_cparams(("arbitrary",)),
        name="in_proj",
    )(*x_pieces, norm_g, mod4, mod4, w_in_t, w_up, b_up, sgu_ws, sgu_bs, cast_w)


GLA_SB = 256
GLA_NC = GLA_SB // GLA_CHUNK


def _head_blocks(x, same64):
    return jnp.where(same64, jnp.concatenate([x] * N_HEADS_A, axis=0), 0.0).astype(BF16)


def _gla_superblock(q, k, vb, v_blocks, la, st_all, tri, mask4, same64, forward):
    c = GLA_CHUNK
    mid, last = (c // 2 - 1, c - 1) if forward else (c // 2, 0)
    la_hi, la_lo = _split_bf16(la)
    b = _dot(tri, la_hi) + _dot(tri, la_lo)
    yield
    rows_of = lambda r: jnp.concatenate(
        [jnp.broadcast_to(b[i * c + r:i * c + r + 1, :], (c, W_A)) for i in range(GLA_NC)], axis=0)
    m = rows_of(mid)
    bl = rows_of(last)
    qe = (q * jnp.exp(b - m)).astype(BF16)
    ke = k * jnp.exp(m - b)
    qi = (q * jnp.exp(b)).astype(BF16)
    ks = (k * jnp.exp(bl - b)).astype(BF16)
    outs = [None] * GLA_NC
    for i in (range(GLA_NC) if forward else reversed(range(GLA_NC))):
        rows = slice(i * c, (i + 1) * c)
        s = _dot_nt(qe[rows, :], _head_blocks(ke[rows, :], same64))
        kv = _dot_tn(vb[rows, :], ks[rows, :])
        yield
        a = jnp.where(mask4, s, 0.0).astype(BF16)
        outs[i] = _dot(a, v_blocks[i]) + _dot_nt(qi[rows, :], st_all.astype(BF16))
        st_all = st_all * jnp.exp(bl[i * c:i * c + 1, :]) + jnp.where(same64, kv, 0.0)
        yield
    return jnp.concatenate(outs, axis=0), st_all


def _interleave(*gens):
    results = [None] * len(gens)
    active = list(enumerate(gens))
    while active:
        for item in list(active):
            try:
                next(item[1])
            except StopIteration as stop:
                results[item[0]] = stop.value
                active.remove(item)
    return results


def _cast_rider(w, n_blocks):
    e, r, c = w.shape
    per = n_blocks // e
    rows = r // per
    assert per * e == n_blocks and rows * per == r and rows % 16 == 0

    def index(b):
        b = jnp.minimum(b, n_blocks - 1)
        return b // per, b % per, 0

    spec = pl.BlockSpec((None, rows, c), index)
    return spec, jax.ShapeDtypeStruct(w.shape, BF16)


def _gla_kernel(*refs, seq, final):
    if final:
        (g4_ref, la_ref, s0f_ref, s0b_ref, gain_ref, sf_prev_ref, sb_prev_ref, k_prev_ref, k_ref,
         o_ref, sf_ref, sb_ref, ck_ref, of_ref, ob_ref) = refs
        for l, src in enumerate((k_prev_ref, k_ref)):
            for h in range(N_HEADS_B):
                ck_ref[l, pl.ds(h, SEQ, stride=N_HEADS_B), :] = src[:, h * DV_B:(h + 1) * DV_B]
    else:
        g4_ref, la_ref, s0f_ref, s0b_ref, gain_ref, o_ref, sf_ref, sb_ref, of_ref, ob_ref = refs
    n = GLA_SB
    nsb = seq // n
    r = lax.broadcasted_iota(jnp.int32, (n, n), 0)
    s = lax.broadcasted_iota(jnp.int32, (n, n), 1)
    same64 = (r // GLA_CHUNK) == (s // GLA_CHUNK)
    lower = same64 & (s <= r)
    upper = same64 & (s >= r)
    tri_f = jnp.where(lower, 1.0, 0.0).astype(BF16)
    tri_b = jnp.where(upper, 1.0, 0.0).astype(BF16)
    ones64 = jnp.where(same64, 1.0, 0.0).astype(BF16)
    key_row = lax.broadcasted_iota(jnp.int32, (GLA_CHUNK, n), 1) % GLA_CHUNK
    qry_row = lax.broadcasted_iota(jnp.int32, (GLA_CHUNK, n), 0)
    mask4_f = key_row <= qry_row
    mask4_b = key_row >= qry_row
    scale = DK_A ** -0.5
    expand = lambda st: jnp.where(same64, jnp.concatenate([st] * N_HEADS_A, axis=0), 0.0)
    compact = lambda st_all: functools.reduce(
        lambda a, b: a + b, [st_all[h * 64:(h + 1) * 64, :] for h in range(N_HEADS_A)])

    def step(i, carry):
        stf, stb = carry
        rf = pl.ds(pl.multiple_of(i * n, n), n)
        rb = pl.ds(pl.multiple_of((nsb - 1 - i) * n, n), n)
        def direction(rows, la_cols, st, tri, mask4, forward):
            q = g4_ref[rows, 0:256] * scale
            k = g4_ref[rows, 256:512]
            v = g4_ref[rows, 512:768]
            v_blocks = [_head_blocks(v[i * GLA_CHUNK:(i + 1) * GLA_CHUNK, :], same64) for i in range(GLA_NC)]
            return _gla_superblock(q, k, v.astype(BF16), v_blocks, la_ref[rows, la_cols], st, tri, mask4, same64,
                                   forward)

        (o_f, stf), (o_b, stb) = _interleave(direction(rf, slice(0, 256), stf, tri_f, mask4_f, True),
                                             direction(rb, slice(256, 512), stb, tri_b, mask4_b, False))
        of_ref[rf, :] = o_f
        ob_ref[rb, :] = o_b
        return stf, stb

    stf, stb = lax.fori_loop(0, nsb, step, (expand(s0f_ref[...]), expand(s0b_ref[...])))
    if final:
        sf_ref[0] = sf_prev_ref[...].T
        sb_ref[0] = sb_prev_ref[...].T
        sf_ref[1] = compact(stf).T
        sb_ref[1] = compact(stb).T
    else:
        sf_ref[...] = compact(stf)
        sb_ref[...] = compact(stb)

    gain = gain_ref[...]

    def finish(i, carry):
        rows = pl.ds(pl.multiple_of(i * n, n), n)
        o = of_ref[rows, :] + ob_ref[rows, :]
        sq_hi, sq_lo = _split_bf16(o * o)
        ms = (_dot(sq_hi, ones64) + _dot(sq_lo, ones64)) * (1.0 / DK_A)
        y = (o * lax.rsqrt(ms + EPS)) * gain
        o_ref[rows, :] = (y * _silu(g4_ref[rows, 768:1024])).astype(BF16)
        return carry

    lax.fori_loop(0, nsb, finish, 0)


def _gla_call(g4, la, s0f, s0b, gain, prev=None, *, batch, seq, row_block0):
    tok = lambda w: pl.BlockSpec((seq, w), lambda b: (row_block0 + b, 0))
    st = pl.BlockSpec((None, 64, 256), lambda b: (b, 0, 0))
    in_specs = [tok(1024), tok(512), st, st, pl.BlockSpec((1, W_A), lambda b: (0, 0))]
    args = [g4, la, s0f, s0b, gain]
    o_spec = pl.BlockSpec((seq, W_A), lambda b: (b, 0))
    o_shape = jax.ShapeDtypeStruct((batch * seq, W_A), BF16)
    if prev is None:
        out_specs = [o_spec, st, st]
        out_shape = [o_shape, jax.ShapeDtypeStruct((batch, 64, 256), F32), jax.ShapeDtypeStruct((batch, 64, 256), F32)]
    else:
        assert seq == SEQ and batch == BATCH and DEPTH == 2
        kblk = pl.BlockSpec((SEQ, W_B), lambda b: (b, 0))
        in_specs += [st, st, kblk, kblk]
        args += list(prev)
        st2 = pl.BlockSpec((None, DEPTH, 256, 64), lambda b: (b, 0, 0, 0))
        st2_shape = jax.ShapeDtypeStruct((batch, DEPTH, 256, 64), F32)
        out_specs = [o_spec, st2, st2, pl.BlockSpec((None, DEPTH, SEQ * N_HEADS_B, DV_B), lambda b: (b, 0, 0, 0))]
        out_shape = [o_shape, st2_shape, st2_shape, jax.ShapeDtypeStruct((batch, DEPTH, SEQ * N_HEADS_B, DV_B), F32)]
    return pl.pallas_call(
        functools.partial(_gla_kernel, seq=seq, final=prev is not None),
        grid=(batch,),
        in_specs=in_specs,
        out_specs=out_specs,
        out_shape=out_shape,
        scratch_shapes=[pltpu.VMEM((seq, W_A), F32), pltpu.VMEM((seq, W_A), F32)],
        compiler_params=_cparams(("arbitrary",)),
        name=f"gla_{seq}",
    )(*args)


def _state_to_kernel(s):
    b = s.shape[0]
    return jnp.transpose(s, (0, 3, 1, 2)).reshape(b, 64, 256)


def _lambda(lv, lam_init):
    l01 = jnp.sum(lv[0:1, :] * lv[1:2, :], axis=-1, keepdims=True)
    l23 = jnp.sum(lv[2:3, :] * lv[3:4, :], axis=-1, keepdims=True)
    return jnp.exp(l01) - jnp.exp(l23) + lam_init


def _softmax_parts(parts):
    mx = functools.reduce(jnp.maximum, [jnp.max(p, axis=-1, keepdims=True) for p in parts])
    es = [jnp.exp(p - mx) for p in parts]
    den = functools.reduce(lambda a, b: a + b, [jnp.sum(e, axis=-1, keepdims=True) for e in es])
    return [e / den for e in es]


def _diff_finish(o, gain, lam_init):
    o = o * lax.rsqrt(jnp.mean(o * o, axis=-1, keepdims=True) + EPS)
    return ((o * gain) * (1.0 - lam_init)).astype(BF16)


QK_SCALE = DH_B ** -0.5


def _key_halves(k):
    first = lax.broadcasted_iota(jnp.int32, k.shape, 1) < DH_B
    return jnp.where(first, k, 0.0).astype(BF16), jnp.where(first, 0.0, k).astype(BF16)


def _attn_prompt_kernel(lv_ref, q_ref, k_ref, v_ref, gain_ref, *rest, lam_init, n_prev):
    prev_refs, (o_ref, *cache_refs) = rest[:n_prev], rest[n_prev:]
    lam = _lambda(lv_ref[...], lam_init)

    def head(h):
        cols = slice(h * DV_B, (h + 1) * DV_B)
        q = (q_ref[:, cols] * QK_SCALE).astype(BF16)
        k1, k2 = _key_halves(k_ref[:, cols])
        s1 = _dot_nt(q, k1)
        s2 = _dot_nt(q, k2)
        yield
        (p1,) = _softmax_parts([s1])
        (p2,) = _softmax_parts([s2])
        a = p1 - lam * p2
        o = _dot(a.astype(BF16), v_ref[:, cols].astype(BF16))
        yield
        o_ref[:, cols] = _diff_finish(o, gain_ref[:, cols], lam_init)
        if cache_refs:
            (cv_ref,) = cache_refs
            for l, src in enumerate((*prev_refs, v_ref)):
                cv_ref[l, pl.ds(h, SEQ, stride=N_HEADS_B), :] = src[:, cols]

    _interleave(*[head(h) for h in range(N_HEADS_B)])


def _attn_prompt_call(lv, qb, kb, vb, gain, prev_v=(), *, lam_init, write_cache=False):
    blk = pl.BlockSpec((SEQ, W_B), lambda b: (b, 0))
    n_prev = len(prev_v)
    assert write_cache or not prev_v
    in_specs = [pl.BlockSpec((4, DH_B), lambda b: (0, 0)), blk, blk, blk,
                pl.BlockSpec((1, W_B), lambda b: (0, 0))] + [blk] * n_prev
    args = [lv, qb, kb, vb, gain, *prev_v]
    out_specs = [blk]
    out_shape = [jax.ShapeDtypeStruct((T_PROMPT, W_B), BF16)]
    if write_cache:
        out_specs.append(pl.BlockSpec((None, n_prev + 1, SEQ * N_HEADS_B, DV_B), lambda b: (b, 0, 0, 0)))
        out_shape.append(jax.ShapeDtypeStruct((BATCH, n_prev + 1, SEQ * N_HEADS_B, DV_B), F32))
    return pl.pallas_call(
        functools.partial(_attn_prompt_kernel, lam_init=lam_init, n_prev=n_prev),
        grid=(BATCH,),
        in_specs=in_specs,
        out_specs=out_specs,
        out_shape=out_shape,
        compiler_params=_cparams(("arbitrary",)),
        name="diff_attn_prompt",
    )(*args)


def _rope(x, cos, sin_signed):
    lane = lax.broadcasted_iota(jnp.int32, x.shape, 1)
    first = (lane % (2 * AXIS_PAIRS)) < AXIS_PAIRS
    partner = jnp.where(first, pltpu.roll(x, LANES - AXIS_PAIRS, 1), pltpu.roll(x, AXIS_PAIRS, 1))
    return x * cos + partner * sin_signed


ATT_TQ = 256


def _attn_sample_kernel(lv_ref, q_ref, k_ref, v_ref, kc_ref, vc_ref, cosq_ref, sinq_ref,
                        cosk_ref, sink_ref, gain_ref, *rest, lam_init, n_cast):
    w_refs, o_ref, wb_refs, (k1_ref, k2_ref) = rest[:n_cast], rest[n_cast], rest[n_cast + 1:2 * n_cast + 1], rest[-2:]
    for w_ref, wb_ref in zip(w_refs, wb_refs):
        wb_ref[...] = w_ref[...].astype(BF16)

    @pl.when(pl.program_id(1) == 0)
    def _():
        for h in range(N_HEADS_B):
            cols = slice(h * DV_B, (h + 1) * DV_B)
            k1_ref[:, cols], k2_ref[:, cols] = _key_halves(_rope(k_ref[:, cols], cosk_ref[...], sink_ref[...]))

    lam = _lambda(lv_ref[...], lam_init)

    def head(h):
        cols = slice(h * DV_B, (h + 1) * DV_B)
        q = (_rope(q_ref[:, cols], cosq_ref[...], sinq_ref[...]) * QK_SCALE).astype(BF16)
        c1, c2 = _key_halves(kc_ref[:, cols])
        s1 = [_dot_nt(q, k1_ref[:, cols]), _dot_nt(q, c1)]
        s2 = [_dot_nt(q, k2_ref[:, cols]), _dot_nt(q, c2)]
        yield
        p1 = _softmax_parts(s1)
        p2 = _softmax_parts(s2)
        a_own = p1[0] - lam * p2[0]
        a_ctx = p1[1] - lam * p2[1]
        o = (_dot(a_own.astype(BF16), v_ref[:, cols].astype(BF16))
             + _dot(a_ctx.astype(BF16), vc_ref[:, cols].astype(BF16)))
        yield
        o_ref[:, cols] = _diff_finish(o, gain_ref[:, cols], lam_init)

    _interleave(*[head(h) for h in range(N_HEADS_B)])


def _attn_sample_call(lv, qb, kb, vb, kc, vc, cos, sin_signed, gain, cast_ws=(), *, lam_init):
    tq = ATT_TQ
    nq = DEC_SEQ // tq
    p0 = T_PROMPT // tq
    s0 = T_PROMPT // DEC_SEQ
    qblk = pl.BlockSpec((tq, W_B), lambda b, t: (p0 + b * nq + t, 0))
    kvblk = pl.BlockSpec((DEC_SEQ, W_B), lambda b, t: (s0 + b, 0))
    cblk = pl.BlockSpec((None, PAST_LEN, W_B), lambda b, t: (b, 0, 0))
    n_steps = DEC_BATCH * nq
    cast_specs = []
    for w in cast_ws:
        assert w.shape[0] % (16 * n_steps) == 0
        cast_specs.append(pl.BlockSpec((w.shape[0] // n_steps, w.shape[1]), lambda b, t: (b * nq + t, 0)))
    return pl.pallas_call(
        functools.partial(_attn_sample_kernel, lam_init=lam_init, n_cast=len(cast_ws)),
        grid=(DEC_BATCH, nq),
        in_specs=[pl.BlockSpec((4, DH_B), lambda b, t: (0, 0)), qblk, kvblk, kvblk, cblk, cblk,
                  pl.BlockSpec((tq, DV_B), lambda b, t: (t, 0)),
                  pl.BlockSpec((tq, DV_B), lambda b, t: (t, 0)),
                  pl.BlockSpec((DEC_SEQ, DV_B), lambda b, t: (0, 0)),
                  pl.BlockSpec((DEC_SEQ, DV_B), lambda b, t: (0, 0)),
                  pl.BlockSpec((1, W_B), lambda b, t: (0, 0))] + cast_specs,
        out_specs=[pl.BlockSpec((tq, W_B), lambda b, t: (b * nq + t, 0))] + cast_specs,
        out_shape=[jax.ShapeDtypeStruct((T_SAMPLE, W_B), BF16)] + [jax.ShapeDtypeStruct(w.shape, BF16) for w in cast_ws],
        scratch_shapes=[pltpu.VMEM((DEC_SEQ, W_B), BF16), pltpu.VMEM((DEC_SEQ, W_B), BF16)],
        compiler_params=_cparams(("arbitrary", "arbitrary")),
        name="diff_attn_sample",
    )(lv, qb, kb, vb, kc, vc, cos, sin_signed, cos, sin_signed, gain, *cast_ws)


def _rope_tables():
    rows = DEC_SEQ // GRID_W
    row = jnp.repeat(jnp.arange(rows, dtype=F32), GRID_W)
    col = jnp.tile(jnp.arange(GRID_W, dtype=F32), rows)
    freqs = ROPE_THETA ** (-jnp.arange(AXIS_PAIRS, dtype=F32) / AXIS_PAIRS)
    ar, ac = row[:, None] * freqs, col[:, None] * freqs
    cos64 = jnp.concatenate([jnp.cos(ar), jnp.cos(ar), jnp.cos(ac), jnp.cos(ac)], axis=-1)
    sin64 = jnp.concatenate([-jnp.sin(ar), jnp.sin(ar), -jnp.sin(ac), jnp.sin(ac)], axis=-1)
    return jnp.tile(cos64, (1, 2)), jnp.tile(sin64, (1, 2))


def _group_mean(x, ones64):
    hi, lo = _split_bf16(x)
    return (_dot(hi, ones64) + _dot(lo, ones64)) * (1.0 / DG_C)


def _sgu(uv, ws_ref, bs_ref, fillers=()):
    fillers = list(fillers)
    fill = lambda: fillers.pop(0)() if fillers else None
    r = lax.broadcasted_iota(jnp.int32, (W_C, W_C), 0)
    s = lax.broadcasted_iota(jnp.int32, (W_C, W_C), 1)
    ones64 = jnp.where((r // DG_C) == (s // DG_C), 1.0, 0.0).astype(BF16)
    lane = lax.broadcasted_iota(jnp.int32, (SGU_CHUNK, W_C), 1)
    outs = []
    for n in range(uv.shape[0] // SGU_CHUNK):
        rows = slice(n * SGU_CHUNK, (n + 1) * SGU_CHUNK)
        u = _gelu_tanh(uv[rows, 0:256])
        v = _gelu_tanh(uv[rows, 256:512])
        mu = _group_mean(v, ones64)
        fill()
        d = v - mu
        var = _group_mean(d * d, ones64)
        fill()
        vn = d * lax.rsqrt(var + EPS)
        s_mix = bs_ref[...]
        for g in range(N_GROUPS_C):
            vn_g = jnp.where((lane // DG_C) == g, vn, 0.0).astype(BF16)
            s_mix = s_mix + _dot(ws_ref[g], vn_g)
        fill()
        outs.append((u * s_mix).astype(BF16))
    while fillers:
        fill()
    return jnp.concatenate(outs, axis=0)


OUT_TM = 1024
OUT_SLABS = 4


SEL_LANE0 = N_EXPERTS


def _top2_gates(logits):
    lane = lax.broadcasted_iota(jnp.int32, logits.shape, 1).astype(F32)
    neg = -jnp.inf
    lg = jnp.where(lane < N_EXPERTS, logits, neg)
    m1 = jnp.max(lg, axis=-1, keepdims=True)
    i1 = jnp.min(jnp.where(lg == m1, lane, float(LANES)), axis=-1, keepdims=True)
    lg2 = jnp.where(lane == i1, neg, lg)
    m2 = jnp.max(lg2, axis=-1, keepdims=True)
    i2 = jnp.min(jnp.where(lg2 == m2, lane, float(LANES)), axis=-1, keepdims=True)
    e2 = jnp.exp(m2 - m1)
    den = 1.0 + e2
    gates = jnp.where(lane == i1, 1.0 / den, 0.0) + jnp.where(lane == i2, e2 / den, 0.0)
    sel = jnp.where((lane == i1 + SEL_LANE0) | (lane == i2 + SEL_LANE0), 1.0, 0.0)
    return gates + sel


def _store_token_tiles(ref, val, r0=0):
    n = val.shape[0]
    for k in range(D_MODEL // LANES):
        ref[pl.ds(8 * r0 + k, n, stride=8), :] = val[:, k * LANES:(k + 1) * LANES]


def _load_token_tiles(ref, n):
    return jnp.concatenate([ref[pl.ds(k, n, stride=8), :] for k in range(D_MODEL // LANES)], axis=-1)


def _out_proj_kernel(*refs, n_x, moe):
    oa_refs, ob_refs, (oc_ref,), x_refs = refs[0:2], refs[2:4], refs[4:5], refs[5:5 + n_x]
    w_ref, g1_ref, nrm_ref, sc_ref, sh_ref, *rest = refs[5 + n_x:]
    if moe:
        rw_ref, x1_ref, h2t_ref, gates_ref, gates_t_ref, wo_ref = rest
    else:
        x1_ref, h2_ref, wo_ref = rest

    @pl.when(pl.program_id(0) == 0)
    def _():
        wo_ref[...] = w_ref[...].astype(BF16)

    if moe:
        rw_hi, rw_lo = _split_bf16(rw_ref[...])

    def rows_chain(r0, n):
        rs = slice(r0, r0 + n)
        y = (_dot(_pick_piece(oa_refs, OUT_TM, rs), wo_ref[0:256, :])
             + _dot(_pick_piece(ob_refs, OUT_TM, rs), wo_ref[256:768, :]) + _dot(oc_ref[rs, :], wo_ref[768:1024, :]))
        yield
        x1 = _pick_piece(x_refs, OUT_TM, rs) + g1_ref[...] * y
        x1_ref[rs, :] = x1
        yn = x1 * lax.rsqrt(jnp.mean(x1 * x1, axis=-1, keepdims=True) + EPS)
        h = (yn * nrm_ref[...]) * (1.0 + sc_ref[...]) + sh_ref[...]
        if moe:
            _store_token_tiles(h2t_ref, h, r0)
            h_hi, h_lo = _split_bf16(h)
            logits = _dot(h_hi, rw_hi) + (_dot(h_lo, rw_hi) + _dot(h_hi, rw_lo))
            yield
            gates = _top2_gates(logits)
            gates_ref[rs, :] = gates
            gates_t_ref[:, rs] = gates.T
        else:
            h2_ref[rs, :] = h.astype(BF16)

    n_slab = OUT_TM // OUT_SLABS
    _interleave(*[rows_chain(i * n_slab, n_slab) for i in range(OUT_SLABS)])


def _out_proj_call(oa_pieces, ob_pieces, oc, x_pieces, w_out, l, mod4, norm_g, router_w=None):
    tm = OUT_TM
    n = T_ALL // tm
    moe = router_w is not None
    row = functools.partial(_mod_row, tm=tm)
    mod_spec = lambda k: pl.BlockSpec((None, None, 1, D_MODEL), lambda i: (row(i), k, 0, 0))
    tok = lambda w: pl.BlockSpec((tm, w), lambda i: (i, 0))
    full = lambda a: pl.BlockSpec(a.shape, lambda i: (0,) * a.ndim)
    in_specs = (_piece_specs(oa_pieces, tm, W_A) + _piece_specs(ob_pieces, tm, W_B) + [tok(W_C)]
                + _piece_specs(x_pieces, tm, D_MODEL)
                + [_resident_layer(w_out, l), mod_spec(2), full(norm_g), mod_spec(4), mod_spec(3)])
    args = [*oa_pieces, *ob_pieces, oc, *x_pieces, w_out, mod4, norm_g, mod4, mod4]
    if moe:
        in_specs.append(full(router_w))
        args.append(router_w)
        out_specs = [tok(D_MODEL), pl.BlockSpec((tm * 8, LANES), lambda i: (i, 0)), tok(LANES),
                     pl.BlockSpec((LANES, tm), lambda i: (0, i))]
        out_shape = [jax.ShapeDtypeStruct((T_ALL, D_MODEL), F32), jax.ShapeDtypeStruct((T_ALL * 8, LANES), F32),
                     jax.ShapeDtypeStruct((T_ALL, LANES), F32), jax.ShapeDtypeStruct((LANES, T_ALL), F32)]
    else:
        out_specs = [tok(D_MODEL), tok(D_MODEL)]
        out_shape = [jax.ShapeDtypeStruct((T_ALL, D_MODEL), F32), jax.ShapeDtypeStruct((T_ALL, D_MODEL), BF16)]
    return pl.pallas_call(
        functools.partial(_out_proj_kernel, n_x=len(x_pieces), moe=moe),
        grid=(n,),
        in_specs=in_specs,
        out_specs=out_specs,
        out_shape=out_shape,
        scratch_shapes=[pltpu.VMEM((D_MODEL, D_MODEL), BF16)],
        compiler_params=_cparams(("arbitrary",)),
        name="out_proj_moe" if moe else "out_proj",
    )(*args)


FFN_TM = 512
MXU_N = 256
FFN_SPLITS = (0, 1024, 2048, D_FF)
assert all(s % MXU_N == 0 for s in FFN_SPLITS)


def _swiglu(h, wg_ref, wu_ref, wd_ref):
    out = None
    for c0, c1 in zip(FFN_SPLITS[:-1], FFN_SPLITS[1:]):
        act = _silu(_dot(h, wg_ref[:, c0:c1])) * _dot(h, wu_ref[:, c0:c1])
        d = _dot(act.astype(BF16), wd_ref[c0:c1, :])
        out = d if out is None else out + d
    return out


def _ffn_kernel(h_ref, x_ref, g2_ref, wg_ref, wu_ref, wd_ref, w_ref, o_ref, wb_ref):
    @pl.when(pl.program_id(0) < RIDER_BLOCKS_20)
    def _():
        wb_ref[...] = w_ref[...].astype(BF16)

    o_ref[...] = x_ref[...] + g2_ref[...] * _swiglu(h_ref[...], wg_ref, wu_ref, wd_ref)


def _ffn_call(h2, x1, mod4, wg, wu, wd, cast_w):
    tm = FFN_TM
    row = functools.partial(_mod_row, tm=tm)
    tok = lambda w: pl.BlockSpec((tm, w), lambda i: (i, 0))
    resident = lambda a: pl.BlockSpec(a.shape, lambda i: (0, 0), pipeline_mode=pl.Buffered(1))
    rider_spec, rider_shape = _cast_rider(cast_w, RIDER_BLOCKS_20)
    return pl.pallas_call(
        _ffn_kernel,
        grid=(T_ALL // tm,),
        in_specs=[tok(D_MODEL), tok(D_MODEL),
                  pl.BlockSpec((None, None, 1, D_MODEL), lambda i: (row(i), 5, 0, 0)),
                  resident(wg), resident(wu), resident(wd), rider_spec],
        out_specs=[tok(D_MODEL), rider_spec],
        out_shape=[jax.ShapeDtypeStruct((T_ALL, D_MODEL), F32), rider_shape],
        compiler_params=_cparams(("arbitrary",)),
        name="ffn_dense",
    )(h2, x1, mod4, wg, wu, wd, cast_w)


MOE_TM = 512
MOE_NT_MAX = (2 * T_ALL) // MOE_TM + N_EXPERTS
MOE_ROWS = MOE_NT_MAX * MOE_TM
PLAN_BLK = 512
MISC_LAST_START = 8
MISC_NT = 16


def _moe_plan_kernel(gt_ref, posa_ref, posb_ref, te_ref, ti_ref, tv_ref, misc_ref):
    tm = float(MOE_TM)
    sel = gt_ref[SEL_LANE0:SEL_LANE0 + N_EXPERTS, :]
    cnt = jnp.sum(sel, axis=1, keepdims=True)
    nt = jnp.floor((cnt + (tm - 1.0)) * (1.0 / tm))
    sub = lax.broadcasted_iota(jnp.int32, (N_EXPERTS, LANES), 0).astype(F32)
    lane = lax.broadcasted_iota(jnp.int32, (N_EXPERTS, LANES), 1).astype(F32)
    nt_b = jnp.broadcast_to(nt, (N_EXPERTS, LANES))
    nt_row = jnp.sum(jnp.where(sub == lane, nt_b, 0.0), axis=0, keepdims=True)
    toff = jnp.sum(jnp.where(lane < sub, jnp.broadcast_to(nt_row, (N_EXPERTS, LANES)), 0.0),
                   axis=1, keepdims=True)
    tend = toff + nt
    n_total = jnp.sum(nt, axis=0, keepdims=True)
    jc = jnp.minimum(lane, n_total - 1.0)
    te = jnp.sum(jnp.where(jc >= tend, 1.0, 0.0), axis=0, keepdims=True)
    te_ref[...] = te.astype(jnp.int32)
    ti_ref[...] = jc[0:1, :].astype(jnp.int32)
    rows_left = cnt - (jc - toff) * tm
    tv = jnp.sum(jnp.where((jc >= toff) & (jc < tend), jnp.minimum(rows_left, tm), 0.0), axis=0, keepdims=True)
    tv_ref[...] = tv.astype(jnp.int32)
    last_start = (tend - 1.0) * tm
    ls_row = jnp.sum(jnp.where(sub + MISC_LAST_START == lane, jnp.broadcast_to(last_start, (N_EXPERTS, LANES)), 0.0),
                     axis=0, keepdims=True)
    nt_row2 = jnp.sum(jnp.where(sub + MISC_NT == lane, nt_b, 0.0), axis=0, keepdims=True)
    misc = jnp.where(lane[0:1, :] == 0.0, n_total, 0.0) + ls_row + nt_row2
    misc_ref[...] = misc.astype(jnp.int32)

    off = toff * tm
    r = lax.broadcasted_iota(jnp.int32, (PLAN_BLK, PLAN_BLK), 0)
    c = lax.broadcasted_iota(jnp.int32, (PLAN_BLK, PLAN_BLK), 1)
    upper = jnp.where(r <= c, 1.0, 0.0).astype(BF16)
    carry = jnp.zeros((N_EXPERTS, 1), F32)
    for blk in range(T_ALL // PLAN_BLK):
        cols = slice(blk * PLAN_BLK, (blk + 1) * PLAN_BLK)
        s = gt_ref[SEL_LANE0:SEL_LANE0 + N_EXPERTS, cols]
        rank = _dot(s.astype(BF16), upper) + carry
        pos = off + rank - 1.0
        posa_ref[:, cols] = jnp.min(jnp.where(s > 0.0, pos, 1e9), axis=0, keepdims=True).astype(jnp.int32)
        posb_ref[:, cols] = jnp.max(jnp.where(s > 0.0, pos, -1.0), axis=0, keepdims=True).astype(jnp.int32)
        carry = carry + jnp.sum(s, axis=1, keepdims=True)


def _moe_plan_call(gates_t):
    row = lambda w: jax.ShapeDtypeStruct((1, w), jnp.int32)
    full = lambda w: pl.BlockSpec((1, w), lambda: (0, 0))
    return pl.pallas_call(
        _moe_plan_kernel,
        in_specs=[pl.BlockSpec((LANES, T_ALL), lambda: (0, 0))],
        out_specs=[full(T_ALL), full(T_ALL), full(LANES), full(LANES), full(LANES), full(LANES)],
        out_shape=[row(T_ALL), row(T_ALL), row(LANES), row(LANES), row(LANES), row(LANES)],
        compiler_params=pltpu.CompilerParams(vmem_limit_bytes=VMEM_LIMIT),
        name="moe_plan",
    )(gates_t)


DMA_UNROLL = 8


def _row_tile(ref, row):
    return ref.at[pl.ds(pl.multiple_of(row * 8, 8), 8), :]


def _moe_scatter_kernel(misc_ref, posa_ref, posb_ref, h_ref, xs_ref, zero_ref, sem):
    tm = h_ref.shape[0] // 8

    @pl.when(pl.program_id(0) == 0)
    def _():
        zero_ref[...] = jnp.zeros_like(zero_ref)

        def zero_tile(first_row):
            start = pl.multiple_of(first_row * 8, 8)
            cp = pltpu.make_async_copy(zero_ref, xs_ref.at[pl.ds(start, MOE_TM * 8), :], sem.at[0])
            cp.start()
            cp.wait()

        for e in range(N_EXPERTS):
            @pl.when(misc_ref[0, MISC_NT + e] > 0)
            def _():
                zero_tile(misc_ref[0, MISC_LAST_START + e])

        def zero_tail(j, carry):
            zero_tile(j * MOE_TM)
            return carry

        lax.fori_loop(misc_ref[0, 0], MOE_NT_MAX, zero_tail, 0)

    def issue(r, carry):
        src = _row_tile(h_ref, r)
        pltpu.make_async_copy(src, _row_tile(xs_ref, posa_ref[0, r]), sem.at[0]).start(priority=0)
        pltpu.make_async_copy(src, _row_tile(xs_ref, posb_ref[0, r]), sem.at[1]).start(priority=1)
        return carry

    lax.fori_loop(0, tm, issue, 0, unroll=DMA_UNROLL)
    for k in range(2):
        pltpu.make_async_copy(h_ref, xs_ref.at[pl.ds(0, tm * 8), :], sem.at[k]).wait()


SCATTER_TM = 512


def _moe_scatter_call(misc, posa3, posb3, h2t):
    tm = SCATTER_TM
    smem_row = pl.BlockSpec((None, 1, tm), lambda i: (i, 0, 0), memory_space=pltpu.SMEM)
    return pl.pallas_call(
        _moe_scatter_kernel,
        grid=(T_ALL // tm,),
        in_specs=[pl.BlockSpec((1, LANES), lambda i: (0, 0), memory_space=pltpu.SMEM), smem_row, smem_row,
                  pl.BlockSpec((tm * 8, LANES), lambda i: (i, 0))],
        out_specs=pl.BlockSpec(memory_space=pl.ANY),
        out_shape=jax.ShapeDtypeStruct((MOE_ROWS * 8, LANES), F32),
        scratch_shapes=[pltpu.VMEM((MOE_TM * 8, LANES), F32), pltpu.SemaphoreType.DMA((2,))],
        compiler_params=_cparams(("arbitrary",)),
        name="moe_scatter",
    )(misc, posa3, posb3, h2t)


MOE_HALF = MOE_TM // 2


def _ffn_grouped_kernel(te_ref, ti_ref, tv_ref, misc_ref, x_ref, wg_ref, wu_ref, wd_ref, o_ref):
    j = pl.program_id(0)
    live = j < misc_ref[0]
    rows = tv_ref[j]

    @pl.when(live & (rows > MOE_HALF))
    def _():
        h = _load_token_tiles(x_ref, MOE_TM).astype(BF16)
        _store_token_tiles(o_ref, _swiglu(h, wg_ref, wu_ref, wd_ref))

    @pl.when(live & (rows <= MOE_HALF))
    def _():
        h = _load_token_tiles(x_ref, MOE_HALF).astype(BF16)
        _store_token_tiles(o_ref, _swiglu(h, wg_ref, wu_ref, wd_ref))
        o_ref[MOE_HALF * 8:, :] = jnp.zeros((MOE_HALF * 8, LANES), F32)

    @pl.when(jnp.logical_not(live))
    def _():
        o_ref[...] = jnp.zeros_like(o_ref)


def _ffn_grouped_call(te, ti, tv, misc, xs, wg, wu, wd):
    expert = lambda *s: pl.BlockSpec((None,) + s, lambda j, te, ti, tv, misc: (te[j], 0, 0))
    grid_spec = pltpu.PrefetchScalarGridSpec(
        num_scalar_prefetch=4,
        grid=(MOE_NT_MAX,),
        in_specs=[pl.BlockSpec((MOE_TM * 8, LANES), lambda j, te, ti, tv, misc: (ti[j], 0)),
                  expert(D_MODEL, D_FF), expert(D_MODEL, D_FF), expert(D_FF, D_MODEL)],
        out_specs=pl.BlockSpec((MOE_TM * 8, LANES), lambda j, te, ti, tv, misc: (j, 0)),
    )
    return pl.pallas_call(
        _ffn_grouped_kernel,
        grid_spec=grid_spec,
        out_shape=jax.ShapeDtypeStruct((MOE_ROWS * 8, LANES), F32),
        compiler_params=_cparams(("arbitrary",)),
        name="ffn_grouped",
    )(te, ti, tv, misc, xs, wg, wu, wd)


COMBINE_TM = 256


def _moe_combine_kernel(posa_ref, posb_ref, posa_next_ref, posb_next_ref, ys_ref, x_ref, g2_ref, gates_ref, nf_ref,
                        op_ref, os_ref, bufa_ref, bufb_ref, sem):
    tm = COMBINE_TM
    i = pl.program_id(0)
    n = pl.num_programs(0)
    slot = i % 2

    def gather(pa_ref, pb_ref, s):
        def issue(r, carry):
            pltpu.make_async_copy(_row_tile(ys_ref, pa_ref[0, r]), _row_tile(bufa_ref.at[s], r),
                                  sem.at[s, 0]).start(priority=0)
            pltpu.make_async_copy(_row_tile(ys_ref, pb_ref[0, r]), _row_tile(bufb_ref.at[s], r),
                                  sem.at[s, 1]).start(priority=1)
            return carry

        lax.fori_loop(0, tm, issue, 0, unroll=DMA_UNROLL)

    @pl.when(i == 0)
    def _():
        gather(posa_ref, posb_ref, 0)

    @pl.when(i + 1 < n)
    def _():
        gather(posa_next_ref, posb_next_ref, 1 - slot)

    gates = gates_ref[...]
    lane = lax.broadcasted_iota(jnp.int32, gates.shape, 1).astype(F32)
    is_sel = (lane >= SEL_LANE0) & (lane < SEL_LANE0 + N_EXPERTS) & (gates > 0.0)
    ia = jnp.min(jnp.where(is_sel, lane, float(LANES)), axis=-1, keepdims=True) - SEL_LANE0
    ib = jnp.max(jnp.where(is_sel, lane, -1.0), axis=-1, keepdims=True) - SEL_LANE0
    wa = jnp.sum(jnp.where(lane == ia, gates, 0.0), axis=-1, keepdims=True)
    wb = jnp.sum(jnp.where(lane == ib, gates, 0.0), axis=-1, keepdims=True)

    pltpu.make_async_copy(ys_ref.at[pl.ds(0, tm * 8), :], bufa_ref.at[slot], sem.at[slot, 0]).wait()
    pltpu.make_async_copy(ys_ref.at[pl.ds(0, tm * 8), :], bufb_ref.at[slot], sem.at[slot, 1]).wait()

    y = wa * _load_token_tiles(bufa_ref.at[slot], tm) + wb * _load_token_tiles(bufb_ref.at[slot], tm)
    out = x_ref[...] + g2_ref[...] * y
    out = (out * lax.rsqrt(jnp.mean(out * out, axis=-1, keepdims=True) + EPS)) * nf_ref[...]

    @pl.when(i < T_PROMPT // tm)
    def _():
        op_ref[...] = out

    @pl.when(i >= T_PROMPT // tm)
    def _():
        os_ref[...] = out


def _moe_combine_call(posa3, posb3, ys, x1, mod4, gates, norm_f):
    tm = COMBINE_TM
    n = T_ALL // tm
    n_p = T_PROMPT // tm
    row = functools.partial(_mod_row, tm=tm)
    smem_row = pl.BlockSpec((None, 1, tm), lambda i: (i, 0, 0), memory_space=pltpu.SMEM)
    smem_next = pl.BlockSpec((None, 1, tm), lambda i: (jnp.minimum(i + 1, n - 1), 0, 0), memory_space=pltpu.SMEM)
    tok = lambda w: pl.BlockSpec((tm, w), lambda i: (i, 0))
    return pl.pallas_call(
        _moe_combine_kernel,
        grid=(n,),
        in_specs=[smem_row, smem_row, smem_next, smem_next, pl.BlockSpec(memory_space=pl.ANY), tok(D_MODEL),
                  pl.BlockSpec((None, None, 1, D_MODEL), lambda i: (row(i), 5, 0, 0)), tok(LANES),
                  pl.BlockSpec((1, D_MODEL), lambda i: (0, 0))],
        out_specs=[pl.BlockSpec((tm, D_MODEL), lambda i: (jnp.minimum(i, n_p - 1), 0)),
                   pl.BlockSpec((tm, D_MODEL), lambda i: (jnp.maximum(i - n_p, 0), 0))],
        out_shape=[jax.ShapeDtypeStruct((T_PROMPT, D_MODEL), F32), jax.ShapeDtypeStruct((T_SAMPLE, D_MODEL), F32)],
        scratch_shapes=[pltpu.VMEM((2, tm * 8, LANES), F32), pltpu.VMEM((2, tm * 8, LANES), F32),
                        pltpu.SemaphoreType.DMA((2, 2))],
        compiler_params=_cparams(("arbitrary",)),
        name="moe_combine",
    )(posa3, posb3, posa3, posb3, ys, x1, mod4, gates, norm_f)


def _moe_call(h2t, x1, mod4, gates, gates_t, wg, wu, wd, norm_f):
    posa, posb, te, ti, tv, misc = _moe_plan_call(gates_t)
    xs = _moe_scatter_call(misc, posa.reshape(T_ALL // SCATTER_TM, 1, SCATTER_TM),
                           posb.reshape(T_ALL // SCATTER_TM, 1, SCATTER_TM), h2t)
    ys = _ffn_grouped_call(te.reshape(LANES), ti.reshape(LANES), tv.reshape(LANES), misc.reshape(LANES), xs, wg, wu, wd)
    return _moe_combine_call(posa.reshape(T_ALL // COMBINE_TM, 1, COMBINE_TM),
                             posb.reshape(T_ALL // COMBINE_TM, 1, COMBINE_TM), ys, x1, mod4, gates, norm_f)


def kernel(x_prompt, x_sample, cache_k, cache_v, state_gla_fwd, state_gla_bwd, c, c_ctx, w_ada, b_ada, norm_mix, norm_ffn, w_in, w_out, gla_w_up, gla_b_up, gla_norm, diff_lambda, diff_norm, sgu_w, sgu_b, ffn_w_gate, ffn_w_up, ffn_w_down, router_w, moe_w_gate, moe_w_up, moe_w_down, norm_f):
    assert DEPTH == 2
    x_pieces = [x_prompt.reshape(T_PROMPT, D_MODEL), x_sample.reshape(T_SAMPLE, D_MODEL)]
    cvecs = jnp.concatenate([c_ctx[None, :], c, jnp.zeros((N_MOD_ROWS - 1 - DEC_BATCH, D_MODEL), F32)], axis=0)
    mod = _ada_call(cvecs, w_ada, b_ada)
    cos, sin_signed = _rope_tables()
    zeros_state = jnp.zeros((BATCH, 64, 256), F32)

    w_in_t = jnp.swapaxes(w_in, 1, 2)
    moe_w = [moe_w_gate[0], moe_w_up[0], moe_w_down[0]]
    for l in range(DEPTH):
        mod4 = mod[l].reshape(N_MOD_ROWS, 6, 1, D_MODEL)
        w_up = jnp.zeros((LANES, 2 * W_A), F32)
        w_up = w_up.at[0:GLA_RANK, 0:W_A].set(gla_w_up[l, 0]).at[GLA_RANK:2 * GLA_RANK, W_A:].set(gla_w_up[l, 1])
        b_up = gla_b_up[l].reshape(1, 2 * W_A)
        bs_full = jnp.repeat(sgu_b[l].T, DG_C, axis=1)
        g4, la, qb, kb, vb, oc, moe_w[l] = _in_proj_call(x_pieces, norm_mix[l][None, :], mod4, w_in_t, l,
                                                         w_up.astype(BF16), b_up, sgu_w[l].astype(BF16), bs_full,
                                                         moe_w[l])

        gain_a = gla_norm[l][None, :]
        if l == 0:
            oa_p, sf0, sb0 = _gla_call(g4, la, zeros_state, zeros_state, gain_a, batch=BATCH, seq=SEQ, row_block0=0)
            kb0, vb0 = kb, vb
        else:
            oa_p, new_sf, new_sb, new_cache_k = _gla_call(g4, la, zeros_state, zeros_state, gain_a, (sf0, sb0, kb0, kb),
                                                          batch=BATCH, seq=SEQ, row_block0=0)
        oa_s, _, _ = _gla_call(g4, la, _state_to_kernel(state_gla_fwd[:, l]),
                               _state_to_kernel(state_gla_bwd[:, l]), gain_a,
                               batch=DEC_BATCH, seq=DEC_SEQ, row_block0=T_PROMPT // DEC_SEQ)

        lam_init = 0.8 - 0.6 * math.exp(-0.3 * l)
        gain_b = diff_norm[l][None, :]
        if l == 0:
            (ob_p,) = _attn_prompt_call(diff_lambda[l], qb, kb, vb, gain_b, lam_init=lam_init)
        else:
            ob_p, new_cache_v = _attn_prompt_call(diff_lambda[l], qb, kb, vb, gain_b, (vb0,),
                                                  lam_init=lam_init, write_cache=True)
        ob_s, *dense_w = _attn_sample_call(diff_lambda[l], qb, kb, vb,
                                           cache_k[:, l].reshape(DEC_BATCH, PAST_LEN, W_B),
                                           cache_v[:, l].reshape(DEC_BATCH, PAST_LEN, W_B),
                                           cos, sin_signed, gain_b,
                                           (ffn_w_gate[0], ffn_w_up[0], ffn_w_down[0]) if l == 0 else (),
                                           lam_init=lam_init)

        if l == 0:
            x1, h2 = _out_proj_call([oa_p, oa_s], [ob_p, ob_s], oc, x_pieces, w_out, l, mod4, norm_ffn[l][None, :])
            x_next, moe_w[2] = _ffn_call(h2, x1, mod4, *dense_w, moe_w[2])
            x_pieces = [x_next]
        else:
            rw = jnp.pad(router_w[0], ((0, 0), (0, LANES - N_EXPERTS)))
            x1, h2t, gates, gates_t = _out_proj_call([oa_p, oa_s], [ob_p, ob_s], oc, x_pieces, w_out, l, mod4,
                                                     norm_ffn[l][None, :], rw)
            y_prompt, y_sample = _moe_call(h2t, x1, mod4, gates, gates_t, *moe_w, norm_f[None, :])

    state_shape = (BATCH, DEPTH, N_HEADS_A, DK_A, DK_A)
    return (y_prompt.reshape(BATCH, SEQ, D_MODEL), y_sample.reshape(DEC_BATCH, DEC_SEQ, D_MODEL),
            new_cache_k.reshape(BATCH, DEPTH, SEQ, N_HEADS_B, DV_B), new_cache_v.reshape(BATCH, DEPTH, SEQ, N_HEADS_B, DV_B),
            new_sf.reshape(state_shape), new_sb.reshape(state_shape))
```

```python
import functools
import math

import jax
import jax.numpy as jnp
import numpy as np
from jax import lax
from jax.experimental import pallas as pl
from jax.experimental.pallas import tpu as pltpu

F32 = jnp.float32
BF16 = jnp.bfloat16

D_MODEL = 1024
BATCH = 32
SEQ = 256
DEPTH = 2
DEC_BATCH = 2
DEC_SEQ = 1024
PAST_LEN = 256
GRID_W = 64
N_HEADS_A = 4
DK_A = 64
W_A = 256
GLA_RANK = 16
GLA_TAU = 16.0
GLA_CHUNK = 64
N_HEADS_B = 4
DH_B = 64
DV_B = 128
W_B = 512
ROPE_THETA = 10000.0
AXIS_PAIRS = DH_B // 4
N_GROUPS_C = 4
DG_C = 64
W_C = 256
SGU_CHUNK = 128
D_FF = 2816
N_EXPERTS = 8
EPS = 1e-6

T_PROMPT = BATCH * SEQ
T_SAMPLE = DEC_BATCH * DEC_SEQ
T_ALL = T_PROMPT + T_SAMPLE
N_MOD_ROWS = 8
LANES = 128
VMEM_LIMIT = 56 * 1024 * 1024


def _cparams(sem):
    return pltpu.CompilerParams(dimension_semantics=sem, vmem_limit_bytes=VMEM_LIMIT)


def _dot(a, b):
    return jnp.dot(a, b, preferred_element_type=F32)


def _dot_nt(a, b):
    return lax.dot_general(a, b, (((1,), (1,)), ((), ())), preferred_element_type=F32)


def _dot_tn(a, b):
    return lax.dot_general(a, b, (((0,), (0,)), ((), ())), preferred_element_type=F32)


def _split_bf16(x):
    hi = x.astype(BF16)
    lo = (x - hi.astype(F32)).astype(BF16)
    return hi, lo


def _dot3(a, w):
    a_hi, a_lo = _split_bf16(a)
    w_hi, w_lo = _split_bf16(w)
    return _dot(a_hi, w_hi) + (_dot(a_lo, w_hi) + _dot(a_hi, w_lo))


def _sigmoid(x):
    return 1.0 / (1.0 + jnp.exp(-x))


def _silu(x):
    return x * _sigmoid(x)


def _gelu_tanh(x):
    c = math.sqrt(2.0 / math.pi)
    return x * (0.5 * (1.0 + jnp.tanh(c * (x + 0.044715 * (x * x * x)))))


def _log_sigmoid(x):
    return jnp.minimum(x, 0.0) - jnp.log(1.0 + jnp.exp(-jnp.abs(x)))


def _mod_row(i, tm):
    n_p = T_PROMPT // tm
    per_b = DEC_SEQ // tm
    return jnp.where(i < n_p, 0, 1 + (i - n_p) // per_b)


ADA_TN = 1536


def _ada_kernel(c_ref, w_ref, b_ref, o_ref):
    a = _silu(c_ref[...])
    o_ref[...] = _dot3(a, w_ref[...]) + b_ref[...]


def _ada_call(cvecs, w_ada, b_ada):
    n_col = (6 * D_MODEL) // ADA_TN
    return pl.pallas_call(
        _ada_kernel,
        grid=(DEPTH, n_col),
        in_specs=[
            pl.BlockSpec((N_MOD_ROWS, D_MODEL), lambda l, j: (0, 0)),
            pl.BlockSpec((None, D_MODEL, ADA_TN), lambda l, j: (l, 0, j)),
            pl.BlockSpec((None, 1, ADA_TN), lambda l, j: (l, 0, j)),
        ],
        out_specs=pl.BlockSpec((None, N_MOD_ROWS, ADA_TN), lambda l, j: (l, 0, j)),
        out_shape=jax.ShapeDtypeStruct((DEPTH, N_MOD_ROWS, 6 * D_MODEL), F32),
        compiler_params=_cparams(("arbitrary", "arbitrary")),
        name="ada_mod",
    )(cvecs, w_ada, b_ada.reshape(DEPTH, 1, 6 * D_MODEL))


IN_TM = 512
IN_COLS = 3104
Z_COL0 = 1024
Z_COLS = 2 * GLA_RANK
W_MAIN = 3072


def _piece_specs(pieces, tm, width):
    specs, t0 = [], 0
    for arr in pieces:
        nt = arr.shape[0] // tm
        specs.append(pl.BlockSpec((tm, width), lambda i, t0=t0, nt=nt: (jnp.clip(i - t0, 0, nt - 1), 0)))
        t0 += nt
    assert t0 * tm == T_ALL and len(pieces) in (1, 2) and (len(pieces) == 1 or pieces[0].shape[0] == T_PROMPT)
    return specs


def _pick_piece(refs, tm, rows=slice(None)):
    if len(refs) == 1:
        return refs[0][rows, :]
    return jnp.where(pl.program_id(0) < T_PROMPT // tm, refs[0][rows, :], refs[1][rows, :])


RIDER_BLOCKS_20 = 16


W_STAGE_ROWS = 512


def _in_proj_kernel(*refs, n_x, layer):
    x_refs = refs[:n_x]
    (nrm_ref, sh_ref, sc_ref, w_hbm, wup_ref, bup_ref, ws_ref, bs_ref, cast_ref,
     g4_ref, la_ref, qb_ref, kb_ref, vb_ref, oc_ref, cast_out_ref, wm_ref, wz_ref, stage_ref, sem) = refs[n_x:]

    @pl.when(pl.program_id(0) < RIDER_BLOCKS_20)
    def _():
        cast_out_ref[...] = cast_ref[...].astype(BF16)

    @pl.when(pl.program_id(0) == 0)
    def _():
        n = W_STAGE_ROWS
        moves = [(src, src if src < Z_COL0 else src - Z_COLS)
                 for src in (*range(0, Z_COL0, n), *range(Z_COL0 + Z_COLS, IN_COLS, n))]
        copy = lambda k: pltpu.make_async_copy(w_hbm.at[layer, pl.ds(moves[k][0], n), :], stage_ref.at[k % 2],
                                               sem.at[k % 2])
        copy(0).start()
        for k, (_, dst) in enumerate(moves):
            if k + 1 < len(moves):
                copy(k + 1).start()
            copy(k).wait()
            wm_ref[dst:dst + n, :] = stage_ref[k % 2].astype(BF16)
        gate_rows = pltpu.make_async_copy(w_hbm.at[layer, pl.ds(Z_COL0, Z_COLS), :],
                                          stage_ref.at[0, pl.ds(0, Z_COLS), :], sem.at[0])
        gate_rows.start()
        gate_rows.wait()
        wz_ref[...] = jnp.zeros_like(wz_ref)
        wz_ref[0:Z_COLS, :] = stage_ref[0, 0:Z_COLS, :].astype(BF16)

    x = _pick_piece(x_refs, IN_TM)
    y = x * lax.rsqrt(jnp.mean(x * x, axis=-1, keepdims=True) + EPS)
    h = (y * nrm_ref[...]) * (1.0 + sc_ref[...]) + sh_ref[...]
    hb = h.astype(BF16)
    def plain_job(ref, c0, w0):
        def job():
            ref[:, c0:c0 + MXU_N] = _dot_nt(hb, wm_ref[w0:w0 + MXU_N, :])
        return job

    jobs = [plain_job(ref, c, w0 + c) for ref, w0, width in
            ((g4_ref, 0, 1024), (qb_ref, 1024, 512), (kb_ref, 1536, 512), (vb_ref, 2048, 512))
            for c in range(0, width, MXU_N)]
    z = _dot_nt(hb, wz_ref[...])
    jobs.pop(0)()
    zz = _dot(z.astype(BF16), wup_ref[...]) + bup_ref[...]
    uv = _dot_nt(hb, wm_ref[2560:3072, :])
    jobs.pop(0)()
    la_ref[...] = _log_sigmoid(zz) * (1.0 / GLA_TAU)
    jobs.pop(0)()
    oc_ref[...] = _sgu(uv, ws_ref, bs_ref, jobs)


def _resident_layer(a, l):
    return pl.BlockSpec((None,) + a.shape[1:], lambda *_: (l,) + (0,) * (a.ndim - 1), pipeline_mode=pl.Buffered(1))


def _in_proj_call(x_pieces, norm_g, mod4, w_in_t, l, w_up, b_up, sgu_ws, sgu_bs, cast_w):
    tm = IN_TM
    rider_spec, rider_shape = _cast_rider(cast_w, RIDER_BLOCKS_20)
    n = T_ALL // tm
    row = functools.partial(_mod_row, tm=tm)
    mod_spec = lambda k: pl.BlockSpec((None, None, 1, D_MODEL), lambda i: (row(i), k, 0, 0))
    full = lambda a: pl.BlockSpec(a.shape, lambda i: (0,) * a.ndim)
    out = lambda w: pl.BlockSpec((tm, w), lambda i: (i, 0))
    return pl.pallas_call(
        functools.partial(_in_proj_kernel, n_x=len(x_pieces), layer=l),
        grid=(n,),
        in_specs=_piece_specs(x_pieces, tm, D_MODEL) + [full(norm_g), mod_spec(0), mod_spec(1), pl.BlockSpec(memory_space=pl.ANY),
                                                        full(w_up), full(b_up), full(sgu_ws), full(sgu_bs),
                                                        rider_spec],
        out_specs=[out(1024), out(512), out(512), out(512), out(512), out(W_C), rider_spec],
        out_shape=[jax.ShapeDtypeStruct((T_ALL, w), F32) for w in (1024, 512, 512, 512, 512)]
        + [jax.ShapeDtypeStruct((T_ALL, W_C), BF16), rider_shape],
        scratch_shapes=[pltpu.VMEM((W_MAIN, D_MODEL), BF16), pltpu.VMEM((LANES, D_MODEL), BF16),
                        pltpu.VMEM((2, W_STAGE_ROWS, D_MODEL), F32), pltpu.SemaphoreType.DMA((2,))],
        compiler_params=_cparams(("arbitrary",)),
        name="in_proj",
    )(*x_pieces, norm_g, mod4, mod4, w_in_t, w_up, b_up, sgu_ws, sgu_bs, cast_w)


GLA_SB = 256
GLA_NC = GLA_SB // GLA_CHUNK


def _head_blocks(x, same64):
    return jnp.where(same64, jnp.concatenate([x] * N_HEADS_A, axis=0), 0.0).astype(BF16)


def _gla_superblock(q, k, vb, v_blocks, la, st_all, tri, mask4, same64, forward):
    c = GLA_CHUNK
    mid, last = (c // 2 - 1, c - 1) if forward else (c // 2, 0)
    la_hi, la_lo = _split_bf16(la)
    b = _dot(tri, la_hi) + _dot(tri, la_lo)
    yield
    rows_of = lambda r: jnp.concatenate(
        [jnp.broadcast_to(b[i * c + r:i * c + r + 1, :], (c, W_A)) for i in range(GLA_NC)], axis=0)
    m = rows_of(mid)
    bl = rows_of(last)
    qe = (q * jnp.exp(b - m)).astype(BF16)
    ke = k * jnp.exp(m - b)
    qi = (q * jnp.exp(b)).astype(BF16)
    ks = (k * jnp.exp(bl - b)).astype(BF16)
    outs = [None] * GLA_NC
    for i in (range(GLA_NC) if forward else reversed(range(GLA_NC))):
        rows = slice(i * c, (i + 1) * c)
        s = _dot_nt(qe[rows, :], _head_blocks(ke[rows, :], same64))
        kv = _dot_tn(vb[rows, :], ks[rows, :])
        yield
        a = jnp.where(mask4, s, 0.0).astype(BF16)
        outs[i] = _dot(a, v_blocks[i]) + _dot_nt(qi[rows, :], st_all.astype(BF16))
        st_all = st_all * jnp.exp(bl[i * c:i * c + 1, :]) + jnp.where(same64, kv, 0.0)
        yield
    return jnp.concatenate(outs, axis=0), st_all


def _interleave(*gens):
    results = [None] * len(gens)
    active = list(enumerate(gens))
    while active:
        for item in list(active):
            try:
                next(item[1])
            except StopIteration as stop:
                results[item[0]] = stop.value
                active.remove(item)
    return results


def _cast_rider(w, n_blocks):
    e, r, c = w.shape
    per = n_blocks // e
    rows = r // per
    assert per * e == n_blocks and rows * per == r and rows % 16 == 0

    def index(b):
        b = jnp.minimum(b, n_blocks - 1)
        return b // per, b % per, 0

    spec = pl.BlockSpec((None, rows, c), index)
    return spec, jax.ShapeDtypeStruct(w.shape, BF16)


def _gla_kernel(*refs, seq, final):
    if final:
        (g4_ref, la_ref, s0f_ref, s0b_ref, gain_ref, sf_prev_ref, sb_prev_ref, k_prev_ref, k_ref,
         o_ref, sf_ref, sb_ref, ck_ref, of_ref, ob_ref) = refs
        for l, src in enumerate((k_prev_ref, k_ref)):
            for h in range(N_HEADS_B):
                ck_ref[l, pl.ds(h, SEQ, stride=N_HEADS_B), :] = src[:, h * DV_B:(h + 1) * DV_B]
    else:
        g4_ref, la_ref, s0f_ref, s0b_ref, gain_ref, o_ref, sf_ref, sb_ref, of_ref, ob_ref = refs
    n = GLA_SB
    nsb = seq // n
    r = lax.broadcasted_iota(jnp.int32, (n, n), 0)
    s = lax.broadcasted_iota(jnp.int32, (n, n), 1)
    same64 = (r // GLA_CHUNK) == (s // GLA_CHUNK)
    lower = same64 & (s <= r)
    upper = same64 & (s >= r)
    tri_f = jnp.where(lower, 1.0, 0.0).astype(BF16)
    tri_b = jnp.where(upper, 1.0, 0.0).astype(BF16)
    ones64 = jnp.where(same64, 1.0, 0.0).astype(BF16)
    key_row = lax.broadcasted_iota(jnp.int32, (GLA_CHUNK, n), 1) % GLA_CHUNK
    qry_row = lax.broadcasted_iota(jnp.int32, (GLA_CHUNK, n), 0)
    mask4_f = key_row <= qry_row
    mask4_b = key_row >= qry_row
    scale = DK_A ** -0.5
    expand = lambda st: jnp.where(same64, jnp.concatenate([st] * N_HEADS_A, axis=0), 0.0)
    compact = lambda st_all: functools.reduce(
        lambda a, b: a + b, [st_all[h * 64:(h + 1) * 64, :] for h in range(N_HEADS_A)])

    def step(i, carry):
        stf, stb = carry
        rf = pl.ds(pl.multiple_of(i * n, n), n)
        rb = pl.ds(pl.multiple_of((nsb - 1 - i) * n, n), n)
        def direction(rows, la_cols, st, tri, mask4, forward):
            q = g4_ref[rows, 0:256] * scale
            k = g4_ref[rows, 256:512]
            v = g4_ref[rows, 512:768]
            v_blocks = [_head_blocks(v[i * GLA_CHUNK:(i + 1) * GLA_CHUNK, :], same64) for i in range(GLA_NC)]
            return _gla_superblock(q, k, v.astype(BF16), v_blocks, la_ref[rows, la_cols], st, tri, mask4, same64,
                                   forward)

        (o_f, stf), (o_b, stb) = _interleave(direction(rf, slice(0, 256), stf, tri_f, mask4_f, True),
                                             direction(rb, slice(256, 512), stb, tri_b, mask4_b, False))
        of_ref[rf, :] = o_f
        ob_ref[rb, :] = o_b
        return stf, stb

    stf, stb = lax.fori_loop(0, nsb, step, (expand(s0f_ref[...]), expand(s0b_ref[...])))
    if final:
        sf_ref[0] = sf_prev_ref[...].T
        sb_ref[0] = sb_prev_ref[...].T
        sf_ref[1] = compact(stf).T
        sb_ref[1] = compact(stb).T
    else:
        sf_ref[...] = compact(stf)
        sb_ref[...] = compact(stb)

    gain = gain_ref[...]

    def finish(i, carry):
        rows = pl.ds(pl.multiple_of(i * n, n), n)
        o = of_ref[rows, :] + ob_ref[rows, :]
        sq_hi, sq_lo = _split_bf16(o * o)
        ms = (_dot(sq_hi, ones64) + _dot(sq_lo, ones64)) * (1.0 / DK_A)
        y = (o * lax.rsqrt(ms + EPS)) * gain
        o_ref[rows, :] = (y * _silu(g4_ref[rows, 768:1024])).astype(BF16)
        return carry

    lax.fori_loop(0, nsb, finish, 0)


def _gla_call(g4, la, s0f, s0b, gain, prev=None, *, batch, seq, row_block0):
    tok = lambda w: pl.BlockSpec((seq, w), lambda b: (row_block0 + b, 0))
    st = pl.BlockSpec((None, 64, 256), lambda b: (b, 0, 0))
    in_specs = [tok(1024), tok(512), st, st, pl.BlockSpec((1, W_A), lambda b: (0, 0))]
    args = [g4, la, s0f, s0b, gain]
    o_spec = pl.BlockSpec((seq, W_A), lambda b: (b, 0))
    o_shape = jax.ShapeDtypeStruct((batch * seq, W_A), BF16)
    if prev is None:
        out_specs = [o_spec, st, st]
        out_shape = [o_shape, jax.ShapeDtypeStruct((batch, 64, 256), F32), jax.ShapeDtypeStruct((batch, 64, 256), F32)]
    else:
        assert seq == SEQ and batch == BATCH and DEPTH == 2
        kblk = pl.BlockSpec((SEQ, W_B), lambda b: (b, 0))
        in_specs += [st, st, kblk, kblk]
        args += list(prev)
        st2 = pl.BlockSpec((None, DEPTH, 256, 64), lambda b: (b, 0, 0, 0))
        st2_shape = jax.ShapeDtypeStruct((batch, DEPTH, 256, 64), F32)
        out_specs = [o_spec, st2, st2, pl.BlockSpec((None, DEPTH, SEQ * N_HEADS_B, DV_B), lambda b: (b, 0, 0, 0))]
        out_shape = [o_shape, st2_shape, st2_shape, jax.ShapeDtypeStruct((batch, DEPTH, SEQ * N_HEADS_B, DV_B), F32)]
    return pl.pallas_call(
        functools.partial(_gla_kernel, seq=seq, final=prev is not None),
        grid=(batch,),
        in_specs=in_specs,
        out_specs=out_specs,
        out_shape=out_shape,
        scratch_shapes=[pltpu.VMEM((seq, W_A), F32), pltpu.VMEM((seq, W_A), F32)],
        compiler_params=_cparams(("arbitrary",)),
        name=f"gla_{seq}",
    )(*args)


def _state_to_kernel(s):
    b = s.shape[0]
    return jnp.transpose(s, (0, 3, 1, 2)).reshape(b, 64, 256)


def _lambda(lv, lam_init):
    l01 = jnp.sum(lv[0:1, :] * lv[1:2, :], axis=-1, keepdims=True)
    l23 = jnp.sum(lv[2:3, :] * lv[3:4, :], axis=-1, keepdims=True)
    return jnp.exp(l01) - jnp.exp(l23) + lam_init


def _softmax_parts(parts):
    mx = functools.reduce(jnp.maximum, [jnp.max(p, axis=-1, keepdims=True) for p in parts])
    es = [jnp.exp(p - mx) for p in parts]
    den = functools.reduce(lambda a, b: a + b, [jnp.sum(e, axis=-1, keepdims=True) for e in es])
    return [e / den for e in es]


def _diff_finish(o, gain, lam_init):
    o = o * lax.rsqrt(jnp.mean(o * o, axis=-1, keepdims=True) + EPS)
    return ((o * gain) * (1.0 - lam_init)).astype(BF16)


QK_SCALE = DH_B ** -0.5


def _key_halves(k):
    first = lax.broadcasted_iota(jnp.int32, k.shape, 1) < DH_B
    return jnp.where(first, k, 0.0).astype(BF16), jnp.where(first, 0.0, k).astype(BF16)


def _attn_prompt_kernel(lv_ref, q_ref, k_ref, v_ref, gain_ref, *rest, lam_init, n_prev):
    prev_refs, (o_ref, *cache_refs) = rest[:n_prev], rest[n_prev:]
    lam = _lambda(lv_ref[...], lam_init)

    def head(h):
        cols = slice(h * DV_B, (h + 1) * DV_B)
        q = (q_ref[:, cols] * QK_SCALE).astype(BF16)
        k1, k2 = _key_halves(k_ref[:, cols])
        s1 = _dot_nt(q, k1)
        s2 = _dot_nt(q, k2)
        yield
        (p1,) = _softmax_parts([s1])
        (p2,) = _softmax_parts([s2])
        a = p1 - lam * p2
        o = _dot(a.astype(BF16), v_ref[:, cols].astype(BF16))
        yield
        o_ref[:, cols] = _diff_finish(o, gain_ref[:, cols], lam_init)
        if cache_refs:
            (cv_ref,) = cache_refs
            for l, src in enumerate((*prev_refs, v_ref)):
                cv_ref[l, pl.ds(h, SEQ, stride=N_HEADS_B), :] = src[:, cols]

    _interleave(*[head(h) for h in range(N_HEADS_B)])


def _attn_prompt_call(lv, qb, kb, vb, gain, prev_v=(), *, lam_init, write_cache=False):
    blk = pl.BlockSpec((SEQ, W_B), lambda b: (b, 0))
    n_prev = len(prev_v)
    assert write_cache or not prev_v
    in_specs = [pl.BlockSpec((4, DH_B), lambda b: (0, 0)), blk, blk, blk,
                pl.BlockSpec((1, W_B), lambda b: (0, 0))] + [blk] * n_prev
    args = [lv, qb, kb, vb, gain, *prev_v]
    out_specs = [blk]
    out_shape = [jax.ShapeDtypeStruct((T_PROMPT, W_B), BF16)]
    if write_cache:
        out_specs.append(pl.BlockSpec((None, n_prev + 1, SEQ * N_HEADS_B, DV_B), lambda b: (b, 0, 0, 0)))
        out_shape.append(jax.ShapeDtypeStruct((BATCH, n_prev + 1, SEQ * N_HEADS_B, DV_B), F32))
    return pl.pallas_call(
        functools.partial(_attn_prompt_kernel, lam_init=lam_init, n_prev=n_prev),
        grid=(BATCH,),
        in_specs=in_specs,
        out_specs=out_specs,
        out_shape=out_shape,
        compiler_params=_cparams(("arbitrary",)),
        name="diff_attn_prompt",
    )(*args)


def _rope(x, cos, sin_signed):
    lane = lax.broadcasted_iota(jnp.int32, x.shape, 1)
    first = (lane % (2 * AXIS_PAIRS)) < AXIS_PAIRS
    partner = jnp.where(first, pltpu.roll(x, LANES - AXIS_PAIRS, 1), pltpu.roll(x, AXIS_PAIRS, 1))
    return x * cos + partner * sin_signed


ATT_TQ = 256


def _attn_sample_kernel(lv_ref, q_ref, k_ref, v_ref, kc_ref, vc_ref, cosq_ref, sinq_ref,
                        cosk_ref, sink_ref, gain_ref, *rest, lam_init, n_cast):
    w_refs, o_ref, wb_refs, (k1_ref, k2_ref) = rest[:n_cast], rest[n_cast], rest[n_cast + 1:2 * n_cast + 1], rest[-2:]
    for w_ref, wb_ref in zip(w_refs, wb_refs):
        wb_ref[...] = w_ref[...].astype(BF16)

    @pl.when(pl.program_id(1) == 0)
    def _():
        for h in range(N_HEADS_B):
            cols = slice(h * DV_B, (h + 1) * DV_B)
            k1_ref[:, cols], k2_ref[:, cols] = _key_halves(_rope(k_ref[:, cols], cosk_ref[...], sink_ref[...]))

    lam = _lambda(lv_ref[...], lam_init)

    def head(h):
        cols = slice(h * DV_B, (h + 1) * DV_B)
        q = (_rope(q_ref[:, cols], cosq_ref[...], sinq_ref[...]) * QK_SCALE).astype(BF16)
        c1, c2 = _key_halves(kc_ref[:, cols])
        s1 = [_dot_nt(q, k1_ref[:, cols]), _dot_nt(q, c1)]
        s2 = [_dot_nt(q, k2_ref[:, cols]), _dot_nt(q, c2)]
        yield
        p1 = _softmax_parts(s1)
        p2 = _softmax_parts(s2)
        a_own = p1[0] - lam * p2[0]
        a_ctx = p1[1] - lam * p2[1]
        o = (_dot(a_own.astype(BF16), v_ref[:, cols].astype(BF16))
             + _dot(a_ctx.astype(BF16), vc_ref[:, cols].astype(BF16)))
        yield
        o_ref[:, cols] = _diff_finish(o, gain_ref[:, cols], lam_init)

    _interleave(*[head(h) for h in range(N_HEADS_B)])


def _attn_sample_call(lv, qb, kb, vb, kc, vc, cos, sin_signed, gain, cast_ws=(), *, lam_init):
    tq = ATT_TQ
    nq = DEC_SEQ // tq
    p0 = T_PROMPT // tq
    s0 = T_PROMPT // DEC_SEQ
    qblk = pl.BlockSpec((tq, W_B), lambda b, t: (p0 + b * nq + t, 0))
    kvblk = pl.BlockSpec((DEC_SEQ, W_B), lambda b, t: (s0 + b, 0))
    cblk = pl.BlockSpec((None, PAST_LEN, W_B), lambda b, t: (b, 0, 0))
    n_steps = DEC_BATCH * nq
    cast_specs = []
    for w in cast_ws:
        assert w.shape[0] % (16 * n_steps) == 0
        cast_specs.append(pl.BlockSpec((w.shape[0] // n_steps, w.shape[1]), lambda b, t: (b * nq + t, 0)))
    return pl.pallas_call(
        functools.partial(_attn_sample_kernel, lam_init=lam_init, n_cast=len(cast_ws)),
        grid=(DEC_BATCH, nq),
        in_specs=[pl.BlockSpec((4, DH_B), lambda b, t: (0, 0)), qblk, kvblk, kvblk, cblk, cblk,
                  pl.BlockSpec((tq, DV_B), lambda b, t: (t, 0)),
                  pl.BlockSpec((tq, DV_B), lambda b, t: (t, 0)),
                  pl.BlockSpec((DEC_SEQ, DV_B), lambda b, t: (0, 0)),
                  pl.BlockSpec((DEC_SEQ, DV_B), lambda b, t: (0, 0)),
                  pl.BlockSpec((1, W_B), lambda b, t: (0, 0))] + cast_specs,
        out_specs=[pl.BlockSpec((tq, W_B), lambda b, t: (b * nq + t, 0))] + cast_specs,
        out_shape=[jax.ShapeDtypeStruct((T_SAMPLE, W_B), BF16)] + [jax.ShapeDtypeStruct(w.shape, BF16) for w in cast_ws],
        scratch_shapes=[pltpu.VMEM((DEC_SEQ, W_B), BF16), pltpu.VMEM((DEC_SEQ, W_B), BF16)],
        compiler_params=_cparams(("arbitrary", "arbitrary")),
        name="diff_attn_sample",
    )(lv, qb, kb, vb, kc, vc, cos, sin_signed, cos, sin_signed, gain, *cast_ws)


def _rope_tables():
    rows = DEC_SEQ // GRID_W
    row = jnp.repeat(jnp.arange(rows, dtype=F32), GRID_W)
    col = jnp.tile(jnp.arange(GRID_W, dtype=F32), rows)
    freqs = ROPE_THETA ** (-jnp.arange(AXIS_PAIRS, dtype=F32) / AXIS_PAIRS)
    ar, ac = row[:, None] * freqs, col[:, None] * freqs
    cos64 = jnp.concatenate([jnp.cos(ar), jnp.cos(ar), jnp.cos(ac), jnp.cos(ac)], axis=-1)
    sin64 = jnp.concatenate([-jnp.sin(ar), jnp.sin(ar), -jnp.sin(ac), jnp.sin(ac)], axis=-1)
    return jnp.tile(cos64, (1, 2)), jnp.tile(sin64, (1, 2))


def _group_mean(x, ones64):
    hi, lo = _split_bf16(x)
    return (_dot(hi, ones64) + _dot(lo, ones64)) * (1.0 / DG_C)


def _sgu(uv, ws_ref, bs_ref, fillers=()):
    fillers = list(fillers)
    fill = lambda: fillers.pop(0)() if fillers else None
    r = lax.broadcasted_iota(jnp.int32, (W_C, W_C), 0)
    s = lax.broadcasted_iota(jnp.int32, (W_C, W_C), 1)
    ones64 = jnp.where((r // DG_C) == (s // DG_C), 1.0, 0.0).astype(BF16)
    lane = lax.broadcasted_iota(jnp.int32, (SGU_CHUNK, W_C), 1)
    outs = []
    for n in range(uv.shape[0] // SGU_CHUNK):
        rows = slice(n * SGU_CHUNK, (n + 1) * SGU_CHUNK)
        u = _gelu_tanh(uv[rows, 0:256])
        v = _gelu_tanh(uv[rows, 256:512])
        mu = _group_mean(v, ones64)
        fill()
        d = v - mu
        var = _group_mean(d * d, ones64)
        fill()
        vn = d * lax.rsqrt(var + EPS)
        s_mix = bs_ref[...]
        for g in range(N_GROUPS_C):
            vn_g = jnp.where((lane // DG_C) == g, vn, 0.0).astype(BF16)
            s_mix = s_mix + _dot(ws_ref[g], vn_g)
        fill()
        outs.append((u * s_mix).astype(BF16))
    while fillers:
        fill()
    return jnp.concatenate(outs, axis=0)


OUT_TM = 1024
OUT_SLABS = 4


SEL_LANE0 = N_EXPERTS


def _top2_gates(logits):
    lane = lax.broadcasted_iota(jnp.int32, logits.shape, 1).astype(F32)
    neg = -jnp.inf
    lg = jnp.where(lane < N_EXPERTS, logits, neg)
    m1 = jnp.max(lg, axis=-1, keepdims=True)
    i1 = jnp.min(jnp.where(lg == m1, lane, float(LANES)), axis=-1, keepdims=True)
    lg2 = jnp.where(lane == i1, neg, lg)
    m2 = jnp.max(lg2, axis=-1, keepdims=True)
    i2 = jnp.min(jnp.where(lg2 == m2, lane, float(LANES)), axis=-1, keepdims=True)
    e2 = jnp.exp(m2 - m1)
    den = 1.0 + e2
    gates = jnp.where(lane == i1, 1.0 / den, 0.0) + jnp.where(lane == i2, e2 / den, 0.0)
    sel = jnp.where((lane == i1 + SEL_LANE0) | (lane == i2 + SEL_LANE0), 1.0, 0.0)
    return gates + sel


def _store_token_tiles(ref, val, r0=0):
    n = val.shape[0]
    for k in range(D_MODEL // LANES):
        ref[pl.ds(8 * r0 + k, n, stride=8), :] = val[:, k * LANES:(k + 1) * LANES]


def _load_token_tiles(ref, n):
    return jnp.concatenate([ref[pl.ds(k, n, stride=8), :] for k in range(D_MODEL // LANES)], axis=-1)


def _out_proj_kernel(*refs, n_x, moe):
    oa_refs, ob_refs, (oc_ref,), x_refs = refs[0:2], refs[2:4], refs[4:5], refs[5:5 + n_x]
    w_ref, g1_ref, nrm_ref, sc_ref, sh_ref, *rest = refs[5 + n_x:]
    if moe:
        rw_ref, x1_ref, h2t_ref, gates_ref, gates_t_ref, wo_ref = rest
    else:
        x1_ref, h2_ref, wo_ref = rest

    @pl.when(pl.program_id(0) == 0)
    def _():
        wo_ref[...] = w_ref[...].astype(BF16)

    if moe:
        rw_hi, rw_lo = _split_bf16(rw_ref[...])

    def rows_chain(r0, n):
        rs = slice(r0, r0 + n)
        y = (_dot(_pick_piece(oa_refs, OUT_TM, rs), wo_ref[0:256, :])
             + _dot(_pick_piece(ob_refs, OUT_TM, rs), wo_ref[256:768, :]) + _dot(oc_ref[rs, :], wo_ref[768:1024, :]))
        yield
        x1 = _pick_piece(x_refs, OUT_TM, rs) + g1_ref[...] * y
        x1_ref[rs, :] = x1
        yn = x1 * lax.rsqrt(jnp.mean(x1 * x1, axis=-1, keepdims=True) + EPS)
        h = (yn * nrm_ref[...]) * (1.0 + sc_ref[...]) + sh_ref[...]
        if moe:
            _store_token_tiles(h2t_ref, h, r0)
            h_hi, h_lo = _split_bf16(h)
            logits = _dot(h_hi, rw_hi) + (_dot(h_lo, rw_hi) + _dot(h_hi, rw_lo))
            yield
            gates = _top2_gates(logits)
            gates_ref[rs, :] = gates
            gates_t_ref[:, rs] = gates.T
        else:
            h2_ref[rs, :] = h.astype(BF16)

    n_slab = OUT_TM // OUT_SLABS
    _interleave(*[rows_chain(i * n_slab, n_slab) for i in range(OUT_SLABS)])


def _out_proj_call(oa_pieces, ob_pieces, oc, x_pieces, w_out, l, mod4, norm_g, router_w=None):
    tm = OUT_TM
    n = T_ALL // tm
    moe = router_w is not None
    row = functools.partial(_mod_row, tm=tm)
    mod_spec = lambda k: pl.BlockSpec((None, None, 1, D_MODEL), lambda i: (row(i), k, 0, 0))
    tok = lambda w: pl.BlockSpec((tm, w), lambda i: (i, 0))
    full = lambda a: pl.BlockSpec(a.shape, lambda i: (0,) * a.ndim)
    in_specs = (_piece_specs(oa_pieces, tm, W_A) + _piece_specs(ob_pieces, tm, W_B) + [tok(W_C)]
                + _piece_specs(x_pieces, tm, D_MODEL)
                + [_resident_layer(w_out, l), mod_spec(2), full(norm_g), mod_spec(4), mod_spec(3)])
    args = [*oa_pieces, *ob_pieces, oc, *x_pieces, w_out, mod4, norm_g, mod4, mod4]
    if moe:
        in_specs.append(full(router_w))
        args.append(router_w)
        out_specs = [tok(D_MODEL), pl.BlockSpec((tm * 8, LANES), lambda i: (i, 0)), tok(LANES),
                     pl.BlockSpec((LANES, tm), lambda i: (0, i))]
        out_shape = [jax.ShapeDtypeStruct((T_ALL, D_MODEL), F32), jax.ShapeDtypeStruct((T_ALL * 8, LANES), F32),
                     jax.ShapeDtypeStruct((T_ALL, LANES), F32), jax.ShapeDtypeStruct((LANES, T_ALL), F32)]
    else:
        out_specs = [tok(D_MODEL), tok(D_MODEL)]
        out_shape = [jax.ShapeDtypeStruct((T_ALL, D_MODEL), F32), jax.ShapeDtypeStruct((T_ALL, D_MODEL), BF16)]
    return pl.pallas_call(
        functools.partial(_out_proj_kernel, n_x=len(x_pieces), moe=moe),
        grid=(n,),
        in_specs=in_specs,
        out_specs=out_specs,
        out_shape=out_shape,
        scratch_shapes=[pltpu.VMEM((D_MODEL, D_MODEL), BF16)],
        compiler_params=_cparams(("arbitrary",)),
        name="out_proj_moe" if moe else "out_proj",
    )(*args)


FFN_TM = 512
MXU_N = 256
FFN_SPLITS = (0, 512, 1024, 1536, 2048, 2560, D_FF)
assert all(s % MXU_N == 0 for s in FFN_SPLITS)


def _swiglu(h, wg_ref, wu_ref, wd_ref):
    out = None
    for c0, c1 in zip(FFN_SPLITS[:-1], FFN_SPLITS[1:]):
        act = _silu(_dot(h, wg_ref[:, c0:c1])) * _dot(h, wu_ref[:, c0:c1])
        d = _dot(act.astype(BF16), wd_ref[c0:c1, :])
        out = d if out is None else out + d
    return out


def _ffn_kernel(h_ref, x_ref, g2_ref, wg_ref, wu_ref, wd_ref, w_ref, o_ref, wb_ref):
    @pl.when(pl.program_id(0) < RIDER_BLOCKS_20)
    def _():
        wb_ref[...] = w_ref[...].astype(BF16)

    o_ref[...] = x_ref[...] + g2_ref[...] * _swiglu(h_ref[...], wg_ref, wu_ref, wd_ref)


def _ffn_call(h2, x1, mod4, wg, wu, wd, cast_w):
    tm = FFN_TM
    row = functools.partial(_mod_row, tm=tm)
    tok = lambda w: pl.BlockSpec((tm, w), lambda i: (i, 0))
    resident = lambda a: pl.BlockSpec(a.shape, lambda i: (0, 0), pipeline_mode=pl.Buffered(1))
    rider_spec, rider_shape = _cast_rider(cast_w, RIDER_BLOCKS_20)
    return pl.pallas_call(
        _ffn_kernel,
        grid=(T_ALL // tm,),
        in_specs=[tok(D_MODEL), tok(D_MODEL),
                  pl.BlockSpec((None, None, 1, D_MODEL), lambda i: (row(i), 5, 0, 0)),
                  resident(wg), resident(wu), resident(wd), rider_spec],
        out_specs=[tok(D_MODEL), rider_spec],
        out_shape=[jax.ShapeDtypeStruct((T_ALL, D_MODEL), F32), rider_shape],
        compiler_params=_cparams(("arbitrary",)),
        name="ffn_dense",
    )(h2, x1, mod4, wg, wu, wd, cast_w)


MOE_TM = 512
MOE_NT_MAX = (2 * T_ALL) // MOE_TM + N_EXPERTS
MOE_ROWS = MOE_NT_MAX * MOE_TM
PLAN_BLK = 512
MISC_LAST_START = 8
MISC_NT = 16


def _moe_plan_kernel(gt_ref, posa_ref, posb_ref, te_ref, ti_ref, tv_ref, misc_ref):
    tm = float(MOE_TM)
    sel = gt_ref[SEL_LANE0:SEL_LANE0 + N_EXPERTS, :]
    cnt = jnp.sum(sel, axis=1, keepdims=True)
    nt = jnp.floor((cnt + (tm - 1.0)) * (1.0 / tm))
    sub = lax.broadcasted_iota(jnp.int32, (N_EXPERTS, LANES), 0).astype(F32)
    lane = lax.broadcasted_iota(jnp.int32, (N_EXPERTS, LANES), 1).astype(F32)
    nt_b = jnp.broadcast_to(nt, (N_EXPERTS, LANES))
    nt_row = jnp.sum(jnp.where(sub == lane, nt_b, 0.0), axis=0, keepdims=True)
    toff = jnp.sum(jnp.where(lane < sub, jnp.broadcast_to(nt_row, (N_EXPERTS, LANES)), 0.0),
                   axis=1, keepdims=True)
    tend = toff + nt
    n_total = jnp.sum(nt, axis=0, keepdims=True)
    jc = jnp.minimum(lane, n_total - 1.0)
    te = jnp.sum(jnp.where(jc >= tend, 1.0, 0.0), axis=0, keepdims=True)
    te_ref[...] = te.astype(jnp.int32)
    ti_ref[...] = jc[0:1, :].astype(jnp.int32)
    rows_left = cnt - (jc - toff) * tm
    tv = jnp.sum(jnp.where((jc >= toff) & (jc < tend), jnp.minimum(rows_left, tm), 0.0), axis=0, keepdims=True)
    tv_ref[...] = tv.astype(jnp.int32)
    last_start = (tend - 1.0) * tm
    ls_row = jnp.sum(jnp.where(sub + MISC_LAST_START == lane, jnp.broadcast_to(last_start, (N_EXPERTS, LANES)), 0.0),
                     axis=0, keepdims=True)
    nt_row2 = jnp.sum(jnp.where(sub + MISC_NT == lane, nt_b, 0.0), axis=0, keepdims=True)
    misc = jnp.where(lane[0:1, :] == 0.0, n_total, 0.0) + ls_row + nt_row2
    misc_ref[...] = misc.astype(jnp.int32)

    off = toff * tm
    r = lax.broadcasted_iota(jnp.int32, (PLAN_BLK, PLAN_BLK), 0)
    c = lax.broadcasted_iota(jnp.int32, (PLAN_BLK, PLAN_BLK), 1)
    upper = jnp.where(r <= c, 1.0, 0.0).astype(BF16)
    carry = jnp.zeros((N_EXPERTS, 1), F32)
    for blk in range(T_ALL // PLAN_BLK):
        cols = slice(blk * PLAN_BLK, (blk + 1) * PLAN_BLK)
        s = gt_ref[SEL_LANE0:SEL_LANE0 + N_EXPERTS, cols]
        rank = _dot(s.astype(BF16), upper) + carry
        pos = off + rank - 1.0
        posa_ref[:, cols] = jnp.min(jnp.where(s > 0.0, pos, 1e9), axis=0, keepdims=True).astype(jnp.int32)
        posb_ref[:, cols] = jnp.max(jnp.where(s > 0.0, pos, -1.0), axis=0, keepdims=True).astype(jnp.int32)
        carry = carry + jnp.sum(s, axis=1, keepdims=True)


def _moe_plan_call(gates_t):
    row = lambda w: jax.ShapeDtypeStruct((1, w), jnp.int32)
    full = lambda w: pl.BlockSpec((1, w), lambda: (0, 0))
    return pl.pallas_call(
        _moe_plan_kernel,
        in_specs=[pl.BlockSpec((LANES, T_ALL), lambda: (0, 0))],
        out_specs=[full(T_ALL), full(T_ALL), full(LANES), full(LANES), full(LANES), full(LANES)],
        out_shape=[row(T_ALL), row(T_ALL), row(LANES), row(LANES), row(LANES), row(LANES)],
        compiler_params=pltpu.CompilerParams(vmem_limit_bytes=VMEM_LIMIT),
        name="moe_plan",
    )(gates_t)


DMA_UNROLL = 8


def _row_tile(ref, row):
    return ref.at[pl.ds(pl.multiple_of(row * 8, 8), 8), :]


def _moe_scatter_kernel(misc_ref, posa_ref, posb_ref, h_ref, xs_ref, zero_ref, sem):
    tm = h_ref.shape[0] // 8

    @pl.when(pl.program_id(0) == 0)
    def _():
        zero_ref[...] = jnp.zeros_like(zero_ref)

        def zero_tile(first_row):
            start = pl.multiple_of(first_row * 8, 8)
            cp = pltpu.make_async_copy(zero_ref, xs_ref.at[pl.ds(start, MOE_TM * 8), :], sem.at[0])
            cp.start()
            cp.wait()

        for e in range(N_EXPERTS):
            @pl.when(misc_ref[0, MISC_NT + e] > 0)
            def _():
                zero_tile(misc_ref[0, MISC_LAST_START + e])

        def zero_tail(j, carry):
            zero_tile(j * MOE_TM)
            return carry

        lax.fori_loop(misc_ref[0, 0], MOE_NT_MAX, zero_tail, 0)

    def issue(r, carry):
        src = _row_tile(h_ref, r)
        pltpu.make_async_copy(src, _row_tile(xs_ref, posa_ref[0, r]), sem.at[0]).start(priority=0)
        pltpu.make_async_copy(src, _row_tile(xs_ref, posb_ref[0, r]), sem.at[1]).start(priority=1)
        return carry

    lax.fori_loop(0, tm, issue, 0, unroll=DMA_UNROLL)
    for k in range(2):
        pltpu.make_async_copy(h_ref, xs_ref.at[pl.ds(0, tm * 8), :], sem.at[k]).wait()


SCATTER_TM = 1024


def _moe_scatter_call(misc, posa3, posb3, h2t):
    tm = SCATTER_TM
    smem_row = pl.BlockSpec((None, 1, tm), lambda i: (i, 0, 0), memory_space=pltpu.SMEM)
    return pl.pallas_call(
        _moe_scatter_kernel,
        grid=(T_ALL // tm,),
        in_specs=[pl.BlockSpec((1, LANES), lambda i: (0, 0), memory_space=pltpu.SMEM), smem_row, smem_row,
                  pl.BlockSpec((tm * 8, LANES), lambda i: (i, 0))],
        out_specs=pl.BlockSpec(memory_space=pl.ANY),
        out_shape=jax.ShapeDtypeStruct((MOE_ROWS * 8, LANES), F32),
        scratch_shapes=[pltpu.VMEM((MOE_TM * 8, LANES), F32), pltpu.SemaphoreType.DMA((2,))],
        compiler_params=_cparams(("arbitrary",)),
        name="moe_scatter",
    )(misc, posa3, posb3, h2t)


MOE_HALF = MOE_TM // 2


def _ffn_grouped_kernel(te_ref, ti_ref, tv_ref, misc_ref, x_ref, wg_ref, wu_ref, wd_ref, o_ref):
    j = pl.program_id(0)
    live = j < misc_ref[0]
    rows = tv_ref[j]

    @pl.when(live & (rows > MOE_HALF))
    def _():
        h = _load_token_tiles(x_ref, MOE_TM).astype(BF16)
        _store_token_tiles(o_ref, _swiglu(h, wg_ref, wu_ref, wd_ref))

    @pl.when(live & (rows <= MOE_HALF))
    def _():
        h = _load_token_tiles(x_ref, MOE_HALF).astype(BF16)
        _store_token_tiles(o_ref, _swiglu(h, wg_ref, wu_ref, wd_ref))
        o_ref[MOE_HALF * 8:, :] = jnp.zeros((MOE_HALF * 8, LANES), F32)

    @pl.when(jnp.logical_not(live))
    def _():
        o_ref[...] = jnp.zeros_like(o_ref)


def _ffn_grouped_call(te, ti, tv, misc, xs, wg, wu, wd):
    expert = lambda *s: pl.BlockSpec((None,) + s, lambda j, te, ti, tv, misc: (te[j], 0, 0))
    grid_spec = pltpu.PrefetchScalarGridSpec(
        num_scalar_prefetch=4,
        grid=(MOE_NT_MAX,),
        in_specs=[pl.BlockSpec((MOE_TM * 8, LANES), lambda j, te, ti, tv, misc: (ti[j], 0)),
                  expert(D_MODEL, D_FF), expert(D_MODEL, D_FF), expert(D_FF, D_MODEL)],
        out_specs=pl.BlockSpec((MOE_TM * 8, LANES), lambda j, te, ti, tv, misc: (j, 0)),
    )
    return pl.pallas_call(
        _ffn_grouped_kernel,
        grid_spec=grid_spec,
        out_shape=jax.ShapeDtypeStruct((MOE_ROWS * 8, LANES), F32),
        compiler_params=_cparams(("arbitrary",)),
        name="ffn_grouped",
    )(te, ti, tv, misc, xs, wg, wu, wd)


COMBINE_TM = 512


def _moe_combine_kernel(posa_ref, posb_ref, posa_next_ref, posb_next_ref, ys_ref, x_ref, g2_ref, gates_ref, nf_ref,
                        op_ref, os_ref, bufa_ref, bufb_ref, sem):
    tm = COMBINE_TM
    i = pl.program_id(0)
    n = pl.num_programs(0)
    slot = i % 2

    def gather(pa_ref, pb_ref, s):
        def issue(r, carry):
            pltpu.make_async_copy(_row_tile(ys_ref, pa_ref[0, r]), _row_tile(bufa_ref.at[s], r),
                                  sem.at[s, 0]).start(priority=0)
            pltpu.make_async_copy(_row_tile(ys_ref, pb_ref[0, r]), _row_tile(bufb_ref.at[s], r),
                                  sem.at[s, 1]).start(priority=1)
            return carry

        lax.fori_loop(0, tm, issue, 0, unroll=DMA_UNROLL)

    @pl.when(i == 0)
    def _():
        gather(posa_ref, posb_ref, 0)

    @pl.when(i + 1 < n)
    def _():
        gather(posa_next_ref, posb_next_ref, 1 - slot)

    gates = gates_ref[...]
    lane = lax.broadcasted_iota(jnp.int32, gates.shape, 1).astype(F32)
    is_sel = (lane >= SEL_LANE0) & (lane < SEL_LANE0 + N_EXPERTS) & (gates > 0.0)
    ia = jnp.min(jnp.where(is_sel, lane, float(LANES)), axis=-1, keepdims=True) - SEL_LANE0
    ib = jnp.max(jnp.where(is_sel, lane, -1.0), axis=-1, keepdims=True) - SEL_LANE0
    wa = jnp.sum(jnp.where(lane == ia, gates, 0.0), axis=-1, keepdims=True)
    wb = jnp.sum(jnp.where(lane == ib, gates, 0.0), axis=-1, keepdims=True)

    pltpu.make_async_copy(ys_ref.at[pl.ds(0, tm * 8), :], bufa_ref.at[slot], sem.at[slot, 0]).wait()
    pltpu.make_async_copy(ys_ref.at[pl.ds(0, tm * 8), :], bufb_ref.at[slot], sem.at[slot, 1]).wait()

    y = wa * _load_token_tiles(bufa_ref.at[slot], tm) + wb * _load_token_tiles(bufb_ref.at[slot], tm)
    out = x_ref[...] + g2_ref[...] * y
    out = (out * lax.rsqrt(jnp.mean(out * out, axis=-1, keepdims=True) + EPS)) * nf_ref[...]

    @pl.when(i < T_PROMPT // tm)
    def _():
        op_ref[...] = out

    @pl.when(i >= T_PROMPT // tm)
    def _():
        os_ref[...] = out


def _moe_combine_call(posa3, posb3, ys, x1, mod4, gates, norm_f):
    tm = COMBINE_TM
    n = T_ALL // tm
    n_p = T_PROMPT // tm
    row = functools.partial(_mod_row, tm=tm)
    smem_row = pl.BlockSpec((None, 1, tm), lambda i: (i, 0, 0), memory_space=pltpu.SMEM)
    smem_next = pl.BlockSpec((None, 1, tm), lambda i: (jnp.minimum(i + 1, n - 1), 0, 0), memory_space=pltpu.SMEM)
    tok = lambda w: pl.BlockSpec((tm, w), lambda i: (i, 0))
    return pl.pallas_call(
        _moe_combine_kernel,
        grid=(n,),
        in_specs=[smem_row, smem_row, smem_next, smem_next, pl.BlockSpec(memory_space=pl.ANY), tok(D_MODEL),
                  pl.BlockSpec((None, None, 1, D_MODEL), lambda i: (row(i), 5, 0, 0)), tok(LANES),
                  pl.BlockSpec((1, D_MODEL), lambda i: (0, 0))],
        out_specs=[pl.BlockSpec((tm, D_MODEL), lambda i: (jnp.minimum(i, n_p - 1), 0)),
                   pl.BlockSpec((tm, D_MODEL), lambda i: (jnp.maximum(i - n_p, 0), 0))],
        out_shape=[jax.ShapeDtypeStruct((T_PROMPT, D_MODEL), F32), jax.ShapeDtypeStruct((T_SAMPLE, D_MODEL), F32)],
        scratch_shapes=[pltpu.VMEM((2, tm * 8, LANES), F32), pltpu.VMEM((2, tm * 8, LANES), F32),
                        pltpu.SemaphoreType.DMA((2, 2))],
        compiler_params=_cparams(("arbitrary",)),
        name="moe_combine",
    )(posa3, posb3, posa3, posb3, ys, x1, mod4, gates, norm_f)


def _moe_call(h2t, x1, mod4, gates, gates_t, wg, wu, wd, norm_f):
    posa, posb, te, ti, tv, misc = _moe_plan_call(gates_t)
    xs = _moe_scatter_call(misc, posa.reshape(T_ALL // SCATTER_TM, 1, SCATTER_TM),
                           posb.reshape(T_ALL // SCATTER_TM, 1, SCATTER_TM), h2t)
    ys = _ffn_grouped_call(te.reshape(LANES), ti.reshape(LANES), tv.reshape(LANES), misc.reshape(LANES), xs, wg, wu, wd)
    return _moe_combine_call(posa.reshape(T_ALL // COMBINE_TM, 1, COMBINE_TM),
                             posb.reshape(T_ALL // COMBINE_TM, 1, COMBINE_TM), ys, x1, mod4, gates, norm_f)


def kernel(x_prompt, x_sample, cache_k, cache_v, state_gla_fwd, state_gla_bwd, c, c_ctx, w_ada, b_ada, norm_mix, norm_ffn, w_in, w_out, gla_w_up, gla_b_up, gla_norm, diff_lambda, diff_norm, sgu_w, sgu_b, ffn_w_gate, ffn_w_up, ffn_w_down, router_w, moe_w_gate, moe_w_up, moe_w_down, norm_f):
    assert DEPTH == 2
    x_pieces = [x_prompt.reshape(T_PROMPT, D_MODEL), x_sample.reshape(T_SAMPLE, D_MODEL)]
    cvecs = jnp.concatenate([c_ctx[None, :], c, jnp.zeros((N_MOD_ROWS - 1 - DEC_BATCH, D_MODEL), F32)], axis=0)
    mod = _ada_call(cvecs, w_ada, b_ada)
    cos, sin_signed = _rope_tables()
    zeros_state = jnp.zeros((BATCH, 64, 256), F32)

    w_in_t = jnp.swapaxes(w_in, 1, 2)
    moe_w = [moe_w_gate[0], moe_w_up[0], moe_w_down[0]]
    for l in range(DEPTH):
        mod4 = mod[l].reshape(N_MOD_ROWS, 6, 1, D_MODEL)
        w_up = jnp.zeros((LANES, 2 * W_A), F32)
        w_up = w_up.at[0:GLA_RANK, 0:W_A].set(gla_w_up[l, 0]).at[GLA_RANK:2 * GLA_RANK, W_A:].set(gla_w_up[l, 1])
        b_up = gla_b_up[l].reshape(1, 2 * W_A)
        bs_full = jnp.repeat(sgu_b[l].T, DG_C, axis=1)
        g4, la, qb, kb, vb, oc, moe_w[l] = _in_proj_call(x_pieces, norm_mix[l][None, :], mod4, w_in_t, l,
                                                         w_up.astype(BF16), b_up, sgu_w[l].astype(BF16), bs_full,
                                                         moe_w[l])

        gain_a = gla_norm[l][None, :]
        if l == 0:
            oa_p, sf0, sb0 = _gla_call(g4, la, zeros_state, zeros_state, gain_a, batch=BATCH, seq=SEQ, row_block0=0)
            kb0, vb0 = kb, vb
        else:
            oa_p, new_sf, new_sb, new_cache_k = _gla_call(g4, la, zeros_state, zeros_state, gain_a, (sf0, sb0, kb0, kb),
                                                          batch=BATCH, seq=SEQ, row_block0=0)
        oa_s, _, _ = _gla_call(g4, la, _state_to_kernel(state_gla_fwd[:, l]),
                               _state_to_kernel(state_gla_bwd[:, l]), gain_a,
                               batch=DEC_BATCH, seq=DEC_SEQ, row_block0=T_PROMPT // DEC_SEQ)

        lam_init = 0.8 - 0.6 * math.exp(-0.3 * l)
        gain_b = diff_norm[l][None, :]
        if l == 0:
            (ob_p,) = _attn_prompt_call(diff_lambda[l], qb, kb, vb, gain_b, lam_init=lam_init)
        else:
            ob_p, new_cache_v = _attn_prompt_call(diff_lambda[l], qb, kb, vb, gain_b, (vb0,),
                                                  lam_init=lam_init, write_cache=True)
        ob_s, *dense_w = _attn_sample_call(diff_lambda[l], qb, kb, vb,
                                           cache_k[:, l].reshape(DEC_BATCH, PAST_LEN, W_B),
                                           cache_v[:, l].reshape(DEC_BATCH, PAST_LEN, W_B),
                                           cos, sin_signed, gain_b,
                                           (ffn_w_gate[0], ffn_w_up[0], ffn_w_down[0]) if l == 0 else (),
                                           lam_init=lam_init)

        if l == 0:
            x1, h2 = _out_proj_call([oa_p, oa_s], [ob_p, ob_s], oc, x_pieces, w_out, l, mod4, norm_ffn[l][None, :])
            x_next, moe_w[2] = _ffn_call(h2, x1, mod4, *dense_w, moe_w[2])
            x_pieces = [x_next]
        else:
            rw = jnp.pad(router_w[0], ((0, 0), (0, LANES - N_EXPERTS)))
            x1, h2t, gates, gates_t = _out_proj_call([oa_p, oa_s], [ob_p, ob_s], oc, x_pieces, w_out, l, mod4,
                                                     norm_ffn[l][None, :], rw)
            y_prompt, y_sample = _moe_call(h2t, x1, mod4, gates, gates_t, *moe_w, norm_f[None, :])

    state_shape = (BATCH, DEPTH, N_HEADS_A, DK_A, DK_A)
    return (y_prompt.reshape(BATCH, SEQ, D_MODEL), y_sample.reshape(DEC_BATCH, DEC_SEQ, D_MODEL),
            new_cache_k.reshape(BATCH, DEPTH, SEQ, N_HEADS_B, DV_B), new_cache_v.reshape(BATCH, DEPTH, SEQ, N_HEADS_B, DV_B),
            new_sf.reshape(state_shape), new_sb.reshape(state_shape))
```

```python
import functools
import math

import jax
import jax.numpy as jnp
import numpy as np
from jax import lax
from jax.experimental import pallas as pl
from jax.experimental.pallas import tpu as pltpu

F32 = jnp.float32
BF16 = jnp.bfloat16

D_MODEL = 1024
BATCH = 32
SEQ = 256
DEPTH = 2
DEC_BATCH = 2
DEC_SEQ = 1024
PAST_LEN = 256
GRID_W = 64
N_HEADS_A = 4
DK_A = 64
W_A = 256
GLA_RANK = 16
GLA_TAU = 16.0
GLA_CHUNK = 64
N_HEADS_B = 4
DH_B = 64
DV_B = 128
W_B = 512
ROPE_THETA = 10000.0
AXIS_PAIRS = DH_B // 4
N_GROUPS_C = 4
DG_C = 64
W_C = 256
SGU_CHUNK = 128
D_FF = 2816
N_EXPERTS = 8
EPS = 1e-6

T_PROMPT = BATCH * SEQ
T_SAMPLE = DEC_BATCH * DEC_SEQ
T_ALL = T_PROMPT + T_SAMPLE
N_MOD_ROWS = 8
LANES = 128
VMEM_LIMIT = 56 * 1024 * 1024


def _cparams(sem):
    return pltpu.CompilerParams(dimension_semantics=sem, vmem_limit_bytes=VMEM_LIMIT)


def _dot(a, b):
    return jnp.dot(a, b, preferred_element_type=F32)


def _dot_nt(a, b):
    return lax.dot_general(a, b, (((1,), (1,)), ((), ())), preferred_element_type=F32)


def _dot_tn(a, b):
    return lax.dot_general(a, b, (((0,), (0,)), ((), ())), preferred_element_type=F32)


def _split_bf16(x):
    hi = x.astype(BF16)
    lo = (x - hi.astype(F32)).astype(BF16)
    return hi, lo


def _dot3(a, w):
    a_hi, a_lo = _split_bf16(a)
    w_hi, w_lo = _split_bf16(w)
    return _dot(a_hi, w_hi) + (_dot(a_lo, w_hi) + _dot(a_hi, w_lo))


def _sigmoid(x):
    return 1.0 / (1.0 + jnp.exp(-x))


def _silu(x):
    return x * _sigmoid(x)


def _gelu_tanh(x):
    c = math.sqrt(2.0 / math.pi)
    return x * (0.5 * (1.0 + jnp.tanh(c * (x + 0.044715 * (x * x * x)))))


def _log_sigmoid(x):
    return jnp.minimum(x, 0.0) - jnp.log(1.0 + jnp.exp(-jnp.abs(x)))


def _mod_row(i, tm):
    n_p = T_PROMPT // tm
    per_b = DEC_SEQ // tm
    return jnp.where(i < n_p, 0, 1 + (i - n_p) // per_b)


ADA_TN = 1536


def _ada_kernel(c_ref, w_ref, b_ref, o_ref):
    a = _silu(c_ref[...])
    o_ref[...] = _dot3(a, w_ref[...]) + b_ref[...]


def _ada_call(cvecs, w_ada, b_ada):
    n_col = (6 * D_MODEL) // ADA_TN
    return pl.pallas_call(
        _ada_kernel,
        grid=(DEPTH, n_col),
        in_specs=[
            pl.BlockSpec((N_MOD_ROWS, D_MODEL), lambda l, j: (0, 0)),
            pl.BlockSpec((None, D_MODEL, ADA_TN), lambda l, j: (l, 0, j)),
            pl.BlockSpec((None, 1, ADA_TN), lambda l, j: (l, 0, j)),
        ],
        out_specs=pl.BlockSpec((None, N_MOD_ROWS, ADA_TN), lambda l, j: (l, 0, j)),
        out_shape=jax.ShapeDtypeStruct((DEPTH, N_MOD_ROWS, 6 * D_MODEL), F32),
        compiler_params=_cparams(("arbitrary", "arbitrary")),
        name="ada_mod",
    )(cvecs, w_ada, b_ada.reshape(DEPTH, 1, 6 * D_MODEL))


IN_TM = 512
IN_COLS = 3104
Z_COL0 = 1024
Z_COLS = 2 * GLA_RANK
W_MAIN = 3072


def _piece_specs(pieces, tm, width):
    specs, t0 = [], 0
    for arr in pieces:
        nt = arr.shape[0] // tm
        specs.append(pl.BlockSpec((tm, width), lambda i, t0=t0, nt=nt: (jnp.clip(i - t0, 0, nt - 1), 0)))
        t0 += nt
    assert t0 * tm == T_ALL and len(pieces) in (1, 2) and (len(pieces) == 1 or pieces[0].shape[0] == T_PROMPT)
    return specs


def _pick_piece(refs, tm, rows=slice(None)):
    if len(refs) == 1:
        return refs[0][rows, :]
    return jnp.where(pl.program_id(0) < T_PROMPT // tm, refs[0][rows, :], refs[1][rows, :])


RIDER_BLOCKS_20 = 16


W_STAGE_ROWS = 512


def _in_proj_kernel(*refs, n_x, layer):
    x_refs = refs[:n_x]
    (nrm_ref, sh_ref, sc_ref, w_hbm, wup_ref, bup_ref, ws_ref, bs_ref, cast_ref,
     g4_ref, la_ref, qb_ref, kb_ref, vb_ref, oc_ref, cast_out_ref, wm_ref, wz_ref, stage_ref, sem) = refs[n_x:]

    @pl.when(pl.program_id(0) < RIDER_BLOCKS_20)
    def _():
        cast_out_ref[...] = cast_ref[...].astype(BF16)

    @pl.when(pl.program_id(0) == 0)
    def _():
        n = W_STAGE_ROWS
        moves = [(src, src if src < Z_COL0 else src - Z_COLS)
                 for src in (*range(0, Z_COL0, n), *range(Z_COL0 + Z_COLS, IN_COLS, n))]
        copy = lambda k: pltpu.make_async_copy(w_hbm.at[layer, pl.ds(moves[k][0], n), :], stage_ref.at[k % 2],
                                               sem.at[k % 2])
        copy(0).start()
        for k, (_, dst) in enumerate(moves):
            if k + 1 < len(moves):
                copy(k + 1).start()
            copy(k).wait()
            wm_ref[dst:dst + n, :] = stage_ref[k % 2].astype(BF16)
        gate_rows = pltpu.make_async_copy(w_hbm.at[layer, pl.ds(Z_COL0, Z_COLS), :],
                                          stage_ref.at[0, pl.ds(0, Z_COLS), :], sem.at[0])
        gate_rows.start()
        gate_rows.wait()
        wz_ref[...] = jnp.zeros_like(wz_ref)
        wz_ref[0:Z_COLS, :] = stage_ref[0, 0:Z_COLS, :].astype(BF16)

    x = _pick_piece(x_refs, IN_TM)
    y = x * lax.rsqrt(jnp.mean(x * x, axis=-1, keepdims=True) + EPS)
    h = (y * nrm_ref[...]) * (1.0 + sc_ref[...]) + sh_ref[...]
    hb = h.astype(BF16)
    def plain_job(ref, c0, w0):
        def job():
            ref[:, c0:c0 + MXU_N] = _dot_nt(hb, wm_ref[w0:w0 + MXU_N, :])
        return job

    jobs = [plain_job(ref, c, w0 + c) for ref, w0, width in
            ((g4_ref, 0, 1024), (qb_ref, 1024, 512), (kb_ref, 1536, 512), (vb_ref, 2048, 512))
            for c in range(0, width, MXU_N)]
    z = _dot_nt(hb, wz_ref[...])
    jobs.pop(0)()
    zz = _dot(z.astype(BF16), wup_ref[...]) + bup_ref[...]
    uv = _dot_nt(hb, wm_ref[2560:3072, :])
    jobs.pop(0)()
    la_ref[...] = _log_sigmoid(zz) * (1.0 / GLA_TAU)
    jobs.pop(0)()
    oc_ref[...] = _sgu(uv, ws_ref, bs_ref, jobs)


def _resident_layer(a, l):
    return pl.BlockSpec((None,) + a.shape[1:], lambda *_: (l,) + (0,) * (a.ndim - 1), pipeline_mode=pl.Buffered(1))


def _in_proj_call(x_pieces, norm_g, mod4, w_in_t, l, w_up, b_up, sgu_ws, sgu_bs, cast_w):
    tm = IN_TM
    rider_spec, rider_shape = _cast_rider(cast_w, RIDER_BLOCKS_20)
    n = T_ALL // tm
    row = functools.partial(_mod_row, tm=tm)
    mod_spec = lambda k: pl.BlockSpec((None, None, 1, D_MODEL), lambda i: (row(i), k, 0, 0))
    full = lambda a: pl.BlockSpec(a.shape, lambda i: (0,) * a.ndim)
    out = lambda w: pl.BlockSpec((tm, w), lambda i: (i, 0))
    return pl.pallas_call(
        functools.partial(_in_proj_kernel, n_x=len(x_pieces), layer=l),
        grid=(n,),
        in_specs=_piece_specs(x_pieces, tm, D_MODEL) + [full(norm_g), mod_spec(0), mod_spec(1), pl.BlockSpec(memory_space=pl.ANY),
                                                        full(w_up), full(b_up), full(sgu_ws), full(sgu_bs),
                                                        rider_spec],
        out_specs=[out(1024), out(512), out(512), out(512), out(512), out(W_C), rider_spec],
        out_shape=[jax.ShapeDtypeStruct((T_ALL, w), F32) for w in (1024, 512, 512, 512, 512)]
        + [jax.ShapeDtypeStruct((T_ALL, W_C), BF16), rider_shape],
        scratch_shapes=[pltpu.VMEM((W_MAIN, D_MODEL), BF16), pltpu.VMEM((LANES, D_MODEL), BF16),
                        pltpu.VMEM((2, W_STAGE_ROWS, D_MODEL), F32), pltpu.SemaphoreType.DMA((2,))],
        compiler_params=_cparams(("arbitrary",)),
        name="in_proj",
    )(*x_pieces, norm_g, mod4, mod4, w_in_t, w_up, b_up, sgu_ws, sgu_bs, cast_w)


GLA_SB = 256
GLA_NC = GLA_SB // GLA_CHUNK


def _head_blocks(x, same64):
    return jnp.where(same64, jnp.concatenate([x] * N_HEADS_A, axis=0), 0.0).astype(BF16)


def _gla_superblock(q, k, vb, v_blocks, la, st_all, tri, mask4, same64, forward):
    c = GLA_CHUNK
    mid, last = (c // 2 - 1, c - 1) if forward else (c // 2, 0)
    la_hi, la_lo = _split_bf16(la)
    b = _dot(tri, la_hi) + _dot(tri, la_lo)
    yield
    rows_of = lambda r: jnp.concatenate(
        [jnp.broadcast_to(b[i * c + r:i * c + r + 1, :], (c, W_A)) for i in range(GLA_NC)], axis=0)
    m = rows_of(mid)
    bl = rows_of(last)
    qe = (q * jnp.exp(b - m)).astype(BF16)
    ke = k * jnp.exp(m - b)
    qi = (q * jnp.exp(b)).astype(BF16)
    ks = (k * jnp.exp(bl - b)).astype(BF16)
    outs = [None] * GLA_NC
    for i in (range(GLA_NC) if forward else reversed(range(GLA_NC))):
        rows = slice(i * c, (i + 1) * c)
        s = _dot_nt(qe[rows, :], _head_blocks(ke[rows, :], same64))
        kv = _dot_tn(vb[rows, :], ks[rows, :])
        yield
        a = jnp.where(mask4, s, 0.0).astype(BF16)
        outs[i] = _dot(a, v_blocks[i]) + _dot_nt(qi[rows, :], st_all.astype(BF16))
        st_all = st_all * jnp.exp(bl[i * c:i * c + 1, :]) + jnp.where(same64, kv, 0.0)
        yield
    return jnp.concatenate(outs, axis=0), st_all


def _interleave(*gens):
    results = [None] * len(gens)
    active = list(enumerate(gens))
    while active:
        for item in list(active):
            try:
                next(item[1])
            except StopIteration as stop:
                results[item[0]] = stop.value
                active.remove(item)
    return results


def _cast_rider(w, n_blocks):
    e, r, c = w.shape
    per = n_blocks // e
    rows = r // per
    assert per * e == n_blocks and rows * per == r and rows % 16 == 0

    def index(b):
        b = jnp.minimum(b, n_blocks - 1)
        return b // per, b % per, 0

    spec = pl.BlockSpec((None, rows, c), index)
    return spec, jax.ShapeDtypeStruct(w.shape, BF16)


def _gla_kernel(*refs, seq, final):
    if final:
        (g4_ref, la_ref, s0f_ref, s0b_ref, gain_ref, sf_prev_ref, sb_prev_ref, k_prev_ref, k_ref,
         o_ref, sf_ref, sb_ref, ck_ref, of_ref, ob_ref) = refs
        for l, src in enumerate((k_prev_ref, k_ref)):
            for h in range(N_HEADS_B):
                ck_ref[l, pl.ds(h, SEQ, stride=N_HEADS_B), :] = src[:, h * DV_B:(h + 1) * DV_B]
    else:
        g4_ref, la_ref, s0f_ref, s0b_ref, gain_ref, o_ref, sf_ref, sb_ref, of_ref, ob_ref = refs
    n = GLA_SB
    nsb = seq // n
    r = lax.broadcasted_iota(jnp.int32, (n, n), 0)
    s = lax.broadcasted_iota(jnp.int32, (n, n), 1)
    same64 = (r // GLA_CHUNK) == (s // GLA_CHUNK)
    lower = same64 & (s <= r)
    upper = same64 & (s >= r)
    tri_f = jnp.where(lower, 1.0, 0.0).astype(BF16)
    tri_b = jnp.where(upper, 1.0, 0.0).astype(BF16)
    ones64 = jnp.where(same64, 1.0, 0.0).astype(BF16)
    key_row = lax.broadcasted_iota(jnp.int32, (GLA_CHUNK, n), 1) % GLA_CHUNK
    qry_row = lax.broadcasted_iota(jnp.int32, (GLA_CHUNK, n), 0)
    mask4_f = key_row <= qry_row
    mask4_b = key_row >= qry_row
    scale = DK_A ** -0.5
    expand = lambda st: jnp.where(same64, jnp.concatenate([st] * N_HEADS_A, axis=0), 0.0)
    compact = lambda st_all: functools.reduce(
        lambda a, b: a + b, [st_all[h * 64:(h + 1) * 64, :] for h in range(N_HEADS_A)])

    def step(i, carry):
        stf, stb = carry
        rf = pl.ds(pl.multiple_of(i * n, n), n)
        rb = pl.ds(pl.multiple_of((nsb - 1 - i) * n, n), n)
        def direction(rows, la_cols, st, tri, mask4, forward):
            q = g4_ref[rows, 0:256] * scale
            k = g4_ref[rows, 256:512]
            v = g4_ref[rows, 512:768]
            v_blocks = [_head_blocks(v[i * GLA_CHUNK:(i + 1) * GLA_CHUNK, :], same64) for i in range(GLA_NC)]
            return _gla_superblock(q, k, v.astype(BF16), v_blocks, la_ref[rows, la_cols], st, tri, mask4, same64,
                                   forward)

        (o_f, stf), (o_b, stb) = _interleave(direction(rf, slice(0, 256), stf, tri_f, mask4_f, True),
                                             direction(rb, slice(256, 512), stb, tri_b, mask4_b, False))
        of_ref[rf, :] = o_f
        ob_ref[rb, :] = o_b
        return stf, stb

    stf, stb = lax.fori_loop(0, nsb, step, (expand(s0f_ref[...]), expand(s0b_ref[...])))
    if final:
        sf_ref[0] = sf_prev_ref[...].T
        sb_ref[0] = sb_prev_ref[...].T
        sf_ref[1] = compact(stf).T
        sb_ref[1] = compact(stb).T
    else:
        sf_ref[...] = compact(stf)
        sb_ref[...] = compact(stb)

    gain = gain_ref[...]

    def finish(i, carry):
        rows = pl.ds(pl.multiple_of(i * n, n), n)
        o = of_ref[rows, :] + ob_ref[rows, :]
        sq_hi, sq_lo = _split_bf16(o * o)
        ms = (_dot(sq_hi, ones64) + _dot(sq_lo, ones64)) * (1.0 / DK_A)
        y = (o * lax.rsqrt(ms + EPS)) * gain
        o_ref[rows, :] = (y * _silu(g4_ref[rows, 768:1024])).astype(BF16)
        return carry

    lax.fori_loop(0, nsb, finish, 0)


def _gla_call(g4, la, s0f, s0b, gain, prev=None, *, batch, seq, row_block0):
    tok = lambda w: pl.BlockSpec((seq, w), lambda b: (row_block0 + b, 0))
    st = pl.BlockSpec((None, 64, 256), lambda b: (b, 0, 0))
    in_specs = [tok(1024), tok(512), st, st, pl.BlockSpec((1, W_A), lambda b: (0, 0))]
    args = [g4, la, s0f, s0b, gain]
    o_spec = pl.BlockSpec((seq, W_A), lambda b: (b, 0))
    o_shape = jax.ShapeDtypeStruct((batch * seq, W_A), BF16)
    if prev is None:
        out_specs = [o_spec, st, st]
        out_shape = [o_shape, jax.ShapeDtypeStruct((batch, 64, 256), F32), jax.ShapeDtypeStruct((batch, 64, 256), F32)]
    else:
        assert seq == SEQ and batch == BATCH and DEPTH == 2
        kblk = pl.BlockSpec((SEQ, W_B), lambda b: (b, 0))
        in_specs += [st, st, kblk, kblk]
        args += list(prev)
        st2 = pl.BlockSpec((None, DEPTH, 256, 64), lambda b: (b, 0, 0, 0))
        st2_shape = jax.ShapeDtypeStruct((batch, DEPTH, 256, 64), F32)
        out_specs = [o_spec, st2, st2, pl.BlockSpec((None, DEPTH, SEQ * N_HEADS_B, DV_B), lambda b: (b, 0, 0, 0))]
        out_shape = [o_shape, st2_shape, st2_shape, jax.ShapeDtypeStruct((batch, DEPTH, SEQ * N_HEADS_B, DV_B), F32)]
    return pl.pallas_call(
        functools.partial(_gla_kernel, seq=seq, final=prev is not None),
        grid=(batch,),
        in_specs=in_specs,
        out_specs=out_specs,
        out_shape=out_shape,
        scratch_shapes=[pltpu.VMEM((seq, W_A), F32), pltpu.VMEM((seq, W_A), F32)],
        compiler_params=_cparams(("arbitrary",)),
        name=f"gla_{seq}",
    )(*args)


def _state_to_kernel(s):
    b = s.shape[0]
    return jnp.transpose(s, (0, 3, 1, 2)).reshape(b, 64, 256)


def _lambda(lv, lam_init):
    l01 = jnp.sum(lv[0:1, :] * lv[1:2, :], axis=-1, keepdims=True)
    l23 = jnp.sum(lv[2:3, :] * lv[3:4, :], axis=-1, keepdims=True)
    return jnp.exp(l01) - jnp.exp(l23) + lam_init


def _softmax_parts(parts):
    mx = functools.reduce(jnp.maximum, [jnp.max(p, axis=-1, keepdims=True) for p in parts])
    es = [jnp.exp(p - mx) for p in parts]
    den = functools.reduce(lambda a, b: a + b, [jnp.sum(e, axis=-1, keepdims=True) for e in es])
    return [e / den for e in es]


def _diff_finish(o, gain, lam_init):
    o = o * lax.rsqrt(jnp.mean(o * o, axis=-1, keepdims=True) + EPS)
    return ((o * gain) * (1.0 - lam_init)).astype(BF16)


QK_SCALE = DH_B ** -0.5


def _key_halves(k):
    first = lax.broadcasted_iota(jnp.int32, k.shape, 1) < DH_B
    return jnp.where(first, k, 0.0).astype(BF16), jnp.where(first, 0.0, k).astype(BF16)


def _attn_prompt_kernel(lv_ref, q_ref, k_ref, v_ref, gain_ref, *rest, lam_init, n_prev):
    prev_refs, (o_ref, *cache_refs) = rest[:n_prev], rest[n_prev:]
    lam = _lambda(lv_ref[...], lam_init)

    def head(h):
        cols = slice(h * DV_B, (h + 1) * DV_B)
        q = (q_ref[:, cols] * QK_SCALE).astype(BF16)
        k1, k2 = _key_halves(k_ref[:, cols])
        s1 = _dot_nt(q, k1)
        s2 = _dot_nt(q, k2)
        yield
        (p1,) = _softmax_parts([s1])
        (p2,) = _softmax_parts([s2])
        a = p1 - lam * p2
        o = _dot(a.astype(BF16), v_ref[:, cols].astype(BF16))
        yield
        o_ref[:, cols] = _diff_finish(o, gain_ref[:, cols], lam_init)
        if cache_refs:
            (cv_ref,) = cache_refs
            for l, src in enumerate((*prev_refs, v_ref)):
                cv_ref[l, pl.ds(h, SEQ, stride=N_HEADS_B), :] = src[:, cols]

    _interleave(*[head(h) for h in range(N_HEADS_B)])


def _attn_prompt_call(lv, qb, kb, vb, gain, prev_v=(), *, lam_init, write_cache=False):
    blk = pl.BlockSpec((SEQ, W_B), lambda b: (b, 0))
    n_prev = len(prev_v)
    assert write_cache or not prev_v
    in_specs = [pl.BlockSpec((4, DH_B), lambda b: (0, 0)), blk, blk, blk,
                pl.BlockSpec((1, W_B), lambda b: (0, 0))] + [blk] * n_prev
    args = [lv, qb, kb, vb, gain, *prev_v]
    out_specs = [blk]
    out_shape = [jax.ShapeDtypeStruct((T_PROMPT, W_B), BF16)]
    if write_cache:
        out_specs.append(pl.BlockSpec((None, n_prev + 1, SEQ * N_HEADS_B, DV_B), lambda b: (b, 0, 0, 0)))
        out_shape.append(jax.ShapeDtypeStruct((BATCH, n_prev + 1, SEQ * N_HEADS_B, DV_B), F32))
    return pl.pallas_call(
        functools.partial(_attn_prompt_kernel, lam_init=lam_init, n_prev=n_prev),
        grid=(BATCH,),
        in_specs=in_specs,
        out_specs=out_specs,
        out_shape=out_shape,
        compiler_params=_cparams(("arbitrary",)),
        name="diff_attn_prompt",
    )(*args)


def _rope(x, cos, sin_signed):
    lane = lax.broadcasted_iota(jnp.int32, x.shape, 1)
    first = (lane % (2 * AXIS_PAIRS)) < AXIS_PAIRS
    partner = jnp.where(first, pltpu.roll(x, LANES - AXIS_PAIRS, 1), pltpu.roll(x, AXIS_PAIRS, 1))
    return x * cos + partner * sin_signed


ATT_TQ = 256


def _attn_sample_kernel(lv_ref, q_ref, k_ref, v_ref, kc_ref, vc_ref, cosq_ref, sinq_ref,
                        cosk_ref, sink_ref, gain_ref, *rest, lam_init, n_cast):
    w_refs, o_ref, wb_refs, (k1_ref, k2_ref) = rest[:n_cast], rest[n_cast], rest[n_cast + 1:2 * n_cast + 1], rest[-2:]
    for w_ref, wb_ref in zip(w_refs, wb_refs):
        wb_ref[...] = w_ref[...].astype(BF16)

    @pl.when(pl.program_id(1) == 0)
    def _():
        for h in range(N_HEADS_B):
            cols = slice(h * DV_B, (h + 1) * DV_B)
            k1_ref[:, cols], k2_ref[:, cols] = _key_halves(_rope(k_ref[:, cols], cosk_ref[...], sink_ref[...]))

    lam = _lambda(lv_ref[...], lam_init)

    def head(h):
        cols = slice(h * DV_B, (h + 1) * DV_B)
        q = (_rope(q_ref[:, cols], cosq_ref[...], sinq_ref[...]) * QK_SCALE).astype(BF16)
        c1, c2 = _key_halves(kc_ref[:, cols])
        s1 = [_dot_nt(q, k1_ref[:, cols]), _dot_nt(q, c1)]
        s2 = [_dot_nt(q, k2_ref[:, cols]), _dot_nt(q, c2)]
        yield
        p1 = _softmax_parts(s1)
        p2 = _softmax_parts(s2)
        a_own = p1[0] - lam * p2[0]
        a_ctx = p1[1] - lam * p2[1]
        o = (_dot(a_own.astype(BF16), v_ref[:, cols].astype(BF16))
             + _dot(a_ctx.astype(BF16), vc_ref[:, cols].astype(BF16)))
        yield
        o_ref[:, cols] = _diff_finish(o, gain_ref[:, cols], lam_init)

    _interleave(*[head(h) for h in range(N_HEADS_B)])


def _attn_sample_call(lv, qb, kb, vb, kc, vc, cos, sin_signed, gain, cast_ws=(), *, lam_init):
    tq = ATT_TQ
    nq = DEC_SEQ // tq
    p0 = T_PROMPT // tq
    s0 = T_PROMPT // DEC_SEQ
    qblk = pl.BlockSpec((tq, W_B), lambda b, t: (p0 + b * nq + t, 0))
    kvblk = pl.BlockSpec((DEC_SEQ, W_B), lambda b, t: (s0 + b, 0))
    cblk = pl.BlockSpec((None, PAST_LEN, W_B), lambda b, t: (b, 0, 0))
    n_steps = DEC_BATCH * nq
    cast_specs = []
    for w in cast_ws:
        assert w.shape[0] % (16 * n_steps) == 0
        cast_specs.append(pl.BlockSpec((w.shape[0] // n_steps, w.shape[1]), lambda b, t: (b * nq + t, 0)))
    return pl.pallas_call(
        functools.partial(_attn_sample_kernel, lam_init=lam_init, n_cast=len(cast_ws)),
        grid=(DEC_BATCH, nq),
        in_specs=[pl.BlockSpec((4, DH_B), lambda b, t: (0, 0)), qblk, kvblk, kvblk, cblk, cblk,
                  pl.BlockSpec((tq, DV_B), lambda b, t: (t, 0)),
                  pl.BlockSpec((tq, DV_B), lambda b, t: (t, 0)),
                  pl.BlockSpec((DEC_SEQ, DV_B), lambda b, t: (0, 0)),
                  pl.BlockSpec((DEC_SEQ, DV_B), lambda b, t: (0, 0)),
                  pl.BlockSpec((1, W_B), lambda b, t: (0, 0))] + cast_specs,
        out_specs=[pl.BlockSpec((tq, W_B), lambda b, t: (b * nq + t, 0))] + cast_specs,
        out_shape=[jax.ShapeDtypeStruct((T_SAMPLE, W_B), BF16)] + [jax.ShapeDtypeStruct(w.shape, BF16) for w in cast_ws],
        scratch_shapes=[pltpu.VMEM((DEC_SEQ, W_B), BF16), pltpu.VMEM((DEC_SEQ, W_B), BF16)],
        compiler_params=_cparams(("arbitrary", "arbitrary")),
        name="diff_attn_sample",
    )(lv, qb, kb, vb, kc, vc, cos, sin_signed, cos, sin_signed, gain, *cast_ws)


def _rope_tables():
    rows = DEC_SEQ // GRID_W
    row = jnp.repeat(jnp.arange(rows, dtype=F32), GRID_W)
    col = jnp.tile(jnp.arange(GRID_W, dtype=F32), rows)
    freqs = ROPE_THETA ** (-jnp.arange(AXIS_PAIRS, dtype=F32) / AXIS_PAIRS)
    ar, ac = row[:, None] * freqs, col[:, None] * freqs
    cos64 = jnp.concatenate([jnp.cos(ar), jnp.cos(ar), jnp.cos(ac), jnp.cos(ac)], axis=-1)
    sin64 = jnp.concatenate([-jnp.sin(ar), jnp.sin(ar), -jnp.sin(ac), jnp.sin(ac)], axis=-1)
    return jnp.tile(cos64, (1, 2)), jnp.tile(sin64, (1, 2))


def _group_mean(x, ones64):
    hi, lo = _split_bf16(x)
    return (_dot(hi, ones64) + _dot(lo, ones64)) * (1.0 / DG_C)


def _sgu(uv, ws_ref, bs_ref, fillers=()):
    fillers = list(fillers)
    fill = lambda: fillers.pop(0)() if fillers else None
    r = lax.broadcasted_iota(jnp.int32, (W_C, W_C), 0)
    s = lax.broadcasted_iota(jnp.int32, (W_C, W_C), 1)
    ones64 = jnp.where((r // DG_C) == (s // DG_C), 1.0, 0.0).astype(BF16)
    lane = lax.broadcasted_iota(jnp.int32, (SGU_CHUNK, W_C), 1)
    outs = []
    for n in range(uv.shape[0] // SGU_CHUNK):
        rows = slice(n * SGU_CHUNK, (n + 1) * SGU_CHUNK)
        u = _gelu_tanh(uv[rows, 0:256])
        v = _gelu_tanh(uv[rows, 256:512])
        mu = _group_mean(v, ones64)
        fill()
        d = v - mu
        var = _group_mean(d * d, ones64)
        fill()
        vn = d * lax.rsqrt(var + EPS)
        s_mix = bs_ref[...]
        for g in range(N_GROUPS_C):
            vn_g = jnp.where((lane // DG_C) == g, vn, 0.0).astype(BF16)
            s_mix = s_mix + _dot(ws_ref[g], vn_g)
        fill()
        outs.append((u * s_mix).astype(BF16))
    while fillers:
        fill()
    return jnp.concatenate(outs, axis=0)


OUT_TM = 1024
OUT_SLABS = 4


SEL_LANE0 = N_EXPERTS


def _top2_gates(logits):
    lane = lax.broadcasted_iota(jnp.int32, logits.shape, 1).astype(F32)
    neg = -jnp.inf
    lg = jnp.where(lane < N_EXPERTS, logits, neg)
    m1 = jnp.max(lg, axis=-1, keepdims=True)
    i1 = jnp.min(jnp.where(lg == m1, lane, float(LANES)), axis=-1, keepdims=True)
    lg2 = jnp.where(lane == i1, neg, lg)
    m2 = jnp.max(lg2, axis=-1, keepdims=True)
    i2 = jnp.min(jnp.where(lg2 == m2, lane, float(LANES)), axis=-1, keepdims=True)
    e2 = jnp.exp(m2 - m1)
    den = 1.0 + e2
    gates = jnp.where(lane == i1, 1.0 / den, 0.0) + jnp.where(lane == i2, e2 / den, 0.0)
    sel = jnp.where((lane == i1 + SEL_LANE0) | (lane == i2 + SEL_LANE0), 1.0, 0.0)
    return gates + sel


def _store_token_tiles(ref, val, r0=0):
    n = val.shape[0]
    for k in range(D_MODEL // LANES):
        ref[pl.ds(8 * r0 + k, n, stride=8), :] = val[:, k * LANES:(k + 1) * LANES]


def _load_token_tiles(ref, n):
    return jnp.concatenate([ref[pl.ds(k, n, stride=8), :] for k in range(D_MODEL // LANES)], axis=-1)


def _out_proj_kernel(*refs, n_x, moe):
    oa_refs, ob_refs, (oc_ref,), x_refs = refs[0:2], refs[2:4], refs[4:5], refs[5:5 + n_x]
    w_ref, g1_ref, nrm_ref, sc_ref, sh_ref, *rest = refs[5 + n_x:]
    if moe:
        rw_ref, x1_ref, h2t_ref, gates_ref, gates_t_ref, wo_ref = rest
    else:
        x1_ref, h2_ref, wo_ref = rest

    @pl.when(pl.program_id(0) == 0)
    def _():
        wo_ref[...] = w_ref[...].astype(BF16)

    if moe:
        rw_hi, rw_lo = _split_bf16(rw_ref[...])

    def rows_chain(r0, n):
        rs = slice(r0, r0 + n)
        y = (_dot(_pick_piece(oa_refs, OUT_TM, rs), wo_ref[0:256, :])
             + _dot(_pick_piece(ob_refs, OUT_TM, rs), wo_ref[256:768, :]) + _dot(oc_ref[rs, :], wo_ref[768:1024, :]))
        yield
        x1 = _pick_piece(x_refs, OUT_TM, rs) + g1_ref[...] * y
        x1_ref[rs, :] = x1
        yn = x1 * lax.rsqrt(jnp.mean(x1 * x1, axis=-1, keepdims=True) + EPS)
        h = (yn * nrm_ref[...]) * (1.0 + sc_ref[...]) + sh_ref[...]
        if moe:
            _store_token_tiles(h2t_ref, h, r0)
            h_hi, h_lo = _split_bf16(h)
            logits = _dot(h_hi, rw_hi) + (_dot(h_lo, rw_hi) + _dot(h_hi, rw_lo))
            yield
            gates = _top2_gates(logits)
            gates_ref[rs, :] = gates
            gates_t_ref[:, rs] = gates.T
        else:
            h2_ref[rs, :] = h.astype(BF16)

    n_slab = OUT_TM // OUT_SLABS
    _interleave(*[rows_chain(i * n_slab, n_slab) for i in range(OUT_SLABS)])


def _out_proj_call(oa_pieces, ob_pieces, oc, x_pieces, w_out, l, mod4, norm_g, router_w=None):
    tm = OUT_TM
    n = T_ALL // tm
    moe = router_w is not None
    row = functools.partial(_mod_row, tm=tm)
    mod_spec = lambda k: pl.BlockSpec((None, None, 1, D_MODEL), lambda i: (row(i), k, 0, 0))
    tok = lambda w: pl.BlockSpec((tm, w), lambda i: (i, 0))
    full = lambda a: pl.BlockSpec(a.shape, lambda i: (0,) * a.ndim)
    in_specs = (_piece_specs(oa_pieces, tm, W_A) + _piece_specs(ob_pieces, tm, W_B) + [tok(W_C)]
                + _piece_specs(x_pieces, tm, D_MODEL)
                + [_resident_layer(w_out, l), mod_spec(2), full(norm_g), mod_spec(4), mod_spec(3)])
    args = [*oa_pieces, *ob_pieces, oc, *x_pieces, w_out, mod4, norm_g, mod4, mod4]
    if moe:
        in_specs.append(full(router_w))
        args.append(router_w)
        out_specs = [tok(D_MODEL), pl.BlockSpec((tm * 8, LANES), lambda i: (i, 0)), tok(LANES),
                     pl.BlockSpec((LANES, tm), lambda i: (0, i))]
        out_shape = [jax.ShapeDtypeStruct((T_ALL, D_MODEL), F32), jax.ShapeDtypeStruct((T_ALL * 8, LANES), F32),
                     jax.ShapeDtypeStruct((T_ALL, LANES), F32), jax.ShapeDtypeStruct((LANES, T_ALL), F32)]
    else:
        out_specs = [tok(D_MODEL), tok(D_MODEL)]
        out_shape = [jax.ShapeDtypeStruct((T_ALL, D_MODEL), F32), jax.ShapeDtypeStruct((T_ALL, D_MODEL), BF16)]
    return pl.pallas_call(
        functools.partial(_out_proj_kernel, n_x=len(x_pieces), moe=moe),
        grid=(n,),
        in_specs=in_specs,
        out_specs=out_specs,
        out_shape=out_shape,
        scratch_shapes=[pltpu.VMEM((D_MODEL, D_MODEL), BF16)],
        compiler_params=_cparams(("arbitrary",)),
        name="out_proj_moe" if moe else "out_proj",
    )(*args)


FFN_TM = 512
MXU_N = 256
FFN_SPLITS = (0, 512, 1024, 1536, 2048, 2560, D_FF)
assert all(s % MXU_N == 0 for s in FFN_SPLITS)


def _swiglu(h, wg_ref, wu_ref, wd_ref):
    out = None
    for c0, c1 in zip(FFN_SPLITS[:-1], FFN_SPLITS[1:]):
        act = _silu(_dot(h, wg_ref[:, c0:c1])) * _dot(h, wu_ref[:, c0:c1])
        d = _dot(act.astype(BF16), wd_ref[c0:c1, :])
        out = d if out is None else out + d
    return out


def _ffn_kernel(h_ref, x_ref, g2_ref, wg_ref, wu_ref, wd_ref, w_ref, o_ref, wb_ref):
    @pl.when(pl.program_id(0) < RIDER_BLOCKS_20)
    def _():
        wb_ref[...] = w_ref[...].astype(BF16)

    o_ref[...] = x_ref[...] + g2_ref[...] * _swiglu(h_ref[...], wg_ref, wu_ref, wd_ref)


def _ffn_call(h2, x1, mod4, wg, wu, wd, cast_w):
    tm = FFN_TM
    row = functools.partial(_mod_row, tm=tm)
    tok = lambda w: pl.BlockSpec((tm, w), lambda i: (i, 0))
    resident = lambda a: pl.BlockSpec(a.shape, lambda i: (0, 0), pipeline_mode=pl.Buffered(1))
    rider_spec, rider_shape = _cast_rider(cast_w, RIDER_BLOCKS_20)
    return pl.pallas_call(
        _ffn_kernel,
        grid=(T_ALL // tm,),
        in_specs=[tok(D_MODEL), tok(D_MODEL),
                  pl.BlockSpec((None, None, 1, D_MODEL), lambda i: (row(i), 5, 0, 0)),
                  resident(wg), resident(wu), resident(wd), rider_spec],
        out_specs=[tok(D_MODEL), rider_spec],
        out_shape=[jax.ShapeDtypeStruct((T_ALL, D_MODEL), F32), rider_shape],
        compiler_params=_cparams(("arbitrary",)),
        name="ffn_dense",
    )(h2, x1, mod4, wg, wu, wd, cast_w)


MOE_TM = 512
MOE_NT_MAX = (2 * T_ALL) // MOE_TM + N_EXPERTS
MOE_ROWS = MOE_NT_MAX * MOE_TM
PLAN_BLK = 512
MISC_LAST_START = 8
MISC_NT = 16


def _moe_plan_kernel(gt_ref, posa_ref, posb_ref, te_ref, ti_ref, tv_ref, misc_ref):
    tm = float(MOE_TM)
    sel = gt_ref[SEL_LANE0:SEL_LANE0 + N_EXPERTS, :]
    cnt = jnp.sum(sel, axis=1, keepdims=True)
    nt = jnp.floor((cnt + (tm - 1.0)) * (1.0 / tm))
    sub = lax.broadcasted_iota(jnp.int32, (N_EXPERTS, LANES), 0).astype(F32)
    lane = lax.broadcasted_iota(jnp.int32, (N_EXPERTS, LANES), 1).astype(F32)
    nt_b = jnp.broadcast_to(nt, (N_EXPERTS, LANES))
    nt_row = jnp.sum(jnp.where(sub == lane, nt_b, 0.0), axis=0, keepdims=True)
    toff = jnp.sum(jnp.where(lane < sub, jnp.broadcast_to(nt_row, (N_EXPERTS, LANES)), 0.0),
                   axis=1, keepdims=True)
    tend = toff + nt
    n_total = jnp.sum(nt, axis=0, keepdims=True)
    jc = jnp.minimum(lane, n_total - 1.0)
    te = jnp.sum(jnp.where(jc >= tend, 1.0, 0.0), axis=0, keepdims=True)
    te_ref[...] = te.astype(jnp.int32)
    ti_ref[...] = jc[0:1, :].astype(jnp.int32)
    rows_left = cnt - (jc - toff) * tm
    tv = jnp.sum(jnp.where((jc >= toff) & (jc < tend), jnp.minimum(rows_left, tm), 0.0), axis=0, keepdims=True)
    tv_ref[...] = tv.astype(jnp.int32)
    last_start = (tend - 1.0) * tm
    ls_row = jnp.sum(jnp.where(sub + MISC_LAST_START == lane, jnp.broadcast_to(last_start, (N_EXPERTS, LANES)), 0.0),
                     axis=0, keepdims=True)
    nt_row2 = jnp.sum(jnp.where(sub + MISC_NT == lane, nt_b, 0.0), axis=0, keepdims=True)
    misc = jnp.where(lane[0:1, :] == 0.0, n_total, 0.0) + ls_row + nt_row2
    misc_ref[...] = misc.astype(jnp.int32)

    off = toff * tm
    r = lax.broadcasted_iota(jnp.int32, (PLAN_BLK, PLAN_BLK), 0)
    c = lax.broadcasted_iota(jnp.int32, (PLAN_BLK, PLAN_BLK), 1)
    upper = jnp.where(r <= c, 1.0, 0.0).astype(BF16)
    carry = jnp.zeros((N_EXPERTS, 1), F32)
    for blk in range(T_ALL // PLAN_BLK):
        cols = slice(blk * PLAN_BLK, (blk + 1) * PLAN_BLK)
        s = gt_ref[SEL_LANE0:SEL_LANE0 + N_EXPERTS, cols]
        rank = _dot(s.astype(BF16), upper) + carry
        pos = off + rank - 1.0
        posa_ref[:, cols] = jnp.min(jnp.where(s > 0.0, pos, 1e9), axis=0, keepdims=True).astype(jnp.int32)
        posb_ref[:, cols] = jnp.max(jnp.where(s > 0.0, pos, -1.0), axis=0, keepdims=True).astype(jnp.int32)
        carry = carry + jnp.sum(s, axis=1, keepdims=True)


def _moe_plan_call(gates_t):
    row = lambda w: jax.ShapeDtypeStruct((1, w), jnp.int32)
    full = lambda w: pl.BlockSpec((1, w), lambda: (0, 0))
    return pl.pallas_call(
        _moe_plan_kernel,
        in_specs=[pl.BlockSpec((LANES, T_ALL), lambda: (0, 0))],
        out_specs=[full(T_ALL), full(T_ALL), full(LANES), full(LANES), full(LANES), full(LANES)],
        out_shape=[row(T_ALL), row(T_ALL), row(LANES), row(LANES), row(LANES), row(LANES)],
        compiler_params=pltpu.CompilerParams(vmem_limit_bytes=VMEM_LIMIT),
        name="moe_plan",
    )(gates_t)


DMA_UNROLL = 8


def _row_tile(ref, row):
    return ref.at[pl.ds(pl.multiple_of(row * 8, 8), 8), :]


def _moe_scatter_kernel(misc_ref, posa_ref, posb_ref, h_ref, xs_ref, zero_ref, sem):
    tm = h_ref.shape[0] // 8

    @pl.when(pl.program_id(0) == 0)
    def _():
        zero_ref[...] = jnp.zeros_like(zero_ref)

        def zero_tile(first_row):
            start = pl.multiple_of(first_row * 8, 8)
            cp = pltpu.make_async_copy(zero_ref, xs_ref.at[pl.ds(start, MOE_TM * 8), :], sem.at[0])
            cp.start()
            cp.wait()

        for e in range(N_EXPERTS):
            @pl.when(misc_ref[0, MISC_NT + e] > 0)
            def _():
                zero_tile(misc_ref[0, MISC_LAST_START + e])

        def zero_tail(j, carry):
            zero_tile(j * MOE_TM)
            return carry

        lax.fori_loop(misc_ref[0, 0], MOE_NT_MAX, zero_tail, 0)

    def issue(r, carry):
        src = _row_tile(h_ref, r)
        pltpu.make_async_copy(src, _row_tile(xs_ref, posa_ref[0, r]), sem.at[0]).start(priority=0)
        pltpu.make_async_copy(src, _row_tile(xs_ref, posb_ref[0, r]), sem.at[1]).start(priority=1)
        return carry

    lax.fori_loop(0, tm, issue, 0, unroll=DMA_UNROLL)
    for k in range(2):
        pltpu.make_async_copy(h_ref, xs_ref.at[pl.ds(0, tm * 8), :], sem.at[k]).wait()


SCATTER_TM = 2048


def _moe_scatter_call(misc, posa3, posb3, h2t):
    tm = SCATTER_TM
    smem_row = pl.BlockSpec((None, 1, tm), lambda i: (i, 0, 0), memory_space=pltpu.SMEM)
    return pl.pallas_call(
        _moe_scatter_kernel,
        grid=(T_ALL // tm,),
        in_specs=[pl.BlockSpec((1, LANES), lambda i: (0, 0), memory_space=pltpu.SMEM), smem_row, smem_row,
                  pl.BlockSpec((tm * 8, LANES), lambda i: (i, 0))],
        out_specs=pl.BlockSpec(memory_space=pl.ANY),
        out_shape=jax.ShapeDtypeStruct((MOE_ROWS * 8, LANES), F32),
        scratch_shapes=[pltpu.VMEM((MOE_TM * 8, LANES), F32), pltpu.SemaphoreType.DMA((2,))],
        compiler_params=_cparams(("arbitrary",)),
        name="moe_scatter",
    )(misc, posa3, posb3, h2t)


MOE_HALF = MOE_TM // 2


def _ffn_grouped_kernel(te_ref, ti_ref, tv_ref, misc_ref, x_ref, wg_ref, wu_ref, wd_ref, o_ref):
    j = pl.program_id(0)
    live = j < misc_ref[0]
    rows = tv_ref[j]

    @pl.when(live & (rows > MOE_HALF))
    def _():
        h = _load_token_tiles(x_ref, MOE_TM).astype(BF16)
        _store_token_tiles(o_ref, _swiglu(h, wg_ref, wu_ref, wd_ref))

    @pl.when(live & (rows <= MOE_HALF))
    def _():
        h = _load_token_tiles(x_ref, MOE_HALF).astype(BF16)
        _store_token_tiles(o_ref, _swiglu(h, wg_ref, wu_ref, wd_ref))
        o_ref[MOE_HALF * 8:, :] = jnp.zeros((MOE_HALF * 8, LANES), F32)

    @pl.when(jnp.logical_not(live))
    def _():
        o_ref[...] = jnp.zeros_like(o_ref)


def _ffn_grouped_call(te, ti, tv, misc, xs, wg, wu, wd):
    expert = lambda *s: pl.BlockSpec((None,) + s, lambda j, te, ti, tv, misc: (te[j], 0, 0))
    grid_spec = pltpu.PrefetchScalarGridSpec(
        num_scalar_prefetch=4,
        grid=(MOE_NT_MAX,),
        in_specs=[pl.BlockSpec((MOE_TM * 8, LANES), lambda j, te, ti, tv, misc: (ti[j], 0)),
                  expert(D_MODEL, D_FF), expert(D_MODEL, D_FF), expert(D_FF, D_MODEL)],
        out_specs=pl.BlockSpec((MOE_TM * 8, LANES), lambda j, te, ti, tv, misc: (j, 0)),
    )
    return pl.pallas_call(
        _ffn_grouped_kernel,
        grid_spec=grid_spec,
        out_shape=jax.ShapeDtypeStruct((MOE_ROWS * 8, LANES), F32),
        compiler_params=_cparams(("arbitrary",)),
        name="ffn_grouped",
    )(te, ti, tv, misc, xs, wg, wu, wd)


COMBINE_TM = 256


def _moe_combine_kernel(posa_ref, posb_ref, posa_next_ref, posb_next_ref, ys_ref, x_ref, g2_ref, gates_ref, nf_ref,
                        op_ref, os_ref, bufa_ref, bufb_ref, sem):
    tm = COMBINE_TM
    i = pl.program_id(0)
    n = pl.num_programs(0)
    slot = i % 2

    def gather(pa_ref, pb_ref, s):
        def issue(r, carry):
            pltpu.make_async_copy(_row_tile(ys_ref, pa_ref[0, r]), _row_tile(bufa_ref.at[s], r),
                                  sem.at[s, 0]).start(priority=0)
            pltpu.make_async_copy(_row_tile(ys_ref, pb_ref[0, r]), _row_tile(bufb_ref.at[s], r),
                                  sem.at[s, 1]).start(priority=1)
            return carry

        lax.fori_loop(0, tm, issue, 0, unroll=DMA_UNROLL)

    @pl.when(i == 0)
    def _():
        gather(posa_ref, posb_ref, 0)

    @pl.when(i + 1 < n)
    def _():
        gather(posa_next_ref, posb_next_ref, 1 - slot)

    gates = gates_ref[...]
    lane = lax.broadcasted_iota(jnp.int32, gates.shape, 1).astype(F32)
    is_sel = (lane >= SEL_LANE0) & (lane < SEL_LANE0 + N_EXPERTS) & (gates > 0.0)
    ia = jnp.min(jnp.where(is_sel, lane, float(LANES)), axis=-1, keepdims=True) - SEL_LANE0
    ib = jnp.max(jnp.where(is_sel, lane, -1.0), axis=-1, keepdims=True) - SEL_LANE0
    wa = jnp.sum(jnp.where(lane == ia, gates, 0.0), axis=-1, keepdims=True)
    wb = jnp.sum(jnp.where(lane == ib, gates, 0.0), axis=-1, keepdims=True)

    pltpu.make_async_copy(ys_ref.at[pl.ds(0, tm * 8), :], bufa_ref.at[slot], sem.at[slot, 0]).wait()
    pltpu.make_async_copy(ys_ref.at[pl.ds(0, tm * 8), :], bufb_ref.at[slot], sem.at[slot, 1]).wait()

    y = wa * _load_token_tiles(bufa_ref.at[slot], tm) + wb * _load_token_tiles(bufb_ref.at[slot], tm)
    out = x_ref[...] + g2_ref[...] * y
    out = (out * lax.rsqrt(jnp.mean(out * out, axis=-1, keepdims=True) + EPS)) * nf_ref[...]

    @pl.when(i < T_PROMPT // tm)
    def _():
        op_ref[...] = out

    @pl.when(i >= T_PROMPT // tm)
    def _():
        os_ref[...] = out


def _moe_combine_call(posa3, posb3, ys, x1, mod4, gates, norm_f):
    tm = COMBINE_TM
    n = T_ALL // tm
    n_p = T_PROMPT // tm
    row = functools.partial(_mod_row, tm=tm)
    smem_row = pl.BlockSpec((None, 1, tm), lambda i: (i, 0, 0), memory_space=pltpu.SMEM)
    smem_next = pl.BlockSpec((None, 1, tm), lambda i: (jnp.minimum(i + 1, n - 1), 0, 0), memory_space=pltpu.SMEM)
    tok = lambda w: pl.BlockSpec((tm, w), lambda i: (i, 0))
    return pl.pallas_call(
        _moe_combine_kernel,
        grid=(n,),
        in_specs=[smem_row, smem_row, smem_next, smem_next, pl.BlockSpec(memory_space=pl.ANY), tok(D_MODEL),
                  pl.BlockSpec((None, None, 1, D_MODEL), lambda i: (row(i), 5, 0, 0)), tok(LANES),
                  pl.BlockSpec((1, D_MODEL), lambda i: (0, 0))],
        out_specs=[pl.BlockSpec((tm, D_MODEL), lambda i: (jnp.minimum(i, n_p - 1), 0)),
                   pl.BlockSpec((tm, D_MODEL), lambda i: (jnp.maximum(i - n_p, 0), 0))],
        out_shape=[jax.ShapeDtypeStruct((T_PROMPT, D_MODEL), F32), jax.ShapeDtypeStruct((T_SAMPLE, D_MODEL), F32)],
        scratch_shapes=[pltpu.VMEM((2, tm * 8, LANES), F32), pltpu.VMEM((2, tm * 8, LANES), F32),
                        pltpu.SemaphoreType.DMA((2, 2))],
        compiler_params=_cparams(("arbitrary",)),
        name="moe_combine",
    )(posa3, posb3, posa3, posb3, ys, x1, mod4, gates, norm_f)


def _moe_call(h2t, x1, mod4, gates, gates_t, wg, wu, wd, norm_f):
    posa, posb, te, ti, tv, misc = _moe_plan_call(gates_t)
    xs = _moe_scatter_call(misc, posa.reshape(T_ALL // SCATTER_TM, 1, SCATTER_TM),
                           posb.reshape(T_ALL // SCATTER_TM, 1, SCATTER_TM), h2t)
    ys = _ffn_grouped_call(te.reshape(LANES), ti.reshape(LANES), tv.reshape(LANES), misc.reshape(LANES), xs, wg, wu, wd)
    return _moe_combine_call(posa.reshape(T_ALL // COMBINE_TM, 1, COMBINE_TM),
                             posb.reshape(T_ALL // COMBINE_TM, 1, COMBINE_TM), ys, x1, mod4, gates, norm_f)


def kernel(x_prompt, x_sample, cache_k, cache_v, state_gla_fwd, state_gla_bwd, c, c_ctx, w_ada, b_ada, norm_mix, norm_ffn, w_in, w_out, gla_w_up, gla_b_up, gla_norm, diff_lambda, diff_norm, sgu_w, sgu_b, ffn_w_gate, ffn_w_up, ffn_w_down, router_w, moe_w_gate, moe_w_up, moe_w_down, norm_f):
    assert DEPTH == 2
    x_pieces = [x_prompt.reshape(T_PROMPT, D_MODEL), x_sample.reshape(T_SAMPLE, D_MODEL)]
    cvecs = jnp.concatenate([c_ctx[None, :], c, jnp.zeros((N_MOD_ROWS - 1 - DEC_BATCH, D_MODEL), F32)], axis=0)
    mod = _ada_call(cvecs, w_ada, b_ada)
    cos, sin_signed = _rope_tables()
    zeros_state = jnp.zeros((BATCH, 64, 256), F32)

    w_in_t = jnp.swapaxes(w_in, 1, 2)
    moe_w = [moe_w_gate[0], moe_w_up[0], moe_w_down[0]]
    for l in range(DEPTH):
        mod4 = mod[l].reshape(N_MOD_ROWS, 6, 1, D_MODEL)
        w_up = jnp.zeros((LANES, 2 * W_A), F32)
        w_up = w_up.at[0:GLA_RANK, 0:W_A].set(gla_w_up[l, 0]).at[GLA_RANK:2 * GLA_RANK, W_A:].set(gla_w_up[l, 1])
        b_up = gla_b_up[l].reshape(1, 2 * W_A)
        bs_full = jnp.repeat(sgu_b[l].T, DG_C, axis=1)
        g4, la, qb, kb, vb, oc, moe_w[l] = _in_proj_call(x_pieces, norm_mix[l][None, :], mod4, w_in_t, l,
                                                         w_up.astype(BF16), b_up, sgu_w[l].astype(BF16), bs_full,
                                                         moe_w[l])

        gain_a = gla_norm[l][None, :]
        if l == 0:
            oa_p, sf0, sb0 = _gla_call(g4, la, zeros_state, zeros_state, gain_a, batch=BATCH, seq=SEQ, row_block0=0)
            kb0, vb0 = kb, vb
        else:
            oa_p, new_sf, new_sb, new_cache_k = _gla_call(g4, la, zeros_state, zeros_state, gain_a, (sf0, sb0, kb0, kb),
                                                          batch=BATCH, seq=SEQ, row_block0=0)
        oa_s, _, _ = _gla_call(g4, la, _state_to_kernel(state_gla_fwd[:, l]),
                               _state_to_kernel(state_gla_bwd[:, l]), gain_a,
                               batch=DEC_BATCH, seq=DEC_SEQ, row_block0=T_PROMPT // DEC_SEQ)

        lam_init = 0.8 - 0.6 * math.exp(-0.3 * l)
        gain_b = diff_norm[l][None, :]
        if l == 0:
            (ob_p,) = _attn_prompt_call(diff_lambda[l], qb, kb, vb, gain_b, lam_init=lam_init)
        else:
            ob_p, new_cache_v = _attn_prompt_call(diff_lambda[l], qb, kb, vb, gain_b, (vb0,),
                                                  lam_init=lam_init, write_cache=True)
        ob_s, *dense_w = _attn_sample_call(diff_lambda[l], qb, kb, vb,
                                           cache_k[:, l].reshape(DEC_BATCH, PAST_LEN, W_B),
                                           cache_v[:, l].reshape(DEC_BATCH, PAST_LEN, W_B),
                                           cos, sin_signed, gain_b,
                                           (ffn_w_gate[0], ffn_w_up[0], ffn_w_down[0]) if l == 0 else (),
                                           lam_init=lam_init)

        if l == 0:
            x1, h2 = _out_proj_call([oa_p, oa_s], [ob_p, ob_s], oc, x_pieces, w_out, l, mod4, norm_ffn[l][None, :])
            x_next, moe_w[2] = _ffn_call(h2, x1, mod4, *dense_w, moe_w[2])
            x_pieces = [x_next]
        else:
            rw = jnp.pad(router_w[0], ((0, 0), (0, LANES - N_EXPERTS)))
            x1, h2t, gates, gates_t = _out_proj_call([oa_p, oa_s], [ob_p, ob_s], oc, x_pieces, w_out, l, mod4,
                                                     norm_ffn[l][None, :], rw)
            y_prompt, y_sample = _moe_call(h2t, x1, mod4, gates, gates_t, *moe_w, norm_f[None, :])

    state_shape = (BATCH, DEPTH, N_HEADS_A, DK_A, DK_A)
    return (y_prompt.reshape(BATCH, SEQ, D_MODEL), y_sample.reshape(DEC_BATCH, DEC_SEQ, D_MODEL),
            new_cache_k.reshape(BATCH, DEPTH, SEQ, N_HEADS_B, DV_B), new_cache_v.reshape(BATCH, DEPTH, SEQ, N_HEADS_B, DV_B),
            new_sf.reshape(state_shape), new_sb.reshape(state_shape))
```

```python
import functools
import math

import jax
import jax.numpy as jnp
import numpy as np
from jax import lax
from jax.experimental import pallas as pl
from jax.experimental.pallas import tpu as pltpu

F32 = jnp.float32
BF16 = jnp.bfloat16

D_MODEL = 1024
BATCH = 32
SEQ = 256
DEPTH = 2
DEC_BATCH = 2
DEC_SEQ = 1024
PAST_LEN = 256
GRID_W = 64
N_HEADS_A = 4
DK_A = 64
W_A = 256
GLA_RANK = 16
GLA_TAU = 16.0
GLA_CHUNK = 64
N_HEADS_B = 4
DH_B = 64
DV_B = 128
W_B = 512
ROPE_THETA = 10000.0
AXIS_PAIRS = DH_B // 4
N_GROUPS_C = 4
DG_C = 64
W_C = 256
SGU_CHUNK = 128
D_FF = 2816
N_EXPERTS = 8
EPS = 1e-6

T_PROMPT = BATCH * SEQ
T_SAMPLE = DEC_BATCH * DEC_SEQ
T_ALL = T_PROMPT + T_SAMPLE
N_MOD_ROWS = 8
LANES = 128
VMEM_LIMIT = 56 * 1024 * 1024


def _cparams(sem):
    return pltpu.CompilerParams(dimension_semantics=sem, vmem_limit_bytes=VMEM_LIMIT)


def _dot(a, b):
    return jnp.dot(a, b, preferred_element_type=F32)


def _dot_nt(a, b):
    return lax.dot_general(a, b, (((1,), (1,)), ((), ())), preferred_element_type=F32)


def _dot_tn(a, b):
    return lax.dot_general(a, b, (((0,), (0,)), ((), ())), preferred_element_type=F32)


def _split_bf16(x):
    hi = x.astype(BF16)
    lo = (x - hi.astype(F32)).astype(BF16)
    return hi, lo


def _dot3(a, w):
    a_hi, a_lo = _split_bf16(a)
    w_hi, w_lo = _split_bf16(w)
    return _dot(a_hi, w_hi) + (_dot(a_lo, w_hi) + _dot(a_hi, w_lo))


def _sigmoid(x):
    return 1.0 / (1.0 + jnp.exp(-x))


def _silu(x):
    return x * _sigmoid(x)


def _gelu_tanh(x):
    c = math.sqrt(2.0 / math.pi)
    return x * (0.5 * (1.0 + jnp.tanh(c * (x + 0.044715 * (x * x * x)))))


def _log_sigmoid(x):
    return jnp.minimum(x, 0.0) - jnp.log(1.0 + jnp.exp(-jnp.abs(x)))


def _mod_row(i, tm):
    n_p = T_PROMPT // tm
    per_b = DEC_SEQ // tm
    return jnp.where(i < n_p, 0, 1 + (i - n_p) // per_b)


ADA_TN = 1536


def _ada_kernel(c_ref, w_ref, b_ref, o_ref):
    a = _silu(c_ref[...])
    o_ref[...] = _dot3(a, w_ref[...]) + b_ref[...]


def _ada_call(cvecs, w_ada, b_ada):
    n_col = (6 * D_MODEL) // ADA_TN
    return pl.pallas_call(
        _ada_kernel,
        grid=(DEPTH, n_col),
        in_specs=[
            pl.BlockSpec((N_MOD_ROWS, D_MODEL), lambda l, j: (0, 0)),
            pl.BlockSpec((None, D_MODEL, ADA_TN), lambda l, j: (l, 0, j)),
            pl.BlockSpec((None, 1, ADA_TN), lambda l, j: (l, 0, j)),
        ],
        out_specs=pl.BlockSpec((None, N_MOD_ROWS, ADA_TN), lambda l, j: (l, 0, j)),
        out_shape=jax.ShapeDtypeStruct((DEPTH, N_MOD_ROWS, 6 * D_MODEL), F32),
        compiler_params=_cparams(("arbitrary", "arbitrary")),
        name="ada_mod",
    )(cvecs, w_ada, b_ada.reshape(DEPTH, 1, 6 * D_MODEL))


IN_TM = 512
IN_COLS = 3104
Z_COL0 = 1024
Z_COLS = 2 * GLA_RANK
W_MAIN = 3072


def _piece_specs(pieces, tm, width):
    specs, t0 = [], 0
    for arr in pieces:
        nt = arr.shape[0] // tm
        specs.append(pl.BlockSpec((tm, width), lambda i, t0=t0, nt=nt: (jnp.clip(i - t0, 0, nt - 1), 0)))
        t0 += nt
    assert t0 * tm == T_ALL and len(pieces) in (1, 2) and (len(pieces) == 1 or pieces[0].shape[0] == T_PROMPT)
    return specs


def _pick_piece(refs, tm, rows=slice(None)):
    if len(refs) == 1:
        return refs[0][rows, :]
    return jnp.where(pl.program_id(0) < T_PROMPT // tm, refs[0][rows, :], refs[1][rows, :])


RIDER_BLOCKS_20 = 16


W_STAGE_ROWS = 512


def _in_proj_kernel(*refs, n_x, layer):
    x_refs = refs[:n_x]
    (nrm_ref, sh_ref, sc_ref, w_hbm, wup_ref, bup_ref, ws_ref, bs_ref, cast_ref,
     g4_ref, la_ref, qb_ref, kb_ref, vb_ref, oc_ref, cast_out_ref, wm_ref, wz_ref, stage_ref, sem) = refs[n_x:]

    @pl.when(pl.program_id(0) < RIDER_BLOCKS_20)
    def _():
        cast_out_ref[...] = cast_ref[...].astype(BF16)

    @pl.when(pl.program_id(0) == 0)
    def _():
        n = W_STAGE_ROWS
        moves = [(src, src if src < Z_COL0 else src - Z_COLS)
                 for src in (*range(0, Z_COL0, n), *range(Z_COL0 + Z_COLS, IN_COLS, n))]
        copy = lambda k: pltpu.make_async_copy(w_hbm.at[layer, pl.ds(moves[k][0], n), :], stage_ref.at[k % 2],
                                               sem.at[k % 2])
        copy(0).start()
        for k, (_, dst) in enumerate(moves):
            if k + 1 < len(moves):
                copy(k + 1).start()
            copy(k).wait()
            wm_ref[dst:dst + n, :] = stage_ref[k % 2].astype(BF16)
        gate_rows = pltpu.make_async_copy(w_hbm.at[layer, pl.ds(Z_COL0, Z_COLS), :],
                                          stage_ref.at[0, pl.ds(0, Z_COLS), :], sem.at[0])
        gate_rows.start()
        gate_rows.wait()
        wz_ref[...] = jnp.zeros_like(wz_ref)
        wz_ref[0:Z_COLS, :] = stage_ref[0, 0:Z_COLS, :].astype(BF16)

    x = _pick_piece(x_refs, IN_TM)
    y = x * lax.rsqrt(jnp.mean(x * x, axis=-1, keepdims=True) + EPS)
    h = (y * nrm_ref[...]) * (1.0 + sc_ref[...]) + sh_ref[...]
    hb = h.astype(BF16)
    def plain_job(ref, c0, w0):
        def job():
            ref[:, c0:c0 + MXU_N] = _dot_nt(hb, wm_ref[w0:w0 + MXU_N, :])
        return job

    jobs = [plain_job(ref, c, w0 + c) for ref, w0, width in
            ((g4_ref, 0, 1024), (qb_ref, 1024, 512), (kb_ref, 1536, 512), (vb_ref, 2048, 512))
            for c in range(0, width, MXU_N)]
    z = _dot_nt(hb, wz_ref[...])
    jobs.pop(0)()
    zz = _dot(z.astype(BF16), wup_ref[...]) + bup_ref[...]
    uv = _dot_nt(hb, wm_ref[2560:3072, :])
    jobs.pop(0)()
    la_ref[...] = _log_sigmoid(zz) * (1.0 / GLA_TAU)
    jobs.pop(0)()
    oc_ref[...] = _sgu(uv, ws_ref, bs_ref, jobs)


def _resident_layer(a, l):
    return pl.BlockSpec((None,) + a.shape[1:], lambda *_: (l,) + (0,) * (a.ndim - 1), pipeline_mode=pl.Buffered(1))


def _in_proj_call(x_pieces, norm_g, mod4, w_in_t, l, w_up, b_up, sgu_ws, sgu_bs, cast_w):
    tm = IN_TM
    rider_spec, rider_shape = _cast_rider(cast_w, RIDER_BLOCKS_20)
    n = T_ALL // tm
    row = functools.partial(_mod_row, tm=tm)
    mod_spec = lambda k: pl.BlockSpec((None, None, 1, D_MODEL), lambda i: (row(i), k, 0, 0))
    full = lambda a: pl.BlockSpec(a.shape, lambda i: (0,) * a.ndim)
    out = lambda w: pl.BlockSpec((tm, w), lambda i: (i, 0))
    return pl.pallas_call(
        functools.partial(_in_proj_kernel, n_x=len(x_pieces), layer=l),
        grid=(n,),
        in_specs=_piece_specs(x_pieces, tm, D_MODEL) + [full(norm_g), mod_spec(0), mod_spec(1), pl.BlockSpec(memory_space=pl.ANY),
                                                        full(w_up), full(b_up), full(sgu_ws), full(sgu_bs),
                                                        rider_spec],
        out_specs=[out(1024), out(512), out(512), out(512), out(512), out(W_C), rider_spec],
        out_shape=[jax.ShapeDtypeStruct((T_ALL, w), F32) for w in (1024, 512, 512, 512, 512)]
        + [jax.ShapeDtypeStruct((T_ALL, W_C), BF16), rider_shape],
        scratch_shapes=[pltpu.VMEM((W_MAIN, D_MODEL), BF16), pltpu.VMEM((LANES, D_MODEL), BF16),
                        pltpu.VMEM((2, W_STAGE_ROWS, D_MODEL), F32), pltpu.SemaphoreType.DMA((2,))],
        compiler_params=_cparams(("arbitrary",)),
        name="in_proj",
    )(*x_pieces, norm_g, mod4, mod4, w_in_t, w_up, b_up, sgu_ws, sgu_bs, cast_w)


GLA_SB = 256
GLA_NC = GLA_SB // GLA_CHUNK


def _head_blocks(x, same64):
    return jnp.where(same64, jnp.concatenate([x] * N_HEADS_A, axis=0), 0.0).astype(BF16)


def _gla_superblock(q, k, vb, v_blocks, la, st_all, tri, mask4, same64, forward):
    c = GLA_CHUNK
    mid, last = (c // 2 - 1, c - 1) if forward else (c // 2, 0)
    la_hi, la_lo = _split_bf16(la)
    b = _dot(tri, la_hi) + _dot(tri, la_lo)
    yield
    rows_of = lambda r: jnp.concatenate(
        [jnp.broadcast_to(b[i * c + r:i * c + r + 1, :], (c, W_A)) for i in range(GLA_NC)], axis=0)
    m = rows_of(mid)
    bl = rows_of(last)
    qe = (q * jnp.exp(b - m)).astype(BF16)
    ke = k * jnp.exp(m - b)
    qi = (q * jnp.exp(b)).astype(BF16)
    ks = (k * jnp.exp(bl - b)).astype(BF16)
    outs = [None] * GLA_NC
    for i in (range(GLA_NC) if forward else reversed(range(GLA_NC))):
        rows = slice(i * c, (i + 1) * c)
        s = _dot_nt(qe[rows, :], _head_blocks(ke[rows, :], same64))
        kv = _dot_tn(vb[rows, :], ks[rows, :])
        yield
        a = jnp.where(mask4, s, 0.0).astype(BF16)
        outs[i] = _dot(a, v_blocks[i]) + _dot_nt(qi[rows, :], st_all.astype(BF16))
        st_all = st_all * jnp.exp(bl[i * c:i * c + 1, :]) + jnp.where(same64, kv, 0.0)
        yield
    return jnp.concatenate(outs, axis=0), st_all


def _interleave(*gens):
    results = [None] * len(gens)
    active = list(enumerate(gens))
    while active:
        for item in list(active):
            try:
                next(item[1])
            except StopIteration as stop:
                results[item[0]] = stop.value
                active.remove(item)
    return results


def _cast_rider(w, n_blocks):
    e, r, c = w.shape
    per = n_blocks // e
    rows = r // per
    assert per * e == n_blocks and rows * per == r and rows % 16 == 0

    def index(b):
        b = jnp.minimum(b, n_blocks - 1)
        return b // per, b % per, 0

    spec = pl.BlockSpec((None, rows, c), index)
    return spec, jax.ShapeDtypeStruct(w.shape, BF16)


def _gla_kernel(*refs, seq, final):
    if final:
        (g4_ref, la_ref, s0f_ref, s0b_ref, gain_ref, sf_prev_ref, sb_prev_ref, k_prev_ref, k_ref,
         o_ref, sf_ref, sb_ref, ck_ref, of_ref, ob_ref) = refs
        for l, src in enumerate((k_prev_ref, k_ref)):
            for h in range(N_HEADS_B):
                ck_ref[l, pl.ds(h, SEQ, stride=N_HEADS_B), :] = src[:, h * DV_B:(h + 1) * DV_B]
    else:
        g4_ref, la_ref, s0f_ref, s0b_ref, gain_ref, o_ref, sf_ref, sb_ref, of_ref, ob_ref = refs
    n = GLA_SB
    nsb = seq // n
    r = lax.broadcasted_iota(jnp.int32, (n, n), 0)
    s = lax.broadcasted_iota(jnp.int32, (n, n), 1)
    same64 = (r // GLA_CHUNK) == (s // GLA_CHUNK)
    lower = same64 & (s <= r)
    upper = same64 & (s >= r)
    tri_f = jnp.where(lower, 1.0, 0.0).astype(BF16)
    tri_b = jnp.where(upper, 1.0, 0.0).astype(BF16)
    ones64 = jnp.where(same64, 1.0, 0.0).astype(BF16)
    key_row = lax.broadcasted_iota(jnp.int32, (GLA_CHUNK, n), 1) % GLA_CHUNK
    qry_row = lax.broadcasted_iota(jnp.int32, (GLA_CHUNK, n), 0)
    mask4_f = key_row <= qry_row
    mask4_b = key_row >= qry_row
    scale = DK_A ** -0.5
    expand = lambda st: jnp.where(same64, jnp.concatenate([st] * N_HEADS_A, axis=0), 0.0)
    compact = lambda st_all: functools.reduce(
        lambda a, b: a + b, [st_all[h * 64:(h + 1) * 64, :] for h in range(N_HEADS_A)])

    def step(i, carry):
        stf, stb = carry
        rf = pl.ds(pl.multiple_of(i * n, n), n)
        rb = pl.ds(pl.multiple_of((nsb - 1 - i) * n, n), n)
        def direction(rows, la_cols, st, tri, mask4, forward):
            q = g4_ref[rows, 0:256] * scale
            k = g4_ref[rows, 256:512]
            v = g4_ref[rows, 512:768]
            v_blocks = [_head_blocks(v[i * GLA_CHUNK:(i + 1) * GLA_CHUNK, :], same64) for i in range(GLA_NC)]
            return _gla_superblock(q, k, v.astype(BF16), v_blocks, la_ref[rows, la_cols], st, tri, mask4, same64,
                                   forward)

        (o_f, stf), (o_b, stb) = _interleave(direction(rf, slice(0, 256), stf, tri_f, mask4_f, True),
                                             direction(rb, slice(256, 512), stb, tri_b, mask4_b, False))
        of_ref[rf, :] = o_f
        ob_ref[rb, :] = o_b
        return stf, stb

    stf, stb = lax.fori_loop(0, nsb, step, (expand(s0f_ref[...]), expand(s0b_ref[...])))
    if final:
        sf_ref[0] = sf_prev_ref[...].T
        sb_ref[0] = sb_prev_ref[...].T
        sf_ref[1] = compact(stf).T
        sb_ref[1] = compact(stb).T
    else:
        sf_ref[...] = compact(stf)
        sb_ref[...] = compact(stb)

    gain = gain_ref[...]

    def finish(i, carry):
        rows = pl.ds(pl.multiple_of(i * n, n), n)
        o = of_ref[rows, :] + ob_ref[rows, :]
        sq_hi, sq_lo = _split_bf16(o * o)
        ms = (_dot(sq_hi, ones64) + _dot(sq_lo, ones64)) * (1.0 / DK_A)
        y = (o * lax.rsqrt(ms + EPS)) * gain
        o_ref[rows, :] = (y * _silu(g4_ref[rows, 768:1024])).astype(BF16)
        return carry

    lax.fori_loop(0, nsb, finish, 0)


def _gla_call(g4, la, s0f, s0b, gain, prev=None, *, batch, seq, row_block0):
    tok = lambda w: pl.BlockSpec((seq, w), lambda b: (row_block0 + b, 0))
    st = pl.BlockSpec((None, 64, 256), lambda b: (b, 0, 0))
    in_specs = [tok(1024), tok(512), st, st, pl.BlockSpec((1, W_A), lambda b: (0, 0))]
    args = [g4, la, s0f, s0b, gain]
    o_spec = pl.BlockSpec((seq, W_A), lambda b: (b, 0))
    o_shape = jax.ShapeDtypeStruct((batch * seq, W_A), BF16)
    if prev is None:
        out_specs = [o_spec, st, st]
        out_shape = [o_shape, jax.ShapeDtypeStruct((batch, 64, 256), F32), jax.ShapeDtypeStruct((batch, 64, 256), F32)]
    else:
        assert seq == SEQ and batch == BATCH and DEPTH == 2
        kblk = pl.BlockSpec((SEQ, W_B), lambda b: (b, 0))
        in_specs += [st, st, kblk, kblk]
        args += list(prev)
        st2 = pl.BlockSpec((None, DEPTH, 256, 64), lambda b: (b, 0, 0, 0))
        st2_shape = jax.ShapeDtypeStruct((batch, DEPTH, 256, 64), F32)
        out_specs = [o_spec, st2, st2, pl.BlockSpec((None, DEPTH, SEQ * N_HEADS_B, DV_B), lambda b: (b, 0, 0, 0))]
        out_shape = [o_shape, st2_shape, st2_shape, jax.ShapeDtypeStruct((batch, DEPTH, SEQ * N_HEADS_B, DV_B), F32)]
    return pl.pallas_call(
        functools.partial(_gla_kernel, seq=seq, final=prev is not None),
        grid=(batch,),
        in_specs=in_specs,
        out_specs=out_specs,
        out_shape=out_shape,
        scratch_shapes=[pltpu.VMEM((seq, W_A), F32), pltpu.VMEM((seq, W_A), F32)],
        compiler_params=_cparams(("arbitrary",)),
        name=f"gla_{seq}",
    )(*args)


def _state_to_kernel(s):
    b = s.shape[0]
    return jnp.transpose(s, (0, 3, 1, 2)).reshape(b, 64, 256)


def _lambda(lv, lam_init):
    l01 = jnp.sum(lv[0:1, :] * lv[1:2, :], axis=-1, keepdims=True)
    l23 = jnp.sum(lv[2:3, :] * lv[3:4, :], axis=-1, keepdims=True)
    return jnp.exp(l01) - jnp.exp(l23) + lam_init


def _softmax_parts(parts):
    mx = functools.reduce(jnp.maximum, [jnp.max(p, axis=-1, keepdims=True) for p in parts])
    es = [jnp.exp(p - mx) for p in parts]
    den = functools.reduce(lambda a, b: a + b, [jnp.sum(e, axis=-1, keepdims=True) for e in es])
    return [e / den for e in es]


def _diff_finish(o, gain, lam_init):
    o = o * lax.rsqrt(jnp.mean(o * o, axis=-1, keepdims=True) + EPS)
    return ((o * gain) * (1.0 - lam_init)).astype(BF16)


QK_SCALE = DH_B ** -0.5


def _key_halves(k):
    first = lax.broadcasted_iota(jnp.int32, k.shape, 1) < DH_B
    return jnp.where(first, k, 0.0).astype(BF16), jnp.where(first, 0.0, k).astype(BF16)


def _attn_prompt_kernel(lv_ref, q_ref, k_ref, v_ref, gain_ref, *rest, lam_init, n_prev):
    prev_refs, (o_ref, *cache_refs) = rest[:n_prev], rest[n_prev:]
    lam = _lambda(lv_ref[...], lam_init)

    def head(h):
        cols = slice(h * DV_B, (h + 1) * DV_B)
        q = (q_ref[:, cols] * QK_SCALE).astype(BF16)
        k1, k2 = _key_halves(k_ref[:, cols])
        s1 = _dot_nt(q, k1)
        s2 = _dot_nt(q, k2)
        yield
        (p1,) = _softmax_parts([s1])
        (p2,) = _softmax_parts([s2])
        a = p1 - lam * p2
        o = _dot(a.astype(BF16), v_ref[:, cols].astype(BF16))
        yield
        o_ref[:, cols] = _diff_finish(o, gain_ref[:, cols], lam_init)
        if cache_refs:
            (cv_ref,) = cache_refs
            for l, src in enumerate((*prev_refs, v_ref)):
                cv_ref[l, pl.ds(h, SEQ, stride=N_HEADS_B), :] = src[:, cols]

    _interleave(*[head(h) for h in range(N_HEADS_B)])


def _attn_prompt_call(lv, qb, kb, vb, gain, prev_v=(), *, lam_init, write_cache=False):
    blk = pl.BlockSpec((SEQ, W_B), lambda b: (b, 0))
    n_prev = len(prev_v)
    assert write_cache or not prev_v
    in_specs = [pl.BlockSpec((4, DH_B), lambda b: (0, 0)), blk, blk, blk,
                pl.BlockSpec((1, W_B), lambda b: (0, 0))] + [blk] * n_prev
    args = [lv, qb, kb, vb, gain, *prev_v]
    out_specs = [blk]
    out_shape = [jax.ShapeDtypeStruct((T_PROMPT, W_B), BF16)]
    if write_cache:
        out_specs.append(pl.BlockSpec((None, n_prev + 1, SEQ * N_HEADS_B, DV_B), lambda b: (b, 0, 0, 0)))
        out_shape.append(jax.ShapeDtypeStruct((BATCH, n_prev + 1, SEQ * N_HEADS_B, DV_B), F32))
    return pl.pallas_call(
        functools.partial(_attn_prompt_kernel, lam_init=lam_init, n_prev=n_prev),
        grid=(BATCH,),
        in_specs=in_specs,
        out_specs=out_specs,
        out_shape=out_shape,
        compiler_params=_cparams(("arbitrary",)),
        name="diff_attn_prompt",
    )(*args)


def _rope(x, cos, sin_signed):
    lane = lax.broadcasted_iota(jnp.int32, x.shape, 1)
    first = (lane % (2 * AXIS_PAIRS)) < AXIS_PAIRS
    partner = jnp.where(first, pltpu.roll(x, LANES - AXIS_PAIRS, 1), pltpu.roll(x, AXIS_PAIRS, 1))
    return x * cos + partner * sin_signed


ATT_TQ = 256


def _attn_sample_kernel(lv_ref, q_ref, k_ref, v_ref, kc_ref, vc_ref, cosq_ref, sinq_ref,
                        cosk_ref, sink_ref, gain_ref, *rest, lam_init, n_cast):
    w_refs, o_ref, wb_refs, (k1_ref, k2_ref) = rest[:n_cast], rest[n_cast], rest[n_cast + 1:2 * n_cast + 1], rest[-2:]
    for w_ref, wb_ref in zip(w_refs, wb_refs):
        wb_ref[...] = w_ref[...].astype(BF16)

    @pl.when(pl.program_id(1) == 0)
    def _():
        for h in range(N_HEADS_B):
            cols = slice(h * DV_B, (h + 1) * DV_B)
            k1_ref[:, cols], k2_ref[:, cols] = _key_halves(_rope(k_ref[:, cols], cosk_ref[...], sink_ref[...]))

    lam = _lambda(lv_ref[...], lam_init)

    def head(h):
        cols = slice(h * DV_B, (h + 1) * DV_B)
        q = (_rope(q_ref[:, cols], cosq_ref[...], sinq_ref[...]) * QK_SCALE).astype(BF16)
        c1, c2 = _key_halves(kc_ref[:, cols])
        s1 = [_dot_nt(q, k1_ref[:, cols]), _dot_nt(q, c1)]
        s2 = [_dot_nt(q, k2_ref[:, cols]), _dot_nt(q, c2)]
        yield
        p1 = _softmax_parts(s1)
        p2 = _softmax_parts(s2)
        a_own = p1[0] - lam * p2[0]
        a_ctx = p1[1] - lam * p2[1]
        o = (_dot(a_own.astype(BF16), v_ref[:, cols].astype(BF16))
             + _dot(a_ctx.astype(BF16), vc_ref[:, cols].astype(BF16)))
        yield
        o_ref[:, cols] = _diff_finish(o, gain_ref[:, cols], lam_init)

    _interleave(*[head(h) for h in range(N_HEADS_B)])


def _attn_sample_call(lv, qb, kb, vb, kc, vc, cos, sin_signed, gain, cast_ws=(), *, lam_init):
    tq = ATT_TQ
    nq = DEC_SEQ // tq
    p0 = T_PROMPT // tq
    s0 = T_PROMPT // DEC_SEQ
    qblk = pl.BlockSpec((tq, W_B), lambda b, t: (p0 + b * nq + t, 0))
    kvblk = pl.BlockSpec((DEC_SEQ, W_B), lambda b, t: (s0 + b, 0))
    cblk = pl.BlockSpec((None, PAST_LEN, W_B), lambda b, t: (b, 0, 0))
    n_steps = DEC_BATCH * nq
    cast_specs = []
    for w in cast_ws:
        assert w.shape[0] % (16 * n_steps) == 0
        cast_specs.append(pl.BlockSpec((w.shape[0] // n_steps, w.shape[1]), lambda b, t: (b * nq + t, 0)))
    return pl.pallas_call(
        functools.partial(_attn_sample_kernel, lam_init=lam_init, n_cast=len(cast_ws)),
        grid=(DEC_BATCH, nq),
        in_specs=[pl.BlockSpec((4, DH_B), lambda b, t: (0, 0)), qblk, kvblk, kvblk, cblk, cblk,
                  pl.BlockSpec((tq, DV_B), lambda b, t: (t, 0)),
                  pl.BlockSpec((tq, DV_B), lambda b, t: (t, 0)),
                  pl.BlockSpec((DEC_SEQ, DV_B), lambda b, t: (0, 0)),
                  pl.BlockSpec((DEC_SEQ, DV_B), lambda b, t: (0, 0)),
                  pl.BlockSpec((1, W_B), lambda b, t: (0, 0))] + cast_specs,
        out_specs=[pl.BlockSpec((tq, W_B), lambda b, t: (b * nq + t, 0))] + cast_specs,
        out_shape=[jax.ShapeDtypeStruct((T_SAMPLE, W_B), BF16)] + [jax.ShapeDtypeStruct(w.shape, BF16) for w in cast_ws],
        scratch_shapes=[pltpu.VMEM((DEC_SEQ, W_B), BF16), pltpu.VMEM((DEC_SEQ, W_B), BF16)],
        compiler_params=_cparams(("arbitrary", "arbitrary")),
        name="diff_attn_sample",
    )(lv, qb, kb, vb, kc, vc, cos, sin_signed, cos, sin_signed, gain, *cast_ws)


def _rope_tables():
    rows = DEC_SEQ // GRID_W
    row = jnp.repeat(jnp.arange(rows, dtype=F32), GRID_W)
    col = jnp.tile(jnp.arange(GRID_W, dtype=F32), rows)
    freqs = ROPE_THETA ** (-jnp.arange(AXIS_PAIRS, dtype=F32) / AXIS_PAIRS)
    ar, ac = row[:, None] * freqs, col[:, None] * freqs
    cos64 = jnp.concatenate([jnp.cos(ar), jnp.cos(ar), jnp.cos(ac), jnp.cos(ac)], axis=-1)
    sin64 = jnp.concatenate([-jnp.sin(ar), jnp.sin(ar), -jnp.sin(ac), jnp.sin(ac)], axis=-1)
    return jnp.tile(cos64, (1, 2)), jnp.tile(sin64, (1, 2))


def _group_mean(x, ones64):
    hi, lo = _split_bf16(x)
    return (_dot(hi, ones64) + _dot(lo, ones64)) * (1.0 / DG_C)


def _sgu(uv, ws_ref, bs_ref, fillers=()):
    fillers = list(fillers)
    fill = lambda: fillers.pop(0)() if fillers else None
    r = lax.broadcasted_iota(jnp.int32, (W_C, W_C), 0)
    s = lax.broadcasted_iota(jnp.int32, (W_C, W_C), 1)
    ones64 = jnp.where((r // DG_C) == (s // DG_C), 1.0, 0.0).astype(BF16)
    lane = lax.broadcasted_iota(jnp.int32, (SGU_CHUNK, W_C), 1)
    outs = []
    for n in range(uv.shape[0] // SGU_CHUNK):
        rows = slice(n * SGU_CHUNK, (n + 1) * SGU_CHUNK)
        u = _gelu_tanh(uv[rows, 0:256])
        v = _gelu_tanh(uv[rows, 256:512])
        mu = _group_mean(v, ones64)
        fill()
        d = v - mu
        var = _group_mean(d * d, ones64)
        fill()
        vn = d * lax.rsqrt(var + EPS)
        s_mix = bs_ref[...]
        for g in range(N_GROUPS_C):
            vn_g = jnp.where((lane // DG_C) == g, vn, 0.0).astype(BF16)
            s_mix = s_mix + _dot(ws_ref[g], vn_g)
        fill()
        outs.append((u * s_mix).astype(BF16))
    while fillers:
        fill()
    return jnp.concatenate(outs, axis=0)


OUT_TM = 1024
OUT_SLABS = 4


SEL_LANE0 = N_EXPERTS


def _top2_gates(logits):
    lane = lax.broadcasted_iota(jnp.int32, logits.shape, 1).astype(F32)
    neg = -jnp.inf
    lg = jnp.where(lane < N_EXPERTS, logits, neg)
    m1 = jnp.max(lg, axis=-1, keepdims=True)
    i1 = jnp.min(jnp.where(lg == m1, lane, float(LANES)), axis=-1, keepdims=True)
    lg2 = jnp.where(lane == i1, neg, lg)
    m2 = jnp.max(lg2, axis=-1, keepdims=True)
    i2 = jnp.min(jnp.where(lg2 == m2, lane, float(LANES)), axis=-1, keepdims=True)
    e2 = jnp.exp(m2 - m1)
    den = 1.0 + e2
    gates = jnp.where(lane == i1, 1.0 / den, 0.0) + jnp.where(lane == i2, e2 / den, 0.0)
    sel = jnp.where((lane == i1 + SEL_LANE0) | (lane == i2 + SEL_LANE0), 1.0, 0.0)
    return gates + sel


def _store_token_tiles(ref, val, r0=0):
    n = val.shape[0]
    for k in range(D_MODEL // LANES):
        ref[pl.ds(8 * r0 + k, n, stride=8), :] = val[:, k * LANES:(k + 1) * LANES]


def _load_token_tiles(ref, n):
    return jnp.concatenate([ref[pl.ds(k, n, stride=8), :] for k in range(D_MODEL // LANES)], axis=-1)


def _out_proj_kernel(*refs, n_x, moe):
    oa_refs, ob_refs, (oc_ref,), x_refs = refs[0:2], refs[2:4], refs[4:5], refs[5:5 + n_x]
    w_ref, g1_ref, nrm_ref, sc_ref, sh_ref, *rest = refs[5 + n_x:]
    if moe:
        rw_ref, x1_ref, h2t_ref, gates_ref, gates_t_ref, wo_ref = rest
    else:
        x1_ref, h2_ref, wo_ref = rest

    @pl.when(pl.program_id(0) == 0)
    def _():
        wo_ref[...] = w_ref[...].astype(BF16)

    if moe:
        rw_hi, rw_lo = _split_bf16(rw_ref[...])

    def rows_chain(r0, n):
        rs = slice(r0, r0 + n)
        y = (_dot(_pick_piece(oa_refs, OUT_TM, rs), wo_ref[0:256, :])
             + _dot(_pick_piece(ob_refs, OUT_TM, rs), wo_ref[256:768, :]) + _dot(oc_ref[rs, :], wo_ref[768:1024, :]))
        yield
        x1 = _pick_piece(x_refs, OUT_TM, rs) + g1_ref[...] * y
        x1_ref[rs, :] = x1
        yn = x1 * lax.rsqrt(jnp.mean(x1 * x1, axis=-1, keepdims=True) + EPS)
        h = (yn * nrm_ref[...]) * (1.0 + sc_ref[...]) + sh_ref[...]
        if moe:
            _store_token_tiles(h2t_ref, h, r0)
            h_hi, h_lo = _split_bf16(h)
            logits = _dot(h_hi, rw_hi) + (_dot(h_lo, rw_hi) + _dot(h_hi, rw_lo))
            yield
            gates = _top2_gates(logits)
            gates_ref[rs, :] = gates
            gates_t_ref[:, rs] = gates.T
        else:
            h2_ref[rs, :] = h.astype(BF16)

    n_slab = OUT_TM // OUT_SLABS
    _interleave(*[rows_chain(i * n_slab, n_slab) for i in range(OUT_SLABS)])


def _out_proj_call(oa_pieces, ob_pieces, oc, x_pieces, w_out, l, mod4, norm_g, router_w=None):
    tm = OUT_TM
    n = T_ALL // tm
    moe = router_w is not None
    row = functools.partial(_mod_row, tm=tm)
    mod_spec = lambda k: pl.BlockSpec((None, None, 1, D_MODEL), lambda i: (row(i), k, 0, 0))
    tok = lambda w: pl.BlockSpec((tm, w), lambda i: (i, 0))
    full = lambda a: pl.BlockSpec(a.shape, lambda i: (0,) * a.ndim)
    in_specs = (_piece_specs(oa_pieces, tm, W_A) + _piece_specs(ob_pieces, tm, W_B) + [tok(W_C)]
                + _piece_specs(x_pieces, tm, D_MODEL)
                + [_resident_layer(w_out, l), mod_spec(2), full(norm_g), mod_spec(4), mod_spec(3)])
    args = [*oa_pieces, *ob_pieces, oc, *x_pieces, w_out, mod4, norm_g, mod4, mod4]
    if moe:
        in_specs.append(full(router_w))
        args.append(router_w)
        out_specs = [tok(D_MODEL), pl.BlockSpec((tm * 8, LANES), lambda i: (i, 0)), tok(LANES),
                     pl.BlockSpec((LANES, tm), lambda i: (0, i))]
        out_shape = [jax.ShapeDtypeStruct((T_ALL, D_MODEL), F32), jax.ShapeDtypeStruct((T_ALL * 8, LANES), F32),
                     jax.ShapeDtypeStruct((T_ALL, LANES), F32), jax.ShapeDtypeStruct((LANES, T_ALL), F32)]
    else:
        out_specs = [tok(D_MODEL), tok(D_MODEL)]
        out_shape = [jax.ShapeDtypeStruct((T_ALL, D_MODEL), F32), jax.ShapeDtypeStruct((T_ALL, D_MODEL), BF16)]
    return pl.pallas_call(
        functools.partial(_out_proj_kernel, n_x=len(x_pieces), moe=moe),
        grid=(n,),
        in_specs=in_specs,
        out_specs=out_specs,
        out_shape=out_shape,
        scratch_shapes=[pltpu.VMEM((D_MODEL, D_MODEL), BF16)],
        compiler_params=_cparams(("arbitrary",)),
        name="out_proj_moe" if moe else "out_proj",
    )(*args)


FFN_TM = 512
MXU_N = 256
FFN_SPLITS = (0, 512, 1024, 1536, 2048, 2560, D_FF)
assert all(s % MXU_N == 0 for s in FFN_SPLITS)


def _swiglu(h, wg_ref, wu_ref, wd_ref):
    out = None
    for c0, c1 in zip(FFN_SPLITS[:-1], FFN_SPLITS[1:]):
        act = _silu(_dot(h, wg_ref[:, c0:c1])) * _dot(h, wu_ref[:, c0:c1])
        d = _dot(act.astype(BF16), wd_ref[c0:c1, :])
        out = d if out is None else out + d
    return out


def _ffn_kernel(h_ref, x_ref, g2_ref, wg_ref, wu_ref, wd_ref, w_ref, o_ref, wb_ref):
    @pl.when(pl.program_id(0) < RIDER_BLOCKS_20)
    def _():
        wb_ref[...] = w_ref[...].astype(BF16)

    o_ref[...] = x_ref[...] + g2_ref[...] * _swiglu(h_ref[...], wg_ref, wu_ref, wd_ref)


def _ffn_call(h2, x1, mod4, wg, wu, wd, cast_w):
    tm = FFN_TM
    row = functools.partial(_mod_row, tm=tm)
    tok = lambda w: pl.BlockSpec((tm, w), lambda i: (i, 0))
    resident = lambda a: pl.BlockSpec(a.shape, lambda i: (0, 0), pipeline_mode=pl.Buffered(1))
    rider_spec, rider_shape = _cast_rider(cast_w, RIDER_BLOCKS_20)
    return pl.pallas_call(
        _ffn_kernel,
        grid=(T_ALL // tm,),
        in_specs=[tok(D_MODEL), tok(D_MODEL),
                  pl.BlockSpec((None, None, 1, D_MODEL), lambda i: (row(i), 5, 0, 0)),
                  resident(wg), resident(wu), resident(wd), rider_spec],
        out_specs=[tok(D_MODEL), rider_spec],
        out_shape=[jax.ShapeDtypeStruct((T_ALL, D_MODEL), F32), rider_shape],
        compiler_params=_cparams(("arbitrary",)),
        name="ffn_dense",
    )(h2, x1, mod4, wg, wu, wd, cast_w)


MOE_TM = 512
MOE_NT_MAX = (2 * T_ALL) // MOE_TM + N_EXPERTS
MOE_ROWS = MOE_NT_MAX * MOE_TM
PLAN_BLK = 512
MISC_LAST_START = 8
MISC_NT = 16


def _moe_plan_kernel(gt_ref, posa_ref, posb_ref, te_ref, ti_ref, tv_ref, misc_ref):
    tm = float(MOE_TM)
    sel = gt_ref[SEL_LANE0:SEL_LANE0 + N_EXPERTS, :]
    cnt = jnp.sum(sel, axis=1, keepdims=True)
    nt = jnp.floor((cnt + (tm - 1.0)) * (1.0 / tm))
    sub = lax.broadcasted_iota(jnp.int32, (N_EXPERTS, LANES), 0).astype(F32)
    lane = lax.broadcasted_iota(jnp.int32, (N_EXPERTS, LANES), 1).astype(F32)
    nt_b = jnp.broadcast_to(nt, (N_EXPERTS, LANES))
    nt_row = jnp.sum(jnp.where(sub == lane, nt_b, 0.0), axis=0, keepdims=True)
    toff = jnp.sum(jnp.where(lane < sub, jnp.broadcast_to(nt_row, (N_EXPERTS, LANES)), 0.0),
                   axis=1, keepdims=True)
    tend = toff + nt
    n_total = jnp.sum(nt, axis=0, keepdims=True)
    jc = jnp.minimum(lane, n_total - 1.0)
    te = jnp.sum(jnp.where(jc >= tend, 1.0, 0.0), axis=0, keepdims=True)
    te_ref[...] = te.astype(jnp.int32)
    ti_ref[...] = jc[0:1, :].astype(jnp.int32)
    rows_left = cnt - (jc - toff) * tm
    tv = jnp.sum(jnp.where((jc >= toff) & (jc < tend), jnp.minimum(rows_left, tm), 0.0), axis=0, keepdims=True)
    tv_ref[...] = tv.astype(jnp.int32)
    last_start = (tend - 1.0) * tm
    ls_row = jnp.sum(jnp.where(sub + MISC_LAST_START == lane, jnp.broadcast_to(last_start, (N_EXPERTS, LANES)), 0.0),
                     axis=0, keepdims=True)
    nt_row2 = jnp.sum(jnp.where(sub + MISC_NT == lane, nt_b, 0.0), axis=0, keepdims=True)
    misc = jnp.where(lane[0:1, :] == 0.0, n_total, 0.0) + ls_row + nt_row2
    misc_ref[...] = misc.astype(jnp.int32)

    off = toff * tm
    r = lax.broadcasted_iota(jnp.int32, (PLAN_BLK, PLAN_BLK), 0)
    c = lax.broadcasted_iota(jnp.int32, (PLAN_BLK, PLAN_BLK), 1)
    upper = jnp.where(r <= c, 1.0, 0.0).astype(BF16)
    carry = jnp.zeros((N_EXPERTS, 1), F32)
    for blk in range(T_ALL // PLAN_BLK):
        cols = slice(blk * PLAN_BLK, (blk + 1) * PLAN_BLK)
        s = gt_ref[SEL_LANE0:SEL_LANE0 + N_EXPERTS, cols]
        rank = _dot(s.astype(BF16), upper) + carry
        pos = off + rank - 1.0
        posa_ref[:, cols] = jnp.min(jnp.where(s > 0.0, pos, 1e9), axis=0, keepdims=True).astype(jnp.int32)
        posb_ref[:, cols] = jnp.max(jnp.where(s > 0.0, pos, -1.0), axis=0, keepdims=True).astype(jnp.int32)
        carry = carry + jnp.sum(s, axis=1, keepdims=True)


def _moe_plan_call(gates_t):
    row = lambda w: jax.ShapeDtypeStruct((1, w), jnp.int32)
    full = lambda w: pl.BlockSpec((1, w), lambda: (0, 0))
    return pl.pallas_call(
        _moe_plan_kernel,
        in_specs=[pl.BlockSpec((LANES, T_ALL), lambda: (0, 0))],
        out_specs=[full(T_ALL), full(T_ALL), full(LANES), full(LANES), full(LANES), full(LANES)],
        out_shape=[row(T_ALL), row(T_ALL), row(LANES), row(LANES), row(LANES), row(LANES)],
        compiler_params=pltpu.CompilerParams(vmem_limit_bytes=VMEM_LIMIT),
        name="moe_plan",
    )(gates_t)


DMA_UNROLL = 8


def _row_tile(ref, row):
    return ref.at[pl.ds(pl.multiple_of(row * 8, 8), 8), :]


def _moe_scatter_kernel(misc_ref, posa_ref, posb_ref, h_ref, xs_ref, zero_ref, sem):
    tm = SCATTER_TM
    base = pl.program_id(0) * tm

    @pl.when(pl.program_id(0) == 0)
    def _():
        zero_ref[...] = jnp.zeros_like(zero_ref)

        def zero_tile(first_row):
            start = pl.multiple_of(first_row * 8, 8)
            cp = pltpu.make_async_copy(zero_ref, xs_ref.at[pl.ds(start, MOE_TM * 8), :], sem.at[0])
            cp.start()
            cp.wait()

        for e in range(N_EXPERTS):
            @pl.when(misc_ref[0, MISC_NT + e] > 0)
            def _():
                zero_tile(misc_ref[0, MISC_LAST_START + e])

        def zero_tail(j, carry):
            zero_tile(j * MOE_TM)
            return carry

        lax.fori_loop(misc_ref[0, 0], MOE_NT_MAX, zero_tail, 0)

    def issue(r, carry):
        src = _row_tile(h_ref, base + r)
        pltpu.make_async_copy(src, _row_tile(xs_ref, posa_ref[0, r]), sem.at[0]).start(priority=0)
        pltpu.make_async_copy(src, _row_tile(xs_ref, posb_ref[0, r]), sem.at[1]).start(priority=1)
        return carry

    lax.fori_loop(0, tm, issue, 0, unroll=DMA_UNROLL)
    for k in range(2):
        pltpu.make_async_copy(h_ref.at[pl.ds(0, tm * 8), :], xs_ref.at[pl.ds(0, tm * 8), :], sem.at[k]).wait()


SCATTER_TM = 2048


def _moe_scatter_call(misc, posa3, posb3, h2t):
    tm = SCATTER_TM
    smem_row = pl.BlockSpec((None, 1, tm), lambda i: (i, 0, 0), memory_space=pltpu.SMEM)
    return pl.pallas_call(
        _moe_scatter_kernel,
        grid=(T_ALL // tm,),
        in_specs=[pl.BlockSpec((1, LANES), lambda i: (0, 0), memory_space=pltpu.SMEM), smem_row, smem_row,
                  pl.BlockSpec(memory_space=pl.ANY)],
        out_specs=pl.BlockSpec(memory_space=pl.ANY),
        out_shape=jax.ShapeDtypeStruct((MOE_ROWS * 8, LANES), F32),
        scratch_shapes=[pltpu.VMEM((MOE_TM * 8, LANES), F32), pltpu.SemaphoreType.DMA((2,))],
        compiler_params=_cparams(("arbitrary",)),
        name="moe_scatter",
    )(misc, posa3, posb3, h2t)


MOE_HALF = MOE_TM // 2


def _ffn_grouped_kernel(te_ref, ti_ref, tv_ref, misc_ref, x_ref, wg_ref, wu_ref, wd_ref, o_ref):
    j = pl.program_id(0)
    live = j < misc_ref[0]
    rows = tv_ref[j]

    @pl.when(live & (rows > MOE_HALF))
    def _():
        h = _load_token_tiles(x_ref, MOE_TM).astype(BF16)
        _store_token_tiles(o_ref, _swiglu(h, wg_ref, wu_ref, wd_ref))

    @pl.when(live & (rows <= MOE_HALF))
    def _():
        h = _load_token_tiles(x_ref, MOE_HALF).astype(BF16)
        _store_token_tiles(o_ref, _swiglu(h, wg_ref, wu_ref, wd_ref))
        o_ref[MOE_HALF * 8:, :] = jnp.zeros((MOE_HALF * 8, LANES), F32)

    @pl.when(jnp.logical_not(live))
    def _():
        o_ref[...] = jnp.zeros_like(o_ref)


def _ffn_grouped_call(te, ti, tv, misc, xs, wg, wu, wd):
    expert = lambda *s: pl.BlockSpec((None,) + s, lambda j, te, ti, tv, misc: (te[j], 0, 0))
    grid_spec = pltpu.PrefetchScalarGridSpec(
        num_scalar_prefetch=4,
        grid=(MOE_NT_MAX,),
        in_specs=[pl.BlockSpec((MOE_TM * 8, LANES), lambda j, te, ti, tv, misc: (ti[j], 0)),
                  expert(D_MODEL, D_FF), expert(D_MODEL, D_FF), expert(D_FF, D_MODEL)],
        out_specs=pl.BlockSpec((MOE_TM * 8, LANES), lambda j, te, ti, tv, misc: (j, 0)),
    )
    return pl.pallas_call(
        _ffn_grouped_kernel,
        grid_spec=grid_spec,
        out_shape=jax.ShapeDtypeStruct((MOE_ROWS * 8, LANES), F32),
        compiler_params=_cparams(("arbitrary",)),
        name="ffn_grouped",
    )(te, ti, tv, misc, xs, wg, wu, wd)


COMBINE_TM = 256


def _moe_combine_kernel(posa_ref, posb_ref, posa_next_ref, posb_next_ref, ys_ref, x_ref, g2_ref, gates_ref, nf_ref,
                        op_ref, os_ref, bufa_ref, bufb_ref, sem):
    tm = COMBINE_TM
    i = pl.program_id(0)
    n = pl.num_programs(0)
    slot = i % 2

    def gather(pa_ref, pb_ref, s):
        def issue(r, carry):
            pltpu.make_async_copy(_row_tile(ys_ref, pa_ref[0, r]), _row_tile(bufa_ref.at[s], r),
                                  sem.at[s, 0]).start(priority=0)
            pltpu.make_async_copy(_row_tile(ys_ref, pb_ref[0, r]), _row_tile(bufb_ref.at[s], r),
                                  sem.at[s, 1]).start(priority=1)
            return carry

        lax.fori_loop(0, tm, issue, 0, unroll=DMA_UNROLL)

    @pl.when(i == 0)
    def _():
        gather(posa_ref, posb_ref, 0)

    @pl.when(i + 1 < n)
    def _():
        gather(posa_next_ref, posb_next_ref, 1 - slot)

    gates = gates_ref[...]
    lane = lax.broadcasted_iota(jnp.int32, gates.shape, 1).astype(F32)
    is_sel = (lane >= SEL_LANE0) & (lane < SEL_LANE0 + N_EXPERTS) & (gates > 0.0)
    ia = jnp.min(jnp.where(is_sel, lane, float(LANES)), axis=-1, keepdims=True) - SEL_LANE0
    ib = jnp.max(jnp.where(is_sel, lane, -1.0), axis=-1, keepdims=True) - SEL_LANE0
    wa = jnp.sum(jnp.where(lane == ia, gates, 0.0), axis=-1, keepdims=True)
    wb = jnp.sum(jnp.where(lane == ib, gates, 0.0), axis=-1, keepdims=True)

    pltpu.make_async_copy(ys_ref.at[pl.ds(0, tm * 8), :], bufa_ref.at[slot], sem.at[slot, 0]).wait()
    pltpu.make_async_copy(ys_ref.at[pl.ds(0, tm * 8), :], bufb_ref.at[slot], sem.at[slot, 1]).wait()

    y = wa * _load_token_tiles(bufa_ref.at[slot], tm) + wb * _load_token_tiles(bufb_ref.at[slot], tm)
    out = x_ref[...] + g2_ref[...] * y
    out = (out * lax.rsqrt(jnp.mean(out * out, axis=-1, keepdims=True) + EPS)) * nf_ref[...]

    @pl.when(i < T_PROMPT // tm)
    def _():
        op_ref[...] = out

    @pl.when(i >= T_PROMPT // tm)
    def _():
        os_ref[...] = out


def _moe_combine_call(posa3, posb3, ys, x1, mod4, gates, norm_f):
    tm = COMBINE_TM
    n = T_ALL // tm
    n_p = T_PROMPT // tm
    row = functools.partial(_mod_row, tm=tm)
    smem_row = pl.BlockSpec((None, 1, tm), lambda i: (i, 0, 0), memory_space=pltpu.SMEM)
    smem_next = pl.BlockSpec((None, 1, tm), lambda i: (jnp.minimum(i + 1, n - 1), 0, 0), memory_space=pltpu.SMEM)
    tok = lambda w: pl.BlockSpec((tm, w), lambda i: (i, 0))
    return pl.pallas_call(
        _moe_combine_kernel,
        grid=(n,),
        in_specs=[smem_row, smem_row, smem_next, smem_next, pl.BlockSpec(memory_space=pl.ANY), tok(D_MODEL),
                  pl.BlockSpec((None, None, 1, D_MODEL), lambda i: (row(i), 5, 0, 0)), tok(LANES),
                  pl.BlockSpec((1, D_MODEL), lambda i: (0, 0))],
        out_specs=[pl.BlockSpec((tm, D_MODEL), lambda i: (jnp.minimum(i, n_p - 1), 0)),
                   pl.BlockSpec((tm, D_MODEL), lambda i: (jnp.maximum(i - n_p, 0), 0))],
        out_shape=[jax.ShapeDtypeStruct((T_PROMPT, D_MODEL), F32), jax.ShapeDtypeStruct((T_SAMPLE, D_MODEL), F32)],
        scratch_shapes=[pltpu.VMEM((2, tm * 8, LANES), F32), pltpu.VMEM((2, tm * 8, LANES), F32),
                        pltpu.SemaphoreType.DMA((2, 2))],
        compiler_params=_cparams(("arbitrary",)),
        name="moe_combine",
    )(posa3, posb3, posa3, posb3, ys, x1, mod4, gates, norm_f)


def _moe_call(h2t, x1, mod4, gates, gates_t, wg, wu, wd, norm_f):
    posa, posb, te, ti, tv, misc = _moe_plan_call(gates_t)
    xs = _moe_scatter_call(misc, posa.reshape(T_ALL // SCATTER_TM, 1, SCATTER_TM),
                           posb.reshape(T_ALL // SCATTER_TM, 1, SCATTER_TM), h2t)
    ys = _ffn_grouped_call(te.reshape(LANES), ti.reshape(LANES), tv.reshape(LANES), misc.reshape(LANES), xs, wg, wu, wd)
    return _moe_combine_call(posa.reshape(T_ALL // COMBINE_TM, 1, COMBINE_TM),
                             posb.reshape(T_ALL // COMBINE_TM, 1, COMBINE_TM), ys, x1, mod4, gates, norm_f)


def kernel(x_prompt, x_sample, cache_k, cache_v, state_gla_fwd, state_gla_bwd, c, c_ctx, w_ada, b_ada, norm_mix, norm_ffn, w_in, w_out, gla_w_up, gla_b_up, gla_norm, diff_lambda, diff_norm, sgu_w, sgu_b, ffn_w_gate, ffn_w_up, ffn_w_down, router_w, moe_w_gate, moe_w_up, moe_w_down, norm_f):
    assert DEPTH == 2
    x_pieces = [x_prompt.reshape(T_PROMPT, D_MODEL), x_sample.reshape(T_SAMPLE, D_MODEL)]
    cvecs = jnp.concatenate([c_ctx[None, :], c, jnp.zeros((N_MOD_ROWS - 1 - DEC_BATCH, D_MODEL), F32)], axis=0)
    mod = _ada_call(cvecs, w_ada, b_ada)
    cos, sin_signed = _rope_tables()
    zeros_state = jnp.zeros((BATCH, 64, 256), F32)

    w_in_t = jnp.swapaxes(w_in, 1, 2)
    moe_w = [moe_w_gate[0], moe_w_up[0], moe_w_down[0]]
    for l in range(DEPTH):
        mod4 = mod[l].reshape(N_MOD_ROWS, 6, 1, D_MODEL)
        w_up = jnp.zeros((LANES, 2 * W_A), F32)
        w_up = w_up.at[0:GLA_RANK, 0:W_A].set(gla_w_up[l, 0]).at[GLA_RANK:2 * GLA_RANK, W_A:].set(gla_w_up[l, 1])
        b_up = gla_b_up[l].reshape(1, 2 * W_A)
        bs_full = jnp.repeat(sgu_b[l].T, DG_C, axis=1)
        g4, la, qb, kb, vb, oc, moe_w[l] = _in_proj_call(x_pieces, norm_mix[l][None, :], mod4, w_in_t, l,
                                                         w_up.astype(BF16), b_up, sgu_w[l].astype(BF16), bs_full,
                                                         moe_w[l])

        gain_a = gla_norm[l][None, :]
        if l == 0:
            oa_p, sf0, sb0 = _gla_call(g4, la, zeros_state, zeros_state, gain_a, batch=BATCH, seq=SEQ, row_block0=0)
            kb0, vb0 = kb, vb
        else:
            oa_p, new_sf, new_sb, new_cache_k = _gla_call(g4, la, zeros_state, zeros_state, gain_a, (sf0, sb0, kb0, kb),
                                                          batch=BATCH, seq=SEQ, row_block0=0)
        oa_s, _, _ = _gla_call(g4, la, _state_to_kernel(state_gla_fwd[:, l]),
                               _state_to_kernel(state_gla_bwd[:, l]), gain_a,
                               batch=DEC_BATCH, seq=DEC_SEQ, row_block0=T_PROMPT // DEC_SEQ)

        lam_init = 0.8 - 0.6 * math.exp(-0.3 * l)
        gain_b = diff_norm[l][None, :]
        if l == 0:
            (ob_p,) = _attn_prompt_call(diff_lambda[l], qb, kb, vb, gain_b, lam_init=lam_init)
        else:
            ob_p, new_cache_v = _attn_prompt_call(diff_lambda[l], qb, kb, vb, gain_b, (vb0,),
                                                  lam_init=lam_init, write_cache=True)
        ob_s, *dense_w = _attn_sample_call(diff_lambda[l], qb, kb, vb,
                                           cache_k[:, l].reshape(DEC_BATCH, PAST_LEN, W_B),
                                           cache_v[:, l].reshape(DEC_BATCH, PAST_LEN, W_B),
                                           cos, sin_signed, gain_b,
                                           (ffn_w_gate[0], ffn_w_up[0], ffn_w_down[0]) if l == 0 else (),
                                           lam_init=lam_init)

        if l == 0:
            x1, h2 = _out_proj_call([oa_p, oa_s], [ob_p, ob_s], oc, x_pieces, w_out, l, mod4, norm_ffn[l][None, :])
            x_next, moe_w[2] = _ffn_call(h2, x1, mod4, *dense_w, moe_w[2])
            x_pieces = [x_next]
        else:
            rw = jnp.pad(router_w[0], ((0, 0), (0, LANES - N_EXPERTS)))
            x1, h2t, gates, gates_t = _out_proj_call([oa_p, oa_s], [ob_p, ob_s], oc, x_pieces, w_out, l, mod4,
                                                     norm_ffn[l][None, :], rw)
            y_prompt, y_sample = _moe_call(h2t, x1, mod4, gates, gates_t, *moe_w, norm_f[None, :])

    state_shape = (BATCH, DEPTH, N_HEADS_A, DK_A, DK_A)
    return (y_prompt.reshape(BATCH, SEQ, D_MODEL), y_sample.reshape(DEC_BATCH, DEC_SEQ, D_MODEL),
            new_cache_k.reshape(BATCH, DEPTH, SEQ, N_HEADS_B, DV_B), new_cache_v.reshape(BATCH, DEPTH, SEQ, N_HEADS_B, DV_B),
            new_sf.reshape(state_shape), new_sb.reshape(state_shape))
```
